```python
import math
import jax, jax.numpy as jnp
from jax import lax
import numpy as np

D_MODEL = 1024
BATCH = 16
SEQ = 4096
DEPTH = 1

A_HEADS = 8
A_KV_HEADS = 2
A_HEAD_DIM = 64
A_WIDTH = A_HEADS * A_HEAD_DIM
A_KV_WIDTH = A_KV_HEADS * A_HEAD_DIM
WINDOW = 128
BLOCK = 128
ROPE_THETA = 500000.0
ROPE_DIM = A_HEAD_DIM // 4
B_HEADS = 4
B_KEY_DIM = 64
B_VAL_DIM = 128
B_KEY_WIDTH = B_HEADS * B_KEY_DIM
B_WIDTH = B_HEADS * B_VAL_DIM
B_GATE_RANK = 16
B_GATE_TEMP = 16.0
B_CHUNK = 64
NORM_EPS = 1e-6
NEG_BIG = -1e30

IN_SPLITS = (A_WIDTH, A_KV_WIDTH, A_KV_WIDTH, A_WIDTH,
             B_KEY_WIDTH, B_KEY_WIDTH, B_WIDTH, B_WIDTH, B_GATE_RANK,
             D_MODEL, D_MODEL)
D_IN = 2 * A_WIDTH + 2 * A_KV_WIDTH + 2 * B_KEY_WIDTH + 2 * B_WIDTH + B_GATE_RANK + 2 * D_MODEL

kernel_name = "hybrid_swa_sink_gla_gated_merge"


def rms_norm(x, g):
    xf = x.astype(jnp.float32)
    y = xf * lax.rsqrt(jnp.mean(xf * xf, axis=-1, keepdims=True) + NORM_EPS)
    return (y * g.astype(jnp.float32)).astype(x.dtype)


def partial_rope(t, positions):
    half = ROPE_DIM // 2
    inv_freq = jnp.exp(-math.log(ROPE_THETA) * jnp.arange(half, dtype=jnp.float32) * (2.0 / ROPE_DIM))
    ang = positions.astype(jnp.float32)[..., None] * inv_freq
    cos = jnp.cos(ang)[:, :, None, :]
    sin = jnp.sin(ang)[:, :, None, :]
    tr = t[..., :ROPE_DIM].astype(jnp.float32)
    t1, t2 = tr[..., :half], tr[..., half:]
    rot = jnp.concatenate([t1 * cos - t2 * sin, t2 * cos + t1 * sin], axis=-1)
    return jnp.concatenate([rot.astype(t.dtype), t[..., ROPE_DIM:]], axis=-1)


def sliding_window_gqa_sinks(q, k, v, sinks):
    b, s = q.shape[0], q.shape[1]
    nb = s // BLOCK
    grp = A_HEADS // A_KV_HEADS
    qb = q.reshape(b, nb, BLOCK, A_KV_HEADS, grp, A_HEAD_DIM).astype(jnp.float32)
    kb = k.reshape(b, nb, BLOCK, A_KV_HEADS, A_HEAD_DIM).astype(jnp.float32)
    vb = v.reshape(b, nb, BLOCK, A_KV_HEADS, A_HEAD_DIM).astype(jnp.float32)

    def with_prev(t):
        prev = jnp.pad(t[:, :-1], ((0, 0), (1, 0), (0, 0), (0, 0), (0, 0)))
        return jnp.concatenate([prev, t], axis=2)

    kw, vw = with_prev(kb), with_prev(vb)
    scores = jnp.einsum('bnqhgd,bnkhd->bnhgqk', qb, kw) * (A_HEAD_DIM ** -0.5)
    qi = jnp.arange(BLOCK)[:, None]
    kj = jnp.arange(2 * BLOCK)[None, :] - BLOCK
    band = (kj <= qi) & (qi - kj < WINDOW)
    blk = jnp.arange(nb)[:, None, None]
    valid = band[None] & ((blk > 0) | (kj[None] >= 0))
    scores = jnp.where(valid[None, :, None, None], scores, NEG_BIG)
    sink = sinks.astype(jnp.float32).reshape(A_KV_HEADS, grp)[None, None, :, :, None, None]
    m = jnp.maximum(jnp.max(scores, axis=-1, keepdims=True), sink)
    p = jnp.exp(scores - m)
    denom = jnp.sum(p, axis=-1, keepdims=True) + jnp.exp(sink - m)
    probs = p / denom
    out = jnp.einsum('bnhgqk,bnkhd->bnqhgd', probs, vw)
    return out.reshape(b, s, A_WIDTH).astype(q.dtype)


def gla_chunked(q, k, v, log_a):
    b, s = q.shape[0], q.shape[1]
    nc = s // B_CHUNK

    def chunks(t):
        return t.astype(jnp.float32).reshape(b, nc, B_CHUNK, B_HEADS, t.shape[-1]).transpose(0, 3, 1, 2, 4)

    qc = chunks(q) * (B_KEY_DIM ** -0.5)
    kc, vc, gc = chunks(k), chunks(v), chunks(log_a)
    cum = jnp.cumsum(gc, axis=3)
    last = cum[:, :, :, -1:, :]
    mid = cum[:, :, :, B_CHUNK // 2 - 1:B_CHUNK // 2, :]
    attn = jnp.einsum('bhnid,bhnjd->bhnij', qc * jnp.exp(cum - mid), kc * jnp.exp(mid - cum))
    causal = jnp.tril(jnp.ones((B_CHUNK, B_CHUNK), dtype=bool))
    attn = jnp.where(causal, attn, 0.0)
    o_intra = jnp.einsum('bhnij,bhnjv->bhniv', attn, vc)
    inc = jnp.einsum('bhnjd,bhnjv->nbhdv', kc * jnp.exp(last - cum), vc)
    decay = jnp.exp(last[:, :, :, 0, :]).transpose(2, 0, 1, 3)

    def step(state, xs):
        dec, add = xs
        return dec[..., None] * state + add, state

    init = jnp.zeros((b, B_HEADS, B_KEY_DIM, B_VAL_DIM), jnp.float32)
    _, s_prev = lax.scan(step, init, (decay, inc))
    o_inter = jnp.einsum('bhnid,nbhdv->bhniv', qc * jnp.exp(cum), s_prev)
    o = (o_intra + o_inter).transpose(0, 2, 3, 1, 4)
    return o.reshape(b, s, B_HEADS, B_VAL_DIM)


def hybrid_layer(x, positions, g_in, w_in, w_alpha_up, b_alpha, attn_sinks, g_gla_norm,
                 w_out_a, w_out_b, w_o):
    b, s = x.shape[0], x.shape[1]
    h = rms_norm(x, g_in)
    proj = jnp.einsum('bsd,de->bse', h, w_in)
    split_points = np.cumsum(IN_SPLITS)[:-1].tolist()
    qa, ka, va, za, qb, kb, vb, zb, a_lr, gate_a, gate_b = jnp.split(proj, split_points, axis=-1)

    qa = partial_rope(qa.reshape(b, s, A_HEADS, A_HEAD_DIM), positions)
    ka = partial_rope(ka.reshape(b, s, A_KV_HEADS, A_HEAD_DIM), positions)
    va = va.reshape(b, s, A_KV_HEADS, A_HEAD_DIM)
    oa = sliding_window_gqa_sinks(qa, ka, va, attn_sinks) * jax.nn.silu(za)
    ya = jnp.einsum('bse,ed->bsd', oa, w_out_a)

    log_a = jax.nn.log_sigmoid(
        (jnp.einsum('bsr,re->bse', a_lr, w_alpha_up) + b_alpha).astype(jnp.float32)) / B_GATE_TEMP
    ob = gla_chunked(qb.reshape(b, s, B_HEADS, B_KEY_DIM),
                     kb.reshape(b, s, B_HEADS, B_KEY_DIM),
                     vb.reshape(b, s, B_HEADS, B_VAL_DIM),
                     log_a.reshape(b, s, B_HEADS, B_KEY_DIM))
    ob = ob * lax.rsqrt(jnp.mean(ob * ob, axis=-1, keepdims=True) + NORM_EPS)
    ob = (ob.reshape(b, s, B_WIDTH) * g_gla_norm.astype(jnp.float32)).astype(x.dtype) * jax.nn.silu(zb)
    yb = jnp.einsum('bse,ed->bsd', ob, w_out_b)

    merged = jax.nn.sigmoid(gate_a) * ya + jax.nn.sigmoid(gate_b) * yb
    return x + jnp.einsum('bsd,de->bse', merged, w_o)


def _fwd_setup_inputs(seed: int = 0) -> dict:
    key = jax.random.key(seed)
    ks = jax.random.split(key, 12)
    nrm = jax.random.normal
    f32 = jnp.float32
    x = nrm(ks[0], (BATCH, SEQ, D_MODEL), f32)
    positions = jnp.broadcast_to(jnp.arange(SEQ, dtype=jnp.int32)[None, :], (BATCH, SEQ))
    g_in = 1.0 + 0.02 * nrm(ks[1], (DEPTH, D_MODEL), f32)
    w_in = nrm(ks[2], (DEPTH, D_MODEL, D_IN), f32) * D_MODEL ** -0.5
    w_alpha_up = nrm(ks[3], (DEPTH, B_GATE_RANK, B_KEY_WIDTH), f32) * B_GATE_RANK ** -0.5
    b_alpha = 0.1 * nrm(ks[4], (DEPTH, B_KEY_WIDTH), f32)
    attn_sinks = 0.5 * nrm(ks[5], (DEPTH, A_HEADS), f32)
    g_gla_norm = 1.0 + 0.02 * nrm(ks[6], (DEPTH, B_WIDTH), f32)
    w_out_a = nrm(ks[7], (DEPTH, A_WIDTH, D_MODEL), f32) * A_WIDTH ** -0.5
    w_out_b = nrm(ks[8], (DEPTH, B_WIDTH, D_MODEL), f32) * B_WIDTH ** -0.5
    w_o = nrm(ks[9], (DEPTH, D_MODEL, D_MODEL), f32) * D_MODEL ** -0.5
    g_final = 1.0 + 0.02 * nrm(ks[10], (D_MODEL,), f32)
    return {"x": x, "positions": positions, "g_in": g_in, "w_in": w_in,
            "w_alpha_up": w_alpha_up, "b_alpha": b_alpha, "attn_sinks": attn_sinks,
            "g_gla_norm": g_gla_norm, "w_out_a": w_out_a, "w_out_b": w_out_b,
            "w_o": w_o, "g_final": g_final}


def _fwd_reference(x, positions, g_in, w_in, w_alpha_up, b_alpha, attn_sinks, g_gla_norm,
              w_out_a, w_out_b, w_o, g_final):
    h = x
    for layer in range(DEPTH):
        h = hybrid_layer(h, positions, g_in[layer], w_in[layer], w_alpha_up[layer], b_alpha[layer],
                         attn_sinks[layer], g_gla_norm[layer], w_out_a[layer], w_out_b[layer],
                         w_o[layer])
    return rms_norm(h, g_final)


import jax as _jax
import jax.numpy as _jnp

TWIN_FORMAT = 'train_step'
FWD_PARAMS = ['x', 'positions', 'g_in', 'w_in', 'w_alpha_up', 'b_alpha', 'attn_sinks', 'g_gla_norm', 'w_out_a', 'w_out_b', 'w_o', 'g_final']
TWIN_WEIGHTS = ['g_in', 'w_in', 'w_alpha_up', 'b_alpha', 'attn_sinks', 'g_gla_norm', 'w_out_a', 'w_out_b', 'w_o', 'g_final']
TWIN_DIFF_INPUT = 'x'
TWIN_INPUTS = ['x', 'positions', 'g_in', 'w_in', 'w_alpha_up', 'b_alpha', 'attn_sinks', 'g_gla_norm', 'w_out_a', 'w_out_b', 'w_o', 'g_final', 'loss_target', 'm_g_in', 'm_w_in', 'm_w_alpha_up', 'm_b_alpha', 'm_attn_sinks', 'm_g_gla_norm', 'm_w_out_a', 'm_w_out_b', 'm_w_o', 'm_g_final', 'v_g_in', 'v_w_in', 'v_w_alpha_up', 'v_b_alpha', 'v_attn_sinks', 'v_g_gla_norm', 'v_w_out_a', 'v_w_out_b', 'v_w_o', 'v_g_final']
TWIN_OUTPUTS = ['loss', 'grad_x', 'grad_g_in', 'grad_w_in', 'grad_w_alpha_up', 'grad_b_alpha', 'grad_attn_sinks', 'grad_g_gla_norm', 'grad_w_out_a', 'grad_w_out_b', 'grad_w_o', 'grad_g_final', 'delta_g_in', 'delta_w_in', 'delta_w_alpha_up', 'delta_b_alpha', 'delta_attn_sinks', 'delta_g_gla_norm', 'delta_w_out_a', 'delta_w_out_b', 'delta_w_o', 'delta_g_final', 'new_m_g_in', 'new_m_w_in', 'new_m_w_alpha_up', 'new_m_b_alpha', 'new_m_attn_sinks', 'new_m_g_gla_norm', 'new_m_w_out_a', 'new_m_w_out_b', 'new_m_w_o', 'new_m_g_final', 'new_v_g_in', 'new_v_w_in', 'new_v_w_alpha_up', 'new_v_b_alpha', 'new_v_attn_sinks', 'new_v_g_gla_norm', 'new_v_w_out_a', 'new_v_w_out_b', 'new_v_w_o', 'new_v_g_final']
TWIN_LEAF_KINDS = {'loss': 'loss', 'grad_x': 'grad_x', 'grad_g_in': 'grad_w', 'grad_w_in': 'grad_w', 'grad_w_alpha_up': 'grad_w', 'grad_b_alpha': 'grad_w', 'grad_attn_sinks': 'grad_w', 'grad_g_gla_norm': 'grad_w', 'grad_w_out_a': 'grad_w', 'grad_w_out_b': 'grad_w', 'grad_w_o': 'grad_w', 'grad_g_final': 'grad_w', 'delta_g_in': 'delta_w', 'delta_w_in': 'delta_w', 'delta_w_alpha_up': 'delta_w', 'delta_b_alpha': 'delta_w', 'delta_attn_sinks': 'delta_w', 'delta_g_gla_norm': 'delta_w', 'delta_w_out_a': 'delta_w', 'delta_w_out_b': 'delta_w', 'delta_w_o': 'delta_w', 'delta_g_final': 'delta_w', 'new_m_g_in': 'new_m', 'new_m_w_in': 'new_m', 'new_m_w_alpha_up': 'new_m', 'new_m_b_alpha': 'new_m', 'new_m_attn_sinks': 'new_m', 'new_m_g_gla_norm': 'new_m', 'new_m_w_out_a': 'new_m', 'new_m_w_out_b': 'new_m', 'new_m_w_o': 'new_m', 'new_m_g_final': 'new_m', 'new_v_g_in': 'new_v', 'new_v_w_in': 'new_v', 'new_v_w_alpha_up': 'new_v', 'new_v_b_alpha': 'new_v', 'new_v_attn_sinks': 'new_v', 'new_v_g_gla_norm': 'new_v', 'new_v_w_out_a': 'new_v', 'new_v_w_out_b': 'new_v', 'new_v_w_o': 'new_v', 'new_v_g_final': 'new_v'}


def _forward(args):
    return _fwd_reference(*[args[k] for k in FWD_PARAMS])


def _output_shape():
    out = _jax.eval_shape(lambda: _forward(_fwd_setup_inputs(0)))
    return out.shape, out.dtype

N_MICROBATCH = 1
ADAM_LR = 0.001
ADAM_B1 = 0.9
ADAM_B2 = 0.999
ADAM_EPS = 1e-08
ADAM_WD = 0.01
ADAM_STEP = 10
PER_EXAMPLE_BATCH_AXIS = {'x': 0, 'positions': 0, 'loss_target': 0}
SHARED_INPUTS = []
_WEIGHT_DTYPES = {'g_in': _jnp.float32, 'w_in': _jnp.float32, 'w_alpha_up': _jnp.float32, 'b_alpha': _jnp.float32, 'attn_sinks': _jnp.float32, 'g_gla_norm': _jnp.float32, 'w_out_a': _jnp.float32, 'w_out_b': _jnp.float32, 'w_o': _jnp.float32, 'g_final': _jnp.float32}
MOMENT_SCALE = {'g_in': 1.676031e-01, 'w_in': 7.468664e-02, 'w_alpha_up': 1.936171e-02, 'b_alpha': 6.879613e-02, 'attn_sinks': 2.706674e-02, 'g_gla_norm': 1.289692e-01, 'w_out_a': 1.564341e-02, 'w_out_b': 7.693724e-02, 'w_o': 7.832812e-02, 'g_final': 6.388165e+01}


def _to_microbatches(a, axis):
    t = _jnp.moveaxis(a, axis, 0)
    t = t.reshape((N_MICROBATCH, t.shape[0] // N_MICROBATCH) + t.shape[1:])
    return _jnp.moveaxis(t, 1, axis + 1)


def setup_inputs(seed: int = 0) -> dict:
    inp = _fwd_setup_inputs(seed)
    key = _jax.random.fold_in(_jax.random.key(seed), 7919)
    shape, _ = _output_shape()
    out = dict(inp)
    out["loss_target"] = _jax.random.normal(_jax.random.fold_in(key, 0), shape, _jnp.float32)
    for i, name in enumerate(TWIN_WEIGHTS):
        w = inp[name].astype(_jnp.float32)
        if MOMENT_SCALE is None:
            s = _jnp.sqrt(_jnp.mean(_jnp.square(w)) + 1e-30)
        else:
            s = MOMENT_SCALE[name]
        km, kv = _jax.random.split(_jax.random.fold_in(key, i + 1))
        out[name] = w
        out["m_" + name] = s * _jax.random.normal(km, w.shape, _jnp.float32)
        out["v_" + name] = (s * s) * _jax.random.uniform(kv, w.shape, _jnp.float32, 0.5, 1.5)
    if N_MICROBATCH > 1:
        for name, axis in PER_EXAMPLE_BATCH_AXIS.items():
            out[name] = _to_microbatches(out[name], axis)
    return {'x': out['x'], 'positions': out['positions'], 'g_in': out['g_in'], 'w_in': out['w_in'], 'w_alpha_up': out['w_alpha_up'], 'b_alpha': out['b_alpha'], 'attn_sinks': out['attn_sinks'], 'g_gla_norm': out['g_gla_norm'], 'w_out_a': out['w_out_a'], 'w_out_b': out['w_out_b'], 'w_o': out['w_o'], 'g_final': out['g_final'], 'loss_target': out['loss_target'], 'm_g_in': out['m_g_in'], 'm_w_in': out['m_w_in'], 'm_w_alpha_up': out['m_w_alpha_up'], 'm_b_alpha': out['m_b_alpha'], 'm_attn_sinks': out['m_attn_sinks'], 'm_g_gla_norm': out['m_g_gla_norm'], 'm_w_out_a': out['m_w_out_a'], 'm_w_out_b': out['m_w_out_b'], 'm_w_o': out['m_w_o'], 'm_g_final': out['m_g_final'], 'v_g_in': out['v_g_in'], 'v_w_in': out['v_w_in'], 'v_w_alpha_up': out['v_w_alpha_up'], 'v_b_alpha': out['v_b_alpha'], 'v_attn_sinks': out['v_attn_sinks'], 'v_g_gla_norm': out['v_g_gla_norm'], 'v_w_out_a': out['v_w_out_a'], 'v_w_out_b': out['v_w_out_b'], 'v_w_o': out['v_w_o'], 'v_g_final': out['v_g_final']}


def _loss(weights, diff, rest, loss_target):
    with _jax.named_scope("forward"):
        args = {**rest, TWIN_DIFF_INPUT: diff, **{k: w.astype(_WEIGHT_DTYPES[k]) for k, w in weights.items()}}
        y = _forward(args)
    with _jax.named_scope("loss_head"):
        err = _jnp.square(y.astype(_jnp.float32) - loss_target)
        return 0.5 * _jnp.sum(_jnp.mean(err, axis=-1)) if err.ndim else 0.5 * err


def _adamw(w, g, m, v):
    m = ADAM_B1 * m + (1.0 - ADAM_B1) * g
    v = ADAM_B2 * v + (1.0 - ADAM_B2) * _jnp.square(g)
    m_hat = m / (1.0 - ADAM_B1 ** ADAM_STEP)
    v_hat = v / (1.0 - ADAM_B2 ** ADAM_STEP)
    delta = -ADAM_LR * (m_hat / (_jnp.sqrt(v_hat) + ADAM_EPS) + ADAM_WD * w)
    return delta, m, v


def reference(x, positions, g_in, w_in, w_alpha_up, b_alpha, attn_sinks, g_gla_norm, w_out_a, w_out_b, w_o, g_final, loss_target, m_g_in, m_w_in, m_w_alpha_up, m_b_alpha, m_attn_sinks, m_g_gla_norm, m_w_out_a, m_w_out_b, m_w_o, m_g_final, v_g_in, v_w_in, v_w_alpha_up, v_b_alpha, v_attn_sinks, v_g_gla_norm, v_w_out_a, v_w_out_b, v_w_o, v_g_final):
    given = dict(x=x, positions=positions, g_in=g_in, w_in=w_in, w_alpha_up=w_alpha_up, b_alpha=b_alpha, attn_sinks=attn_sinks, g_gla_norm=g_gla_norm, w_out_a=w_out_a, w_out_b=w_out_b, w_o=w_o, g_final=g_final, loss_target=loss_target, m_g_in=m_g_in, m_w_in=m_w_in, m_w_alpha_up=m_w_alpha_up, m_b_alpha=m_b_alpha, m_attn_sinks=m_attn_sinks, m_g_gla_norm=m_g_gla_norm, m_w_out_a=m_w_out_a, m_w_out_b=m_w_out_b, m_w_o=m_w_o, m_g_final=m_g_final, v_g_in=v_g_in, v_w_in=v_w_in, v_w_alpha_up=v_w_alpha_up, v_b_alpha=v_b_alpha, v_attn_sinks=v_attn_sinks, v_g_gla_norm=v_g_gla_norm, v_w_out_a=v_w_out_a, v_w_out_b=v_w_out_b, v_w_o=v_w_o, v_g_final=v_g_final)
    weights = {n: given[n] for n in TWIN_WEIGHTS}
    shared = {n: given[n] for n in SHARED_INPUTS}
    per_example = {n: given[n] for n in ['x', 'positions']}
    grad_fn = _jax.value_and_grad(_loss, argnums=(0, 1))

    def one_microbatch(ex, loss_target):
        ex = dict(ex)
        diff = ex.pop(TWIN_DIFF_INPUT)
        return grad_fn(weights, diff, {**shared, **ex}, loss_target)

    if N_MICROBATCH == 1:
        loss, (grad_w, grad_x) = one_microbatch(per_example, given["loss_target"])
    else:
        def body(carry, xs):
            loss_sum, grad_sum = carry
            l_k, (gw_k, gx_k) = one_microbatch(xs[0], xs[1])
            with _jax.named_scope("update"):
                return (loss_sum + l_k, _jax.tree.map(_jnp.add, grad_sum, gw_k)), gx_k

        init = (_jnp.zeros((), _jnp.float32), _jax.tree.map(_jnp.zeros_like, weights))
        (loss, grad_w), grad_x = _jax.lax.scan(body, init, (per_example, given["loss_target"]))
    with _jax.named_scope("update"):
        delta_w, new_m, new_v = {}, {}, {}
        for n in TWIN_WEIGHTS:
            delta_w[n], new_m[n], new_v[n] = _adamw(weights[n], grad_w[n], given["m_" + n], given["v_" + n])
    return (loss, grad_x, *[grad_w[n] for n in TWIN_WEIGHTS], *[delta_w[n] for n in TWIN_WEIGHTS],
            *[new_m[n] for n in TWIN_WEIGHTS], *[new_v[n] for n in TWIN_WEIGHTS])
```

```python
import math

import numpy as np
import jax
import jax.numpy as jnp
from jax import lax
from jax.experimental import pallas as pl
from jax.experimental.pallas import tpu as pltpu

D_MODEL = 1024
A_HEADS, A_KV_HEADS, A_HEAD_DIM = 8, 2, 64
A_GROUP = A_HEADS // A_KV_HEADS
A_WIDTH, A_KV_WIDTH = 512, 128
BLOCK = 128
ROPE_THETA = 500000.0
ROPE_DIM = 16
B_HEADS, B_KEY_DIM, B_VAL_DIM = 4, 64, 128
B_KEY_WIDTH, B_WIDTH = 256, 512
B_GATE_RANK = 16
B_GATE_TEMP = 16.0
B_CHUNK = 64
NORM_EPS = 1e-6
NEG_BIG = -1e30
D_IN = 4880

ADAM_LR, ADAM_B1, ADAM_B2, ADAM_EPS, ADAM_WD, ADAM_STEP = 0.001, 0.9, 0.999, 1e-08, 0.01, 10

LANE = 128
ALR_AT = 2816
D_PAD = D_IN + (LANE - B_GATE_RANK)
PIECES = (("qkv", 0, 768), ("za", 768, 1280), ("qkb", 1280, 1792), ("vb", 1792, 2304),
          ("zb", 2304, 2816), ("alr", 2816, 2944), ("ga", 2944, 3968), ("gb", 3968, 4992))

GLA_BLOCK = 256
VMEM_LIMIT = 56 * 1024 * 1024

_F32 = jnp.float32
_MX = jnp.bfloat16
_ST = jnp.bfloat16

_MESH = pl.DeviceIdType.MESH
_ANY = pl.BlockSpec(memory_space=pl.ANY)


def _cparams(sem=None, vmem=None):
    return pltpu.CompilerParams(dimension_semantics=sem, vmem_limit_bytes=vmem)


def _dot(a, b):
    return jnp.dot(a.astype(_MX), b.astype(_MX), preferred_element_type=_F32)


def _dot_nt(a, b):
    return lax.dot_general(a.astype(_MX), b.astype(_MX), (((1,), (1,)), ((), ())),
                           preferred_element_type=_F32)


def _dot_tn(a, b):
    return lax.dot_general(a.astype(_MX), b.astype(_MX), (((0,), (0,)), ((), ())),
                           preferred_element_type=_F32)


def _dot_ones(ones_mat, v):
    o = ones_mat.astype(jnp.bfloat16)
    v0 = v.astype(jnp.bfloat16)
    r1 = v - v0.astype(_F32)
    v1 = r1.astype(jnp.bfloat16)
    v2 = (r1 - v1.astype(_F32)).astype(jnp.bfloat16)
    d = lambda t: jnp.dot(o, t, preferred_element_type=_F32)
    return d(v0) + d(v1) + d(v2)


def _sigmoid(x):
    return 1.0 / (1.0 + jnp.exp(-x))


def _log_sigmoid(x):
    return jnp.minimum(x, 0.0) - jnp.log(1.0 + jnp.exp(-jnp.abs(x)))


def _lane_tile(t, width):
    reps = width // t.shape[1]
    return t if reps == 1 else jnp.tile(t, (1, reps))


def _rope(t, cos, sa, sb, sign):
    w = t.shape[1]
    rot = pltpu.roll(t, w - 8, 1) * _lane_tile(sa, w) + pltpu.roll(t, 8, 1) * _lane_tile(sb, w)
    return t * _lane_tile(cos, w) + sign * rot


def _rms_bwd(dy_g, n, r):
    return r * (dy_g - n * jnp.mean(dy_g * n, axis=-1, keepdims=True))


def _rope_tables(pos_f):
    t = pos_f.shape[0]
    tm = min(t, 1024)
    lane = np.arange(LANE) % A_HEAD_DIM
    half = ROPE_DIM // 2
    inv = np.exp((np.float32(-math.log(ROPE_THETA)) * np.arange(half, dtype=np.float32)) * np.float32(2.0 / ROPE_DIM))
    consts = np.zeros((8, LANE), np.float32)
    consts[0] = np.where(lane < ROPE_DIM, inv[lane % half], 0.0)
    consts[1] = np.where(lane < half, -1.0, 0.0)
    consts[2] = np.where((lane >= half) & (lane < ROPE_DIM), 1.0, 0.0)

    def body(pos_ref, c_ref, cos_ref, sa_ref, sb_ref):
        ang = pos_ref[...] * c_ref[0:1, :]
        s = jnp.sin(ang)
        cos_ref[...] = jnp.cos(ang)
        sa_ref[...] = s * c_ref[1:2, :]
        sb_ref[...] = s * c_ref[2:3, :]

    tab = jax.ShapeDtypeStruct((t, LANE), _F32)
    row = pl.BlockSpec((tm, LANE), lambda i: (i, 0))
    return pl.pallas_call(
        body, name="rope_tables", grid=(t // tm,),
        in_specs=[pl.BlockSpec((tm, 1), lambda i: (i, 0)), pl.BlockSpec((8, LANE), lambda i: (0, 0))],
        out_specs=[row, row, row], out_shape=[tab, tab, tab],
        compiler_params=_cparams(("parallel",)),
    )(pos_f, jnp.asarray(consts))


def _in_proj(x2, g_in, wp, cos, sa, sb):
    t = x2.shape[0]
    tm = min(t, 512)

    def body(x_ref, g_ref, w_ref, cos_ref, sa_ref, sb_ref, h_ref, qkv_ref, za_ref, qkb_ref,
             vb_ref, zb_ref, alr_ref, ga_ref, gb_ref):
        xv = x_ref[...]
        r = lax.rsqrt(jnp.mean(xv * xv, axis=-1, keepdims=True) + NORM_EPS)
        h = (xv * r * g_ref[...]).astype(_MX)
        h_ref[...] = h.astype(_ST)
        outs = dict(za=za_ref, qkb=qkb_ref, vb=vb_ref, zb=zb_ref, alr=alr_ref, ga=ga_ref, gb=gb_ref)
        for name, a, b in PIECES:
            p = jnp.dot(h, w_ref[:, a:b], preferred_element_type=_F32)
            if name == "qkv":
                c, s1, s2 = cos_ref[...], sa_ref[...], sb_ref[...]
                qkv_ref[:, 0:512] = _rope(p[:, 0:512], c, s1, s2, 1.0).astype(_ST)
                qkv_ref[:, 512:640] = _rope(p[:, 512:640], c, s1, s2, 1.0).astype(_ST)
                qkv_ref[:, 640:768] = p[:, 640:768].astype(_ST)
            else:
                outs[name][...] = p.astype(outs[name].dtype)

    rows = lambda w: pl.BlockSpec((tm, w), lambda i: (i, 0))
    shp = lambda name, w: jax.ShapeDtypeStruct((t, w), _F32 if name == "qkb" else _ST)
    widths = [D_MODEL] + [b - a for _, a, b in PIECES]
    return pl.pallas_call(
        body, name="in_proj", grid=(t // tm,),
        in_specs=[rows(D_MODEL), pl.BlockSpec((1, D_MODEL), lambda i: (0, 0)),
                  pl.BlockSpec((D_MODEL, D_PAD), lambda i: (0, 0)), rows(LANE), rows(LANE), rows(LANE)],
        out_specs=[rows(w) for w in widths],
        out_shape=[shp(n, w) for n, w in zip(["h"] + [p[0] for p in PIECES], widths)],
        compiler_params=_cparams(("parallel",), VMEM_LIMIT),
    )(x2, g_in, wp, cos, sa, sb)


def _attn_group(qkv_ref, kvp_ref, sink_ref, g, n):
    gq = A_GROUP * BLOCK
    heads = [A_GROUP * g + j for j in range(A_GROUP)]
    qg = jnp.concatenate([qkv_ref[:, h * 64:(h + 1) * 64] for h in heads], axis=0)
    qg = (qg.astype(_F32) * (A_HEAD_DIM ** -0.5)).astype(_MX)
    kw = jnp.concatenate([kvp_ref[:, g * 64:(g + 1) * 64],
                          qkv_ref[:, 512 + g * 64:512 + (g + 1) * 64]], axis=0).astype(_MX)
    vw = jnp.concatenate([kvp_ref[:, 128 + g * 64:128 + (g + 1) * 64],
                          qkv_ref[:, 640 + g * 64:640 + (g + 1) * 64]], axis=0).astype(_MX)
    s = _dot_nt(qg, kw)
    qi = lax.broadcasted_iota(jnp.int32, (gq, 2 * BLOCK), 0) & (BLOCK - 1)
    kj = lax.broadcasted_iota(jnp.int32, (gq, 2 * BLOCK), 1) - BLOCK
    valid = (kj <= qi) & (qi - kj < BLOCK) & ((n > 0) | (kj >= 0))
    s = jnp.where(valid, s, NEG_BIG)
    sink = jnp.concatenate([jnp.full((BLOCK, 1), sink_ref[h], _F32) for h in heads], axis=0)
    m = jnp.maximum(jnp.max(s, axis=1, keepdims=True), sink)
    p = jnp.exp(s - m)
    den = jnp.sum(p, axis=1, keepdims=True) + jnp.exp(sink - m)
    probs = p / den
    p_sink = jnp.exp(sink - m) / den
    return qg, kw, vw, probs, p_sink


def _unstack_heads(parts):
    cols = []
    for o in parts:
        cols += [o[j * BLOCK:(j + 1) * BLOCK, :] for j in range(A_GROUP)]
    return jnp.concatenate(cols, axis=1)


def _stack_heads(t, g):
    return jnp.concatenate([t[:, (A_GROUP * g + j) * 64:(A_GROUP * g + j + 1) * 64] for j in range(A_GROUP)], axis=0)


def _attn_fwd(qkv, za, sinks, nseq):
    t = qkv.shape[0]
    nb = t // nseq // BLOCK

    def body(sink_ref, qkv_ref, kvp_ref, za_ref, oa_ref):
        n = pl.program_id(1)
        parts = []
        for g in range(A_KV_HEADS):
            _, _, vw, probs, _ = _attn_group(qkv_ref, kvp_ref, sink_ref, g, n)
            parts.append(_dot(probs, vw))
        o = _unstack_heads(parts)
        z = za_ref[...].astype(_F32)
        oa_ref[...] = (o * (z * _sigmoid(z))).astype(_ST)

    cur = lambda w: pl.BlockSpec((BLOCK, w), lambda s, n: (s * nb + n, 0))
    return pl.pallas_call(
        body, name="attn_fwd", grid=(nseq, nb),
        in_specs=[pl.BlockSpec(memory_space=pltpu.SMEM), cur(768),
                  pl.BlockSpec((BLOCK, 256), lambda s, n: (s * nb + jnp.maximum(n - 1, 0), 2)), cur(512)],
        out_specs=cur(512), out_shape=jax.ShapeDtypeStruct((t, A_WIDTH), _ST),
        compiler_params=_cparams(("parallel", "arbitrary")),
    )(sinks, qkv, qkv, za)


def _attn_bwd(qkv, za, doa, sinks, cos, sa, sb, nseq):
    t = qkv.shape[0]
    nb = t // nseq // BLOCK

    def body(sink_ref, qkv_ref, kvp_ref, za_ref, doa_ref, cos_ref, sa_ref, sb_ref,
             dqkv_ref, dza_ref, dsink_ref, ck_ref, cv_ref):
        s_id, i = pl.program_id(0), pl.program_id(1)
        n = nb - 1 - i

        @pl.when((s_id == 0) & (i == 0))
        def _():
            dsink_ref[...] = jnp.zeros_like(dsink_ref)

        @pl.when(i == 0)
        def _():
            ck_ref[...] = jnp.zeros_like(ck_ref)
            cv_ref[...] = jnp.zeros_like(cv_ref)

        z = za_ref[...].astype(_F32)
        sz = _sigmoid(z)
        d_o = doa_ref[...].astype(_F32)
        d_attn = d_o * (z * sz)
        o_parts, dq_parts, dk_cols, dv_cols = [], [], [], []
        for g in range(A_KV_HEADS):
            qg, kw, vw, probs, p_sink = _attn_group(qkv_ref, kvp_ref, sink_ref, g, n)
            o_g = _dot(probs, vw)
            do_g = _stack_heads(d_attn, g)
            dp = _dot_nt(do_g, vw)
            delta = jnp.sum(do_g * o_g, axis=1, keepdims=True)
            ds = probs * (dp - delta)
            for j in range(A_GROUP):
                h = A_GROUP * g + j
                rows = slice(j * BLOCK, (j + 1) * BLOCK)
                val = -jnp.sum(p_sink[rows] * delta[rows])
                dsink_ref[h:h + 1, :] = dsink_ref[h:h + 1, :] + val
            dq_parts.append(_dot(ds, kw) * (A_HEAD_DIM ** -0.5))
            dkw = _dot_tn(ds, qg)
            dvw = _dot_tn(probs, do_g)
            lanes = slice(g * 64, (g + 1) * 64)
            dk_cols.append(dkw[BLOCK:] + ck_ref[:, lanes])
            dv_cols.append(dvw[BLOCK:] + cv_ref[:, lanes])
            ck_ref[:, lanes] = dkw[:BLOCK]
            cv_ref[:, lanes] = dvw[:BLOCK]
            o_parts.append(o_g)
        o = _unstack_heads(o_parts)
        dza_ref[...] = (d_o * o * (sz * (1.0 + z * (1.0 - sz)))).astype(_ST)
        c, s1, s2 = cos_ref[...], sa_ref[...], sb_ref[...]
        dqkv_ref[:, 0:512] = _rope(_unstack_heads(dq_parts), c, s1, s2, -1.0).astype(_ST)
        dqkv_ref[:, 512:640] = _rope(jnp.concatenate(dk_cols, axis=1), c, s1, s2, -1.0).astype(_ST)
        dqkv_ref[:, 640:768] = jnp.concatenate(dv_cols, axis=1).astype(_ST)

    cur = lambda w: pl.BlockSpec((BLOCK, w), lambda s, i: (s * nb + nb - 1 - i, 0))
    return pl.pallas_call(
        body, name="attn_bwd", grid=(nseq, nb),
        in_specs=[pl.BlockSpec(memory_space=pltpu.SMEM), cur(768),
                  pl.BlockSpec((BLOCK, 256), lambda s, i: (s * nb + jnp.maximum(nb - 2 - i, 0), 2)),
                  cur(512), cur(512), cur(LANE), cur(LANE), cur(LANE)],
        out_specs=[cur(768), cur(512), pl.BlockSpec((8, LANE), lambda s, i: (0, 0))],
        out_shape=[jax.ShapeDtypeStruct((t, 768), _ST), jax.ShapeDtypeStruct((t, 512), _ST),
                   jax.ShapeDtypeStruct((8, LANE), _F32)],
        scratch_shapes=[pltpu.VMEM((BLOCK, A_KV_WIDTH), _F32), pltpu.VMEM((BLOCK, A_KV_WIDTH), _F32)],
        compiler_params=_cparams(("arbitrary", "arbitrary")),
    )(sinks, qkv, qkv, za, doa, cos, sa, sb)


def _gla_chunk_terms(la, qkb_ref, r0):
    g = la[r0:r0 + B_CHUNK, :]
    ri = lax.broadcasted_iota(jnp.int32, (B_CHUNK, B_CHUNK), 0)
    ci = lax.broadcasted_iota(jnp.int32, (B_CHUNK, B_CHUNK), 1)
    cum = _dot_ones((ri >= ci).astype(_F32), g)
    last = cum[B_CHUNK - 1:B_CHUNK, :]
    mid = cum[B_CHUNK // 2 - 1:B_CHUNK // 2, :]
    q = qkb_ref[r0:r0 + B_CHUNK, 0:B_KEY_WIDTH].astype(_F32) * (B_KEY_DIM ** -0.5)
    k = qkb_ref[r0:r0 + B_CHUNK, B_KEY_WIDTH:2 * B_KEY_WIDTH].astype(_F32)
    e_q, e_k, e_l, e_c = jnp.exp(cum - mid), jnp.exp(mid - cum), jnp.exp(last - cum), jnp.exp(cum)
    dec_col = jnp.exp(jnp.sum(g.T, axis=1, keepdims=True))
    return dict(qm=q * e_q, km=k * e_k, kl=k * e_l, qc=q * e_c, e_q=e_q, e_k=e_k, e_l=e_l, e_c=e_c,
                dec_col=dec_col, dec_row=jnp.exp(last), causal=ri >= ci, ri=ri)


def _gate_logits(alr_ref, wup_ref, b_ref):
    return _dot(alr_ref[...], wup_ref[...]) + b_ref[...]


def _gla_fwd(qkb, vb, zb, alr, wup, b_alpha, gn, nseq):
    t = qkb.shape[0]
    tb = min(GLA_BLOCK, t // nseq)
    nblk = t // nseq // tb
    cpb = tb // B_CHUNK

    def body(qkb_ref, vb_ref, zb_ref, alr_ref, wup_ref, b_ref, gn_ref, ob_ref, oraw_ref, sst_ref, s_ref):
        @pl.when(pl.program_id(1) == 0)
        def _():
            s_ref[...] = jnp.zeros_like(s_ref)

        la = _log_sigmoid(_gate_logits(alr_ref, wup_ref, b_ref)) * (1.0 / B_GATE_TEMP)
        o_rows = []
        for c in range(cpb):
            r0 = c * B_CHUNK
            tm = _gla_chunk_terms(la, qkb_ref, r0)
            o_heads = []
            for h in range(B_HEADS):
                kl_, vl_ = slice(h * 64, (h + 1) * 64), slice(h * 128, (h + 1) * 128)
                st = s_ref[kl_, :]
                sst_ref[c, kl_, :] = st
                v = vb_ref[r0:r0 + B_CHUNK, vl_]
                a = jnp.where(tm["causal"], _dot_nt(tm["qm"][:, kl_], tm["km"][:, kl_]), 0.0)
                o_heads.append(_dot(a, v) + _dot(tm["qc"][:, kl_], st))
                s_ref[kl_, :] = tm["dec_col"][kl_, :] * st + _dot_tn(tm["kl"][:, kl_], v)
            o_rows.append(jnp.concatenate(o_heads, axis=1))
        o = jnp.concatenate(o_rows, axis=0)
        oraw_ref[...] = o
        z = zb_ref[...].astype(_F32)
        gate = z * _sigmoid(z)
        for h in range(B_HEADS):
            vl_ = slice(h * 128, (h + 1) * 128)
            oh = o[:, vl_]
            r = lax.rsqrt(jnp.mean(oh * oh, axis=-1, keepdims=True) + NORM_EPS)
            ob_ref[:, vl_] = ((oh * r) * gn_ref[:, vl_] * gate[:, vl_]).astype(_ST)

    rows = lambda w: pl.BlockSpec((tb, w), lambda s, i: (s * nblk + i, 0))
    full = lambda a, b: pl.BlockSpec((a, b), lambda s, i: (0, 0))
    return pl.pallas_call(
        body, name="gla_fwd", grid=(nseq, nblk),
        in_specs=[rows(512), rows(512), rows(512), rows(LANE), full(LANE, B_KEY_WIDTH),
                  full(1, B_KEY_WIDTH), full(1, B_WIDTH)],
        out_specs=[rows(512), rows(512),
                   pl.BlockSpec((cpb, B_KEY_WIDTH, B_VAL_DIM), lambda s, i: (s * nblk + i, 0, 0))],
        out_shape=[jax.ShapeDtypeStruct((t, B_WIDTH), _ST), jax.ShapeDtypeStruct((t, B_WIDTH), _F32),
                   jax.ShapeDtypeStruct((t // B_CHUNK, B_KEY_WIDTH, B_VAL_DIM), _F32)],
        scratch_shapes=[pltpu.VMEM((B_KEY_WIDTH, B_VAL_DIM), _F32)],
        compiler_params=_cparams(("parallel", "arbitrary")),
    )(qkb, vb, zb, alr, wup, b_alpha, gn)


def _gla_bwd(qkb, vb, zb, alr, oraw, dob, sst, wup, b_alpha, gn, nseq):
    t = qkb.shape[0]
    tb = min(GLA_BLOCK, t // nseq)
    nblk = t // nseq // tb
    cpb = tb // B_CHUNK

    def body(qkb_ref, vb_ref, zb_ref, alr_ref, oraw_ref, dob_ref, sst_ref, wup_ref, b_ref, gn_ref,
             dqkb_ref, dvb_ref, dzb_ref, dalr_ref, dwup_ref, db_ref, dgn_ref, ds_ref, dla_ref):
        s_id, i = pl.program_id(0), pl.program_id(1)

        @pl.when((s_id == 0) & (i == 0))
        def _():
            dwup_ref[...] = jnp.zeros_like(dwup_ref)
            db_ref[...] = jnp.zeros_like(db_ref)
            dgn_ref[...] = jnp.zeros_like(dgn_ref)

        @pl.when(i == 0)
        def _():
            ds_ref[...] = jnp.zeros_like(ds_ref)

        a_pre = _gate_logits(alr_ref, wup_ref, b_ref)
        la = _log_sigmoid(a_pre) * (1.0 / B_GATE_TEMP)

        z = zb_ref[...].astype(_F32)
        sz = _sigmoid(z)
        d_ob = dob_ref[...].astype(_F32)
        tg = d_ob * (z * sz)
        dsilu = sz * (1.0 + z * (1.0 - sz))
        do_cols, dgn_cols = [], []
        for h in range(B_HEADS):
            vl_ = slice(h * 128, (h + 1) * 128)
            oh = oraw_ref[:, vl_].astype(_F32)
            r = lax.rsqrt(jnp.mean(oh * oh, axis=-1, keepdims=True) + NORM_EPS)
            on = oh * r
            gnh = gn_ref[:, vl_]
            dzb_ref[:, vl_] = (d_ob[:, vl_] * (on * gnh) * dsilu[:, vl_]).astype(_ST)
            dgn_cols.append(jnp.sum(tg[:, vl_] * on, axis=0, keepdims=True))
            do_cols.append(_rms_bwd(tg[:, vl_] * gnh, on, r))
        dgn_ref[...] = dgn_ref[...] + jnp.concatenate(dgn_cols, axis=1)
        d_o = jnp.concatenate(do_cols, axis=1)

        for c in reversed(range(cpb)):
            r0 = c * B_CHUNK
            tm = _gla_chunk_terms(la, qkb_ref, r0)
            dq_cols, dk_cols, dv_cols, dg_cols = [], [], [], []
            for h in range(B_HEADS):
                kl_, vl_ = slice(h * 64, (h + 1) * 64), slice(h * 128, (h + 1) * 128)
                qm, km, kl, qc = tm["qm"][:, kl_], tm["km"][:, kl_], tm["kl"][:, kl_], tm["qc"][:, kl_]
                st = sst_ref[c, kl_, :]
                dsn = ds_ref[kl_, :]
                doh = d_o[r0:r0 + B_CHUNK, vl_]
                v = vb_ref[r0:r0 + B_CHUNK, vl_]
                a = jnp.where(tm["causal"], _dot_nt(qm, km), 0.0)
                da = jnp.where(tm["causal"], _dot_nt(doh, v), 0.0)
                dqc = _dot_nt(doh, st)
                dqm = _dot(da, km)
                dkm = _dot_tn(da, qm)
                dkl = _dot_nt(v, dsn)
                dv_cols.append(_dot_tn(a, doh) + _dot(kl, dsn))
                ds_ref[kl_, :] = _dot_tn(qc, doh) + tm["dec_col"][kl_, :] * dsn
                dq_cols.append((dqm * tm["e_q"][:, kl_] + dqc * tm["e_c"][:, kl_]) * (B_KEY_DIM ** -0.5))
                dk_cols.append(dkm * tm["e_k"][:, kl_] + dkl * tm["e_l"][:, kl_])
                dcum = dqm * qm - dkm * km + dqc * qc - dkl * kl
                ddec = jnp.sum((dsn * st).T, axis=0, keepdims=True)
                dlast = jnp.sum(dkl * kl, axis=0, keepdims=True) + ddec * tm["dec_row"][:, kl_]
                dcum = jnp.where(tm["ri"] == B_CHUNK - 1, dcum + dlast, dcum)
                dg_cols.append(_dot_ones((tm["ri"] <= lax.broadcasted_iota(jnp.int32, (B_CHUNK, B_CHUNK), 1)
                                          ).astype(_F32), dcum))
            rows = slice(r0, r0 + B_CHUNK)
            dqkb_ref[rows, 0:B_KEY_WIDTH] = jnp.concatenate(dq_cols, axis=1).astype(_ST)
            dqkb_ref[rows, B_KEY_WIDTH:2 * B_KEY_WIDTH] = jnp.concatenate(dk_cols, axis=1).astype(_ST)
            dvb_ref[rows, :] = jnp.concatenate(dv_cols, axis=1).astype(_ST)
            dla_ref[rows, :] = jnp.concatenate(dg_cols, axis=1)

        da_pre = dla_ref[...] * (1.0 / B_GATE_TEMP) * (1.0 - _sigmoid(a_pre))
        dalr_ref[...] = _dot_nt(da_pre, wup_ref[...]).astype(_ST)
        dwup_ref[...] = dwup_ref[...] + _dot_tn(alr_ref[...], da_pre)
        db_ref[...] = db_ref[...] + jnp.sum(da_pre, axis=0, keepdims=True)

    blk = lambda s, i: s * nblk + nblk - 1 - i
    rows = lambda w: pl.BlockSpec((tb, w), lambda s, i: (blk(s, i), 0))
    full = lambda a, b: pl.BlockSpec((a, b), lambda s, i: (0, 0))
    act = lambda w: jax.ShapeDtypeStruct((t, w), _ST)
    return pl.pallas_call(
        body, name="gla_bwd", grid=(nseq, nblk),
        in_specs=[rows(512), rows(512), rows(512), rows(LANE), rows(512), rows(512),
                  pl.BlockSpec((cpb, B_KEY_WIDTH, B_VAL_DIM), lambda s, i: (blk(s, i), 0, 0)),
                  full(LANE, B_KEY_WIDTH), full(1, B_KEY_WIDTH), full(1, B_WIDTH)],
        out_specs=[rows(512), rows(512), rows(512), rows(LANE), full(LANE, B_KEY_WIDTH),
                   full(1, B_KEY_WIDTH), full(1, B_WIDTH)],
        out_shape=[act(512), act(512), act(512), act(LANE),
                   jax.ShapeDtypeStruct((LANE, B_KEY_WIDTH), _F32),
                   jax.ShapeDtypeStruct((1, B_KEY_WIDTH), _F32), jax.ShapeDtypeStruct((1, B_WIDTH), _F32)],
        scratch_shapes=[pltpu.VMEM((B_KEY_WIDTH, B_VAL_DIM), _F32), pltpu.VMEM((tb, B_KEY_WIDTH), _F32)],
        compiler_params=_cparams(("arbitrary", "arbitrary")),
    )(qkb, vb, zb, alr, oraw, dob, sst, wup, b_alpha, gn)


def _merge_loss(oa, ob, ga, gb, x2, tgt, wa, wb, wo, g_final):
    t = x2.shape[0]
    tm = min(t, 256)
    nt = t // tm

    def body(oa_ref, ob_ref, ga_ref, gb_ref, x_ref, t_ref, wa_ref, wb_ref, wo_ref, gf_ref,
             dh_ref, doa_ref, dob_ref, dga_ref, dgb_ref, dwa_ref, dwb_ref, dwo_ref, dgf_ref, loss_ref):
        i = pl.program_id(0)

        @pl.when(i == 0)
        def _():
            dwa_ref[...] = jnp.zeros_like(dwa_ref)
            dwb_ref[...] = jnp.zeros_like(dwb_ref)
            dwo_ref[...] = jnp.zeros_like(dwo_ref)
            dgf_ref[...] = jnp.zeros_like(dgf_ref)
            loss_ref[...] = jnp.zeros_like(loss_ref)

        oa_v, ob_v = oa_ref[...], ob_ref[...]
        ya, yb = _dot(oa_v, wa_ref[...]), _dot(ob_v, wb_ref[...])
        sga, sgb = _sigmoid(ga_ref[...].astype(_F32)), _sigmoid(gb_ref[...].astype(_F32))
        merged = (sga * ya + sgb * yb).astype(_MX)
        out = x_ref[...] + _dot(merged, wo_ref[...])
        r = lax.rsqrt(jnp.mean(out * out, axis=-1, keepdims=True) + NORM_EPS)
        nrm = out * r
        gf = gf_ref[...]
        err = nrm * gf - t_ref[...]
        loss_ref[...] = loss_ref[...] + (0.5 / D_MODEL) * jnp.sum(err * err, axis=0, keepdims=True)
        dy = err * (1.0 / D_MODEL)
        dgf_ref[...] = dgf_ref[...] + jnp.sum(dy * nrm, axis=0, keepdims=True)
        dh = _rms_bwd(dy * gf, nrm, r)
        dh_ref[...] = dh
        dh_mx = dh.astype(_MX)
        dmer = _dot_nt(dh_mx, wo_ref[...])
        dwo_ref[...] = dwo_ref[...] + _dot_tn(merged, dh_mx)
        dya, dyb = (dmer * sga).astype(_MX), (dmer * sgb).astype(_MX)
        dga_ref[...] = (dmer * ya * sga * (1.0 - sga)).astype(_ST)
        dgb_ref[...] = (dmer * yb * sgb * (1.0 - sgb)).astype(_ST)
        doa_ref[...] = _dot_nt(dya, wa_ref[...]).astype(_ST)
        dob_ref[...] = _dot_nt(dyb, wb_ref[...]).astype(_ST)
        dwa_ref[...] = dwa_ref[...] + _dot_tn(oa_v, dya)
        dwb_ref[...] = dwb_ref[...] + _dot_tn(ob_v, dyb)

    rows = lambda w: pl.BlockSpec((tm, w), lambda i: (i, 0))
    full = lambda a, b: pl.BlockSpec((a, b), lambda i: (0, 0))
    return pl.pallas_call(
        body, name="merge_loss", grid=(nt,),
        in_specs=[rows(512), rows(512), rows(D_MODEL), rows(D_MODEL), rows(D_MODEL), rows(D_MODEL),
                  full(A_WIDTH, D_MODEL), full(B_WIDTH, D_MODEL), full(D_MODEL, D_MODEL), full(1, D_MODEL)],
        out_specs=[rows(D_MODEL), rows(512), rows(512), rows(D_MODEL), rows(D_MODEL),
                   full(A_WIDTH, D_MODEL), full(B_WIDTH, D_MODEL), full(D_MODEL, D_MODEL),
                   full(1, D_MODEL), full(1, D_MODEL)],
        out_shape=[jax.ShapeDtypeStruct((t, D_MODEL), _F32), jax.ShapeDtypeStruct((t, 512), _ST),
                   jax.ShapeDtypeStruct((t, 512), _ST), jax.ShapeDtypeStruct((t, D_MODEL), _ST),
                   jax.ShapeDtypeStruct((t, D_MODEL), _ST),
                   jax.ShapeDtypeStruct((A_WIDTH, D_MODEL), _F32), jax.ShapeDtypeStruct((B_WIDTH, D_MODEL), _F32),
                   jax.ShapeDtypeStruct((D_MODEL, D_MODEL), _F32), jax.ShapeDtypeStruct((1, D_MODEL), _F32),
                   jax.ShapeDtypeStruct((1, D_MODEL), _F32)],
        compiler_params=_cparams(("arbitrary",), VMEM_LIMIT),
    )(oa, ob, ga, gb, x2, tgt, wa, wb, wo, g_final)


def _in_proj_bwd_x(dpieces, wp, x2, dh2, g_in):
    t = x2.shape[0]
    tm = min(t, 512)
    np_ = len(PIECES)

    def body(*refs):
        dp_refs = refs[:np_]
        w_ref, x_ref, dh2_ref, g_ref, gx_ref, dg_ref = refs[np_:]

        @pl.when(pl.program_id(0) == 0)
        def _():
            dg_ref[...] = jnp.zeros_like(dg_ref)

        dh = None
        for (name, a, b), dp in zip(PIECES, dp_refs):
            part = _dot_nt(dp[...], w_ref[:, a:b])
            dh = part if dh is None else dh + part
        xv = x_ref[...]
        r = lax.rsqrt(jnp.mean(xv * xv, axis=-1, keepdims=True) + NORM_EPS)
        nrm = xv * r
        dg_ref[...] = dg_ref[...] + jnp.sum(dh * nrm, axis=0, keepdims=True)
        gx_ref[...] = dh2_ref[...] + _rms_bwd(dh * g_ref[...], nrm, r)

    rows = lambda w: pl.BlockSpec((tm, w), lambda i: (i, 0))
    full = lambda a, b: pl.BlockSpec((a, b), lambda i: (0, 0))
    return pl.pallas_call(
        body, name="in_proj_bwd_x", grid=(t // tm,),
        in_specs=[rows(b - a) for _, a, b in PIECES] + [full(D_MODEL, D_PAD), rows(D_MODEL), rows(D_MODEL),
                                                          full(1, D_MODEL)],
        out_specs=[rows(D_MODEL), full(1, D_MODEL)],
        out_shape=[jax.ShapeDtypeStruct((t, D_MODEL), _F32), jax.ShapeDtypeStruct((1, D_MODEL), _F32)],
        compiler_params=_cparams(("arbitrary",), VMEM_LIMIT),
    )(*dpieces, wp, x2, dh2, g_in)


def _in_proj_bwd_w(h, dpieces, name):
    t = h.shape[0]
    tm = min(t, 512)
    np_ = len(dpieces)
    widths = [d.shape[1] for d in dpieces]

    def body(*refs):
        h_ref, dp_refs, dw_refs = refs[0], refs[1:1 + np_], refs[1 + np_:]

        @pl.when(pl.program_id(0) == 0)
        def _():
            for dw in dw_refs:
                dw[...] = jnp.zeros_like(dw)

        hv = h_ref[...]
        for dp, dw in zip(dp_refs, dw_refs):
            dw[...] = dw[...] + _dot_tn(hv, dp[...])

    rows = lambda w: pl.BlockSpec((tm, w), lambda i: (i, 0))
    return pl.pallas_call(
        body, name=name, grid=(t // tm,),
        in_specs=[rows(D_MODEL)] + [rows(w) for w in widths],
        out_specs=[pl.BlockSpec((D_MODEL, w), lambda i: (0, 0)) for w in widths],
        out_shape=[jax.ShapeDtypeStruct((D_MODEL, w), _F32) for w in widths],
        compiler_params=_cparams(("arbitrary",), VMEM_LIMIT),
    )(h, *dpieces)


def _place():
    return lax.axis_index("x"), lax.axis_index("y"), lax.axis_index("c")


def _other_chips(x, y):
    return [(1 - x, y), (x, 1 - y), (1 - x, 1 - y)]


def _gather_shards(shards):
    nw = len(shards)

    def body(*refs):
        ins, outs = refs[:nw], refs[nw:2 * nw]
        send, recv, lsem = refs[2 * nw:]
        x, y, c = _place()
        me = 2 * x + y
        copies = []
        for i in range(nw):
            mine = pltpu.make_async_copy(ins[i], outs[i].at[me], lsem.at[i])
            mine.start()
            copies.append(mine)
            for j, (px, py) in enumerate(_other_chips(x, y)):
                cp = pltpu.make_async_remote_copy(
                    src_ref=ins[i], dst_ref=outs[i].at[me], send_sem=send.at[3 * i + j],
                    recv_sem=recv.at[3 * i + j], device_id=(px, py, c), device_id_type=_MESH)
                cp.start()
                copies.append(cp)
        for cp in copies:
            cp.wait()

    return pl.pallas_call(
        body, name="gather_weights",
        in_specs=[_ANY] * nw, out_specs=[_ANY] * nw,
        out_shape=[jax.ShapeDtypeStruct((4,) + s.shape, s.dtype) for s in shards],
        scratch_shapes=[pltpu.SemaphoreType.DMA((3 * nw,)), pltpu.SemaphoreType.DMA((3 * nw,)),
                        pltpu.SemaphoreType.DMA((nw,))],
    )(*shards)


def _pair_exchange(grads):
    nw = len(grads)

    def body(*refs):
        ins, outs = refs[:nw], refs[nw:2 * nw]
        send, recv = refs[2 * nw:]
        x, y, c = _place()
        copies = []
        for i in range(nw):
            cp = pltpu.make_async_remote_copy(
                src_ref=ins[i].at[1 - c], dst_ref=outs[i], send_sem=send.at[i], recv_sem=recv.at[i],
                device_id=(x, y, 1 - c), device_id_type=_MESH)
            cp.start()
            copies.append(cp)
        for cp in copies:
            cp.wait()

    return pl.pallas_call(
        body, name="grad_pair_exchange",
        in_specs=[_ANY] * nw, out_specs=[_ANY] * nw,
        out_shape=[jax.ShapeDtypeStruct(g.shape[1:], g.dtype) for g in grads],
        scratch_shapes=[pltpu.SemaphoreType.DMA((nw,)), pltpu.SemaphoreType.DMA((nw,))],
    )(*grads)


def _add_own_half(g, r, c_arr, name):
    _, _, h, n = g.shape
    bh = min(h, 256)

    def body(c_ref, g_ref, r_ref, o_ref):
        o_ref[...] = g_ref[...] + r_ref[...]

    return pl.pallas_call(
        body, name=name,
        grid_spec=pltpu.PrefetchScalarGridSpec(
            num_scalar_prefetch=1, grid=(4, h // bh),
            in_specs=[pl.BlockSpec((None, None, bh, n), lambda k, i, c_ref: (c_ref[0], k, i, 0)),
                      pl.BlockSpec((None, bh, n), lambda k, i, c_ref: (k, i, 0))],
            out_specs=pl.BlockSpec((None, bh, n), lambda k, i, c_ref: (k, i, 0))),
        out_shape=jax.ShapeDtypeStruct(r.shape, r.dtype),
        compiler_params=_cparams(("parallel", "parallel")),
    )(c_arr, g, r)


def _chip_exchange(parts):
    nw = len(parts)

    def body(*refs):
        ins, outs = refs[:nw], refs[nw:2 * nw]
        send, recv, lsem = refs[2 * nw:]
        x, y, c = _place()
        me = 2 * x + y
        copies = []
        for i in range(nw):
            mine = pltpu.make_async_copy(ins[i].at[me], outs[i].at[me], lsem.at[i])
            mine.start()
            copies.append(mine)
            for j, (px, py) in enumerate(_other_chips(x, y)):
                cp = pltpu.make_async_remote_copy(
                    src_ref=ins[i].at[2 * px + py], dst_ref=outs[i].at[me], send_sem=send.at[3 * i + j],
                    recv_sem=recv.at[3 * i + j], device_id=(px, py, c), device_id_type=_MESH)
                cp.start()
                copies.append(cp)
        for cp in copies:
            cp.wait()

    return pl.pallas_call(
        body, name="grad_chip_exchange",
        in_specs=[_ANY] * nw, out_specs=[_ANY] * nw,
        out_shape=[jax.ShapeDtypeStruct(p.shape, p.dtype) for p in parts],
        scratch_shapes=[pltpu.SemaphoreType.DMA((3 * nw,)), pltpu.SemaphoreType.DMA((3 * nw,)),
                        pltpu.SemaphoreType.DMA((nw,))],
    )(*parts)


def _sum_chips(q, name):
    _, h, n = q.shape
    bh = min(h, 256)

    def body(q_ref, o_ref):
        o_ref[...] = ((q_ref[0] + q_ref[1]) + q_ref[2]) + q_ref[3]

    return pl.pallas_call(
        body, name=name, grid=(h // bh,),
        in_specs=[pl.BlockSpec((4, bh, n), lambda i: (0, i, 0))],
        out_specs=pl.BlockSpec((bh, n), lambda i: (i, 0)),
        out_shape=jax.ShapeDtypeStruct((h, n), q.dtype),
        compiler_params=_cparams(("parallel",)),
    )(q)


def _pair_share(halves):
    nw = len(halves)

    def body(*refs):
        ins, outs = refs[:nw], refs[nw:2 * nw]
        send, recv, lsem = refs[2 * nw:]
        x, y, c = _place()
        copies = []
        for i in range(nw):
            mine = pltpu.make_async_copy(ins[i], outs[i].at[c], lsem.at[i])
            mine.start()
            copies.append(mine)
            cp = pltpu.make_async_remote_copy(
                src_ref=ins[i], dst_ref=outs[i].at[c], send_sem=send.at[i], recv_sem=recv.at[i],
                device_id=(x, y, 1 - c), device_id_type=_MESH)
            cp.start()
            copies.append(cp)
        for cp in copies:
            cp.wait()

    return pl.pallas_call(
        body, name="grad_pair_share",
        in_specs=[_ANY] * nw, out_specs=[_ANY] * nw,
        out_shape=[jax.ShapeDtypeStruct((2,) + s.shape, s.dtype) for s in halves],
        scratch_shapes=[pltpu.SemaphoreType.DMA((nw,)), pltpu.SemaphoreType.DMA((nw,)),
                        pltpu.SemaphoreType.DMA((nw,))],
    )(*halves)


def _all_sum_small(v):
    n = v.shape[1]

    def body(v_ref, tot_ref, all_ref, send, recv):
        x, y, c = _place()
        me = 4 * x + 2 * y + c
        all_ref[me] = v_ref[...]
        copies = []
        for r in range(1, 8):
            px = 1 - x if r & 4 else x
            py = 1 - y if r & 2 else y
            pc = 1 - c if r & 1 else c
            cp = pltpu.make_async_remote_copy(
                src_ref=v_ref, dst_ref=all_ref.at[me], send_sem=send.at[r - 1], recv_sem=recv.at[r - 1],
                device_id=(px, py, pc), device_id_type=_MESH)
            cp.start()
            copies.append(cp)
        for cp in copies:
            cp.wait()
        acc = all_ref[0]
        for d in range(1, 8):
            acc = acc + all_ref[d]
        tot_ref[...] = acc

    vm = pl.BlockSpec(memory_space=pltpu.VMEM)
    return pl.pallas_call(
        body, name="small_all_sum",
        in_specs=[vm], out_specs=[vm, vm],
        out_shape=[jax.ShapeDtypeStruct((8, n), _F32), jax.ShapeDtypeStruct((8, 8, n), _F32)],
        scratch_shapes=[pltpu.SemaphoreType.DMA((7,)), pltpu.SemaphoreType.DMA((7,))],
    )(v)[0]


def _adamw(w, g, m, v, name):
    r, n = w.shape
    br = r
    for cand in (256, 128):
        if r > cand and r % cand == 0:
            br = cand
            break

    def body(w_ref, g_ref, m_ref, v_ref, d_ref, nm_ref, nv_ref):
        gv = g_ref[...]
        m2 = ADAM_B1 * m_ref[...] + (1.0 - ADAM_B1) * gv
        v2 = ADAM_B2 * v_ref[...] + (1.0 - ADAM_B2) * (gv * gv)
        m_hat = m2 / (1.0 - ADAM_B1 ** ADAM_STEP)
        v_hat = v2 / (1.0 - ADAM_B2 ** ADAM_STEP)
        d_ref[...] = -ADAM_LR * (m_hat / (jnp.sqrt(v_hat) + ADAM_EPS) + ADAM_WD * w_ref[...])
        nm_ref[...] = m2
        nv_ref[...] = v2

    blk = pl.BlockSpec((br, n), lambda i: (i, 0))
    shp = jax.ShapeDtypeStruct((r, n), _F32)
    return pl.pallas_call(
        body, name=name, grid=(r // br,),
        in_specs=[blk] * 4, out_specs=[blk] * 3, out_shape=[shp] * 3,
        compiler_params=_cparams(("parallel",)),
    )(w, g, m, v)


def _pad_cols(w_full):
    r = w_full.shape[0]
    e = ALR_AT + B_GATE_RANK
    return jnp.concatenate([w_full[:, :e], jnp.zeros((r, LANE - B_GATE_RANK), w_full.dtype), w_full[:, e:]], axis=1)


def _unpad_cols(w_pad):
    e = ALR_AT + B_GATE_RANK
    return jnp.concatenate([w_pad[:, :e], w_pad[:, ALR_AT + LANE:]], axis=1)


def kernel(x, positions, g_in, w_in, w_alpha_up, b_alpha, attn_sinks, g_gla_norm, w_out_a, w_out_b, w_o, g_final, loss_target, m_g_in, m_w_in, m_w_alpha_up, m_b_alpha, m_attn_sinks, m_g_gla_norm, m_w_out_a, m_w_out_b, m_w_o, m_g_final, v_g_in, v_w_in, v_w_alpha_up, v_b_alpha, v_attn_sinks, v_g_gla_norm, v_w_out_a, v_w_out_b, v_w_o, v_g_final):
    nseq, seq, _ = x.shape
    t = nseq * seq
    cx, cy, cc = _place()
    chip = 2 * cx + cy

    shards = [w_in[0].astype(_MX), w_out_a[0].astype(_MX), w_out_b[0].astype(_MX), w_o[0].astype(_MX),
              w_alpha_up[0].astype(_MX)]
    win_g, wa_g, wb_g, wo_g, wup_g = _gather_shards(shards)
    cols = lambda g4: jnp.transpose(g4, (1, 0, 2)).reshape(g4.shape[1], 4 * g4.shape[2])
    wp = _pad_cols(cols(win_g))
    wa, wb = cols(wa_g), cols(wb_g)
    wo = wo_g.reshape(D_MODEL, D_MODEL)
    wup = jnp.concatenate([cols(wup_g), jnp.zeros((LANE - B_GATE_RANK, B_KEY_WIDTH), _MX)], axis=0)

    x2 = x.reshape(t, D_MODEL)
    tgt = loss_target.reshape(t, D_MODEL)
    pos_f = positions.astype(_F32).reshape(t, 1)
    sinks = attn_sinks.reshape(A_HEADS)
    gf = g_final.reshape(1, D_MODEL)

    cos, sa, sb = _rope_tables(pos_f)
    h, qkv, za, qkb, vb, zb, alr, ga, gb = _in_proj(x2, g_in, wp, cos, sa, sb)
    oa = _attn_fwd(qkv, za, sinks, nseq)
    ob, oraw, sst = _gla_fwd(qkb, vb, zb, alr, wup, b_alpha, g_gla_norm, nseq)

    dh2, doa, dob, dga, dgb, dwa, dwb, dwo, dgf, lossv = _merge_loss(oa, ob, ga, gb, x2, tgt, wa, wb, wo, gf)

    dqkv, dza, dsink = _attn_bwd(qkv, za, doa, sinks, cos, sa, sb, nseq)
    dqkb, dvb, dzb, dalr, dwup, dba, dgn = _gla_bwd(qkb, vb, zb, alr, oraw, dob, sst, wup, b_alpha, g_gla_norm, nseq)
    dpieces = [dqkv, dza, dqkb, dvb, dzb, dalr, dga, dgb]
    grad_x2, dgin = _in_proj_bwd_x(dpieces, wp, x2, dh2, g_in)
    dw_lo = _in_proj_bwd_w(h, dpieces[:6], "in_proj_bwd_w_mixers")
    dw_hi = _in_proj_bwd_w(h, dpieces[6:], "in_proj_bwd_w_gates")
    dwin = _unpad_cols(jnp.concatenate(list(dw_lo) + list(dw_hi), axis=1))

    def by_half_cols(gw):
        r, n4 = gw.shape
        return jnp.transpose(gw.reshape(2, r // 2, 4, n4 // 4), (0, 2, 1, 3))

    def by_half_rows(gw):
        r4, n = gw.shape
        return jnp.transpose(gw.reshape(4, 2, r4 // 8, n), (1, 0, 2, 3))

    g_halves = [by_half_cols(dwin), by_half_cols(dwa), by_half_cols(dwb), by_half_rows(dwo)]
    from_sibling = _pair_exchange(g_halves)
    c_arr = jnp.reshape(cc, (1,)).astype(jnp.int32)
    names = ("w_in", "w_out_a", "w_out_b", "w_o")
    pair_sums = [_add_own_half(g, r, c_arr, "pair_sum_" + nm) for g, r, nm in zip(g_halves, from_sibling, names)]
    from_chips = _chip_exchange(pair_sums)
    my_halves = [_sum_chips(q, "chip_sum_" + nm) for q, nm in zip(from_chips, names)]
    full = _pair_share(my_halves)
    g_win, g_wa, g_wb, g_wo = [f.reshape(2 * f.shape[1], f.shape[2]) for f in full]

    small = jnp.concatenate([
        dgin, dgf, dgn, dba,
        jnp.pad(dsink[:, 0].reshape(1, A_HEADS), ((0, 0), (0, LANE - A_HEADS))),
        jnp.pad(jnp.sum(lossv, axis=1, keepdims=True), ((0, 0), (0, LANE - 1))),
        dwup[:B_GATE_RANK].reshape(1, B_GATE_RANK * B_KEY_WIDTH)], axis=1)
    tot = _all_sum_small(jnp.pad(small, ((0, 7), (0, 0))))[0:1]
    o = 0
    def take(n):
        nonlocal o
        o += n
        return tot[:, o - n:o]
    g_gin, g_gf, g_gn, g_ba = take(D_MODEL), take(D_MODEL), take(B_WIDTH), take(B_KEY_WIDTH)
    g_sink = take(LANE)[:, :A_HEADS]
    loss = take(LANE)[0, 0]
    g_wup_full = take(B_GATE_RANK * B_KEY_WIDTH).reshape(B_GATE_RANK, B_KEY_WIDTH)
    nup = B_KEY_WIDTH // 4
    g_wup = lax.dynamic_slice(g_wup_full, (0, chip * nup), (B_GATE_RANK, nup))

    def pack(*parts):
        return jnp.concatenate([p.reshape(1, -1) for p in parts], axis=1)

    sm_w = pack(g_in, g_final, g_gla_norm, b_alpha, attn_sinks, w_alpha_up)
    sm_g = pack(g_gin, g_gf, g_gn, g_ba, g_sink, g_wup)
    sm_m = pack(m_g_in, m_g_final, m_g_gla_norm, m_b_alpha, m_attn_sinks, m_w_alpha_up)
    sm_v = pack(v_g_in, v_g_final, v_g_gla_norm, v_b_alpha, v_attn_sinks, v_w_alpha_up)
    sm_out = _adamw(sm_w, sm_g, sm_m, sm_v, "adamw_small")

    def unpack(p):
        sizes = (D_MODEL, D_MODEL, B_WIDTH, B_KEY_WIDTH, A_HEADS, B_GATE_RANK * nup)
        outs, at = [], 0
        for s in sizes:
            outs.append(p[:, at:at + s])
            at += s
        gi, gfin, gnn, ba, sk, wu = outs
        return dict(g_in=gi, g_final=gfin.reshape(D_MODEL), g_gla_norm=gnn, b_alpha=ba, attn_sinks=sk,
                    w_alpha_up=wu.reshape(1, B_GATE_RANK, nup))

    big = {}
    for nm, w, g, m, v in (("w_in", w_in, g_win, m_w_in, v_w_in), ("w_out_a", w_out_a, g_wa, m_w_out_a, v_w_out_a),
                           ("w_out_b", w_out_b, g_wb, m_w_out_b, v_w_out_b), ("w_o", w_o, g_wo, m_w_o, v_w_o)):
        d, nm_, nv_ = _adamw(w[0], g, m[0], v[0], "adamw_" + nm)
        big[nm] = (g[None], d[None], nm_[None], nv_[None])

    order = ("g_in", "w_in", "w_alpha_up", "b_alpha", "attn_sinks", "g_gla_norm", "w_out_a", "w_out_b", "w_o", "g_final")
    small_sets = [unpack(sm_g)] + [unpack(p) for p in sm_out]
    outs = []
    for kind in range(4):
        for nm in order:
            outs.append(big[nm][kind] if nm in big else small_sets[kind][nm])
    return (loss, grad_x2.reshape(x.shape), *outs)
```

```python
import math

import numpy as np
import jax
import jax.numpy as jnp
from jax import lax
from jax.experimental import pallas as pl
from jax.experimental.pallas import tpu as pltpu

D_MODEL = 1024
A_HEADS, A_KV_HEADS, A_HEAD_DIM = 8, 2, 64
A_GROUP = A_HEADS // A_KV_HEADS
A_WIDTH, A_KV_WIDTH = 512, 128
BLOCK = 128
ROPE_THETA = 500000.0
ROPE_DIM = 16
B_HEADS, B_KEY_DIM, B_VAL_DIM = 4, 64, 128
B_KEY_WIDTH, B_WIDTH = 256, 512
B_GATE_RANK = 16
B_GATE_TEMP = 16.0
B_CHUNK = 64
NORM_EPS = 1e-6
NEG_BIG = -1e30
D_IN = 4880

ADAM_LR, ADAM_B1, ADAM_B2, ADAM_EPS, ADAM_WD, ADAM_STEP = 0.001, 0.9, 0.999, 1e-08, 0.01, 10

LANE = 128
ALR_AT = 2816
D_PAD = D_IN + (LANE - B_GATE_RANK)
PIECES = (("qkv", 0, 768), ("za", 768, 1280), ("qkb", 1280, 1792), ("vb", 1792, 2304),
          ("zb", 2304, 2816), ("alr", 2816, 2944), ("ga", 2944, 3968), ("gb", 3968, 4992))

GLA_BLOCK = 256
VMEM_LIMIT = 56 * 1024 * 1024

_F32 = jnp.float32
_MX = jnp.bfloat16
_ST = jnp.bfloat16

_MESH = pl.DeviceIdType.MESH
_ANY = pl.BlockSpec(memory_space=pl.ANY)


def _cparams(sem=None, vmem=None):
    return pltpu.CompilerParams(dimension_semantics=sem, vmem_limit_bytes=vmem)


def _dot(a, b):
    return jnp.dot(a.astype(_MX), b.astype(_MX), preferred_element_type=_F32)


def _dot_nt(a, b):
    return lax.dot_general(a.astype(_MX), b.astype(_MX), (((1,), (1,)), ((), ())),
                           preferred_element_type=_F32)


def _dot_tn(a, b):
    return lax.dot_general(a.astype(_MX), b.astype(_MX), (((0,), (0,)), ((), ())),
                           preferred_element_type=_F32)


def _dot_ones(ones_mat, v):
    o = ones_mat.astype(jnp.bfloat16)
    v0 = v.astype(jnp.bfloat16)
    r1 = v - v0.astype(_F32)
    v1 = r1.astype(jnp.bfloat16)
    v2 = (r1 - v1.astype(_F32)).astype(jnp.bfloat16)
    d = lambda t: jnp.dot(o, t, preferred_element_type=_F32)
    return d(v0) + d(v1) + d(v2)


def _sigmoid(x):
    return 1.0 / (1.0 + jnp.exp(-x))


def _log_sigmoid(x):
    return jnp.minimum(x, 0.0) - jnp.log(1.0 + jnp.exp(-jnp.abs(x)))


def _lane_tile(t, width):
    reps = width // t.shape[1]
    return t if reps == 1 else jnp.tile(t, (1, reps))


def _rope(t, cos, sa, sb, sign):
    w = t.shape[1]
    rot = pltpu.roll(t, w - 8, 1) * _lane_tile(sa, w) + pltpu.roll(t, 8, 1) * _lane_tile(sb, w)
    return t * _lane_tile(cos, w) + sign * rot


def _rms_bwd(dy_g, n, r):
    return r * (dy_g - n * jnp.mean(dy_g * n, axis=-1, keepdims=True))


def _rope_tables(pos_f):
    t = pos_f.shape[0]
    tm = min(t, 1024)
    lane = np.arange(LANE) % A_HEAD_DIM
    half = ROPE_DIM // 2
    inv = np.exp((np.float32(-math.log(ROPE_THETA)) * np.arange(half, dtype=np.float32)) * np.float32(2.0 / ROPE_DIM))
    consts = np.zeros((8, LANE), np.float32)
    consts[0] = np.where(lane < ROPE_DIM, inv[lane % half], 0.0)
    consts[1] = np.where(lane < half, -1.0, 0.0)
    consts[2] = np.where((lane >= half) & (lane < ROPE_DIM), 1.0, 0.0)

    def body(pos_ref, c_ref, cos_ref, sa_ref, sb_ref):
        ang = pos_ref[...] * c_ref[0:1, :]
        s = jnp.sin(ang)
        cos_ref[...] = jnp.cos(ang)
        sa_ref[...] = s * c_ref[1:2, :]
        sb_ref[...] = s * c_ref[2:3, :]

    tab = jax.ShapeDtypeStruct((t, LANE), _F32)
    row = pl.BlockSpec((tm, LANE), lambda i: (i, 0))
    return pl.pallas_call(
        body, name="rope_tables", grid=(t // tm,),
        in_specs=[pl.BlockSpec((tm, 1), lambda i: (i, 0)), pl.BlockSpec((8, LANE), lambda i: (0, 0))],
        out_specs=[row, row, row], out_shape=[tab, tab, tab],
        compiler_params=_cparams(("parallel",)),
    )(pos_f, jnp.asarray(consts))


def _in_proj(x2, g_in, wp, cos, sa, sb):
    t = x2.shape[0]
    tm = min(t, 512)

    def body(x_ref, g_ref, w_ref, cos_ref, sa_ref, sb_ref, h_ref, qkv_ref, za_ref, qkb_ref,
             vb_ref, zb_ref, alr_ref, ga_ref, gb_ref):
        xv = x_ref[...]
        r = lax.rsqrt(jnp.mean(xv * xv, axis=-1, keepdims=True) + NORM_EPS)
        h = (xv * r * g_ref[...]).astype(_MX)
        h_ref[...] = h.astype(_ST)
        outs = dict(za=za_ref, qkb=qkb_ref, vb=vb_ref, zb=zb_ref, alr=alr_ref, ga=ga_ref, gb=gb_ref)
        for name, a, b in PIECES:
            p = jnp.dot(h, w_ref[:, a:b], preferred_element_type=_F32)
            if name == "qkv":
                c, s1, s2 = cos_ref[...], sa_ref[...], sb_ref[...]
                qkv_ref[:, 0:512] = _rope(p[:, 0:512], c, s1, s2, 1.0).astype(_ST)
                qkv_ref[:, 512:640] = _rope(p[:, 512:640], c, s1, s2, 1.0).astype(_ST)
                qkv_ref[:, 640:768] = p[:, 640:768].astype(_ST)
            else:
                outs[name][...] = p.astype(outs[name].dtype)

    rows = lambda w: pl.BlockSpec((tm, w), lambda i: (i, 0))
    shp = lambda name, w: jax.ShapeDtypeStruct((t, w), _F32 if name == "qkb" else _ST)
    widths = [D_MODEL] + [b - a for _, a, b in PIECES]
    return pl.pallas_call(
        body, name="in_proj", grid=(t // tm,),
        in_specs=[rows(D_MODEL), pl.BlockSpec((1, D_MODEL), lambda i: (0, 0)),
                  pl.BlockSpec((D_MODEL, D_PAD), lambda i: (0, 0)), rows(LANE), rows(LANE), rows(LANE)],
        out_specs=[rows(w) for w in widths],
        out_shape=[shp(n, w) for n, w in zip(["h"] + [p[0] for p in PIECES], widths)],
        compiler_params=_cparams(("parallel",), VMEM_LIMIT),
    )(x2, g_in, wp, cos, sa, sb)


def _attn_operands(qkv_ref, kvp_ref, want_bwd):
    kf = jnp.concatenate([kvp_ref[:, 0:128], qkv_ref[:, 512:640]], axis=0).astype(_F32) * (A_HEAD_DIM ** -0.5)
    vf = jnp.concatenate([kvp_ref[:, 128:256], qkv_ref[:, 640:768]], axis=0).astype(_F32)
    lo = lax.broadcasted_iota(jnp.int32, (1, LANE), 1) < 64

    def on_lanes(a):
        sw = pltpu.roll(a, 64, 1)
        z = jnp.zeros_like(a)
        return [[jnp.where(lo, a, z).astype(_MX), jnp.where(lo, z, sw).astype(_MX)],
                [jnp.where(lo, sw, z).astype(_MX), jnp.where(lo, z, a).astype(_MX)]]

    def on_rows(a):
        at = a.T.astype(_MX)
        z = jnp.zeros((64, at.shape[1]), _MX)
        top, bot = at[0:64], at[64:128]
        return [[jnp.concatenate([top, z], axis=0), jnp.concatenate([z, top], axis=0)],
                [jnp.concatenate([bot, z], axis=0), jnp.concatenate([z, bot], axis=0)]]

    ops = dict(k_lanes=on_lanes(kf), v_rows=on_rows(vf), lo=lo)
    if want_bwd:
        ops.update(v_lanes=on_lanes(vf), k_rows=on_rows(kf))
    return ops


def _attn_valid(n):
    kj = lax.broadcasted_iota(jnp.int32, (2 * BLOCK, BLOCK), 0) - BLOCK
    qi = lax.broadcasted_iota(jnp.int32, (2 * BLOCK, BLOCK), 1)
    return (kj <= qi) & (qi - kj < BLOCK) & ((n > 0) | (kj >= 0))


def _attn_softmax_t(k_lanes, q_pair, valid, sink):
    s = jnp.where(valid, _dot_nt(k_lanes, q_pair), NEG_BIG)
    m = jnp.maximum(jnp.max(s, axis=0, keepdims=True), sink)
    e = jnp.exp(s - m)
    e_sink = jnp.exp(sink - m)
    inv = 1.0 / (jnp.sum(e, axis=0, keepdims=True) + e_sink)
    return e, e_sink, inv


def _attn_fwd(qkv, za, sinks, nseq):
    t = qkv.shape[0]
    nb = t // nseq // BLOCK

    def body(sink_ref, qkv_ref, kvp_ref, za_ref, oa_ref):
        n = pl.program_id(1)
        ops = _attn_operands(qkv_ref, kvp_ref, False)
        valid = _attn_valid(n)
        for pr in range(A_HEADS // 2):
            lanes = slice(pr * LANE, (pr + 1) * LANE)
            g = pr // (A_GROUP // 2)
            q_pair = qkv_ref[:, lanes]
            ot = None
            for half in range(2):
                e, _, inv = _attn_softmax_t(ops["k_lanes"][g][half], q_pair, valid, sink_ref[2 * pr + half])
                part = _dot(ops["v_rows"][g][half], e) * inv
                ot = part if ot is None else ot + part
            z = za_ref[:, lanes].astype(_F32)
            oa_ref[:, lanes] = (ot.T * (z * _sigmoid(z))).astype(_ST)

    cur = lambda w: pl.BlockSpec((BLOCK, w), lambda s, n: (s * nb + n, 0))
    return pl.pallas_call(
        body, name="attn_fwd", grid=(nseq, nb),
        in_specs=[pl.BlockSpec(memory_space=pltpu.SMEM), cur(768),
                  pl.BlockSpec((BLOCK, 256), lambda s, n: (s * nb + jnp.maximum(n - 1, 0), 2)), cur(512)],
        out_specs=cur(512), out_shape=jax.ShapeDtypeStruct((t, A_WIDTH), _ST),
        compiler_params=_cparams(("parallel", "arbitrary")),
    )(sinks, qkv, qkv, za)


def _attn_bwd(qkv, za, doa, sinks, cos, sa, sb, nseq):
    t = qkv.shape[0]
    nb = t // nseq // BLOCK

    def body(sink_ref, qkv_ref, kvp_ref, za_ref, doa_ref, cos_ref, sa_ref, sb_ref,
             dqkv_ref, dza_ref, dsink_ref, ck_ref, cv_ref):
        s_id, i = pl.program_id(0), pl.program_id(1)
        n = nb - 1 - i

        @pl.when((s_id == 0) & (i == 0))
        def _():
            dsink_ref[...] = jnp.zeros_like(dsink_ref)

        @pl.when(i == 0)
        def _():
            ck_ref[...] = jnp.zeros_like(ck_ref)
            cv_ref[...] = jnp.zeros_like(cv_ref)

        ops = _attn_operands(qkv_ref, kvp_ref, True)
        lo = ops["lo"]
        valid = _attn_valid(n)
        dk_acc = [None] * A_KV_HEADS
        dv_acc = [None] * A_KV_HEADS
        dq_pairs = []
        for pr in range(A_HEADS // 2):
            lanes = slice(pr * LANE, (pr + 1) * LANE)
            g = pr // (A_GROUP // 2)
            q_pair = qkv_ref[:, lanes]
            z = za_ref[:, lanes].astype(_F32)
            sz = _sigmoid(z)
            d_oa = doa_ref[:, lanes].astype(_F32)
            d_att = d_oa * (z * sz)
            zero = jnp.zeros_like(d_att)
            q_f = q_pair.astype(_F32)
            ot, dqt = None, None
            for half in range(2):
                h = 2 * pr + half
                e, e_sink, inv = _attn_softmax_t(ops["k_lanes"][g][half], q_pair, valid, sink_ref[h])
                pn = e * inv
                dpt = _dot_nt(ops["v_lanes"][g][half], d_att)
                delta = jnp.sum(pn * dpt, axis=0, keepdims=True)
                ds = (pn * (dpt - delta)).astype(_MX)
                pn = pn.astype(_MX)
                dsink_ref[h:h + 1, :] = dsink_ref[h:h + 1, :] - jnp.sum(e_sink * inv * delta)
                o_part = _dot(ops["v_rows"][g][half], pn)
                dq_part = _dot(ops["k_rows"][g][half], ds)
                ot = o_part if ot is None else ot + o_part
                dqt = dq_part if dqt is None else dqt + dq_part
                mine = lo if half == 0 else jnp.logical_not(lo)
                dk_part = _dot(ds, jnp.where(mine, q_f, zero))
                dv_part = _dot(pn, jnp.where(mine, d_att, zero))
                dk_acc[g] = dk_part if dk_acc[g] is None else dk_acc[g] + dk_part
                dv_acc[g] = dv_part if dv_acc[g] is None else dv_acc[g] + dv_part
            dza_ref[:, lanes] = (d_oa * ot.T * (sz * (1.0 + z * (1.0 - sz)))).astype(_ST)
            dq_pairs.append(dqt.T)

        def fold(acc, scale):
            both = [a + pltpu.roll(a, 64, 1) for a in acc]
            return jnp.where(lo, both[0], both[1]) * scale

        dk_full = fold(dk_acc, A_HEAD_DIM ** -0.5)
        dv_full = fold(dv_acc, 1.0)
        dk_cur = dk_full[BLOCK:] + ck_ref[...]
        dv_cur = dv_full[BLOCK:] + cv_ref[...]
        ck_ref[...] = dk_full[:BLOCK]
        cv_ref[...] = dv_full[:BLOCK]
        c, s1, s2 = cos_ref[...], sa_ref[...], sb_ref[...]
        dqkv_ref[:, 0:512] = _rope(jnp.concatenate(dq_pairs, axis=1), c, s1, s2, -1.0).astype(_ST)
        dqkv_ref[:, 512:640] = _rope(dk_cur, c, s1, s2, -1.0).astype(_ST)
        dqkv_ref[:, 640:768] = dv_cur.astype(_ST)

    cur = lambda w: pl.BlockSpec((BLOCK, w), lambda s, i: (s * nb + nb - 1 - i, 0))
    return pl.pallas_call(
        body, name="attn_bwd", grid=(nseq, nb),
        in_specs=[pl.BlockSpec(memory_space=pltpu.SMEM), cur(768),
                  pl.BlockSpec((BLOCK, 256), lambda s, i: (s * nb + jnp.maximum(nb - 2 - i, 0), 2)),
                  cur(512), cur(512), cur(LANE), cur(LANE), cur(LANE)],
        out_specs=[cur(768), cur(512), pl.BlockSpec((8, LANE), lambda s, i: (0, 0))],
        out_shape=[jax.ShapeDtypeStruct((t, 768), _ST), jax.ShapeDtypeStruct((t, 512), _ST),
                   jax.ShapeDtypeStruct((8, LANE), _F32)],
        scratch_shapes=[pltpu.VMEM((BLOCK, A_KV_WIDTH), _F32), pltpu.VMEM((BLOCK, A_KV_WIDTH), _F32)],
        compiler_params=_cparams(("arbitrary", "arbitrary")),
    )(sinks, qkv, qkv, za, doa, cos, sa, sb)


def _gla_chunk_terms(la, qkb_ref, r0):
    g = la[r0:r0 + B_CHUNK, :]
    ri = lax.broadcasted_iota(jnp.int32, (B_CHUNK, B_CHUNK), 0)
    ci = lax.broadcasted_iota(jnp.int32, (B_CHUNK, B_CHUNK), 1)
    cum = _dot_ones((ri >= ci).astype(_F32), g)
    last = cum[B_CHUNK - 1:B_CHUNK, :]
    mid = cum[B_CHUNK // 2 - 1:B_CHUNK // 2, :]
    q = qkb_ref[r0:r0 + B_CHUNK, 0:B_KEY_WIDTH].astype(_F32) * (B_KEY_DIM ** -0.5)
    k = qkb_ref[r0:r0 + B_CHUNK, B_KEY_WIDTH:2 * B_KEY_WIDTH].astype(_F32)
    e_q, e_k, e_l, e_c = jnp.exp(cum - mid), jnp.exp(mid - cum), jnp.exp(last - cum), jnp.exp(cum)
    dec_col = jnp.exp(jnp.sum(g.T, axis=1, keepdims=True))
    return dict(qm=q * e_q, km=k * e_k, kl=k * e_l, qc=q * e_c, e_q=e_q, e_k=e_k, e_l=e_l, e_c=e_c,
                dec_col=dec_col, dec_row=jnp.exp(last), causal=ri >= ci, ri=ri)


def _gate_logits(alr_ref, wup_ref, b_ref):
    return _dot(alr_ref[...], wup_ref[...]) + b_ref[...]


def _gla_fwd(qkb, vb, zb, alr, wup, b_alpha, gn, nseq):
    t = qkb.shape[0]
    tb = min(GLA_BLOCK, t // nseq)
    nblk = t // nseq // tb
    cpb = tb // B_CHUNK

    def body(qkb_ref, vb_ref, zb_ref, alr_ref, wup_ref, b_ref, gn_ref, ob_ref, oraw_ref, sst_ref, s_ref):
        @pl.when(pl.program_id(1) == 0)
        def _():
            s_ref[...] = jnp.zeros_like(s_ref)

        la = _log_sigmoid(_gate_logits(alr_ref, wup_ref, b_ref)) * (1.0 / B_GATE_TEMP)
        o_rows = []
        for c in range(cpb):
            r0 = c * B_CHUNK
            tm = _gla_chunk_terms(la, qkb_ref, r0)
            o_heads = []
            for h in range(B_HEADS):
                kl_, vl_ = slice(h * 64, (h + 1) * 64), slice(h * 128, (h + 1) * 128)
                st = s_ref[kl_, :]
                sst_ref[c, kl_, :] = st
                v = vb_ref[r0:r0 + B_CHUNK, vl_]
                a = jnp.where(tm["causal"], _dot_nt(tm["qm"][:, kl_], tm["km"][:, kl_]), 0.0)
                o_heads.append(_dot(a, v) + _dot(tm["qc"][:, kl_], st))
                s_ref[kl_, :] = tm["dec_col"][kl_, :] * st + _dot_tn(tm["kl"][:, kl_], v)
            o_rows.append(jnp.concatenate(o_heads, axis=1))
        o = jnp.concatenate(o_rows, axis=0)
        oraw_ref[...] = o
        z = zb_ref[...].astype(_F32)
        gate = z * _sigmoid(z)
        for h in range(B_HEADS):
            vl_ = slice(h * 128, (h + 1) * 128)
            oh = o[:, vl_]
            r = lax.rsqrt(jnp.mean(oh * oh, axis=-1, keepdims=True) + NORM_EPS)
            ob_ref[:, vl_] = ((oh * r) * gn_ref[:, vl_] * gate[:, vl_]).astype(_ST)

    rows = lambda w: pl.BlockSpec((tb, w), lambda s, i: (s * nblk + i, 0))
    full = lambda a, b: pl.BlockSpec((a, b), lambda s, i: (0, 0))
    return pl.pallas_call(
        body, name="gla_fwd", grid=(nseq, nblk),
        in_specs=[rows(512), rows(512), rows(512), rows(LANE), full(LANE, B_KEY_WIDTH),
                  full(1, B_KEY_WIDTH), full(1, B_WIDTH)],
        out_specs=[rows(512), rows(512),
                   pl.BlockSpec((cpb, B_KEY_WIDTH, B_VAL_DIM), lambda s, i: (s * nblk + i, 0, 0))],
        out_shape=[jax.ShapeDtypeStruct((t, B_WIDTH), _ST), jax.ShapeDtypeStruct((t, B_WIDTH), _F32),
                   jax.ShapeDtypeStruct((t // B_CHUNK, B_KEY_WIDTH, B_VAL_DIM), _F32)],
        scratch_shapes=[pltpu.VMEM((B_KEY_WIDTH, B_VAL_DIM), _F32)],
        compiler_params=_cparams(("parallel", "arbitrary")),
    )(qkb, vb, zb, alr, wup, b_alpha, gn)


def _gla_bwd(qkb, vb, zb, alr, oraw, dob, sst, wup, b_alpha, gn, nseq):
    t = qkb.shape[0]
    tb = min(GLA_BLOCK, t // nseq)
    nblk = t // nseq // tb
    cpb = tb // B_CHUNK

    def body(qkb_ref, vb_ref, zb_ref, alr_ref, oraw_ref, dob_ref, sst_ref, wup_ref, b_ref, gn_ref,
             dqkb_ref, dvb_ref, dzb_ref, dalr_ref, dwup_ref, db_ref, dgn_ref, ds_ref, dla_ref):
        s_id, i = pl.program_id(0), pl.program_id(1)

        @pl.when((s_id == 0) & (i == 0))
        def _():
            dwup_ref[...] = jnp.zeros_like(dwup_ref)
            db_ref[...] = jnp.zeros_like(db_ref)
            dgn_ref[...] = jnp.zeros_like(dgn_ref)

        @pl.when(i == 0)
        def _():
            ds_ref[...] = jnp.zeros_like(ds_ref)

        a_pre = _gate_logits(alr_ref, wup_ref, b_ref)
        la = _log_sigmoid(a_pre) * (1.0 / B_GATE_TEMP)

        z = zb_ref[...].astype(_F32)
        sz = _sigmoid(z)
        d_ob = dob_ref[...].astype(_F32)
        tg = d_ob * (z * sz)
        dsilu = sz * (1.0 + z * (1.0 - sz))
        do_cols, dgn_cols = [], []
        for h in range(B_HEADS):
            vl_ = slice(h * 128, (h + 1) * 128)
            oh = oraw_ref[:, vl_].astype(_F32)
            r = lax.rsqrt(jnp.mean(oh * oh, axis=-1, keepdims=True) + NORM_EPS)
            on = oh * r
            gnh = gn_ref[:, vl_]
            dzb_ref[:, vl_] = (d_ob[:, vl_] * (on * gnh) * dsilu[:, vl_]).astype(_ST)
            dgn_cols.append(jnp.sum(tg[:, vl_] * on, axis=0, keepdims=True))
            do_cols.append(_rms_bwd(tg[:, vl_] * gnh, on, r))
        dgn_ref[...] = dgn_ref[...] + jnp.concatenate(dgn_cols, axis=1)
        d_o = jnp.concatenate(do_cols, axis=1)

        for c in reversed(range(cpb)):
            r0 = c * B_CHUNK
            tm = _gla_chunk_terms(la, qkb_ref, r0)
            dq_cols, dk_cols, dv_cols, dg_cols = [], [], [], []
            for h in range(B_HEADS):
                kl_, vl_ = slice(h * 64, (h + 1) * 64), slice(h * 128, (h + 1) * 128)
                qm, km, kl, qc = tm["qm"][:, kl_], tm["km"][:, kl_], tm["kl"][:, kl_], tm["qc"][:, kl_]
                st = sst_ref[c, kl_, :]
                dsn = ds_ref[kl_, :]
                doh = d_o[r0:r0 + B_CHUNK, vl_]
                v = vb_ref[r0:r0 + B_CHUNK, vl_]
                a = jnp.where(tm["causal"], _dot_nt(qm, km), 0.0)
                da = jnp.where(tm["causal"], _dot_nt(doh, v), 0.0)
                dqc = _dot_nt(doh, st)
                dqm = _dot(da, km)
                dkm = _dot_tn(da, qm)
                dkl = _dot_nt(v, dsn)
                dv_cols.append(_dot_tn(a, doh) + _dot(kl, dsn))
                ds_ref[kl_, :] = _dot_tn(qc, doh) + tm["dec_col"][kl_, :] * dsn
                dq_cols.append((dqm * tm["e_q"][:, kl_] + dqc * tm["e_c"][:, kl_]) * (B_KEY_DIM ** -0.5))
                dk_cols.append(dkm * tm["e_k"][:, kl_] + dkl * tm["e_l"][:, kl_])
                dcum = dqm * qm - dkm * km + dqc * qc - dkl * kl
                ddec = jnp.sum((dsn * st).T, axis=0, keepdims=True)
                dlast = jnp.sum(dkl * kl, axis=0, keepdims=True) + ddec * tm["dec_row"][:, kl_]
                dcum = jnp.where(tm["ri"] == B_CHUNK - 1, dcum + dlast, dcum)
                dg_cols.append(_dot_ones((tm["ri"] <= lax.broadcasted_iota(jnp.int32, (B_CHUNK, B_CHUNK), 1)
                                          ).astype(_F32), dcum))
            rows = slice(r0, r0 + B_CHUNK)
            dqkb_ref[rows, 0:B_KEY_WIDTH] = jnp.concatenate(dq_cols, axis=1).astype(_ST)
            dqkb_ref[rows, B_KEY_WIDTH:2 * B_KEY_WIDTH] = jnp.concatenate(dk_cols, axis=1).astype(_ST)
            dvb_ref[rows, :] = jnp.concatenate(dv_cols, axis=1).astype(_ST)
            dla_ref[rows, :] = jnp.concatenate(dg_cols, axis=1)

        da_pre = dla_ref[...] * (1.0 / B_GATE_TEMP) * (1.0 - _sigmoid(a_pre))
        dalr_ref[...] = _dot_nt(da_pre, wup_ref[...]).astype(_ST)
        dwup_ref[...] = dwup_ref[...] + _dot_tn(alr_ref[...], da_pre)
        db_ref[...] = db_ref[...] + jnp.sum(da_pre, axis=0, keepdims=True)

    blk = lambda s, i: s * nblk + nblk - 1 - i
    rows = lambda w: pl.BlockSpec((tb, w), lambda s, i: (blk(s, i), 0))
    full = lambda a, b: pl.BlockSpec((a, b), lambda s, i: (0, 0))
    act = lambda w: jax.ShapeDtypeStruct((t, w), _ST)
    return pl.pallas_call(
        body, name="gla_bwd", grid=(nseq, nblk),
        in_specs=[rows(512), rows(512), rows(512), rows(LANE), rows(512), rows(512),
                  pl.BlockSpec((cpb, B_KEY_WIDTH, B_VAL_DIM), lambda s, i: (blk(s, i), 0, 0)),
                  full(LANE, B_KEY_WIDTH), full(1, B_KEY_WIDTH), full(1, B_WIDTH)],
        out_specs=[rows(512), rows(512), rows(512), rows(LANE), full(LANE, B_KEY_WIDTH),
                   full(1, B_KEY_WIDTH), full(1, B_WIDTH)],
        out_shape=[act(512), act(512), act(512), act(LANE),
                   jax.ShapeDtypeStruct((LANE, B_KEY_WIDTH), _F32),
                   jax.ShapeDtypeStruct((1, B_KEY_WIDTH), _F32), jax.ShapeDtypeStruct((1, B_WIDTH), _F32)],
        scratch_shapes=[pltpu.VMEM((B_KEY_WIDTH, B_VAL_DIM), _F32), pltpu.VMEM((tb, B_KEY_WIDTH), _F32)],
        compiler_params=_cparams(("arbitrary", "arbitrary")),
    )(qkb, vb, zb, alr, oraw, dob, sst, wup, b_alpha, gn)


def _merge_loss(oa, ob, ga, gb, x2, tgt, wa, wb, wo, g_final):
    t = x2.shape[0]
    tm = min(t, 256)
    nt = t // tm

    def body(oa_ref, ob_ref, ga_ref, gb_ref, x_ref, t_ref, wa_ref, wb_ref, wo_ref, gf_ref,
             dh_ref, doa_ref, dob_ref, dga_ref, dgb_ref, dwa_ref, dwb_ref, dwo_ref, dgf_ref, loss_ref):
        i = pl.program_id(0)

        @pl.when(i == 0)
        def _():
            dwa_ref[...] = jnp.zeros_like(dwa_ref)
            dwb_ref[...] = jnp.zeros_like(dwb_ref)
            dwo_ref[...] = jnp.zeros_like(dwo_ref)
            dgf_ref[...] = jnp.zeros_like(dgf_ref)
            loss_ref[...] = jnp.zeros_like(loss_ref)

        oa_v, ob_v = oa_ref[...], ob_ref[...]
        ya, yb = _dot(oa_v, wa_ref[...]), _dot(ob_v, wb_ref[...])
        sga, sgb = _sigmoid(ga_ref[...].astype(_F32)), _sigmoid(gb_ref[...].astype(_F32))
        merged = (sga * ya + sgb * yb).astype(_MX)
        out = x_ref[...] + _dot(merged, wo_ref[...])
        r = lax.rsqrt(jnp.mean(out * out, axis=-1, keepdims=True) + NORM_EPS)
        nrm = out * r
        gf = gf_ref[...]
        err = nrm * gf - t_ref[...]
        loss_ref[...] = loss_ref[...] + (0.5 / D_MODEL) * jnp.sum(err * err, axis=0, keepdims=True)
        dy = err * (1.0 / D_MODEL)
        dgf_ref[...] = dgf_ref[...] + jnp.sum(dy * nrm, axis=0, keepdims=True)
        dh = _rms_bwd(dy * gf, nrm, r)
        dh_ref[...] = dh
        dh_mx = dh.astype(_MX)
        dmer = _dot_nt(dh_mx, wo_ref[...])
        dwo_ref[...] = dwo_ref[...] + _dot_tn(merged, dh_mx)
        dya, dyb = (dmer * sga).astype(_MX), (dmer * sgb).astype(_MX)
        dga_ref[...] = (dmer * ya * sga * (1.0 - sga)).astype(_ST)
        dgb_ref[...] = (dmer * yb * sgb * (1.0 - sgb)).astype(_ST)
        doa_ref[...] = _dot_nt(dya, wa_ref[...]).astype(_ST)
        dob_ref[...] = _dot_nt(dyb, wb_ref[...]).astype(_ST)
        dwa_ref[...] = dwa_ref[...] + _dot_tn(oa_v, dya)
        dwb_ref[...] = dwb_ref[...] + _dot_tn(ob_v, dyb)

    rows = lambda w: pl.BlockSpec((tm, w), lambda i: (i, 0))
    full = lambda a, b: pl.BlockSpec((a, b), lambda i: (0, 0))
    return pl.pallas_call(
        body, name="merge_loss", grid=(nt,),
        in_specs=[rows(512), rows(512), rows(D_MODEL), rows(D_MODEL), rows(D_MODEL), rows(D_MODEL),
                  full(A_WIDTH, D_MODEL), full(B_WIDTH, D_MODEL), full(D_MODEL, D_MODEL), full(1, D_MODEL)],
        out_specs=[rows(D_MODEL), rows(512), rows(512), rows(D_MODEL), rows(D_MODEL),
                   full(A_WIDTH, D_MODEL), full(B_WIDTH, D_MODEL), full(D_MODEL, D_MODEL),
                   full(1, D_MODEL), full(1, D_MODEL)],
        out_shape=[jax.ShapeDtypeStruct((t, D_MODEL), _F32), jax.ShapeDtypeStruct((t, 512), _ST),
                   jax.ShapeDtypeStruct((t, 512), _ST), jax.ShapeDtypeStruct((t, D_MODEL), _ST),
                   jax.ShapeDtypeStruct((t, D_MODEL), _ST),
                   jax.ShapeDtypeStruct((A_WIDTH, D_MODEL), _F32), jax.ShapeDtypeStruct((B_WIDTH, D_MODEL), _F32),
                   jax.ShapeDtypeStruct((D_MODEL, D_MODEL), _F32), jax.ShapeDtypeStruct((1, D_MODEL), _F32),
                   jax.ShapeDtypeStruct((1, D_MODEL), _F32)],
        compiler_params=_cparams(("arbitrary",), VMEM_LIMIT),
    )(oa, ob, ga, gb, x2, tgt, wa, wb, wo, g_final)


def _in_proj_bwd_x(dpieces, wp, x2, dh2, g_in):
    t = x2.shape[0]
    tm = min(t, 512)
    np_ = len(PIECES)

    def body(*refs):
        dp_refs = refs[:np_]
        w_ref, x_ref, dh2_ref, g_ref, gx_ref, dg_ref = refs[np_:]

        @pl.when(pl.program_id(0) == 0)
        def _():
            dg_ref[...] = jnp.zeros_like(dg_ref)

        dh = None
        for (name, a, b), dp in zip(PIECES, dp_refs):
            part = _dot_nt(dp[...], w_ref[:, a:b])
            dh = part if dh is None else dh + part
        xv = x_ref[...]
        r = lax.rsqrt(jnp.mean(xv * xv, axis=-1, keepdims=True) + NORM_EPS)
        nrm = xv * r
        dg_ref[...] = dg_ref[...] + jnp.sum(dh * nrm, axis=0, keepdims=True)
        gx_ref[...] = dh2_ref[...] + _rms_bwd(dh * g_ref[...], nrm, r)

    rows = lambda w: pl.BlockSpec((tm, w), lambda i: (i, 0))
    full = lambda a, b: pl.BlockSpec((a, b), lambda i: (0, 0))
    return pl.pallas_call(
        body, name="in_proj_bwd_x", grid=(t // tm,),
        in_specs=[rows(b - a) for _, a, b in PIECES] + [full(D_MODEL, D_PAD), rows(D_MODEL), rows(D_MODEL),
                                                          full(1, D_MODEL)],
        out_specs=[rows(D_MODEL), full(1, D_MODEL)],
        out_shape=[jax.ShapeDtypeStruct((t, D_MODEL), _F32), jax.ShapeDtypeStruct((1, D_MODEL), _F32)],
        compiler_params=_cparams(("arbitrary",), VMEM_LIMIT),
    )(*dpieces, wp, x2, dh2, g_in)


def _in_proj_bwd_w(h, dpieces, name):
    t = h.shape[0]
    tm = min(t, 512)
    np_ = len(dpieces)
    widths = [d.shape[1] for d in dpieces]

    def body(*refs):
        h_ref, dp_refs, dw_refs = refs[0], refs[1:1 + np_], refs[1 + np_:]

        @pl.when(pl.program_id(0) == 0)
        def _():
            for dw in dw_refs:
                dw[...] = jnp.zeros_like(dw)

        hv = h_ref[...]
        for dp, dw in zip(dp_refs, dw_refs):
            dw[...] = dw[...] + _dot_tn(hv, dp[...])

    rows = lambda w: pl.BlockSpec((tm, w), lambda i: (i, 0))
    return pl.pallas_call(
        body, name=name, grid=(t // tm,),
        in_specs=[rows(D_MODEL)] + [rows(w) for w in widths],
        out_specs=[pl.BlockSpec((D_MODEL, w), lambda i: (0, 0)) for w in widths],
        out_shape=[jax.ShapeDtypeStruct((D_MODEL, w), _F32) for w in widths],
        compiler_params=_cparams(("arbitrary",), VMEM_LIMIT),
    )(h, *dpieces)


def _place():
    return lax.axis_index("x"), lax.axis_index("y"), lax.axis_index("c")


def _other_chips(x, y):
    return [(1 - x, y), (x, 1 - y), (1 - x, 1 - y)]


def _gather_shards(shards):
    nw = len(shards)

    def body(*refs):
        ins, outs = refs[:nw], refs[nw:2 * nw]
        send, recv, lsem = refs[2 * nw:]
        x, y, c = _place()
        me = 2 * x + y
        copies = []
        for i in range(nw):
            mine = pltpu.make_async_copy(ins[i], outs[i].at[me], lsem.at[i])
            mine.start()
            copies.append(mine)
            for j, (px, py) in enumerate(_other_chips(x, y)):
                cp = pltpu.make_async_remote_copy(
                    src_ref=ins[i], dst_ref=outs[i].at[me], send_sem=send.at[3 * i + j],
                    recv_sem=recv.at[3 * i + j], device_id=(px, py, c), device_id_type=_MESH)
                cp.start()
                copies.append(cp)
        for cp in copies:
            cp.wait()

    return pl.pallas_call(
        body, name="gather_weights",
        in_specs=[_ANY] * nw, out_specs=[_ANY] * nw,
        out_shape=[jax.ShapeDtypeStruct((4,) + s.shape, s.dtype) for s in shards],
        scratch_shapes=[pltpu.SemaphoreType.DMA((3 * nw,)), pltpu.SemaphoreType.DMA((3 * nw,)),
                        pltpu.SemaphoreType.DMA((nw,))],
    )(*shards)


def _pair_exchange(grads):
    nw = len(grads)

    def body(*refs):
        ins, outs = refs[:nw], refs[nw:2 * nw]
        send, recv = refs[2 * nw:]
        x, y, c = _place()
        copies = []
        for i in range(nw):
            cp = pltpu.make_async_remote_copy(
                src_ref=ins[i].at[1 - c], dst_ref=outs[i], send_sem=send.at[i], recv_sem=recv.at[i],
                device_id=(x, y, 1 - c), device_id_type=_MESH)
            cp.start()
            copies.append(cp)
        for cp in copies:
            cp.wait()

    return pl.pallas_call(
        body, name="grad_pair_exchange",
        in_specs=[_ANY] * nw, out_specs=[_ANY] * nw,
        out_shape=[jax.ShapeDtypeStruct(g.shape[1:], g.dtype) for g in grads],
        scratch_shapes=[pltpu.SemaphoreType.DMA((nw,)), pltpu.SemaphoreType.DMA((nw,))],
    )(*grads)


def _add_own_half(g, r, c_arr, name):
    _, _, h, n = g.shape
    bh = min(h, 256)

    def body(c_ref, g_ref, r_ref, o_ref):
        o_ref[...] = g_ref[...] + r_ref[...]

    return pl.pallas_call(
        body, name=name,
        grid_spec=pltpu.PrefetchScalarGridSpec(
            num_scalar_prefetch=1, grid=(4, h // bh),
            in_specs=[pl.BlockSpec((None, None, bh, n), lambda k, i, c_ref: (c_ref[0], k, i, 0)),
                      pl.BlockSpec((None, bh, n), lambda k, i, c_ref: (k, i, 0))],
            out_specs=pl.BlockSpec((None, bh, n), lambda k, i, c_ref: (k, i, 0))),
        out_shape=jax.ShapeDtypeStruct(r.shape, r.dtype),
        compiler_params=_cparams(("parallel", "parallel")),
    )(c_arr, g, r)


def _chip_exchange(parts):
    nw = len(parts)

    def body(*refs):
        ins, outs = refs[:nw], refs[nw:2 * nw]
        send, recv, lsem = refs[2 * nw:]
        x, y, c = _place()
        me = 2 * x + y
        copies = []
        for i in range(nw):
            mine = pltpu.make_async_copy(ins[i].at[me], outs[i].at[me], lsem.at[i])
            mine.start()
            copies.append(mine)
            for j, (px, py) in enumerate(_other_chips(x, y)):
                cp = pltpu.make_async_remote_copy(
                    src_ref=ins[i].at[2 * px + py], dst_ref=outs[i].at[me], send_sem=send.at[3 * i + j],
                    recv_sem=recv.at[3 * i + j], device_id=(px, py, c), device_id_type=_MESH)
                cp.start()
                copies.append(cp)
        for cp in copies:
            cp.wait()

    return pl.pallas_call(
        body, name="grad_chip_exchange",
        in_specs=[_ANY] * nw, out_specs=[_ANY] * nw,
        out_shape=[jax.ShapeDtypeStruct(p.shape, p.dtype) for p in parts],
        scratch_shapes=[pltpu.SemaphoreType.DMA((3 * nw,)), pltpu.SemaphoreType.DMA((3 * nw,)),
                        pltpu.SemaphoreType.DMA((nw,))],
    )(*parts)


def _sum_chips(q, name):
    _, h, n = q.shape
    bh = min(h, 256)

    def body(q_ref, o_ref):
        o_ref[...] = ((q_ref[0] + q_ref[1]) + q_ref[2]) + q_ref[3]

    return pl.pallas_call(
        body, name=name, grid=(h // bh,),
        in_specs=[pl.BlockSpec((4, bh, n), lambda i: (0, i, 0))],
        out_specs=pl.BlockSpec((bh, n), lambda i: (i, 0)),
        out_shape=jax.ShapeDtypeStruct((h, n), q.dtype),
        compiler_params=_cparams(("parallel",)),
    )(q)


def _pair_share(halves):
    nw = len(halves)

    def body(*refs):
        ins, outs = refs[:nw], refs[nw:2 * nw]
        send, recv, lsem = refs[2 * nw:]
        x, y, c = _place()
        copies = []
        for i in range(nw):
            mine = pltpu.make_async_copy(ins[i], outs[i].at[c], lsem.at[i])
            mine.start()
            copies.append(mine)
            cp = pltpu.make_async_remote_copy(
                src_ref=ins[i], dst_ref=outs[i].at[c], send_sem=send.at[i], recv_sem=recv.at[i],
                device_id=(x, y, 1 - c), device_id_type=_MESH)
            cp.start()
            copies.append(cp)
        for cp in copies:
            cp.wait()

    return pl.pallas_call(
        body, name="grad_pair_share",
        in_specs=[_ANY] * nw, out_specs=[_ANY] * nw,
        out_shape=[jax.ShapeDtypeStruct((2,) + s.shape, s.dtype) for s in halves],
        scratch_shapes=[pltpu.SemaphoreType.DMA((nw,)), pltpu.SemaphoreType.DMA((nw,)),
                        pltpu.SemaphoreType.DMA((nw,))],
    )(*halves)


def _all_sum_small(v):
    n = v.shape[1]

    def body(v_ref, tot_ref, all_ref, send, recv):
        x, y, c = _place()
        me = 4 * x + 2 * y + c
        all_ref[me] = v_ref[...]
        copies = []
        for r in range(1, 8):
            px = 1 - x if r & 4 else x
            py = 1 - y if r & 2 else y
            pc = 1 - c if r & 1 else c
            cp = pltpu.make_async_remote_copy(
                src_ref=v_ref, dst_ref=all_ref.at[me], send_sem=send.at[r - 1], recv_sem=recv.at[r - 1],
                device_id=(px, py, pc), device_id_type=_MESH)
            cp.start()
            copies.append(cp)
        for cp in copies:
            cp.wait()
        acc = all_ref[0]
        for d in range(1, 8):
            acc = acc + all_ref[d]
        tot_ref[...] = acc

    vm = pl.BlockSpec(memory_space=pltpu.VMEM)
    return pl.pallas_call(
        body, name="small_all_sum",
        in_specs=[vm], out_specs=[vm, vm],
        out_shape=[jax.ShapeDtypeStruct((8, n), _F32), jax.ShapeDtypeStruct((8, 8, n), _F32)],
        scratch_shapes=[pltpu.SemaphoreType.DMA((7,)), pltpu.SemaphoreType.DMA((7,))],
    )(v)[0]


def _adamw(w, g, m, v, name):
    r, n = w.shape
    br = r
    for cand in (256, 128):
        if r > cand and r % cand == 0:
            br = cand
            break

    def body(w_ref, g_ref, m_ref, v_ref, d_ref, nm_ref, nv_ref):
        gv = g_ref[...]
        m2 = ADAM_B1 * m_ref[...] + (1.0 - ADAM_B1) * gv
        v2 = ADAM_B2 * v_ref[...] + (1.0 - ADAM_B2) * (gv * gv)
        m_hat = m2 / (1.0 - ADAM_B1 ** ADAM_STEP)
        v_hat = v2 / (1.0 - ADAM_B2 ** ADAM_STEP)
        d_ref[...] = -ADAM_LR * (m_hat / (jnp.sqrt(v_hat) + ADAM_EPS) + ADAM_WD * w_ref[...])
        nm_ref[...] = m2
        nv_ref[...] = v2

    blk = pl.BlockSpec((br, n), lambda i: (i, 0))
    shp = jax.ShapeDtypeStruct((r, n), _F32)
    return pl.pallas_call(
        body, name=name, grid=(r // br,),
        in_specs=[blk] * 4, out_specs=[blk] * 3, out_shape=[shp] * 3,
        compiler_params=_cparams(("parallel",)),
    )(w, g, m, v)


def _pad_cols(w_full):
    r = w_full.shape[0]
    e = ALR_AT + B_GATE_RANK
    return jnp.concatenate([w_full[:, :e], jnp.zeros((r, LANE - B_GATE_RANK), w_full.dtype), w_full[:, e:]], axis=1)


def _unpad_cols(w_pad):
    e = ALR_AT + B_GATE_RANK
    return jnp.concatenate([w_pad[:, :e], w_pad[:, ALR_AT + LANE:]], axis=1)


def kernel(x, positions, g_in, w_in, w_alpha_up, b_alpha, attn_sinks, g_gla_norm, w_out_a, w_out_b, w_o, g_final, loss_target, m_g_in, m_w_in, m_w_alpha_up, m_b_alpha, m_attn_sinks, m_g_gla_norm, m_w_out_a, m_w_out_b, m_w_o, m_g_final, v_g_in, v_w_in, v_w_alpha_up, v_b_alpha, v_attn_sinks, v_g_gla_norm, v_w_out_a, v_w_out_b, v_w_o, v_g_final):
    nseq, seq, _ = x.shape
    t = nseq * seq
    cx, cy, cc = _place()
    chip = 2 * cx + cy

    shards = [w_in[0].astype(_MX), w_out_a[0].astype(_MX), w_out_b[0].astype(_MX), w_o[0].astype(_MX),
              w_alpha_up[0].astype(_MX)]
    win_g, wa_g, wb_g, wo_g, wup_g = _gather_shards(shards)
    cols = lambda g4: jnp.transpose(g4, (1, 0, 2)).reshape(g4.shape[1], 4 * g4.shape[2])
    wp = _pad_cols(cols(win_g))
    wa, wb = cols(wa_g), cols(wb_g)
    wo = wo_g.reshape(D_MODEL, D_MODEL)
    wup = jnp.concatenate([cols(wup_g), jnp.zeros((LANE - B_GATE_RANK, B_KEY_WIDTH), _MX)], axis=0)

    x2 = x.reshape(t, D_MODEL)
    tgt = loss_target.reshape(t, D_MODEL)
    pos_f = positions.astype(_F32).reshape(t, 1)
    sinks = attn_sinks.reshape(A_HEADS)
    gf = g_final.reshape(1, D_MODEL)

    cos, sa, sb = _rope_tables(pos_f)
    h, qkv, za, qkb, vb, zb, alr, ga, gb = _in_proj(x2, g_in, wp, cos, sa, sb)
    oa = _attn_fwd(qkv, za, sinks, nseq)
    ob, oraw, sst = _gla_fwd(qkb, vb, zb, alr, wup, b_alpha, g_gla_norm, nseq)

    dh2, doa, dob, dga, dgb, dwa, dwb, dwo, dgf, lossv = _merge_loss(oa, ob, ga, gb, x2, tgt, wa, wb, wo, gf)

    dqkv, dza, dsink = _attn_bwd(qkv, za, doa, sinks, cos, sa, sb, nseq)
    dqkb, dvb, dzb, dalr, dwup, dba, dgn = _gla_bwd(qkb, vb, zb, alr, oraw, dob, sst, wup, b_alpha, g_gla_norm, nseq)
    dpieces = [dqkv, dza, dqkb, dvb, dzb, dalr, dga, dgb]
    grad_x2, dgin = _in_proj_bwd_x(dpieces, wp, x2, dh2, g_in)
    dw_lo = _in_proj_bwd_w(h, dpieces[:6], "in_proj_bwd_w_mixers")
    dw_hi = _in_proj_bwd_w(h, dpieces[6:], "in_proj_bwd_w_gates")
    dwin = _unpad_cols(jnp.concatenate(list(dw_lo) + list(dw_hi), axis=1))

    def by_half_cols(gw):
        r, n4 = gw.shape
        return jnp.transpose(gw.reshape(2, r // 2, 4, n4 // 4), (0, 2, 1, 3))

    def by_half_rows(gw):
        r4, n = gw.shape
        return jnp.transpose(gw.reshape(4, 2, r4 // 8, n), (1, 0, 2, 3))

    g_halves = [by_half_cols(dwin), by_half_cols(dwa), by_half_cols(dwb), by_half_rows(dwo)]
    from_sibling = _pair_exchange(g_halves)
    c_arr = jnp.reshape(cc, (1,)).astype(jnp.int32)
    names = ("w_in", "w_out_a", "w_out_b", "w_o")
    pair_sums = [_add_own_half(g, r, c_arr, "pair_sum_" + nm) for g, r, nm in zip(g_halves, from_sibling, names)]
    from_chips = _chip_exchange(pair_sums)
    my_halves = [_sum_chips(q, "chip_sum_" + nm) for q, nm in zip(from_chips, names)]
    full = _pair_share(my_halves)
    g_win, g_wa, g_wb, g_wo = [f.reshape(2 * f.shape[1], f.shape[2]) for f in full]

    small = jnp.concatenate([
        dgin, dgf, dgn, dba,
        jnp.pad(dsink[:, 0].reshape(1, A_HEADS), ((0, 0), (0, LANE - A_HEADS))),
        jnp.pad(jnp.sum(lossv, axis=1, keepdims=True), ((0, 0), (0, LANE - 1))),
        dwup[:B_GATE_RANK].reshape(1, B_GATE_RANK * B_KEY_WIDTH)], axis=1)
    tot = _all_sum_small(jnp.pad(small, ((0, 7), (0, 0))))[0:1]
    o = 0
    def take(n):
        nonlocal o
        o += n
        return tot[:, o - n:o]
    g_gin, g_gf, g_gn, g_ba = take(D_MODEL), take(D_MODEL), take(B_WIDTH), take(B_KEY_WIDTH)
    g_sink = take(LANE)[:, :A_HEADS]
    loss = take(LANE)[0, 0]
    g_wup_full = take(B_GATE_RANK * B_KEY_WIDTH).reshape(B_GATE_RANK, B_KEY_WIDTH)
    nup = B_KEY_WIDTH // 4
    g_wup = lax.dynamic_slice(g_wup_full, (0, chip * nup), (B_GATE_RANK, nup))

    def pack(*parts):
        return jnp.concatenate([p.reshape(1, -1) for p in parts], axis=1)

    sm_w = pack(g_in, g_final, g_gla_norm, b_alpha, attn_sinks, w_alpha_up)
    sm_g = pack(g_gin, g_gf, g_gn, g_ba, g_sink, g_wup)
    sm_m = pack(m_g_in, m_g_final, m_g_gla_norm, m_b_alpha, m_attn_sinks, m_w_alpha_up)
    sm_v = pack(v_g_in, v_g_final, v_g_gla_norm, v_b_alpha, v_attn_sinks, v_w_alpha_up)
    sm_out = _adamw(sm_w, sm_g, sm_m, sm_v, "adamw_small")

    def unpack(p):
        sizes = (D_MODEL, D_MODEL, B_WIDTH, B_KEY_WIDTH, A_HEADS, B_GATE_RANK * nup)
        outs, at = [], 0
        for s in sizes:
            outs.append(p[:, at:at + s])
            at += s
        gi, gfin, gnn, ba, sk, wu = outs
        return dict(g_in=gi, g_final=gfin.reshape(D_MODEL), g_gla_norm=gnn, b_alpha=ba, attn_sinks=sk,
                    w_alpha_up=wu.reshape(1, B_GATE_RANK, nup))

    big = {}
    for nm, w, g, m, v in (("w_in", w_in, g_win, m_w_in, v_w_in), ("w_out_a", w_out_a, g_wa, m_w_out_a, v_w_out_a),
                           ("w_out_b", w_out_b, g_wb, m_w_out_b, v_w_out_b), ("w_o", w_o, g_wo, m_w_o, v_w_o)):
        d, nm_, nv_ = _adamw(w[0], g, m[0], v[0], "adamw_" + nm)
        big[nm] = (g[None], d[None], nm_[None], nv_[None])

    order = ("g_in", "w_in", "w_alpha_up", "b_alpha", "attn_sinks", "g_gla_norm", "w_out_a", "w_out_b", "w_o", "g_final")
    small_sets = [unpack(sm_g)] + [unpack(p) for p in sm_out]
    outs = []
    for kind in range(4):
        for nm in order:
            outs.append(big[nm][kind] if nm in big else small_sets[kind][nm])
    return (loss, grad_x2.reshape(x.shape), *outs)
```

```python
import math

import numpy as np
import jax
import jax.numpy as jnp
from jax import lax
from jax.experimental import pallas as pl
from jax.experimental.pallas import tpu as pltpu

D_MODEL = 1024
A_HEADS, A_KV_HEADS, A_HEAD_DIM = 8, 2, 64
A_GROUP = A_HEADS // A_KV_HEADS
A_WIDTH, A_KV_WIDTH = 512, 128
BLOCK = 128
ROPE_THETA = 500000.0
ROPE_DIM = 16
B_HEADS, B_KEY_DIM, B_VAL_DIM = 4, 64, 128
B_KEY_WIDTH, B_WIDTH = 256, 512
B_GATE_RANK = 16
B_GATE_TEMP = 16.0
B_CHUNK = 64
NORM_EPS = 1e-6
NEG_BIG = -1e30
D_IN = 4880

ADAM_LR, ADAM_B1, ADAM_B2, ADAM_EPS, ADAM_WD, ADAM_STEP = 0.001, 0.9, 0.999, 1e-08, 0.01, 10

LANE = 128
ALR_AT = 2816
D_PAD = D_IN + (LANE - B_GATE_RANK)
PIECES = (("qkv", 0, 768), ("za", 768, 1280), ("qkb", 1280, 1792), ("vb", 1792, 2304),
          ("zb", 2304, 2816), ("alr", 2816, 2944), ("ga", 2944, 3968), ("gb", 3968, 4992))

GLA_BLOCK = 256
VMEM_LIMIT = 56 * 1024 * 1024

_F32 = jnp.float32
_MX = jnp.bfloat16
_ST = jnp.bfloat16

_MESH = pl.DeviceIdType.MESH
_ANY = pl.BlockSpec(memory_space=pl.ANY)


def _cparams(sem=None, vmem=None):
    return pltpu.CompilerParams(dimension_semantics=sem, vmem_limit_bytes=vmem)


def _dot(a, b):
    return jnp.dot(a.astype(_MX), b.astype(_MX), preferred_element_type=_F32)


def _dot_nt(a, b):
    return lax.dot_general(a.astype(_MX), b.astype(_MX), (((1,), (1,)), ((), ())),
                           preferred_element_type=_F32)


def _dot_tn(a, b):
    return lax.dot_general(a.astype(_MX), b.astype(_MX), (((0,), (0,)), ((), ())),
                           preferred_element_type=_F32)


def _dot_ones(ones_mat, v):
    o = ones_mat.astype(jnp.bfloat16)
    v0 = v.astype(jnp.bfloat16)
    r1 = v - v0.astype(_F32)
    v1 = r1.astype(jnp.bfloat16)
    v2 = (r1 - v1.astype(_F32)).astype(jnp.bfloat16)
    d = lambda t: jnp.dot(o, t, preferred_element_type=_F32)
    return d(v0) + d(v1) + d(v2)


def _sigmoid(x):
    return 1.0 / (1.0 + jnp.exp(-x))


def _log_sigmoid(x):
    return jnp.minimum(x, 0.0) - jnp.log(1.0 + jnp.exp(-jnp.abs(x)))


def _lane_tile(t, width):
    reps = width // t.shape[1]
    return t if reps == 1 else jnp.tile(t, (1, reps))


def _rope(t, cos, sa, sb, sign):
    w = t.shape[1]
    rot = pltpu.roll(t, w - 8, 1) * _lane_tile(sa, w) + pltpu.roll(t, 8, 1) * _lane_tile(sb, w)
    return t * _lane_tile(cos, w) + sign * rot


def _rms_bwd(dy_g, n, r):
    return r * (dy_g - n * jnp.mean(dy_g * n, axis=-1, keepdims=True))


def _rope_tables(pos_f):
    t = pos_f.shape[0]
    tm = min(t, 1024)
    lane = np.arange(LANE) % A_HEAD_DIM
    half = ROPE_DIM // 2
    inv = np.exp((np.float32(-math.log(ROPE_THETA)) * np.arange(half, dtype=np.float32)) * np.float32(2.0 / ROPE_DIM))
    consts = np.zeros((8, LANE), np.float32)
    consts[0] = np.where(lane < ROPE_DIM, inv[lane % half], 0.0)
    consts[1] = np.where(lane < half, -1.0, 0.0)
    consts[2] = np.where((lane >= half) & (lane < ROPE_DIM), 1.0, 0.0)

    def body(pos_ref, c_ref, cos_ref, sa_ref, sb_ref):
        ang = pos_ref[...] * c_ref[0:1, :]
        s = jnp.sin(ang)
        cos_ref[...] = jnp.cos(ang)
        sa_ref[...] = s * c_ref[1:2, :]
        sb_ref[...] = s * c_ref[2:3, :]

    tab = jax.ShapeDtypeStruct((t, LANE), _F32)
    row = pl.BlockSpec((tm, LANE), lambda i: (i, 0))
    return pl.pallas_call(
        body, name="rope_tables", grid=(t // tm,),
        in_specs=[pl.BlockSpec((tm, 1), lambda i: (i, 0)), pl.BlockSpec((8, LANE), lambda i: (0, 0))],
        out_specs=[row, row, row], out_shape=[tab, tab, tab],
        compiler_params=_cparams(("parallel",)),
    )(pos_f, jnp.asarray(consts))


def _in_proj(x2, g_in, wp, cos, sa, sb):
    t = x2.shape[0]
    tm = min(t, 512)

    def body(x_ref, g_ref, w_ref, cos_ref, sa_ref, sb_ref, h_ref, qkv_ref, za_ref, qkb_ref,
             vb_ref, zb_ref, alr_ref, ga_ref, gb_ref):
        xv = x_ref[...]
        r = lax.rsqrt(jnp.mean(xv * xv, axis=-1, keepdims=True) + NORM_EPS)
        h = (xv * r * g_ref[...]).astype(_MX)
        h_ref[...] = h.astype(_ST)
        outs = dict(za=za_ref, qkb=qkb_ref, vb=vb_ref, zb=zb_ref, alr=alr_ref, ga=ga_ref, gb=gb_ref)
        for name, a, b in PIECES:
            p = jnp.dot(h, w_ref[:, a:b], preferred_element_type=_F32)
            if name == "qkv":
                c, s1, s2 = cos_ref[...], sa_ref[...], sb_ref[...]
                qkv_ref[:, 0:512] = _rope(p[:, 0:512], c, s1, s2, 1.0).astype(_ST)
                qkv_ref[:, 512:640] = _rope(p[:, 512:640], c, s1, s2, 1.0).astype(_ST)
                qkv_ref[:, 640:768] = p[:, 640:768].astype(_ST)
            else:
                outs[name][...] = p.astype(outs[name].dtype)

    rows = lambda w: pl.BlockSpec((tm, w), lambda i: (i, 0))
    shp = lambda name, w: jax.ShapeDtypeStruct((t, w), _F32 if name == "qkb" else _ST)
    widths = [D_MODEL] + [b - a for _, a, b in PIECES]
    return pl.pallas_call(
        body, name="in_proj", grid=(t // tm,),
        in_specs=[rows(D_MODEL), pl.BlockSpec((1, D_MODEL), lambda i: (0, 0)),
                  pl.BlockSpec((D_MODEL, D_PAD), lambda i: (0, 0)), rows(LANE), rows(LANE), rows(LANE)],
        out_specs=[rows(w) for w in widths],
        out_shape=[shp(n, w) for n, w in zip(["h"] + [p[0] for p in PIECES], widths)],
        compiler_params=_cparams(("parallel",), VMEM_LIMIT),
    )(x2, g_in, wp, cos, sa, sb)


def _attn_operands(qkv_ref, kvp_ref, want_bwd):
    kf = jnp.concatenate([kvp_ref[:, 0:128], qkv_ref[:, 512:640]], axis=0).astype(_F32) * (A_HEAD_DIM ** -0.5)
    vf = jnp.concatenate([kvp_ref[:, 128:256], qkv_ref[:, 640:768]], axis=0).astype(_F32)
    lo = lax.broadcasted_iota(jnp.int32, (1, LANE), 1) < 64

    def on_lanes(a):
        sw = pltpu.roll(a, 64, 1)
        z = jnp.zeros_like(a)
        return [[jnp.where(lo, a, z).astype(_MX), jnp.where(lo, z, sw).astype(_MX)],
                [jnp.where(lo, sw, z).astype(_MX), jnp.where(lo, z, a).astype(_MX)]]

    def on_rows(a):
        at = a.T.astype(_MX)
        z = jnp.zeros((64, at.shape[1]), _MX)
        top, bot = at[0:64], at[64:128]
        return [[jnp.concatenate([top, z], axis=0), jnp.concatenate([z, top], axis=0)],
                [jnp.concatenate([bot, z], axis=0), jnp.concatenate([z, bot], axis=0)]]

    ops = dict(k_lanes=on_lanes(kf), v_rows=on_rows(vf), lo=lo)
    if want_bwd:
        ops.update(v_lanes=on_lanes(vf), k_rows=on_rows(kf))
    return ops


def _attn_valid(n):
    kj = lax.broadcasted_iota(jnp.int32, (2 * BLOCK, BLOCK), 0) - BLOCK
    qi = lax.broadcasted_iota(jnp.int32, (2 * BLOCK, BLOCK), 1)
    return (kj <= qi) & (qi - kj < BLOCK) & ((n > 0) | (kj >= 0))


def _attn_softmax_t(k_lanes, q_pair, valid, sink):
    s = jnp.where(valid, _dot_nt(k_lanes, q_pair), NEG_BIG)
    m = jnp.maximum(jnp.max(s, axis=0, keepdims=True), sink)
    e = jnp.exp(s - m)
    e_sink = jnp.exp(sink - m)
    inv = 1.0 / (jnp.sum(e, axis=0, keepdims=True) + e_sink)
    return e, e_sink, inv


def _attn_fwd(qkv, za, sinks, nseq):
    t = qkv.shape[0]
    nb = t // nseq // BLOCK

    def body(sink_ref, qkv_ref, kvp_ref, za_ref, oa_ref):
        n = pl.program_id(1)
        ops = _attn_operands(qkv_ref, kvp_ref, False)
        valid = _attn_valid(n)
        for pr in range(A_HEADS // 2):
            lanes = slice(pr * LANE, (pr + 1) * LANE)
            g = pr // (A_GROUP // 2)
            q_pair = qkv_ref[:, lanes]
            ot = None
            for half in range(2):
                e, _, inv = _attn_softmax_t(ops["k_lanes"][g][half], q_pair, valid, sink_ref[2 * pr + half])
                part = _dot(ops["v_rows"][g][half], e) * inv
                ot = part if ot is None else ot + part
            z = za_ref[:, lanes].astype(_F32)
            oa_ref[:, lanes] = (ot.T * (z * _sigmoid(z))).astype(_ST)

    cur = lambda w: pl.BlockSpec((BLOCK, w), lambda s, n: (s * nb + n, 0))
    return pl.pallas_call(
        body, name="attn_fwd", grid=(nseq, nb),
        in_specs=[pl.BlockSpec(memory_space=pltpu.SMEM), cur(768),
                  pl.BlockSpec((BLOCK, 256), lambda s, n: (s * nb + jnp.maximum(n - 1, 0), 2)), cur(512)],
        out_specs=cur(512), out_shape=jax.ShapeDtypeStruct((t, A_WIDTH), _ST),
        compiler_params=_cparams(("parallel", "arbitrary")),
    )(sinks, qkv, qkv, za)


def _attn_bwd(qkv, za, doa, sinks, cos, sa, sb, nseq):
    t = qkv.shape[0]
    nb = t // nseq // BLOCK

    def body(sink_ref, qkv_ref, kvp_ref, za_ref, doa_ref, cos_ref, sa_ref, sb_ref,
             dqkv_ref, dza_ref, dsink_ref, ck_ref, cv_ref):
        s_id, i = pl.program_id(0), pl.program_id(1)
        n = nb - 1 - i

        @pl.when((s_id == 0) & (i == 0))
        def _():
            dsink_ref[...] = jnp.zeros_like(dsink_ref)

        @pl.when(i == 0)
        def _():
            ck_ref[...] = jnp.zeros_like(ck_ref)
            cv_ref[...] = jnp.zeros_like(cv_ref)

        ops = _attn_operands(qkv_ref, kvp_ref, True)
        lo = ops["lo"]
        valid = _attn_valid(n)
        dk_acc = [None] * A_KV_HEADS
        dv_acc = [None] * A_KV_HEADS
        dq_pairs = []
        for pr in range(A_HEADS // 2):
            lanes = slice(pr * LANE, (pr + 1) * LANE)
            g = pr // (A_GROUP // 2)
            q_pair = qkv_ref[:, lanes]
            z = za_ref[:, lanes].astype(_F32)
            sz = _sigmoid(z)
            d_oa = doa_ref[:, lanes].astype(_F32)
            d_att = d_oa * (z * sz)
            zero = jnp.zeros_like(d_att)
            q_f = q_pair.astype(_F32)
            ot, dqt = None, None
            for half in range(2):
                h = 2 * pr + half
                e, e_sink, inv = _attn_softmax_t(ops["k_lanes"][g][half], q_pair, valid, sink_ref[h])
                pn = e * inv
                dpt = _dot_nt(ops["v_lanes"][g][half], d_att)
                delta = jnp.sum(pn * dpt, axis=0, keepdims=True)
                ds = (pn * (dpt - delta)).astype(_MX)
                pn = pn.astype(_MX)
                dsink_ref[h:h + 1, :] = dsink_ref[h:h + 1, :] - jnp.sum(e_sink * inv * delta)
                o_part = _dot(ops["v_rows"][g][half], pn)
                dq_part = _dot(ops["k_rows"][g][half], ds)
                ot = o_part if ot is None else ot + o_part
                dqt = dq_part if dqt is None else dqt + dq_part
                mine = lo if half == 0 else jnp.logical_not(lo)
                dk_part = _dot(ds, jnp.where(mine, q_f, zero))
                dv_part = _dot(pn, jnp.where(mine, d_att, zero))
                dk_acc[g] = dk_part if dk_acc[g] is None else dk_acc[g] + dk_part
                dv_acc[g] = dv_part if dv_acc[g] is None else dv_acc[g] + dv_part
            dza_ref[:, lanes] = (d_oa * ot.T * (sz * (1.0 + z * (1.0 - sz)))).astype(_ST)
            dq_pairs.append(dqt.T)

        def fold(acc, scale):
            both = [a + pltpu.roll(a, 64, 1) for a in acc]
            return jnp.where(lo, both[0], both[1]) * scale

        dk_full = fold(dk_acc, A_HEAD_DIM ** -0.5)
        dv_full = fold(dv_acc, 1.0)
        dk_cur = dk_full[BLOCK:] + ck_ref[...]
        dv_cur = dv_full[BLOCK:] + cv_ref[...]
        ck_ref[...] = dk_full[:BLOCK]
        cv_ref[...] = dv_full[:BLOCK]
        c, s1, s2 = cos_ref[...], sa_ref[...], sb_ref[...]
        dqkv_ref[:, 0:512] = _rope(jnp.concatenate(dq_pairs, axis=1), c, s1, s2, -1.0).astype(_ST)
        dqkv_ref[:, 512:640] = _rope(dk_cur, c, s1, s2, -1.0).astype(_ST)
        dqkv_ref[:, 640:768] = dv_cur.astype(_ST)

    cur = lambda w: pl.BlockSpec((BLOCK, w), lambda s, i: (s * nb + nb - 1 - i, 0))
    return pl.pallas_call(
        body, name="attn_bwd", grid=(nseq, nb),
        in_specs=[pl.BlockSpec(memory_space=pltpu.SMEM), cur(768),
                  pl.BlockSpec((BLOCK, 256), lambda s, i: (s * nb + jnp.maximum(nb - 2 - i, 0), 2)),
                  cur(512), cur(512), cur(LANE), cur(LANE), cur(LANE)],
        out_specs=[cur(768), cur(512), pl.BlockSpec((8, LANE), lambda s, i: (0, 0))],
        out_shape=[jax.ShapeDtypeStruct((t, 768), _ST), jax.ShapeDtypeStruct((t, 512), _ST),
                   jax.ShapeDtypeStruct((8, LANE), _F32)],
        scratch_shapes=[pltpu.VMEM((BLOCK, A_KV_WIDTH), _F32), pltpu.VMEM((BLOCK, A_KV_WIDTH), _F32)],
        compiler_params=_cparams(("arbitrary", "arbitrary")),
    )(sinks, qkv, qkv, za, doa, cos, sa, sb)


def _gla_chunk_terms(la, qkb_ref, r0):
    g = la[r0:r0 + B_CHUNK, :]
    ri = lax.broadcasted_iota(jnp.int32, (B_CHUNK, B_CHUNK), 0)
    ci = lax.broadcasted_iota(jnp.int32, (B_CHUNK, B_CHUNK), 1)
    cum = _dot_ones((ri >= ci).astype(_F32), g)
    last = cum[B_CHUNK - 1:B_CHUNK, :]
    mid = cum[B_CHUNK // 2 - 1:B_CHUNK // 2, :]
    q = qkb_ref[r0:r0 + B_CHUNK, 0:B_KEY_WIDTH].astype(_F32) * (B_KEY_DIM ** -0.5)
    k = qkb_ref[r0:r0 + B_CHUNK, B_KEY_WIDTH:2 * B_KEY_WIDTH].astype(_F32)
    e_q, e_k, e_l, e_c = jnp.exp(cum - mid), jnp.exp(mid - cum), jnp.exp(last - cum), jnp.exp(cum)
    dec_col = jnp.exp(jnp.sum(g.T, axis=1, keepdims=True))
    return dict(qm=q * e_q, km=k * e_k, kl=k * e_l, qc=q * e_c, e_q=e_q, e_k=e_k, e_l=e_l, e_c=e_c,
                dec_col=dec_col, dec_row=jnp.exp(last), causal=ri >= ci, ri=ri)


def _gate_logits(alr_ref, wup_ref, b_ref):
    return _dot(alr_ref[...], wup_ref[...]) + b_ref[...]


def _gla_fwd(qkb, vb, zb, alr, wup, b_alpha, gn, nseq):
    t = qkb.shape[0]
    tb = min(GLA_BLOCK, t // nseq)
    nblk = t // nseq // tb
    cpb = tb // B_CHUNK

    def body(qkb_ref, vb_ref, zb_ref, alr_ref, wup_ref, b_ref, gn_ref, ob_ref, oraw_ref, sst_ref, s_ref):
        @pl.when(pl.program_id(1) == 0)
        def _():
            s_ref[...] = jnp.zeros_like(s_ref)

        la = _log_sigmoid(_gate_logits(alr_ref, wup_ref, b_ref)) * (1.0 / B_GATE_TEMP)
        o_rows = []
        for c in range(cpb):
            r0 = c * B_CHUNK
            tm = _gla_chunk_terms(la, qkb_ref, r0)
            o_heads = []
            for h in range(B_HEADS):
                kl_, vl_ = slice(h * 64, (h + 1) * 64), slice(h * 128, (h + 1) * 128)
                st = s_ref[kl_, :]
                sst_ref[c, kl_, :] = st
                v = vb_ref[r0:r0 + B_CHUNK, vl_]
                a = jnp.where(tm["causal"], _dot_nt(tm["qm"][:, kl_], tm["km"][:, kl_]), 0.0)
                o_heads.append(_dot(a, v) + _dot(tm["qc"][:, kl_], st))
                s_ref[kl_, :] = tm["dec_col"][kl_, :] * st + _dot_tn(tm["kl"][:, kl_], v)
            o_rows.append(jnp.concatenate(o_heads, axis=1))
        o = jnp.concatenate(o_rows, axis=0)
        oraw_ref[...] = o
        z = zb_ref[...].astype(_F32)
        gate = z * _sigmoid(z)
        for h in range(B_HEADS):
            vl_ = slice(h * 128, (h + 1) * 128)
            oh = o[:, vl_]
            r = lax.rsqrt(jnp.mean(oh * oh, axis=-1, keepdims=True) + NORM_EPS)
            ob_ref[:, vl_] = ((oh * r) * gn_ref[:, vl_] * gate[:, vl_]).astype(_ST)

    rows = lambda w: pl.BlockSpec((tb, w), lambda s, i: (s * nblk + i, 0))
    full = lambda a, b: pl.BlockSpec((a, b), lambda s, i: (0, 0))
    return pl.pallas_call(
        body, name="gla_fwd", grid=(nseq, nblk),
        in_specs=[rows(512), rows(512), rows(512), rows(LANE), full(LANE, B_KEY_WIDTH),
                  full(1, B_KEY_WIDTH), full(1, B_WIDTH)],
        out_specs=[rows(512), rows(512),
                   pl.BlockSpec((cpb, B_KEY_WIDTH, B_VAL_DIM), lambda s, i: (s * nblk + i, 0, 0))],
        out_shape=[jax.ShapeDtypeStruct((t, B_WIDTH), _ST), jax.ShapeDtypeStruct((t, B_WIDTH), _F32),
                   jax.ShapeDtypeStruct((t // B_CHUNK, B_KEY_WIDTH, B_VAL_DIM), _F32)],
        scratch_shapes=[pltpu.VMEM((B_KEY_WIDTH, B_VAL_DIM), _F32)],
        compiler_params=_cparams(("parallel", "arbitrary")),
    )(qkb, vb, zb, alr, wup, b_alpha, gn)


def _gla_bwd(qkb, vb, zb, alr, oraw, dob, sst, wup, b_alpha, gn, nseq):
    t = qkb.shape[0]
    tb = min(GLA_BLOCK, t // nseq)
    nblk = t // nseq // tb
    cpb = tb // B_CHUNK

    def body(qkb_ref, vb_ref, zb_ref, alr_ref, oraw_ref, dob_ref, sst_ref, wup_ref, b_ref, gn_ref,
             dqkb_ref, dvb_ref, dzb_ref, dalr_ref, dwup_ref, db_ref, dgn_ref, ds_ref, dla_ref):
        s_id, i = pl.program_id(0), pl.program_id(1)

        @pl.when((s_id == 0) & (i == 0))
        def _():
            dwup_ref[...] = jnp.zeros_like(dwup_ref)
            db_ref[...] = jnp.zeros_like(db_ref)
            dgn_ref[...] = jnp.zeros_like(dgn_ref)

        @pl.when(i == 0)
        def _():
            ds_ref[...] = jnp.zeros_like(ds_ref)

        a_pre = _gate_logits(alr_ref, wup_ref, b_ref)
        la = _log_sigmoid(a_pre) * (1.0 / B_GATE_TEMP)

        z = zb_ref[...].astype(_F32)
        sz = _sigmoid(z)
        d_ob = dob_ref[...].astype(_F32)
        tg = d_ob * (z * sz)
        dsilu = sz * (1.0 + z * (1.0 - sz))
        do_cols, dgn_cols = [], []
        for h in range(B_HEADS):
            vl_ = slice(h * 128, (h + 1) * 128)
            oh = oraw_ref[:, vl_].astype(_F32)
            r = lax.rsqrt(jnp.mean(oh * oh, axis=-1, keepdims=True) + NORM_EPS)
            on = oh * r
            gnh = gn_ref[:, vl_]
            dzb_ref[:, vl_] = (d_ob[:, vl_] * (on * gnh) * dsilu[:, vl_]).astype(_ST)
            dgn_cols.append(jnp.sum(tg[:, vl_] * on, axis=0, keepdims=True))
            do_cols.append(_rms_bwd(tg[:, vl_] * gnh, on, r))
        dgn_ref[...] = dgn_ref[...] + jnp.concatenate(dgn_cols, axis=1)
        d_o = jnp.concatenate(do_cols, axis=1)

        for c in reversed(range(cpb)):
            r0 = c * B_CHUNK
            tm = _gla_chunk_terms(la, qkb_ref, r0)
            dq_cols, dk_cols, dv_cols, dg_cols = [], [], [], []
            for h in range(B_HEADS):
                kl_, vl_ = slice(h * 64, (h + 1) * 64), slice(h * 128, (h + 1) * 128)
                qm, km, kl, qc = tm["qm"][:, kl_], tm["km"][:, kl_], tm["kl"][:, kl_], tm["qc"][:, kl_]
                st = sst_ref[c, kl_, :]
                dsn = ds_ref[kl_, :]
                doh = d_o[r0:r0 + B_CHUNK, vl_]
                v = vb_ref[r0:r0 + B_CHUNK, vl_]
                a = jnp.where(tm["causal"], _dot_nt(qm, km), 0.0)
                da = jnp.where(tm["causal"], _dot_nt(doh, v), 0.0)
                dqc = _dot_nt(doh, st)
                dqm = _dot(da, km)
                dkm = _dot_tn(da, qm)
                dkl = _dot_nt(v, dsn)
                dv_cols.append(_dot_tn(a, doh) + _dot(kl, dsn))
                ds_ref[kl_, :] = _dot_tn(qc, doh) + tm["dec_col"][kl_, :] * dsn
                dq_cols.append((dqm * tm["e_q"][:, kl_] + dqc * tm["e_c"][:, kl_]) * (B_KEY_DIM ** -0.5))
                dk_cols.append(dkm * tm["e_k"][:, kl_] + dkl * tm["e_l"][:, kl_])
                dcum = dqm * qm - dkm * km + dqc * qc - dkl * kl
                ddec = jnp.sum((dsn * st).T, axis=0, keepdims=True)
                dlast = jnp.sum(dkl * kl, axis=0, keepdims=True) + ddec * tm["dec_row"][:, kl_]
                dcum = jnp.where(tm["ri"] == B_CHUNK - 1, dcum + dlast, dcum)
                dg_cols.append(_dot_ones((tm["ri"] <= lax.broadcasted_iota(jnp.int32, (B_CHUNK, B_CHUNK), 1)
                                          ).astype(_F32), dcum))
            rows = slice(r0, r0 + B_CHUNK)
            dqkb_ref[rows, 0:B_KEY_WIDTH] = jnp.concatenate(dq_cols, axis=1).astype(_ST)
            dqkb_ref[rows, B_KEY_WIDTH:2 * B_KEY_WIDTH] = jnp.concatenate(dk_cols, axis=1).astype(_ST)
            dvb_ref[rows, :] = jnp.concatenate(dv_cols, axis=1).astype(_ST)
            dla_ref[rows, :] = jnp.concatenate(dg_cols, axis=1)

        da_pre = dla_ref[...] * (1.0 / B_GATE_TEMP) * (1.0 - _sigmoid(a_pre))
        dalr_ref[...] = _dot_nt(da_pre, wup_ref[...]).astype(_ST)
        dwup_ref[...] = dwup_ref[...] + _dot_tn(alr_ref[...], da_pre)
        db_ref[...] = db_ref[...] + jnp.sum(da_pre, axis=0, keepdims=True)

    blk = lambda s, i: s * nblk + nblk - 1 - i
    rows = lambda w: pl.BlockSpec((tb, w), lambda s, i: (blk(s, i), 0))
    full = lambda a, b: pl.BlockSpec((a, b), lambda s, i: (0, 0))
    act = lambda w: jax.ShapeDtypeStruct((t, w), _ST)
    return pl.pallas_call(
        body, name="gla_bwd", grid=(nseq, nblk),
        in_specs=[rows(512), rows(512), rows(512), rows(LANE), rows(512), rows(512),
                  pl.BlockSpec((cpb, B_KEY_WIDTH, B_VAL_DIM), lambda s, i: (blk(s, i), 0, 0)),
                  full(LANE, B_KEY_WIDTH), full(1, B_KEY_WIDTH), full(1, B_WIDTH)],
        out_specs=[rows(512), rows(512), rows(512), rows(LANE), full(LANE, B_KEY_WIDTH),
                   full(1, B_KEY_WIDTH), full(1, B_WIDTH)],
        out_shape=[act(512), act(512), act(512), act(LANE),
                   jax.ShapeDtypeStruct((LANE, B_KEY_WIDTH), _F32),
                   jax.ShapeDtypeStruct((1, B_KEY_WIDTH), _F32), jax.ShapeDtypeStruct((1, B_WIDTH), _F32)],
        scratch_shapes=[pltpu.VMEM((B_KEY_WIDTH, B_VAL_DIM), _F32), pltpu.VMEM((tb, B_KEY_WIDTH), _F32)],
        compiler_params=_cparams(("arbitrary", "arbitrary")),
    )(qkb, vb, zb, alr, oraw, dob, sst, wup, b_alpha, gn)


def _merge_loss(oa, ob, ga, gb, x2, tgt, wa, wb, wo, g_final):
    t = x2.shape[0]
    tm = min(t, 256)
    nt = t // tm

    def body(oa_ref, ob_ref, ga_ref, gb_ref, x_ref, t_ref, wa_ref, wb_ref, wo_ref, gf_ref,
             dh_ref, doa_ref, dob_ref, dga_ref, dgb_ref, dwa_ref, dwb_ref, dwo_ref, dgf_ref, loss_ref):
        i = pl.program_id(0)

        @pl.when(i == 0)
        def _():
            dwa_ref[...] = jnp.zeros_like(dwa_ref)
            dwb_ref[...] = jnp.zeros_like(dwb_ref)
            dwo_ref[...] = jnp.zeros_like(dwo_ref)
            dgf_ref[...] = jnp.zeros_like(dgf_ref)
            loss_ref[...] = jnp.zeros_like(loss_ref)

        oa_v, ob_v = oa_ref[...], ob_ref[...]
        ya, yb = _dot(oa_v, wa_ref[...]), _dot(ob_v, wb_ref[...])
        sga, sgb = _sigmoid(ga_ref[...].astype(_F32)), _sigmoid(gb_ref[...].astype(_F32))
        merged = (sga * ya + sgb * yb).astype(_MX)
        out = x_ref[...] + _dot(merged, wo_ref[...])
        r = lax.rsqrt(jnp.mean(out * out, axis=-1, keepdims=True) + NORM_EPS)
        nrm = out * r
        gf = gf_ref[...]
        err = nrm * gf - t_ref[...]
        loss_ref[...] = loss_ref[...] + (0.5 / D_MODEL) * jnp.sum(err * err, axis=0, keepdims=True)
        dy = err * (1.0 / D_MODEL)
        dgf_ref[...] = dgf_ref[...] + jnp.sum(dy * nrm, axis=0, keepdims=True)
        dh = _rms_bwd(dy * gf, nrm, r)
        dh_ref[...] = dh
        dh_mx = dh.astype(_MX)
        dmer = _dot_nt(dh_mx, wo_ref[...])
        dwo_ref[...] = dwo_ref[...] + _dot_tn(merged, dh_mx)
        dya, dyb = (dmer * sga).astype(_MX), (dmer * sgb).astype(_MX)
        dga_ref[...] = (dmer * ya * sga * (1.0 - sga)).astype(_ST)
        dgb_ref[...] = (dmer * yb * sgb * (1.0 - sgb)).astype(_ST)
        doa_ref[...] = _dot_nt(dya, wa_ref[...]).astype(_ST)
        dob_ref[...] = _dot_nt(dyb, wb_ref[...]).astype(_ST)
        dwa_ref[...] = dwa_ref[...] + _dot_tn(oa_v, dya)
        dwb_ref[...] = dwb_ref[...] + _dot_tn(ob_v, dyb)

    rows = lambda w: pl.BlockSpec((tm, w), lambda i: (i, 0))
    full = lambda a, b: pl.BlockSpec((a, b), lambda i: (0, 0))
    return pl.pallas_call(
        body, name="merge_loss", grid=(nt,),
        in_specs=[rows(512), rows(512), rows(D_MODEL), rows(D_MODEL), rows(D_MODEL), rows(D_MODEL),
                  full(A_WIDTH, D_MODEL), full(B_WIDTH, D_MODEL), full(D_MODEL, D_MODEL), full(1, D_MODEL)],
        out_specs=[rows(D_MODEL), rows(512), rows(512), rows(D_MODEL), rows(D_MODEL),
                   full(A_WIDTH, D_MODEL), full(B_WIDTH, D_MODEL), full(D_MODEL, D_MODEL),
                   full(1, D_MODEL), full(1, D_MODEL)],
        out_shape=[jax.ShapeDtypeStruct((t, D_MODEL), _F32), jax.ShapeDtypeStruct((t, 512), _ST),
                   jax.ShapeDtypeStruct((t, 512), _ST), jax.ShapeDtypeStruct((t, D_MODEL), _ST),
                   jax.ShapeDtypeStruct((t, D_MODEL), _ST),
                   jax.ShapeDtypeStruct((A_WIDTH, D_MODEL), _F32), jax.ShapeDtypeStruct((B_WIDTH, D_MODEL), _F32),
                   jax.ShapeDtypeStruct((D_MODEL, D_MODEL), _F32), jax.ShapeDtypeStruct((1, D_MODEL), _F32),
                   jax.ShapeDtypeStruct((1, D_MODEL), _F32)],
        compiler_params=_cparams(("arbitrary",), VMEM_LIMIT),
    )(oa, ob, ga, gb, x2, tgt, wa, wb, wo, g_final)


def _in_proj_bwd_x(dpieces, wp, x2, dh2, g_in):
    t = x2.shape[0]
    tm = min(t, 512)
    np_ = len(PIECES)

    def body(*refs):
        dp_refs = refs[:np_]
        w_ref, x_ref, dh2_ref, g_ref, gx_ref, dg_ref = refs[np_:]

        @pl.when(pl.program_id(0) == 0)
        def _():
            dg_ref[...] = jnp.zeros_like(dg_ref)

        dh = None
        for (name, a, b), dp in zip(PIECES, dp_refs):
            part = _dot_nt(dp[...], w_ref[:, a:b])
            dh = part if dh is None else dh + part
        xv = x_ref[...]
        r = lax.rsqrt(jnp.mean(xv * xv, axis=-1, keepdims=True) + NORM_EPS)
        nrm = xv * r
        dg_ref[...] = dg_ref[...] + jnp.sum(dh * nrm, axis=0, keepdims=True)
        gx_ref[...] = dh2_ref[...] + _rms_bwd(dh * g_ref[...], nrm, r)

    rows = lambda w: pl.BlockSpec((tm, w), lambda i: (i, 0))
    full = lambda a, b: pl.BlockSpec((a, b), lambda i: (0, 0))
    return pl.pallas_call(
        body, name="in_proj_bwd_x", grid=(t // tm,),
        in_specs=[rows(b - a) for _, a, b in PIECES] + [full(D_MODEL, D_PAD), rows(D_MODEL), rows(D_MODEL),
                                                          full(1, D_MODEL)],
        out_specs=[rows(D_MODEL), full(1, D_MODEL)],
        out_shape=[jax.ShapeDtypeStruct((t, D_MODEL), _F32), jax.ShapeDtypeStruct((1, D_MODEL), _F32)],
        compiler_params=_cparams(("arbitrary",), VMEM_LIMIT),
    )(*dpieces, wp, x2, dh2, g_in)


def _in_proj_bwd_w(h, dpieces, name):
    t = h.shape[0]
    tm = min(t, 512)
    np_ = len(dpieces)
    widths = [d.shape[1] for d in dpieces]

    def body(*refs):
        h_ref, dp_refs, dw_refs = refs[0], refs[1:1 + np_], refs[1 + np_:]

        @pl.when(pl.program_id(0) == 0)
        def _():
            for dw in dw_refs:
                dw[...] = jnp.zeros_like(dw)

        hv = h_ref[...]
        for dp, dw in zip(dp_refs, dw_refs):
            dw[...] = dw[...] + _dot_tn(hv, dp[...])

    rows = lambda w: pl.BlockSpec((tm, w), lambda i: (i, 0))
    return pl.pallas_call(
        body, name=name, grid=(t // tm,),
        in_specs=[rows(D_MODEL)] + [rows(w) for w in widths],
        out_specs=[pl.BlockSpec((D_MODEL, w), lambda i: (0, 0)) for w in widths],
        out_shape=[jax.ShapeDtypeStruct((D_MODEL, w), _F32) for w in widths],
        compiler_params=_cparams(("arbitrary",), VMEM_LIMIT),
    )(h, *dpieces)


def _place():
    return lax.axis_index("x"), lax.axis_index("y"), lax.axis_index("c")


def _other_chips(x, y):
    return [(1 - x, y), (x, 1 - y), (1 - x, 1 - y)]


def _gather_shards(shards):
    nw = len(shards)

    def body(*refs):
        ins, outs = refs[:nw], refs[nw:2 * nw]
        send, recv, lsem = refs[2 * nw:]
        x, y, c = _place()
        me = 2 * x + y
        copies = []
        for i in range(nw):
            mine = pltpu.make_async_copy(ins[i], outs[i].at[me], lsem.at[i])
            mine.start()
            copies.append(mine)
            for j, (px, py) in enumerate(_other_chips(x, y)):
                cp = pltpu.make_async_remote_copy(
                    src_ref=ins[i], dst_ref=outs[i].at[me], send_sem=send.at[3 * i + j],
                    recv_sem=recv.at[3 * i + j], device_id=(px, py, c), device_id_type=_MESH)
                cp.start()
                copies.append(cp)
        for cp in copies:
            cp.wait()

    return pl.pallas_call(
        body, name="gather_weights",
        in_specs=[_ANY] * nw, out_specs=[_ANY] * nw,
        out_shape=[jax.ShapeDtypeStruct((4,) + s.shape, s.dtype) for s in shards],
        scratch_shapes=[pltpu.SemaphoreType.DMA((3 * nw,)), pltpu.SemaphoreType.DMA((3 * nw,)),
                        pltpu.SemaphoreType.DMA((nw,))],
    )(*shards)


def _pair_exchange(grads):
    nw = len(grads)

    def body(*refs):
        ins, outs = refs[:nw], refs[nw:2 * nw]
        send, recv = refs[2 * nw:]
        x, y, c = _place()
        copies = []
        for i in range(nw):
            cp = pltpu.make_async_remote_copy(
                src_ref=ins[i].at[1 - c], dst_ref=outs[i], send_sem=send.at[i], recv_sem=recv.at[i],
                device_id=(x, y, 1 - c), device_id_type=_MESH)
            cp.start()
            copies.append(cp)
        for cp in copies:
            cp.wait()

    return pl.pallas_call(
        body, name="grad_pair_exchange",
        in_specs=[_ANY] * nw, out_specs=[_ANY] * nw,
        out_shape=[jax.ShapeDtypeStruct(g.shape[1:], g.dtype) for g in grads],
        scratch_shapes=[pltpu.SemaphoreType.DMA((nw,)), pltpu.SemaphoreType.DMA((nw,))],
    )(*grads)


def _add_own_half(g, r, c_arr, name):
    _, _, h, n = g.shape
    bh = min(h, 256)

    def body(c_ref, g_ref, r_ref, o_ref):
        o_ref[...] = g_ref[...] + r_ref[...]

    return pl.pallas_call(
        body, name=name,
        grid_spec=pltpu.PrefetchScalarGridSpec(
            num_scalar_prefetch=1, grid=(4, h // bh),
            in_specs=[pl.BlockSpec((None, None, bh, n), lambda k, i, c_ref: (c_ref[0], k, i, 0)),
                      pl.BlockSpec((None, bh, n), lambda k, i, c_ref: (k, i, 0))],
            out_specs=pl.BlockSpec((None, bh, n), lambda k, i, c_ref: (k, i, 0))),
        out_shape=jax.ShapeDtypeStruct(r.shape, r.dtype),
        compiler_params=_cparams(("parallel", "parallel")),
    )(c_arr, g, r)


def _chip_exchange(parts):
    nw = len(parts)

    def body(*refs):
        ins, outs = refs[:nw], refs[nw:2 * nw]
        send, recv, lsem = refs[2 * nw:]
        x, y, c = _place()
        me = 2 * x + y
        copies = []
        for i in range(nw):
            mine = pltpu.make_async_copy(ins[i].at[me], outs[i].at[me], lsem.at[i])
            mine.start()
            copies.append(mine)
            for j, (px, py) in enumerate(_other_chips(x, y)):
                cp = pltpu.make_async_remote_copy(
                    src_ref=ins[i].at[2 * px + py], dst_ref=outs[i].at[me], send_sem=send.at[3 * i + j],
                    recv_sem=recv.at[3 * i + j], device_id=(px, py, c), device_id_type=_MESH)
                cp.start()
                copies.append(cp)
        for cp in copies:
            cp.wait()

    return pl.pallas_call(
        body, name="grad_chip_exchange",
        in_specs=[_ANY] * nw, out_specs=[_ANY] * nw,
        out_shape=[jax.ShapeDtypeStruct(p.shape, p.dtype) for p in parts],
        scratch_shapes=[pltpu.SemaphoreType.DMA((3 * nw,)), pltpu.SemaphoreType.DMA((3 * nw,)),
                        pltpu.SemaphoreType.DMA((nw,))],
    )(*parts)


def _sum_chips(q, c_arr, name):
    _, h, n = q.shape
    bh = min(h, 256)

    def body(c_ref, q_ref, o_ref):
        o_ref[...] = ((q_ref[0] + q_ref[1]) + q_ref[2]) + q_ref[3]

    return pl.pallas_call(
        body, name=name,
        grid_spec=pltpu.PrefetchScalarGridSpec(
            num_scalar_prefetch=1, grid=(h // bh,),
            in_specs=[pl.BlockSpec((4, bh, n), lambda i, c_ref: (0, i, 0))],
            out_specs=pl.BlockSpec((None, bh, n), lambda i, c_ref: (c_ref[0], i, 0))),
        out_shape=jax.ShapeDtypeStruct((2, h, n), q.dtype),
        compiler_params=_cparams(("parallel",)),
    )(c_arr, q)


def _pair_share(bufs):
    nw = len(bufs)

    def body(*refs):
        ins, outs = refs[:nw], refs[nw:2 * nw]
        send, recv = refs[2 * nw:]
        x, y, c = _place()
        copies = []
        for i in range(nw):
            cp = pltpu.make_async_remote_copy(
                src_ref=ins[i].at[c], dst_ref=outs[i].at[c], send_sem=send.at[i], recv_sem=recv.at[i],
                device_id=(x, y, 1 - c), device_id_type=_MESH)
            cp.start()
            copies.append(cp)
        for cp in copies:
            cp.wait()

    return pl.pallas_call(
        body, name="grad_pair_share",
        in_specs=[_ANY] * nw, out_specs=[_ANY] * nw,
        out_shape=[jax.ShapeDtypeStruct(b.shape, b.dtype) for b in bufs],
        input_output_aliases={i: i for i in range(nw)},
        scratch_shapes=[pltpu.SemaphoreType.DMA((nw,)), pltpu.SemaphoreType.DMA((nw,))],
    )(*bufs)


def _all_sum_small(v):
    n = v.shape[1]

    def body(v_ref, tot_ref, all_ref, send, recv):
        x, y, c = _place()
        me = 4 * x + 2 * y + c
        all_ref[me] = v_ref[...]
        copies = []
        for r in range(1, 8):
            px = 1 - x if r & 4 else x
            py = 1 - y if r & 2 else y
            pc = 1 - c if r & 1 else c
            cp = pltpu.make_async_remote_copy(
                src_ref=v_ref, dst_ref=all_ref.at[me], send_sem=send.at[r - 1], recv_sem=recv.at[r - 1],
                device_id=(px, py, pc), device_id_type=_MESH)
            cp.start()
            copies.append(cp)
        for cp in copies:
            cp.wait()
        acc = all_ref[0]
        for d in range(1, 8):
            acc = acc + all_ref[d]
        tot_ref[...] = acc

    vm = pl.BlockSpec(memory_space=pltpu.VMEM)
    return pl.pallas_call(
        body, name="small_all_sum",
        in_specs=[vm], out_specs=[vm, vm],
        out_shape=[jax.ShapeDtypeStruct((8, n), _F32), jax.ShapeDtypeStruct((8, 8, n), _F32)],
        scratch_shapes=[pltpu.SemaphoreType.DMA((7,)), pltpu.SemaphoreType.DMA((7,))],
    )(v)[0]


def _adamw(w, g, m, v, name):
    _, r, n = w.shape
    br = r
    for cand in (256, 128):
        if r > cand and r % cand == 0:
            br = cand
            break

    def body(w_ref, g_ref, m_ref, v_ref, d_ref, nm_ref, nv_ref):
        gv = g_ref[...]
        m2 = ADAM_B1 * m_ref[...] + (1.0 - ADAM_B1) * gv
        v2 = ADAM_B2 * v_ref[...] + (1.0 - ADAM_B2) * (gv * gv)
        m_hat = m2 / (1.0 - ADAM_B1 ** ADAM_STEP)
        v_hat = v2 / (1.0 - ADAM_B2 ** ADAM_STEP)
        d_ref[...] = -ADAM_LR * (m_hat / (jnp.sqrt(v_hat) + ADAM_EPS) + ADAM_WD * w_ref[...])
        nm_ref[...] = m2
        nv_ref[...] = v2

    blk = pl.BlockSpec((None, br, n), lambda i: (0, i, 0))
    shp = jax.ShapeDtypeStruct((1, r, n), _F32)
    return pl.pallas_call(
        body, name=name, grid=(r // br,),
        in_specs=[blk] * 4, out_specs=[blk] * 3, out_shape=[shp] * 3,
        compiler_params=_cparams(("parallel",)),
    )(w, g, m, v)


def _pad_cols(w_full):
    r = w_full.shape[0]
    e = ALR_AT + B_GATE_RANK
    return jnp.concatenate([w_full[:, :e], jnp.zeros((r, LANE - B_GATE_RANK), w_full.dtype), w_full[:, e:]], axis=1)


def _unpad_cols(w_pad):
    e = ALR_AT + B_GATE_RANK
    return jnp.concatenate([w_pad[:, :e], w_pad[:, ALR_AT + LANE:]], axis=1)


def kernel(x, positions, g_in, w_in, w_alpha_up, b_alpha, attn_sinks, g_gla_norm, w_out_a, w_out_b, w_o, g_final, loss_target, m_g_in, m_w_in, m_w_alpha_up, m_b_alpha, m_attn_sinks, m_g_gla_norm, m_w_out_a, m_w_out_b, m_w_o, m_g_final, v_g_in, v_w_in, v_w_alpha_up, v_b_alpha, v_attn_sinks, v_g_gla_norm, v_w_out_a, v_w_out_b, v_w_o, v_g_final):
    nseq, seq, _ = x.shape
    t = nseq * seq
    cx, cy, cc = _place()
    chip = 2 * cx + cy

    shards = [w_in[0].astype(_MX), w_out_a[0].astype(_MX), w_out_b[0].astype(_MX), w_o[0].astype(_MX),
              w_alpha_up[0].astype(_MX)]
    win_g, wa_g, wb_g, wo_g, wup_g = _gather_shards(shards)
    cols = lambda g4: jnp.transpose(g4, (1, 0, 2)).reshape(g4.shape[1], 4 * g4.shape[2])
    wp = _pad_cols(cols(win_g))
    wa, wb = cols(wa_g), cols(wb_g)
    wo = wo_g.reshape(D_MODEL, D_MODEL)
    wup = jnp.concatenate([cols(wup_g), jnp.zeros((LANE - B_GATE_RANK, B_KEY_WIDTH), _MX)], axis=0)

    x2 = x.reshape(t, D_MODEL)
    tgt = loss_target.reshape(t, D_MODEL)
    pos_f = positions.astype(_F32).reshape(t, 1)
    sinks = attn_sinks.reshape(A_HEADS)
    gf = g_final.reshape(1, D_MODEL)

    cos, sa, sb = _rope_tables(pos_f)
    h, qkv, za, qkb, vb, zb, alr, ga, gb = _in_proj(x2, g_in, wp, cos, sa, sb)
    oa = _attn_fwd(qkv, za, sinks, nseq)
    ob, oraw, sst = _gla_fwd(qkb, vb, zb, alr, wup, b_alpha, g_gla_norm, nseq)

    dh2, doa, dob, dga, dgb, dwa, dwb, dwo, dgf, lossv = _merge_loss(oa, ob, ga, gb, x2, tgt, wa, wb, wo, gf)

    dqkv, dza, dsink = _attn_bwd(qkv, za, doa, sinks, cos, sa, sb, nseq)
    dqkb, dvb, dzb, dalr, dwup, dba, dgn = _gla_bwd(qkb, vb, zb, alr, oraw, dob, sst, wup, b_alpha, g_gla_norm, nseq)
    dpieces = [dqkv, dza, dqkb, dvb, dzb, dalr, dga, dgb]
    grad_x2, dgin = _in_proj_bwd_x(dpieces, wp, x2, dh2, g_in)
    dw_lo = _in_proj_bwd_w(h, dpieces[:6], "in_proj_bwd_w_mixers")
    dw_hi = _in_proj_bwd_w(h, dpieces[6:], "in_proj_bwd_w_gates")
    dwin = _unpad_cols(jnp.concatenate(list(dw_lo) + list(dw_hi), axis=1))

    def by_half_cols(gw):
        r, n4 = gw.shape
        return jnp.transpose(gw.reshape(2, r // 2, 4, n4 // 4), (0, 2, 1, 3))

    def by_half_rows(gw):
        r4, n = gw.shape
        return jnp.transpose(gw.reshape(4, 2, r4 // 8, n), (1, 0, 2, 3))

    g_halves = [by_half_cols(dwin), by_half_cols(dwa), by_half_cols(dwb), by_half_rows(dwo)]
    from_sibling = _pair_exchange(g_halves)
    c_arr = jnp.reshape(cc, (1,)).astype(jnp.int32)
    names = ("w_in", "w_out_a", "w_out_b", "w_o")
    pair_sums = [_add_own_half(g, r, c_arr, "pair_sum_" + nm) for g, r, nm in zip(g_halves, from_sibling, names)]
    from_chips = _chip_exchange(pair_sums)
    my_halves = [_sum_chips(q, c_arr, "chip_sum_" + nm) for q, nm in zip(from_chips, names)]
    full = _pair_share(my_halves)
    g_win, g_wa, g_wb, g_wo = [f.reshape(1, 2 * f.shape[1], f.shape[2]) for f in full]

    small = jnp.concatenate([
        dgin, dgf, dgn, dba,
        jnp.pad(dsink[:, 0].reshape(1, A_HEADS), ((0, 0), (0, LANE - A_HEADS))),
        jnp.pad(jnp.sum(lossv, axis=1, keepdims=True), ((0, 0), (0, LANE - 1))),
        dwup[:B_GATE_RANK].reshape(1, B_GATE_RANK * B_KEY_WIDTH)], axis=1)
    tot = _all_sum_small(jnp.pad(small, ((0, 7), (0, 0))))[0:1]
    o = 0
    def take(n):
        nonlocal o
        o += n
        return tot[:, o - n:o]
    g_gin, g_gf, g_gn, g_ba = take(D_MODEL), take(D_MODEL), take(B_WIDTH), take(B_KEY_WIDTH)
    g_sink = take(LANE)[:, :A_HEADS]
    loss = take(LANE)[0, 0]
    g_wup_full = take(B_GATE_RANK * B_KEY_WIDTH).reshape(B_GATE_RANK, B_KEY_WIDTH)
    nup = B_KEY_WIDTH // 4
    g_wup = lax.dynamic_slice(g_wup_full, (0, chip * nup), (B_GATE_RANK, nup))

    def pack(*parts):
        return jnp.concatenate([p.reshape(1, -1) for p in parts], axis=1)

    sm_w = pack(g_in, g_final, g_gla_norm, b_alpha, attn_sinks, w_alpha_up)
    sm_g = pack(g_gin, g_gf, g_gn, g_ba, g_sink, g_wup)
    sm_m = pack(m_g_in, m_g_final, m_g_gla_norm, m_b_alpha, m_attn_sinks, m_w_alpha_up)
    sm_v = pack(v_g_in, v_g_final, v_g_gla_norm, v_b_alpha, v_attn_sinks, v_w_alpha_up)
    sm_out = [p[0] for p in _adamw(sm_w[None], sm_g[None], sm_m[None], sm_v[None], "adamw_small")]

    def unpack(p):
        sizes = (D_MODEL, D_MODEL, B_WIDTH, B_KEY_WIDTH, A_HEADS, B_GATE_RANK * nup)
        outs, at = [], 0
        for s in sizes:
            outs.append(p[:, at:at + s])
            at += s
        gi, gfin, gnn, ba, sk, wu = outs
        return dict(g_in=gi, g_final=gfin.reshape(D_MODEL), g_gla_norm=gnn, b_alpha=ba, attn_sinks=sk,
                    w_alpha_up=wu.reshape(1, B_GATE_RANK, nup))

    big = {}
    for nm, w, g, m, v in (("w_in", w_in, g_win, m_w_in, v_w_in), ("w_out_a", w_out_a, g_wa, m_w_out_a, v_w_out_a),
                           ("w_out_b", w_out_b, g_wb, m_w_out_b, v_w_out_b), ("w_o", w_o, g_wo, m_w_o, v_w_o)):
        big[nm] = (g,) + tuple(_adamw(w, g, m, v, "adamw_" + nm))

    order = ("g_in", "w_in", "w_alpha_up", "b_alpha", "attn_sinks", "g_gla_norm", "w_out_a", "w_out_b", "w_o", "g_final")
    small_sets = [unpack(sm_g)] + [unpack(p) for p in sm_out]
    outs = []
    for kind in range(4):
        for nm in order:
            outs.append(big[nm][kind] if nm in big else small_sets[kind][nm])
    return (loss, grad_x2.reshape(x.shape), *outs)
```

```python
import math
from typing import NamedTuple

import numpy as np
import jax
import jax.numpy as jnp
from jax import lax
from jax.experimental import pallas as pl
from jax.experimental.pallas import tpu as pltpu

D_MODEL = 1024
A_HEADS, A_KV_HEADS, A_HEAD_DIM = 8, 2, 64
A_GROUP = A_HEADS // A_KV_HEADS
A_WIDTH, A_KV_WIDTH = 512, 128
BLOCK = 128
ROPE_THETA = 500000.0
ROPE_DIM = 16
B_HEADS, B_KEY_DIM, B_VAL_DIM = 4, 64, 128
B_KEY_WIDTH, B_WIDTH = 256, 512
B_GATE_RANK = 16
B_GATE_TEMP = 16.0
B_CHUNK = 64
NORM_EPS = 1e-6
NEG_BIG = -1e30
D_IN = 4880

ADAM_LR, ADAM_B1, ADAM_B2, ADAM_EPS, ADAM_WD, ADAM_STEP = 0.001, 0.9, 0.999, 1e-08, 0.01, 10

LANE = 128
ALR_AT = 2816
PIECES = (("qkv", 0, 768), ("za", 768, 1280), ("qkb", 1280, 1792), ("vb", 1792, 2304),
          ("zb", 2304, 2816), ("alr", ALR_AT, ALR_AT + LANE), ("ga", 2832, 3856), ("gb", 3856, 4880))
SHARD = D_IN // 4
WINDOW_STEP = 1216
WINDOW_ROWS = 1232

GLA_BLOCK = 256
VMEM_LIMIT = 56 * 1024 * 1024

_F32 = jnp.float32
_MX = jnp.bfloat16
_ST = jnp.bfloat16

_MESH = pl.DeviceIdType.MESH
_ANY = pl.BlockSpec(memory_space=pl.ANY)


def _cparams(sem=None, vmem=None):
    return pltpu.CompilerParams(dimension_semantics=sem, vmem_limit_bytes=vmem)


def _dot(a, b):
    return jnp.dot(a.astype(_MX), b.astype(_MX), preferred_element_type=_F32)


def _dot_nt(a, b):
    return lax.dot_general(a.astype(_MX), b.astype(_MX), (((1,), (1,)), ((), ())),
                           preferred_element_type=_F32)


def _dot_tn(a, b):
    return lax.dot_general(a.astype(_MX), b.astype(_MX), (((0,), (0,)), ((), ())),
                           preferred_element_type=_F32)


def _dot_ones(ones_mat, v):
    o = ones_mat.astype(jnp.bfloat16)
    v0 = v.astype(jnp.bfloat16)
    r1 = v - v0.astype(_F32)
    v1 = r1.astype(jnp.bfloat16)
    v2 = (r1 - v1.astype(_F32)).astype(jnp.bfloat16)
    d = lambda t: jnp.dot(o, t, preferred_element_type=_F32)
    return d(v0) + d(v1) + d(v2)


def _sigmoid(x):
    return 1.0 / (1.0 + jnp.exp(-x))


def _log_sigmoid(x):
    return jnp.minimum(x, 0.0) - jnp.log(1.0 + jnp.exp(-jnp.abs(x)))


def _lane_tile(t, width):
    reps = width // t.shape[1]
    return t if reps == 1 else jnp.tile(t, (1, reps))


def _rope(t, cos, sa, sb, sign):
    w = t.shape[1]
    rot = pltpu.roll(t, w - 8, 1) * _lane_tile(sa, w) + pltpu.roll(t, 8, 1) * _lane_tile(sb, w)
    return t * _lane_tile(cos, w) + sign * rot


def _rms_bwd(dy_g, n, r):
    return r * (dy_g - n * jnp.mean(dy_g * n, axis=-1, keepdims=True))


def _rope_tables(pos_f):
    t = pos_f.shape[0]
    tm = min(t, 1024)
    lane = np.arange(LANE) % A_HEAD_DIM
    half = ROPE_DIM // 2
    inv = np.exp((np.float32(-math.log(ROPE_THETA)) * np.arange(half, dtype=np.float32)) * np.float32(2.0 / ROPE_DIM))
    consts = np.zeros((8, LANE), np.float32)
    consts[0] = np.where(lane < ROPE_DIM, inv[lane % half], 0.0)
    consts[1] = np.where(lane < half, -1.0, 0.0)
    consts[2] = np.where((lane >= half) & (lane < ROPE_DIM), 1.0, 0.0)

    def body(pos_ref, c_ref, cos_ref, sa_ref, sb_ref):
        ang = pos_ref[...] * c_ref[0:1, :]
        s = jnp.sin(ang)
        cos_ref[...] = jnp.cos(ang)
        sa_ref[...] = s * c_ref[1:2, :]
        sb_ref[...] = s * c_ref[2:3, :]

    tab = jax.ShapeDtypeStruct((t, LANE), _F32)
    row = pl.BlockSpec((tm, LANE), lambda i: (i, 0))
    return pl.pallas_call(
        body, name="rope_tables", grid=(t // tm,),
        in_specs=[pl.BlockSpec((tm, 1), lambda i: (i, 0)), pl.BlockSpec((8, LANE), lambda i: (0, 0))],
        out_specs=[row, row, row], out_shape=[tab, tab, tab],
        compiler_params=_cparams(("parallel",)),
    )(pos_f, jnp.asarray(consts))


def _in_proj(x2, g_in, wt, cos, sa, sb):
    t = x2.shape[0]
    tm = min(t, 512)

    def body(x_ref, g_ref, w_ref, cos_ref, sa_ref, sb_ref, h_ref, qkv_ref, za_ref, qkb_ref,
             vb_ref, zb_ref, alr_ref, ga_ref, gb_ref):
        xv = x_ref[...]
        r = lax.rsqrt(jnp.mean(xv * xv, axis=-1, keepdims=True) + NORM_EPS)
        h = (xv * r * g_ref[...]).astype(_MX)
        h_ref[...] = h.astype(_ST)
        outs = dict(za=za_ref, qkb=qkb_ref, vb=vb_ref, zb=zb_ref, alr=alr_ref, ga=ga_ref, gb=gb_ref)
        for name, a, b in PIECES:
            p = _dot_nt(h, w_ref[a:b, :])
            if name == "qkv":
                c, s1, s2 = cos_ref[...], sa_ref[...], sb_ref[...]
                qkv_ref[:, 0:512] = _rope(p[:, 0:512], c, s1, s2, 1.0).astype(_ST)
                qkv_ref[:, 512:640] = _rope(p[:, 512:640], c, s1, s2, 1.0).astype(_ST)
                qkv_ref[:, 640:768] = p[:, 640:768].astype(_ST)
            else:
                outs[name][...] = p.astype(outs[name].dtype)

    rows = lambda w: pl.BlockSpec((tm, w), lambda i: (i, 0))
    shp = lambda name, w: jax.ShapeDtypeStruct((t, w), _F32 if name == "qkb" else _ST)
    widths = [D_MODEL] + [b - a for _, a, b in PIECES]
    return pl.pallas_call(
        body, name="in_proj", grid=(t // tm,),
        in_specs=[rows(D_MODEL), pl.BlockSpec((1, D_MODEL), lambda i: (0, 0)),
                  pl.BlockSpec((D_IN, D_MODEL), lambda i: (0, 0)), rows(LANE), rows(LANE), rows(LANE)],
        out_specs=[rows(w) for w in widths],
        out_shape=[shp(n, w) for n, w in zip(["h"] + [p[0] for p in PIECES], widths)],
        compiler_params=_cparams(("parallel",), VMEM_LIMIT),
    )(x2, g_in, wt, cos, sa, sb)


def _attn_operands(qkv_ref, kvp_ref, want_bwd):
    kf = jnp.concatenate([kvp_ref[:, 0:128], qkv_ref[:, 512:640]], axis=0).astype(_F32) * (A_HEAD_DIM ** -0.5)
    vf = jnp.concatenate([kvp_ref[:, 128:256], qkv_ref[:, 640:768]], axis=0).astype(_F32)
    lo = lax.broadcasted_iota(jnp.int32, (1, LANE), 1) < 64

    def on_lanes(a):
        sw = pltpu.roll(a, 64, 1)
        z = jnp.zeros_like(a)
        return [[jnp.where(lo, a, z).astype(_MX), jnp.where(lo, z, sw).astype(_MX)],
                [jnp.where(lo, sw, z).astype(_MX), jnp.where(lo, z, a).astype(_MX)]]

    def on_rows(a):
        at = a.T.astype(_MX)
        z = jnp.zeros((64, at.shape[1]), _MX)
        top, bot = at[0:64], at[64:128]
        return [[jnp.concatenate([top, z], axis=0), jnp.concatenate([z, top], axis=0)],
                [jnp.concatenate([bot, z], axis=0), jnp.concatenate([z, bot], axis=0)]]

    ops = dict(k_lanes=on_lanes(kf), v_rows=on_rows(vf), lo=lo)
    if want_bwd:
        ops.update(v_lanes=on_lanes(vf), k_rows=on_rows(kf))
    return ops


def _attn_valid(n):
    kj = lax.broadcasted_iota(jnp.int32, (2 * BLOCK, BLOCK), 0) - BLOCK
    qi = lax.broadcasted_iota(jnp.int32, (2 * BLOCK, BLOCK), 1)
    return (kj <= qi) & (qi - kj < BLOCK) & ((n > 0) | (kj >= 0))


def _attn_softmax_t(k_lanes, q_pair, valid, sink):
    s = jnp.where(valid, _dot_nt(k_lanes, q_pair), NEG_BIG)
    m = jnp.maximum(jnp.max(s, axis=0, keepdims=True), sink)
    e = jnp.exp(s - m)
    e_sink = jnp.exp(sink - m)
    inv = 1.0 / (jnp.sum(e, axis=0, keepdims=True) + e_sink)
    return e, e_sink, inv


def _attn_fwd(qkv, za, sinks, nseq):
    t = qkv.shape[0]
    nb = t // nseq // BLOCK

    def body(sink_ref, qkv_ref, kvp_ref, za_ref, oa_ref):
        n = pl.program_id(1)
        ops = _attn_operands(qkv_ref, kvp_ref, False)
        valid = _attn_valid(n)
        for pr in range(A_HEADS // 2):
            lanes = slice(pr * LANE, (pr + 1) * LANE)
            g = pr // (A_GROUP // 2)
            q_pair = qkv_ref[:, lanes]
            ot = None
            for half in range(2):
                e, _, inv = _attn_softmax_t(ops["k_lanes"][g][half], q_pair, valid, sink_ref[2 * pr + half])
                part = _dot(ops["v_rows"][g][half], e) * inv
                ot = part if ot is None else ot + part
            z = za_ref[:, lanes].astype(_F32)
            oa_ref[:, lanes] = (ot.T * (z * _sigmoid(z))).astype(_ST)

    cur = lambda w: pl.BlockSpec((BLOCK, w), lambda s, n: (s * nb + n, 0))
    return pl.pallas_call(
        body, name="attn_fwd", grid=(nseq, nb),
        in_specs=[pl.BlockSpec(memory_space=pltpu.SMEM), cur(768),
                  pl.BlockSpec((BLOCK, 256), lambda s, n: (s * nb + jnp.maximum(n - 1, 0), 2)), cur(512)],
        out_specs=cur(512), out_shape=jax.ShapeDtypeStruct((t, A_WIDTH), _ST),
        compiler_params=_cparams(("parallel", "arbitrary")),
    )(sinks, qkv, qkv, za)


def _attn_bwd(qkv, za, doa, sinks, cos, sa, sb, nseq):
    t = qkv.shape[0]
    nb = t // nseq // BLOCK

    def body(sink_ref, qkv_ref, kvp_ref, za_ref, doa_ref, cos_ref, sa_ref, sb_ref,
             dqkv_ref, dza_ref, dsink_ref, ck_ref, cv_ref):
        s_id, i = pl.program_id(0), pl.program_id(1)
        n = nb - 1 - i

        @pl.when((s_id == 0) & (i == 0))
        def _():
            dsink_ref[...] = jnp.zeros_like(dsink_ref)

        @pl.when(i == 0)
        def _():
            ck_ref[...] = jnp.zeros_like(ck_ref)
            cv_ref[...] = jnp.zeros_like(cv_ref)

        ops = _attn_operands(qkv_ref, kvp_ref, True)
        lo = ops["lo"]
        valid = _attn_valid(n)
        dk_acc = [None] * A_KV_HEADS
        dv_acc = [None] * A_KV_HEADS
        dq_pairs = []
        for pr in range(A_HEADS // 2):
            lanes = slice(pr * LANE, (pr + 1) * LANE)
            g = pr // (A_GROUP // 2)
            q_pair = qkv_ref[:, lanes]
            z = za_ref[:, lanes].astype(_F32)
            sz = _sigmoid(z)
            d_oa = doa_ref[:, lanes].astype(_F32)
            d_att = d_oa * (z * sz)
            zero = jnp.zeros_like(d_att)
            q_f = q_pair.astype(_F32)
            ot, dqt = None, None
            for half in range(2):
                h = 2 * pr + half
                e, e_sink, inv = _attn_softmax_t(ops["k_lanes"][g][half], q_pair, valid, sink_ref[h])
                pn = e * inv
                dpt = _dot_nt(ops["v_lanes"][g][half], d_att)
                delta = jnp.sum(pn * dpt, axis=0, keepdims=True)
                ds = (pn * (dpt - delta)).astype(_MX)
                pn = pn.astype(_MX)
                dsink_ref[h:h + 1, :] = dsink_ref[h:h + 1, :] - jnp.sum(e_sink * inv * delta)
                o_part = _dot(ops["v_rows"][g][half], pn)
                dq_part = _dot(ops["k_rows"][g][half], ds)
                ot = o_part if ot is None else ot + o_part
                dqt = dq_part if dqt is None else dqt + dq_part
                mine = lo if half == 0 else jnp.logical_not(lo)
                dk_part = _dot(ds, jnp.where(mine, q_f, zero))
                dv_part = _dot(pn, jnp.where(mine, d_att, zero))
                dk_acc[g] = dk_part if dk_acc[g] is None else dk_acc[g] + dk_part
                dv_acc[g] = dv_part if dv_acc[g] is None else dv_acc[g] + dv_part
            dza_ref[:, lanes] = (d_oa * ot.T * (sz * (1.0 + z * (1.0 - sz)))).astype(_ST)
            dq_pairs.append(dqt.T)

        def fold(acc, scale):
            both = [a + pltpu.roll(a, 64, 1) for a in acc]
            return jnp.where(lo, both[0], both[1]) * scale

        dk_full = fold(dk_acc, A_HEAD_DIM ** -0.5)
        dv_full = fold(dv_acc, 1.0)
        dk_cur = dk_full[BLOCK:] + ck_ref[...]
        dv_cur = dv_full[BLOCK:] + cv_ref[...]
        ck_ref[...] = dk_full[:BLOCK]
        cv_ref[...] = dv_full[:BLOCK]
        c, s1, s2 = cos_ref[...], sa_ref[...], sb_ref[...]
        dqkv_ref[:, 0:512] = _rope(jnp.concatenate(dq_pairs, axis=1), c, s1, s2, -1.0).astype(_ST)
        dqkv_ref[:, 512:640] = _rope(dk_cur, c, s1, s2, -1.0).astype(_ST)
        dqkv_ref[:, 640:768] = dv_cur.astype(_ST)

    cur = lambda w: pl.BlockSpec((BLOCK, w), lambda s, i: (s * nb + nb - 1 - i, 0))
    return pl.pallas_call(
        body, name="attn_bwd", grid=(nseq, nb),
        in_specs=[pl.BlockSpec(memory_space=pltpu.SMEM), cur(768),
                  pl.BlockSpec((BLOCK, 256), lambda s, i: (s * nb + jnp.maximum(nb - 2 - i, 0), 2)),
                  cur(512), cur(512), cur(LANE), cur(LANE), cur(LANE)],
        out_specs=[cur(768), cur(512), pl.BlockSpec((8, LANE), lambda s, i: (0, 0))],
        out_shape=[jax.ShapeDtypeStruct((t, 768), _ST), jax.ShapeDtypeStruct((t, 512), _ST),
                   jax.ShapeDtypeStruct((8, LANE), _F32)],
        scratch_shapes=[pltpu.VMEM((BLOCK, A_KV_WIDTH), _F32), pltpu.VMEM((BLOCK, A_KV_WIDTH), _F32)],
        compiler_params=_cparams(("arbitrary", "arbitrary")),
    )(sinks, qkv, qkv, za, doa, cos, sa, sb)


def _gla_chunk_terms(la, qkb_ref, r0):
    g = la[r0:r0 + B_CHUNK, :]
    ri = lax.broadcasted_iota(jnp.int32, (B_CHUNK, B_CHUNK), 0)
    ci = lax.broadcasted_iota(jnp.int32, (B_CHUNK, B_CHUNK), 1)
    cum = _dot_ones((ri >= ci).astype(_F32), g)
    last = cum[B_CHUNK - 1:B_CHUNK, :]
    mid = cum[B_CHUNK // 2 - 1:B_CHUNK // 2, :]
    q = qkb_ref[r0:r0 + B_CHUNK, 0:B_KEY_WIDTH].astype(_F32) * (B_KEY_DIM ** -0.5)
    k = qkb_ref[r0:r0 + B_CHUNK, B_KEY_WIDTH:2 * B_KEY_WIDTH].astype(_F32)
    e_q, e_k, e_l, e_c = jnp.exp(cum - mid), jnp.exp(mid - cum), jnp.exp(last - cum), jnp.exp(cum)
    dec_col = jnp.exp(jnp.sum(g.T, axis=1, keepdims=True))
    return dict(qm=q * e_q, km=k * e_k, kl=k * e_l, qc=q * e_c, e_q=e_q, e_k=e_k, e_l=e_l, e_c=e_c,
                dec_col=dec_col, dec_row=jnp.exp(last), causal=ri >= ci, ri=ri)


def _gate_logits(alr_ref, wup_ref, b_ref):
    return _dot(alr_ref[...], wup_ref[...]) + b_ref[...]


def _gla_fwd(qkb, vb, zb, alr, wup, b_alpha, gn, nseq):
    t = qkb.shape[0]
    tb = min(GLA_BLOCK, t // nseq)
    nblk = t // nseq // tb
    cpb = tb // B_CHUNK

    def body(qkb_ref, vb_ref, zb_ref, alr_ref, wup_ref, b_ref, gn_ref, ob_ref, oraw_ref, sst_ref, s_ref):
        @pl.when(pl.program_id(1) == 0)
        def _():
            s_ref[...] = jnp.zeros_like(s_ref)

        la = _log_sigmoid(_gate_logits(alr_ref, wup_ref, b_ref)) * (1.0 / B_GATE_TEMP)
        o_rows = []
        for c in range(cpb):
            r0 = c * B_CHUNK
            tm = _gla_chunk_terms(la, qkb_ref, r0)
            o_heads = []
            for h in range(B_HEADS):
                kl_, vl_ = slice(h * 64, (h + 1) * 64), slice(h * 128, (h + 1) * 128)
                st = s_ref[kl_, :]
                sst_ref[c, kl_, :] = st
                v = vb_ref[r0:r0 + B_CHUNK, vl_]
                a = jnp.where(tm["causal"], _dot_nt(tm["qm"][:, kl_], tm["km"][:, kl_]), 0.0)
                o_heads.append(_dot(a, v) + _dot(tm["qc"][:, kl_], st))
                s_ref[kl_, :] = tm["dec_col"][kl_, :] * st + _dot_tn(tm["kl"][:, kl_], v)
            o_rows.append(jnp.concatenate(o_heads, axis=1))
        o = jnp.concatenate(o_rows, axis=0)
        oraw_ref[...] = o
        z = zb_ref[...].astype(_F32)
        gate = z * _sigmoid(z)
        for h in range(B_HEADS):
            vl_ = slice(h * 128, (h + 1) * 128)
            oh = o[:, vl_]
            r = lax.rsqrt(jnp.mean(oh * oh, axis=-1, keepdims=True) + NORM_EPS)
            ob_ref[:, vl_] = ((oh * r) * gn_ref[:, vl_] * gate[:, vl_]).astype(_ST)

    rows = lambda w: pl.BlockSpec((tb, w), lambda s, i: (s * nblk + i, 0))
    full = lambda a, b: pl.BlockSpec((a, b), lambda s, i: (0, 0))
    return pl.pallas_call(
        body, name="gla_fwd", grid=(nseq, nblk),
        in_specs=[rows(512), rows(512), rows(512), rows(LANE), full(LANE, B_KEY_WIDTH),
                  full(1, B_KEY_WIDTH), full(1, B_WIDTH)],
        out_specs=[rows(512), rows(512),
                   pl.BlockSpec((cpb, B_KEY_WIDTH, B_VAL_DIM), lambda s, i: (s * nblk + i, 0, 0))],
        out_shape=[jax.ShapeDtypeStruct((t, B_WIDTH), _ST), jax.ShapeDtypeStruct((t, B_WIDTH), _F32),
                   jax.ShapeDtypeStruct((t // B_CHUNK, B_KEY_WIDTH, B_VAL_DIM), _F32)],
        scratch_shapes=[pltpu.VMEM((B_KEY_WIDTH, B_VAL_DIM), _F32)],
        compiler_params=_cparams(("parallel", "arbitrary")),
    )(qkb, vb, zb, alr, wup, b_alpha, gn)


def _gla_bwd(qkb, vb, zb, alr, oraw, dob, sst, wup, b_alpha, gn, nseq):
    t = qkb.shape[0]
    tb = min(GLA_BLOCK, t // nseq)
    nblk = t // nseq // tb
    cpb = tb // B_CHUNK

    def body(qkb_ref, vb_ref, zb_ref, alr_ref, oraw_ref, dob_ref, sst_ref, wup_ref, b_ref, gn_ref,
             dqkb_ref, dvb_ref, dzb_ref, dalr_ref, dwup_ref, db_ref, dgn_ref, ds_ref, dla_ref):
        s_id, i = pl.program_id(0), pl.program_id(1)

        @pl.when((s_id == 0) & (i == 0))
        def _():
            dwup_ref[...] = jnp.zeros_like(dwup_ref)
            db_ref[...] = jnp.zeros_like(db_ref)
            dgn_ref[...] = jnp.zeros_like(dgn_ref)

        @pl.when(i == 0)
        def _():
            ds_ref[...] = jnp.zeros_like(ds_ref)

        a_pre = _gate_logits(alr_ref, wup_ref, b_ref)
        la = _log_sigmoid(a_pre) * (1.0 / B_GATE_TEMP)

        z = zb_ref[...].astype(_F32)
        sz = _sigmoid(z)
        d_ob = dob_ref[...].astype(_F32)
        tg = d_ob * (z * sz)
        dsilu = sz * (1.0 + z * (1.0 - sz))
        do_cols, dgn_cols = [], []
        for h in range(B_HEADS):
            vl_ = slice(h * 128, (h + 1) * 128)
            oh = oraw_ref[:, vl_].astype(_F32)
            r = lax.rsqrt(jnp.mean(oh * oh, axis=-1, keepdims=True) + NORM_EPS)
            on = oh * r
            gnh = gn_ref[:, vl_]
            dzb_ref[:, vl_] = (d_ob[:, vl_] * (on * gnh) * dsilu[:, vl_]).astype(_ST)
            dgn_cols.append(jnp.sum(tg[:, vl_] * on, axis=0, keepdims=True))
            do_cols.append(_rms_bwd(tg[:, vl_] * gnh, on, r))
        dgn_ref[...] = dgn_ref[...] + jnp.concatenate(dgn_cols, axis=1)
        d_o = jnp.concatenate(do_cols, axis=1)

        for c in reversed(range(cpb)):
            r0 = c * B_CHUNK
            tm = _gla_chunk_terms(la, qkb_ref, r0)
            dq_cols, dk_cols, dv_cols, dg_cols = [], [], [], []
            for h in range(B_HEADS):
                kl_, vl_ = slice(h * 64, (h + 1) * 64), slice(h * 128, (h + 1) * 128)
                qm, km, kl, qc = tm["qm"][:, kl_], tm["km"][:, kl_], tm["kl"][:, kl_], tm["qc"][:, kl_]
                st = sst_ref[c, kl_, :]
                dsn = ds_ref[kl_, :]
                doh = d_o[r0:r0 + B_CHUNK, vl_]
                v = vb_ref[r0:r0 + B_CHUNK, vl_]
                a = jnp.where(tm["causal"], _dot_nt(qm, km), 0.0)
                da = jnp.where(tm["causal"], _dot_nt(doh, v), 0.0)
                dqc = _dot_nt(doh, st)
                dqm = _dot(da, km)
                dkm = _dot_tn(da, qm)
                dkl = _dot_nt(v, dsn)
                dv_cols.append(_dot_tn(a, doh) + _dot(kl, dsn))
                ds_ref[kl_, :] = _dot_tn(qc, doh) + tm["dec_col"][kl_, :] * dsn
                dq_cols.append((dqm * tm["e_q"][:, kl_] + dqc * tm["e_c"][:, kl_]) * (B_KEY_DIM ** -0.5))
                dk_cols.append(dkm * tm["e_k"][:, kl_] + dkl * tm["e_l"][:, kl_])
                dcum = dqm * qm - dkm * km + dqc * qc - dkl * kl
                ddec = jnp.sum((dsn * st).T, axis=0, keepdims=True)
                dlast = jnp.sum(dkl * kl, axis=0, keepdims=True) + ddec * tm["dec_row"][:, kl_]
                dcum = jnp.where(tm["ri"] == B_CHUNK - 1, dcum + dlast, dcum)
                dg_cols.append(_dot_ones((tm["ri"] <= lax.broadcasted_iota(jnp.int32, (B_CHUNK, B_CHUNK), 1)
                                          ).astype(_F32), dcum))
            rows = slice(r0, r0 + B_CHUNK)
            dqkb_ref[rows, 0:B_KEY_WIDTH] = jnp.concatenate(dq_cols, axis=1).astype(_ST)
            dqkb_ref[rows, B_KEY_WIDTH:2 * B_KEY_WIDTH] = jnp.concatenate(dk_cols, axis=1).astype(_ST)
            dvb_ref[rows, :] = jnp.concatenate(dv_cols, axis=1).astype(_ST)
            dla_ref[rows, :] = jnp.concatenate(dg_cols, axis=1)

        da_pre = dla_ref[...] * (1.0 / B_GATE_TEMP) * (1.0 - _sigmoid(a_pre))
        dalr_ref[...] = _dot_nt(da_pre, wup_ref[...]).astype(_ST)
        dwup_ref[...] = dwup_ref[...] + _dot_tn(alr_ref[...], da_pre)
        db_ref[...] = db_ref[...] + jnp.sum(da_pre, axis=0, keepdims=True)

    blk = lambda s, i: s * nblk + nblk - 1 - i
    rows = lambda w: pl.BlockSpec((tb, w), lambda s, i: (blk(s, i), 0))
    full = lambda a, b: pl.BlockSpec((a, b), lambda s, i: (0, 0))
    act = lambda w: jax.ShapeDtypeStruct((t, w), _ST)
    return pl.pallas_call(
        body, name="gla_bwd", grid=(nseq, nblk),
        in_specs=[rows(512), rows(512), rows(512), rows(LANE), rows(512), rows(512),
                  pl.BlockSpec((cpb, B_KEY_WIDTH, B_VAL_DIM), lambda s, i: (blk(s, i), 0, 0)),
                  full(LANE, B_KEY_WIDTH), full(1, B_KEY_WIDTH), full(1, B_WIDTH)],
        out_specs=[rows(512), rows(512), rows(512), rows(LANE), full(LANE, B_KEY_WIDTH),
                   full(1, B_KEY_WIDTH), full(1, B_WIDTH)],
        out_shape=[act(512), act(512), act(512), act(LANE),
                   jax.ShapeDtypeStruct((LANE, B_KEY_WIDTH), _F32),
                   jax.ShapeDtypeStruct((1, B_KEY_WIDTH), _F32), jax.ShapeDtypeStruct((1, B_WIDTH), _F32)],
        scratch_shapes=[pltpu.VMEM((B_KEY_WIDTH, B_VAL_DIM), _F32), pltpu.VMEM((tb, B_KEY_WIDTH), _F32)],
        compiler_params=_cparams(("arbitrary", "arbitrary")),
    )(qkb, vb, zb, alr, oraw, dob, sst, wup, b_alpha, gn)


def _merge_loss(oa, ob, ga, gb, x2, tgt, wa, wb, wo, g_final):
    t = x2.shape[0]
    tm = min(t, 256)
    nt = t // tm

    def body(oa_ref, ob_ref, ga_ref, gb_ref, x_ref, t_ref, wa_ref, wb_ref, wo_ref, gf_ref,
             dh_ref, doa_ref, dob_ref, dga_ref, dgb_ref, dwa_ref, dwb_ref, dwo_ref, dgf_ref, loss_ref):
        i = pl.program_id(0)

        @pl.when(i == 0)
        def _():
            dwa_ref[...] = jnp.zeros_like(dwa_ref)
            dwb_ref[...] = jnp.zeros_like(dwb_ref)
            dwo_ref[...] = jnp.zeros_like(dwo_ref)
            dgf_ref[...] = jnp.zeros_like(dgf_ref)
            loss_ref[...] = jnp.zeros_like(loss_ref)

        oa_v, ob_v = oa_ref[...], ob_ref[...]
        ya, yb = _dot(oa_v, wa_ref[...]), _dot(ob_v, wb_ref[...])
        sga, sgb = _sigmoid(ga_ref[...].astype(_F32)), _sigmoid(gb_ref[...].astype(_F32))
        merged = (sga * ya + sgb * yb).astype(_MX)
        out = x_ref[...] + _dot(merged, wo_ref[...])
        r = lax.rsqrt(jnp.mean(out * out, axis=-1, keepdims=True) + NORM_EPS)
        nrm = out * r
        gf = gf_ref[...]
        err = nrm * gf - t_ref[...]
        loss_ref[...] = loss_ref[...] + (0.5 / D_MODEL) * jnp.sum(err * err, axis=0, keepdims=True)
        dy = err * (1.0 / D_MODEL)
        dgf_ref[...] = dgf_ref[...] + jnp.sum(dy * nrm, axis=0, keepdims=True)
        dh = _rms_bwd(dy * gf, nrm, r)
        dh_ref[...] = dh
        dh_mx = dh.astype(_MX)
        dmer = _dot_nt(dh_mx, wo_ref[...])
        dwo_ref[...] = dwo_ref[...] + _dot_tn(merged, dh_mx)
        dya, dyb = (dmer * sga).astype(_MX), (dmer * sgb).astype(_MX)
        dga_ref[...] = (dmer * ya * sga * (1.0 - sga)).astype(_ST)
        dgb_ref[...] = (dmer * yb * sgb * (1.0 - sgb)).astype(_ST)
        doa_ref[...] = _dot_nt(dya, wa_ref[...]).astype(_ST)
        dob_ref[...] = _dot_nt(dyb, wb_ref[...]).astype(_ST)
        dwa_ref[...] = dwa_ref[...] + _dot_tn(oa_v, dya)
        dwb_ref[...] = dwb_ref[...] + _dot_tn(ob_v, dyb)

    rows = lambda w: pl.BlockSpec((tm, w), lambda i: (i, 0))
    full = lambda a, b: pl.BlockSpec((a, b), lambda i: (0, 0))
    return pl.pallas_call(
        body, name="merge_loss", grid=(nt,),
        in_specs=[rows(512), rows(512), rows(D_MODEL), rows(D_MODEL), rows(D_MODEL), rows(D_MODEL),
                  full(A_WIDTH, D_MODEL), full(B_WIDTH, D_MODEL), full(D_MODEL, D_MODEL), full(1, D_MODEL)],
        out_specs=[rows(D_MODEL), rows(512), rows(512), rows(D_MODEL), rows(D_MODEL),
                   full(A_WIDTH, D_MODEL), full(B_WIDTH, D_MODEL), full(D_MODEL, D_MODEL),
                   full(1, D_MODEL), full(1, D_MODEL)],
        out_shape=[jax.ShapeDtypeStruct((t, D_MODEL), _F32), jax.ShapeDtypeStruct((t, 512), _ST),
                   jax.ShapeDtypeStruct((t, 512), _ST), jax.ShapeDtypeStruct((t, D_MODEL), _ST),
                   jax.ShapeDtypeStruct((t, D_MODEL), _ST),
                   jax.ShapeDtypeStruct((A_WIDTH, D_MODEL), _F32), jax.ShapeDtypeStruct((B_WIDTH, D_MODEL), _F32),
                   jax.ShapeDtypeStruct((D_MODEL, D_MODEL), _F32), jax.ShapeDtypeStruct((1, D_MODEL), _F32),
                   jax.ShapeDtypeStruct((1, D_MODEL), _F32)],
        compiler_params=_cparams(("arbitrary",), VMEM_LIMIT),
    )(oa, ob, ga, gb, x2, tgt, wa, wb, wo, g_final)


def _in_proj_bwd_x(dpieces, wt, x2, dh2, g_in):
    t = x2.shape[0]
    tm = min(t, 256)
    np_ = len(PIECES)

    def body(*refs):
        dp_refs = refs[:np_]
        w_ref, x_ref, dh2_ref, g_ref, gx_ref, dg_ref = refs[np_:]

        @pl.when(pl.program_id(0) == 0)
        def _():
            dg_ref[...] = jnp.zeros_like(dg_ref)

        dh = None
        for (name, a, b), dp in zip(PIECES, dp_refs):
            part = _dot(dp[...], w_ref[a:b, :])
            dh = part if dh is None else dh + part
        xv = x_ref[...]
        r = lax.rsqrt(jnp.mean(xv * xv, axis=-1, keepdims=True) + NORM_EPS)
        nrm = xv * r
        dg_ref[...] = dg_ref[...] + jnp.sum(dh * nrm, axis=0, keepdims=True)
        gx_ref[...] = dh2_ref[...] + _rms_bwd(dh * g_ref[...], nrm, r)

    rows = lambda w: pl.BlockSpec((tm, w), lambda i: (i, 0))
    full = lambda a, b: pl.BlockSpec((a, b), lambda i: (0, 0))
    return pl.pallas_call(
        body, name="in_proj_bwd_x", grid=(t // tm,),
        in_specs=[rows(b - a) for _, a, b in PIECES] + [full(D_IN, D_MODEL), rows(D_MODEL), rows(D_MODEL),
                                                          full(1, D_MODEL)],
        out_specs=[rows(D_MODEL), full(1, D_MODEL)],
        out_shape=[jax.ShapeDtypeStruct((t, D_MODEL), _F32), jax.ShapeDtypeStruct((1, D_MODEL), _F32)],
        compiler_params=_cparams(("arbitrary",), VMEM_LIMIT),
    )(*dpieces, wt, x2, dh2, g_in)


def _in_proj_bwd_w(h, dpieces):
    t = h.shape[0]
    tm = min(t, 512)
    nt = t // tm
    np_ = len(PIECES)

    def body(*refs):
        h_ref, dp_refs, out_ref = refs[0], refs[1:1 + np_], refs[1 + np_]
        acc_ref, sem = refs[2 + np_:]
        i = pl.program_id(0)

        @pl.when(i == 0)
        def _():
            acc_ref[...] = jnp.zeros_like(acc_ref)

        hv = h_ref[...]
        writes = []
        for j, ((name, a, b), dp) in enumerate(zip(PIECES, dp_refs)):
            part = _dot_tn(dp[...], hv)
            if name == "alr":
                b = a + B_GATE_RANK
                part = part[0:B_GATE_RANK]
            acc_ref[a:b, :] = acc_ref[a:b, :] + part
            writes.append(pltpu.make_async_copy(acc_ref.at[a:b], out_ref.at[a:b], sem.at[j]))

            @pl.when(i == nt - 1)
            def _(cp=writes[-1]):
                cp.start()

        @pl.when(i == nt - 1)
        def _():
            for cp in writes:
                cp.wait()

    rows = lambda w: pl.BlockSpec((tm, w), lambda i: (i, 0))
    return pl.pallas_call(
        body, name="in_proj_bwd_w", grid=(nt,),
        in_specs=[rows(D_MODEL)] + [rows(b - a) for _, a, b in PIECES],
        out_specs=_ANY, out_shape=jax.ShapeDtypeStruct((D_IN, D_MODEL), _F32),
        scratch_shapes=[pltpu.VMEM((D_IN, D_MODEL), _F32), pltpu.SemaphoreType.DMA((np_,))],
        compiler_params=_cparams(("arbitrary",), VMEM_LIMIT),
    )(h, *dpieces)


def _place():
    return lax.axis_index("x"), lax.axis_index("y"), lax.axis_index("c")


def _other_chips(x, y):
    return [(1 - x, y), (x, 1 - y), (1 - x, 1 - y)]


class _Split(NamedTuple):
    by_rows: bool
    step: int
    size: int

    def half(self, ref, c):
        r, n = ref.shape[-2:]
        if self.by_rows:
            return ref.at[:, pl.ds(pl.multiple_of(c * (n // 2), LANE), n // 2)]
        return ref.at[pl.ds(pl.multiple_of(c * (r // 2), 16), r // 2), :]

    def chip_part(self, ref, k):
        if self.by_rows:
            return ref.at[pl.ds(pl.multiple_of(k * self.step, 16), self.size), :]
        return ref.at[:, pl.ds(pl.multiple_of(k * self.size, LANE), self.size)]

    def half_shape(self, shape):
        r, n = shape
        return (r, n // 2) if self.by_rows else (r // 2, n)

    def part_shape(self, shape):
        r, n = shape
        return (self.size, n) if self.by_rows else (r, self.size)


SPLIT_W_IN_T = _Split(True, WINDOW_STEP, WINDOW_ROWS)
SPLIT_W_O = _Split(True, 256, 256)
SPLIT_W_OUT = _Split(False, 256, 256)


def _gather_weights(shards, splits, fulls):
    nw = len(shards)

    def body(*refs):
        ins, outs = refs[:nw], refs[nw:2 * nw]
        send_a, recv_a, send_b, recv_b = refs[2 * nw:]
        x, y, c = _place()
        me = 2 * x + y
        peers = _other_chips(x, y)

        def place(i, k, half):
            if splits[i] is None:
                return outs[i].at[k]
            if fulls[i][0] == 4 and len(fulls[i]) == 3:
                whole = outs[i].at[k]
            else:
                whole = splits[i].chip_part(outs[i], k)
            return splits[i].half(whole, half)

        first, passed = [], []
        for i in range(nw):
            src = ins[i] if splits[i] is None else splits[i].half(ins[i], c)
            for j, (px, py) in enumerate(peers):
                cp = pltpu.make_async_remote_copy(
                    src_ref=src, dst_ref=place(i, me, c), send_sem=send_a.at[3 * i + j],
                    recv_sem=recv_a.at[3 * i + j], device_id=(px, py, c), device_id_type=_MESH)
                cp.start()
                first.append(cp)
        for i in range(nw):
            for j, (px, py) in enumerate(peers):
                landed = place(i, 2 * px + py, c)
                pltpu.make_async_remote_copy(
                    src_ref=landed, dst_ref=landed, send_sem=send_a.at[3 * i + j], recv_sem=recv_a.at[3 * i + j],
                    device_id=(px, py, c), device_id_type=_MESH).wait_recv()
                if splits[i] is not None:
                    cp = pltpu.make_async_remote_copy(
                        src_ref=landed, dst_ref=landed, send_sem=send_b.at[3 * i + j], recv_sem=recv_b.at[3 * i + j],
                        device_id=(x, y, 1 - c), device_id_type=_MESH)
                    cp.start()
                    passed.append(cp)
        for i in range(nw):
            if splits[i] is None:
                continue
            for j, (px, py) in enumerate(peers):
                theirs = place(i, 2 * px + py, 1 - c)
                pltpu.make_async_remote_copy(
                    src_ref=theirs, dst_ref=theirs, send_sem=send_b.at[3 * i + j], recv_sem=recv_b.at[3 * i + j],
                    device_id=(x, y, 1 - c), device_id_type=_MESH).wait_recv()
        for cp in first + passed:
            cp.wait_send()

    return pl.pallas_call(
        body, name="gather_weights",
        in_specs=[_ANY] * nw, out_specs=[_ANY] * nw,
        out_shape=[jax.ShapeDtypeStruct(f, s.dtype) for f, s in zip(fulls, shards)],
        scratch_shapes=[pltpu.SemaphoreType.DMA((3 * nw,)) for _ in range(4)],
    )(*shards)


def _assemble_w_in_t(slots):
    bw = 256
    ov = WINDOW_ROWS - WINDOW_STEP

    def body(s_ref, o_ref):
        for k in range(4):
            base = k * WINDOW_STEP
            lo = 0 if k == 0 else ov
            if k > 0:
                o_ref[base:base + ov, :] = s_ref[k - 1, WINDOW_STEP:WINDOW_ROWS, :] + s_ref[k, 0:ov, :]
            hi = WINDOW_ROWS if k == 3 else WINDOW_STEP
            o_ref[base + lo:base + hi, :] = s_ref[k, lo:hi, :]

    return pl.pallas_call(
        body, name="assemble_w_in_t", grid=(D_MODEL // bw,),
        in_specs=[pl.BlockSpec((4, WINDOW_ROWS, bw), lambda i: (0, 0, i))],
        out_specs=pl.BlockSpec((D_IN, bw), lambda i: (0, i)),
        out_shape=jax.ShapeDtypeStruct((D_IN, D_MODEL), slots.dtype),
        compiler_params=_cparams(("parallel",)),
    )(slots)


def _pair_exchange(grads, splits):
    nw = len(grads)

    def body(*refs):
        ins, outs = refs[:nw], refs[nw:2 * nw]
        send, recv = refs[2 * nw:]
        x, y, c = _place()
        copies = []
        for i in range(nw):
            cp = pltpu.make_async_remote_copy(
                src_ref=splits[i].half(ins[i], 1 - c), dst_ref=outs[i], send_sem=send.at[i], recv_sem=recv.at[i],
                device_id=(x, y, 1 - c), device_id_type=_MESH)
            cp.start()
            copies.append(cp)
        for cp in copies:
            cp.wait()

    return pl.pallas_call(
        body, name="grad_pair_exchange",
        in_specs=[_ANY] * nw, out_specs=[_ANY] * nw,
        out_shape=[jax.ShapeDtypeStruct(sp.half_shape(g.shape), g.dtype) for g, sp in zip(grads, splits)],
        scratch_shapes=[pltpu.SemaphoreType.DMA((nw,)), pltpu.SemaphoreType.DMA((nw,))],
    )(*grads)


def _row_block(rows):
    for cand in (976, 176, 256, 128):
        if rows % cand == 0:
            return cand
    return rows


def _pair_sum(g, r, split, c_arr, name):
    hr, hn = r.shape
    br = _row_block(hr)
    if split.by_rows:
        g_spec = pl.BlockSpec((br, hn), lambda i, c_ref: (i, c_ref[0]))
    else:
        g_spec = pl.BlockSpec((br, hn), lambda i, c_ref: (c_ref[0] * (hr // br) + i, 0))

    def body(c_ref, g_ref, r_ref, o_ref):
        o_ref[...] = (g_ref[...] + r_ref[...]).astype(o_ref.dtype)

    return pl.pallas_call(
        body, name=name,
        grid_spec=pltpu.PrefetchScalarGridSpec(
            num_scalar_prefetch=1, grid=(hr // br,),
            in_specs=[g_spec, pl.BlockSpec((br, hn), lambda i, c_ref: (i, 0))],
            out_specs=pl.BlockSpec((br, hn), lambda i, c_ref: (i, 0))),
        out_shape=jax.ShapeDtypeStruct(r.shape, _MX),
        compiler_params=_cparams(("parallel",)),
    )(c_arr, g, r)


def _chip_exchange(parts, splits):
    nw = len(parts)

    def body(*refs):
        ins, outs = refs[:nw], refs[nw:2 * nw]
        send, recv, lsem = refs[2 * nw:]
        x, y, c = _place()
        me = 2 * x + y
        copies = []
        for i in range(nw):
            mine = pltpu.make_async_copy(splits[i].chip_part(ins[i], me), outs[i].at[me], lsem.at[i])
            mine.start()
            copies.append(mine)
            for j, (px, py) in enumerate(_other_chips(x, y)):
                cp = pltpu.make_async_remote_copy(
                    src_ref=splits[i].chip_part(ins[i], 2 * px + py), dst_ref=outs[i].at[me],
                    send_sem=send.at[3 * i + j], recv_sem=recv.at[3 * i + j],
                    device_id=(px, py, c), device_id_type=_MESH)
                cp.start()
                copies.append(cp)
        for cp in copies:
            cp.wait()

    return pl.pallas_call(
        body, name="grad_chip_exchange",
        in_specs=[_ANY] * nw, out_specs=[_ANY] * nw,
        out_shape=[jax.ShapeDtypeStruct((4,) + sp.part_shape(p.shape), p.dtype) for p, sp in zip(parts, splits)],
        scratch_shapes=[pltpu.SemaphoreType.DMA((3 * nw,)), pltpu.SemaphoreType.DMA((3 * nw,)),
                        pltpu.SemaphoreType.DMA((nw,))],
    )(*parts)


def _sum_chips(q, split, c_arr, name):
    _, hr, hn = q.shape
    br = _row_block(hr)
    if split.by_rows:
        out_shape = (hr, 2 * hn)
        o_spec = pl.BlockSpec((br, hn), lambda i, c_ref: (i, c_ref[0]))
    else:
        out_shape = (2 * hr, hn)
        o_spec = pl.BlockSpec((br, hn), lambda i, c_ref: (c_ref[0] * (hr // br) + i, 0))

    def body(c_ref, q_ref, o_ref):
        f = lambda k: q_ref[k].astype(_F32)
        o_ref[...] = ((f(0) + f(1)) + f(2)) + f(3)

    return pl.pallas_call(
        body, name=name,
        grid_spec=pltpu.PrefetchScalarGridSpec(
            num_scalar_prefetch=1, grid=(hr // br,),
            in_specs=[pl.BlockSpec((4, br, hn), lambda i, c_ref: (0, i, 0))], out_specs=o_spec),
        out_shape=jax.ShapeDtypeStruct(out_shape, _F32),
        compiler_params=_cparams(("parallel",)),
    )(c_arr, q)


def _pair_share(bufs, splits):
    nw = len(bufs)

    def body(*refs):
        ins, outs = refs[:nw], refs[nw:2 * nw]
        send, recv = refs[2 * nw:]
        x, y, c = _place()
        copies = []
        for i in range(nw):
            cp = pltpu.make_async_remote_copy(
                src_ref=splits[i].half(ins[i], c), dst_ref=splits[i].half(outs[i], c), send_sem=send.at[i],
                recv_sem=recv.at[i], device_id=(x, y, 1 - c), device_id_type=_MESH)
            cp.start()
            copies.append(cp)
        for cp in copies:
            cp.wait()

    return pl.pallas_call(
        body, name="grad_pair_share",
        in_specs=[_ANY] * nw, out_specs=[_ANY] * nw,
        out_shape=[jax.ShapeDtypeStruct(b.shape, b.dtype) for b in bufs],
        input_output_aliases={i: i for i in range(nw)},
        scratch_shapes=[pltpu.SemaphoreType.DMA((nw,)), pltpu.SemaphoreType.DMA((nw,))],
    )(*bufs)


def _all_sum_small(v):
    n = v.shape[1]

    def body(v_ref, tot_ref, all_ref, send, recv):
        x, y, c = _place()
        me = 4 * x + 2 * y + c
        all_ref[me] = v_ref[...]
        copies = []
        for r in range(1, 8):
            px = 1 - x if r & 4 else x
            py = 1 - y if r & 2 else y
            pc = 1 - c if r & 1 else c
            cp = pltpu.make_async_remote_copy(
                src_ref=v_ref, dst_ref=all_ref.at[me], send_sem=send.at[r - 1], recv_sem=recv.at[r - 1],
                device_id=(px, py, pc), device_id_type=_MESH)
            cp.start()
            copies.append(cp)
        for cp in copies:
            cp.wait()
        acc = all_ref[0]
        for d in range(1, 8):
            acc = acc + all_ref[d]
        tot_ref[...] = acc

    vm = pl.BlockSpec(memory_space=pltpu.VMEM)
    return pl.pallas_call(
        body, name="small_all_sum",
        in_specs=[vm], out_specs=[vm, vm],
        out_shape=[jax.ShapeDtypeStruct((8, n), _F32), jax.ShapeDtypeStruct((8, 8, n), _F32)],
        scratch_shapes=[pltpu.SemaphoreType.DMA((7,)), pltpu.SemaphoreType.DMA((7,))],
    )(v)[0]


def _adamw(w, g, m, v, name):
    lead = w.shape[0] != 1
    r, n = (w.shape[0], w.shape[2]) if lead else w.shape[1:]
    br = r
    for cand in (256, 244, 128):
        if r > cand and r % cand == 0:
            br = cand
            break

    def body(w_ref, g_ref, m_ref, v_ref, d_ref, nm_ref, nv_ref):
        gv = g_ref[...]
        m2 = ADAM_B1 * m_ref[...] + (1.0 - ADAM_B1) * gv
        v2 = ADAM_B2 * v_ref[...] + (1.0 - ADAM_B2) * (gv * gv)
        m_hat = m2 / (1.0 - ADAM_B1 ** ADAM_STEP)
        v_hat = v2 / (1.0 - ADAM_B2 ** ADAM_STEP)
        d_ref[...] = -ADAM_LR * (m_hat / (jnp.sqrt(v_hat) + ADAM_EPS) + ADAM_WD * w_ref[...])
        nm_ref[...] = m2
        nv_ref[...] = v2

    blk = pl.BlockSpec((br, 1, n), lambda i: (i, 0, 0)) if lead else pl.BlockSpec((None, br, n), lambda i: (0, i, 0))
    shp = jax.ShapeDtypeStruct(w.shape, _F32)
    return pl.pallas_call(
        body, name=name, grid=(r // br,),
        in_specs=[blk] * 4, out_specs=[blk] * 3, out_shape=[shp] * 3,
        compiler_params=_cparams(("parallel",)),
    )(w, g, m, v)


def kernel(x, positions, g_in, w_in, w_alpha_up, b_alpha, attn_sinks, g_gla_norm, w_out_a, w_out_b, w_o, g_final, loss_target, m_g_in, m_w_in, m_w_alpha_up, m_b_alpha, m_attn_sinks, m_g_gla_norm, m_w_out_a, m_w_out_b, m_w_o, m_g_final, v_g_in, v_w_in, v_w_alpha_up, v_b_alpha, v_attn_sinks, v_g_gla_norm, v_w_out_a, v_w_out_b, v_w_o, v_g_final):
    nseq, seq, _ = x.shape
    t = nseq * seq
    cx, cy, cc = _place()
    chip = 2 * cx + cy
    c_arr = jnp.reshape(cc, (1,)).astype(jnp.int32)

    tr = lambda w: jnp.transpose(w, (2, 0, 1))
    w_in_t = tr(w_in).reshape(SHARD, D_MODEL).astype(_MX)
    window = lax.dynamic_update_slice(jnp.zeros((WINDOW_ROWS, D_MODEL), _MX), w_in_t, (4 * chip, 0))
    shards = [window, w_out_a[0].astype(_MX), w_out_b[0].astype(_MX), w_o[0].astype(_MX), w_alpha_up[0].astype(_MX)]
    splits = [SPLIT_W_IN_T, SPLIT_W_OUT, SPLIT_W_OUT, SPLIT_W_O, None]
    fulls = [(4, WINDOW_ROWS, D_MODEL), (A_WIDTH, D_MODEL), (B_WIDTH, D_MODEL), (D_MODEL, D_MODEL),
             (4, B_GATE_RANK, B_KEY_WIDTH // 4)]
    win_g, wa, wb, wo, wup_g = _gather_weights(shards, splits, fulls)
    nsh = D_MODEL // 4
    win_g = lax.dynamic_update_slice(win_g, window[None], (chip, 0, 0))
    wa = lax.dynamic_update_slice(wa, shards[1], (0, nsh * chip))
    wb = lax.dynamic_update_slice(wb, shards[2], (0, nsh * chip))
    wo = lax.dynamic_update_slice(wo, shards[3], (nsh * chip, 0))
    wup_g = lax.dynamic_update_slice(wup_g, shards[4][None], (chip, 0, 0))
    wt = _assemble_w_in_t(win_g)
    wup = jnp.concatenate([jnp.transpose(wup_g, (1, 0, 2)).reshape(B_GATE_RANK, B_KEY_WIDTH),
                           jnp.zeros((LANE - B_GATE_RANK, B_KEY_WIDTH), _MX)], axis=0)

    x2 = x.reshape(t, D_MODEL)
    tgt = loss_target.reshape(t, D_MODEL)
    pos_f = positions.astype(_F32).reshape(t, 1)
    sinks = attn_sinks.reshape(A_HEADS)
    gf = g_final.reshape(1, D_MODEL)

    cos, sa, sb = _rope_tables(pos_f)
    h, qkv, za, qkb, vb, zb, alr, ga, gb = _in_proj(x2, g_in, wt, cos, sa, sb)
    oa = _attn_fwd(qkv, za, sinks, nseq)
    ob, oraw, sst = _gla_fwd(qkb, vb, zb, alr, wup, b_alpha, g_gla_norm, nseq)

    dh2, doa, dob, dga, dgb, dwa, dwb, dwo, dgf, lossv = _merge_loss(oa, ob, ga, gb, x2, tgt, wa, wb, wo, gf)

    dqkv, dza, dsink = _attn_bwd(qkv, za, doa, sinks, cos, sa, sb, nseq)
    dqkb, dvb, dzb, dalr, dwup, dba, dgn = _gla_bwd(qkb, vb, zb, alr, oraw, dob, sst, wup, b_alpha, g_gla_norm, nseq)
    dpieces = [dqkv, dza, dqkb, dvb, dzb, dalr, dga, dgb]
    grad_x2, dgin = _in_proj_bwd_x(dpieces, wt, x2, dh2, g_in)
    dwin_t = _in_proj_bwd_w(h, dpieces)

    grads = [dwin_t, dwa, dwb, dwo]
    gsplits = [SPLIT_W_IN_T, SPLIT_W_OUT, SPLIT_W_OUT, SPLIT_W_O]
    names = ("w_in", "w_out_a", "w_out_b", "w_o")
    from_sibling = _pair_exchange(grads, gsplits)
    pair_sums = [_pair_sum(g, r, sp, c_arr, "pair_sum_" + nm)
                 for g, r, sp, nm in zip(grads, from_sibling, gsplits, names)]
    from_chips = _chip_exchange(pair_sums, gsplits)
    reduced = [_sum_chips(q, sp, c_arr, "chip_sum_" + nm) for q, sp, nm in zip(from_chips, gsplits, names)]
    g_window, g_wa, g_wb, g_wo = _pair_share(reduced, gsplits)
    g_win_t = lax.dynamic_slice(g_window, (4 * chip, 0), (SHARD, D_MODEL)).reshape(SHARD, 1, D_MODEL)

    small = jnp.concatenate([
        dgin, dgf, dgn, dba,
        jnp.pad(dsink[:, 0].reshape(1, A_HEADS), ((0, 0), (0, LANE - A_HEADS))),
        jnp.pad(jnp.sum(lossv, axis=1, keepdims=True), ((0, 0), (0, LANE - 1))),
        dwup[:B_GATE_RANK].reshape(1, B_GATE_RANK * B_KEY_WIDTH)], axis=1)
    tot = _all_sum_small(jnp.pad(small, ((0, 7), (0, 0))))[0:1]
    o = 0
    def take(n):
        nonlocal o
        o += n
        return tot[:, o - n:o]
    g_gin, g_gf, g_gn, g_ba = take(D_MODEL), take(D_MODEL), take(B_WIDTH), take(B_KEY_WIDTH)
    g_sink = take(LANE)[:, :A_HEADS]
    loss = take(LANE)[0, 0]
    g_wup_full = take(B_GATE_RANK * B_KEY_WIDTH).reshape(B_GATE_RANK, B_KEY_WIDTH)
    nup = B_KEY_WIDTH // 4
    g_wup = lax.dynamic_slice(g_wup_full, (0, chip * nup), (B_GATE_RANK, nup))

    def pack(*parts):
        return jnp.concatenate([p.reshape(1, -1) for p in parts], axis=1)

    sm_w = pack(g_in, g_final, g_gla_norm, b_alpha, attn_sinks, w_alpha_up)
    sm_g = pack(g_gin, g_gf, g_gn, g_ba, g_sink, g_wup)
    sm_m = pack(m_g_in, m_g_final, m_g_gla_norm, m_b_alpha, m_attn_sinks, m_w_alpha_up)
    sm_v = pack(v_g_in, v_g_final, v_g_gla_norm, v_b_alpha, v_attn_sinks, v_w_alpha_up)
    sm_out = [p[0] for p in _adamw(sm_w[None], sm_g[None], sm_m[None], sm_v[None], "adamw_small")]

    def unpack(p):
        sizes = (D_MODEL, D_MODEL, B_WIDTH, B_KEY_WIDTH, A_HEADS, B_GATE_RANK * nup)
        outs, at = [], 0
        for s in sizes:
            outs.append(p[:, at:at + s])
            at += s
        gi, gfin, gnn, ba, sk, wu = outs
        return dict(g_in=gi, g_final=gfin.reshape(D_MODEL), g_gla_norm=gnn, b_alpha=ba, attn_sinks=sk,
                    w_alpha_up=wu.reshape(1, B_GATE_RANK, nup))

    untr = lambda a: jnp.transpose(a, (1, 2, 0))
    big = dict(w_in=tuple(untr(a) for a in (g_win_t,) + tuple(_adamw(tr(w_in), g_win_t, tr(m_w_in), tr(v_w_in), "adamw_w_in"))))
    for nm, w, g, m, v in (("w_out_a", w_out_a, g_wa, m_w_out_a, v_w_out_a),
                           ("w_out_b", w_out_b, g_wb, m_w_out_b, v_w_out_b), ("w_o", w_o, g_wo, m_w_o, v_w_o)):
        big[nm] = (g[None],) + tuple(_adamw(w, g[None], m, v, "adamw_" + nm))

    order = ("g_in", "w_in", "w_alpha_up", "b_alpha", "attn_sinks", "g_gla_norm", "w_out_a", "w_out_b", "w_o", "g_final")
    small_sets = [unpack(sm_g)] + [unpack(p) for p in sm_out]
    outs = []
    for kind in range(4):
        for nm in order:
            outs.append(big[nm][kind] if nm in big else small_sets[kind][nm])
    return (loss, grad_x2.reshape(x.shape), *outs)
```

```python
import math
from typing import NamedTuple

import numpy as np
import jax
import jax.numpy as jnp
from jax import lax
from jax.experimental import pallas as pl
from jax.experimental.pallas import tpu as pltpu

D_MODEL = 1024
A_HEADS, A_KV_HEADS, A_HEAD_DIM = 8, 2, 64
A_GROUP = A_HEADS // A_KV_HEADS
A_WIDTH, A_KV_WIDTH = 512, 128
BLOCK = 128
ROPE_THETA = 500000.0
ROPE_DIM = 16
B_HEADS, B_KEY_DIM, B_VAL_DIM = 4, 64, 128
B_KEY_WIDTH, B_WIDTH = 256, 512
B_GATE_RANK = 16
B_GATE_TEMP = 16.0
B_CHUNK = 64
NORM_EPS = 1e-6
NEG_BIG = -1e30
D_IN = 4880

ADAM_LR, ADAM_B1, ADAM_B2, ADAM_EPS, ADAM_WD, ADAM_STEP = 0.001, 0.9, 0.999, 1e-08, 0.01, 10

LANE = 128
ALR_AT = 2816
PIECES = (("qkv", 0, 768), ("za", 768, 1280), ("qkb", 1280, 1792), ("vb", 1792, 2304),
          ("zb", 2304, 2816), ("alr", ALR_AT, ALR_AT + LANE), ("ga", 2832, 3856), ("gb", 3856, 4880))
SHARD = D_IN // 4
WINDOW_STEP = 1216
WINDOW_ROWS = 1232

GLA_BLOCK = 256
VMEM_LIMIT = 56 * 1024 * 1024

_F32 = jnp.float32
_MX = jnp.bfloat16
_ST = jnp.bfloat16

_MESH = pl.DeviceIdType.MESH
_ANY = pl.BlockSpec(memory_space=pl.ANY)


def _cparams(sem=None, vmem=None):
    return pltpu.CompilerParams(dimension_semantics=sem, vmem_limit_bytes=vmem)


def _dot(a, b):
    return jnp.dot(a.astype(_MX), b.astype(_MX), preferred_element_type=_F32)


def _dot_nt(a, b):
    return lax.dot_general(a.astype(_MX), b.astype(_MX), (((1,), (1,)), ((), ())),
                           preferred_element_type=_F32)


def _dot_tn(a, b):
    return lax.dot_general(a.astype(_MX), b.astype(_MX), (((0,), (0,)), ((), ())),
                           preferred_element_type=_F32)


def _dot_ones(ones_mat, v):
    o = ones_mat.astype(jnp.bfloat16)
    v0 = v.astype(jnp.bfloat16)
    r1 = v - v0.astype(_F32)
    v1 = r1.astype(jnp.bfloat16)
    v2 = (r1 - v1.astype(_F32)).astype(jnp.bfloat16)
    d = lambda t: jnp.dot(o, t, preferred_element_type=_F32)
    return d(v0) + d(v1) + d(v2)


def _sigmoid(x):
    return 1.0 / (1.0 + jnp.exp(-x))


def _log_sigmoid(x):
    return jnp.minimum(x, 0.0) - jnp.log(1.0 + jnp.exp(-jnp.abs(x)))


def _lane_tile(t, width):
    reps = width // t.shape[1]
    return t if reps == 1 else jnp.tile(t, (1, reps))


def _rope(t, cos, sa, sb, sign):
    w = t.shape[1]
    rot = pltpu.roll(t, w - 8, 1) * _lane_tile(sa, w) + pltpu.roll(t, 8, 1) * _lane_tile(sb, w)
    return t * _lane_tile(cos, w) + sign * rot


def _rms_bwd(dy_g, n, r):
    return r * (dy_g - n * jnp.mean(dy_g * n, axis=-1, keepdims=True))


def _rope_tables(pos_f):
    t = pos_f.shape[0]
    tm = min(t, 1024)
    lane = np.arange(LANE) % A_HEAD_DIM
    half = ROPE_DIM // 2
    inv = np.exp((np.float32(-math.log(ROPE_THETA)) * np.arange(half, dtype=np.float32)) * np.float32(2.0 / ROPE_DIM))
    consts = np.zeros((8, LANE), np.float32)
    consts[0] = np.where(lane < ROPE_DIM, inv[lane % half], 0.0)
    consts[1] = np.where(lane < half, -1.0, 0.0)
    consts[2] = np.where((lane >= half) & (lane < ROPE_DIM), 1.0, 0.0)

    def body(pos_ref, c_ref, cos_ref, sa_ref, sb_ref):
        ang = pos_ref[...] * c_ref[0:1, :]
        s = jnp.sin(ang)
        cos_ref[...] = jnp.cos(ang)
        sa_ref[...] = s * c_ref[1:2, :]
        sb_ref[...] = s * c_ref[2:3, :]

    tab = jax.ShapeDtypeStruct((t, LANE), _F32)
    row = pl.BlockSpec((tm, LANE), lambda i: (i, 0))
    return pl.pallas_call(
        body, name="rope_tables", grid=(t // tm,),
        in_specs=[pl.BlockSpec((tm, 1), lambda i: (i, 0)), pl.BlockSpec((8, LANE), lambda i: (0, 0))],
        out_specs=[row, row, row], out_shape=[tab, tab, tab],
        compiler_params=_cparams(("parallel",)),
    )(pos_f, jnp.asarray(consts))


def _in_proj(x2, g_in, wt, cos, sa, sb):
    t = x2.shape[0]
    tm = min(t, 512)

    def body(x_ref, g_ref, w_ref, cos_ref, sa_ref, sb_ref, h_ref, qkv_ref, za_ref, qkb_ref,
             vb_ref, zb_ref, alr_ref, ga_ref, gb_ref):
        xv = x_ref[...]
        r = lax.rsqrt(jnp.mean(xv * xv, axis=-1, keepdims=True) + NORM_EPS)
        h = (xv * r * g_ref[...]).astype(_MX)
        h_ref[...] = h.astype(_ST)
        outs = dict(za=za_ref, qkb=qkb_ref, vb=vb_ref, zb=zb_ref, alr=alr_ref, ga=ga_ref, gb=gb_ref)
        for name, a, b in PIECES:
            p = _dot_nt(h, w_ref[a:b, :])
            if name == "qkv":
                c, s1, s2 = cos_ref[...], sa_ref[...], sb_ref[...]
                qkv_ref[:, 0:512] = _rope(p[:, 0:512], c, s1, s2, 1.0).astype(_ST)
                qkv_ref[:, 512:640] = _rope(p[:, 512:640], c, s1, s2, 1.0).astype(_ST)
                qkv_ref[:, 640:768] = p[:, 640:768].astype(_ST)
            else:
                outs[name][...] = p.astype(outs[name].dtype)

    rows = lambda w: pl.BlockSpec((tm, w), lambda i: (i, 0))
    shp = lambda name, w: jax.ShapeDtypeStruct((t, w), _F32 if name == "qkb" else _ST)
    widths = [D_MODEL] + [b - a for _, a, b in PIECES]
    return pl.pallas_call(
        body, name="in_proj", grid=(t // tm,),
        in_specs=[rows(D_MODEL), pl.BlockSpec((1, D_MODEL), lambda i: (0, 0)),
                  pl.BlockSpec((D_IN, D_MODEL), lambda i: (0, 0)), rows(LANE), rows(LANE), rows(LANE)],
        out_specs=[rows(w) for w in widths],
        out_shape=[shp(n, w) for n, w in zip(["h"] + [p[0] for p in PIECES], widths)],
        compiler_params=_cparams(("parallel",), VMEM_LIMIT),
    )(x2, g_in, wt, cos, sa, sb)


def _attn_operands(qkv_ref, kvp_ref, want_bwd):
    kf = jnp.concatenate([kvp_ref[:, 0:128], qkv_ref[:, 512:640]], axis=0).astype(_F32) * (A_HEAD_DIM ** -0.5)
    vf = jnp.concatenate([kvp_ref[:, 128:256], qkv_ref[:, 640:768]], axis=0).astype(_F32)
    lo = lax.broadcasted_iota(jnp.int32, (1, LANE), 1) < 64

    def on_lanes(a):
        sw = pltpu.roll(a, 64, 1)
        z = jnp.zeros_like(a)
        return [[jnp.where(lo, a, z).astype(_MX), jnp.where(lo, z, sw).astype(_MX)],
                [jnp.where(lo, sw, z).astype(_MX), jnp.where(lo, z, a).astype(_MX)]]

    def on_rows(a):
        at = a.T.astype(_MX)
        z = jnp.zeros((64, at.shape[1]), _MX)
        top, bot = at[0:64], at[64:128]
        return [[jnp.concatenate([top, z], axis=0), jnp.concatenate([z, top], axis=0)],
                [jnp.concatenate([bot, z], axis=0), jnp.concatenate([z, bot], axis=0)]]

    ops = dict(k_lanes=on_lanes(kf), v_rows=on_rows(vf), lo=lo)
    if want_bwd:
        ops.update(v_lanes=on_lanes(vf), k_rows=on_rows(kf))
    return ops


def _attn_valid(n):
    kj = lax.broadcasted_iota(jnp.int32, (2 * BLOCK, BLOCK), 0) - BLOCK
    qi = lax.broadcasted_iota(jnp.int32, (2 * BLOCK, BLOCK), 1)
    return (kj <= qi) & (qi - kj < BLOCK) & ((n > 0) | (kj >= 0))


def _attn_softmax_t(k_lanes, q_pair, valid, sink):
    s = jnp.where(valid, _dot_nt(k_lanes, q_pair), NEG_BIG)
    m = jnp.maximum(jnp.max(s, axis=0, keepdims=True), sink)
    e = jnp.exp(s - m)
    e_sink = jnp.exp(sink - m)
    inv = 1.0 / (jnp.sum(e, axis=0, keepdims=True) + e_sink)
    return e, e_sink, inv


def _attn_fwd(qkv, za, sinks, nseq):
    t = qkv.shape[0]
    nb = t // nseq // BLOCK

    def body(sink_ref, qkv_ref, kvp_ref, za_ref, oa_ref):
        n = pl.program_id(1)
        ops = _attn_operands(qkv_ref, kvp_ref, False)
        valid = _attn_valid(n)
        for pr in range(A_HEADS // 2):
            lanes = slice(pr * LANE, (pr + 1) * LANE)
            g = pr // (A_GROUP // 2)
            q_pair = qkv_ref[:, lanes]
            ot = None
            for half in range(2):
                e, _, inv = _attn_softmax_t(ops["k_lanes"][g][half], q_pair, valid, sink_ref[2 * pr + half])
                part = _dot(ops["v_rows"][g][half], e) * inv
                ot = part if ot is None else ot + part
            z = za_ref[:, lanes].astype(_F32)
            oa_ref[:, lanes] = (ot.T * (z * _sigmoid(z))).astype(_ST)

    cur = lambda w: pl.BlockSpec((BLOCK, w), lambda s, n: (s * nb + n, 0))
    return pl.pallas_call(
        body, name="attn_fwd", grid=(nseq, nb),
        in_specs=[pl.BlockSpec(memory_space=pltpu.SMEM), cur(768),
                  pl.BlockSpec((BLOCK, 256), lambda s, n: (s * nb + jnp.maximum(n - 1, 0), 2)), cur(512)],
        out_specs=cur(512), out_shape=jax.ShapeDtypeStruct((t, A_WIDTH), _ST),
        compiler_params=_cparams(("parallel", "arbitrary")),
    )(sinks, qkv, qkv, za)


def _attn_bwd(qkv, za, doa, sinks, cos, sa, sb, nseq):
    t = qkv.shape[0]
    nb = t // nseq // BLOCK

    def body(sink_ref, qkv_ref, kvp_ref, za_ref, doa_ref, cos_ref, sa_ref, sb_ref,
             dqkv_ref, dza_ref, dsink_ref, ck_ref, cv_ref):
        s_id, i = pl.program_id(0), pl.program_id(1)
        n = nb - 1 - i

        @pl.when((s_id == 0) & (i == 0))
        def _():
            dsink_ref[...] = jnp.zeros_like(dsink_ref)

        @pl.when(i == 0)
        def _():
            ck_ref[...] = jnp.zeros_like(ck_ref)
            cv_ref[...] = jnp.zeros_like(cv_ref)

        ops = _attn_operands(qkv_ref, kvp_ref, True)
        lo = ops["lo"]
        valid = _attn_valid(n)
        dk_acc = [None] * A_KV_HEADS
        dv_acc = [None] * A_KV_HEADS
        dq_pairs = []
        for pr in range(A_HEADS // 2):
            lanes = slice(pr * LANE, (pr + 1) * LANE)
            g = pr // (A_GROUP // 2)
            q_pair = qkv_ref[:, lanes]
            z = za_ref[:, lanes].astype(_F32)
            sz = _sigmoid(z)
            d_oa = doa_ref[:, lanes].astype(_F32)
            d_att = d_oa * (z * sz)
            zero = jnp.zeros_like(d_att)
            q_f = q_pair.astype(_F32)
            ot, dqt = None, None
            for half in range(2):
                h = 2 * pr + half
                e, e_sink, inv = _attn_softmax_t(ops["k_lanes"][g][half], q_pair, valid, sink_ref[h])
                pn = e * inv
                dpt = _dot_nt(ops["v_lanes"][g][half], d_att)
                delta = jnp.sum(pn * dpt, axis=0, keepdims=True)
                ds = (pn * (dpt - delta)).astype(_MX)
                pn = pn.astype(_MX)
                dsink_ref[h:h + 1, :] = dsink_ref[h:h + 1, :] - jnp.sum(e_sink * inv * delta)
                o_part = _dot(ops["v_rows"][g][half], pn)
                dq_part = _dot(ops["k_rows"][g][half], ds)
                ot = o_part if ot is None else ot + o_part
                dqt = dq_part if dqt is None else dqt + dq_part
                mine = lo if half == 0 else jnp.logical_not(lo)
                dk_part = _dot(ds, jnp.where(mine, q_f, zero))
                dv_part = _dot(pn, jnp.where(mine, d_att, zero))
                dk_acc[g] = dk_part if dk_acc[g] is None else dk_acc[g] + dk_part
                dv_acc[g] = dv_part if dv_acc[g] is None else dv_acc[g] + dv_part
            dza_ref[:, lanes] = (d_oa * ot.T * (sz * (1.0 + z * (1.0 - sz)))).astype(_ST)
            dq_pairs.append(dqt.T)

        def fold(acc, scale):
            both = [a + pltpu.roll(a, 64, 1) for a in acc]
            return jnp.where(lo, both[0], both[1]) * scale

        dk_full = fold(dk_acc, A_HEAD_DIM ** -0.5)
        dv_full = fold(dv_acc, 1.0)
        dk_cur = dk_full[BLOCK:] + ck_ref[...]
        dv_cur = dv_full[BLOCK:] + cv_ref[...]
        ck_ref[...] = dk_full[:BLOCK]
        cv_ref[...] = dv_full[:BLOCK]
        c, s1, s2 = cos_ref[...], sa_ref[...], sb_ref[...]
        dqkv_ref[:, 0:512] = _rope(jnp.concatenate(dq_pairs, axis=1), c, s1, s2, -1.0).astype(_ST)
        dqkv_ref[:, 512:640] = _rope(dk_cur, c, s1, s2, -1.0).astype(_ST)
        dqkv_ref[:, 640:768] = dv_cur.astype(_ST)

    cur = lambda w: pl.BlockSpec((BLOCK, w), lambda s, i: (s * nb + nb - 1 - i, 0))
    return pl.pallas_call(
        body, name="attn_bwd", grid=(nseq, nb),
        in_specs=[pl.BlockSpec(memory_space=pltpu.SMEM), cur(768),
                  pl.BlockSpec((BLOCK, 256), lambda s, i: (s * nb + jnp.maximum(nb - 2 - i, 0), 2)),
                  cur(512), cur(512), cur(LANE), cur(LANE), cur(LANE)],
        out_specs=[cur(768), cur(512), pl.BlockSpec((8, LANE), lambda s, i: (0, 0))],
        out_shape=[jax.ShapeDtypeStruct((t, 768), _ST), jax.ShapeDtypeStruct((t, 512), _ST),
                   jax.ShapeDtypeStruct((8, LANE), _F32)],
        scratch_shapes=[pltpu.VMEM((BLOCK, A_KV_WIDTH), _F32), pltpu.VMEM((BLOCK, A_KV_WIDTH), _F32)],
        compiler_params=_cparams(("arbitrary", "arbitrary")),
    )(sinks, qkv, qkv, za, doa, cos, sa, sb)


def _gla_chunk_terms(la, qkb_ref, r0):
    g = la[r0:r0 + B_CHUNK, :]
    ri = lax.broadcasted_iota(jnp.int32, (B_CHUNK, B_CHUNK), 0)
    ci = lax.broadcasted_iota(jnp.int32, (B_CHUNK, B_CHUNK), 1)
    cum = _dot_ones((ri >= ci).astype(_F32), g)
    last = cum[B_CHUNK - 1:B_CHUNK, :]
    mid = cum[B_CHUNK // 2 - 1:B_CHUNK // 2, :]
    q = qkb_ref[r0:r0 + B_CHUNK, 0:B_KEY_WIDTH].astype(_F32) * (B_KEY_DIM ** -0.5)
    k = qkb_ref[r0:r0 + B_CHUNK, B_KEY_WIDTH:2 * B_KEY_WIDTH].astype(_F32)
    e_q, e_k, e_l, e_c = jnp.exp(cum - mid), jnp.exp(mid - cum), jnp.exp(last - cum), jnp.exp(cum)
    dec_col = jnp.exp(jnp.sum(g.T, axis=1, keepdims=True))
    return dict(qm=q * e_q, km=k * e_k, kl=k * e_l, qc=q * e_c, e_q=e_q, e_k=e_k, e_l=e_l, e_c=e_c,
                dec_col=dec_col, dec_row=jnp.exp(last), causal=ri >= ci, ri=ri)


def _gate_logits(alr_ref, wup_ref, b_ref):
    return _dot(alr_ref[...], wup_ref[...]) + b_ref[...]


def _gla_fwd(qkb, vb, zb, alr, wup, b_alpha, gn, nseq):
    t = qkb.shape[0]
    tb = min(GLA_BLOCK, t // nseq)
    nblk = t // nseq // tb
    cpb = tb // B_CHUNK

    def body(qkb_ref, vb_ref, zb_ref, alr_ref, wup_ref, b_ref, gn_ref, ob_ref, oraw_ref, sst_ref, s_ref):
        @pl.when(pl.program_id(1) == 0)
        def _():
            s_ref[...] = jnp.zeros_like(s_ref)

        la = _log_sigmoid(_gate_logits(alr_ref, wup_ref, b_ref)) * (1.0 / B_GATE_TEMP)
        terms = [_gla_chunk_terms(la, qkb_ref, c * B_CHUNK) for c in range(cpb)]
        o_intra, inc = {}, {}
        for c, tm in enumerate(terms):
            for h in range(B_HEADS):
                kl_, vl_ = slice(h * 64, (h + 1) * 64), slice(h * 128, (h + 1) * 128)
                v = vb_ref[c * B_CHUNK:(c + 1) * B_CHUNK, vl_]
                a = jnp.where(tm["causal"], _dot_nt(tm["qm"][:, kl_], tm["km"][:, kl_]), 0.0)
                o_intra[c, h] = _dot(a, v)
                inc[c, h] = _dot_tn(tm["kl"][:, kl_], v)
        o_heads = {}
        for h in range(B_HEADS):
            kl_ = slice(h * 64, (h + 1) * 64)
            st = s_ref[kl_, :]
            for c, tm in enumerate(terms):
                sst_ref[c, kl_, :] = st
                o_heads[c, h] = o_intra[c, h] + _dot(tm["qc"][:, kl_], st)
                st = tm["dec_col"][kl_, :] * st + inc[c, h]
            s_ref[kl_, :] = st
        o = jnp.concatenate([jnp.concatenate([o_heads[c, h] for h in range(B_HEADS)], axis=1)
                             for c in range(cpb)], axis=0)
        oraw_ref[...] = o
        z = zb_ref[...].astype(_F32)
        gate = z * _sigmoid(z)
        for h in range(B_HEADS):
            vl_ = slice(h * 128, (h + 1) * 128)
            oh = o[:, vl_]
            r = lax.rsqrt(jnp.mean(oh * oh, axis=-1, keepdims=True) + NORM_EPS)
            ob_ref[:, vl_] = ((oh * r) * gn_ref[:, vl_] * gate[:, vl_]).astype(_ST)

    rows = lambda w: pl.BlockSpec((tb, w), lambda s, i: (s * nblk + i, 0))
    full = lambda a, b: pl.BlockSpec((a, b), lambda s, i: (0, 0))
    return pl.pallas_call(
        body, name="gla_fwd", grid=(nseq, nblk),
        in_specs=[rows(512), rows(512), rows(512), rows(LANE), full(LANE, B_KEY_WIDTH),
                  full(1, B_KEY_WIDTH), full(1, B_WIDTH)],
        out_specs=[rows(512), rows(512),
                   pl.BlockSpec((cpb, B_KEY_WIDTH, B_VAL_DIM), lambda s, i: (s * nblk + i, 0, 0))],
        out_shape=[jax.ShapeDtypeStruct((t, B_WIDTH), _ST), jax.ShapeDtypeStruct((t, B_WIDTH), _F32),
                   jax.ShapeDtypeStruct((t // B_CHUNK, B_KEY_WIDTH, B_VAL_DIM), _F32)],
        scratch_shapes=[pltpu.VMEM((B_KEY_WIDTH, B_VAL_DIM), _F32)],
        compiler_params=_cparams(("parallel", "arbitrary")),
    )(qkb, vb, zb, alr, wup, b_alpha, gn)


def _gla_bwd(qkb, vb, zb, alr, oraw, dob, sst, wup, b_alpha, gn, nseq):
    t = qkb.shape[0]
    tb = min(GLA_BLOCK, t // nseq)
    nblk = t // nseq // tb
    cpb = tb // B_CHUNK

    def body(qkb_ref, vb_ref, zb_ref, alr_ref, oraw_ref, dob_ref, sst_ref, wup_ref, b_ref, gn_ref,
             dqkb_ref, dvb_ref, dzb_ref, dalr_ref, dwup_ref, db_ref, dgn_ref, ds_ref):
        s_id, i = pl.program_id(0), pl.program_id(1)

        @pl.when((s_id == 0) & (i == 0))
        def _():
            dwup_ref[...] = jnp.zeros_like(dwup_ref)
            db_ref[...] = jnp.zeros_like(db_ref)
            dgn_ref[...] = jnp.zeros_like(dgn_ref)

        @pl.when(i == 0)
        def _():
            ds_ref[...] = jnp.zeros_like(ds_ref)

        a_pre = _gate_logits(alr_ref, wup_ref, b_ref)
        la = _log_sigmoid(a_pre) * (1.0 / B_GATE_TEMP)

        z = zb_ref[...].astype(_F32)
        sz = _sigmoid(z)
        d_ob = dob_ref[...].astype(_F32)
        tg = d_ob * (z * sz)
        dsilu = sz * (1.0 + z * (1.0 - sz))
        do_cols, dgn_cols = [], []
        for h in range(B_HEADS):
            vl_ = slice(h * 128, (h + 1) * 128)
            oh = oraw_ref[:, vl_].astype(_F32)
            r = lax.rsqrt(jnp.mean(oh * oh, axis=-1, keepdims=True) + NORM_EPS)
            on = oh * r
            gnh = gn_ref[:, vl_]
            dzb_ref[:, vl_] = (d_ob[:, vl_] * (on * gnh) * dsilu[:, vl_]).astype(_ST)
            dgn_cols.append(jnp.sum(tg[:, vl_] * on, axis=0, keepdims=True))
            do_cols.append(_rms_bwd(tg[:, vl_] * gnh, on, r))
        dgn_ref[...] = dgn_ref[...] + jnp.concatenate(dgn_cols, axis=1)
        d_o = jnp.concatenate(do_cols, axis=1)

        ri = lax.broadcasted_iota(jnp.int32, (tb, tb), 0)
        ci = lax.broadcasted_iota(jnp.int32, (tb, tb), 1)
        same = (ri // B_CHUNK) == (ci // B_CHUNK)
        low = same & (ri >= ci)
        upto_mid = same & ((ci % B_CHUNK) < B_CHUNK // 2)
        sums = _dot_ones(jnp.concatenate([m.astype(_F32) for m in (low, same, upto_mid)], axis=0), la)
        cum, last, mid = sums[0:tb], sums[tb:2 * tb], sums[2 * tb:3 * tb]
        e_q, e_k, e_l, e_c = jnp.exp(cum - mid), jnp.exp(mid - cum), jnp.exp(last - cum), jnp.exp(cum)
        q = qkb_ref[:, 0:B_KEY_WIDTH] * (B_KEY_DIM ** -0.5)
        k = qkb_ref[:, B_KEY_WIDTH:2 * B_KEY_WIDTH]
        qm, km, kl, qc = q * e_q, k * e_k, k * e_l, q * e_c
        lane_head = lax.broadcasted_iota(jnp.int32, (1, B_KEY_WIDTH), 1) // B_KEY_DIM
        d_o_mx = d_o.astype(_MX)

        def on_diagonal(st):
            z = jnp.zeros((B_KEY_DIM, B_VAL_DIM), st.dtype)
            return jnp.concatenate([jnp.concatenate(
                [st[h * B_KEY_DIM:(h + 1) * B_KEY_DIM] if g == h else z for g in range(B_HEADS)], axis=1)
                for h in range(B_HEADS)], axis=0)

        def diagonal_of(full):
            return jnp.concatenate([full[h * B_KEY_DIM:(h + 1) * B_KEY_DIM, h * B_VAL_DIM:(h + 1) * B_VAL_DIM]
                                    for h in range(B_HEADS)], axis=0)

        dqm, dkm, dv_cols = None, None, []
        for h in range(B_HEADS):
            vl_ = slice(h * B_VAL_DIM, (h + 1) * B_VAL_DIM)
            mine = lane_head == h
            qz, kz = jnp.where(mine, qm, 0.0).astype(_MX), jnp.where(mine, km, 0.0).astype(_MX)
            a = jnp.where(low, _dot_nt(qz, kz), 0.0).astype(_MX)
            da = jnp.where(low, _dot_nt(d_o_mx[:, vl_], vb_ref[:, vl_]), 0.0).astype(_MX)
            dqm_h, dkm_h = _dot(da, kz), _dot_tn(da, qz)
            dqm = dqm_h if dqm is None else dqm + dqm_h
            dkm = dkm_h if dkm is None else dkm + dkm_h
            dv_cols.append(_dot_tn(a, d_o_mx[:, vl_]))
        dv = jnp.concatenate(dv_cols, axis=1)

        chunk = [slice(c * B_CHUNK, (c + 1) * B_CHUNK) for c in range(cpb)]
        dqc_rows, g_loc = [], []
        for c in range(cpb):
            dqc_rows.append(_dot_nt(d_o_mx[chunk[c]], on_diagonal(sst_ref[c].astype(_MX))))
            g_loc.append(diagonal_of(_dot_tn(qc[chunk[c]], d_o_mx[chunk[c]])))
        cur = ds_ref[...]
        d_state = [None] * cpb
        for c in reversed(range(cpb)):
            d_state[c] = cur
            cur = g_loc[c] + jnp.exp(jnp.sum(la[chunk[c]].T, axis=1, keepdims=True)) * cur
        ds_ref[...] = cur
        dkl_rows, dv_rows, dlast_rows = [], [], []
        ones8 = jnp.ones((8, B_VAL_DIM), _F32)
        for c in range(cpb):
            dsd = on_diagonal(d_state[c].astype(_MX))
            dkl_c = _dot_nt(vb_ref[chunk[c], :], dsd)
            dkl_rows.append(dkl_c)
            dv_rows.append(_dot(kl[chunk[c]], dsd))
            prod = d_state[c] * sst_ref[c]
            p0 = prod.astype(jnp.bfloat16)
            p1 = (prod - p0.astype(_F32)).astype(jnp.bfloat16)
            p2 = (prod - p0.astype(_F32) - p1.astype(_F32)).astype(jnp.bfloat16)
            ddec = (_dot_nt(ones8, p0) + _dot_nt(ones8, p1) + _dot_nt(ones8, p2))[0:1]
            r_last = c * B_CHUNK + B_CHUNK - 1
            dlast = jnp.sum(dkl_c * kl[chunk[c]], axis=0, keepdims=True) + ddec * jnp.exp(last[r_last:r_last + 1])
            dlast_rows.append(jnp.broadcast_to(dlast, (B_CHUNK, B_KEY_WIDTH)))
        dqc, dkl = jnp.concatenate(dqc_rows, axis=0), jnp.concatenate(dkl_rows, axis=0)
        dqkb_ref[:, 0:B_KEY_WIDTH] = ((dqm * e_q + dqc * e_c) * (B_KEY_DIM ** -0.5)).astype(_ST)
        dqkb_ref[:, B_KEY_WIDTH:2 * B_KEY_WIDTH] = (dkm * e_k + dkl * e_l).astype(_ST)
        dvb_ref[...] = (dv + jnp.concatenate(dv_rows, axis=0)).astype(_ST)
        dcum = dqm * qm - dkm * km + dqc * qc - dkl * kl
        row = lax.broadcasted_iota(jnp.int32, (tb, B_KEY_WIDTH), 0)
        dcum = jnp.where(row % B_CHUNK == B_CHUNK - 1, dcum + jnp.concatenate(dlast_rows, axis=0), dcum)
        dla = _dot_ones((same & (ri <= ci)).astype(_F32), dcum)

        da_pre = dla * (1.0 / B_GATE_TEMP) * (1.0 - _sigmoid(a_pre))
        dalr_ref[...] = _dot_nt(da_pre, wup_ref[...]).astype(_ST)
        dwup_ref[...] = dwup_ref[...] + _dot_tn(alr_ref[...], da_pre)
        db_ref[...] = db_ref[...] + jnp.sum(da_pre, axis=0, keepdims=True)

    blk = lambda s, i: s * nblk + nblk - 1 - i
    rows = lambda w: pl.BlockSpec((tb, w), lambda s, i: (blk(s, i), 0))
    full = lambda a, b: pl.BlockSpec((a, b), lambda s, i: (0, 0))
    act = lambda w: jax.ShapeDtypeStruct((t, w), _ST)
    return pl.pallas_call(
        body, name="gla_bwd", grid=(nseq, nblk),
        in_specs=[rows(512), rows(512), rows(512), rows(LANE), rows(512), rows(512),
                  pl.BlockSpec((cpb, B_KEY_WIDTH, B_VAL_DIM), lambda s, i: (blk(s, i), 0, 0)),
                  full(LANE, B_KEY_WIDTH), full(1, B_KEY_WIDTH), full(1, B_WIDTH)],
        out_specs=[rows(512), rows(512), rows(512), rows(LANE), full(LANE, B_KEY_WIDTH),
                   full(1, B_KEY_WIDTH), full(1, B_WIDTH)],
        out_shape=[act(512), act(512), act(512), act(LANE),
                   jax.ShapeDtypeStruct((LANE, B_KEY_WIDTH), _F32),
                   jax.ShapeDtypeStruct((1, B_KEY_WIDTH), _F32), jax.ShapeDtypeStruct((1, B_WIDTH), _F32)],
        scratch_shapes=[pltpu.VMEM((B_KEY_WIDTH, B_VAL_DIM), _F32)],
        compiler_params=_cparams(("arbitrary", "arbitrary")),
    )(qkb, vb, zb, alr, oraw, dob, sst, wup, b_alpha, gn)


def _merge_loss(oa, ob, ga, gb, x2, tgt, wa, wb, wo, g_final):
    t = x2.shape[0]
    tm = min(t, 256)
    nt = t // tm

    def body(oa_ref, ob_ref, ga_ref, gb_ref, x_ref, t_ref, wa_ref, wb_ref, wo_ref, gf_ref,
             dh_ref, doa_ref, dob_ref, dga_ref, dgb_ref, dwa_ref, dwb_ref, dwo_ref, dgf_ref, loss_ref):
        i = pl.program_id(0)

        @pl.when(i == 0)
        def _():
            dwa_ref[...] = jnp.zeros_like(dwa_ref)
            dwb_ref[...] = jnp.zeros_like(dwb_ref)
            dwo_ref[...] = jnp.zeros_like(dwo_ref)
            dgf_ref[...] = jnp.zeros_like(dgf_ref)
            loss_ref[...] = jnp.zeros_like(loss_ref)

        oa_v, ob_v = oa_ref[...], ob_ref[...]
        ya, yb = _dot(oa_v, wa_ref[...]), _dot(ob_v, wb_ref[...])
        sga, sgb = _sigmoid(ga_ref[...].astype(_F32)), _sigmoid(gb_ref[...].astype(_F32))
        merged = (sga * ya + sgb * yb).astype(_MX)
        out = x_ref[...] + _dot(merged, wo_ref[...])
        r = lax.rsqrt(jnp.mean(out * out, axis=-1, keepdims=True) + NORM_EPS)
        nrm = out * r
        gf = gf_ref[...]
        err = nrm * gf - t_ref[...]
        loss_ref[...] = loss_ref[...] + (0.5 / D_MODEL) * jnp.sum(err * err, axis=0, keepdims=True)
        dy = err * (1.0 / D_MODEL)
        dgf_ref[...] = dgf_ref[...] + jnp.sum(dy * nrm, axis=0, keepdims=True)
        dh = _rms_bwd(dy * gf, nrm, r)
        dh_ref[...] = dh
        dh_mx = dh.astype(_MX)
        dmer = _dot_nt(dh_mx, wo_ref[...])
        dwo_ref[...] = dwo_ref[...] + _dot_tn(merged, dh_mx)
        dya, dyb = (dmer * sga).astype(_MX), (dmer * sgb).astype(_MX)
        dga_ref[...] = (dmer * ya * sga * (1.0 - sga)).astype(_ST)
        dgb_ref[...] = (dmer * yb * sgb * (1.0 - sgb)).astype(_ST)
        doa_ref[...] = _dot_nt(dya, wa_ref[...]).astype(_ST)
        dob_ref[...] = _dot_nt(dyb, wb_ref[...]).astype(_ST)
        dwa_ref[...] = dwa_ref[...] + _dot_tn(oa_v, dya)
        dwb_ref[...] = dwb_ref[...] + _dot_tn(ob_v, dyb)

    rows = lambda w: pl.BlockSpec((tm, w), lambda i: (i, 0))
    full = lambda a, b: pl.BlockSpec((a, b), lambda i: (0, 0))
    return pl.pallas_call(
        body, name="merge_loss", grid=(nt,),
        in_specs=[rows(512), rows(512), rows(D_MODEL), rows(D_MODEL), rows(D_MODEL), rows(D_MODEL),
                  full(A_WIDTH, D_MODEL), full(B_WIDTH, D_MODEL), full(D_MODEL, D_MODEL), full(1, D_MODEL)],
        out_specs=[rows(D_MODEL), rows(512), rows(512), rows(D_MODEL), rows(D_MODEL),
                   full(A_WIDTH, D_MODEL), full(B_WIDTH, D_MODEL), full(D_MODEL, D_MODEL),
                   full(1, D_MODEL), full(1, D_MODEL)],
        out_shape=[jax.ShapeDtypeStruct((t, D_MODEL), _F32), jax.ShapeDtypeStruct((t, 512), _ST),
                   jax.ShapeDtypeStruct((t, 512), _ST), jax.ShapeDtypeStruct((t, D_MODEL), _ST),
                   jax.ShapeDtypeStruct((t, D_MODEL), _ST),
                   jax.ShapeDtypeStruct((A_WIDTH, D_MODEL), _F32), jax.ShapeDtypeStruct((B_WIDTH, D_MODEL), _F32),
                   jax.ShapeDtypeStruct((D_MODEL, D_MODEL), _F32), jax.ShapeDtypeStruct((1, D_MODEL), _F32),
                   jax.ShapeDtypeStruct((1, D_MODEL), _F32)],
        compiler_params=_cparams(("arbitrary",), VMEM_LIMIT),
    )(oa, ob, ga, gb, x2, tgt, wa, wb, wo, g_final)


def _in_proj_bwd_x(dpieces, wt, x2, dh2, g_in):
    t = x2.shape[0]
    tm = min(t, 256)
    np_ = len(PIECES)

    def body(*refs):
        dp_refs = refs[:np_]
        w_ref, x_ref, dh2_ref, g_ref, gx_ref, dg_ref = refs[np_:]

        @pl.when(pl.program_id(0) == 0)
        def _():
            dg_ref[...] = jnp.zeros_like(dg_ref)

        dh = None
        for (name, a, b), dp in zip(PIECES, dp_refs):
            part = _dot(dp[...], w_ref[a:b, :])
            dh = part if dh is None else dh + part
        xv = x_ref[...]
        r = lax.rsqrt(jnp.mean(xv * xv, axis=-1, keepdims=True) + NORM_EPS)
        nrm = xv * r
        dg_ref[...] = dg_ref[...] + jnp.sum(dh * nrm, axis=0, keepdims=True)
        gx_ref[...] = dh2_ref[...] + _rms_bwd(dh * g_ref[...], nrm, r)

    rows = lambda w: pl.BlockSpec((tm, w), lambda i: (i, 0))
    full = lambda a, b: pl.BlockSpec((a, b), lambda i: (0, 0))
    return pl.pallas_call(
        body, name="in_proj_bwd_x", grid=(t // tm,),
        in_specs=[rows(b - a) for _, a, b in PIECES] + [full(D_IN, D_MODEL), rows(D_MODEL), rows(D_MODEL),
                                                          full(1, D_MODEL)],
        out_specs=[rows(D_MODEL), full(1, D_MODEL)],
        out_shape=[jax.ShapeDtypeStruct((t, D_MODEL), _F32), jax.ShapeDtypeStruct((1, D_MODEL), _F32)],
        compiler_params=_cparams(("arbitrary",), VMEM_LIMIT),
    )(*dpieces, wt, x2, dh2, g_in)


def _in_proj_bwd_w(h, dpieces):
    t = h.shape[0]
    tm = min(t, 512)
    nt = t // tm
    np_ = len(PIECES)

    def body(*refs):
        h_ref, dp_refs, out_ref = refs[0], refs[1:1 + np_], refs[1 + np_]
        acc_ref, sem = refs[2 + np_:]
        i = pl.program_id(0)

        @pl.when(i == 0)
        def _():
            acc_ref[...] = jnp.zeros_like(acc_ref)

        hv = h_ref[...]
        writes = []
        for j, ((name, a, b), dp) in enumerate(zip(PIECES, dp_refs)):
            part = _dot_tn(dp[...], hv)
            if name == "alr":
                b = a + B_GATE_RANK
                part = part[0:B_GATE_RANK]
            acc_ref[a:b, :] = acc_ref[a:b, :] + part
            writes.append(pltpu.make_async_copy(acc_ref.at[a:b], out_ref.at[a:b], sem.at[j]))

            @pl.when(i == nt - 1)
            def _(cp=writes[-1]):
                cp.start()

        @pl.when(i == nt - 1)
        def _():
            for cp in writes:
                cp.wait()

    rows = lambda w: pl.BlockSpec((tm, w), lambda i: (i, 0))
    return pl.pallas_call(
        body, name="in_proj_bwd_w", grid=(nt,),
        in_specs=[rows(D_MODEL)] + [rows(b - a) for _, a, b in PIECES],
        out_specs=_ANY, out_shape=jax.ShapeDtypeStruct((D_IN, D_MODEL), _F32),
        scratch_shapes=[pltpu.VMEM((D_IN, D_MODEL), _F32), pltpu.SemaphoreType.DMA((np_,))],
        compiler_params=_cparams(("arbitrary",), VMEM_LIMIT),
    )(h, *dpieces)


def _place():
    return lax.axis_index("x"), lax.axis_index("y"), lax.axis_index("c")


def _other_chips(x, y):
    return [(1 - x, y), (x, 1 - y), (1 - x, 1 - y)]


class _Split(NamedTuple):
    by_rows: bool
    step: int
    size: int

    def half(self, ref, c):
        r, n = ref.shape[-2:]
        if self.by_rows:
            return ref.at[:, pl.ds(pl.multiple_of(c * (n // 2), LANE), n // 2)]
        return ref.at[pl.ds(pl.multiple_of(c * (r // 2), 16), r // 2), :]

    def chip_part(self, ref, k):
        if self.by_rows:
            return ref.at[pl.ds(pl.multiple_of(k * self.step, 16), self.size), :]
        return ref.at[:, pl.ds(pl.multiple_of(k * self.size, LANE), self.size)]

    def half_shape(self, shape):
        r, n = shape
        return (r, n // 2) if self.by_rows else (r // 2, n)

    def part_shape(self, shape):
        r, n = shape
        return (self.size, n) if self.by_rows else (r, self.size)


SPLIT_W_IN_T = _Split(True, WINDOW_STEP, WINDOW_ROWS)
SPLIT_W_O = _Split(True, 256, 256)
SPLIT_W_OUT = _Split(False, 256, 256)


def _gather_weights(shards, splits, fulls):
    nw = len(shards)

    def body(*refs):
        ins, outs = refs[:nw], refs[nw:2 * nw]
        send_a, recv_a, send_b, recv_b = refs[2 * nw:]
        x, y, c = _place()
        me = 2 * x + y
        peers = _other_chips(x, y)

        def place(i, k, half):
            if splits[i] is None:
                return outs[i].at[k]
            if fulls[i][0] == 4 and len(fulls[i]) == 3:
                whole = outs[i].at[k]
            else:
                whole = splits[i].chip_part(outs[i], k)
            return splits[i].half(whole, half)

        first, passed = [], []
        for i in range(nw):
            src = ins[i] if splits[i] is None else splits[i].half(ins[i], c)
            for j, (px, py) in enumerate(peers):
                cp = pltpu.make_async_remote_copy(
                    src_ref=src, dst_ref=place(i, me, c), send_sem=send_a.at[3 * i + j],
                    recv_sem=recv_a.at[3 * i + j], device_id=(px, py, c), device_id_type=_MESH)
                cp.start()
                first.append(cp)
        for i in range(nw):
            for j, (px, py) in enumerate(peers):
                landed = place(i, 2 * px + py, c)
                pltpu.make_async_remote_copy(
                    src_ref=landed, dst_ref=landed, send_sem=send_a.at[3 * i + j], recv_sem=recv_a.at[3 * i + j],
                    device_id=(px, py, c), device_id_type=_MESH).wait_recv()
                if splits[i] is not None:
                    cp = pltpu.make_async_remote_copy(
                        src_ref=landed, dst_ref=landed, send_sem=send_b.at[3 * i + j], recv_sem=recv_b.at[3 * i + j],
                        device_id=(x, y, 1 - c), device_id_type=_MESH)
                    cp.start()
                    passed.append(cp)
        for i in range(nw):
            if splits[i] is None:
                continue
            for j, (px, py) in enumerate(peers):
                theirs = place(i, 2 * px + py, 1 - c)
                pltpu.make_async_remote_copy(
                    src_ref=theirs, dst_ref=theirs, send_sem=send_b.at[3 * i + j], recv_sem=recv_b.at[3 * i + j],
                    device_id=(x, y, 1 - c), device_id_type=_MESH).wait_recv()
        for cp in first + passed:
            cp.wait_send()

    return pl.pallas_call(
        body, name="gather_weights",
        in_specs=[_ANY] * nw, out_specs=[_ANY] * nw,
        out_shape=[jax.ShapeDtypeStruct(f, s.dtype) for f, s in zip(fulls, shards)],
        scratch_shapes=[pltpu.SemaphoreType.DMA((3 * nw,)) for _ in range(4)],
    )(*shards)


def _assemble_w_in_t(slots):
    bw = 256
    ov = WINDOW_ROWS - WINDOW_STEP

    def body(s_ref, o_ref):
        for k in range(4):
            base = k * WINDOW_STEP
            lo = 0 if k == 0 else ov
            if k > 0:
                o_ref[base:base + ov, :] = s_ref[k - 1, WINDOW_STEP:WINDOW_ROWS, :] + s_ref[k, 0:ov, :]
            hi = WINDOW_ROWS if k == 3 else WINDOW_STEP
            o_ref[base + lo:base + hi, :] = s_ref[k, lo:hi, :]

    return pl.pallas_call(
        body, name="assemble_w_in_t", grid=(D_MODEL // bw,),
        in_specs=[pl.BlockSpec((4, WINDOW_ROWS, bw), lambda i: (0, 0, i))],
        out_specs=pl.BlockSpec((D_IN, bw), lambda i: (0, i)),
        out_shape=jax.ShapeDtypeStruct((D_IN, D_MODEL), slots.dtype),
        compiler_params=_cparams(("parallel",)),
    )(slots)


def _pair_exchange(grads, splits):
    nw = len(grads)

    def body(*refs):
        ins, outs = refs[:nw], refs[nw:2 * nw]
        send, recv = refs[2 * nw:]
        x, y, c = _place()
        copies = []
        for i in range(nw):
            cp = pltpu.make_async_remote_copy(
                src_ref=splits[i].half(ins[i], 1 - c), dst_ref=outs[i], send_sem=send.at[i], recv_sem=recv.at[i],
                device_id=(x, y, 1 - c), device_id_type=_MESH)
            cp.start()
            copies.append(cp)
        for cp in copies:
            cp.wait()

    return pl.pallas_call(
        body, name="grad_pair_exchange",
        in_specs=[_ANY] * nw, out_specs=[_ANY] * nw,
        out_shape=[jax.ShapeDtypeStruct(sp.half_shape(g.shape), g.dtype) for g, sp in zip(grads, splits)],
        scratch_shapes=[pltpu.SemaphoreType.DMA((nw,)), pltpu.SemaphoreType.DMA((nw,))],
    )(*grads)


def _row_block(rows):
    for cand in (976, 176, 256, 128):
        if rows % cand == 0:
            return cand
    return rows


def _pair_sum(g, r, split, c_arr, name):
    hr, hn = r.shape
    br = _row_block(hr)
    if split.by_rows:
        g_spec = pl.BlockSpec((br, hn), lambda i, c_ref: (i, c_ref[0]))
    else:
        g_spec = pl.BlockSpec((br, hn), lambda i, c_ref: (c_ref[0] * (hr // br) + i, 0))

    def body(c_ref, g_ref, r_ref, o_ref):
        o_ref[...] = (g_ref[...] + r_ref[...]).astype(o_ref.dtype)

    return pl.pallas_call(
        body, name=name,
        grid_spec=pltpu.PrefetchScalarGridSpec(
            num_scalar_prefetch=1, grid=(hr // br,),
            in_specs=[g_spec, pl.BlockSpec((br, hn), lambda i, c_ref: (i, 0))],
            out_specs=pl.BlockSpec((br, hn), lambda i, c_ref: (i, 0))),
        out_shape=jax.ShapeDtypeStruct(r.shape, _MX),
        compiler_params=_cparams(("parallel",)),
    )(c_arr, g, r)


def _chip_exchange(parts, splits):
    nw = len(parts)

    def body(*refs):
        ins, outs = refs[:nw], refs[nw:2 * nw]
        send, recv, lsem = refs[2 * nw:]
        x, y, c = _place()
        me = 2 * x + y
        copies = []
        for i in range(nw):
            mine = pltpu.make_async_copy(splits[i].chip_part(ins[i], me), outs[i].at[me], lsem.at[i])
            mine.start()
            copies.append(mine)
            for j, (px, py) in enumerate(_other_chips(x, y)):
                cp = pltpu.make_async_remote_copy(
                    src_ref=splits[i].chip_part(ins[i], 2 * px + py), dst_ref=outs[i].at[me],
                    send_sem=send.at[3 * i + j], recv_sem=recv.at[3 * i + j],
                    device_id=(px, py, c), device_id_type=_MESH)
                cp.start()
                copies.append(cp)
        for cp in copies:
            cp.wait()

    return pl.pallas_call(
        body, name="grad_chip_exchange",
        in_specs=[_ANY] * nw, out_specs=[_ANY] * nw,
        out_shape=[jax.ShapeDtypeStruct((4,) + sp.part_shape(p.shape), p.dtype) for p, sp in zip(parts, splits)],
        scratch_shapes=[pltpu.SemaphoreType.DMA((3 * nw,)), pltpu.SemaphoreType.DMA((3 * nw,)),
                        pltpu.SemaphoreType.DMA((nw,))],
    )(*parts)


def _sum_chips(q, split, c_arr, name):
    _, hr, hn = q.shape
    br = _row_block(hr)
    if split.by_rows:
        out_shape = (hr, 2 * hn)
        o_spec = pl.BlockSpec((br, hn), lambda i, c_ref: (i, c_ref[0]))
    else:
        out_shape = (2 * hr, hn)
        o_spec = pl.BlockSpec((br, hn), lambda i, c_ref: (c_ref[0] * (hr // br) + i, 0))

    def body(c_ref, q_ref, o_ref):
        f = lambda k: q_ref[k].astype(_F32)
        o_ref[...] = ((f(0) + f(1)) + f(2)) + f(3)

    return pl.pallas_call(
        body, name=name,
        grid_spec=pltpu.PrefetchScalarGridSpec(
            num_scalar_prefetch=1, grid=(hr // br,),
            in_specs=[pl.BlockSpec((4, br, hn), lambda i, c_ref: (0, i, 0))], out_specs=o_spec),
        out_shape=jax.ShapeDtypeStruct(out_shape, _F32),
        compiler_params=_cparams(("parallel",)),
    )(c_arr, q)


def _pair_share(bufs, splits):
    nw = len(bufs)

    def body(*refs):
        ins, outs = refs[:nw], refs[nw:2 * nw]
        send, recv = refs[2 * nw:]
        x, y, c = _place()
        copies = []
        for i in range(nw):
            cp = pltpu.make_async_remote_copy(
                src_ref=splits[i].half(ins[i], c), dst_ref=splits[i].half(outs[i], c), send_sem=send.at[i],
                recv_sem=recv.at[i], device_id=(x, y, 1 - c), device_id_type=_MESH)
            cp.start()
            copies.append(cp)
        for cp in copies:
            cp.wait()

    return pl.pallas_call(
        body, name="grad_pair_share",
        in_specs=[_ANY] * nw, out_specs=[_ANY] * nw,
        out_shape=[jax.ShapeDtypeStruct(b.shape, b.dtype) for b in bufs],
        input_output_aliases={i: i for i in range(nw)},
        scratch_shapes=[pltpu.SemaphoreType.DMA((nw,)), pltpu.SemaphoreType.DMA((nw,))],
    )(*bufs)


def _all_sum_small(v):
    n = v.shape[1]

    def body(v_ref, tot_ref, all_ref, send, recv):
        x, y, c = _place()
        me = 4 * x + 2 * y + c
        all_ref[me] = v_ref[...]
        copies = []
        for r in range(1, 8):
            px = 1 - x if r & 4 else x
            py = 1 - y if r & 2 else y
            pc = 1 - c if r & 1 else c
            cp = pltpu.make_async_remote_copy(
                src_ref=v_ref, dst_ref=all_ref.at[me], send_sem=send.at[r - 1], recv_sem=recv.at[r - 1],
                device_id=(px, py, pc), device_id_type=_MESH)
            cp.start()
            copies.append(cp)
        for cp in copies:
            cp.wait()
        acc = all_ref[0]
        for d in range(1, 8):
            acc = acc + all_ref[d]
        tot_ref[...] = acc

    vm = pl.BlockSpec(memory_space=pltpu.VMEM)
    return pl.pallas_call(
        body, name="small_all_sum",
        in_specs=[vm], out_specs=[vm, vm],
        out_shape=[jax.ShapeDtypeStruct((8, n), _F32), jax.ShapeDtypeStruct((8, 8, n), _F32)],
        scratch_shapes=[pltpu.SemaphoreType.DMA((7,)), pltpu.SemaphoreType.DMA((7,))],
    )(v)[0]


def _adamw(w, g, m, v, name):
    lead = w.shape[0] != 1
    r, n = (w.shape[0], w.shape[2]) if lead else w.shape[1:]
    br = r
    for cand in (256, 244, 128):
        if r > cand and r % cand == 0:
            br = cand
            break

    def body(w_ref, g_ref, m_ref, v_ref, d_ref, nm_ref, nv_ref):
        gv = g_ref[...]
        m2 = ADAM_B1 * m_ref[...] + (1.0 - ADAM_B1) * gv
        v2 = ADAM_B2 * v_ref[...] + (1.0 - ADAM_B2) * (gv * gv)
        m_hat = m2 / (1.0 - ADAM_B1 ** ADAM_STEP)
        v_hat = v2 / (1.0 - ADAM_B2 ** ADAM_STEP)
        d_ref[...] = -ADAM_LR * (m_hat / (jnp.sqrt(v_hat) + ADAM_EPS) + ADAM_WD * w_ref[...])
        nm_ref[...] = m2
        nv_ref[...] = v2

    blk = pl.BlockSpec((br, 1, n), lambda i: (i, 0, 0)) if lead else pl.BlockSpec((None, br, n), lambda i: (0, i, 0))
    shp = jax.ShapeDtypeStruct(w.shape, _F32)
    return pl.pallas_call(
        body, name=name, grid=(r // br,),
        in_specs=[blk] * 4, out_specs=[blk] * 3, out_shape=[shp] * 3,
        compiler_params=_cparams(("parallel",)),
    )(w, g, m, v)


def kernel(x, positions, g_in, w_in, w_alpha_up, b_alpha, attn_sinks, g_gla_norm, w_out_a, w_out_b, w_o, g_final, loss_target, m_g_in, m_w_in, m_w_alpha_up, m_b_alpha, m_attn_sinks, m_g_gla_norm, m_w_out_a, m_w_out_b, m_w_o, m_g_final, v_g_in, v_w_in, v_w_alpha_up, v_b_alpha, v_attn_sinks, v_g_gla_norm, v_w_out_a, v_w_out_b, v_w_o, v_g_final):
    nseq, seq, _ = x.shape
    t = nseq * seq
    cx, cy, cc = _place()
    chip = 2 * cx + cy
    c_arr = jnp.reshape(cc, (1,)).astype(jnp.int32)

    tr = lambda w: jnp.transpose(w, (2, 0, 1))
    w_in_t = tr(w_in).reshape(SHARD, D_MODEL).astype(_MX)
    window = lax.dynamic_update_slice(jnp.zeros((WINDOW_ROWS, D_MODEL), _MX), w_in_t, (4 * chip, 0))
    shards = [window, w_out_a[0].astype(_MX), w_out_b[0].astype(_MX), w_o[0].astype(_MX), w_alpha_up[0].astype(_MX)]
    splits = [SPLIT_W_IN_T, SPLIT_W_OUT, SPLIT_W_OUT, SPLIT_W_O, None]
    fulls = [(4, WINDOW_ROWS, D_MODEL), (A_WIDTH, D_MODEL), (B_WIDTH, D_MODEL), (D_MODEL, D_MODEL),
             (4, B_GATE_RANK, B_KEY_WIDTH // 4)]
    win_g, wa, wb, wo, wup_g = _gather_weights(shards, splits, fulls)
    nsh = D_MODEL // 4
    win_g = lax.dynamic_update_slice(win_g, window[None], (chip, 0, 0))
    wa = lax.dynamic_update_slice(wa, shards[1], (0, nsh * chip))
    wb = lax.dynamic_update_slice(wb, shards[2], (0, nsh * chip))
    wo = lax.dynamic_update_slice(wo, shards[3], (nsh * chip, 0))
    wup_g = lax.dynamic_update_slice(wup_g, shards[4][None], (chip, 0, 0))
    wt = _assemble_w_in_t(win_g)
    wup = jnp.concatenate([jnp.transpose(wup_g, (1, 0, 2)).reshape(B_GATE_RANK, B_KEY_WIDTH),
                           jnp.zeros((LANE - B_GATE_RANK, B_KEY_WIDTH), _MX)], axis=0)

    x2 = x.reshape(t, D_MODEL)
    tgt = loss_target.reshape(t, D_MODEL)
    pos_f = positions.astype(_F32).reshape(t, 1)
    sinks = attn_sinks.reshape(A_HEADS)
    gf = g_final.reshape(1, D_MODEL)

    cos, sa, sb = _rope_tables(pos_f)
    h, qkv, za, qkb, vb, zb, alr, ga, gb = _in_proj(x2, g_in, wt, cos, sa, sb)
    oa = _attn_fwd(qkv, za, sinks, nseq)
    ob, oraw, sst = _gla_fwd(qkb, vb, zb, alr, wup, b_alpha, g_gla_norm, nseq)

    dh2, doa, dob, dga, dgb, dwa, dwb, dwo, dgf, lossv = _merge_loss(oa, ob, ga, gb, x2, tgt, wa, wb, wo, gf)

    dqkv, dza, dsink = _attn_bwd(qkv, za, doa, sinks, cos, sa, sb, nseq)
    dqkb, dvb, dzb, dalr, dwup, dba, dgn = _gla_bwd(qkb, vb, zb, alr, oraw, dob, sst, wup, b_alpha, g_gla_norm, nseq)
    dpieces = [dqkv, dza, dqkb, dvb, dzb, dalr, dga, dgb]
    grad_x2, dgin = _in_proj_bwd_x(dpieces, wt, x2, dh2, g_in)
    dwin_t = _in_proj_bwd_w(h, dpieces)

    grads = [dwin_t, dwa, dwb, dwo]
    gsplits = [SPLIT_W_IN_T, SPLIT_W_OUT, SPLIT_W_OUT, SPLIT_W_O]
    names = ("w_in", "w_out_a", "w_out_b", "w_o")
    from_sibling = _pair_exchange(grads, gsplits)
    pair_sums = [_pair_sum(g, r, sp, c_arr, "pair_sum_" + nm)
                 for g, r, sp, nm in zip(grads, from_sibling, gsplits, names)]
    from_chips = _chip_exchange(pair_sums, gsplits)
    reduced = [_sum_chips(q, sp, c_arr, "chip_sum_" + nm) for q, sp, nm in zip(from_chips, gsplits, names)]
    g_window, g_wa, g_wb, g_wo = _pair_share(reduced, gsplits)
    g_win_t = lax.dynamic_slice(g_window, (4 * chip, 0), (SHARD, D_MODEL)).reshape(SHARD, 1, D_MODEL)

    small = jnp.concatenate([
        dgin, dgf, dgn, dba,
        jnp.pad(dsink[:, 0].reshape(1, A_HEADS), ((0, 0), (0, LANE - A_HEADS))),
        jnp.pad(jnp.sum(lossv, axis=1, keepdims=True), ((0, 0), (0, LANE - 1))),
        dwup[:B_GATE_RANK].reshape(1, B_GATE_RANK * B_KEY_WIDTH)], axis=1)
    tot = _all_sum_small(jnp.pad(small, ((0, 7), (0, 0))))[0:1]
    o = 0
    def take(n):
        nonlocal o
        o += n
        return tot[:, o - n:o]
    g_gin, g_gf, g_gn, g_ba = take(D_MODEL), take(D_MODEL), take(B_WIDTH), take(B_KEY_WIDTH)
    g_sink = take(LANE)[:, :A_HEADS]
    loss = take(LANE)[0, 0]
    g_wup_full = take(B_GATE_RANK * B_KEY_WIDTH).reshape(B_GATE_RANK, B_KEY_WIDTH)
    nup = B_KEY_WIDTH // 4
    g_wup = lax.dynamic_slice(g_wup_full, (0, chip * nup), (B_GATE_RANK, nup))

    def pack(*parts):
        return jnp.concatenate([p.reshape(1, -1) for p in parts], axis=1)

    sm_w = pack(g_in, g_final, g_gla_norm, b_alpha, attn_sinks, w_alpha_up)
    sm_g = pack(g_gin, g_gf, g_gn, g_ba, g_sink, g_wup)
    sm_m = pack(m_g_in, m_g_final, m_g_gla_norm, m_b_alpha, m_attn_sinks, m_w_alpha_up)
    sm_v = pack(v_g_in, v_g_final, v_g_gla_norm, v_b_alpha, v_attn_sinks, v_w_alpha_up)
    sm_out = [p[0] for p in _adamw(sm_w[None], sm_g[None], sm_m[None], sm_v[None], "adamw_small")]

    def unpack(p):
        sizes = (D_MODEL, D_MODEL, B_WIDTH, B_KEY_WIDTH, A_HEADS, B_GATE_RANK * nup)
        outs, at = [], 0
        for s in sizes:
            outs.append(p[:, at:at + s])
            at += s
        gi, gfin, gnn, ba, sk, wu = outs
        return dict(g_in=gi, g_final=gfin.reshape(D_MODEL), g_gla_norm=gnn, b_alpha=ba, attn_sinks=sk,
                    w_alpha_up=wu.reshape(1, B_GATE_RANK, nup))

    untr = lambda a: jnp.transpose(a, (1, 2, 0))
    big = dict(w_in=tuple(untr(a) for a in (g_win_t,) + tuple(_adamw(tr(w_in), g_win_t, tr(m_w_in), tr(v_w_in), "adamw_w_in"))))
    for nm, w, g, m, v in (("w_out_a", w_out_a, g_wa, m_w_out_a, v_w_out_a),
                           ("w_out_b", w_out_b, g_wb, m_w_out_b, v_w_out_b), ("w_o", w_o, g_wo, m_w_o, v_w_o)):
        big[nm] = (g[None],) + tuple(_adamw(w, g[None], m, v, "adamw_" + nm))

    order = ("g_in", "w_in", "w_alpha_up", "b_alpha", "attn_sinks", "g_gla_norm", "w_out_a", "w_out_b", "w_o", "g_final")
    small_sets = [unpack(sm_g)] + [unpack(p) for p in sm_out]
    outs = []
    for kind in range(4):
        for nm in order:
            outs.append(big[nm][kind] if nm in big else small_sets[kind][nm])
    return (loss, grad_x2.reshape(x.shape), *outs)
```

```python
import math
from typing import NamedTuple

import numpy as np
import jax
import jax.numpy as jnp
from jax import lax
from jax.experimental import pallas as pl
from jax.experimental.pallas import tpu as pltpu

D_MODEL = 1024
A_HEADS, A_KV_HEADS, A_HEAD_DIM = 8, 2, 64
A_GROUP = A_HEADS // A_KV_HEADS
A_WIDTH, A_KV_WIDTH = 512, 128
BLOCK = 128
ROPE_THETA = 500000.0
ROPE_DIM = 16
B_HEADS, B_KEY_DIM, B_VAL_DIM = 4, 64, 128
B_KEY_WIDTH, B_WIDTH = 256, 512
B_GATE_RANK = 16
B_GATE_TEMP = 16.0
B_CHUNK = 64
NORM_EPS = 1e-6
NEG_BIG = -1e30
D_IN = 4880

ADAM_LR, ADAM_B1, ADAM_B2, ADAM_EPS, ADAM_WD, ADAM_STEP = 0.001, 0.9, 0.999, 1e-08, 0.01, 10

LANE = 128
ALR_AT = 2816
PIECES = (("qkv", 0, 768), ("za", 768, 1280), ("qkb", 1280, 1792), ("vb", 1792, 2304),
          ("zb", 2304, 2816), ("alr", ALR_AT, ALR_AT + LANE), ("ga", 2832, 3856), ("gb", 3856, 4880))
SHARD = D_IN // 4
WINDOW_STEP = 1216
WINDOW_ROWS = 1232

GLA_BLOCK = 256
VMEM_LIMIT = 56 * 1024 * 1024

_F32 = jnp.float32
_MX = jnp.bfloat16
_ST = jnp.bfloat16

_MESH = pl.DeviceIdType.MESH
_ANY = pl.BlockSpec(memory_space=pl.ANY)


def _cparams(sem=None, vmem=None):
    return pltpu.CompilerParams(dimension_semantics=sem, vmem_limit_bytes=vmem)


def _dot(a, b):
    return jnp.dot(a.astype(_MX), b.astype(_MX), preferred_element_type=_F32)


def _dot_nt(a, b):
    return lax.dot_general(a.astype(_MX), b.astype(_MX), (((1,), (1,)), ((), ())),
                           preferred_element_type=_F32)


def _dot_tn(a, b):
    return lax.dot_general(a.astype(_MX), b.astype(_MX), (((0,), (0,)), ((), ())),
                           preferred_element_type=_F32)


def _dot_ones(ones_mat, v):
    o = ones_mat.astype(jnp.bfloat16)
    v0 = v.astype(jnp.bfloat16)
    r1 = v - v0.astype(_F32)
    v1 = r1.astype(jnp.bfloat16)
    v2 = (r1 - v1.astype(_F32)).astype(jnp.bfloat16)
    d = lambda t: jnp.dot(o, t, preferred_element_type=_F32)
    return d(v0) + d(v1) + d(v2)


def _sigmoid(x):
    return 1.0 / (1.0 + jnp.exp(-x))


def _log_sigmoid(x):
    return jnp.minimum(x, 0.0) - jnp.log(1.0 + jnp.exp(-jnp.abs(x)))


def _lane_tile(t, width):
    reps = width // t.shape[1]
    return t if reps == 1 else jnp.tile(t, (1, reps))


def _rope(t, cos, sa, sb, sign):
    w = t.shape[1]
    rot = pltpu.roll(t, w - 8, 1) * _lane_tile(sa, w) + pltpu.roll(t, 8, 1) * _lane_tile(sb, w)
    return t * _lane_tile(cos, w) + sign * rot


def _rms_bwd(dy_g, n, r):
    return r * (dy_g - n * jnp.mean(dy_g * n, axis=-1, keepdims=True))


ROPE_ROWS = 256


def _rope_consts():
    lane = np.arange(LANE) % A_HEAD_DIM
    half = ROPE_DIM // 2
    inv = np.exp((np.float32(-math.log(ROPE_THETA)) * np.arange(half, dtype=np.float32)) * np.float32(2.0 / ROPE_DIM))
    consts = np.zeros((8, LANE), np.float32)
    consts[0] = np.where(lane < ROPE_DIM, inv[lane % half], 0.0)
    consts[1] = np.where(lane < half, -1.0, 0.0)
    consts[2] = np.where((lane >= half) & (lane < ROPE_DIM), 1.0, 0.0)
    return jnp.asarray(consts)


def _rope_tables_into(pos_ref, c_ref, cos_ref, sa_ref, sb_ref):
    def rows_of(b, carry):
        rows = pl.ds(pl.multiple_of(b * ROPE_ROWS, ROPE_ROWS), ROPE_ROWS)
        ang = pos_ref[rows, :] * c_ref[0:1, :]
        s = jnp.sin(ang)
        cos_ref[rows, :] = jnp.cos(ang)
        sa_ref[rows, :] = s * c_ref[1:2, :]
        sb_ref[rows, :] = s * c_ref[2:3, :]
        return carry

    lax.fori_loop(0, pos_ref.shape[0] // ROPE_ROWS, rows_of, 0)


def _in_proj(x2, g_in, wt, cos, sa, sb):
    t = x2.shape[0]
    tm = min(t, 512)

    def body(x_ref, g_ref, w_ref, cos_ref, sa_ref, sb_ref, h_ref, qkv_ref, za_ref, qkb_ref,
             vb_ref, zb_ref, alr_ref, ga_ref, gb_ref):
        xv = x_ref[...]
        r = lax.rsqrt(jnp.mean(xv * xv, axis=-1, keepdims=True) + NORM_EPS)
        h = (xv * r * g_ref[...]).astype(_MX)
        h_ref[...] = h.astype(_ST)
        outs = dict(za=za_ref, qkb=qkb_ref, vb=vb_ref, zb=zb_ref, alr=alr_ref, ga=ga_ref, gb=gb_ref)
        for name, a, b in PIECES:
            p = _dot_nt(h, w_ref[a:b, :])
            if name == "qkv":
                c, s1, s2 = cos_ref[...], sa_ref[...], sb_ref[...]
                qkv_ref[:, 0:512] = _rope(p[:, 0:512], c, s1, s2, 1.0).astype(_ST)
                qkv_ref[:, 512:640] = _rope(p[:, 512:640], c, s1, s2, 1.0).astype(_ST)
                qkv_ref[:, 640:768] = p[:, 640:768].astype(_ST)
            else:
                outs[name][...] = p.astype(outs[name].dtype)

    rows = lambda w: pl.BlockSpec((tm, w), lambda i: (i, 0))
    shp = lambda name, w: jax.ShapeDtypeStruct((t, w), _F32 if name == "qkb" else _ST)
    widths = [D_MODEL] + [b - a for _, a, b in PIECES]
    return pl.pallas_call(
        body, name="in_proj", grid=(t // tm,),
        in_specs=[rows(D_MODEL), pl.BlockSpec((1, D_MODEL), lambda i: (0, 0)),
                  pl.BlockSpec((D_IN, D_MODEL), lambda i: (0, 0)), rows(LANE), rows(LANE), rows(LANE)],
        out_specs=[rows(w) for w in widths],
        out_shape=[shp(n, w) for n, w in zip(["h"] + [p[0] for p in PIECES], widths)],
        compiler_params=_cparams(("parallel",), VMEM_LIMIT),
    )(x2, g_in, wt, cos, sa, sb)


def _attn_operands(qkv_ref, kvp_ref, want_bwd):
    kf = jnp.concatenate([kvp_ref[:, 0:128], qkv_ref[:, 512:640]], axis=0).astype(_F32) * (A_HEAD_DIM ** -0.5)
    vf = jnp.concatenate([kvp_ref[:, 128:256], qkv_ref[:, 640:768]], axis=0).astype(_F32)
    lo = lax.broadcasted_iota(jnp.int32, (1, LANE), 1) < 64

    def on_lanes(a):
        sw = pltpu.roll(a, 64, 1)
        z = jnp.zeros_like(a)
        return [[jnp.where(lo, a, z).astype(_MX), jnp.where(lo, z, sw).astype(_MX)],
                [jnp.where(lo, sw, z).astype(_MX), jnp.where(lo, z, a).astype(_MX)]]

    def on_rows(a):
        at = a.T.astype(_MX)
        z = jnp.zeros((64, at.shape[1]), _MX)
        top, bot = at[0:64], at[64:128]
        return [[jnp.concatenate([top, z], axis=0), jnp.concatenate([z, top], axis=0)],
                [jnp.concatenate([bot, z], axis=0), jnp.concatenate([z, bot], axis=0)]]

    ops = dict(k_lanes=on_lanes(kf), v_rows=on_rows(vf), lo=lo)
    if want_bwd:
        ops.update(v_lanes=on_lanes(vf), k_rows=on_rows(kf))
    return ops


def _attn_valid(n):
    kj = lax.broadcasted_iota(jnp.int32, (2 * BLOCK, 2 * BLOCK), 0) - BLOCK
    qi = lax.broadcasted_iota(jnp.int32, (2 * BLOCK, 2 * BLOCK), 1) & (BLOCK - 1)
    return (kj <= qi) & (qi - kj < BLOCK) & ((n > 0) | (kj >= 0))


def _attn_sinks(sink_ref, h_a, h_b):
    first = lax.broadcasted_iota(jnp.int32, (1, 2 * BLOCK), 1) < BLOCK
    return jnp.where(first, sink_ref[h_a], sink_ref[h_b])


def _attn_softmax_t(k_lanes, q_pair, valid, sink):
    s = jnp.where(valid, _dot_nt(k_lanes, q_pair), NEG_BIG)
    m = jnp.maximum(jnp.max(s, axis=0, keepdims=True), sink)
    e = jnp.exp(s - m)
    e_sink = jnp.exp(sink - m)
    inv = 1.0 / (jnp.sum(e, axis=0, keepdims=True) + e_sink)
    return e, e_sink, inv


def _attn_fwd(qkv, za, sinks, nseq):
    t = qkv.shape[0]
    nb = t // nseq // BLOCK

    def body(sink_ref, qkv_ref, kvp_ref, za_ref, oa_ref):
        n = pl.program_id(1)
        ops = _attn_operands(qkv_ref, kvp_ref, False)
        valid = _attn_valid(n)[:, 0:BLOCK]
        for pr in range(A_HEADS // 2):
            lanes = slice(pr * LANE, (pr + 1) * LANE)
            g = pr // (A_GROUP // 2)
            q_pair = qkv_ref[:, lanes]
            ot = None
            for half in range(2):
                e, _, inv = _attn_softmax_t(ops["k_lanes"][g][half], q_pair, valid, sink_ref[2 * pr + half])
                part = _dot(ops["v_rows"][g][half], e) * inv
                ot = part if ot is None else ot + part
            z = za_ref[:, lanes].astype(_F32)
            oa_ref[:, lanes] = (ot.T * (z * _sigmoid(z))).astype(_ST)

    cur = lambda w: pl.BlockSpec((BLOCK, w), lambda s, n: (s * nb + n, 0))
    return pl.pallas_call(
        body, name="attn_fwd", grid=(nseq, nb),
        in_specs=[pl.BlockSpec(memory_space=pltpu.SMEM), cur(768),
                  pl.BlockSpec((BLOCK, 256), lambda s, n: (s * nb + jnp.maximum(n - 1, 0), 2)), cur(512)],
        out_specs=cur(512), out_shape=jax.ShapeDtypeStruct((t, A_WIDTH), _ST),
        compiler_params=_cparams(("parallel", "arbitrary")),
    )(sinks, qkv, qkv, za)


def _attn_bwd(qkv, za, doa, sinks, cos, sa, sb, nseq):
    t = qkv.shape[0]
    nb = t // nseq // BLOCK

    def body(sink_ref, qkv_ref, kvp_ref, za_ref, doa_ref, cos_ref, sa_ref, sb_ref,
             dqkv_ref, dza_ref, dsink_ref, ck_ref, cv_ref):
        s_id, i = pl.program_id(0), pl.program_id(1)
        n = nb - 1 - i

        @pl.when((s_id == 0) & (i == 0))
        def _():
            dsink_ref[...] = jnp.zeros_like(dsink_ref)

        @pl.when(i == 0)
        def _():
            ck_ref[...] = jnp.zeros_like(ck_ref)
            cv_ref[...] = jnp.zeros_like(cv_ref)

        ops = _attn_operands(qkv_ref, kvp_ref, True)
        lo = ops["lo"]
        valid = _attn_valid(n)
        dk_acc, dv_acc, dq_pairs = [], [], []
        for g in range(A_KV_HEADS):
            pairs = [slice((2 * g + j) * LANE, (2 * g + j + 1) * LANE) for j in range(2)]
            q_both = jnp.concatenate([qkv_ref[:, p] for p in pairs], axis=0)
            q_f = q_both.astype(_F32)
            z = [za_ref[:, p].astype(_F32) for p in pairs]
            sz = [_sigmoid(t) for t in z]
            d_oa = [doa_ref[:, p].astype(_F32) for p in pairs]
            d_att = jnp.concatenate([d_oa[j] * (z[j] * sz[j]) for j in range(2)], axis=0)
            zero = jnp.zeros_like(d_att)
            ot, dqt, ds_all, pn_all, qz_all, daz_all = None, None, [], [], [], []
            for half in range(2):
                heads = (4 * g + half, 4 * g + 2 + half)
                e, e_sink, inv = _attn_softmax_t(ops["k_lanes"][g][half], q_both, valid,
                                                 _attn_sinks(sink_ref, *heads))
                pn = e * inv
                dpt = _dot_nt(ops["v_lanes"][g][half], d_att)
                delta = jnp.sum(pn * dpt, axis=0, keepdims=True)
                ds = (pn * (dpt - delta)).astype(_MX)
                pn = pn.astype(_MX)
                d_sink = e_sink * inv * delta
                for j, h in enumerate(heads):
                    dsink_ref[h:h + 1, :] = dsink_ref[h:h + 1, :] - jnp.sum(d_sink[:, j * BLOCK:(j + 1) * BLOCK])
                o_part = _dot(ops["v_rows"][g][half], pn)
                dq_part = _dot(ops["k_rows"][g][half], ds)
                ot = o_part if ot is None else ot + o_part
                dqt = dq_part if dqt is None else dqt + dq_part
                mine = lo if half == 0 else jnp.logical_not(lo)
                ds_all.append(ds)
                pn_all.append(pn)
                qz_all.append(jnp.where(mine, q_f, zero).astype(_MX))
                daz_all.append(jnp.where(mine, d_att, zero).astype(_MX))
            dk_acc.append(_dot(jnp.concatenate(ds_all, axis=1), jnp.concatenate(qz_all, axis=0)))
            dv_acc.append(_dot(jnp.concatenate(pn_all, axis=1), jnp.concatenate(daz_all, axis=0)))
            for j, p in enumerate(pairs):
                cols = slice(j * BLOCK, (j + 1) * BLOCK)
                dza_ref[:, p] = (d_oa[j] * ot[:, cols].T * (sz[j] * (1.0 + z[j] * (1.0 - sz[j])))).astype(_ST)
                dq_pairs.append(dqt[:, cols].T)

        def fold(acc, scale):
            both = [a + pltpu.roll(a, 64, 1) for a in acc]
            return jnp.where(lo, both[0], both[1]) * scale

        dk_full = fold(dk_acc, A_HEAD_DIM ** -0.5)
        dv_full = fold(dv_acc, 1.0)
        dk_cur = dk_full[BLOCK:] + ck_ref[...]
        dv_cur = dv_full[BLOCK:] + cv_ref[...]
        ck_ref[...] = dk_full[:BLOCK]
        cv_ref[...] = dv_full[:BLOCK]
        c, s1, s2 = cos_ref[...], sa_ref[...], sb_ref[...]
        dqkv_ref[:, 0:512] = _rope(jnp.concatenate(dq_pairs, axis=1), c, s1, s2, -1.0).astype(_ST)
        dqkv_ref[:, 512:640] = _rope(dk_cur, c, s1, s2, -1.0).astype(_ST)
        dqkv_ref[:, 640:768] = dv_cur.astype(_ST)

    cur = lambda w: pl.BlockSpec((BLOCK, w), lambda s, i: (s * nb + nb - 1 - i, 0))
    return pl.pallas_call(
        body, name="attn_bwd", grid=(nseq, nb),
        in_specs=[pl.BlockSpec(memory_space=pltpu.SMEM), cur(768),
                  pl.BlockSpec((BLOCK, 256), lambda s, i: (s * nb + jnp.maximum(nb - 2 - i, 0), 2)),
                  cur(512), cur(512), cur(LANE), cur(LANE), cur(LANE)],
        out_specs=[cur(768), cur(512), pl.BlockSpec((8, LANE), lambda s, i: (0, 0))],
        out_shape=[jax.ShapeDtypeStruct((t, 768), _ST), jax.ShapeDtypeStruct((t, 512), _ST),
                   jax.ShapeDtypeStruct((8, LANE), _F32)],
        scratch_shapes=[pltpu.VMEM((BLOCK, A_KV_WIDTH), _F32), pltpu.VMEM((BLOCK, A_KV_WIDTH), _F32)],
        compiler_params=_cparams(("arbitrary", "arbitrary")),
    )(sinks, qkv, qkv, za, doa, cos, sa, sb)


def _gla_chunk_terms(la, qkb_ref, r0):
    g = la[r0:r0 + B_CHUNK, :]
    ri = lax.broadcasted_iota(jnp.int32, (B_CHUNK, B_CHUNK), 0)
    ci = lax.broadcasted_iota(jnp.int32, (B_CHUNK, B_CHUNK), 1)
    cum = _dot_ones((ri >= ci).astype(_F32), g)
    last = cum[B_CHUNK - 1:B_CHUNK, :]
    mid = cum[B_CHUNK // 2 - 1:B_CHUNK // 2, :]
    q = qkb_ref[r0:r0 + B_CHUNK, 0:B_KEY_WIDTH].astype(_F32) * (B_KEY_DIM ** -0.5)
    k = qkb_ref[r0:r0 + B_CHUNK, B_KEY_WIDTH:2 * B_KEY_WIDTH].astype(_F32)
    e_q, e_k, e_l, e_c = jnp.exp(cum - mid), jnp.exp(mid - cum), jnp.exp(last - cum), jnp.exp(cum)
    dec_col = jnp.exp(jnp.sum(g.T, axis=1, keepdims=True))
    return dict(qm=q * e_q, km=k * e_k, kl=k * e_l, qc=q * e_c, e_q=e_q, e_k=e_k, e_l=e_l, e_c=e_c,
                dec_col=dec_col, dec_row=jnp.exp(last), causal=ri >= ci, ri=ri)


def _gate_logits(alr_ref, wup_ref, b_ref):
    return _dot(alr_ref[...], wup_ref[...]) + b_ref[...]


def _gla_fwd(qkb, vb, zb, alr, wup, b_alpha, gn, nseq):
    t = qkb.shape[0]
    tb = min(GLA_BLOCK, t // nseq)
    nblk = t // nseq // tb
    cpb = tb // B_CHUNK

    def body(qkb_ref, vb_ref, zb_ref, alr_ref, wup_ref, b_ref, gn_ref, ob_ref, oraw_ref, sst_ref, s_ref):
        @pl.when(pl.program_id(1) == 0)
        def _():
            s_ref[...] = jnp.zeros_like(s_ref)

        la = _log_sigmoid(_gate_logits(alr_ref, wup_ref, b_ref)) * (1.0 / B_GATE_TEMP)
        terms = [_gla_chunk_terms(la, qkb_ref, c * B_CHUNK) for c in range(cpb)]
        o_intra, inc = {}, {}
        for c, tm in enumerate(terms):
            for h in range(B_HEADS):
                kl_, vl_ = slice(h * 64, (h + 1) * 64), slice(h * 128, (h + 1) * 128)
                v = vb_ref[c * B_CHUNK:(c + 1) * B_CHUNK, vl_]
                a = jnp.where(tm["causal"], _dot_nt(tm["qm"][:, kl_], tm["km"][:, kl_]), 0.0)
                o_intra[c, h] = _dot(a, v)
                inc[c, h] = _dot_tn(tm["kl"][:, kl_], v)
        o_heads = {}
        for h in range(B_HEADS):
            kl_ = slice(h * 64, (h + 1) * 64)
            st = s_ref[kl_, :]
            for c, tm in enumerate(terms):
                sst_ref[c, kl_, :] = st
                o_heads[c, h] = o_intra[c, h] + _dot(tm["qc"][:, kl_], st)
                st = tm["dec_col"][kl_, :] * st + inc[c, h]
            s_ref[kl_, :] = st
        o = jnp.concatenate([jnp.concatenate([o_heads[c, h] for h in range(B_HEADS)], axis=1)
                             for c in range(cpb)], axis=0)
        oraw_ref[...] = o
        z = zb_ref[...].astype(_F32)
        gate = z * _sigmoid(z)
        for h in range(B_HEADS):
            vl_ = slice(h * 128, (h + 1) * 128)
            oh = o[:, vl_]
            r = lax.rsqrt(jnp.mean(oh * oh, axis=-1, keepdims=True) + NORM_EPS)
            ob_ref[:, vl_] = ((oh * r) * gn_ref[:, vl_] * gate[:, vl_]).astype(_ST)

    rows = lambda w: pl.BlockSpec((tb, w), lambda s, i: (s * nblk + i, 0))
    full = lambda a, b: pl.BlockSpec((a, b), lambda s, i: (0, 0))
    return pl.pallas_call(
        body, name="gla_fwd", grid=(nseq, nblk),
        in_specs=[rows(512), rows(512), rows(512), rows(LANE), full(LANE, B_KEY_WIDTH),
                  full(1, B_KEY_WIDTH), full(1, B_WIDTH)],
        out_specs=[rows(512), rows(512),
                   pl.BlockSpec((cpb, B_KEY_WIDTH, B_VAL_DIM), lambda s, i: (s * nblk + i, 0, 0))],
        out_shape=[jax.ShapeDtypeStruct((t, B_WIDTH), _ST), jax.ShapeDtypeStruct((t, B_WIDTH), _F32),
                   jax.ShapeDtypeStruct((t // B_CHUNK, B_KEY_WIDTH, B_VAL_DIM), _F32)],
        scratch_shapes=[pltpu.VMEM((B_KEY_WIDTH, B_VAL_DIM), _F32)],
        compiler_params=_cparams(("parallel", "arbitrary")),
    )(qkb, vb, zb, alr, wup, b_alpha, gn)


def _gla_bwd(qkb, vb, zb, alr, oraw, dob, sst, wup, b_alpha, gn, nseq):
    t = qkb.shape[0]
    tb = min(GLA_BLOCK, t // nseq)
    nblk = t // nseq // tb
    cpb = tb // B_CHUNK

    def body(qkb_ref, vb_ref, zb_ref, alr_ref, oraw_ref, dob_ref, sst_ref, wup_ref, b_ref, gn_ref,
             dqkb_ref, dvb_ref, dzb_ref, dalr_ref, dwup_ref, db_ref, dgn_ref, ds_ref):
        s_id, i = pl.program_id(0), pl.program_id(1)

        @pl.when((s_id == 0) & (i == 0))
        def _():
            dwup_ref[...] = jnp.zeros_like(dwup_ref)
            db_ref[...] = jnp.zeros_like(db_ref)
            dgn_ref[...] = jnp.zeros_like(dgn_ref)

        @pl.when(i == 0)
        def _():
            ds_ref[...] = jnp.zeros_like(ds_ref)

        a_pre = _gate_logits(alr_ref, wup_ref, b_ref)
        la = _log_sigmoid(a_pre) * (1.0 / B_GATE_TEMP)

        z = zb_ref[...].astype(_F32)
        sz = _sigmoid(z)
        d_ob = dob_ref[...].astype(_F32)
        tg = d_ob * (z * sz)
        dsilu = sz * (1.0 + z * (1.0 - sz))
        do_cols, dgn_cols = [], []
        for h in range(B_HEADS):
            vl_ = slice(h * 128, (h + 1) * 128)
            oh = oraw_ref[:, vl_].astype(_F32)
            r = lax.rsqrt(jnp.mean(oh * oh, axis=-1, keepdims=True) + NORM_EPS)
            on = oh * r
            gnh = gn_ref[:, vl_]
            dzb_ref[:, vl_] = (d_ob[:, vl_] * (on * gnh) * dsilu[:, vl_]).astype(_ST)
            dgn_cols.append(jnp.sum(tg[:, vl_] * on, axis=0, keepdims=True))
            do_cols.append(_rms_bwd(tg[:, vl_] * gnh, on, r))
        dgn_ref[...] = dgn_ref[...] + jnp.concatenate(dgn_cols, axis=1)
        d_o = jnp.concatenate(do_cols, axis=1)

        ri = lax.broadcasted_iota(jnp.int32, (tb, tb), 0)
        ci = lax.broadcasted_iota(jnp.int32, (tb, tb), 1)
        same = (ri // B_CHUNK) == (ci // B_CHUNK)
        low = same & (ri >= ci)
        upto_mid = same & ((ci % B_CHUNK) < B_CHUNK // 2)
        sums = _dot_ones(jnp.concatenate([m.astype(_F32) for m in (low, same, upto_mid)], axis=0), la)
        cum, last, mid = sums[0:tb], sums[tb:2 * tb], sums[2 * tb:3 * tb]
        e_q, e_k, e_l, e_c = jnp.exp(cum - mid), jnp.exp(mid - cum), jnp.exp(last - cum), jnp.exp(cum)
        q = qkb_ref[:, 0:B_KEY_WIDTH] * (B_KEY_DIM ** -0.5)
        k = qkb_ref[:, B_KEY_WIDTH:2 * B_KEY_WIDTH]
        qm, km, kl, qc = q * e_q, k * e_k, k * e_l, q * e_c
        lane_head = lax.broadcasted_iota(jnp.int32, (1, B_KEY_WIDTH), 1) // B_KEY_DIM
        d_o_mx = d_o.astype(_MX)

        def on_diagonal(st):
            z = jnp.zeros((B_KEY_DIM, B_VAL_DIM), st.dtype)
            return jnp.concatenate([jnp.concatenate(
                [st[h * B_KEY_DIM:(h + 1) * B_KEY_DIM] if g == h else z for g in range(B_HEADS)], axis=1)
                for h in range(B_HEADS)], axis=0)

        def diagonal_of(full):
            return jnp.concatenate([full[h * B_KEY_DIM:(h + 1) * B_KEY_DIM, h * B_VAL_DIM:(h + 1) * B_VAL_DIM]
                                    for h in range(B_HEADS)], axis=0)

        dqm, dkm, dv_cols = None, None, []
        for h in range(B_HEADS):
            vl_ = slice(h * B_VAL_DIM, (h + 1) * B_VAL_DIM)
            mine = lane_head == h
            qz, kz = jnp.where(mine, qm, 0.0).astype(_MX), jnp.where(mine, km, 0.0).astype(_MX)
            a = jnp.where(low, _dot_nt(qz, kz), 0.0).astype(_MX)
            da = jnp.where(low, _dot_nt(d_o_mx[:, vl_], vb_ref[:, vl_]), 0.0).astype(_MX)
            dqm_h, dkm_h = _dot(da, kz), _dot_tn(da, qz)
            dqm = dqm_h if dqm is None else dqm + dqm_h
            dkm = dkm_h if dkm is None else dkm + dkm_h
            dv_cols.append(_dot_tn(a, d_o_mx[:, vl_]))
        dv = jnp.concatenate(dv_cols, axis=1)

        chunk = [slice(c * B_CHUNK, (c + 1) * B_CHUNK) for c in range(cpb)]
        dqc_rows, g_loc = [], []
        for c in range(cpb):
            dqc_rows.append(_dot_nt(d_o_mx[chunk[c]], on_diagonal(sst_ref[c].astype(_MX))))
            g_loc.append(diagonal_of(_dot_tn(qc[chunk[c]], d_o_mx[chunk[c]])))
        cur = ds_ref[...]
        d_state = [None] * cpb
        for c in reversed(range(cpb)):
            d_state[c] = cur
            cur = g_loc[c] + jnp.exp(jnp.sum(la[chunk[c]].T, axis=1, keepdims=True)) * cur
        ds_ref[...] = cur
        dkl_rows, dv_rows, dlast_rows = [], [], []
        ones8 = jnp.ones((8, B_VAL_DIM), _F32)
        for c in range(cpb):
            dsd = on_diagonal(d_state[c].astype(_MX))
            dkl_c = _dot_nt(vb_ref[chunk[c], :], dsd)
            dkl_rows.append(dkl_c)
            dv_rows.append(_dot(kl[chunk[c]], dsd))
            prod = d_state[c] * sst_ref[c]
            p0 = prod.astype(jnp.bfloat16)
            p1 = (prod - p0.astype(_F32)).astype(jnp.bfloat16)
            p2 = (prod - p0.astype(_F32) - p1.astype(_F32)).astype(jnp.bfloat16)
            ddec = (_dot_nt(ones8, p0) + _dot_nt(ones8, p1) + _dot_nt(ones8, p2))[0:1]
            r_last = c * B_CHUNK + B_CHUNK - 1
            dlast = jnp.sum(dkl_c * kl[chunk[c]], axis=0, keepdims=True) + ddec * jnp.exp(last[r_last:r_last + 1])
            dlast_rows.append(jnp.broadcast_to(dlast, (B_CHUNK, B_KEY_WIDTH)))
        dqc, dkl = jnp.concatenate(dqc_rows, axis=0), jnp.concatenate(dkl_rows, axis=0)
        dqkb_ref[:, 0:B_KEY_WIDTH] = ((dqm * e_q + dqc * e_c) * (B_KEY_DIM ** -0.5)).astype(_ST)
        dqkb_ref[:, B_KEY_WIDTH:2 * B_KEY_WIDTH] = (dkm * e_k + dkl * e_l).astype(_ST)
        dvb_ref[...] = (dv + jnp.concatenate(dv_rows, axis=0)).astype(_ST)
        dcum = dqm * qm - dkm * km + dqc * qc - dkl * kl
        row = lax.broadcasted_iota(jnp.int32, (tb, B_KEY_WIDTH), 0)
        dcum = jnp.where(row % B_CHUNK == B_CHUNK - 1, dcum + jnp.concatenate(dlast_rows, axis=0), dcum)
        dla = _dot_ones((same & (ri <= ci)).astype(_F32), dcum)

        da_pre = dla * (1.0 / B_GATE_TEMP) * (1.0 - _sigmoid(a_pre))
        dalr_ref[...] = _dot_nt(da_pre, wup_ref[...]).astype(_ST)
        dwup_ref[...] = dwup_ref[...] + _dot_tn(alr_ref[...], da_pre)
        db_ref[...] = db_ref[...] + jnp.sum(da_pre, axis=0, keepdims=True)

    blk = lambda s, i: s * nblk + nblk - 1 - i
    rows = lambda w: pl.BlockSpec((tb, w), lambda s, i: (blk(s, i), 0))
    full = lambda a, b: pl.BlockSpec((a, b), lambda s, i: (0, 0))
    act = lambda w: jax.ShapeDtypeStruct((t, w), _ST)
    return pl.pallas_call(
        body, name="gla_bwd", grid=(nseq, nblk),
        in_specs=[rows(512), rows(512), rows(512), rows(LANE), rows(512), rows(512),
                  pl.BlockSpec((cpb, B_KEY_WIDTH, B_VAL_DIM), lambda s, i: (blk(s, i), 0, 0)),
                  full(LANE, B_KEY_WIDTH), full(1, B_KEY_WIDTH), full(1, B_WIDTH)],
        out_specs=[rows(512), rows(512), rows(512), rows(LANE), full(LANE, B_KEY_WIDTH),
                   full(1, B_KEY_WIDTH), full(1, B_WIDTH)],
        out_shape=[act(512), act(512), act(512), act(LANE),
                   jax.ShapeDtypeStruct((LANE, B_KEY_WIDTH), _F32),
                   jax.ShapeDtypeStruct((1, B_KEY_WIDTH), _F32), jax.ShapeDtypeStruct((1, B_WIDTH), _F32)],
        scratch_shapes=[pltpu.VMEM((B_KEY_WIDTH, B_VAL_DIM), _F32)],
        compiler_params=_cparams(("arbitrary", "arbitrary")),
    )(qkb, vb, zb, alr, oraw, dob, sst, wup, b_alpha, gn)


def _merge_loss(oa, ob, ga, gb, x2, tgt, wa, wb, wo, g_final):
    t = x2.shape[0]
    tm = min(t, 256)
    nt = t // tm

    def body(oa_ref, ob_ref, ga_ref, gb_ref, x_ref, t_ref, wa_ref, wb_ref, wo_ref, gf_ref,
             dh_ref, doa_ref, dob_ref, dga_ref, dgb_ref, dwa_ref, dwb_ref, dwo_ref, dgf_ref, loss_ref):
        i = pl.program_id(0)

        @pl.when(i == 0)
        def _():
            dwa_ref[...] = jnp.zeros_like(dwa_ref)
            dwb_ref[...] = jnp.zeros_like(dwb_ref)
            dwo_ref[...] = jnp.zeros_like(dwo_ref)
            dgf_ref[...] = jnp.zeros_like(dgf_ref)
            loss_ref[...] = jnp.zeros_like(loss_ref)

        oa_v, ob_v = oa_ref[...], ob_ref[...]
        ya, yb = _dot(oa_v, wa_ref[...]), _dot(ob_v, wb_ref[...])
        sga, sgb = _sigmoid(ga_ref[...].astype(_F32)), _sigmoid(gb_ref[...].astype(_F32))
        merged = (sga * ya + sgb * yb).astype(_MX)
        out = x_ref[...] + _dot(merged, wo_ref[...])
        r = lax.rsqrt(jnp.mean(out * out, axis=-1, keepdims=True) + NORM_EPS)
        nrm = out * r
        gf = gf_ref[...]
        err = nrm * gf - t_ref[...]
        loss_ref[...] = loss_ref[...] + (0.5 / D_MODEL) * jnp.sum(err * err, axis=0, keepdims=True)
        dy = err * (1.0 / D_MODEL)
        dgf_ref[...] = dgf_ref[...] + jnp.sum(dy * nrm, axis=0, keepdims=True)
        dh = _rms_bwd(dy * gf, nrm, r)
        dh_ref[...] = dh
        dh_mx = dh.astype(_MX)
        dmer = _dot_nt(dh_mx, wo_ref[...])
        dwo_ref[...] = dwo_ref[...] + _dot_tn(merged, dh_mx)
        dya, dyb = (dmer * sga).astype(_MX), (dmer * sgb).astype(_MX)
        dga_ref[...] = (dmer * ya * sga * (1.0 - sga)).astype(_ST)
        dgb_ref[...] = (dmer * yb * sgb * (1.0 - sgb)).astype(_ST)
        doa_ref[...] = _dot_nt(dya, wa_ref[...]).astype(_ST)
        dob_ref[...] = _dot_nt(dyb, wb_ref[...]).astype(_ST)
        dwa_ref[...] = dwa_ref[...] + _dot_tn(oa_v, dya)
        dwb_ref[...] = dwb_ref[...] + _dot_tn(ob_v, dyb)

    rows = lambda w: pl.BlockSpec((tm, w), lambda i: (i, 0))
    full = lambda a, b: pl.BlockSpec((a, b), lambda i: (0, 0))
    return pl.pallas_call(
        body, name="merge_loss", grid=(nt,),
        in_specs=[rows(512), rows(512), rows(D_MODEL), rows(D_MODEL), rows(D_MODEL), rows(D_MODEL),
                  full(A_WIDTH, D_MODEL), full(B_WIDTH, D_MODEL), full(D_MODEL, D_MODEL), full(1, D_MODEL)],
        out_specs=[rows(D_MODEL), rows(512), rows(512), rows(D_MODEL), rows(D_MODEL),
                   full(A_WIDTH, D_MODEL), full(B_WIDTH, D_MODEL), full(D_MODEL, D_MODEL),
                   full(1, D_MODEL), full(1, D_MODEL)],
        out_shape=[jax.ShapeDtypeStruct((t, D_MODEL), _F32), jax.ShapeDtypeStruct((t, 512), _ST),
                   jax.ShapeDtypeStruct((t, 512), _ST), jax.ShapeDtypeStruct((t, D_MODEL), _ST),
                   jax.ShapeDtypeStruct((t, D_MODEL), _ST),
                   jax.ShapeDtypeStruct((A_WIDTH, D_MODEL), _F32), jax.ShapeDtypeStruct((B_WIDTH, D_MODEL), _F32),
                   jax.ShapeDtypeStruct((D_MODEL, D_MODEL), _F32), jax.ShapeDtypeStruct((1, D_MODEL), _F32),
                   jax.ShapeDtypeStruct((1, D_MODEL), _F32)],
        compiler_params=_cparams(("arbitrary",), VMEM_LIMIT),
    )(oa, ob, ga, gb, x2, tgt, wa, wb, wo, g_final)


def _in_proj_bwd_x(dpieces, wt, x2, dh2, g_in):
    t = x2.shape[0]
    tm = min(t, 256)
    np_ = len(PIECES)

    def body(*refs):
        dp_refs = refs[:np_]
        w_ref, x_ref, dh2_ref, g_ref, gx_ref, dg_ref = refs[np_:]

        @pl.when(pl.program_id(0) == 0)
        def _():
            dg_ref[...] = jnp.zeros_like(dg_ref)

        dh = None
        for (name, a, b), dp in zip(PIECES, dp_refs):
            part = _dot(dp[...], w_ref[a:b, :])
            dh = part if dh is None else dh + part
        xv = x_ref[...]
        r = lax.rsqrt(jnp.mean(xv * xv, axis=-1, keepdims=True) + NORM_EPS)
        nrm = xv * r
        dg_ref[...] = dg_ref[...] + jnp.sum(dh * nrm, axis=0, keepdims=True)
        gx_ref[...] = dh2_ref[...] + _rms_bwd(dh * g_ref[...], nrm, r)

    rows = lambda w: pl.BlockSpec((tm, w), lambda i: (i, 0))
    full = lambda a, b: pl.BlockSpec((a, b), lambda i: (0, 0))
    return pl.pallas_call(
        body, name="in_proj_bwd_x", grid=(t // tm,),
        in_specs=[rows(b - a) for _, a, b in PIECES] + [full(D_IN, D_MODEL), rows(D_MODEL), rows(D_MODEL),
                                                          full(1, D_MODEL)],
        out_specs=[rows(D_MODEL), full(1, D_MODEL)],
        out_shape=[jax.ShapeDtypeStruct((t, D_MODEL), _F32), jax.ShapeDtypeStruct((1, D_MODEL), _F32)],
        compiler_params=_cparams(("arbitrary",), VMEM_LIMIT),
    )(*dpieces, wt, x2, dh2, g_in)


def _in_proj_bwd_w(h, dpieces):
    t = h.shape[0]
    tm = min(t, 512)
    nt = t // tm
    np_ = len(PIECES)

    def body(*refs):
        h_ref, dp_refs, out_ref = refs[0], refs[1:1 + np_], refs[1 + np_]
        acc_ref, sem = refs[2 + np_:]
        i = pl.program_id(0)

        @pl.when(i == 0)
        def _():
            acc_ref[...] = jnp.zeros_like(acc_ref)

        hv = h_ref[...]
        writes = []
        for j, ((name, a, b), dp) in enumerate(zip(PIECES, dp_refs)):
            part = _dot_tn(dp[...], hv)
            if name == "alr":
                b = a + B_GATE_RANK
                part = part[0:B_GATE_RANK]
            acc_ref[a:b, :] = acc_ref[a:b, :] + part
            writes.append(pltpu.make_async_copy(acc_ref.at[a:b], out_ref.at[a:b], sem.at[j]))

            @pl.when(i == nt - 1)
            def _(cp=writes[-1]):
                cp.start()

        @pl.when(i == nt - 1)
        def _():
            for cp in writes:
                cp.wait()

    rows = lambda w: pl.BlockSpec((tm, w), lambda i: (i, 0))
    return pl.pallas_call(
        body, name="in_proj_bwd_w", grid=(nt,),
        in_specs=[rows(D_MODEL)] + [rows(b - a) for _, a, b in PIECES],
        out_specs=_ANY, out_shape=jax.ShapeDtypeStruct((D_IN, D_MODEL), _F32),
        scratch_shapes=[pltpu.VMEM((D_IN, D_MODEL), _F32), pltpu.SemaphoreType.DMA((np_,))],
        compiler_params=_cparams(("arbitrary",), VMEM_LIMIT),
    )(h, *dpieces)


def _place():
    return lax.axis_index("x"), lax.axis_index("y"), lax.axis_index("c")


def _other_chips(x, y):
    return [(1 - x, y), (x, 1 - y), (1 - x, 1 - y)]


class _Split(NamedTuple):
    by_rows: bool
    step: int
    size: int

    def half(self, ref, c):
        r, n = ref.shape[-2:]
        if self.by_rows:
            return ref.at[:, pl.ds(pl.multiple_of(c * (n // 2), LANE), n // 2)]
        return ref.at[pl.ds(pl.multiple_of(c * (r // 2), 16), r // 2), :]

    def chip_part(self, ref, k):
        if self.by_rows:
            return ref.at[pl.ds(pl.multiple_of(k * self.step, 16), self.size), :]
        return ref.at[:, pl.ds(pl.multiple_of(k * self.size, LANE), self.size)]

    def half_shape(self, shape):
        r, n = shape
        return (r, n // 2) if self.by_rows else (r // 2, n)

    def part_shape(self, shape):
        r, n = shape
        return (self.size, n) if self.by_rows else (r, self.size)


SPLIT_W_IN_T = _Split(True, WINDOW_STEP, WINDOW_ROWS)
SPLIT_W_O = _Split(True, 256, 256)
SPLIT_W_OUT = _Split(False, 256, 256)


def _gather_weights(shards, splits, fulls, pos_f):
    nw = len(shards)
    t = pos_f.shape[0]

    def body(*refs):
        ins, (pos_ref, c_ref) = refs[:nw], refs[nw:nw + 2]
        outs, tables = refs[nw + 2:2 * nw + 2], refs[2 * nw + 2:2 * nw + 5]
        send_a, recv_a, send_b, recv_b = refs[2 * nw + 5:]
        x, y, c = _place()
        me = 2 * x + y
        peers = _other_chips(x, y)

        def place(i, k, half):
            if splits[i] is None:
                return outs[i].at[k]
            if fulls[i][0] == 4 and len(fulls[i]) == 3:
                whole = outs[i].at[k]
            else:
                whole = splits[i].chip_part(outs[i], k)
            return splits[i].half(whole, half)

        first, passed = [], []
        for i in range(nw):
            src = ins[i] if splits[i] is None else splits[i].half(ins[i], c)
            for j, (px, py) in enumerate(peers):
                cp = pltpu.make_async_remote_copy(
                    src_ref=src, dst_ref=place(i, me, c), send_sem=send_a.at[3 * i + j],
                    recv_sem=recv_a.at[3 * i + j], device_id=(px, py, c), device_id_type=_MESH)
                cp.start()
                first.append(cp)
        _rope_tables_into(pos_ref, c_ref, *tables)
        for i in range(nw):
            for j, (px, py) in enumerate(peers):
                landed = place(i, 2 * px + py, c)
                pltpu.make_async_remote_copy(
                    src_ref=landed, dst_ref=landed, send_sem=send_a.at[3 * i + j], recv_sem=recv_a.at[3 * i + j],
                    device_id=(px, py, c), device_id_type=_MESH).wait_recv()
                if splits[i] is not None:
                    cp = pltpu.make_async_remote_copy(
                        src_ref=landed, dst_ref=landed, send_sem=send_b.at[3 * i + j], recv_sem=recv_b.at[3 * i + j],
                        device_id=(x, y, 1 - c), device_id_type=_MESH)
                    cp.start()
                    passed.append(cp)
        for i in range(nw):
            if splits[i] is None:
                continue
            for j, (px, py) in enumerate(peers):
                theirs = place(i, 2 * px + py, 1 - c)
                pltpu.make_async_remote_copy(
                    src_ref=theirs, dst_ref=theirs, send_sem=send_b.at[3 * i + j], recv_sem=recv_b.at[3 * i + j],
                    device_id=(x, y, 1 - c), device_id_type=_MESH).wait_recv()
        for cp in first + passed:
            cp.wait_send()

    vm = pl.BlockSpec(memory_space=pltpu.VMEM)
    tab = jax.ShapeDtypeStruct((t, LANE), _F32)
    return pl.pallas_call(
        body, name="gather_weights",
        in_specs=[_ANY] * nw + [vm, vm], out_specs=[_ANY] * nw + [vm] * 3,
        out_shape=[jax.ShapeDtypeStruct(f, s.dtype) for f, s in zip(fulls, shards)] + [tab] * 3,
        scratch_shapes=[pltpu.SemaphoreType.DMA((3 * nw,)) for _ in range(4)],
        compiler_params=_cparams(None, VMEM_LIMIT),
    )(*shards, pos_f, _rope_consts())


def _assemble_w_in_t(slots):
    bw = 256
    ov = WINDOW_ROWS - WINDOW_STEP

    def body(s_ref, o_ref):
        for k in range(4):
            base = k * WINDOW_STEP
            lo = 0 if k == 0 else ov
            if k > 0:
                o_ref[base:base + ov, :] = s_ref[k - 1, WINDOW_STEP:WINDOW_ROWS, :] + s_ref[k, 0:ov, :]
            hi = WINDOW_ROWS if k == 3 else WINDOW_STEP
            o_ref[base + lo:base + hi, :] = s_ref[k, lo:hi, :]

    return pl.pallas_call(
        body, name="assemble_w_in_t", grid=(D_MODEL // bw,),
        in_specs=[pl.BlockSpec((4, WINDOW_ROWS, bw), lambda i: (0, 0, i))],
        out_specs=pl.BlockSpec((D_IN, bw), lambda i: (0, i)),
        out_shape=jax.ShapeDtypeStruct((D_IN, D_MODEL), slots.dtype),
        compiler_params=_cparams(("parallel",)),
    )(slots)


def _pair_exchange(grads, splits):
    nw = len(grads)

    def body(*refs):
        ins, outs = refs[:nw], refs[nw:2 * nw]
        send, recv = refs[2 * nw:]
        x, y, c = _place()
        copies = []
        for i in range(nw):
            cp = pltpu.make_async_remote_copy(
                src_ref=splits[i].half(ins[i], 1 - c), dst_ref=outs[i], send_sem=send.at[i], recv_sem=recv.at[i],
                device_id=(x, y, 1 - c), device_id_type=_MESH)
            cp.start()
            copies.append(cp)
        for cp in copies:
            cp.wait()

    return pl.pallas_call(
        body, name="grad_pair_exchange",
        in_specs=[_ANY] * nw, out_specs=[_ANY] * nw,
        out_shape=[jax.ShapeDtypeStruct(sp.half_shape(g.shape), g.dtype) for g, sp in zip(grads, splits)],
        scratch_shapes=[pltpu.SemaphoreType.DMA((nw,)), pltpu.SemaphoreType.DMA((nw,))],
    )(*grads)


def _row_block(rows):
    for cand in (976, 176, 256, 128):
        if rows % cand == 0:
            return cand
    return rows


def _pair_sum(g, r, split, c_arr, name):
    hr, hn = r.shape
    br = _row_block(hr)
    if split.by_rows:
        g_spec = pl.BlockSpec((br, hn), lambda i, c_ref: (i, c_ref[0]))
    else:
        g_spec = pl.BlockSpec((br, hn), lambda i, c_ref: (c_ref[0] * (hr // br) + i, 0))

    def body(c_ref, g_ref, r_ref, o_ref):
        o_ref[...] = (g_ref[...] + r_ref[...]).astype(o_ref.dtype)

    return pl.pallas_call(
        body, name=name,
        grid_spec=pltpu.PrefetchScalarGridSpec(
            num_scalar_prefetch=1, grid=(hr // br,),
            in_specs=[g_spec, pl.BlockSpec((br, hn), lambda i, c_ref: (i, 0))],
            out_specs=pl.BlockSpec((br, hn), lambda i, c_ref: (i, 0))),
        out_shape=jax.ShapeDtypeStruct(r.shape, _MX),
        compiler_params=_cparams(("parallel",)),
    )(c_arr, g, r)


def _chip_exchange(parts, splits, small):
    nw = len(parts)

    def body(*refs):
        ins, small_ref, outs, all_ref = refs[:nw], refs[nw], refs[nw + 1:2 * nw + 1], refs[2 * nw + 1]
        send, recv, lsem, s_send, s_recv = refs[2 * nw + 2:]
        x, y, c = _place()
        me = 2 * x + y
        dev = 2 * me + c
        copies = [pltpu.make_async_copy(small_ref, all_ref.at[dev], lsem.at[nw])]
        copies[0].start()
        for r in range(1, 8):
            peer = (1 - x if r & 4 else x, 1 - y if r & 2 else y, 1 - c if r & 1 else c)
            cp = pltpu.make_async_remote_copy(
                src_ref=small_ref, dst_ref=all_ref.at[dev], send_sem=s_send.at[r - 1], recv_sem=s_recv.at[r - 1],
                device_id=peer, device_id_type=_MESH)
            cp.start()
            copies.append(cp)
        for i in range(nw):
            mine = pltpu.make_async_copy(splits[i].chip_part(ins[i], me), outs[i].at[me], lsem.at[i])
            mine.start()
            copies.append(mine)
            for j, (px, py) in enumerate(_other_chips(x, y)):
                cp = pltpu.make_async_remote_copy(
                    src_ref=splits[i].chip_part(ins[i], 2 * px + py), dst_ref=outs[i].at[me],
                    send_sem=send.at[3 * i + j], recv_sem=recv.at[3 * i + j],
                    device_id=(px, py, c), device_id_type=_MESH)
                cp.start()
                copies.append(cp)
        for cp in copies:
            cp.wait()

    return pl.pallas_call(
        body, name="grad_chip_exchange",
        in_specs=[_ANY] * (nw + 1), out_specs=[_ANY] * (nw + 1),
        out_shape=[jax.ShapeDtypeStruct((4,) + sp.part_shape(p.shape), p.dtype) for p, sp in zip(parts, splits)]
        + [jax.ShapeDtypeStruct((8,) + small.shape, small.dtype)],
        scratch_shapes=[pltpu.SemaphoreType.DMA((3 * nw,)), pltpu.SemaphoreType.DMA((3 * nw,)),
                        pltpu.SemaphoreType.DMA((nw + 1,)), pltpu.SemaphoreType.DMA((7,)),
                        pltpu.SemaphoreType.DMA((7,))],
    )(*parts, small)


def _sum_chips(q, split, c_arr, name):
    _, hr, hn = q.shape
    br = _row_block(hr)
    if split.by_rows:
        out_shape = (hr, 2 * hn)
        o_spec = pl.BlockSpec((br, hn), lambda i, c_ref: (i, c_ref[0]))
    else:
        out_shape = (2 * hr, hn)
        o_spec = pl.BlockSpec((br, hn), lambda i, c_ref: (c_ref[0] * (hr // br) + i, 0))

    def body(c_ref, q_ref, o_ref):
        f = lambda k: q_ref[k].astype(_F32)
        o_ref[...] = ((f(0) + f(1)) + f(2)) + f(3)

    return pl.pallas_call(
        body, name=name,
        grid_spec=pltpu.PrefetchScalarGridSpec(
            num_scalar_prefetch=1, grid=(hr // br,),
            in_specs=[pl.BlockSpec((4, br, hn), lambda i, c_ref: (0, i, 0))], out_specs=o_spec),
        out_shape=jax.ShapeDtypeStruct(out_shape, _F32),
        compiler_params=_cparams(("parallel",)),
    )(c_arr, q)


def _pair_share(bufs, splits):
    nw = len(bufs)

    def body(*refs):
        ins, outs = refs[:nw], refs[nw:2 * nw]
        send, recv = refs[2 * nw:]
        x, y, c = _place()
        copies = []
        for i in range(nw):
            cp = pltpu.make_async_remote_copy(
                src_ref=splits[i].half(ins[i], c), dst_ref=splits[i].half(outs[i], c), send_sem=send.at[i],
                recv_sem=recv.at[i], device_id=(x, y, 1 - c), device_id_type=_MESH)
            cp.start()
            copies.append(cp)
        for cp in copies:
            cp.wait()

    return pl.pallas_call(
        body, name="grad_pair_share",
        in_specs=[_ANY] * nw, out_specs=[_ANY] * nw,
        out_shape=[jax.ShapeDtypeStruct(b.shape, b.dtype) for b in bufs],
        input_output_aliases={i: i for i in range(nw)},
        scratch_shapes=[pltpu.SemaphoreType.DMA((nw,)), pltpu.SemaphoreType.DMA((nw,))],
    )(*bufs)


def _sum_devices(parts):
    def body(p_ref, tot_ref):
        acc = p_ref[0]
        for d in range(1, 8):
            acc = acc + p_ref[d]
        tot_ref[...] = acc

    vm = pl.BlockSpec(memory_space=pltpu.VMEM)
    return pl.pallas_call(
        body, name="small_sum", in_specs=[vm], out_specs=vm,
        out_shape=jax.ShapeDtypeStruct(parts.shape[1:], parts.dtype),
    )(parts)


def _adamw(w, g, m, v, name):
    lead = w.shape[0] != 1
    r, n = (w.shape[0], w.shape[2]) if lead else w.shape[1:]
    br = r
    for cand in (256, 244, 128):
        if r > cand and r % cand == 0:
            br = cand
            break

    def body(w_ref, g_ref, m_ref, v_ref, d_ref, nm_ref, nv_ref):
        gv = g_ref[...]
        m2 = ADAM_B1 * m_ref[...] + (1.0 - ADAM_B1) * gv
        v2 = ADAM_B2 * v_ref[...] + (1.0 - ADAM_B2) * (gv * gv)
        m_hat = m2 / (1.0 - ADAM_B1 ** ADAM_STEP)
        v_hat = v2 / (1.0 - ADAM_B2 ** ADAM_STEP)
        d_ref[...] = -ADAM_LR * (m_hat / (jnp.sqrt(v_hat) + ADAM_EPS) + ADAM_WD * w_ref[...])
        nm_ref[...] = m2
        nv_ref[...] = v2

    blk = pl.BlockSpec((br, 1, n), lambda i: (i, 0, 0)) if lead else pl.BlockSpec((None, br, n), lambda i: (0, i, 0))
    shp = jax.ShapeDtypeStruct(w.shape, _F32)
    return pl.pallas_call(
        body, name=name, grid=(r // br,),
        in_specs=[blk] * 4, out_specs=[blk] * 3, out_shape=[shp] * 3,
        compiler_params=_cparams(("parallel",)),
    )(w, g, m, v)


def kernel(x, positions, g_in, w_in, w_alpha_up, b_alpha, attn_sinks, g_gla_norm, w_out_a, w_out_b, w_o, g_final, loss_target, m_g_in, m_w_in, m_w_alpha_up, m_b_alpha, m_attn_sinks, m_g_gla_norm, m_w_out_a, m_w_out_b, m_w_o, m_g_final, v_g_in, v_w_in, v_w_alpha_up, v_b_alpha, v_attn_sinks, v_g_gla_norm, v_w_out_a, v_w_out_b, v_w_o, v_g_final):
    nseq, seq, _ = x.shape
    t = nseq * seq
    cx, cy, cc = _place()
    chip = 2 * cx + cy
    c_arr = jnp.reshape(cc, (1,)).astype(jnp.int32)

    tr = lambda w: jnp.transpose(w, (2, 0, 1))
    w_in_t = tr(w_in).reshape(SHARD, D_MODEL).astype(_MX)
    pad = WINDOW_ROWS - SHARD
    window = lax.switch(chip, [lambda w, k=k: jnp.pad(w, ((4 * k, pad - 4 * k), (0, 0))) for k in range(4)], w_in_t)
    shards = [window, w_out_a[0].astype(_MX), w_out_b[0].astype(_MX), w_o[0].astype(_MX), w_alpha_up[0].astype(_MX)]
    splits = [SPLIT_W_IN_T, SPLIT_W_OUT, SPLIT_W_OUT, SPLIT_W_O, None]
    fulls = [(4, WINDOW_ROWS, D_MODEL), (A_WIDTH, D_MODEL), (B_WIDTH, D_MODEL), (D_MODEL, D_MODEL),
             (4, B_GATE_RANK, B_KEY_WIDTH // 4)]
    pos_f = positions.astype(_F32).reshape(t, 1)
    win_g, wa, wb, wo, wup_g, cos, sa, sb = _gather_weights(shards, splits, fulls, pos_f)
    nsh = D_MODEL // 4
    win_g = lax.dynamic_update_slice(win_g, window[None], (chip, 0, 0))
    wa = lax.dynamic_update_slice(wa, shards[1], (0, nsh * chip))
    wb = lax.dynamic_update_slice(wb, shards[2], (0, nsh * chip))
    wo = lax.dynamic_update_slice(wo, shards[3], (nsh * chip, 0))
    wup_g = lax.dynamic_update_slice(wup_g, shards[4][None], (chip, 0, 0))
    wt = _assemble_w_in_t(win_g)
    wup = jnp.concatenate([jnp.transpose(wup_g, (1, 0, 2)).reshape(B_GATE_RANK, B_KEY_WIDTH),
                           jnp.zeros((LANE - B_GATE_RANK, B_KEY_WIDTH), _MX)], axis=0)

    x2 = x.reshape(t, D_MODEL)
    tgt = loss_target.reshape(t, D_MODEL)
    sinks = attn_sinks.reshape(A_HEADS)
    gf = g_final.reshape(1, D_MODEL)

    h, qkv, za, qkb, vb, zb, alr, ga, gb = _in_proj(x2, g_in, wt, cos, sa, sb)
    oa = _attn_fwd(qkv, za, sinks, nseq)
    ob, oraw, sst = _gla_fwd(qkb, vb, zb, alr, wup, b_alpha, g_gla_norm, nseq)

    dh2, doa, dob, dga, dgb, dwa, dwb, dwo, dgf, lossv = _merge_loss(oa, ob, ga, gb, x2, tgt, wa, wb, wo, gf)

    dqkv, dza, dsink = _attn_bwd(qkv, za, doa, sinks, cos, sa, sb, nseq)
    dqkb, dvb, dzb, dalr, dwup, dba, dgn = _gla_bwd(qkb, vb, zb, alr, oraw, dob, sst, wup, b_alpha, g_gla_norm, nseq)
    dpieces = [dqkv, dza, dqkb, dvb, dzb, dalr, dga, dgb]
    grad_x2, dgin = _in_proj_bwd_x(dpieces, wt, x2, dh2, g_in)
    dwin_t = _in_proj_bwd_w(h, dpieces)

    grads = [dwin_t, dwa, dwb, dwo]
    gsplits = [SPLIT_W_IN_T, SPLIT_W_OUT, SPLIT_W_OUT, SPLIT_W_O]
    names = ("w_in", "w_out_a", "w_out_b", "w_o")
    from_sibling = _pair_exchange(grads, gsplits)
    pair_sums = [_pair_sum(g, r, sp, c_arr, "pair_sum_" + nm)
                 for g, r, sp, nm in zip(grads, from_sibling, gsplits, names)]
    small = jnp.concatenate([
        dgin, dgf, dgn, dba,
        jnp.pad(dsink[:, 0].reshape(1, A_HEADS), ((0, 0), (0, LANE - A_HEADS))),
        jnp.pad(jnp.sum(lossv, axis=1, keepdims=True), ((0, 0), (0, LANE - 1))),
        dwup[:B_GATE_RANK].reshape(1, B_GATE_RANK * B_KEY_WIDTH)], axis=1)
    *from_chips, small_parts = _chip_exchange(pair_sums, gsplits, jnp.pad(small, ((0, 7), (0, 0))))
    reduced = [_sum_chips(q, sp, c_arr, "chip_sum_" + nm) for q, sp, nm in zip(from_chips, gsplits, names)]
    g_window, g_wa, g_wb, g_wo = _pair_share(reduced, gsplits)
    g_win_t = lax.switch(chip, [lambda w, k=k: w[4 * k:4 * k + SHARD].reshape(SHARD, 1, D_MODEL) for k in range(4)],
                         g_window)
    tot = _sum_devices(small_parts)[0:1]
    o = 0
    def take(n):
        nonlocal o
        o += n
        return tot[:, o - n:o]
    g_gin, g_gf, g_gn, g_ba = take(D_MODEL), take(D_MODEL), take(B_WIDTH), take(B_KEY_WIDTH)
    g_sink = take(LANE)[:, :A_HEADS]
    loss = take(LANE)[0, 0]
    g_wup_full = take(B_GATE_RANK * B_KEY_WIDTH).reshape(B_GATE_RANK, B_KEY_WIDTH)
    nup = B_KEY_WIDTH // 4
    g_wup = lax.dynamic_slice(g_wup_full, (0, chip * nup), (B_GATE_RANK, nup))

    def pack(*parts):
        return jnp.concatenate([p.reshape(1, -1) for p in parts], axis=1)

    sm_w = pack(g_in, g_final, g_gla_norm, b_alpha, attn_sinks, w_alpha_up)
    sm_g = pack(g_gin, g_gf, g_gn, g_ba, g_sink, g_wup)
    sm_m = pack(m_g_in, m_g_final, m_g_gla_norm, m_b_alpha, m_attn_sinks, m_w_alpha_up)
    sm_v = pack(v_g_in, v_g_final, v_g_gla_norm, v_b_alpha, v_attn_sinks, v_w_alpha_up)
    sm_out = [p[0] for p in _adamw(sm_w[None], sm_g[None], sm_m[None], sm_v[None], "adamw_small")]

    def unpack(p):
        sizes = (D_MODEL, D_MODEL, B_WIDTH, B_KEY_WIDTH, A_HEADS, B_GATE_RANK * nup)
        outs, at = [], 0
        for s in sizes:
            outs.append(p[:, at:at + s])
            at += s
        gi, gfin, gnn, ba, sk, wu = outs
        return dict(g_in=gi, g_final=gfin.reshape(D_MODEL), g_gla_norm=gnn, b_alpha=ba, attn_sinks=sk,
                    w_alpha_up=wu.reshape(1, B_GATE_RANK, nup))

    untr = lambda a: jnp.transpose(a, (1, 2, 0))
    big = dict(w_in=tuple(untr(a) for a in (g_win_t,) + tuple(_adamw(tr(w_in), g_win_t, tr(m_w_in), tr(v_w_in), "adamw_w_in"))))
    for nm, w, g, m, v in (("w_out_a", w_out_a, g_wa, m_w_out_a, v_w_out_a),
                           ("w_out_b", w_out_b, g_wb, m_w_out_b, v_w_out_b), ("w_o", w_o, g_wo, m_w_o, v_w_o)):
        big[nm] = (g[None],) + tuple(_adamw(w, g[None], m, v, "adamw_" + nm))

    order = ("g_in", "w_in", "w_alpha_up", "b_alpha", "attn_sinks", "g_gla_norm", "w_out_a", "w_out_b", "w_o", "g_final")
    small_sets = [unpack(sm_g)] + [unpack(p) for p in sm_out]
    outs = []
    for kind in range(4):
        for nm in order:
            outs.append(big[nm][kind] if nm in big else small_sets[kind][nm])
    return (loss, grad_x2.reshape(x.shape), *outs)
```

```python
import math
from typing import NamedTuple

import numpy as np
import jax
import jax.numpy as jnp
from jax import lax
from jax.experimental import pallas as pl
from jax.experimental.pallas import tpu as pltpu

D_MODEL = 1024
A_HEADS, A_KV_HEADS, A_HEAD_DIM = 8, 2, 64
A_GROUP = A_HEADS // A_KV_HEADS
A_WIDTH, A_KV_WIDTH = 512, 128
BLOCK = 128
ROPE_THETA = 500000.0
ROPE_DIM = 16
B_HEADS, B_KEY_DIM, B_VAL_DIM = 4, 64, 128
B_KEY_WIDTH, B_WIDTH = 256, 512
B_GATE_RANK = 16
B_GATE_TEMP = 16.0
B_CHUNK = 64
NORM_EPS = 1e-6
NEG_BIG = -1e30
D_IN = 4880

ADAM_LR, ADAM_B1, ADAM_B2, ADAM_EPS, ADAM_WD, ADAM_STEP = 0.001, 0.9, 0.999, 1e-08, 0.01, 10

LANE = 128
ALR_AT = 2816
PIECES = (("qkv", 0, 768), ("za", 768, 1280), ("qkb", 1280, 1792), ("vb", 1792, 2304),
          ("zb", 2304, 2816), ("alr", ALR_AT, ALR_AT + LANE), ("ga", 2832, 3856), ("gb", 3856, 4880))
SHARD = D_IN // 4
WINDOW_STEP = 1216
WINDOW_ROWS = 1232

GLA_BLOCK = 256
MERGE_SLAB = 16
VMEM_LIMIT = 56 * 1024 * 1024

_F32 = jnp.float32
_MX = jnp.bfloat16
_ST = jnp.bfloat16

_MESH = pl.DeviceIdType.MESH
_ANY = pl.BlockSpec(memory_space=pl.ANY)


def _cparams(sem=None, vmem=None):
    return pltpu.CompilerParams(dimension_semantics=sem, vmem_limit_bytes=vmem)


def _dot(a, b):
    return jnp.dot(a.astype(_MX), b.astype(_MX), preferred_element_type=_F32)


def _dot_nt(a, b):
    return lax.dot_general(a.astype(_MX), b.astype(_MX), (((1,), (1,)), ((), ())),
                           preferred_element_type=_F32)


def _dot_tn(a, b):
    return lax.dot_general(a.astype(_MX), b.astype(_MX), (((0,), (0,)), ((), ())),
                           preferred_element_type=_F32)


def _dot_ones(ones_mat, v):
    o = ones_mat.astype(jnp.bfloat16)
    v0 = v.astype(jnp.bfloat16)
    r1 = v - v0.astype(_F32)
    v1 = r1.astype(jnp.bfloat16)
    v2 = (r1 - v1.astype(_F32)).astype(jnp.bfloat16)
    d = lambda t: jnp.dot(o, t, preferred_element_type=_F32)
    return d(v0) + d(v1) + d(v2)


def _sigmoid(x):
    return 1.0 / (1.0 + jnp.exp(-x))


def _log_sigmoid(x):
    return jnp.minimum(x, 0.0) - jnp.log(1.0 + jnp.exp(-jnp.abs(x)))


def _lane_tile(t, width):
    reps = width // t.shape[1]
    return t if reps == 1 else jnp.tile(t, (1, reps))


def _rope(t, cos, sa, sb, sign):
    w = t.shape[1]
    rot = pltpu.roll(t, w - 8, 1) * _lane_tile(sa, w) + pltpu.roll(t, 8, 1) * _lane_tile(sb, w)
    return t * _lane_tile(cos, w) + sign * rot


def _rms_bwd(dy_g, n, r):
    return r * (dy_g - n * jnp.mean(dy_g * n, axis=-1, keepdims=True))


ROPE_ROWS = 256


def _rope_consts():
    lane = np.arange(LANE) % A_HEAD_DIM
    half = ROPE_DIM // 2
    inv = np.exp((np.float32(-math.log(ROPE_THETA)) * np.arange(half, dtype=np.float32)) * np.float32(2.0 / ROPE_DIM))
    consts = np.zeros((8, LANE), np.float32)
    consts[0] = np.where(lane < ROPE_DIM, inv[lane % half], 0.0)
    consts[1] = np.where(lane < half, -1.0, 0.0)
    consts[2] = np.where((lane >= half) & (lane < ROPE_DIM), 1.0, 0.0)
    return jnp.asarray(consts)


def _rope_tables_into(pos_ref, c_ref, cos_ref, sa_ref, sb_ref):
    def rows_of(b, carry):
        rows = pl.ds(pl.multiple_of(b * ROPE_ROWS, ROPE_ROWS), ROPE_ROWS)
        ang = pos_ref[rows, :] * c_ref[0:1, :]
        s = jnp.sin(ang)
        cos_ref[rows, :] = jnp.cos(ang)
        sa_ref[rows, :] = s * c_ref[1:2, :]
        sb_ref[rows, :] = s * c_ref[2:3, :]
        return carry

    lax.fori_loop(0, pos_ref.shape[0] // ROPE_ROWS, rows_of, 0)


def _in_proj(x2, g_in, wt, cos, sa, sb):
    t = x2.shape[0]
    tm = min(t, 512)

    def body(x_ref, g_ref, w_ref, cos_ref, sa_ref, sb_ref, h_ref, qkv_ref, za_ref, qkb_ref,
             vb_ref, zb_ref, alr_ref, ga_ref, gb_ref):
        xv = x_ref[...]
        r = lax.rsqrt(jnp.mean(xv * xv, axis=-1, keepdims=True) + NORM_EPS)
        h = (xv * r * g_ref[...]).astype(_MX)
        h_ref[...] = h.astype(_ST)
        outs = dict(za=za_ref, qkb=qkb_ref, vb=vb_ref, zb=zb_ref, alr=alr_ref, ga=ga_ref, gb=gb_ref)
        for name, a, b in PIECES:
            p = _dot_nt(h, w_ref[a:b, :])
            if name == "qkv":
                c, s1, s2 = cos_ref[...], sa_ref[...], sb_ref[...]
                qkv_ref[:, 0:512] = _rope(p[:, 0:512], c, s1, s2, 1.0).astype(_ST)
                qkv_ref[:, 512:640] = _rope(p[:, 512:640], c, s1, s2, 1.0).astype(_ST)
                qkv_ref[:, 640:768] = p[:, 640:768].astype(_ST)
            else:
                outs[name][...] = p.astype(outs[name].dtype)

    rows = lambda w: pl.BlockSpec((tm, w), lambda i: (i, 0))
    shp = lambda name, w: jax.ShapeDtypeStruct((t, w), _F32 if name == "qkb" else _ST)
    widths = [D_MODEL] + [b - a for _, a, b in PIECES]
    return pl.pallas_call(
        body, name="in_proj", grid=(t // tm,),
        in_specs=[rows(D_MODEL), pl.BlockSpec((1, D_MODEL), lambda i: (0, 0)),
                  pl.BlockSpec((D_IN, D_MODEL), lambda i: (0, 0)), rows(LANE), rows(LANE), rows(LANE)],
        out_specs=[rows(w) for w in widths],
        out_shape=[shp(n, w) for n, w in zip(["h"] + [p[0] for p in PIECES], widths)],
        compiler_params=_cparams(("parallel",), VMEM_LIMIT),
    )(x2, g_in, wt, cos, sa, sb)


def _attn_operands(qkv_ref, kvp_ref, want_bwd):
    kf = jnp.concatenate([kvp_ref[:, 0:128], qkv_ref[:, 512:640]], axis=0).astype(_F32) * (A_HEAD_DIM ** -0.5)
    vf = jnp.concatenate([kvp_ref[:, 128:256], qkv_ref[:, 640:768]], axis=0).astype(_F32)
    lo = lax.broadcasted_iota(jnp.int32, (1, LANE), 1) < 64

    def on_lanes(a):
        sw = pltpu.roll(a, 64, 1)
        z = jnp.zeros_like(a)
        return [[jnp.where(lo, a, z).astype(_MX), jnp.where(lo, z, sw).astype(_MX)],
                [jnp.where(lo, sw, z).astype(_MX), jnp.where(lo, z, a).astype(_MX)]]

    def on_rows(a):
        at = a.T.astype(_MX)
        z = jnp.zeros((64, at.shape[1]), _MX)
        top, bot = at[0:64], at[64:128]
        return [[jnp.concatenate([top, z], axis=0), jnp.concatenate([z, top], axis=0)],
                [jnp.concatenate([bot, z], axis=0), jnp.concatenate([z, bot], axis=0)]]

    ops = dict(k_lanes=on_lanes(kf), v_rows=on_rows(vf), lo=lo)
    if want_bwd:
        ops.update(v_lanes=on_lanes(vf), k_rows=on_rows(kf))
    return ops


def _attn_valid(n):
    kj = lax.broadcasted_iota(jnp.int32, (2 * BLOCK, 2 * BLOCK), 0) - BLOCK
    qi = lax.broadcasted_iota(jnp.int32, (2 * BLOCK, 2 * BLOCK), 1) & (BLOCK - 1)
    return (kj <= qi) & (qi - kj < BLOCK) & ((n > 0) | (kj >= 0))


def _attn_sinks(sink_ref, h_a, h_b):
    first = lax.broadcasted_iota(jnp.int32, (1, 2 * BLOCK), 1) < BLOCK
    return jnp.where(first, sink_ref[h_a], sink_ref[h_b])


def _attn_softmax_t(k_lanes, q_pair, valid, sink):
    s = jnp.where(valid, _dot_nt(k_lanes, q_pair), NEG_BIG)
    m = jnp.maximum(jnp.max(s, axis=0, keepdims=True), sink)
    e = jnp.exp(s - m)
    e_sink = jnp.exp(sink - m)
    inv = 1.0 / (jnp.sum(e, axis=0, keepdims=True) + e_sink)
    return e, e_sink, inv


def _attn_fwd(qkv, za, sinks, nseq):
    t = qkv.shape[0]
    nb = t // nseq // BLOCK

    def body(sink_ref, qkv_ref, kvp_ref, za_ref, oa_ref):
        n = pl.program_id(1)
        ops = _attn_operands(qkv_ref, kvp_ref, False)
        valid = _attn_valid(n)[:, 0:BLOCK]
        for pr in range(A_HEADS // 2):
            lanes = slice(pr * LANE, (pr + 1) * LANE)
            g = pr // (A_GROUP // 2)
            q_pair = qkv_ref[:, lanes]
            ot = None
            for half in range(2):
                e, _, inv = _attn_softmax_t(ops["k_lanes"][g][half], q_pair, valid, sink_ref[2 * pr + half])
                part = _dot(ops["v_rows"][g][half], e) * inv
                ot = part if ot is None else ot + part
            z = za_ref[:, lanes].astype(_F32)
            oa_ref[:, lanes] = (ot.T * (z * _sigmoid(z))).astype(_ST)

    cur = lambda w: pl.BlockSpec((BLOCK, w), lambda s, n: (s * nb + n, 0))
    return pl.pallas_call(
        body, name="attn_fwd", grid=(nseq, nb),
        in_specs=[pl.BlockSpec(memory_space=pltpu.SMEM), cur(768),
                  pl.BlockSpec((BLOCK, 256), lambda s, n: (s * nb + jnp.maximum(n - 1, 0), 2)), cur(512)],
        out_specs=cur(512), out_shape=jax.ShapeDtypeStruct((t, A_WIDTH), _ST),
        compiler_params=_cparams(("parallel", "arbitrary")),
    )(sinks, qkv, qkv, za)


def _attn_bwd(qkv, za, doa, sinks, cos, sa, sb, nseq):
    t = qkv.shape[0]
    nb = t // nseq // BLOCK

    def body(sink_ref, qkv_ref, kvp_ref, za_ref, doa_ref, cos_ref, sa_ref, sb_ref,
             dqkv_ref, dza_ref, dsink_ref, ck_ref, cv_ref):
        s_id, i = pl.program_id(0), pl.program_id(1)
        n = nb - 1 - i

        @pl.when((s_id == 0) & (i == 0))
        def _():
            dsink_ref[...] = jnp.zeros_like(dsink_ref)

        @pl.when(i == 0)
        def _():
            ck_ref[...] = jnp.zeros_like(ck_ref)
            cv_ref[...] = jnp.zeros_like(cv_ref)

        ops = _attn_operands(qkv_ref, kvp_ref, True)
        lo = ops["lo"]
        valid = _attn_valid(n)
        dk_acc, dv_acc, dq_pairs = [], [], []
        for g in range(A_KV_HEADS):
            pairs = [slice((2 * g + j) * LANE, (2 * g + j + 1) * LANE) for j in range(2)]
            q_both = jnp.concatenate([qkv_ref[:, p] for p in pairs], axis=0)
            q_f = q_both.astype(_F32)
            z = [za_ref[:, p].astype(_F32) for p in pairs]
            sz = [_sigmoid(t) for t in z]
            d_oa = [doa_ref[:, p].astype(_F32) for p in pairs]
            d_att = jnp.concatenate([d_oa[j] * (z[j] * sz[j]) for j in range(2)], axis=0)
            zero = jnp.zeros_like(d_att)
            ot, dqt, ds_all, pn_all, qz_all, daz_all = None, None, [], [], [], []
            for half in range(2):
                heads = (4 * g + half, 4 * g + 2 + half)
                e, e_sink, inv = _attn_softmax_t(ops["k_lanes"][g][half], q_both, valid,
                                                 _attn_sinks(sink_ref, *heads))
                pn = e * inv
                dpt = _dot_nt(ops["v_lanes"][g][half], d_att)
                delta = jnp.sum(pn * dpt, axis=0, keepdims=True)
                ds = (pn * (dpt - delta)).astype(_MX)
                pn = pn.astype(_MX)
                d_sink = e_sink * inv * delta
                for j, h in enumerate(heads):
                    dsink_ref[h:h + 1, :] = dsink_ref[h:h + 1, :] - jnp.sum(d_sink[:, j * BLOCK:(j + 1) * BLOCK])
                o_part = _dot(ops["v_rows"][g][half], pn)
                dq_part = _dot(ops["k_rows"][g][half], ds)
                ot = o_part if ot is None else ot + o_part
                dqt = dq_part if dqt is None else dqt + dq_part
                mine = lo if half == 0 else jnp.logical_not(lo)
                ds_all.append(ds)
                pn_all.append(pn)
                qz_all.append(jnp.where(mine, q_f, zero).astype(_MX))
                daz_all.append(jnp.where(mine, d_att, zero).astype(_MX))
            dk_acc.append(_dot(jnp.concatenate(ds_all, axis=1), jnp.concatenate(qz_all, axis=0)))
            dv_acc.append(_dot(jnp.concatenate(pn_all, axis=1), jnp.concatenate(daz_all, axis=0)))
            for j, p in enumerate(pairs):
                cols = slice(j * BLOCK, (j + 1) * BLOCK)
                dza_ref[:, p] = (d_oa[j] * ot[:, cols].T * (sz[j] * (1.0 + z[j] * (1.0 - sz[j])))).astype(_ST)
                dq_pairs.append(dqt[:, cols].T)

        def fold(acc, scale):
            both = [a + pltpu.roll(a, 64, 1) for a in acc]
            return jnp.where(lo, both[0], both[1]) * scale

        dk_full = fold(dk_acc, A_HEAD_DIM ** -0.5)
        dv_full = fold(dv_acc, 1.0)
        dk_cur = dk_full[BLOCK:] + ck_ref[...]
        dv_cur = dv_full[BLOCK:] + cv_ref[...]
        ck_ref[...] = dk_full[:BLOCK]
        cv_ref[...] = dv_full[:BLOCK]
        c, s1, s2 = cos_ref[...], sa_ref[...], sb_ref[...]
        dqkv_ref[:, 0:512] = _rope(jnp.concatenate(dq_pairs, axis=1), c, s1, s2, -1.0).astype(_ST)
        dqkv_ref[:, 512:640] = _rope(dk_cur, c, s1, s2, -1.0).astype(_ST)
        dqkv_ref[:, 640:768] = dv_cur.astype(_ST)

    cur = lambda w: pl.BlockSpec((BLOCK, w), lambda s, i: (s * nb + nb - 1 - i, 0))
    return pl.pallas_call(
        body, name="attn_bwd", grid=(nseq, nb),
        in_specs=[pl.BlockSpec(memory_space=pltpu.SMEM), cur(768),
                  pl.BlockSpec((BLOCK, 256), lambda s, i: (s * nb + jnp.maximum(nb - 2 - i, 0), 2)),
                  cur(512), cur(512), cur(LANE), cur(LANE), cur(LANE)],
        out_specs=[cur(768), cur(512), pl.BlockSpec((8, LANE), lambda s, i: (0, 0))],
        out_shape=[jax.ShapeDtypeStruct((t, 768), _ST), jax.ShapeDtypeStruct((t, 512), _ST),
                   jax.ShapeDtypeStruct((8, LANE), _F32)],
        scratch_shapes=[pltpu.VMEM((BLOCK, A_KV_WIDTH), _F32), pltpu.VMEM((BLOCK, A_KV_WIDTH), _F32)],
        compiler_params=_cparams(("arbitrary", "arbitrary")),
    )(sinks, qkv, qkv, za, doa, cos, sa, sb)


def _gla_chunk_terms(la, qkb_ref, r0):
    g = la[r0:r0 + B_CHUNK, :]
    ri = lax.broadcasted_iota(jnp.int32, (B_CHUNK, B_CHUNK), 0)
    ci = lax.broadcasted_iota(jnp.int32, (B_CHUNK, B_CHUNK), 1)
    cum = _dot_ones((ri >= ci).astype(_F32), g)
    last = cum[B_CHUNK - 1:B_CHUNK, :]
    mid = cum[B_CHUNK // 2 - 1:B_CHUNK // 2, :]
    q = qkb_ref[r0:r0 + B_CHUNK, 0:B_KEY_WIDTH].astype(_F32) * (B_KEY_DIM ** -0.5)
    k = qkb_ref[r0:r0 + B_CHUNK, B_KEY_WIDTH:2 * B_KEY_WIDTH].astype(_F32)
    e_q, e_k, e_l, e_c = jnp.exp(cum - mid), jnp.exp(mid - cum), jnp.exp(last - cum), jnp.exp(cum)
    dec_col = jnp.exp(jnp.sum(g.T, axis=1, keepdims=True))
    return dict(qm=q * e_q, km=k * e_k, kl=k * e_l, qc=q * e_c, e_q=e_q, e_k=e_k, e_l=e_l, e_c=e_c,
                dec_col=dec_col, dec_row=jnp.exp(last), causal=ri >= ci, ri=ri)


def _gate_logits(alr_ref, wup_ref, b_ref):
    return _dot(alr_ref[...], wup_ref[...]) + b_ref[...]


def _gla_fwd(qkb, vb, zb, alr, wup, b_alpha, gn, nseq):
    t = qkb.shape[0]
    tb = min(GLA_BLOCK, t // nseq)
    nblk = t // nseq // tb
    cpb = tb // B_CHUNK

    def body(qkb_ref, vb_ref, zb_ref, alr_ref, wup_ref, b_ref, gn_ref, ob_ref, oraw_ref, sst_ref, s_ref):
        @pl.when(pl.program_id(1) == 0)
        def _():
            s_ref[...] = jnp.zeros_like(s_ref)

        la = _log_sigmoid(_gate_logits(alr_ref, wup_ref, b_ref)) * (1.0 / B_GATE_TEMP)
        terms = [_gla_chunk_terms(la, qkb_ref, c * B_CHUNK) for c in range(cpb)]
        o_intra, inc = {}, {}
        for c, tm in enumerate(terms):
            for h in range(B_HEADS):
                kl_, vl_ = slice(h * 64, (h + 1) * 64), slice(h * 128, (h + 1) * 128)
                v = vb_ref[c * B_CHUNK:(c + 1) * B_CHUNK, vl_]
                a = jnp.where(tm["causal"], _dot_nt(tm["qm"][:, kl_], tm["km"][:, kl_]), 0.0)
                o_intra[c, h] = _dot(a, v)
                inc[c, h] = _dot_tn(tm["kl"][:, kl_], v)
        o_heads = {}
        for h in range(B_HEADS):
            kl_ = slice(h * 64, (h + 1) * 64)
            st = s_ref[kl_, :]
            for c, tm in enumerate(terms):
                sst_ref[c, kl_, :] = st
                o_heads[c, h] = o_intra[c, h] + _dot(tm["qc"][:, kl_], st)
                st = tm["dec_col"][kl_, :] * st + inc[c, h]
            s_ref[kl_, :] = st
        o = jnp.concatenate([jnp.concatenate([o_heads[c, h] for h in range(B_HEADS)], axis=1)
                             for c in range(cpb)], axis=0)
        oraw_ref[...] = o
        z = zb_ref[...].astype(_F32)
        gate = z * _sigmoid(z)
        for h in range(B_HEADS):
            vl_ = slice(h * 128, (h + 1) * 128)
            oh = o[:, vl_]
            r = lax.rsqrt(jnp.mean(oh * oh, axis=-1, keepdims=True) + NORM_EPS)
            ob_ref[:, vl_] = ((oh * r) * gn_ref[:, vl_] * gate[:, vl_]).astype(_ST)

    rows = lambda w: pl.BlockSpec((tb, w), lambda s, i: (s * nblk + i, 0))
    full = lambda a, b: pl.BlockSpec((a, b), lambda s, i: (0, 0))
    return pl.pallas_call(
        body, name="gla_fwd", grid=(nseq, nblk),
        in_specs=[rows(512), rows(512), rows(512), rows(LANE), full(LANE, B_KEY_WIDTH),
                  full(1, B_KEY_WIDTH), full(1, B_WIDTH)],
        out_specs=[rows(512), rows(512),
                   pl.BlockSpec((cpb, B_KEY_WIDTH, B_VAL_DIM), lambda s, i: (s * nblk + i, 0, 0))],
        out_shape=[jax.ShapeDtypeStruct((t, B_WIDTH), _ST), jax.ShapeDtypeStruct((t, B_WIDTH), _F32),
                   jax.ShapeDtypeStruct((t // B_CHUNK, B_KEY_WIDTH, B_VAL_DIM), _F32)],
        scratch_shapes=[pltpu.VMEM((B_KEY_WIDTH, B_VAL_DIM), _F32)],
        compiler_params=_cparams(("parallel", "arbitrary")),
    )(qkb, vb, zb, alr, wup, b_alpha, gn)


def _gla_bwd(qkb, vb, zb, alr, oraw, dob, sst, wup, b_alpha, gn, nseq):
    t = qkb.shape[0]
    tb = min(GLA_BLOCK, t // nseq)
    nblk = t // nseq // tb
    cpb = tb // B_CHUNK

    def body(qkb_ref, vb_ref, zb_ref, alr_ref, oraw_ref, dob_ref, sst_ref, wup_ref, b_ref, gn_ref,
             dqkb_ref, dvb_ref, dzb_ref, dalr_ref, dwup_ref, db_ref, dgn_ref, ds_ref):
        s_id, i = pl.program_id(0), pl.program_id(1)

        @pl.when((s_id == 0) & (i == 0))
        def _():
            dwup_ref[...] = jnp.zeros_like(dwup_ref)
            db_ref[...] = jnp.zeros_like(db_ref)
            dgn_ref[...] = jnp.zeros_like(dgn_ref)

        @pl.when(i == 0)
        def _():
            ds_ref[...] = jnp.zeros_like(ds_ref)

        a_pre = _gate_logits(alr_ref, wup_ref, b_ref)
        la = _log_sigmoid(a_pre) * (1.0 / B_GATE_TEMP)

        z = zb_ref[...].astype(_F32)
        sz = _sigmoid(z)
        d_ob = dob_ref[...].astype(_F32)
        tg = d_ob * (z * sz)
        dsilu = sz * (1.0 + z * (1.0 - sz))
        do_cols, dgn_cols = [], []
        for h in range(B_HEADS):
            vl_ = slice(h * 128, (h + 1) * 128)
            oh = oraw_ref[:, vl_].astype(_F32)
            r = lax.rsqrt(jnp.mean(oh * oh, axis=-1, keepdims=True) + NORM_EPS)
            on = oh * r
            gnh = gn_ref[:, vl_]
            dzb_ref[:, vl_] = (d_ob[:, vl_] * (on * gnh) * dsilu[:, vl_]).astype(_ST)
            dgn_cols.append(jnp.sum(tg[:, vl_] * on, axis=0, keepdims=True))
            do_cols.append(_rms_bwd(tg[:, vl_] * gnh, on, r))
        dgn_ref[...] = dgn_ref[...] + jnp.concatenate(dgn_cols, axis=1)
        d_o = jnp.concatenate(do_cols, axis=1)

        ri = lax.broadcasted_iota(jnp.int32, (tb, tb), 0)
        ci = lax.broadcasted_iota(jnp.int32, (tb, tb), 1)
        same = (ri // B_CHUNK) == (ci // B_CHUNK)
        low = same & (ri >= ci)
        upto_mid = same & ((ci % B_CHUNK) < B_CHUNK // 2)
        sums = _dot_ones(jnp.concatenate([m.astype(_F32) for m in (low, same, upto_mid)], axis=0), la)
        cum, last, mid = sums[0:tb], sums[tb:2 * tb], sums[2 * tb:3 * tb]
        e_q, e_k, e_l, e_c = jnp.exp(cum - mid), jnp.exp(mid - cum), jnp.exp(last - cum), jnp.exp(cum)
        q = qkb_ref[:, 0:B_KEY_WIDTH] * (B_KEY_DIM ** -0.5)
        k = qkb_ref[:, B_KEY_WIDTH:2 * B_KEY_WIDTH]
        qm, km, kl, qc = q * e_q, k * e_k, k * e_l, q * e_c
        lane_head = lax.broadcasted_iota(jnp.int32, (1, B_KEY_WIDTH), 1) // B_KEY_DIM
        d_o_mx = d_o.astype(_MX)

        def on_diagonal(st):
            z = jnp.zeros((B_KEY_DIM, B_VAL_DIM), st.dtype)
            return jnp.concatenate([jnp.concatenate(
                [st[h * B_KEY_DIM:(h + 1) * B_KEY_DIM] if g == h else z for g in range(B_HEADS)], axis=1)
                for h in range(B_HEADS)], axis=0)

        def diagonal_of(full):
            return jnp.concatenate([full[h * B_KEY_DIM:(h + 1) * B_KEY_DIM, h * B_VAL_DIM:(h + 1) * B_VAL_DIM]
                                    for h in range(B_HEADS)], axis=0)

        dqm, dkm, dv_cols = None, None, []
        for h in range(B_HEADS):
            vl_ = slice(h * B_VAL_DIM, (h + 1) * B_VAL_DIM)
            mine = lane_head == h
            qz, kz = jnp.where(mine, qm, 0.0).astype(_MX), jnp.where(mine, km, 0.0).astype(_MX)
            a = jnp.where(low, _dot_nt(qz, kz), 0.0).astype(_MX)
            da = jnp.where(low, _dot_nt(d_o_mx[:, vl_], vb_ref[:, vl_]), 0.0).astype(_MX)
            dqm_h, dkm_h = _dot(da, kz), _dot_tn(da, qz)
            dqm = dqm_h if dqm is None else dqm + dqm_h
            dkm = dkm_h if dkm is None else dkm + dkm_h
            dv_cols.append(_dot_tn(a, d_o_mx[:, vl_]))
        dv = jnp.concatenate(dv_cols, axis=1)

        chunk = [slice(c * B_CHUNK, (c + 1) * B_CHUNK) for c in range(cpb)]
        dqc_rows, g_loc = [], []
        for c in range(cpb):
            dqc_rows.append(_dot_nt(d_o_mx[chunk[c]], on_diagonal(sst_ref[c].astype(_MX))))
            g_loc.append(diagonal_of(_dot_tn(qc[chunk[c]], d_o_mx[chunk[c]])))
        cur = ds_ref[...]
        d_state = [None] * cpb
        for c in reversed(range(cpb)):
            d_state[c] = cur
            cur = g_loc[c] + jnp.exp(jnp.sum(la[chunk[c]].T, axis=1, keepdims=True)) * cur
        ds_ref[...] = cur
        dkl_rows, dv_rows, dlast_rows = [], [], []
        ones8 = jnp.ones((8, B_VAL_DIM), _F32)
        for c in range(cpb):
            dsd = on_diagonal(d_state[c].astype(_MX))
            dkl_c = _dot_nt(vb_ref[chunk[c], :], dsd)
            dkl_rows.append(dkl_c)
            dv_rows.append(_dot(kl[chunk[c]], dsd))
            prod = d_state[c] * sst_ref[c]
            p0 = prod.astype(jnp.bfloat16)
            p1 = (prod - p0.astype(_F32)).astype(jnp.bfloat16)
            p2 = (prod - p0.astype(_F32) - p1.astype(_F32)).astype(jnp.bfloat16)
            ddec = (_dot_nt(ones8, p0) + _dot_nt(ones8, p1) + _dot_nt(ones8, p2))[0:1]
            r_last = c * B_CHUNK + B_CHUNK - 1
            dlast = jnp.sum(dkl_c * kl[chunk[c]], axis=0, keepdims=True) + ddec * jnp.exp(last[r_last:r_last + 1])
            dlast_rows.append(jnp.broadcast_to(dlast, (B_CHUNK, B_KEY_WIDTH)))
        dqc, dkl = jnp.concatenate(dqc_rows, axis=0), jnp.concatenate(dkl_rows, axis=0)
        dqkb_ref[:, 0:B_KEY_WIDTH] = ((dqm * e_q + dqc * e_c) * (B_KEY_DIM ** -0.5)).astype(_ST)
        dqkb_ref[:, B_KEY_WIDTH:2 * B_KEY_WIDTH] = (dkm * e_k + dkl * e_l).astype(_ST)
        dvb_ref[...] = (dv + jnp.concatenate(dv_rows, axis=0)).astype(_ST)
        dcum = dqm * qm - dkm * km + dqc * qc - dkl * kl
        row = lax.broadcasted_iota(jnp.int32, (tb, B_KEY_WIDTH), 0)
        dcum = jnp.where(row % B_CHUNK == B_CHUNK - 1, dcum + jnp.concatenate(dlast_rows, axis=0), dcum)
        dla = _dot_ones((same & (ri <= ci)).astype(_F32), dcum)

        da_pre = dla * (1.0 / B_GATE_TEMP) * (1.0 - _sigmoid(a_pre))
        dalr_ref[...] = _dot_nt(da_pre, wup_ref[...]).astype(_ST)
        dwup_ref[...] = dwup_ref[...] + _dot_tn(alr_ref[...], da_pre)
        db_ref[...] = db_ref[...] + jnp.sum(da_pre, axis=0, keepdims=True)

    blk = lambda s, i: s * nblk + nblk - 1 - i
    rows = lambda w: pl.BlockSpec((tb, w), lambda s, i: (blk(s, i), 0))
    full = lambda a, b: pl.BlockSpec((a, b), lambda s, i: (0, 0))
    act = lambda w: jax.ShapeDtypeStruct((t, w), _ST)
    return pl.pallas_call(
        body, name="gla_bwd", grid=(nseq, nblk),
        in_specs=[rows(512), rows(512), rows(512), rows(LANE), rows(512), rows(512),
                  pl.BlockSpec((cpb, B_KEY_WIDTH, B_VAL_DIM), lambda s, i: (blk(s, i), 0, 0)),
                  full(LANE, B_KEY_WIDTH), full(1, B_KEY_WIDTH), full(1, B_WIDTH)],
        out_specs=[rows(512), rows(512), rows(512), rows(LANE), full(LANE, B_KEY_WIDTH),
                   full(1, B_KEY_WIDTH), full(1, B_WIDTH)],
        out_shape=[act(512), act(512), act(512), act(LANE),
                   jax.ShapeDtypeStruct((LANE, B_KEY_WIDTH), _F32),
                   jax.ShapeDtypeStruct((1, B_KEY_WIDTH), _F32), jax.ShapeDtypeStruct((1, B_WIDTH), _F32)],
        scratch_shapes=[pltpu.VMEM((B_KEY_WIDTH, B_VAL_DIM), _F32)],
        compiler_params=_cparams(("arbitrary", "arbitrary")),
    )(qkb, vb, zb, alr, oraw, dob, sst, wup, b_alpha, gn)


def _merge_loss(oa, ob, ga, gb, x2, tgt, wa, wb, wo, g_final):
    t = x2.shape[0]
    tm = min(t, 256)
    nt = t // tm

    def body(oa_ref, ob_ref, ga_ref, gb_ref, x_ref, t_ref, wa_ref, wb_ref, wo_ref, gf_ref,
             dh_ref, doa_ref, dob_ref, dga_ref, dgb_ref, dwa_ref, dwb_ref, dwo_ref, dgf_ref, loss_ref,
             ya_s, yb_s, out_s, dmer_s, mrg_s, dya_s, dyb_s):
        i = pl.program_id(0)

        @pl.when(i == 0)
        def _():
            dwa_ref[...] = jnp.zeros_like(dwa_ref)
            dwb_ref[...] = jnp.zeros_like(dwb_ref)
            dwo_ref[...] = jnp.zeros_like(dwo_ref)
            dgf_ref[...] = jnp.zeros_like(dgf_ref)
            loss_ref[...] = jnp.zeros_like(loss_ref)

        slabs = [slice(s, s + MERGE_SLAB) for s in range(0, tm, MERGE_SLAB)]
        fold = lambda a: a[0:8] + a[8:16]
        ya_s[...] = _dot(oa_ref[...], wa_ref[...])
        yb_s[...] = _dot(ob_ref[...], wb_ref[...])
        for rows_ in slabs:
            sga, sgb = _sigmoid(ga_ref[rows_, :].astype(_F32)), _sigmoid(gb_ref[rows_, :].astype(_F32))
            mrg_s[rows_, :] = (sga * ya_s[rows_, :] + sgb * yb_s[rows_, :]).astype(_MX)
        out_s[...] = x_ref[...] + _dot(mrg_s[...], wo_ref[...])
        gf = gf_ref[...]
        loss8 = jnp.zeros((8, D_MODEL), _F32)
        dgf8 = jnp.zeros((8, D_MODEL), _F32)
        for rows_ in slabs:
            out = out_s[rows_, :]
            r = lax.rsqrt(jnp.mean(out * out, axis=-1, keepdims=True) + NORM_EPS)
            nrm = out * r
            err = nrm * gf - t_ref[rows_, :]
            loss8 = loss8 + fold(err * err)
            dy = err * (1.0 / D_MODEL)
            dgf8 = dgf8 + fold(dy * nrm)
            dh = _rms_bwd(dy * gf, nrm, r)
            dh_ref[rows_, :] = dh.astype(_ST)
        loss_ref[...] = loss_ref[...] + (0.5 / D_MODEL) * jnp.sum(loss8, axis=0, keepdims=True)
        dgf_ref[...] = dgf_ref[...] + jnp.sum(dgf8, axis=0, keepdims=True)
        dmer_s[...] = _dot_nt(dh_ref[...], wo_ref[...])
        dwo_ref[...] = dwo_ref[...] + _dot_tn(mrg_s[...], dh_ref[...])
        for rows_ in slabs:
            sga, sgb = _sigmoid(ga_ref[rows_, :].astype(_F32)), _sigmoid(gb_ref[rows_, :].astype(_F32))
            dmer = dmer_s[rows_, :]
            da, db = dmer * sga, dmer * sgb
            dya_s[rows_, :] = da.astype(_MX)
            dyb_s[rows_, :] = db.astype(_MX)
            dga_ref[rows_, :] = (da * ya_s[rows_, :] * (1.0 - sga)).astype(_ST)
            dgb_ref[rows_, :] = (db * yb_s[rows_, :] * (1.0 - sgb)).astype(_ST)
        doa_ref[...] = _dot_nt(dya_s[...], wa_ref[...]).astype(_ST)
        dob_ref[...] = _dot_nt(dyb_s[...], wb_ref[...]).astype(_ST)
        dwa_ref[...] = dwa_ref[...] + _dot_tn(oa_ref[...], dya_s[...])
        dwb_ref[...] = dwb_ref[...] + _dot_tn(ob_ref[...], dyb_s[...])

    rows = lambda w: pl.BlockSpec((tm, w), lambda i: (i, 0))
    full = lambda a, b: pl.BlockSpec((a, b), lambda i: (0, 0))
    return pl.pallas_call(
        body, name="merge_loss", grid=(nt,),
        in_specs=[rows(512), rows(512), rows(D_MODEL), rows(D_MODEL), rows(D_MODEL), rows(D_MODEL),
                  full(A_WIDTH, D_MODEL), full(B_WIDTH, D_MODEL), full(D_MODEL, D_MODEL), full(1, D_MODEL)],
        out_specs=[rows(D_MODEL), rows(512), rows(512), rows(D_MODEL), rows(D_MODEL),
                   full(A_WIDTH, D_MODEL), full(B_WIDTH, D_MODEL), full(D_MODEL, D_MODEL),
                   full(1, D_MODEL), full(1, D_MODEL)],
        out_shape=[jax.ShapeDtypeStruct((t, D_MODEL), _ST), jax.ShapeDtypeStruct((t, 512), _ST),
                   jax.ShapeDtypeStruct((t, 512), _ST), jax.ShapeDtypeStruct((t, D_MODEL), _ST),
                   jax.ShapeDtypeStruct((t, D_MODEL), _ST),
                   jax.ShapeDtypeStruct((A_WIDTH, D_MODEL), _F32), jax.ShapeDtypeStruct((B_WIDTH, D_MODEL), _F32),
                   jax.ShapeDtypeStruct((D_MODEL, D_MODEL), _F32), jax.ShapeDtypeStruct((1, D_MODEL), _F32),
                   jax.ShapeDtypeStruct((1, D_MODEL), _F32)],
        scratch_shapes=[pltpu.VMEM((tm, D_MODEL), _F32)] * 4 + [pltpu.VMEM((tm, D_MODEL), _MX)] * 3,
        compiler_params=_cparams(("arbitrary",), VMEM_LIMIT),
    )(oa, ob, ga, gb, x2, tgt, wa, wb, wo, g_final)


def _in_proj_bwd_x(dpieces, wt, x2, dh2, g_in):
    t = x2.shape[0]
    tm = min(t, 256)
    np_ = len(PIECES)

    def body(*refs):
        dp_refs = refs[:np_]
        w_ref, x_ref, dh2_ref, g_ref, gx_ref, dg_ref = refs[np_:]

        @pl.when(pl.program_id(0) == 0)
        def _():
            dg_ref[...] = jnp.zeros_like(dg_ref)

        dh = None
        for (name, a, b), dp in zip(PIECES, dp_refs):
            part = _dot(dp[...], w_ref[a:b, :])
            dh = part if dh is None else dh + part
        xv = x_ref[...]
        r = lax.rsqrt(jnp.mean(xv * xv, axis=-1, keepdims=True) + NORM_EPS)
        nrm = xv * r
        dg_ref[...] = dg_ref[...] + jnp.sum(dh * nrm, axis=0, keepdims=True)
        gx_ref[...] = dh2_ref[...].astype(_F32) + _rms_bwd(dh * g_ref[...], nrm, r)

    rows = lambda w: pl.BlockSpec((tm, w), lambda i: (i, 0))
    full = lambda a, b: pl.BlockSpec((a, b), lambda i: (0, 0))
    return pl.pallas_call(
        body, name="in_proj_bwd_x", grid=(t // tm,),
        in_specs=[rows(b - a) for _, a, b in PIECES] + [full(D_IN, D_MODEL), rows(D_MODEL), rows(D_MODEL),
                                                          full(1, D_MODEL)],
        out_specs=[rows(D_MODEL), full(1, D_MODEL)],
        out_shape=[jax.ShapeDtypeStruct((t, D_MODEL), _F32), jax.ShapeDtypeStruct((1, D_MODEL), _F32)],
        compiler_params=_cparams(("arbitrary",), VMEM_LIMIT),
    )(*dpieces, wt, x2, dh2, g_in)


def _in_proj_bwd_w(h, dpieces):
    t = h.shape[0]
    tm = min(t, 512)
    nt = t // tm
    np_ = len(PIECES)

    def body(*refs):
        h_ref, dp_refs, out_ref = refs[0], refs[1:1 + np_], refs[1 + np_]
        acc_ref, sem = refs[2 + np_:]
        i = pl.program_id(0)

        @pl.when(i == 0)
        def _():
            acc_ref[...] = jnp.zeros_like(acc_ref)

        hv = h_ref[...]
        writes = []
        for j, ((name, a, b), dp) in enumerate(zip(PIECES, dp_refs)):
            part = _dot_tn(dp[...], hv)
            if name == "alr":
                b = a + B_GATE_RANK
                part = part[0:B_GATE_RANK]
            acc_ref[a:b, :] = acc_ref[a:b, :] + part
            writes.append(pltpu.make_async_copy(acc_ref.at[a:b], out_ref.at[a:b], sem.at[j]))

            @pl.when(i == nt - 1)
            def _(cp=writes[-1]):
                cp.start()

        @pl.when(i == nt - 1)
        def _():
            for cp in writes:
                cp.wait()

    rows = lambda w: pl.BlockSpec((tm, w), lambda i: (i, 0))
    return pl.pallas_call(
        body, name="in_proj_bwd_w", grid=(nt,),
        in_specs=[rows(D_MODEL)] + [rows(b - a) for _, a, b in PIECES],
        out_specs=_ANY, out_shape=jax.ShapeDtypeStruct((D_IN, D_MODEL), _F32),
        scratch_shapes=[pltpu.VMEM((D_IN, D_MODEL), _F32), pltpu.SemaphoreType.DMA((np_,))],
        compiler_params=_cparams(("arbitrary",), VMEM_LIMIT),
    )(h, *dpieces)


def _place():
    return lax.axis_index("x"), lax.axis_index("y"), lax.axis_index("c")


def _other_chips(x, y):
    return [(1 - x, y), (x, 1 - y), (1 - x, 1 - y)]


class _Split(NamedTuple):
    by_rows: bool
    step: int
    size: int

    def half(self, ref, c):
        r, n = ref.shape[-2:]
        if self.by_rows:
            return ref.at[:, pl.ds(pl.multiple_of(c * (n // 2), LANE), n // 2)]
        return ref.at[pl.ds(pl.multiple_of(c * (r // 2), 16), r // 2), :]

    def chip_part(self, ref, k):
        if self.by_rows:
            return ref.at[pl.ds(pl.multiple_of(k * self.step, 16), self.size), :]
        return ref.at[:, pl.ds(pl.multiple_of(k * self.size, LANE), self.size)]

    def half_shape(self, shape):
        r, n = shape
        return (r, n // 2) if self.by_rows else (r // 2, n)

    def part_shape(self, shape):
        r, n = shape
        return (self.size, n) if self.by_rows else (r, self.size)


SPLIT_W_IN_T = _Split(True, WINDOW_STEP, WINDOW_ROWS)
SPLIT_W_O = _Split(True, 256, 256)
SPLIT_W_OUT = _Split(False, 256, 256)


def _gather_weights(shards, splits, fulls, pos_f):
    nw = len(shards)
    t = pos_f.shape[0]

    def body(*refs):
        ins, (pos_ref, c_ref) = refs[:nw], refs[nw:nw + 2]
        outs, tables = refs[nw + 2:2 * nw + 2], refs[2 * nw + 2:2 * nw + 5]
        send_a, recv_a, send_b, recv_b = refs[2 * nw + 5:]
        x, y, c = _place()
        me = 2 * x + y
        peers = _other_chips(x, y)

        def place(i, k, half):
            if splits[i] is None:
                return outs[i].at[k]
            if fulls[i][0] == 4 and len(fulls[i]) == 3:
                whole = outs[i].at[k]
            else:
                whole = splits[i].chip_part(outs[i], k)
            return splits[i].half(whole, half)

        first, passed = [], []
        for i in range(nw):
            src = ins[i] if splits[i] is None else splits[i].half(ins[i], c)
            for j, (px, py) in enumerate(peers):
                cp = pltpu.make_async_remote_copy(
                    src_ref=src, dst_ref=place(i, me, c), send_sem=send_a.at[3 * i + j],
                    recv_sem=recv_a.at[3 * i + j], device_id=(px, py, c), device_id_type=_MESH)
                cp.start()
                first.append(cp)
        _rope_tables_into(pos_ref, c_ref, *tables)
        for i in range(nw):
            for j, (px, py) in enumerate(peers):
                landed = place(i, 2 * px + py, c)
                pltpu.make_async_remote_copy(
                    src_ref=landed, dst_ref=landed, send_sem=send_a.at[3 * i + j], recv_sem=recv_a.at[3 * i + j],
                    device_id=(px, py, c), device_id_type=_MESH).wait_recv()
                if splits[i] is not None:
                    cp = pltpu.make_async_remote_copy(
                        src_ref=landed, dst_ref=landed, send_sem=send_b.at[3 * i + j], recv_sem=recv_b.at[3 * i + j],
                        device_id=(x, y, 1 - c), device_id_type=_MESH)
                    cp.start()
                    passed.append(cp)
        for i in range(nw):
            if splits[i] is None:
                continue
            for j, (px, py) in enumerate(peers):
                theirs = place(i, 2 * px + py, 1 - c)
                pltpu.make_async_remote_copy(
                    src_ref=theirs, dst_ref=theirs, send_sem=send_b.at[3 * i + j], recv_sem=recv_b.at[3 * i + j],
                    device_id=(x, y, 1 - c), device_id_type=_MESH).wait_recv()
        for cp in first + passed:
            cp.wait_send()

    vm = pl.BlockSpec(memory_space=pltpu.VMEM)
    tab = jax.ShapeDtypeStruct((t, LANE), _F32)
    return pl.pallas_call(
        body, name="gather_weights",
        in_specs=[_ANY] * nw + [vm, vm], out_specs=[_ANY] * nw + [vm] * 3,
        out_shape=[jax.ShapeDtypeStruct(f, s.dtype) for f, s in zip(fulls, shards)] + [tab] * 3,
        scratch_shapes=[pltpu.SemaphoreType.DMA((3 * nw,)) for _ in range(4)],
        compiler_params=_cparams(None, VMEM_LIMIT),
    )(*shards, pos_f, _rope_consts())


def _assemble_w_in_t(slots):
    bw = 256
    ov = WINDOW_ROWS - WINDOW_STEP

    def body(s_ref, o_ref):
        for k in range(4):
            base = k * WINDOW_STEP
            lo = 0 if k == 0 else ov
            if k > 0:
                o_ref[base:base + ov, :] = s_ref[k - 1, WINDOW_STEP:WINDOW_ROWS, :] + s_ref[k, 0:ov, :]
            hi = WINDOW_ROWS if k == 3 else WINDOW_STEP
            o_ref[base + lo:base + hi, :] = s_ref[k, lo:hi, :]

    return pl.pallas_call(
        body, name="assemble_w_in_t", grid=(D_MODEL // bw,),
        in_specs=[pl.BlockSpec((4, WINDOW_ROWS, bw), lambda i: (0, 0, i))],
        out_specs=pl.BlockSpec((D_IN, bw), lambda i: (0, i)),
        out_shape=jax.ShapeDtypeStruct((D_IN, D_MODEL), slots.dtype),
        compiler_params=_cparams(("parallel",)),
    )(slots)


def _pair_exchange(grads, splits):
    nw = len(grads)

    def body(*refs):
        ins, outs = refs[:nw], refs[nw:2 * nw]
        send, recv = refs[2 * nw:]
        x, y, c = _place()
        copies = []
        for i in range(nw):
            cp = pltpu.make_async_remote_copy(
                src_ref=splits[i].half(ins[i], 1 - c), dst_ref=outs[i], send_sem=send.at[i], recv_sem=recv.at[i],
                device_id=(x, y, 1 - c), device_id_type=_MESH)
            cp.start()
            copies.append(cp)
        for cp in copies:
            cp.wait()

    return pl.pallas_call(
        body, name="grad_pair_exchange",
        in_specs=[_ANY] * nw, out_specs=[_ANY] * nw,
        out_shape=[jax.ShapeDtypeStruct(sp.half_shape(g.shape), g.dtype) for g, sp in zip(grads, splits)],
        scratch_shapes=[pltpu.SemaphoreType.DMA((nw,)), pltpu.SemaphoreType.DMA((nw,))],
    )(*grads)


def _row_block(rows):
    for cand in (976, 176, 256, 128):
        if rows % cand == 0:
            return cand
    return rows


def _pair_sum(g, r, split, c_arr, name):
    hr, hn = r.shape
    br = _row_block(hr)
    if split.by_rows:
        g_spec = pl.BlockSpec((br, hn), lambda i, c_ref: (i, c_ref[0]))
    else:
        g_spec = pl.BlockSpec((br, hn), lambda i, c_ref: (c_ref[0] * (hr // br) + i, 0))

    def body(c_ref, g_ref, r_ref, o_ref):
        o_ref[...] = (g_ref[...] + r_ref[...]).astype(o_ref.dtype)

    return pl.pallas_call(
        body, name=name,
        grid_spec=pltpu.PrefetchScalarGridSpec(
            num_scalar_prefetch=1, grid=(hr // br,),
            in_specs=[g_spec, pl.BlockSpec((br, hn), lambda i, c_ref: (i, 0))],
            out_specs=pl.BlockSpec((br, hn), lambda i, c_ref: (i, 0))),
        out_shape=jax.ShapeDtypeStruct(r.shape, _MX),
        compiler_params=_cparams(("parallel",)),
    )(c_arr, g, r)


def _chip_exchange(parts, splits, small):
    nw = len(parts)

    def body(*refs):
        ins, small_ref, outs, all_ref = refs[:nw], refs[nw], refs[nw + 1:2 * nw + 1], refs[2 * nw + 1]
        send, recv, lsem, s_send, s_recv = refs[2 * nw + 2:]
        x, y, c = _place()
        me = 2 * x + y
        dev = 2 * me + c
        copies = [pltpu.make_async_copy(small_ref, all_ref.at[dev], lsem.at[nw])]
        copies[0].start()
        for r in range(1, 8):
            peer = (1 - x if r & 4 else x, 1 - y if r & 2 else y, 1 - c if r & 1 else c)
            cp = pltpu.make_async_remote_copy(
                src_ref=small_ref, dst_ref=all_ref.at[dev], send_sem=s_send.at[r - 1], recv_sem=s_recv.at[r - 1],
                device_id=peer, device_id_type=_MESH)
            cp.start()
            copies.append(cp)
        for i in range(nw):
            mine = pltpu.make_async_copy(splits[i].chip_part(ins[i], me), outs[i].at[me], lsem.at[i])
            mine.start()
            copies.append(mine)
            for j, (px, py) in enumerate(_other_chips(x, y)):
                cp = pltpu.make_async_remote_copy(
                    src_ref=splits[i].chip_part(ins[i], 2 * px + py), dst_ref=outs[i].at[me],
                    send_sem=send.at[3 * i + j], recv_sem=recv.at[3 * i + j],
                    device_id=(px, py, c), device_id_type=_MESH)
                cp.start()
                copies.append(cp)
        for cp in copies:
            cp.wait()

    return pl.pallas_call(
        body, name="grad_chip_exchange",
        in_specs=[_ANY] * (nw + 1), out_specs=[_ANY] * (nw + 1),
        out_shape=[jax.ShapeDtypeStruct((4,) + sp.part_shape(p.shape), p.dtype) for p, sp in zip(parts, splits)]
        + [jax.ShapeDtypeStruct((8,) + small.shape, small.dtype)],
        scratch_shapes=[pltpu.SemaphoreType.DMA((3 * nw,)), pltpu.SemaphoreType.DMA((3 * nw,)),
                        pltpu.SemaphoreType.DMA((nw + 1,)), pltpu.SemaphoreType.DMA((7,)),
                        pltpu.SemaphoreType.DMA((7,))],
    )(*parts, small)


def _sum_chips(q, split, c_arr, name):
    _, hr, hn = q.shape
    br = _row_block(hr)
    if split.by_rows:
        out_shape = (hr, 2 * hn)
        o_spec = pl.BlockSpec((br, hn), lambda i, c_ref: (i, c_ref[0]))
    else:
        out_shape = (2 * hr, hn)
        o_spec = pl.BlockSpec((br, hn), lambda i, c_ref: (c_ref[0] * (hr // br) + i, 0))

    def body(c_ref, q_ref, o_ref):
        f = lambda k: q_ref[k].astype(_F32)
        o_ref[...] = ((f(0) + f(1)) + f(2)) + f(3)

    return pl.pallas_call(
        body, name=name,
        grid_spec=pltpu.PrefetchScalarGridSpec(
            num_scalar_prefetch=1, grid=(hr // br,),
            in_specs=[pl.BlockSpec((4, br, hn), lambda i, c_ref: (0, i, 0))], out_specs=o_spec),
        out_shape=jax.ShapeDtypeStruct(out_shape, _F32),
        compiler_params=_cparams(("parallel",)),
    )(c_arr, q)


def _pair_share(bufs, splits):
    nw = len(bufs)

    def body(*refs):
        ins, outs = refs[:nw], refs[nw:2 * nw]
        send, recv = refs[2 * nw:]
        x, y, c = _place()
        copies = []
        for i in range(nw):
            cp = pltpu.make_async_remote_copy(
                src_ref=splits[i].half(ins[i], c), dst_ref=splits[i].half(outs[i], c), send_sem=send.at[i],
                recv_sem=recv.at[i], device_id=(x, y, 1 - c), device_id_type=_MESH)
            cp.start()
            copies.append(cp)
        for cp in copies:
            cp.wait()

    return pl.pallas_call(
        body, name="grad_pair_share",
        in_specs=[_ANY] * nw, out_specs=[_ANY] * nw,
        out_shape=[jax.ShapeDtypeStruct(b.shape, b.dtype) for b in bufs],
        input_output_aliases={i: i for i in range(nw)},
        scratch_shapes=[pltpu.SemaphoreType.DMA((nw,)), pltpu.SemaphoreType.DMA((nw,))],
    )(*bufs)


def _sum_devices(parts):
    def body(p_ref, tot_ref):
        acc = p_ref[0]
        for d in range(1, 8):
            acc = acc + p_ref[d]
        tot_ref[...] = acc

    vm = pl.BlockSpec(memory_space=pltpu.VMEM)
    return pl.pallas_call(
        body, name="small_sum", in_specs=[vm], out_specs=vm,
        out_shape=jax.ShapeDtypeStruct(parts.shape[1:], parts.dtype),
    )(parts)


def _adamw(w, g, m, v, name):
    lead = w.shape[0] != 1
    r, n = (w.shape[0], w.shape[2]) if lead else w.shape[1:]
    br = r
    for cand in (256, 244, 128):
        if r > cand and r % cand == 0:
            br = cand
            break

    def body(w_ref, g_ref, m_ref, v_ref, d_ref, nm_ref, nv_ref):
        gv = g_ref[...]
        m2 = ADAM_B1 * m_ref[...] + (1.0 - ADAM_B1) * gv
        v2 = ADAM_B2 * v_ref[...] + (1.0 - ADAM_B2) * (gv * gv)
        m_hat = m2 / (1.0 - ADAM_B1 ** ADAM_STEP)
        v_hat = v2 / (1.0 - ADAM_B2 ** ADAM_STEP)
        d_ref[...] = -ADAM_LR * (m_hat / (jnp.sqrt(v_hat) + ADAM_EPS) + ADAM_WD * w_ref[...])
        nm_ref[...] = m2
        nv_ref[...] = v2

    blk = pl.BlockSpec((br, 1, n), lambda i: (i, 0, 0)) if lead else pl.BlockSpec((None, br, n), lambda i: (0, i, 0))
    shp = jax.ShapeDtypeStruct(w.shape, _F32)
    return pl.pallas_call(
        body, name=name, grid=(r // br,),
        in_specs=[blk] * 4, out_specs=[blk] * 3, out_shape=[shp] * 3,
        compiler_params=_cparams(("parallel",)),
    )(w, g, m, v)


def kernel(x, positions, g_in, w_in, w_alpha_up, b_alpha, attn_sinks, g_gla_norm, w_out_a, w_out_b, w_o, g_final, loss_target, m_g_in, m_w_in, m_w_alpha_up, m_b_alpha, m_attn_sinks, m_g_gla_norm, m_w_out_a, m_w_out_b, m_w_o, m_g_final, v_g_in, v_w_in, v_w_alpha_up, v_b_alpha, v_attn_sinks, v_g_gla_norm, v_w_out_a, v_w_out_b, v_w_o, v_g_final):
    nseq, seq, _ = x.shape
    t = nseq * seq
    cx, cy, cc = _place()
    chip = 2 * cx + cy
    c_arr = jnp.reshape(cc, (1,)).astype(jnp.int32)

    tr = lambda w: jnp.transpose(w, (2, 0, 1))
    w_in_t = tr(w_in).reshape(SHARD, D_MODEL).astype(_MX)
    pad = WINDOW_ROWS - SHARD
    window = lax.switch(chip, [lambda w, k=k: jnp.pad(w, ((4 * k, pad - 4 * k), (0, 0))) for k in range(4)], w_in_t)
    shards = [window, w_out_a[0].astype(_MX), w_out_b[0].astype(_MX), w_o[0].astype(_MX), w_alpha_up[0].astype(_MX)]
    splits = [SPLIT_W_IN_T, SPLIT_W_OUT, SPLIT_W_OUT, SPLIT_W_O, None]
    fulls = [(4, WINDOW_ROWS, D_MODEL), (A_WIDTH, D_MODEL), (B_WIDTH, D_MODEL), (D_MODEL, D_MODEL),
             (4, B_GATE_RANK, B_KEY_WIDTH // 4)]
    pos_f = positions.astype(_F32).reshape(t, 1)
    win_g, wa, wb, wo, wup_g, cos, sa, sb = _gather_weights(shards, splits, fulls, pos_f)
    nsh = D_MODEL // 4
    win_g = lax.dynamic_update_slice(win_g, window[None], (chip, 0, 0))
    wa = lax.dynamic_update_slice(wa, shards[1], (0, nsh * chip))
    wb = lax.dynamic_update_slice(wb, shards[2], (0, nsh * chip))
    wo = lax.dynamic_update_slice(wo, shards[3], (nsh * chip, 0))
    wup_g = lax.dynamic_update_slice(wup_g, shards[4][None], (chip, 0, 0))
    wt = _assemble_w_in_t(win_g)
    wup = jnp.concatenate([jnp.transpose(wup_g, (1, 0, 2)).reshape(B_GATE_RANK, B_KEY_WIDTH),
                           jnp.zeros((LANE - B_GATE_RANK, B_KEY_WIDTH), _MX)], axis=0)

    x2 = x.reshape(t, D_MODEL)
    tgt = loss_target.reshape(t, D_MODEL)
    sinks = attn_sinks.reshape(A_HEADS)
    gf = g_final.reshape(1, D_MODEL)

    h, qkv, za, qkb, vb, zb, alr, ga, gb = _in_proj(x2, g_in, wt, cos, sa, sb)
    oa = _attn_fwd(qkv, za, sinks, nseq)
    ob, oraw, sst = _gla_fwd(qkb, vb, zb, alr, wup, b_alpha, g_gla_norm, nseq)

    dh2, doa, dob, dga, dgb, dwa, dwb, dwo, dgf, lossv = _merge_loss(oa, ob, ga, gb, x2, tgt, wa, wb, wo, gf)

    dqkv, dza, dsink = _attn_bwd(qkv, za, doa, sinks, cos, sa, sb, nseq)
    dqkb, dvb, dzb, dalr, dwup, dba, dgn = _gla_bwd(qkb, vb, zb, alr, oraw, dob, sst, wup, b_alpha, g_gla_norm, nseq)
    dpieces = [dqkv, dza, dqkb, dvb, dzb, dalr, dga, dgb]
    grad_x2, dgin = _in_proj_bwd_x(dpieces, wt, x2, dh2, g_in)
    dwin_t = _in_proj_bwd_w(h, dpieces)

    grads = [dwin_t, dwa, dwb, dwo]
    gsplits = [SPLIT_W_IN_T, SPLIT_W_OUT, SPLIT_W_OUT, SPLIT_W_O]
    names = ("w_in", "w_out_a", "w_out_b", "w_o")
    from_sibling = _pair_exchange(grads, gsplits)
    pair_sums = [_pair_sum(g, r, sp, c_arr, "pair_sum_" + nm)
                 for g, r, sp, nm in zip(grads, from_sibling, gsplits, names)]
    small = jnp.concatenate([
        dgin, dgf, dgn, dba,
        jnp.pad(dsink[:, 0].reshape(1, A_HEADS), ((0, 0), (0, LANE - A_HEADS))),
        jnp.pad(jnp.sum(lossv, axis=1, keepdims=True), ((0, 0), (0, LANE - 1))),
        dwup[:B_GATE_RANK].reshape(1, B_GATE_RANK * B_KEY_WIDTH)], axis=1)
    *from_chips, small_parts = _chip_exchange(pair_sums, gsplits, jnp.pad(small, ((0, 7), (0, 0))))
    reduced = [_sum_chips(q, sp, c_arr, "chip_sum_" + nm) for q, sp, nm in zip(from_chips, gsplits, names)]
    g_window, g_wa, g_wb, g_wo = _pair_share(reduced, gsplits)
    g_win_t = lax.switch(chip, [lambda w, k=k: w[4 * k:4 * k + SHARD].reshape(SHARD, 1, D_MODEL) for k in range(4)],
                         g_window)
    tot = _sum_devices(small_parts)[0:1]
    o = 0
    def take(n):
        nonlocal o
        o += n
        return tot[:, o - n:o]
    g_gin, g_gf, g_gn, g_ba = take(D_MODEL), take(D_MODEL), take(B_WIDTH), take(B_KEY_WIDTH)
    g_sink = take(LANE)[:, :A_HEADS]
    loss = take(LANE)[0, 0]
    g_wup_full = take(B_GATE_RANK * B_KEY_WIDTH).reshape(B_GATE_RANK, B_KEY_WIDTH)
    nup = B_KEY_WIDTH // 4
    g_wup = lax.dynamic_slice(g_wup_full, (0, chip * nup), (B_GATE_RANK, nup))

    def pack(*parts):
        return jnp.concatenate([p.reshape(1, -1) for p in parts], axis=1)

    sm_w = pack(g_in, g_final, g_gla_norm, b_alpha, attn_sinks, w_alpha_up)
    sm_g = pack(g_gin, g_gf, g_gn, g_ba, g_sink, g_wup)
    sm_m = pack(m_g_in, m_g_final, m_g_gla_norm, m_b_alpha, m_attn_sinks, m_w_alpha_up)
    sm_v = pack(v_g_in, v_g_final, v_g_gla_norm, v_b_alpha, v_attn_sinks, v_w_alpha_up)
    sm_out = [p[0] for p in _adamw(sm_w[None], sm_g[None], sm_m[None], sm_v[None], "adamw_small")]

    def unpack(p):
        sizes = (D_MODEL, D_MODEL, B_WIDTH, B_KEY_WIDTH, A_HEADS, B_GATE_RANK * nup)
        outs, at = [], 0
        for s in sizes:
            outs.append(p[:, at:at + s])
            at += s
        gi, gfin, gnn, ba, sk, wu = outs
        return dict(g_in=gi, g_final=gfin.reshape(D_MODEL), g_gla_norm=gnn, b_alpha=ba, attn_sinks=sk,
                    w_alpha_up=wu.reshape(1, B_GATE_RANK, nup))

    untr = lambda a: jnp.transpose(a, (1, 2, 0))
    big = dict(w_in=tuple(untr(a) for a in (g_win_t,) + tuple(_adamw(tr(w_in), g_win_t, tr(m_w_in), tr(v_w_in), "adamw_w_in"))))
    for nm, w, g, m, v in (("w_out_a", w_out_a, g_wa, m_w_out_a, v_w_out_a),
                           ("w_out_b", w_out_b, g_wb, m_w_out_b, v_w_out_b), ("w_o", w_o, g_wo, m_w_o, v_w_o)):
        big[nm] = (g[None],) + tuple(_adamw(w, g[None], m, v, "adamw_" + nm))

    order = ("g_in", "w_in", "w_alpha_up", "b_alpha", "attn_sinks", "g_gla_norm", "w_out_a", "w_out_b", "w_o", "g_final")
    small_sets = [unpack(sm_g)] + [unpack(p) for p in sm_out]
    outs = []
    for kind in range(4):
        for nm in order:
            outs.append(big[nm][kind] if nm in big else small_sets[kind][nm])
    return (loss, grad_x2.reshape(x.shape), *outs)
```

```python
import math
from typing import NamedTuple

import numpy as np
import jax
import jax.numpy as jnp
from jax import lax
from jax.experimental import pallas as pl
from jax.experimental.pallas import tpu as pltpu

D_MODEL = 1024
A_HEADS, A_KV_HEADS, A_HEAD_DIM = 8, 2, 64
A_GROUP = A_HEADS // A_KV_HEADS
A_WIDTH, A_KV_WIDTH = 512, 128
BLOCK = 128
ROPE_THETA = 500000.0
ROPE_DIM = 16
B_HEADS, B_KEY_DIM, B_VAL_DIM = 4, 64, 128
B_KEY_WIDTH, B_WIDTH = 256, 512
B_GATE_RANK = 16
B_GATE_TEMP = 16.0
B_CHUNK = 64
NORM_EPS = 1e-6
NEG_BIG = -1e30
D_IN = 4880

ADAM_LR, ADAM_B1, ADAM_B2, ADAM_EPS, ADAM_WD, ADAM_STEP = 0.001, 0.9, 0.999, 1e-08, 0.01, 10

LANE = 128
ALR_AT = 2816
PIECES = (("qkv", 0, 768), ("za", 768, 1280), ("qkb", 1280, 1792), ("vb", 1792, 2304),
          ("zb", 2304, 2816), ("alr", ALR_AT, ALR_AT + LANE), ("ga", 2832, 3856), ("gb", 3856, 4880))
SHARD = D_IN // 4
WINDOW_STEP = 1216
WINDOW_ROWS = 1232

GLA_BLOCK = 256
MERGE_SLAB = 16
VMEM_LIMIT = 56 * 1024 * 1024

_F32 = jnp.float32
_MX = jnp.bfloat16
_ST = jnp.bfloat16

_MESH = pl.DeviceIdType.MESH
_ANY = pl.BlockSpec(memory_space=pl.ANY)


def _cparams(sem=None, vmem=None):
    return pltpu.CompilerParams(dimension_semantics=sem, vmem_limit_bytes=vmem)


def _dot(a, b):
    return jnp.dot(a.astype(_MX), b.astype(_MX), preferred_element_type=_F32)


def _dot_nt(a, b):
    return lax.dot_general(a.astype(_MX), b.astype(_MX), (((1,), (1,)), ((), ())),
                           preferred_element_type=_F32)


def _dot_tn(a, b):
    return lax.dot_general(a.astype(_MX), b.astype(_MX), (((0,), (0,)), ((), ())),
                           preferred_element_type=_F32)


def _dot_ones(ones_mat, v):
    o = ones_mat.astype(jnp.bfloat16)
    v0 = v.astype(jnp.bfloat16)
    r1 = v - v0.astype(_F32)
    v1 = r1.astype(jnp.bfloat16)
    v2 = (r1 - v1.astype(_F32)).astype(jnp.bfloat16)
    d = lambda t: jnp.dot(o, t, preferred_element_type=_F32)
    return d(v0) + d(v1) + d(v2)


def _sigmoid(x):
    return 0.5 * jnp.tanh(0.5 * x) + 0.5


def _log_sigmoid(x):
    return jnp.minimum(x, 0.0) - jnp.log(1.0 + jnp.exp(-jnp.abs(x)))


def _lane_tile(t, width):
    reps = width // t.shape[1]
    return t if reps == 1 else jnp.tile(t, (1, reps))


def _rope(t, cos, sa, sb, sign):
    w = t.shape[1]
    rot = pltpu.roll(t, w - 8, 1) * _lane_tile(sa, w) + pltpu.roll(t, 8, 1) * _lane_tile(sb, w)
    return t * _lane_tile(cos, w) + sign * rot


def _rms_bwd(dy_g, n, r):
    return r * (dy_g - n * jnp.mean(dy_g * n, axis=-1, keepdims=True))


ROPE_ROWS = 256


def _rope_consts():
    lane = np.arange(LANE) % A_HEAD_DIM
    half = ROPE_DIM // 2
    inv = np.exp((np.float32(-math.log(ROPE_THETA)) * np.arange(half, dtype=np.float32)) * np.float32(2.0 / ROPE_DIM))
    consts = np.zeros((8, LANE), np.float32)
    consts[0] = np.where(lane < ROPE_DIM, inv[lane % half], 0.0)
    consts[1] = np.where(lane < half, -1.0, 0.0)
    consts[2] = np.where((lane >= half) & (lane < ROPE_DIM), 1.0, 0.0)
    return jnp.asarray(consts)


def _rope_tables_into(pos_ref, c_ref, cos_ref, sa_ref, sb_ref):
    def rows_of(b, carry):
        rows = pl.ds(pl.multiple_of(b * ROPE_ROWS, ROPE_ROWS), ROPE_ROWS)
        ang = pos_ref[rows, :] * c_ref[0:1, :]
        s = jnp.sin(ang)
        cos_ref[rows, :] = jnp.cos(ang)
        sa_ref[rows, :] = s * c_ref[1:2, :]
        sb_ref[rows, :] = s * c_ref[2:3, :]
        return carry

    lax.fori_loop(0, pos_ref.shape[0] // ROPE_ROWS, rows_of, 0)


def _in_proj(x2, g_in, wt, cos, sa, sb):
    t = x2.shape[0]
    tm = min(t, 512)

    def body(x_ref, g_ref, w_ref, cos_ref, sa_ref, sb_ref, h_ref, qkv_ref, za_ref, qkb_ref,
             vb_ref, zb_ref, alr_ref, ga_ref, gb_ref):
        xv = x_ref[...]
        r = lax.rsqrt(jnp.mean(xv * xv, axis=-1, keepdims=True) + NORM_EPS)
        h = (xv * r * g_ref[...]).astype(_MX)
        h_ref[...] = h.astype(_ST)
        outs = dict(za=za_ref, qkb=qkb_ref, vb=vb_ref, zb=zb_ref, alr=alr_ref, ga=ga_ref, gb=gb_ref)
        for name, a, b in PIECES:
            p = _dot_nt(h, w_ref[a:b, :])
            if name == "qkv":
                c, s1, s2 = cos_ref[...], sa_ref[...], sb_ref[...]
                qkv_ref[:, 0:512] = _rope(p[:, 0:512], c, s1, s2, 1.0).astype(_ST)
                qkv_ref[:, 512:640] = _rope(p[:, 512:640], c, s1, s2, 1.0).astype(_ST)
                qkv_ref[:, 640:768] = p[:, 640:768].astype(_ST)
            else:
                outs[name][...] = p.astype(outs[name].dtype)

    rows = lambda w: pl.BlockSpec((tm, w), lambda i: (i, 0))
    shp = lambda name, w: jax.ShapeDtypeStruct((t, w), _F32 if name == "qkb" else _ST)
    widths = [D_MODEL] + [b - a for _, a, b in PIECES]
    return pl.pallas_call(
        body, name="in_proj", grid=(t // tm,),
        in_specs=[rows(D_MODEL), pl.BlockSpec((1, D_MODEL), lambda i: (0, 0)),
                  pl.BlockSpec((D_IN, D_MODEL), lambda i: (0, 0)), rows(LANE), rows(LANE), rows(LANE)],
        out_specs=[rows(w) for w in widths],
        out_shape=[shp(n, w) for n, w in zip(["h"] + [p[0] for p in PIECES], widths)],
        compiler_params=_cparams(("parallel",), VMEM_LIMIT),
    )(x2, g_in, wt, cos, sa, sb)


def _attn_operands(qkv_ref, kvp_ref, want_bwd):
    kf = jnp.concatenate([kvp_ref[:, 0:128], qkv_ref[:, 512:640]], axis=0).astype(_F32) * (A_HEAD_DIM ** -0.5)
    vf = jnp.concatenate([kvp_ref[:, 128:256], qkv_ref[:, 640:768]], axis=0).astype(_F32)
    lo = lax.broadcasted_iota(jnp.int32, (1, LANE), 1) < 64

    def on_lanes(a):
        sw = pltpu.roll(a, 64, 1)
        z = jnp.zeros_like(a)
        return [[jnp.where(lo, a, z).astype(_MX), jnp.where(lo, z, sw).astype(_MX)],
                [jnp.where(lo, sw, z).astype(_MX), jnp.where(lo, z, a).astype(_MX)]]

    def on_rows(a):
        at = a.T.astype(_MX)
        z = jnp.zeros((64, at.shape[1]), _MX)
        top, bot = at[0:64], at[64:128]
        return [[jnp.concatenate([top, z], axis=0), jnp.concatenate([z, top], axis=0)],
                [jnp.concatenate([bot, z], axis=0), jnp.concatenate([z, bot], axis=0)]]

    ops = dict(k_lanes=on_lanes(kf), v_rows=on_rows(vf), lo=lo)
    if want_bwd:
        ops.update(v_lanes=on_lanes(vf), k_rows=on_rows(kf))
    return ops


def _attn_valid(n):
    kj = lax.broadcasted_iota(jnp.int32, (2 * BLOCK, 2 * BLOCK), 0) - BLOCK
    qi = lax.broadcasted_iota(jnp.int32, (2 * BLOCK, 2 * BLOCK), 1) & (BLOCK - 1)
    return (kj <= qi) & (qi - kj < BLOCK) & ((n > 0) | (kj >= 0))


def _attn_sinks(sink_ref, h_a, h_b):
    first = lax.broadcasted_iota(jnp.int32, (1, 2 * BLOCK), 1) < BLOCK
    return jnp.where(first, sink_ref[h_a], sink_ref[h_b])


def _attn_softmax_t(k_lanes, q_pair, valid, sink):
    s = jnp.where(valid, _dot_nt(k_lanes, q_pair), NEG_BIG)
    m = jnp.maximum(jnp.max(s, axis=0, keepdims=True), sink)
    e = jnp.exp(s - m)
    e_sink = jnp.exp(sink - m)
    inv = 1.0 / (jnp.sum(e, axis=0, keepdims=True) + e_sink)
    return e, e_sink, inv


def _attn_fwd(qkv, za, sinks, nseq):
    t = qkv.shape[0]
    nb = t // nseq // BLOCK

    def body(sink_ref, qkv_ref, kvp_ref, za_ref, oa_ref):
        n = pl.program_id(1)
        ops = _attn_operands(qkv_ref, kvp_ref, False)
        valid = _attn_valid(n)[:, 0:BLOCK]
        for pr in range(A_HEADS // 2):
            lanes = slice(pr * LANE, (pr + 1) * LANE)
            g = pr // (A_GROUP // 2)
            q_pair = qkv_ref[:, lanes]
            ot = None
            for half in range(2):
                e, _, inv = _attn_softmax_t(ops["k_lanes"][g][half], q_pair, valid, sink_ref[2 * pr + half])
                part = _dot(ops["v_rows"][g][half], e) * inv
                ot = part if ot is None else ot + part
            z = za_ref[:, lanes].astype(_F32)
            oa_ref[:, lanes] = (ot.T * (z * _sigmoid(z))).astype(_ST)

    cur = lambda w: pl.BlockSpec((BLOCK, w), lambda s, n: (s * nb + n, 0))
    return pl.pallas_call(
        body, name="attn_fwd", grid=(nseq, nb),
        in_specs=[pl.BlockSpec(memory_space=pltpu.SMEM), cur(768),
                  pl.BlockSpec((BLOCK, 256), lambda s, n: (s * nb + jnp.maximum(n - 1, 0), 2)), cur(512)],
        out_specs=cur(512), out_shape=jax.ShapeDtypeStruct((t, A_WIDTH), _ST),
        compiler_params=_cparams(("parallel", "arbitrary")),
    )(sinks, qkv, qkv, za)


def _attn_bwd(qkv, za, doa, sinks, cos, sa, sb, nseq):
    t = qkv.shape[0]
    nb = t // nseq // BLOCK

    def body(sink_ref, qkv_ref, kvp_ref, za_ref, doa_ref, cos_ref, sa_ref, sb_ref,
             dqkv_ref, dza_ref, dsink_ref, ck_ref, cv_ref):
        s_id, i = pl.program_id(0), pl.program_id(1)
        n = nb - 1 - i

        @pl.when((s_id == 0) & (i == 0))
        def _():
            dsink_ref[...] = jnp.zeros_like(dsink_ref)

        @pl.when(i == 0)
        def _():
            ck_ref[...] = jnp.zeros_like(ck_ref)
            cv_ref[...] = jnp.zeros_like(cv_ref)

        ops = _attn_operands(qkv_ref, kvp_ref, True)
        lo = ops["lo"]
        valid = _attn_valid(n)
        dk_acc, dv_acc, dq_pairs = [], [], []
        for g in range(A_KV_HEADS):
            pairs = [slice((2 * g + j) * LANE, (2 * g + j + 1) * LANE) for j in range(2)]
            q_both = jnp.concatenate([qkv_ref[:, p] for p in pairs], axis=0)
            q_f = q_both.astype(_F32)
            z = [za_ref[:, p].astype(_F32) for p in pairs]
            sz = [_sigmoid(t) for t in z]
            d_oa = [doa_ref[:, p].astype(_F32) for p in pairs]
            d_att = jnp.concatenate([d_oa[j] * (z[j] * sz[j]) for j in range(2)], axis=0)
            zero = jnp.zeros_like(d_att)
            ot, dqt, ds_all, pn_all, qz_all, daz_all = None, None, [], [], [], []
            for half in range(2):
                heads = (4 * g + half, 4 * g + 2 + half)
                e, e_sink, inv = _attn_softmax_t(ops["k_lanes"][g][half], q_both, valid,
                                                 _attn_sinks(sink_ref, *heads))
                pn = e * inv
                dpt = _dot_nt(ops["v_lanes"][g][half], d_att)
                delta = jnp.sum(pn * dpt, axis=0, keepdims=True)
                ds = (pn * (dpt - delta)).astype(_MX)
                pn = pn.astype(_MX)
                d_sink = e_sink * inv * delta
                for j, h in enumerate(heads):
                    dsink_ref[h:h + 1, :] = dsink_ref[h:h + 1, :] - jnp.sum(d_sink[:, j * BLOCK:(j + 1) * BLOCK])
                o_part = _dot(ops["v_rows"][g][half], pn)
                dq_part = _dot(ops["k_rows"][g][half], ds)
                ot = o_part if ot is None else ot + o_part
                dqt = dq_part if dqt is None else dqt + dq_part
                mine = lo if half == 0 else jnp.logical_not(lo)
                ds_all.append(ds)
                pn_all.append(pn)
                qz_all.append(jnp.where(mine, q_f, zero).astype(_MX))
                daz_all.append(jnp.where(mine, d_att, zero).astype(_MX))
            dk_acc.append(_dot(jnp.concatenate(ds_all, axis=1), jnp.concatenate(qz_all, axis=0)))
            dv_acc.append(_dot(jnp.concatenate(pn_all, axis=1), jnp.concatenate(daz_all, axis=0)))
            for j, p in enumerate(pairs):
                cols = slice(j * BLOCK, (j + 1) * BLOCK)
                dza_ref[:, p] = (d_oa[j] * ot[:, cols].T * (sz[j] * (1.0 + z[j] * (1.0 - sz[j])))).astype(_ST)
                dq_pairs.append(dqt[:, cols].T)

        def fold(acc, scale):
            both = [a + pltpu.roll(a, 64, 1) for a in acc]
            return jnp.where(lo, both[0], both[1]) * scale

        dk_full = fold(dk_acc, A_HEAD_DIM ** -0.5)
        dv_full = fold(dv_acc, 1.0)
        dk_cur = dk_full[BLOCK:] + ck_ref[...]
        dv_cur = dv_full[BLOCK:] + cv_ref[...]
        ck_ref[...] = dk_full[:BLOCK]
        cv_ref[...] = dv_full[:BLOCK]
        c, s1, s2 = cos_ref[...], sa_ref[...], sb_ref[...]
        dqkv_ref[:, 0:512] = _rope(jnp.concatenate(dq_pairs, axis=1), c, s1, s2, -1.0).astype(_ST)
        dqkv_ref[:, 512:640] = _rope(dk_cur, c, s1, s2, -1.0).astype(_ST)
        dqkv_ref[:, 640:768] = dv_cur.astype(_ST)

    cur = lambda w: pl.BlockSpec((BLOCK, w), lambda s, i: (s * nb + nb - 1 - i, 0))
    return pl.pallas_call(
        body, name="attn_bwd", grid=(nseq, nb),
        in_specs=[pl.BlockSpec(memory_space=pltpu.SMEM), cur(768),
                  pl.BlockSpec((BLOCK, 256), lambda s, i: (s * nb + jnp.maximum(nb - 2 - i, 0), 2)),
                  cur(512), cur(512), cur(LANE), cur(LANE), cur(LANE)],
        out_specs=[cur(768), cur(512), pl.BlockSpec((8, LANE), lambda s, i: (0, 0))],
        out_shape=[jax.ShapeDtypeStruct((t, 768), _ST), jax.ShapeDtypeStruct((t, 512), _ST),
                   jax.ShapeDtypeStruct((8, LANE), _F32)],
        scratch_shapes=[pltpu.VMEM((BLOCK, A_KV_WIDTH), _F32), pltpu.VMEM((BLOCK, A_KV_WIDTH), _F32)],
        compiler_params=_cparams(("arbitrary", "arbitrary")),
    )(sinks, qkv, qkv, za, doa, cos, sa, sb)


def _gla_chunk_terms(la, qkb_ref, r0):
    g = la[r0:r0 + B_CHUNK, :]
    ri = lax.broadcasted_iota(jnp.int32, (B_CHUNK, B_CHUNK), 0)
    ci = lax.broadcasted_iota(jnp.int32, (B_CHUNK, B_CHUNK), 1)
    cum = _dot_ones((ri >= ci).astype(_F32), g)
    last = cum[B_CHUNK - 1:B_CHUNK, :]
    mid = cum[B_CHUNK // 2 - 1:B_CHUNK // 2, :]
    q = qkb_ref[r0:r0 + B_CHUNK, 0:B_KEY_WIDTH].astype(_F32) * (B_KEY_DIM ** -0.5)
    k = qkb_ref[r0:r0 + B_CHUNK, B_KEY_WIDTH:2 * B_KEY_WIDTH].astype(_F32)
    e_q, e_k, e_l, e_c = jnp.exp(cum - mid), jnp.exp(mid - cum), jnp.exp(last - cum), jnp.exp(cum)
    dec_col = jnp.exp(jnp.sum(g.T, axis=1, keepdims=True))
    return dict(qm=q * e_q, km=k * e_k, kl=k * e_l, qc=q * e_c, e_q=e_q, e_k=e_k, e_l=e_l, e_c=e_c,
                dec_col=dec_col, dec_row=jnp.exp(last), causal=ri >= ci, ri=ri)


def _gate_logits(alr_ref, wup_ref, b_ref):
    return _dot(alr_ref[...], wup_ref[...]) + b_ref[...]


def _gla_fwd(qkb, vb, zb, alr, wup, b_alpha, gn, nseq):
    t = qkb.shape[0]
    tb = min(GLA_BLOCK, t // nseq)
    nblk = t // nseq // tb
    cpb = tb // B_CHUNK

    def body(qkb_ref, vb_ref, zb_ref, alr_ref, wup_ref, b_ref, gn_ref, ob_ref, oraw_ref, sst_ref, s_ref):
        @pl.when(pl.program_id(1) == 0)
        def _():
            s_ref[...] = jnp.zeros_like(s_ref)

        la = _log_sigmoid(_gate_logits(alr_ref, wup_ref, b_ref)) * (1.0 / B_GATE_TEMP)
        terms = [_gla_chunk_terms(la, qkb_ref, c * B_CHUNK) for c in range(cpb)]
        o_intra, inc = {}, {}
        for c, tm in enumerate(terms):
            for h in range(B_HEADS):
                kl_, vl_ = slice(h * 64, (h + 1) * 64), slice(h * 128, (h + 1) * 128)
                v = vb_ref[c * B_CHUNK:(c + 1) * B_CHUNK, vl_]
                a = jnp.where(tm["causal"], _dot_nt(tm["qm"][:, kl_], tm["km"][:, kl_]), 0.0)
                o_intra[c, h] = _dot(a, v)
                inc[c, h] = _dot_tn(tm["kl"][:, kl_], v)
        o_heads = {}
        for h in range(B_HEADS):
            kl_ = slice(h * 64, (h + 1) * 64)
            st = s_ref[kl_, :]
            for c, tm in enumerate(terms):
                sst_ref[c, kl_, :] = st
                o_heads[c, h] = o_intra[c, h] + _dot(tm["qc"][:, kl_], st)
                st = tm["dec_col"][kl_, :] * st + inc[c, h]
            s_ref[kl_, :] = st
        o = jnp.concatenate([jnp.concatenate([o_heads[c, h] for h in range(B_HEADS)], axis=1)
                             for c in range(cpb)], axis=0)
        oraw_ref[...] = o
        z = zb_ref[...].astype(_F32)
        gate = z * _sigmoid(z)
        for h in range(B_HEADS):
            vl_ = slice(h * 128, (h + 1) * 128)
            oh = o[:, vl_]
            r = lax.rsqrt(jnp.mean(oh * oh, axis=-1, keepdims=True) + NORM_EPS)
            ob_ref[:, vl_] = ((oh * r) * gn_ref[:, vl_] * gate[:, vl_]).astype(_ST)

    rows = lambda w: pl.BlockSpec((tb, w), lambda s, i: (s * nblk + i, 0))
    full = lambda a, b: pl.BlockSpec((a, b), lambda s, i: (0, 0))
    return pl.pallas_call(
        body, name="gla_fwd", grid=(nseq, nblk),
        in_specs=[rows(512), rows(512), rows(512), rows(LANE), full(LANE, B_KEY_WIDTH),
                  full(1, B_KEY_WIDTH), full(1, B_WIDTH)],
        out_specs=[rows(512), rows(512),
                   pl.BlockSpec((cpb, B_KEY_WIDTH, B_VAL_DIM), lambda s, i: (s * nblk + i, 0, 0))],
        out_shape=[jax.ShapeDtypeStruct((t, B_WIDTH), _ST), jax.ShapeDtypeStruct((t, B_WIDTH), _F32),
                   jax.ShapeDtypeStruct((t // B_CHUNK, B_KEY_WIDTH, B_VAL_DIM), _F32)],
        scratch_shapes=[pltpu.VMEM((B_KEY_WIDTH, B_VAL_DIM), _F32)],
        compiler_params=_cparams(("parallel", "arbitrary")),
    )(qkb, vb, zb, alr, wup, b_alpha, gn)


def _gla_bwd(qkb, vb, zb, alr, oraw, dob, sst, wup, b_alpha, gn, nseq):
    t = qkb.shape[0]
    tb = min(GLA_BLOCK, t // nseq)
    nblk = t // nseq // tb
    cpb = tb // B_CHUNK

    def body(qkb_ref, vb_ref, zb_ref, alr_ref, oraw_ref, dob_ref, sst_ref, wup_ref, b_ref, gn_ref,
             dqkb_ref, dvb_ref, dzb_ref, dalr_ref, dwup_ref, db_ref, dgn_ref, ds_ref):
        s_id, i = pl.program_id(0), pl.program_id(1)

        @pl.when((s_id == 0) & (i == 0))
        def _():
            dwup_ref[...] = jnp.zeros_like(dwup_ref)
            db_ref[...] = jnp.zeros_like(db_ref)
            dgn_ref[...] = jnp.zeros_like(dgn_ref)

        @pl.when(i == 0)
        def _():
            ds_ref[...] = jnp.zeros_like(ds_ref)

        a_pre = _gate_logits(alr_ref, wup_ref, b_ref)
        la = _log_sigmoid(a_pre) * (1.0 / B_GATE_TEMP)

        z = zb_ref[...].astype(_F32)
        sz = _sigmoid(z)
        d_ob = dob_ref[...].astype(_F32)
        tg = d_ob * (z * sz)
        dsilu = sz * (1.0 + z * (1.0 - sz))
        do_cols, dgn_cols = [], []
        for h in range(B_HEADS):
            vl_ = slice(h * 128, (h + 1) * 128)
            oh = oraw_ref[:, vl_].astype(_F32)
            r = lax.rsqrt(jnp.mean(oh * oh, axis=-1, keepdims=True) + NORM_EPS)
            on = oh * r
            gnh = gn_ref[:, vl_]
            dzb_ref[:, vl_] = (d_ob[:, vl_] * (on * gnh) * dsilu[:, vl_]).astype(_ST)
            dgn_cols.append(jnp.sum(tg[:, vl_] * on, axis=0, keepdims=True))
            do_cols.append(_rms_bwd(tg[:, vl_] * gnh, on, r))
        dgn_ref[...] = dgn_ref[...] + jnp.concatenate(dgn_cols, axis=1)
        d_o = jnp.concatenate(do_cols, axis=1)

        ri = lax.broadcasted_iota(jnp.int32, (tb, tb), 0)
        ci = lax.broadcasted_iota(jnp.int32, (tb, tb), 1)
        same = (ri // B_CHUNK) == (ci // B_CHUNK)
        low = same & (ri >= ci)
        upto_mid = same & ((ci % B_CHUNK) < B_CHUNK // 2)
        sums = _dot_ones(jnp.concatenate([m.astype(_F32) for m in (low, same, upto_mid)], axis=0), la)
        cum, last, mid = sums[0:tb], sums[tb:2 * tb], sums[2 * tb:3 * tb]
        e_q, e_k, e_l, e_c = jnp.exp(cum - mid), jnp.exp(mid - cum), jnp.exp(last - cum), jnp.exp(cum)
        q = qkb_ref[:, 0:B_KEY_WIDTH] * (B_KEY_DIM ** -0.5)
        k = qkb_ref[:, B_KEY_WIDTH:2 * B_KEY_WIDTH]
        qm, km, kl, qc = q * e_q, k * e_k, k * e_l, q * e_c
        lane_head = lax.broadcasted_iota(jnp.int32, (1, B_KEY_WIDTH), 1) // B_KEY_DIM
        d_o_mx = d_o.astype(_MX)

        def on_diagonal(st):
            z = jnp.zeros((B_KEY_DIM, B_VAL_DIM), st.dtype)
            return jnp.concatenate([jnp.concatenate(
                [st[h * B_KEY_DIM:(h + 1) * B_KEY_DIM] if g == h else z for g in range(B_HEADS)], axis=1)
                for h in range(B_HEADS)], axis=0)

        def diagonal_of(full):
            return jnp.concatenate([full[h * B_KEY_DIM:(h + 1) * B_KEY_DIM, h * B_VAL_DIM:(h + 1) * B_VAL_DIM]
                                    for h in range(B_HEADS)], axis=0)

        dqm, dkm, dv_cols = None, None, []
        for h in range(B_HEADS):
            vl_ = slice(h * B_VAL_DIM, (h + 1) * B_VAL_DIM)
            mine = lane_head == h
            qz, kz = jnp.where(mine, qm, 0.0).astype(_MX), jnp.where(mine, km, 0.0).astype(_MX)
            a = jnp.where(low, _dot_nt(qz, kz), 0.0).astype(_MX)
            da = jnp.where(low, _dot_nt(d_o_mx[:, vl_], vb_ref[:, vl_]), 0.0).astype(_MX)
            dqm_h, dkm_h = _dot(da, kz), _dot_tn(da, qz)
            dqm = dqm_h if dqm is None else dqm + dqm_h
            dkm = dkm_h if dkm is None else dkm + dkm_h
            dv_cols.append(_dot_tn(a, d_o_mx[:, vl_]))
        dv = jnp.concatenate(dv_cols, axis=1)

        chunk = [slice(c * B_CHUNK, (c + 1) * B_CHUNK) for c in range(cpb)]
        dqc_rows, g_loc = [], []
        for c in range(cpb):
            dqc_rows.append(_dot_nt(d_o_mx[chunk[c]], on_diagonal(sst_ref[c].astype(_MX))))
            g_loc.append(diagonal_of(_dot_tn(qc[chunk[c]], d_o_mx[chunk[c]])))
        cur = ds_ref[...]
        d_state = [None] * cpb
        for c in reversed(range(cpb)):
            d_state[c] = cur
            cur = g_loc[c] + jnp.exp(jnp.sum(la[chunk[c]].T, axis=1, keepdims=True)) * cur
        ds_ref[...] = cur
        dkl_rows, dv_rows, dlast_rows = [], [], []
        ones8 = jnp.ones((8, B_VAL_DIM), _F32)
        for c in range(cpb):
            dsd = on_diagonal(d_state[c].astype(_MX))
            dkl_c = _dot_nt(vb_ref[chunk[c], :], dsd)
            dkl_rows.append(dkl_c)
            dv_rows.append(_dot(kl[chunk[c]], dsd))
            prod = d_state[c] * sst_ref[c]
            p0 = prod.astype(jnp.bfloat16)
            p1 = (prod - p0.astype(_F32)).astype(jnp.bfloat16)
            p2 = (prod - p0.astype(_F32) - p1.astype(_F32)).astype(jnp.bfloat16)
            ddec = (_dot_nt(ones8, p0) + _dot_nt(ones8, p1) + _dot_nt(ones8, p2))[0:1]
            r_last = c * B_CHUNK + B_CHUNK - 1
            dlast = jnp.sum(dkl_c * kl[chunk[c]], axis=0, keepdims=True) + ddec * jnp.exp(last[r_last:r_last + 1])
            dlast_rows.append(jnp.broadcast_to(dlast, (B_CHUNK, B_KEY_WIDTH)))
        dqc, dkl = jnp.concatenate(dqc_rows, axis=0), jnp.concatenate(dkl_rows, axis=0)
        dqkb_ref[:, 0:B_KEY_WIDTH] = ((dqm * e_q + dqc * e_c) * (B_KEY_DIM ** -0.5)).astype(_ST)
        dqkb_ref[:, B_KEY_WIDTH:2 * B_KEY_WIDTH] = (dkm * e_k + dkl * e_l).astype(_ST)
        dvb_ref[...] = (dv + jnp.concatenate(dv_rows, axis=0)).astype(_ST)
        dcum = dqm * qm - dkm * km + dqc * qc - dkl * kl
        row = lax.broadcasted_iota(jnp.int32, (tb, B_KEY_WIDTH), 0)
        dcum = jnp.where(row % B_CHUNK == B_CHUNK - 1, dcum + jnp.concatenate(dlast_rows, axis=0), dcum)
        dla = _dot_ones((same & (ri <= ci)).astype(_F32), dcum)

        da_pre = dla * (1.0 / B_GATE_TEMP) * (1.0 - _sigmoid(a_pre))
        dalr_ref[...] = _dot_nt(da_pre, wup_ref[...]).astype(_ST)
        dwup_ref[...] = dwup_ref[...] + _dot_tn(alr_ref[...], da_pre)
        db_ref[...] = db_ref[...] + jnp.sum(da_pre, axis=0, keepdims=True)

    blk = lambda s, i: s * nblk + nblk - 1 - i
    rows = lambda w: pl.BlockSpec((tb, w), lambda s, i: (blk(s, i), 0))
    full = lambda a, b: pl.BlockSpec((a, b), lambda s, i: (0, 0))
    act = lambda w: jax.ShapeDtypeStruct((t, w), _ST)
    return pl.pallas_call(
        body, name="gla_bwd", grid=(nseq, nblk),
        in_specs=[rows(512), rows(512), rows(512), rows(LANE), rows(512), rows(512),
                  pl.BlockSpec((cpb, B_KEY_WIDTH, B_VAL_DIM), lambda s, i: (blk(s, i), 0, 0)),
                  full(LANE, B_KEY_WIDTH), full(1, B_KEY_WIDTH), full(1, B_WIDTH)],
        out_specs=[rows(512), rows(512), rows(512), rows(LANE), full(LANE, B_KEY_WIDTH),
                   full(1, B_KEY_WIDTH), full(1, B_WIDTH)],
        out_shape=[act(512), act(512), act(512), act(LANE),
                   jax.ShapeDtypeStruct((LANE, B_KEY_WIDTH), _F32),
                   jax.ShapeDtypeStruct((1, B_KEY_WIDTH), _F32), jax.ShapeDtypeStruct((1, B_WIDTH), _F32)],
        scratch_shapes=[pltpu.VMEM((B_KEY_WIDTH, B_VAL_DIM), _F32)],
        compiler_params=_cparams(("arbitrary", "arbitrary")),
    )(qkb, vb, zb, alr, oraw, dob, sst, wup, b_alpha, gn)


def _merge_loss(oa, ob, ga, gb, x2, tgt, wa, wb, wo, g_final):
    t = x2.shape[0]
    tm = min(t, 256)
    nt = t // tm

    def body(oa_ref, ob_ref, ga_ref, gb_ref, x_ref, t_ref, wa_ref, wb_ref, wo_ref, gf_ref,
             dh_ref, doa_ref, dob_ref, dga_ref, dgb_ref, dwa_ref, dwb_ref, dwo_ref, dgf_ref, loss_ref,
             ya_s, yb_s, out_s, dmer_s, mrg_s, dya_s, dyb_s):
        first = pl.program_id(0) == 0
        so_far = lambda ref: jnp.where(first, 0.0, ref[...])

        slabs = [slice(s, s + MERGE_SLAB) for s in range(0, tm, MERGE_SLAB)]
        fold = lambda a: a[0:8] + a[8:16]
        ya_s[...] = _dot(oa_ref[...], wa_ref[...])
        yb_s[...] = _dot(ob_ref[...], wb_ref[...])
        for rows_ in slabs:
            sga, sgb = _sigmoid(ga_ref[rows_, :].astype(_F32)), _sigmoid(gb_ref[rows_, :].astype(_F32))
            mrg_s[rows_, :] = (sga * ya_s[rows_, :] + sgb * yb_s[rows_, :]).astype(_MX)
        out_s[...] = x_ref[...] + _dot(mrg_s[...], wo_ref[...])
        gf = gf_ref[...]
        loss8 = jnp.zeros((8, D_MODEL), _F32)
        dgf8 = jnp.zeros((8, D_MODEL), _F32)
        for rows_ in slabs:
            out = out_s[rows_, :]
            r = lax.rsqrt(jnp.mean(out * out, axis=-1, keepdims=True) + NORM_EPS)
            nrm = out * r
            err = nrm * gf - t_ref[rows_, :]
            loss8 = loss8 + fold(err * err)
            dy = err * (1.0 / D_MODEL)
            dgf8 = dgf8 + fold(dy * nrm)
            dh = _rms_bwd(dy * gf, nrm, r)
            dh_ref[rows_, :] = dh.astype(_ST)
        loss_ref[...] = so_far(loss_ref) + (0.5 / D_MODEL) * jnp.sum(loss8, axis=0, keepdims=True)
        dgf_ref[...] = so_far(dgf_ref) + jnp.sum(dgf8, axis=0, keepdims=True)
        dmer_s[...] = _dot_nt(dh_ref[...], wo_ref[...])
        dwo_ref[...] = so_far(dwo_ref) + _dot_tn(mrg_s[...], dh_ref[...])
        for rows_ in slabs:
            sga, sgb = _sigmoid(ga_ref[rows_, :].astype(_F32)), _sigmoid(gb_ref[rows_, :].astype(_F32))
            dmer = dmer_s[rows_, :]
            da, db = dmer * sga, dmer * sgb
            dya_s[rows_, :] = da.astype(_MX)
            dyb_s[rows_, :] = db.astype(_MX)
            dga_ref[rows_, :] = (da * ya_s[rows_, :] * (1.0 - sga)).astype(_ST)
            dgb_ref[rows_, :] = (db * yb_s[rows_, :] * (1.0 - sgb)).astype(_ST)
        doa_ref[...] = _dot_nt(dya_s[...], wa_ref[...]).astype(_ST)
        dob_ref[...] = _dot_nt(dyb_s[...], wb_ref[...]).astype(_ST)
        dwa_ref[...] = so_far(dwa_ref) + _dot_tn(oa_ref[...], dya_s[...])
        dwb_ref[...] = so_far(dwb_ref) + _dot_tn(ob_ref[...], dyb_s[...])

    rows = lambda w: pl.BlockSpec((tm, w), lambda i: (i, 0))
    full = lambda a, b: pl.BlockSpec((a, b), lambda i: (0, 0))
    return pl.pallas_call(
        body, name="merge_loss", grid=(nt,),
        in_specs=[rows(512), rows(512), rows(D_MODEL), rows(D_MODEL), rows(D_MODEL), rows(D_MODEL),
                  full(A_WIDTH, D_MODEL), full(B_WIDTH, D_MODEL), full(D_MODEL, D_MODEL), full(1, D_MODEL)],
        out_specs=[rows(D_MODEL), rows(512), rows(512), rows(D_MODEL), rows(D_MODEL),
                   full(A_WIDTH, D_MODEL), full(B_WIDTH, D_MODEL), full(D_MODEL, D_MODEL),
                   full(1, D_MODEL), full(1, D_MODEL)],
        out_shape=[jax.ShapeDtypeStruct((t, D_MODEL), _ST), jax.ShapeDtypeStruct((t, 512), _ST),
                   jax.ShapeDtypeStruct((t, 512), _ST), jax.ShapeDtypeStruct((t, D_MODEL), _ST),
                   jax.ShapeDtypeStruct((t, D_MODEL), _ST),
                   jax.ShapeDtypeStruct((A_WIDTH, D_MODEL), _F32), jax.ShapeDtypeStruct((B_WIDTH, D_MODEL), _F32),
                   jax.ShapeDtypeStruct((D_MODEL, D_MODEL), _F32), jax.ShapeDtypeStruct((1, D_MODEL), _F32),
                   jax.ShapeDtypeStruct((1, D_MODEL), _F32)],
        scratch_shapes=[pltpu.VMEM((tm, D_MODEL), _F32)] * 4 + [pltpu.VMEM((tm, D_MODEL), _MX)] * 3,
        compiler_params=_cparams(("arbitrary",), VMEM_LIMIT),
    )(oa, ob, ga, gb, x2, tgt, wa, wb, wo, g_final)


def _in_proj_bwd_x(dpieces, wt, x2, dh2, g_in):
    t = x2.shape[0]
    tm = min(t, 256)
    np_ = len(PIECES)

    def body(*refs):
        dp_refs = refs[:np_]
        w_ref, x_ref, dh2_ref, g_ref, gx_ref, dg_ref = refs[np_:]

        @pl.when(pl.program_id(0) == 0)
        def _():
            dg_ref[...] = jnp.zeros_like(dg_ref)

        dh = None
        for (name, a, b), dp in zip(PIECES, dp_refs):
            part = _dot(dp[...], w_ref[a:b, :])
            dh = part if dh is None else dh + part
        xv = x_ref[...]
        r = lax.rsqrt(jnp.mean(xv * xv, axis=-1, keepdims=True) + NORM_EPS)
        nrm = xv * r
        dg_ref[...] = dg_ref[...] + jnp.sum(dh * nrm, axis=0, keepdims=True)
        gx_ref[...] = dh2_ref[...].astype(_F32) + _rms_bwd(dh * g_ref[...], nrm, r)

    rows = lambda w: pl.BlockSpec((tm, w), lambda i: (i, 0))
    full = lambda a, b: pl.BlockSpec((a, b), lambda i: (0, 0))
    return pl.pallas_call(
        body, name="in_proj_bwd_x", grid=(t // tm,),
        in_specs=[rows(b - a) for _, a, b in PIECES] + [full(D_IN, D_MODEL), rows(D_MODEL), rows(D_MODEL),
                                                          full(1, D_MODEL)],
        out_specs=[rows(D_MODEL), full(1, D_MODEL)],
        out_shape=[jax.ShapeDtypeStruct((t, D_MODEL), _F32), jax.ShapeDtypeStruct((1, D_MODEL), _F32)],
        compiler_params=_cparams(("arbitrary",), VMEM_LIMIT),
    )(*dpieces, wt, x2, dh2, g_in)


def _in_proj_bwd_w(h, dpieces):
    t = h.shape[0]
    tm = min(t, 512)
    nt = t // tm
    np_ = len(PIECES)

    def body(*refs):
        h_ref, dp_refs, out_ref = refs[0], refs[1:1 + np_], refs[1 + np_]
        acc_ref, sem = refs[2 + np_:]
        i = pl.program_id(0)
        hv = h_ref[...]
        writes = []
        for j, ((name, a, b), dp) in enumerate(zip(PIECES, dp_refs)):
            part = _dot_tn(dp[...], hv)
            if name == "alr":
                b = a + B_GATE_RANK
                part = part[0:B_GATE_RANK]
            acc_ref[a:b, :] = jnp.where(i == 0, 0.0, acc_ref[a:b, :]) + part
            writes.append(pltpu.make_async_copy(acc_ref.at[a:b], out_ref.at[a:b], sem.at[j]))

            @pl.when(i == nt - 1)
            def _(cp=writes[-1]):
                cp.start()

        @pl.when(i == nt - 1)
        def _():
            for cp in writes:
                cp.wait()

    rows = lambda w: pl.BlockSpec((tm, w), lambda i: (i, 0))
    return pl.pallas_call(
        body, name="in_proj_bwd_w", grid=(nt,),
        in_specs=[rows(D_MODEL)] + [rows(b - a) for _, a, b in PIECES],
        out_specs=_ANY, out_shape=jax.ShapeDtypeStruct((D_IN, D_MODEL), _F32),
        scratch_shapes=[pltpu.VMEM((D_IN, D_MODEL), _F32), pltpu.SemaphoreType.DMA((np_,))],
        compiler_params=_cparams(("arbitrary",), VMEM_LIMIT),
    )(h, *dpieces)


def _place():
    return lax.axis_index("x"), lax.axis_index("y"), lax.axis_index("c")


def _other_chips(x, y):
    return [(1 - x, y), (x, 1 - y), (1 - x, 1 - y)]


class _Split(NamedTuple):
    by_rows: bool
    step: int
    size: int

    def half(self, ref, c):
        r, n = ref.shape[-2:]
        if self.by_rows:
            return ref.at[:, pl.ds(pl.multiple_of(c * (n // 2), LANE), n // 2)]
        return ref.at[pl.ds(pl.multiple_of(c * (r // 2), 16), r // 2), :]

    def chip_part(self, ref, k):
        if self.by_rows:
            return ref.at[pl.ds(pl.multiple_of(k * self.step, 16), self.size), :]
        return ref.at[:, pl.ds(pl.multiple_of(k * self.size, LANE), self.size)]

    def half_shape(self, shape):
        r, n = shape
        return (r, n // 2) if self.by_rows else (r // 2, n)

    def part_shape(self, shape):
        r, n = shape
        return (self.size, n) if self.by_rows else (r, self.size)


SPLIT_W_IN_T = _Split(True, WINDOW_STEP, WINDOW_ROWS)
SPLIT_W_O = _Split(True, 256, 256)
SPLIT_W_OUT = _Split(False, 256, 256)


def _gather_weights(shards, splits, fulls, pos_f):
    nw = len(shards)
    t = pos_f.shape[0]

    def body(*refs):
        ins, (pos_ref, c_ref) = refs[:nw], refs[nw:nw + 2]
        outs, tables = refs[nw + 2:2 * nw + 2], refs[2 * nw + 2:2 * nw + 5]
        send_a, recv_a, send_b, recv_b = refs[2 * nw + 5:]
        x, y, c = _place()
        me = 2 * x + y
        peers = _other_chips(x, y)

        def place(i, k, half):
            if splits[i] is None:
                return outs[i].at[k]
            if fulls[i][0] == 4 and len(fulls[i]) == 3:
                whole = outs[i].at[k]
            else:
                whole = splits[i].chip_part(outs[i], k)
            return splits[i].half(whole, half)

        first, passed = [], []
        for i in range(nw):
            src = ins[i] if splits[i] is None else splits[i].half(ins[i], c)
            for j, (px, py) in enumerate(peers):
                cp = pltpu.make_async_remote_copy(
                    src_ref=src, dst_ref=place(i, me, c), send_sem=send_a.at[3 * i + j],
                    recv_sem=recv_a.at[3 * i + j], device_id=(px, py, c), device_id_type=_MESH)
                cp.start()
                first.append(cp)
        _rope_tables_into(pos_ref, c_ref, *tables)
        for i in range(nw):
            for j, (px, py) in enumerate(peers):
                landed = place(i, 2 * px + py, c)
                pltpu.make_async_remote_copy(
                    src_ref=landed, dst_ref=landed, send_sem=send_a.at[3 * i + j], recv_sem=recv_a.at[3 * i + j],
                    device_id=(px, py, c), device_id_type=_MESH).wait_recv()
                if splits[i] is not None:
                    cp = pltpu.make_async_remote_copy(
                        src_ref=landed, dst_ref=landed, send_sem=send_b.at[3 * i + j], recv_sem=recv_b.at[3 * i + j],
                        device_id=(x, y, 1 - c), device_id_type=_MESH)
                    cp.start()
                    passed.append(cp)
        for i in range(nw):
            if splits[i] is None:
                continue
            for j, (px, py) in enumerate(peers):
                theirs = place(i, 2 * px + py, 1 - c)
                pltpu.make_async_remote_copy(
                    src_ref=theirs, dst_ref=theirs, send_sem=send_b.at[3 * i + j], recv_sem=recv_b.at[3 * i + j],
                    device_id=(x, y, 1 - c), device_id_type=_MESH).wait_recv()
        for cp in first + passed:
            cp.wait_send()

    vm = pl.BlockSpec(memory_space=pltpu.VMEM)
    tab = jax.ShapeDtypeStruct((t, LANE), _F32)
    return pl.pallas_call(
        body, name="gather_weights",
        in_specs=[_ANY] * nw + [vm, vm], out_specs=[_ANY] * nw + [vm] * 3,
        out_shape=[jax.ShapeDtypeStruct(f, s.dtype) for f, s in zip(fulls, shards)] + [tab] * 3,
        scratch_shapes=[pltpu.SemaphoreType.DMA((3 * nw,)) for _ in range(4)],
        compiler_params=_cparams(None, VMEM_LIMIT),
    )(*shards, pos_f, _rope_consts())


def _assemble_w_in_t(slots):
    bw = 256
    ov = WINDOW_ROWS - WINDOW_STEP

    def body(s_ref, o_ref):
        for k in range(4):
            base = k * WINDOW_STEP
            lo = 0 if k == 0 else ov
            if k > 0:
                o_ref[base:base + ov, :] = s_ref[k - 1, WINDOW_STEP:WINDOW_ROWS, :] + s_ref[k, 0:ov, :]
            hi = WINDOW_ROWS if k == 3 else WINDOW_STEP
            o_ref[base + lo:base + hi, :] = s_ref[k, lo:hi, :]

    return pl.pallas_call(
        body, name="assemble_w_in_t", grid=(D_MODEL // bw,),
        in_specs=[pl.BlockSpec((4, WINDOW_ROWS, bw), lambda i: (0, 0, i))],
        out_specs=pl.BlockSpec((D_IN, bw), lambda i: (0, i)),
        out_shape=jax.ShapeDtypeStruct((D_IN, D_MODEL), slots.dtype),
        compiler_params=_cparams(("parallel",)),
    )(slots)


def _pair_exchange(grads, splits):
    nw = len(grads)

    def body(*refs):
        ins, outs = refs[:nw], refs[nw:2 * nw]
        send, recv = refs[2 * nw:]
        x, y, c = _place()
        copies = []
        for i in range(nw):
            cp = pltpu.make_async_remote_copy(
                src_ref=splits[i].half(ins[i], 1 - c), dst_ref=outs[i], send_sem=send.at[i], recv_sem=recv.at[i],
                device_id=(x, y, 1 - c), device_id_type=_MESH)
            cp.start()
            copies.append(cp)
        for cp in copies:
            cp.wait()

    return pl.pallas_call(
        body, name="grad_pair_exchange",
        in_specs=[_ANY] * nw, out_specs=[_ANY] * nw,
        out_shape=[jax.ShapeDtypeStruct(sp.half_shape(g.shape), g.dtype) for g, sp in zip(grads, splits)],
        scratch_shapes=[pltpu.SemaphoreType.DMA((nw,)), pltpu.SemaphoreType.DMA((nw,))],
    )(*grads)


def _row_block(rows):
    for cand in (976, 176, 256, 128):
        if rows % cand == 0:
            return cand
    return rows


def _pair_sum(g, r, split, c_arr, name):
    hr, hn = r.shape
    br = _row_block(hr)
    if split.by_rows:
        g_spec = pl.BlockSpec((br, hn), lambda i, c_ref: (i, c_ref[0]))
    else:
        g_spec = pl.BlockSpec((br, hn), lambda i, c_ref: (c_ref[0] * (hr // br) + i, 0))

    def body(c_ref, g_ref, r_ref, o_ref):
        o_ref[...] = (g_ref[...] + r_ref[...]).astype(o_ref.dtype)

    return pl.pallas_call(
        body, name=name,
        grid_spec=pltpu.PrefetchScalarGridSpec(
            num_scalar_prefetch=1, grid=(hr // br,),
            in_specs=[g_spec, pl.BlockSpec((br, hn), lambda i, c_ref: (i, 0))],
            out_specs=pl.BlockSpec((br, hn), lambda i, c_ref: (i, 0))),
        out_shape=jax.ShapeDtypeStruct(r.shape, _MX),
        compiler_params=_cparams(("parallel",)),
    )(c_arr, g, r)


def _chip_exchange(parts, splits, small):
    nw = len(parts)

    def body(*refs):
        ins, small_ref, outs, all_ref = refs[:nw], refs[nw], refs[nw + 1:2 * nw + 1], refs[2 * nw + 1]
        send, recv, lsem, s_send, s_recv = refs[2 * nw + 2:]
        x, y, c = _place()
        me = 2 * x + y
        dev = 2 * me + c
        copies = [pltpu.make_async_copy(small_ref, all_ref.at[dev], lsem.at[nw])]
        copies[0].start()
        for r in range(1, 8):
            peer = (1 - x if r & 4 else x, 1 - y if r & 2 else y, 1 - c if r & 1 else c)
            cp = pltpu.make_async_remote_copy(
                src_ref=small_ref, dst_ref=all_ref.at[dev], send_sem=s_send.at[r - 1], recv_sem=s_recv.at[r - 1],
                device_id=peer, device_id_type=_MESH)
            cp.start()
            copies.append(cp)
        for i in range(nw):
            mine = pltpu.make_async_copy(splits[i].chip_part(ins[i], me), outs[i].at[me], lsem.at[i])
            mine.start()
            copies.append(mine)
            for j, (px, py) in enumerate(_other_chips(x, y)):
                cp = pltpu.make_async_remote_copy(
                    src_ref=splits[i].chip_part(ins[i], 2 * px + py), dst_ref=outs[i].at[me],
                    send_sem=send.at[3 * i + j], recv_sem=recv.at[3 * i + j],
                    device_id=(px, py, c), device_id_type=_MESH)
                cp.start()
                copies.append(cp)
        for cp in copies:
            cp.wait()

    return pl.pallas_call(
        body, name="grad_chip_exchange",
        in_specs=[_ANY] * (nw + 1), out_specs=[_ANY] * (nw + 1),
        out_shape=[jax.ShapeDtypeStruct((4,) + sp.part_shape(p.shape), p.dtype) for p, sp in zip(parts, splits)]
        + [jax.ShapeDtypeStruct((8,) + small.shape, small.dtype)],
        scratch_shapes=[pltpu.SemaphoreType.DMA((3 * nw,)), pltpu.SemaphoreType.DMA((3 * nw,)),
                        pltpu.SemaphoreType.DMA((nw + 1,)), pltpu.SemaphoreType.DMA((7,)),
                        pltpu.SemaphoreType.DMA((7,))],
    )(*parts, small)


def _sum_chips(q, split, c_arr, name):
    _, hr, hn = q.shape
    br = _row_block(hr)
    if split.by_rows:
        out_shape = (hr, 2 * hn)
        o_spec = pl.BlockSpec((br, hn), lambda i, c_ref: (i, c_ref[0]))
    else:
        out_shape = (2 * hr, hn)
        o_spec = pl.BlockSpec((br, hn), lambda i, c_ref: (c_ref[0] * (hr // br) + i, 0))

    def body(c_ref, q_ref, o_ref):
        f = lambda k: q_ref[k].astype(_F32)
        o_ref[...] = ((f(0) + f(1)) + f(2)) + f(3)

    return pl.pallas_call(
        body, name=name,
        grid_spec=pltpu.PrefetchScalarGridSpec(
            num_scalar_prefetch=1, grid=(hr // br,),
            in_specs=[pl.BlockSpec((4, br, hn), lambda i, c_ref: (0, i, 0))], out_specs=o_spec),
        out_shape=jax.ShapeDtypeStruct(out_shape, _F32),
        compiler_params=_cparams(("parallel",)),
    )(c_arr, q)


def _pair_share(bufs, splits):
    nw = len(bufs)

    def body(*refs):
        ins, outs = refs[:nw], refs[nw:2 * nw]
        send, recv = refs[2 * nw:]
        x, y, c = _place()
        copies = []
        for i in range(nw):
            cp = pltpu.make_async_remote_copy(
                src_ref=splits[i].half(ins[i], c), dst_ref=splits[i].half(outs[i], c), send_sem=send.at[i],
                recv_sem=recv.at[i], device_id=(x, y, 1 - c), device_id_type=_MESH)
            cp.start()
            copies.append(cp)
        for cp in copies:
            cp.wait()

    return pl.pallas_call(
        body, name="grad_pair_share",
        in_specs=[_ANY] * nw, out_specs=[_ANY] * nw,
        out_shape=[jax.ShapeDtypeStruct(b.shape, b.dtype) for b in bufs],
        input_output_aliases={i: i for i in range(nw)},
        scratch_shapes=[pltpu.SemaphoreType.DMA((nw,)), pltpu.SemaphoreType.DMA((nw,))],
    )(*bufs)


def _sum_devices(parts):
    def body(p_ref, tot_ref):
        acc = p_ref[0]
        for d in range(1, 8):
            acc = acc + p_ref[d]
        tot_ref[...] = acc

    vm = pl.BlockSpec(memory_space=pltpu.VMEM)
    return pl.pallas_call(
        body, name="small_sum", in_specs=[vm], out_specs=vm,
        out_shape=jax.ShapeDtypeStruct(parts.shape[1:], parts.dtype),
    )(parts)


def _adamw(w, g, m, v, name):
    lead = w.shape[0] != 1
    r, n = (w.shape[0], w.shape[2]) if lead else w.shape[1:]
    br = r
    for cand in (256, 244, 128):
        if r > cand and r % cand == 0:
            br = cand
            break

    def body(w_ref, g_ref, m_ref, v_ref, d_ref, nm_ref, nv_ref):
        gv = g_ref[...]
        m2 = ADAM_B1 * m_ref[...] + (1.0 - ADAM_B1) * gv
        v2 = ADAM_B2 * v_ref[...] + (1.0 - ADAM_B2) * (gv * gv)
        m_hat = m2 / (1.0 - ADAM_B1 ** ADAM_STEP)
        v_hat = v2 / (1.0 - ADAM_B2 ** ADAM_STEP)
        d_ref[...] = -ADAM_LR * (m_hat / (jnp.sqrt(v_hat) + ADAM_EPS) + ADAM_WD * w_ref[...])
        nm_ref[...] = m2
        nv_ref[...] = v2

    blk = pl.BlockSpec((br, 1, n), lambda i: (i, 0, 0)) if lead else pl.BlockSpec((None, br, n), lambda i: (0, i, 0))
    shp = jax.ShapeDtypeStruct(w.shape, _F32)
    return pl.pallas_call(
        body, name=name, grid=(r // br,),
        in_specs=[blk] * 4, out_specs=[blk] * 3, out_shape=[shp] * 3,
        compiler_params=_cparams(("parallel",)),
    )(w, g, m, v)


def kernel(x, positions, g_in, w_in, w_alpha_up, b_alpha, attn_sinks, g_gla_norm, w_out_a, w_out_b, w_o, g_final, loss_target, m_g_in, m_w_in, m_w_alpha_up, m_b_alpha, m_attn_sinks, m_g_gla_norm, m_w_out_a, m_w_out_b, m_w_o, m_g_final, v_g_in, v_w_in, v_w_alpha_up, v_b_alpha, v_attn_sinks, v_g_gla_norm, v_w_out_a, v_w_out_b, v_w_o, v_g_final):
    nseq, seq, _ = x.shape
    t = nseq * seq
    cx, cy, cc = _place()
    chip = 2 * cx + cy
    c_arr = jnp.reshape(cc, (1,)).astype(jnp.int32)

    tr = lambda w: jnp.transpose(w, (2, 0, 1))
    w_in_t = tr(w_in).reshape(SHARD, D_MODEL).astype(_MX)
    pad = WINDOW_ROWS - SHARD
    window = lax.switch(chip, [lambda w, k=k: jnp.pad(w, ((4 * k, pad - 4 * k), (0, 0))) for k in range(4)], w_in_t)
    shards = [window, w_out_a[0].astype(_MX), w_out_b[0].astype(_MX), w_o[0].astype(_MX), w_alpha_up[0].astype(_MX)]
    splits = [SPLIT_W_IN_T, SPLIT_W_OUT, SPLIT_W_OUT, SPLIT_W_O, None]
    fulls = [(4, WINDOW_ROWS, D_MODEL), (A_WIDTH, D_MODEL), (B_WIDTH, D_MODEL), (D_MODEL, D_MODEL),
             (4, B_GATE_RANK, B_KEY_WIDTH // 4)]
    pos_f = positions.astype(_F32).reshape(t, 1)
    win_g, wa, wb, wo, wup_g, cos, sa, sb = _gather_weights(shards, splits, fulls, pos_f)
    nsh = D_MODEL // 4
    win_g = lax.dynamic_update_slice(win_g, window[None], (chip, 0, 0))
    wa = lax.dynamic_update_slice(wa, shards[1], (0, nsh * chip))
    wb = lax.dynamic_update_slice(wb, shards[2], (0, nsh * chip))
    wo = lax.dynamic_update_slice(wo, shards[3], (nsh * chip, 0))
    wup_g = lax.dynamic_update_slice(wup_g, shards[4][None], (chip, 0, 0))
    wt = _assemble_w_in_t(win_g)
    wup = jnp.concatenate([jnp.transpose(wup_g, (1, 0, 2)).reshape(B_GATE_RANK, B_KEY_WIDTH),
                           jnp.zeros((LANE - B_GATE_RANK, B_KEY_WIDTH), _MX)], axis=0)

    x2 = x.reshape(t, D_MODEL)
    tgt = loss_target.reshape(t, D_MODEL)
    sinks = attn_sinks.reshape(A_HEADS)
    gf = g_final.reshape(1, D_MODEL)

    h, qkv, za, qkb, vb, zb, alr, ga, gb = _in_proj(x2, g_in, wt, cos, sa, sb)
    oa = _attn_fwd(qkv, za, sinks, nseq)
    ob, oraw, sst = _gla_fwd(qkb, vb, zb, alr, wup, b_alpha, g_gla_norm, nseq)

    dh2, doa, dob, dga, dgb, dwa, dwb, dwo, dgf, lossv = _merge_loss(oa, ob, ga, gb, x2, tgt, wa, wb, wo, gf)

    dqkv, dza, dsink = _attn_bwd(qkv, za, doa, sinks, cos, sa, sb, nseq)
    dqkb, dvb, dzb, dalr, dwup, dba, dgn = _gla_bwd(qkb, vb, zb, alr, oraw, dob, sst, wup, b_alpha, g_gla_norm, nseq)
    dpieces = [dqkv, dza, dqkb, dvb, dzb, dalr, dga, dgb]
    grad_x2, dgin = _in_proj_bwd_x(dpieces, wt, x2, dh2, g_in)
    dwin_t = _in_proj_bwd_w(h, dpieces)

    grads = [dwin_t, dwa, dwb, dwo]
    gsplits = [SPLIT_W_IN_T, SPLIT_W_OUT, SPLIT_W_OUT, SPLIT_W_O]
    names = ("w_in", "w_out_a", "w_out_b", "w_o")
    from_sibling = _pair_exchange(grads, gsplits)
    pair_sums = [_pair_sum(g, r, sp, c_arr, "pair_sum_" + nm)
                 for g, r, sp, nm in zip(grads, from_sibling, gsplits, names)]
    small = jnp.concatenate([
        dgin, dgf, dgn, dba,
        jnp.pad(dsink[:, 0].reshape(1, A_HEADS), ((0, 0), (0, LANE - A_HEADS))),
        jnp.pad(jnp.sum(lossv, axis=1, keepdims=True), ((0, 0), (0, LANE - 1))),
        dwup[:B_GATE_RANK].reshape(1, B_GATE_RANK * B_KEY_WIDTH)], axis=1)
    *from_chips, small_parts = _chip_exchange(pair_sums, gsplits, jnp.pad(small, ((0, 7), (0, 0))))
    reduced = [_sum_chips(q, sp, c_arr, "chip_sum_" + nm) for q, sp, nm in zip(from_chips, gsplits, names)]
    g_window, g_wa, g_wb, g_wo = _pair_share(reduced, gsplits)
    g_win_t = lax.switch(chip, [lambda w, k=k: w[4 * k:4 * k + SHARD].reshape(SHARD, 1, D_MODEL) for k in range(4)],
                         g_window)
    tot = _sum_devices(small_parts)[0:1]
    o = 0
    def take(n):
        nonlocal o
        o += n
        return tot[:, o - n:o]
    g_gin, g_gf, g_gn, g_ba = take(D_MODEL), take(D_MODEL), take(B_WIDTH), take(B_KEY_WIDTH)
    g_sink = take(LANE)[:, :A_HEADS]
    loss = take(LANE)[0, 0]
    g_wup_full = take(B_GATE_RANK * B_KEY_WIDTH).reshape(B_GATE_RANK, B_KEY_WIDTH)
    nup = B_KEY_WIDTH // 4
    g_wup = lax.dynamic_slice(g_wup_full, (0, chip * nup), (B_GATE_RANK, nup))

    def pack(*parts):
        return jnp.concatenate([p.reshape(1, -1) for p in parts], axis=1)

    sm_w = pack(g_in, g_final, g_gla_norm, b_alpha, attn_sinks, w_alpha_up)
    sm_g = pack(g_gin, g_gf, g_gn, g_ba, g_sink, g_wup)
    sm_m = pack(m_g_in, m_g_final, m_g_gla_norm, m_b_alpha, m_attn_sinks, m_w_alpha_up)
    sm_v = pack(v_g_in, v_g_final, v_g_gla_norm, v_b_alpha, v_attn_sinks, v_w_alpha_up)
    sm_out = [p[0] for p in _adamw(sm_w[None], sm_g[None], sm_m[None], sm_v[None], "adamw_small")]

    def unpack(p):
        sizes = (D_MODEL, D_MODEL, B_WIDTH, B_KEY_WIDTH, A_HEADS, B_GATE_RANK * nup)
        outs, at = [], 0
        for s in sizes:
            outs.append(p[:, at:at + s])
            at += s
        gi, gfin, gnn, ba, sk, wu = outs
        return dict(g_in=gi, g_final=gfin.reshape(D_MODEL), g_gla_norm=gnn, b_alpha=ba, attn_sinks=sk,
                    w_alpha_up=wu.reshape(1, B_GATE_RANK, nup))

    untr = lambda a: jnp.transpose(a, (1, 2, 0))
    big = dict(w_in=tuple(untr(a) for a in (g_win_t,) + tuple(_adamw(tr(w_in), g_win_t, tr(m_w_in), tr(v_w_in), "adamw_w_in"))))
    for nm, w, g, m, v in (("w_out_a", w_out_a, g_wa, m_w_out_a, v_w_out_a),
                           ("w_out_b", w_out_b, g_wb, m_w_out_b, v_w_out_b), ("w_o", w_o, g_wo, m_w_o, v_w_o)):
        big[nm] = (g[None],) + tuple(_adamw(w, g[None], m, v, "adamw_" + nm))

    order = ("g_in", "w_in", "w_alpha_up", "b_alpha", "attn_sinks", "g_gla_norm", "w_out_a", "w_out_b", "w_o", "g_final")
    small_sets = [unpack(sm_g)] + [unpack(p) for p in sm_out]
    outs = []
    for kind in range(4):
        for nm in order:
            outs.append(big[nm][kind] if nm in big else small_sets[kind][nm])
    return (loss, grad_x2.reshape(x.shape), *outs)
```

```python
import math
from typing import NamedTuple

import numpy as np
import jax
import jax.numpy as jnp
from jax import lax
from jax.experimental import pallas as pl
from jax.experimental.pallas import tpu as pltpu

D_MODEL = 1024
A_HEADS, A_KV_HEADS, A_HEAD_DIM = 8, 2, 64
A_GROUP = A_HEADS // A_KV_HEADS
A_WIDTH, A_KV_WIDTH = 512, 128
BLOCK = 128
ROPE_THETA = 500000.0
ROPE_DIM = 16
B_HEADS, B_KEY_DIM, B_VAL_DIM = 4, 64, 128
B_KEY_WIDTH, B_WIDTH = 256, 512
B_GATE_RANK = 16
B_GATE_TEMP = 16.0
B_CHUNK = 64
NORM_EPS = 1e-6
NEG_BIG = -1e30
D_IN = 4880

ADAM_LR, ADAM_B1, ADAM_B2, ADAM_EPS, ADAM_WD, ADAM_STEP = 0.001, 0.9, 0.999, 1e-08, 0.01, 10

LANE = 128
ALR_AT = 2816
PIECES = (("qkv", 0, 768), ("za", 768, 1280), ("qkb", 1280, 1792), ("vb", 1792, 2304),
          ("zb", 2304, 2816), ("alr", ALR_AT, ALR_AT + LANE), ("ga", 2832, 3856), ("gb", 3856, 4880))
SHARD = D_IN // 4
WINDOW_STEP = 1216
WINDOW_ROWS = 1232

GLA_BLOCK = 256
MERGE_SLAB = 16
VMEM_LIMIT = 56 * 1024 * 1024

_F32 = jnp.float32
_MX = jnp.bfloat16
_ST = jnp.bfloat16

_MESH = pl.DeviceIdType.MESH
_ANY = pl.BlockSpec(memory_space=pl.ANY)


def _cparams(sem=None, vmem=None):
    return pltpu.CompilerParams(dimension_semantics=sem, vmem_limit_bytes=vmem)


def _dot(a, b):
    return jnp.dot(a.astype(_MX), b.astype(_MX), preferred_element_type=_F32)


def _dot_nt(a, b):
    return lax.dot_general(a.astype(_MX), b.astype(_MX), (((1,), (1,)), ((), ())),
                           preferred_element_type=_F32)


def _dot_tn(a, b):
    return lax.dot_general(a.astype(_MX), b.astype(_MX), (((0,), (0,)), ((), ())),
                           preferred_element_type=_F32)


def _dot_ones(ones_mat, v):
    o = ones_mat.astype(jnp.bfloat16)
    v0 = v.astype(jnp.bfloat16)
    r1 = v - v0.astype(_F32)
    v1 = r1.astype(jnp.bfloat16)
    v2 = (r1 - v1.astype(_F32)).astype(jnp.bfloat16)
    d = lambda t: jnp.dot(o, t, preferred_element_type=_F32)
    return d(v0) + d(v1) + d(v2)


def _sigmoid(x):
    return 0.5 * jnp.tanh(0.5 * x) + 0.5


def _log_sigmoid(x):
    return jnp.minimum(x, 0.0) - jnp.log(1.0 + jnp.exp(-jnp.abs(x)))


def _lane_tile(t, width):
    reps = width // t.shape[1]
    return t if reps == 1 else jnp.tile(t, (1, reps))


def _rope(t, cos, sa, sb, sign):
    w = t.shape[1]
    rot = pltpu.roll(t, w - 8, 1) * _lane_tile(sa, w) + pltpu.roll(t, 8, 1) * _lane_tile(sb, w)
    return t * _lane_tile(cos, w) + sign * rot


def _rms_bwd(dy_g, n, r):
    return r * (dy_g - n * jnp.mean(dy_g * n, axis=-1, keepdims=True))


ROPE_ROWS = 256


def _rope_consts():
    lane = np.arange(LANE) % A_HEAD_DIM
    half = ROPE_DIM // 2
    inv = np.exp((np.float32(-math.log(ROPE_THETA)) * np.arange(half, dtype=np.float32)) * np.float32(2.0 / ROPE_DIM))
    consts = np.zeros((8, LANE), np.float32)
    consts[0] = np.where(lane < ROPE_DIM, inv[lane % half], 0.0)
    consts[1] = np.where(lane < half, -1.0, 0.0)
    consts[2] = np.where((lane >= half) & (lane < ROPE_DIM), 1.0, 0.0)
    return jnp.asarray(consts)


def _rope_tables_into(pos_ref, c_ref, cos_ref, sa_ref, sb_ref):
    def rows_of(b, carry):
        rows = pl.ds(pl.multiple_of(b * ROPE_ROWS, ROPE_ROWS), ROPE_ROWS)
        ang = pos_ref[rows, :] * c_ref[0:1, :]
        s = jnp.sin(ang)
        cos_ref[rows, :] = jnp.cos(ang)
        sa_ref[rows, :] = s * c_ref[1:2, :]
        sb_ref[rows, :] = s * c_ref[2:3, :]
        return carry

    lax.fori_loop(0, pos_ref.shape[0] // ROPE_ROWS, rows_of, 0)


def _in_proj(x2, g_in, wt, cos, sa, sb):
    t = x2.shape[0]
    tm = min(t, 512)

    def body(x_ref, g_ref, w_ref, cos_ref, sa_ref, sb_ref, h_ref, qkv_ref, za_ref, qkb_ref,
             vb_ref, zb_ref, alr_ref, ga_ref, gb_ref):
        xv = x_ref[...]
        r = lax.rsqrt(jnp.mean(xv * xv, axis=-1, keepdims=True) + NORM_EPS)
        h = (xv * r * g_ref[...]).astype(_MX)
        h_ref[...] = h.astype(_ST)
        outs = dict(za=za_ref, qkb=qkb_ref, vb=vb_ref, zb=zb_ref, alr=alr_ref, ga=ga_ref, gb=gb_ref)
        for name, a, b in PIECES:
            p = _dot_nt(h, w_ref[a:b, :])
            if name == "qkv":
                c, s1, s2 = cos_ref[...], sa_ref[...], sb_ref[...]
                qkv_ref[:, 0:512] = _rope(p[:, 0:512], c, s1, s2, 1.0).astype(_ST)
                qkv_ref[:, 512:640] = _rope(p[:, 512:640], c, s1, s2, 1.0).astype(_ST)
                qkv_ref[:, 640:768] = p[:, 640:768].astype(_ST)
            else:
                outs[name][...] = p.astype(outs[name].dtype)

    rows = lambda w: pl.BlockSpec((tm, w), lambda i: (i, 0))
    shp = lambda name, w: jax.ShapeDtypeStruct((t, w), _F32 if name == "qkb" else _ST)
    widths = [D_MODEL] + [b - a for _, a, b in PIECES]
    return pl.pallas_call(
        body, name="in_proj", grid=(t // tm,),
        in_specs=[rows(D_MODEL), pl.BlockSpec((1, D_MODEL), lambda i: (0, 0)),
                  pl.BlockSpec((D_IN, D_MODEL), lambda i: (0, 0), pipeline_mode=pl.Buffered(1)),
                  rows(LANE), rows(LANE), rows(LANE)],
        out_specs=[rows(w) for w in widths],
        out_shape=[shp(n, w) for n, w in zip(["h"] + [p[0] for p in PIECES], widths)],
        compiler_params=_cparams(("parallel",), VMEM_LIMIT),
    )(x2, g_in, wt, cos, sa, sb)


def _attn_operands(qkv_ref, kvp_ref, want_bwd):
    kf = jnp.concatenate([kvp_ref[:, 0:128], qkv_ref[:, 512:640]], axis=0).astype(_F32) * (A_HEAD_DIM ** -0.5)
    vf = jnp.concatenate([kvp_ref[:, 128:256], qkv_ref[:, 640:768]], axis=0).astype(_F32)
    lo = lax.broadcasted_iota(jnp.int32, (1, LANE), 1) < 64

    def on_lanes(a):
        sw = pltpu.roll(a, 64, 1)
        z = jnp.zeros_like(a)
        return [[jnp.where(lo, a, z).astype(_MX), jnp.where(lo, z, sw).astype(_MX)],
                [jnp.where(lo, sw, z).astype(_MX), jnp.where(lo, z, a).astype(_MX)]]

    def on_rows(a):
        at = a.T.astype(_MX)
        z = jnp.zeros((64, at.shape[1]), _MX)
        top, bot = at[0:64], at[64:128]
        return [[jnp.concatenate([top, z], axis=0), jnp.concatenate([z, top], axis=0)],
                [jnp.concatenate([bot, z], axis=0), jnp.concatenate([z, bot], axis=0)]]

    ops = dict(k_lanes=on_lanes(kf), v_rows=on_rows(vf), lo=lo)
    if want_bwd:
        ops.update(v_lanes=on_lanes(vf), k_rows=on_rows(kf))
    return ops


def _attn_valid(n):
    kj = lax.broadcasted_iota(jnp.int32, (2 * BLOCK, 2 * BLOCK), 0) - BLOCK
    qi = lax.broadcasted_iota(jnp.int32, (2 * BLOCK, 2 * BLOCK), 1) & (BLOCK - 1)
    return (kj <= qi) & (qi - kj < BLOCK) & ((n > 0) | (kj >= 0))


def _attn_sinks(sink_ref, h_a, h_b):
    first = lax.broadcasted_iota(jnp.int32, (1, 2 * BLOCK), 1) < BLOCK
    return jnp.where(first, sink_ref[h_a], sink_ref[h_b])


def _attn_softmax_t(k_lanes, q_pair, valid, sink):
    s = jnp.where(valid, _dot_nt(k_lanes, q_pair), NEG_BIG)
    m = jnp.maximum(jnp.max(s, axis=0, keepdims=True), sink)
    e = jnp.exp(s - m)
    e_sink = jnp.exp(sink - m)
    inv = 1.0 / (jnp.sum(e, axis=0, keepdims=True) + e_sink)
    return e, e_sink, inv


def _attn_fwd(qkv, za, sinks, nseq):
    t = qkv.shape[0]
    nb = t // nseq // BLOCK

    def body(sink_ref, qkv_ref, kvp_ref, za_ref, oa_ref):
        n = pl.program_id(1)
        ops = _attn_operands(qkv_ref, kvp_ref, False)
        valid = _attn_valid(n)[:, 0:BLOCK]
        for pr in range(A_HEADS // 2):
            lanes = slice(pr * LANE, (pr + 1) * LANE)
            g = pr // (A_GROUP // 2)
            q_pair = qkv_ref[:, lanes]
            ot = None
            for half in range(2):
                e, _, inv = _attn_softmax_t(ops["k_lanes"][g][half], q_pair, valid, sink_ref[2 * pr + half])
                part = _dot(ops["v_rows"][g][half], e) * inv
                ot = part if ot is None else ot + part
            z = za_ref[:, lanes].astype(_F32)
            oa_ref[:, lanes] = (ot.T * (z * _sigmoid(z))).astype(_ST)

    cur = lambda w: pl.BlockSpec((BLOCK, w), lambda s, n: (s * nb + n, 0))
    return pl.pallas_call(
        body, name="attn_fwd", grid=(nseq, nb),
        in_specs=[pl.BlockSpec(memory_space=pltpu.SMEM), cur(768),
                  pl.BlockSpec((BLOCK, 256), lambda s, n: (s * nb + jnp.maximum(n - 1, 0), 2)), cur(512)],
        out_specs=cur(512), out_shape=jax.ShapeDtypeStruct((t, A_WIDTH), _ST),
        compiler_params=_cparams(("parallel", "arbitrary")),
    )(sinks, qkv, qkv, za)


def _attn_bwd(qkv, za, doa, sinks, cos, sa, sb, nseq):
    t = qkv.shape[0]
    nb = t // nseq // BLOCK

    def body(sink_ref, qkv_ref, kvp_ref, za_ref, doa_ref, cos_ref, sa_ref, sb_ref,
             dqkv_ref, dza_ref, dsink_ref, ck_ref, cv_ref):
        s_id, i = pl.program_id(0), pl.program_id(1)
        n = nb - 1 - i

        @pl.when((s_id == 0) & (i == 0))
        def _():
            dsink_ref[...] = jnp.zeros_like(dsink_ref)

        @pl.when(i == 0)
        def _():
            ck_ref[...] = jnp.zeros_like(ck_ref)
            cv_ref[...] = jnp.zeros_like(cv_ref)

        ops = _attn_operands(qkv_ref, kvp_ref, True)
        lo = ops["lo"]
        valid = _attn_valid(n)
        dk_acc, dv_acc, dq_pairs = [], [], []
        for g in range(A_KV_HEADS):
            pairs = [slice((2 * g + j) * LANE, (2 * g + j + 1) * LANE) for j in range(2)]
            q_both = jnp.concatenate([qkv_ref[:, p] for p in pairs], axis=0)
            q_f = q_both.astype(_F32)
            z = [za_ref[:, p].astype(_F32) for p in pairs]
            sz = [_sigmoid(t) for t in z]
            d_oa = [doa_ref[:, p].astype(_F32) for p in pairs]
            d_att = jnp.concatenate([d_oa[j] * (z[j] * sz[j]) for j in range(2)], axis=0)
            zero = jnp.zeros_like(d_att)
            ot, dqt, ds_all, pn_all, qz_all, daz_all = None, None, [], [], [], []
            for half in range(2):
                heads = (4 * g + half, 4 * g + 2 + half)
                e, e_sink, inv = _attn_softmax_t(ops["k_lanes"][g][half], q_both, valid,
                                                 _attn_sinks(sink_ref, *heads))
                pn = e * inv
                dpt = _dot_nt(ops["v_lanes"][g][half], d_att)
                delta = jnp.sum(pn * dpt, axis=0, keepdims=True)
                ds = (pn * (dpt - delta)).astype(_MX)
                pn = pn.astype(_MX)
                d_sink = e_sink * inv * delta
                for j, h in enumerate(heads):
                    dsink_ref[h:h + 1, :] = dsink_ref[h:h + 1, :] - jnp.sum(d_sink[:, j * BLOCK:(j + 1) * BLOCK])
                o_part = _dot(ops["v_rows"][g][half], pn)
                dq_part = _dot(ops["k_rows"][g][half], ds)
                ot = o_part if ot is None else ot + o_part
                dqt = dq_part if dqt is None else dqt + dq_part
                mine = lo if half == 0 else jnp.logical_not(lo)
                ds_all.append(ds)
                pn_all.append(pn)
                qz_all.append(jnp.where(mine, q_f, zero).astype(_MX))
                daz_all.append(jnp.where(mine, d_att, zero).astype(_MX))
            dk_acc.append(_dot(jnp.concatenate(ds_all, axis=1), jnp.concatenate(qz_all, axis=0)))
            dv_acc.append(_dot(jnp.concatenate(pn_all, axis=1), jnp.concatenate(daz_all, axis=0)))
            for j, p in enumerate(pairs):
                cols = slice(j * BLOCK, (j + 1) * BLOCK)
                dza_ref[:, p] = (d_oa[j] * ot[:, cols].T * (sz[j] * (1.0 + z[j] * (1.0 - sz[j])))).astype(_ST)
                dq_pairs.append(dqt[:, cols].T)

        def fold(acc, scale):
            both = [a + pltpu.roll(a, 64, 1) for a in acc]
            return jnp.where(lo, both[0], both[1]) * scale

        dk_full = fold(dk_acc, A_HEAD_DIM ** -0.5)
        dv_full = fold(dv_acc, 1.0)
        dk_cur = dk_full[BLOCK:] + ck_ref[...]
        dv_cur = dv_full[BLOCK:] + cv_ref[...]
        ck_ref[...] = dk_full[:BLOCK]
        cv_ref[...] = dv_full[:BLOCK]
        c, s1, s2 = cos_ref[...], sa_ref[...], sb_ref[...]
        dqkv_ref[:, 0:512] = _rope(jnp.concatenate(dq_pairs, axis=1), c, s1, s2, -1.0).astype(_ST)
        dqkv_ref[:, 512:640] = _rope(dk_cur, c, s1, s2, -1.0).astype(_ST)
        dqkv_ref[:, 640:768] = dv_cur.astype(_ST)

    cur = lambda w: pl.BlockSpec((BLOCK, w), lambda s, i: (s * nb + nb - 1 - i, 0))
    return pl.pallas_call(
        body, name="attn_bwd", grid=(nseq, nb),
        in_specs=[pl.BlockSpec(memory_space=pltpu.SMEM), cur(768),
                  pl.BlockSpec((BLOCK, 256), lambda s, i: (s * nb + jnp.maximum(nb - 2 - i, 0), 2)),
                  cur(512), cur(512), cur(LANE), cur(LANE), cur(LANE)],
        out_specs=[cur(768), cur(512), pl.BlockSpec((8, LANE), lambda s, i: (0, 0))],
        out_shape=[jax.ShapeDtypeStruct((t, 768), _ST), jax.ShapeDtypeStruct((t, 512), _ST),
                   jax.ShapeDtypeStruct((8, LANE), _F32)],
        scratch_shapes=[pltpu.VMEM((BLOCK, A_KV_WIDTH), _F32), pltpu.VMEM((BLOCK, A_KV_WIDTH), _F32)],
        compiler_params=_cparams(("arbitrary", "arbitrary")),
    )(sinks, qkv, qkv, za, doa, cos, sa, sb)


def _gla_chunk_terms(la, qkb_ref, r0):
    g = la[r0:r0 + B_CHUNK, :]
    ri = lax.broadcasted_iota(jnp.int32, (B_CHUNK, B_CHUNK), 0)
    ci = lax.broadcasted_iota(jnp.int32, (B_CHUNK, B_CHUNK), 1)
    cum = _dot_ones((ri >= ci).astype(_F32), g)
    last = cum[B_CHUNK - 1:B_CHUNK, :]
    mid = cum[B_CHUNK // 2 - 1:B_CHUNK // 2, :]
    q = qkb_ref[r0:r0 + B_CHUNK, 0:B_KEY_WIDTH].astype(_F32) * (B_KEY_DIM ** -0.5)
    k = qkb_ref[r0:r0 + B_CHUNK, B_KEY_WIDTH:2 * B_KEY_WIDTH].astype(_F32)
    e_q, e_k, e_l, e_c = jnp.exp(cum - mid), jnp.exp(mid - cum), jnp.exp(last - cum), jnp.exp(cum)
    dec_col = jnp.exp(jnp.sum(g.T, axis=1, keepdims=True))
    return dict(qm=q * e_q, km=k * e_k, kl=k * e_l, qc=q * e_c, e_q=e_q, e_k=e_k, e_l=e_l, e_c=e_c,
                dec_col=dec_col, dec_row=jnp.exp(last), causal=ri >= ci, ri=ri)


def _gate_logits(alr_ref, wup_ref, b_ref):
    return _dot(alr_ref[...], wup_ref[...]) + b_ref[...]


def _gla_fwd(qkb, vb, zb, alr, wup, b_alpha, gn, nseq):
    t = qkb.shape[0]
    tb = min(GLA_BLOCK, t // nseq)
    nblk = t // nseq // tb
    cpb = tb // B_CHUNK

    def body(qkb_ref, vb_ref, zb_ref, alr_ref, wup_ref, b_ref, gn_ref, ob_ref, oraw_ref, sst_ref, s_ref):
        @pl.when(pl.program_id(1) == 0)
        def _():
            s_ref[...] = jnp.zeros_like(s_ref)

        la = _log_sigmoid(_gate_logits(alr_ref, wup_ref, b_ref)) * (1.0 / B_GATE_TEMP)
        terms = [_gla_chunk_terms(la, qkb_ref, c * B_CHUNK) for c in range(cpb)]
        o_intra, inc = {}, {}
        for c, tm in enumerate(terms):
            for h in range(B_HEADS):
                kl_, vl_ = slice(h * 64, (h + 1) * 64), slice(h * 128, (h + 1) * 128)
                v = vb_ref[c * B_CHUNK:(c + 1) * B_CHUNK, vl_]
                a = jnp.where(tm["causal"], _dot_nt(tm["qm"][:, kl_], tm["km"][:, kl_]), 0.0)
                o_intra[c, h] = _dot(a, v)
                inc[c, h] = _dot_tn(tm["kl"][:, kl_], v)
        o_heads = {}
        for h in range(B_HEADS):
            kl_ = slice(h * 64, (h + 1) * 64)
            st = s_ref[kl_, :]
            for c, tm in enumerate(terms):
                sst_ref[c, kl_, :] = st
                o_heads[c, h] = o_intra[c, h] + _dot(tm["qc"][:, kl_], st)
                st = tm["dec_col"][kl_, :] * st + inc[c, h]
            s_ref[kl_, :] = st
        o = jnp.concatenate([jnp.concatenate([o_heads[c, h] for h in range(B_HEADS)], axis=1)
                             for c in range(cpb)], axis=0)
        oraw_ref[...] = o
        z = zb_ref[...].astype(_F32)
        gate = z * _sigmoid(z)
        for h in range(B_HEADS):
            vl_ = slice(h * 128, (h + 1) * 128)
            oh = o[:, vl_]
            r = lax.rsqrt(jnp.mean(oh * oh, axis=-1, keepdims=True) + NORM_EPS)
            ob_ref[:, vl_] = ((oh * r) * gn_ref[:, vl_] * gate[:, vl_]).astype(_ST)

    rows = lambda w: pl.BlockSpec((tb, w), lambda s, i: (s * nblk + i, 0))
    full = lambda a, b: pl.BlockSpec((a, b), lambda s, i: (0, 0))
    return pl.pallas_call(
        body, name="gla_fwd", grid=(nseq, nblk),
        in_specs=[rows(512), rows(512), rows(512), rows(LANE), full(LANE, B_KEY_WIDTH),
                  full(1, B_KEY_WIDTH), full(1, B_WIDTH)],
        out_specs=[rows(512), rows(512),
                   pl.BlockSpec((cpb, B_KEY_WIDTH, B_VAL_DIM), lambda s, i: (s * nblk + i, 0, 0))],
        out_shape=[jax.ShapeDtypeStruct((t, B_WIDTH), _ST), jax.ShapeDtypeStruct((t, B_WIDTH), _F32),
                   jax.ShapeDtypeStruct((t // B_CHUNK, B_KEY_WIDTH, B_VAL_DIM), _F32)],
        scratch_shapes=[pltpu.VMEM((B_KEY_WIDTH, B_VAL_DIM), _F32)],
        compiler_params=_cparams(("parallel", "arbitrary")),
    )(qkb, vb, zb, alr, wup, b_alpha, gn)


def _gla_bwd(qkb, vb, zb, alr, oraw, dob, sst, wup, b_alpha, gn, nseq):
    t = qkb.shape[0]
    tb = min(GLA_BLOCK, t // nseq)
    nblk = t // nseq // tb
    cpb = tb // B_CHUNK

    def body(qkb_ref, vb_ref, zb_ref, alr_ref, oraw_ref, dob_ref, sst_ref, wup_ref, b_ref, gn_ref,
             dqkb_ref, dvb_ref, dzb_ref, dalr_ref, dwup_ref, db_ref, dgn_ref, ds_ref):
        s_id, i = pl.program_id(0), pl.program_id(1)

        @pl.when((s_id == 0) & (i == 0))
        def _():
            dwup_ref[...] = jnp.zeros_like(dwup_ref)
            db_ref[...] = jnp.zeros_like(db_ref)
            dgn_ref[...] = jnp.zeros_like(dgn_ref)

        @pl.when(i == 0)
        def _():
            ds_ref[...] = jnp.zeros_like(ds_ref)

        a_pre = _gate_logits(alr_ref, wup_ref, b_ref)
        la = _log_sigmoid(a_pre) * (1.0 / B_GATE_TEMP)

        z = zb_ref[...].astype(_F32)
        sz = _sigmoid(z)
        d_ob = dob_ref[...].astype(_F32)
        tg = d_ob * (z * sz)
        dsilu = sz * (1.0 + z * (1.0 - sz))
        do_cols, dgn_cols = [], []
        for h in range(B_HEADS):
            vl_ = slice(h * 128, (h + 1) * 128)
            oh = oraw_ref[:, vl_].astype(_F32)
            r = lax.rsqrt(jnp.mean(oh * oh, axis=-1, keepdims=True) + NORM_EPS)
            on = oh * r
            gnh = gn_ref[:, vl_]
            dzb_ref[:, vl_] = (d_ob[:, vl_] * (on * gnh) * dsilu[:, vl_]).astype(_ST)
            dgn_cols.append(jnp.sum(tg[:, vl_] * on, axis=0, keepdims=True))
            do_cols.append(_rms_bwd(tg[:, vl_] * gnh, on, r))
        dgn_ref[...] = dgn_ref[...] + jnp.concatenate(dgn_cols, axis=1)
        d_o = jnp.concatenate(do_cols, axis=1)

        ri = lax.broadcasted_iota(jnp.int32, (tb, tb), 0)
        ci = lax.broadcasted_iota(jnp.int32, (tb, tb), 1)
        same = (ri // B_CHUNK) == (ci // B_CHUNK)
        low = same & (ri >= ci)
        upto_mid = same & ((ci % B_CHUNK) < B_CHUNK // 2)
        sums = _dot_ones(jnp.concatenate([m.astype(_F32) for m in (low, same, upto_mid)], axis=0), la)
        cum, last, mid = sums[0:tb], sums[tb:2 * tb], sums[2 * tb:3 * tb]
        e_q, e_k, e_l, e_c = jnp.exp(cum - mid), jnp.exp(mid - cum), jnp.exp(last - cum), jnp.exp(cum)
        q = qkb_ref[:, 0:B_KEY_WIDTH] * (B_KEY_DIM ** -0.5)
        k = qkb_ref[:, B_KEY_WIDTH:2 * B_KEY_WIDTH]
        qm, km, kl, qc = q * e_q, k * e_k, k * e_l, q * e_c
        lane_head = lax.broadcasted_iota(jnp.int32, (1, B_KEY_WIDTH), 1) // B_KEY_DIM
        d_o_mx = d_o.astype(_MX)

        def on_diagonal(st):
            z = jnp.zeros((B_KEY_DIM, B_VAL_DIM), st.dtype)
            return jnp.concatenate([jnp.concatenate(
                [st[h * B_KEY_DIM:(h + 1) * B_KEY_DIM] if g == h else z for g in range(B_HEADS)], axis=1)
                for h in range(B_HEADS)], axis=0)

        def diagonal_of(full):
            return jnp.concatenate([full[h * B_KEY_DIM:(h + 1) * B_KEY_DIM, h * B_VAL_DIM:(h + 1) * B_VAL_DIM]
                                    for h in range(B_HEADS)], axis=0)

        dqm, dkm, dv_cols = None, None, []
        for h in range(B_HEADS):
            vl_ = slice(h * B_VAL_DIM, (h + 1) * B_VAL_DIM)
            mine = lane_head == h
            qz, kz = jnp.where(mine, qm, 0.0).astype(_MX), jnp.where(mine, km, 0.0).astype(_MX)
            a = jnp.where(low, _dot_nt(qz, kz), 0.0).astype(_MX)
            da = jnp.where(low, _dot_nt(d_o_mx[:, vl_], vb_ref[:, vl_]), 0.0).astype(_MX)
            dqm_h, dkm_h = _dot(da, kz), _dot_tn(da, qz)
            dqm = dqm_h if dqm is None else dqm + dqm_h
            dkm = dkm_h if dkm is None else dkm + dkm_h
            dv_cols.append(_dot_tn(a, d_o_mx[:, vl_]))
        dv = jnp.concatenate(dv_cols, axis=1)

        chunk = [slice(c * B_CHUNK, (c + 1) * B_CHUNK) for c in range(cpb)]
        dqc_rows, g_loc = [], []
        for c in range(cpb):
            dqc_rows.append(_dot_nt(d_o_mx[chunk[c]], on_diagonal(sst_ref[c].astype(_MX))))
            g_loc.append(diagonal_of(_dot_tn(qc[chunk[c]], d_o_mx[chunk[c]])))
        cur = ds_ref[...]
        d_state = [None] * cpb
        for c in reversed(range(cpb)):
            d_state[c] = cur
            cur = g_loc[c] + jnp.exp(jnp.sum(la[chunk[c]].T, axis=1, keepdims=True)) * cur
        ds_ref[...] = cur
        dkl_rows, dv_rows, dlast_rows = [], [], []
        ones8 = jnp.ones((8, B_VAL_DIM), _F32)
        for c in range(cpb):
            dsd = on_diagonal(d_state[c].astype(_MX))
            dkl_c = _dot_nt(vb_ref[chunk[c], :], dsd)
            dkl_rows.append(dkl_c)
            dv_rows.append(_dot(kl[chunk[c]], dsd))
            prod = d_state[c] * sst_ref[c]
            p0 = prod.astype(jnp.bfloat16)
            p1 = (prod - p0.astype(_F32)).astype(jnp.bfloat16)
            p2 = (prod - p0.astype(_F32) - p1.astype(_F32)).astype(jnp.bfloat16)
            ddec = (_dot_nt(ones8, p0) + _dot_nt(ones8, p1) + _dot_nt(ones8, p2))[0:1]
            r_last = c * B_CHUNK + B_CHUNK - 1
            dlast = jnp.sum(dkl_c * kl[chunk[c]], axis=0, keepdims=True) + ddec * jnp.exp(last[r_last:r_last + 1])
            dlast_rows.append(jnp.broadcast_to(dlast, (B_CHUNK, B_KEY_WIDTH)))
        dqc, dkl = jnp.concatenate(dqc_rows, axis=0), jnp.concatenate(dkl_rows, axis=0)
        dqkb_ref[:, 0:B_KEY_WIDTH] = ((dqm * e_q + dqc * e_c) * (B_KEY_DIM ** -0.5)).astype(_ST)
        dqkb_ref[:, B_KEY_WIDTH:2 * B_KEY_WIDTH] = (dkm * e_k + dkl * e_l).astype(_ST)
        dvb_ref[...] = (dv + jnp.concatenate(dv_rows, axis=0)).astype(_ST)
        dcum = dqm * qm - dkm * km + dqc * qc - dkl * kl
        row = lax.broadcasted_iota(jnp.int32, (tb, B_KEY_WIDTH), 0)
        dcum = jnp.where(row % B_CHUNK == B_CHUNK - 1, dcum + jnp.concatenate(dlast_rows, axis=0), dcum)
        dla = _dot_ones((same & (ri <= ci)).astype(_F32), dcum)

        da_pre = dla * (1.0 / B_GATE_TEMP) * (1.0 - _sigmoid(a_pre))
        dalr_ref[...] = _dot_nt(da_pre, wup_ref[...]).astype(_ST)
        dwup_ref[...] = dwup_ref[...] + _dot_tn(alr_ref[...], da_pre)
        db_ref[...] = db_ref[...] + jnp.sum(da_pre, axis=0, keepdims=True)

    blk = lambda s, i: s * nblk + nblk - 1 - i
    rows = lambda w: pl.BlockSpec((tb, w), lambda s, i: (blk(s, i), 0))
    full = lambda a, b: pl.BlockSpec((a, b), lambda s, i: (0, 0))
    act = lambda w: jax.ShapeDtypeStruct((t, w), _ST)
    return pl.pallas_call(
        body, name="gla_bwd", grid=(nseq, nblk),
        in_specs=[rows(512), rows(512), rows(512), rows(LANE), rows(512), rows(512),
                  pl.BlockSpec((cpb, B_KEY_WIDTH, B_VAL_DIM), lambda s, i: (blk(s, i), 0, 0)),
                  full(LANE, B_KEY_WIDTH), full(1, B_KEY_WIDTH), full(1, B_WIDTH)],
        out_specs=[rows(512), rows(512), rows(512), rows(LANE), full(LANE, B_KEY_WIDTH),
                   full(1, B_KEY_WIDTH), full(1, B_WIDTH)],
        out_shape=[act(512), act(512), act(512), act(LANE),
                   jax.ShapeDtypeStruct((LANE, B_KEY_WIDTH), _F32),
                   jax.ShapeDtypeStruct((1, B_KEY_WIDTH), _F32), jax.ShapeDtypeStruct((1, B_WIDTH), _F32)],
        scratch_shapes=[pltpu.VMEM((B_KEY_WIDTH, B_VAL_DIM), _F32)],
        compiler_params=_cparams(("arbitrary", "arbitrary")),
    )(qkb, vb, zb, alr, oraw, dob, sst, wup, b_alpha, gn)


def _merge_loss(oa, ob, ga, gb, x2, tgt, wa, wb, wo, g_final):
    t = x2.shape[0]
    tm = min(t, 512)
    nt = t // tm

    def body(oa_ref, ob_ref, ga_ref, gb_ref, x_ref, t_ref, wa_ref, wb_ref, wo_ref, gf_ref,
             dh_ref, doa_ref, dob_ref, dga_ref, dgb_ref, dwa_ref, dwb_ref, dwo_ref, dgf_ref, loss_ref,
             ya_s, yb_s, out_s, dmer_s, mrg_s, dya_s, dyb_s):
        first = pl.program_id(0) == 0
        so_far = lambda ref: jnp.where(first, 0.0, ref[...])

        slabs = [slice(s, s + MERGE_SLAB) for s in range(0, tm, MERGE_SLAB)]
        fold = lambda a: a[0:8] + a[8:16]
        ya_s[...] = _dot(oa_ref[...], wa_ref[...])
        yb_s[...] = _dot(ob_ref[...], wb_ref[...])
        for rows_ in slabs:
            sga, sgb = _sigmoid(ga_ref[rows_, :].astype(_F32)), _sigmoid(gb_ref[rows_, :].astype(_F32))
            mrg_s[rows_, :] = (sga * ya_s[rows_, :] + sgb * yb_s[rows_, :]).astype(_MX)
        out_s[...] = x_ref[...] + _dot(mrg_s[...], wo_ref[...])
        gf = gf_ref[...]
        loss8 = jnp.zeros((8, D_MODEL), _F32)
        dgf8 = jnp.zeros((8, D_MODEL), _F32)
        for rows_ in slabs:
            out = out_s[rows_, :]
            r = lax.rsqrt(jnp.mean(out * out, axis=-1, keepdims=True) + NORM_EPS)
            nrm = out * r
            err = nrm * gf - t_ref[rows_, :]
            loss8 = loss8 + fold(err * err)
            dy = err * (1.0 / D_MODEL)
            dgf8 = dgf8 + fold(dy * nrm)
            dh = _rms_bwd(dy * gf, nrm, r)
            dh_ref[rows_, :] = dh.astype(_ST)
        loss_ref[...] = so_far(loss_ref) + (0.5 / D_MODEL) * jnp.sum(loss8, axis=0, keepdims=True)
        dgf_ref[...] = so_far(dgf_ref) + jnp.sum(dgf8, axis=0, keepdims=True)
        dmer_s[...] = _dot_nt(dh_ref[...], wo_ref[...])
        dwo_ref[...] = so_far(dwo_ref) + _dot_tn(mrg_s[...], dh_ref[...])
        for rows_ in slabs:
            sga, sgb = _sigmoid(ga_ref[rows_, :].astype(_F32)), _sigmoid(gb_ref[rows_, :].astype(_F32))
            dmer = dmer_s[rows_, :]
            da, db = dmer * sga, dmer * sgb
            dya_s[rows_, :] = da.astype(_MX)
            dyb_s[rows_, :] = db.astype(_MX)
            dga_ref[rows_, :] = (da * ya_s[rows_, :] * (1.0 - sga)).astype(_ST)
            dgb_ref[rows_, :] = (db * yb_s[rows_, :] * (1.0 - sgb)).astype(_ST)
        doa_ref[...] = _dot_nt(dya_s[...], wa_ref[...]).astype(_ST)
        dob_ref[...] = _dot_nt(dyb_s[...], wb_ref[...]).astype(_ST)
        dwa_ref[...] = so_far(dwa_ref) + _dot_tn(oa_ref[...], dya_s[...])
        dwb_ref[...] = so_far(dwb_ref) + _dot_tn(ob_ref[...], dyb_s[...])

    rows = lambda w: pl.BlockSpec((tm, w), lambda i: (i, 0))
    full = lambda a, b: pl.BlockSpec((a, b), lambda i: (0, 0), pipeline_mode=pl.Buffered(1))
    return pl.pallas_call(
        body, name="merge_loss", grid=(nt,),
        in_specs=[rows(512), rows(512), rows(D_MODEL), rows(D_MODEL), rows(D_MODEL), rows(D_MODEL),
                  full(A_WIDTH, D_MODEL), full(B_WIDTH, D_MODEL), full(D_MODEL, D_MODEL), full(1, D_MODEL)],
        out_specs=[rows(D_MODEL), rows(512), rows(512), rows(D_MODEL), rows(D_MODEL),
                   full(A_WIDTH, D_MODEL), full(B_WIDTH, D_MODEL), full(D_MODEL, D_MODEL),
                   full(1, D_MODEL), full(1, D_MODEL)],
        out_shape=[jax.ShapeDtypeStruct((t, D_MODEL), _ST), jax.ShapeDtypeStruct((t, 512), _ST),
                   jax.ShapeDtypeStruct((t, 512), _ST), jax.ShapeDtypeStruct((t, D_MODEL), _ST),
                   jax.ShapeDtypeStruct((t, D_MODEL), _ST),
                   jax.ShapeDtypeStruct((A_WIDTH, D_MODEL), _F32), jax.ShapeDtypeStruct((B_WIDTH, D_MODEL), _F32),
                   jax.ShapeDtypeStruct((D_MODEL, D_MODEL), _F32), jax.ShapeDtypeStruct((1, D_MODEL), _F32),
                   jax.ShapeDtypeStruct((1, D_MODEL), _F32)],
        scratch_shapes=[pltpu.VMEM((tm, D_MODEL), _F32)] * 4 + [pltpu.VMEM((tm, D_MODEL), _MX)] * 3,
        compiler_params=_cparams(("arbitrary",), VMEM_LIMIT),
    )(oa, ob, ga, gb, x2, tgt, wa, wb, wo, g_final)


def _in_proj_bwd_x(dpieces, wt, x2, dh2, g_in):
    t = x2.shape[0]
    tm = min(t, 512)
    np_ = len(PIECES)

    def body(*refs):
        dp_refs = refs[:np_]
        w_ref, x_ref, dh2_ref, g_ref, gx_ref, dg_ref = refs[np_:]

        @pl.when(pl.program_id(0) == 0)
        def _():
            dg_ref[...] = jnp.zeros_like(dg_ref)

        dh = None
        for (name, a, b), dp in zip(PIECES, dp_refs):
            part = _dot(dp[...], w_ref[a:b, :])
            dh = part if dh is None else dh + part
        xv = x_ref[...]
        r = lax.rsqrt(jnp.mean(xv * xv, axis=-1, keepdims=True) + NORM_EPS)
        nrm = xv * r
        dg_ref[...] = dg_ref[...] + jnp.sum(dh * nrm, axis=0, keepdims=True)
        gx_ref[...] = dh2_ref[...].astype(_F32) + _rms_bwd(dh * g_ref[...], nrm, r)

    rows = lambda w: pl.BlockSpec((tm, w), lambda i: (i, 0))
    full = lambda a, b: pl.BlockSpec((a, b), lambda i: (0, 0), pipeline_mode=pl.Buffered(1))
    return pl.pallas_call(
        body, name="in_proj_bwd_x", grid=(t // tm,),
        in_specs=[rows(b - a) for _, a, b in PIECES] + [full(D_IN, D_MODEL), rows(D_MODEL), rows(D_MODEL),
                                                          full(1, D_MODEL)],
        out_specs=[rows(D_MODEL), full(1, D_MODEL)],
        out_shape=[jax.ShapeDtypeStruct((t, D_MODEL), _F32), jax.ShapeDtypeStruct((1, D_MODEL), _F32)],
        compiler_params=_cparams(("arbitrary",), VMEM_LIMIT),
    )(*dpieces, wt, x2, dh2, g_in)


def _in_proj_bwd_w(h, dpieces):
    t = h.shape[0]
    tm = min(t, 1024)
    nt = t // tm
    np_ = len(PIECES)

    def body(*refs):
        h_ref, dp_refs, out_ref = refs[0], refs[1:1 + np_], refs[1 + np_]
        acc_ref, sem = refs[2 + np_:]
        i = pl.program_id(0)
        hv = h_ref[...]
        writes = []
        for j, ((name, a, b), dp) in enumerate(zip(PIECES, dp_refs)):
            part = _dot_tn(dp[...], hv)
            if name == "alr":
                b = a + B_GATE_RANK
                part = part[0:B_GATE_RANK]
            acc_ref[a:b, :] = jnp.where(i == 0, 0.0, acc_ref[a:b, :]) + part
            writes.append(pltpu.make_async_copy(acc_ref.at[a:b], out_ref.at[a:b], sem.at[j]))

            @pl.when(i == nt - 1)
            def _(cp=writes[-1]):
                cp.start()

        @pl.when(i == nt - 1)
        def _():
            for cp in writes:
                cp.wait()

    rows = lambda w: pl.BlockSpec((tm, w), lambda i: (i, 0))
    return pl.pallas_call(
        body, name="in_proj_bwd_w", grid=(nt,),
        in_specs=[rows(D_MODEL)] + [rows(b - a) for _, a, b in PIECES],
        out_specs=_ANY, out_shape=jax.ShapeDtypeStruct((D_IN, D_MODEL), _F32),
        scratch_shapes=[pltpu.VMEM((D_IN, D_MODEL), _F32), pltpu.SemaphoreType.DMA((np_,))],
        compiler_params=_cparams(("arbitrary",), VMEM_LIMIT),
    )(h, *dpieces)


def _place():
    return lax.axis_index("x"), lax.axis_index("y"), lax.axis_index("c")


def _other_chips(x, y):
    return [(1 - x, y), (x, 1 - y), (1 - x, 1 - y)]


class _Split(NamedTuple):
    by_rows: bool
    step: int
    size: int

    def half(self, ref, c):
        r, n = ref.shape[-2:]
        if self.by_rows:
            return ref.at[:, pl.ds(pl.multiple_of(c * (n // 2), LANE), n // 2)]
        return ref.at[pl.ds(pl.multiple_of(c * (r // 2), 16), r // 2), :]

    def chip_part(self, ref, k):
        if self.by_rows:
            return ref.at[pl.ds(pl.multiple_of(k * self.step, 16), self.size), :]
        return ref.at[:, pl.ds(pl.multiple_of(k * self.size, LANE), self.size)]

    def half_shape(self, shape):
        r, n = shape
        return (r, n // 2) if self.by_rows else (r // 2, n)

    def part_shape(self, shape):
        r, n = shape
        return (self.size, n) if self.by_rows else (r, self.size)


SPLIT_W_IN_T = _Split(True, WINDOW_STEP, WINDOW_ROWS)
SPLIT_W_O = _Split(True, 256, 256)
SPLIT_W_OUT = _Split(False, 256, 256)


def _gather_weights(shards, splits, fulls, pos_f):
    nw = len(shards)
    t = pos_f.shape[0]

    def body(*refs):
        ins, (pos_ref, c_ref) = refs[:nw], refs[nw:nw + 2]
        outs, tables = refs[nw + 2:2 * nw + 2], refs[2 * nw + 2:2 * nw + 5]
        send_a, recv_a, send_b, recv_b = refs[2 * nw + 5:]
        x, y, c = _place()
        me = 2 * x + y
        peers = _other_chips(x, y)

        def place(i, k, half):
            if splits[i] is None:
                return outs[i].at[k]
            if fulls[i][0] == 4 and len(fulls[i]) == 3:
                whole = outs[i].at[k]
            else:
                whole = splits[i].chip_part(outs[i], k)
            return splits[i].half(whole, half)

        first, passed = [], []
        for i in range(nw):
            src = ins[i] if splits[i] is None else splits[i].half(ins[i], c)
            for j, (px, py) in enumerate(peers):
                cp = pltpu.make_async_remote_copy(
                    src_ref=src, dst_ref=place(i, me, c), send_sem=send_a.at[3 * i + j],
                    recv_sem=recv_a.at[3 * i + j], device_id=(px, py, c), device_id_type=_MESH)
                cp.start()
                first.append(cp)
        _rope_tables_into(pos_ref, c_ref, *tables)
        for i in range(nw):
            for j, (px, py) in enumerate(peers):
                landed = place(i, 2 * px + py, c)
                pltpu.make_async_remote_copy(
                    src_ref=landed, dst_ref=landed, send_sem=send_a.at[3 * i + j], recv_sem=recv_a.at[3 * i + j],
                    device_id=(px, py, c), device_id_type=_MESH).wait_recv()
                if splits[i] is not None:
                    cp = pltpu.make_async_remote_copy(
                        src_ref=landed, dst_ref=landed, send_sem=send_b.at[3 * i + j], recv_sem=recv_b.at[3 * i + j],
                        device_id=(x, y, 1 - c), device_id_type=_MESH)
                    cp.start()
                    passed.append(cp)
        for i in range(nw):
            if splits[i] is None:
                continue
            for j, (px, py) in enumerate(peers):
                theirs = place(i, 2 * px + py, 1 - c)
                pltpu.make_async_remote_copy(
                    src_ref=theirs, dst_ref=theirs, send_sem=send_b.at[3 * i + j], recv_sem=recv_b.at[3 * i + j],
                    device_id=(x, y, 1 - c), device_id_type=_MESH).wait_recv()
        for cp in first + passed:
            cp.wait_send()

    vm = pl.BlockSpec(memory_space=pltpu.VMEM)
    tab = jax.ShapeDtypeStruct((t, LANE), _F32)
    return pl.pallas_call(
        body, name="gather_weights",
        in_specs=[_ANY] * nw + [vm, vm], out_specs=[_ANY] * nw + [vm] * 3,
        out_shape=[jax.ShapeDtypeStruct(f, s.dtype) for f, s in zip(fulls, shards)] + [tab] * 3,
        scratch_shapes=[pltpu.SemaphoreType.DMA((3 * nw,)) for _ in range(4)],
        compiler_params=_cparams(None, VMEM_LIMIT),
    )(*shards, pos_f, _rope_consts())


def _assemble_w_in_t(slots):
    bw = 256
    ov = WINDOW_ROWS - WINDOW_STEP

    def body(s_ref, o_ref):
        for k in range(4):
            base = k * WINDOW_STEP
            lo = 0 if k == 0 else ov
            if k > 0:
                o_ref[base:base + ov, :] = s_ref[k - 1, WINDOW_STEP:WINDOW_ROWS, :] + s_ref[k, 0:ov, :]
            hi = WINDOW_ROWS if k == 3 else WINDOW_STEP
            o_ref[base + lo:base + hi, :] = s_ref[k, lo:hi, :]

    return pl.pallas_call(
        body, name="assemble_w_in_t", grid=(D_MODEL // bw,),
        in_specs=[pl.BlockSpec((4, WINDOW_ROWS, bw), lambda i: (0, 0, i))],
        out_specs=pl.BlockSpec((D_IN, bw), lambda i: (0, i)),
        out_shape=jax.ShapeDtypeStruct((D_IN, D_MODEL), slots.dtype),
        compiler_params=_cparams(("parallel",)),
    )(slots)


def _pair_exchange(grads, splits):
    nw = len(grads)

    def body(*refs):
        ins, outs = refs[:nw], refs[nw:2 * nw]
        send, recv = refs[2 * nw:]
        x, y, c = _place()
        copies = []
        for i in range(nw):
            cp = pltpu.make_async_remote_copy(
                src_ref=splits[i].half(ins[i], 1 - c), dst_ref=outs[i], send_sem=send.at[i], recv_sem=recv.at[i],
                device_id=(x, y, 1 - c), device_id_type=_MESH)
            cp.start()
            copies.append(cp)
        for cp in copies:
            cp.wait()

    return pl.pallas_call(
        body, name="grad_pair_exchange",
        in_specs=[_ANY] * nw, out_specs=[_ANY] * nw,
        out_shape=[jax.ShapeDtypeStruct(sp.half_shape(g.shape), g.dtype) for g, sp in zip(grads, splits)],
        scratch_shapes=[pltpu.SemaphoreType.DMA((nw,)), pltpu.SemaphoreType.DMA((nw,))],
    )(*grads)


def _row_block(rows):
    for cand in (976, 176, 256, 128):
        if rows % cand == 0:
            return cand
    return rows


def _pair_sum(g, r, split, c_arr, name):
    hr, hn = r.shape
    br = _row_block(hr)
    if split.by_rows:
        g_spec = pl.BlockSpec((br, hn), lambda i, c_ref: (i, c_ref[0]))
    else:
        g_spec = pl.BlockSpec((br, hn), lambda i, c_ref: (c_ref[0] * (hr // br) + i, 0))

    def body(c_ref, g_ref, r_ref, o_ref):
        o_ref[...] = (g_ref[...] + r_ref[...]).astype(o_ref.dtype)

    return pl.pallas_call(
        body, name=name,
        grid_spec=pltpu.PrefetchScalarGridSpec(
            num_scalar_prefetch=1, grid=(hr // br,),
            in_specs=[g_spec, pl.BlockSpec((br, hn), lambda i, c_ref: (i, 0))],
            out_specs=pl.BlockSpec((br, hn), lambda i, c_ref: (i, 0))),
        out_shape=jax.ShapeDtypeStruct(r.shape, _MX),
        compiler_params=_cparams(("parallel",)),
    )(c_arr, g, r)


def _chip_exchange(parts, splits, small):
    nw = len(parts)

    def body(*refs):
        ins, small_ref, outs, all_ref = refs[:nw], refs[nw], refs[nw + 1:2 * nw + 1], refs[2 * nw + 1]
        send, recv, lsem, s_send, s_recv = refs[2 * nw + 2:]
        x, y, c = _place()
        me = 2 * x + y
        dev = 2 * me + c
        copies = [pltpu.make_async_copy(small_ref, all_ref.at[dev], lsem.at[nw])]
        copies[0].start()
        for r in range(1, 8):
            peer = (1 - x if r & 4 else x, 1 - y if r & 2 else y, 1 - c if r & 1 else c)
            cp = pltpu.make_async_remote_copy(
                src_ref=small_ref, dst_ref=all_ref.at[dev], send_sem=s_send.at[r - 1], recv_sem=s_recv.at[r - 1],
                device_id=peer, device_id_type=_MESH)
            cp.start()
            copies.append(cp)
        for i in range(nw):
            mine = pltpu.make_async_copy(splits[i].chip_part(ins[i], me), outs[i].at[me], lsem.at[i])
            mine.start()
            copies.append(mine)
            for j, (px, py) in enumerate(_other_chips(x, y)):
                cp = pltpu.make_async_remote_copy(
                    src_ref=splits[i].chip_part(ins[i], 2 * px + py), dst_ref=outs[i].at[me],
                    send_sem=send.at[3 * i + j], recv_sem=recv.at[3 * i + j],
                    device_id=(px, py, c), device_id_type=_MESH)
                cp.start()
                copies.append(cp)
        for cp in copies:
            cp.wait()

    return pl.pallas_call(
        body, name="grad_chip_exchange",
        in_specs=[_ANY] * (nw + 1), out_specs=[_ANY] * (nw + 1),
        out_shape=[jax.ShapeDtypeStruct((4,) + sp.part_shape(p.shape), p.dtype) for p, sp in zip(parts, splits)]
        + [jax.ShapeDtypeStruct((8,) + small.shape, small.dtype)],
        scratch_shapes=[pltpu.SemaphoreType.DMA((3 * nw,)), pltpu.SemaphoreType.DMA((3 * nw,)),
                        pltpu.SemaphoreType.DMA((nw + 1,)), pltpu.SemaphoreType.DMA((7,)),
                        pltpu.SemaphoreType.DMA((7,))],
    )(*parts, small)


def _sum_chips(q, split, c_arr, name):
    _, hr, hn = q.shape
    br = _row_block(hr)
    if split.by_rows:
        out_shape = (hr, 2 * hn)
        o_spec = pl.BlockSpec((br, hn), lambda i, c_ref: (i, c_ref[0]))
    else:
        out_shape = (2 * hr, hn)
        o_spec = pl.BlockSpec((br, hn), lambda i, c_ref: (c_ref[0] * (hr // br) + i, 0))

    def body(c_ref, q_ref, o_ref):
        f = lambda k: q_ref[k].astype(_F32)
        o_ref[...] = ((f(0) + f(1)) + f(2)) + f(3)

    return pl.pallas_call(
        body, name=name,
        grid_spec=pltpu.PrefetchScalarGridSpec(
            num_scalar_prefetch=1, grid=(hr // br,),
            in_specs=[pl.BlockSpec((4, br, hn), lambda i, c_ref: (0, i, 0))], out_specs=o_spec),
        out_shape=jax.ShapeDtypeStruct(out_shape, _F32),
        compiler_params=_cparams(("parallel",)),
    )(c_arr, q)


def _pair_share(bufs, splits):
    nw = len(bufs)

    def body(*refs):
        ins, outs = refs[:nw], refs[nw:2 * nw]
        send, recv = refs[2 * nw:]
        x, y, c = _place()
        copies = []
        for i in range(nw):
            cp = pltpu.make_async_remote_copy(
                src_ref=splits[i].half(ins[i], c), dst_ref=splits[i].half(outs[i], c), send_sem=send.at[i],
                recv_sem=recv.at[i], device_id=(x, y, 1 - c), device_id_type=_MESH)
            cp.start()
            copies.append(cp)
        for cp in copies:
            cp.wait()

    return pl.pallas_call(
        body, name="grad_pair_share",
        in_specs=[_ANY] * nw, out_specs=[_ANY] * nw,
        out_shape=[jax.ShapeDtypeStruct(b.shape, b.dtype) for b in bufs],
        input_output_aliases={i: i for i in range(nw)},
        scratch_shapes=[pltpu.SemaphoreType.DMA((nw,)), pltpu.SemaphoreType.DMA((nw,))],
    )(*bufs)


def _sum_devices(parts):
    def body(p_ref, tot_ref):
        acc = p_ref[0]
        for d in range(1, 8):
            acc = acc + p_ref[d]
        tot_ref[...] = acc

    vm = pl.BlockSpec(memory_space=pltpu.VMEM)
    return pl.pallas_call(
        body, name="small_sum", in_specs=[vm], out_specs=vm,
        out_shape=jax.ShapeDtypeStruct(parts.shape[1:], parts.dtype),
    )(parts)


def _adamw(w, g, m, v, name):
    lead = w.shape[0] != 1
    r, n = (w.shape[0], w.shape[2]) if lead else w.shape[1:]
    br = r
    for cand in (256, 244, 128):
        if r > cand and r % cand == 0:
            br = cand
            break

    def body(w_ref, g_ref, m_ref, v_ref, d_ref, nm_ref, nv_ref):
        gv = g_ref[...]
        m2 = ADAM_B1 * m_ref[...] + (1.0 - ADAM_B1) * gv
        v2 = ADAM_B2 * v_ref[...] + (1.0 - ADAM_B2) * (gv * gv)
        m_hat = m2 / (1.0 - ADAM_B1 ** ADAM_STEP)
        v_hat = v2 / (1.0 - ADAM_B2 ** ADAM_STEP)
        d_ref[...] = -ADAM_LR * (m_hat / (jnp.sqrt(v_hat) + ADAM_EPS) + ADAM_WD * w_ref[...])
        nm_ref[...] = m2
        nv_ref[...] = v2

    blk = pl.BlockSpec((br, 1, n), lambda i: (i, 0, 0)) if lead else pl.BlockSpec((None, br, n), lambda i: (0, i, 0))
    shp = jax.ShapeDtypeStruct(w.shape, _F32)
    return pl.pallas_call(
        body, name=name, grid=(r // br,),
        in_specs=[blk] * 4, out_specs=[blk] * 3, out_shape=[shp] * 3,
        compiler_params=_cparams(("parallel",)),
    )(w, g, m, v)


def kernel(x, positions, g_in, w_in, w_alpha_up, b_alpha, attn_sinks, g_gla_norm, w_out_a, w_out_b, w_o, g_final, loss_target, m_g_in, m_w_in, m_w_alpha_up, m_b_alpha, m_attn_sinks, m_g_gla_norm, m_w_out_a, m_w_out_b, m_w_o, m_g_final, v_g_in, v_w_in, v_w_alpha_up, v_b_alpha, v_attn_sinks, v_g_gla_norm, v_w_out_a, v_w_out_b, v_w_o, v_g_final):
    nseq, seq, _ = x.shape
    t = nseq * seq
    cx, cy, cc = _place()
    chip = 2 * cx + cy
    c_arr = jnp.reshape(cc, (1,)).astype(jnp.int32)

    tr = lambda w: jnp.transpose(w, (2, 0, 1))
    w_in_t = tr(w_in).reshape(SHARD, D_MODEL).astype(_MX)
    pad = WINDOW_ROWS - SHARD
    window = lax.switch(chip, [lambda w, k=k: jnp.pad(w, ((4 * k, pad - 4 * k), (0, 0))) for k in range(4)], w_in_t)
    shards = [window, w_out_a[0].astype(_MX), w_out_b[0].astype(_MX), w_o[0].astype(_MX), w_alpha_up[0].astype(_MX)]
    splits = [SPLIT_W_IN_T, SPLIT_W_OUT, SPLIT_W_OUT, SPLIT_W_O, None]
    fulls = [(4, WINDOW_ROWS, D_MODEL), (A_WIDTH, D_MODEL), (B_WIDTH, D_MODEL), (D_MODEL, D_MODEL),
             (4, B_GATE_RANK, B_KEY_WIDTH // 4)]
    pos_f = positions.astype(_F32).reshape(t, 1)
    win_g, wa, wb, wo, wup_g, cos, sa, sb = _gather_weights(shards, splits, fulls, pos_f)
    nsh = D_MODEL // 4
    win_g = lax.dynamic_update_slice(win_g, window[None], (chip, 0, 0))
    wa = lax.dynamic_update_slice(wa, shards[1], (0, nsh * chip))
    wb = lax.dynamic_update_slice(wb, shards[2], (0, nsh * chip))
    wo = lax.dynamic_update_slice(wo, shards[3], (nsh * chip, 0))
    wup_g = lax.dynamic_update_slice(wup_g, shards[4][None], (chip, 0, 0))
    wt = _assemble_w_in_t(win_g)
    wup = jnp.concatenate([jnp.transpose(wup_g, (1, 0, 2)).reshape(B_GATE_RANK, B_KEY_WIDTH),
                           jnp.zeros((LANE - B_GATE_RANK, B_KEY_WIDTH), _MX)], axis=0)

    x2 = x.reshape(t, D_MODEL)
    tgt = loss_target.reshape(t, D_MODEL)
    sinks = attn_sinks.reshape(A_HEADS)
    gf = g_final.reshape(1, D_MODEL)

    h, qkv, za, qkb, vb, zb, alr, ga, gb = _in_proj(x2, g_in, wt, cos, sa, sb)
    oa = _attn_fwd(qkv, za, sinks, nseq)
    ob, oraw, sst = _gla_fwd(qkb, vb, zb, alr, wup, b_alpha, g_gla_norm, nseq)

    dh2, doa, dob, dga, dgb, dwa, dwb, dwo, dgf, lossv = _merge_loss(oa, ob, ga, gb, x2, tgt, wa, wb, wo, gf)

    dqkv, dza, dsink = _attn_bwd(qkv, za, doa, sinks, cos, sa, sb, nseq)
    dqkb, dvb, dzb, dalr, dwup, dba, dgn = _gla_bwd(qkb, vb, zb, alr, oraw, dob, sst, wup, b_alpha, g_gla_norm, nseq)
    dpieces = [dqkv, dza, dqkb, dvb, dzb, dalr, dga, dgb]
    grad_x2, dgin = _in_proj_bwd_x(dpieces, wt, x2, dh2, g_in)
    dwin_t = _in_proj_bwd_w(h, dpieces)

    grads = [dwin_t, dwa, dwb, dwo]
    gsplits = [SPLIT_W_IN_T, SPLIT_W_OUT, SPLIT_W_OUT, SPLIT_W_O]
    names = ("w_in", "w_out_a", "w_out_b", "w_o")
    from_sibling = _pair_exchange(grads, gsplits)
    pair_sums = [_pair_sum(g, r, sp, c_arr, "pair_sum_" + nm)
                 for g, r, sp, nm in zip(grads, from_sibling, gsplits, names)]
    small = jnp.concatenate([
        dgin, dgf, dgn, dba,
        jnp.pad(dsink[:, 0].reshape(1, A_HEADS), ((0, 0), (0, LANE - A_HEADS))),
        jnp.pad(jnp.sum(lossv, axis=1, keepdims=True), ((0, 0), (0, LANE - 1))),
        dwup[:B_GATE_RANK].reshape(1, B_GATE_RANK * B_KEY_WIDTH)], axis=1)
    *from_chips, small_parts = _chip_exchange(pair_sums, gsplits, jnp.pad(small, ((0, 7), (0, 0))))
    reduced = [_sum_chips(q, sp, c_arr, "chip_sum_" + nm) for q, sp, nm in zip(from_chips, gsplits, names)]
    g_window, g_wa, g_wb, g_wo = _pair_share(reduced, gsplits)
    g_win_t = lax.switch(chip, [lambda w, k=k: w[4 * k:4 * k + SHARD].reshape(SHARD, 1, D_MODEL) for k in range(4)],
                         g_window)
    tot = _sum_devices(small_parts)[0:1]
    o = 0
    def take(n):
        nonlocal o
        o += n
        return tot[:, o - n:o]
    g_gin, g_gf, g_gn, g_ba = take(D_MODEL), take(D_MODEL), take(B_WIDTH), take(B_KEY_WIDTH)
    g_sink = take(LANE)[:, :A_HEADS]
    loss = take(LANE)[0, 0]
    g_wup_full = take(B_GATE_RANK * B_KEY_WIDTH).reshape(B_GATE_RANK, B_KEY_WIDTH)
    nup = B_KEY_WIDTH // 4
    g_wup = lax.dynamic_slice(g_wup_full, (0, chip * nup), (B_GATE_RANK, nup))

    def pack(*parts):
        return jnp.concatenate([p.reshape(1, -1) for p in parts], axis=1)

    sm_w = pack(g_in, g_final, g_gla_norm, b_alpha, attn_sinks, w_alpha_up)
    sm_g = pack(g_gin, g_gf, g_gn, g_ba, g_sink, g_wup)
    sm_m = pack(m_g_in, m_g_final, m_g_gla_norm, m_b_alpha, m_attn_sinks, m_w_alpha_up)
    sm_v = pack(v_g_in, v_g_final, v_g_gla_norm, v_b_alpha, v_attn_sinks, v_w_alpha_up)
    sm_out = [p[0] for p in _adamw(sm_w[None], sm_g[None], sm_m[None], sm_v[None], "adamw_small")]

    def unpack(p):
        sizes = (D_MODEL, D_MODEL, B_WIDTH, B_KEY_WIDTH, A_HEADS, B_GATE_RANK * nup)
        outs, at = [], 0
        for s in sizes:
            outs.append(p[:, at:at + s])
            at += s
        gi, gfin, gnn, ba, sk, wu = outs
        return dict(g_in=gi, g_final=gfin.reshape(D_MODEL), g_gla_norm=gnn, b_alpha=ba, attn_sinks=sk,
                    w_alpha_up=wu.reshape(1, B_GATE_RANK, nup))

    untr = lambda a: jnp.transpose(a, (1, 2, 0))
    big = dict(w_in=tuple(untr(a) for a in (g_win_t,) + tuple(_adamw(tr(w_in), g_win_t, tr(m_w_in), tr(v_w_in), "adamw_w_in"))))
    for nm, w, g, m, v in (("w_out_a", w_out_a, g_wa, m_w_out_a, v_w_out_a),
                           ("w_out_b", w_out_b, g_wb, m_w_out_b, v_w_out_b), ("w_o", w_o, g_wo, m_w_o, v_w_o)):
        big[nm] = (g[None],) + tuple(_adamw(w, g[None], m, v, "adamw_" + nm))

    order = ("g_in", "w_in", "w_alpha_up", "b_alpha", "attn_sinks", "g_gla_norm", "w_out_a", "w_out_b", "w_o", "g_final")
    small_sets = [unpack(sm_g)] + [unpack(p) for p in sm_out]
    outs = []
    for kind in range(4):
        for nm in order:
            outs.append(big[nm][kind] if nm in big else small_sets[kind][nm])
    return (loss, grad_x2.reshape(x.shape), *outs)
```

```python
import math
from typing import NamedTuple

import numpy as np
import jax
import jax.numpy as jnp
from jax import lax
from jax.experimental import pallas as pl
from jax.experimental.pallas import tpu as pltpu

D_MODEL = 1024
A_HEADS, A_KV_HEADS, A_HEAD_DIM = 8, 2, 64
A_GROUP = A_HEADS // A_KV_HEADS
A_WIDTH, A_KV_WIDTH = 512, 128
BLOCK = 128
ROPE_THETA = 500000.0
ROPE_DIM = 16
B_HEADS, B_KEY_DIM, B_VAL_DIM = 4, 64, 128
B_KEY_WIDTH, B_WIDTH = 256, 512
B_GATE_RANK = 16
B_GATE_TEMP = 16.0
B_CHUNK = 64
NORM_EPS = 1e-6
NEG_BIG = -1e30
D_IN = 4880

ADAM_LR, ADAM_B1, ADAM_B2, ADAM_EPS, ADAM_WD, ADAM_STEP = 0.001, 0.9, 0.999, 1e-08, 0.01, 10

LANE = 128
ALR_AT = 2816
PIECES = (("qkv", 0, 768), ("za", 768, 1280), ("qkb", 1280, 1792), ("vb", 1792, 2304),
          ("zb", 2304, 2816), ("alr", ALR_AT, ALR_AT + LANE), ("ga", 2832, 3856), ("gb", 3856, 4880))
SHARD = D_IN // 4
WINDOW_STEP = 1216
WINDOW_ROWS = 1232

GLA_BLOCK = 256
MERGE_SLAB = 16
VMEM_LIMIT = 56 * 1024 * 1024

_F32 = jnp.float32
_MX = jnp.bfloat16
_ST = jnp.bfloat16

_MESH = pl.DeviceIdType.MESH
_ANY = pl.BlockSpec(memory_space=pl.ANY)


def _cparams(sem=None, vmem=None):
    return pltpu.CompilerParams(dimension_semantics=sem, vmem_limit_bytes=vmem)


def _dot(a, b):
    return jnp.dot(a.astype(_MX), b.astype(_MX), preferred_element_type=_F32)


def _dot_nt(a, b):
    return lax.dot_general(a.astype(_MX), b.astype(_MX), (((1,), (1,)), ((), ())),
                           preferred_element_type=_F32)


def _dot_tn(a, b):
    return lax.dot_general(a.astype(_MX), b.astype(_MX), (((0,), (0,)), ((), ())),
                           preferred_element_type=_F32)


def _dot_ones(ones_mat, v):
    o = ones_mat.astype(jnp.bfloat16)
    v0 = v.astype(jnp.bfloat16)
    r1 = v - v0.astype(_F32)
    v1 = r1.astype(jnp.bfloat16)
    v2 = (r1 - v1.astype(_F32)).astype(jnp.bfloat16)
    d = lambda t: jnp.dot(o, t, preferred_element_type=_F32)
    return d(v0) + d(v1) + d(v2)


def _sigmoid(x):
    return 0.5 * jnp.tanh(0.5 * x) + 0.5


def _log_sigmoid(x):
    return jnp.minimum(x, 0.0) - jnp.log(1.0 + jnp.exp(-jnp.abs(x)))


def _lane_tile(t, width):
    reps = width // t.shape[1]
    return t if reps == 1 else jnp.tile(t, (1, reps))


def _rope(t, cos, sa, sb, sign):
    w = t.shape[1]
    rot = pltpu.roll(t, w - 8, 1) * _lane_tile(sa, w) + pltpu.roll(t, 8, 1) * _lane_tile(sb, w)
    return t * _lane_tile(cos, w) + sign * rot


def _rms_bwd(dy_g, n, r):
    return r * (dy_g - n * jnp.mean(dy_g * n, axis=-1, keepdims=True))


ROPE_ROWS = 256


def _rope_consts():
    lane = np.arange(LANE) % A_HEAD_DIM
    half = ROPE_DIM // 2
    inv = np.exp((np.float32(-math.log(ROPE_THETA)) * np.arange(half, dtype=np.float32)) * np.float32(2.0 / ROPE_DIM))
    consts = np.zeros((8, LANE), np.float32)
    consts[0] = np.where(lane < ROPE_DIM, inv[lane % half], 0.0)
    consts[1] = np.where(lane < half, -1.0, 0.0)
    consts[2] = np.where((lane >= half) & (lane < ROPE_DIM), 1.0, 0.0)
    return jnp.asarray(consts)


def _rope_tables_into(pos_ref, c_ref, cos_ref, sa_ref, sb_ref):
    def rows_of(b, carry):
        rows = pl.ds(pl.multiple_of(b * ROPE_ROWS, ROPE_ROWS), ROPE_ROWS)
        ang = pos_ref[rows, :] * c_ref[0:1, :]
        s = jnp.sin(ang)
        cos_ref[rows, :] = jnp.cos(ang)
        sa_ref[rows, :] = s * c_ref[1:2, :]
        sb_ref[rows, :] = s * c_ref[2:3, :]
        return carry

    lax.fori_loop(0, pos_ref.shape[0] // ROPE_ROWS, rows_of, 0)


def _in_proj(x2, g_in, wt, cos, sa, sb):
    t = x2.shape[0]
    tm = min(t, 512)

    def body(x_ref, g_ref, w_ref, cos_ref, sa_ref, sb_ref, h_ref, qkv_ref, za_ref, qkb_ref,
             vb_ref, zb_ref, alr_ref, ga_ref, gb_ref):
        xv = x_ref[...]
        r = lax.rsqrt(jnp.mean(xv * xv, axis=-1, keepdims=True) + NORM_EPS)
        h = (xv * r * g_ref[...]).astype(_MX)
        h_ref[...] = h.astype(_ST)
        outs = dict(za=za_ref, qkb=qkb_ref, vb=vb_ref, zb=zb_ref, alr=alr_ref, ga=ga_ref, gb=gb_ref)
        for name, a, b in PIECES:
            p = _dot_nt(h, w_ref[a:b, :])
            if name == "qkv":
                c, s1, s2 = cos_ref[...], sa_ref[...], sb_ref[...]
                qkv_ref[:, 0:512] = _rope(p[:, 0:512], c, s1, s2, 1.0).astype(_ST)
                qkv_ref[:, 512:640] = _rope(p[:, 512:640], c, s1, s2, 1.0).astype(_ST)
                qkv_ref[:, 640:768] = p[:, 640:768].astype(_ST)
            else:
                outs[name][...] = p.astype(outs[name].dtype)

    rows = lambda w: pl.BlockSpec((tm, w), lambda i: (i, 0))
    shp = lambda name, w: jax.ShapeDtypeStruct((t, w), _F32 if name == "qkb" else _ST)
    widths = [D_MODEL] + [b - a for _, a, b in PIECES]
    return pl.pallas_call(
        body, name="in_proj", grid=(t // tm,),
        in_specs=[rows(D_MODEL), pl.BlockSpec((1, D_MODEL), lambda i: (0, 0)),
                  pl.BlockSpec((D_IN, D_MODEL), lambda i: (0, 0), pipeline_mode=pl.Buffered(1)),
                  rows(LANE), rows(LANE), rows(LANE)],
        out_specs=[rows(w) for w in widths],
        out_shape=[shp(n, w) for n, w in zip(["h"] + [p[0] for p in PIECES], widths)],
        compiler_params=_cparams(("parallel",), VMEM_LIMIT),
    )(x2, g_in, wt, cos, sa, sb)


def _attn_operands(k_prev, k_cur, v_prev, v_cur, want_bwd):
    kf = jnp.concatenate([k_prev, k_cur], axis=0).astype(_F32) * (A_HEAD_DIM ** -0.5)
    vf = jnp.concatenate([v_prev, v_cur], axis=0).astype(_F32)
    lo = lax.broadcasted_iota(jnp.int32, (1, LANE), 1) < 64

    def on_lanes(a):
        sw = pltpu.roll(a, 64, 1)
        z = jnp.zeros_like(a)
        return [[jnp.where(lo, a, z).astype(_MX), jnp.where(lo, z, sw).astype(_MX)],
                [jnp.where(lo, sw, z).astype(_MX), jnp.where(lo, z, a).astype(_MX)]]

    def on_rows(a):
        at = a.T.astype(_MX)
        z = jnp.zeros((64, at.shape[1]), _MX)
        top, bot = at[0:64], at[64:128]
        return [[jnp.concatenate([top, z], axis=0), jnp.concatenate([z, top], axis=0)],
                [jnp.concatenate([bot, z], axis=0), jnp.concatenate([z, bot], axis=0)]]

    ops = dict(k_lanes=on_lanes(kf), v_rows=on_rows(vf), lo=lo)
    if want_bwd:
        ops.update(v_lanes=on_lanes(vf), k_rows=on_rows(kf))
    return ops


def _attn_valid(n):
    kj = lax.broadcasted_iota(jnp.int32, (2 * BLOCK, 2 * BLOCK), 0) - BLOCK
    qi = lax.broadcasted_iota(jnp.int32, (2 * BLOCK, 2 * BLOCK), 1) & (BLOCK - 1)
    return (kj <= qi) & (qi - kj < BLOCK) & ((n > 0) | (kj >= 0))


def _attn_sinks(sink_ref, h_a, h_b):
    first = lax.broadcasted_iota(jnp.int32, (1, 2 * BLOCK), 1) < BLOCK
    return jnp.where(first, sink_ref[h_a], sink_ref[h_b])


def _attn_softmax_t(k_lanes, q_pair, valid, sink):
    s = jnp.where(valid, _dot_nt(k_lanes, q_pair), NEG_BIG)
    m = jnp.maximum(jnp.max(s, axis=0, keepdims=True), sink)
    e = jnp.exp(s - m)
    e_sink = jnp.exp(sink - m)
    inv = 1.0 / (jnp.sum(e, axis=0, keepdims=True) + e_sink)
    return e, e_sink, inv


ATTN_TILE = 4


def _attn_kv(qkv_ref, kvp_ref, j):
    rows = slice(j * BLOCK, (j + 1) * BLOCK)
    if j == 0:
        k_prev, v_prev = kvp_ref[:, 0:128], kvp_ref[:, 128:256]
    else:
        before = slice((j - 1) * BLOCK, j * BLOCK)
        k_prev, v_prev = qkv_ref[before, 512:640], qkv_ref[before, 640:768]
    return k_prev, qkv_ref[rows, 512:640], v_prev, qkv_ref[rows, 640:768]


def _attn_fwd(qkv, za, sinks, nseq):
    t = qkv.shape[0]
    tile = ATTN_TILE * BLOCK
    nt = t // nseq // tile

    def body(sink_ref, qkv_ref, kvp_ref, za_ref, oa_ref):
        for j in range(ATTN_TILE):
            rows = slice(j * BLOCK, (j + 1) * BLOCK)
            ops = _attn_operands(*_attn_kv(qkv_ref, kvp_ref, j), False)
            valid = _attn_valid(ATTN_TILE * pl.program_id(1) + j)[:, 0:BLOCK]
            for pr in range(A_HEADS // 2):
                lanes = slice(pr * LANE, (pr + 1) * LANE)
                g = pr // (A_GROUP // 2)
                q_pair = qkv_ref[rows, lanes]
                ot = None
                for half in range(2):
                    e, _, inv = _attn_softmax_t(ops["k_lanes"][g][half], q_pair, valid, sink_ref[2 * pr + half])
                    part = _dot(ops["v_rows"][g][half], e) * inv
                    ot = part if ot is None else ot + part
                z = za_ref[rows, lanes].astype(_F32)
                oa_ref[rows, lanes] = (ot.T * (z * _sigmoid(z))).astype(_ST)

    cur = lambda w: pl.BlockSpec((tile, w), lambda s, n: (s * nt + n, 0))
    return pl.pallas_call(
        body, name="attn_fwd", grid=(nseq, nt),
        in_specs=[pl.BlockSpec(memory_space=pltpu.SMEM), cur(768),
                  pl.BlockSpec((BLOCK, 256), lambda s, n: (ATTN_TILE * (s * nt + n) - jnp.minimum(n, 1), 2)),
                  cur(512)],
        out_specs=cur(512), out_shape=jax.ShapeDtypeStruct((t, A_WIDTH), _ST),
        compiler_params=_cparams(("parallel", "arbitrary")),
    )(sinks, qkv, qkv, za)


def _attn_bwd(qkv, za, doa, sinks, cos, sa, sb, nseq):
    t = qkv.shape[0]
    tile = ATTN_TILE * BLOCK
    nt = t // nseq // tile

    def body(sink_ref, qkv_ref, kvp_ref, za_ref, doa_ref, cos_ref, sa_ref, sb_ref,
             dqkv_ref, dza_ref, dsink_ref, ck_ref, cv_ref):
        s_id, i = pl.program_id(0), pl.program_id(1)

        @pl.when((s_id == 0) & (i == 0))
        def _():
            dsink_ref[...] = jnp.zeros_like(dsink_ref)

        @pl.when(i == 0)
        def _():
            ck_ref[...] = jnp.zeros_like(ck_ref)
            cv_ref[...] = jnp.zeros_like(cv_ref)

        carry_k, carry_v = ck_ref[...], cv_ref[...]
        for j in reversed(range(ATTN_TILE)):
            rows = slice(j * BLOCK, (j + 1) * BLOCK)
            ops = _attn_operands(*_attn_kv(qkv_ref, kvp_ref, j), True)
            lo = ops["lo"]
            valid = _attn_valid(ATTN_TILE * (nt - 1 - i) + j)
            dk_acc, dv_acc, dq_pairs = [], [], []
            for g in range(A_KV_HEADS):
                pairs = [slice((2 * g + p) * LANE, (2 * g + p + 1) * LANE) for p in range(2)]
                q_both = jnp.concatenate([qkv_ref[rows, p] for p in pairs], axis=0)
                q_f = q_both.astype(_F32)
                z = [za_ref[rows, p].astype(_F32) for p in pairs]
                sz = [_sigmoid(v) for v in z]
                d_oa = [doa_ref[rows, p].astype(_F32) for p in pairs]
                d_att = jnp.concatenate([d_oa[p] * (z[p] * sz[p]) for p in range(2)], axis=0)
                zero = jnp.zeros_like(d_att)
                ot, dqt, ds_all, pn_all, qz_all, daz_all = None, None, [], [], [], []
                for half in range(2):
                    heads = (4 * g + half, 4 * g + 2 + half)
                    e, e_sink, inv = _attn_softmax_t(ops["k_lanes"][g][half], q_both, valid,
                                                     _attn_sinks(sink_ref, *heads))
                    pn = e * inv
                    dpt = _dot_nt(ops["v_lanes"][g][half], d_att)
                    delta = jnp.sum(pn * dpt, axis=0, keepdims=True)
                    ds = (pn * (dpt - delta)).astype(_MX)
                    pn = pn.astype(_MX)
                    d_sink = e_sink * inv * delta
                    for p, h in enumerate(heads):
                        dsink_ref[h:h + 1, :] = dsink_ref[h:h + 1, :] - jnp.sum(d_sink[:, p * BLOCK:(p + 1) * BLOCK])
                    o_part = _dot(ops["v_rows"][g][half], pn)
                    dq_part = _dot(ops["k_rows"][g][half], ds)
                    ot = o_part if ot is None else ot + o_part
                    dqt = dq_part if dqt is None else dqt + dq_part
                    mine = lo if half == 0 else jnp.logical_not(lo)
                    ds_all.append(ds)
                    pn_all.append(pn)
                    qz_all.append(jnp.where(mine, q_f, zero).astype(_MX))
                    daz_all.append(jnp.where(mine, d_att, zero).astype(_MX))
                dk_acc.append(_dot(jnp.concatenate(ds_all, axis=1), jnp.concatenate(qz_all, axis=0)))
                dv_acc.append(_dot(jnp.concatenate(pn_all, axis=1), jnp.concatenate(daz_all, axis=0)))
                for p, lanes in enumerate(pairs):
                    cols = slice(p * BLOCK, (p + 1) * BLOCK)
                    dza_ref[rows, lanes] = (d_oa[p] * ot[:, cols].T * (sz[p] * (1.0 + z[p] * (1.0 - sz[p])))).astype(_ST)
                    dq_pairs.append(dqt[:, cols].T)

            def fold(acc, scale):
                both = [a + pltpu.roll(a, 64, 1) for a in acc]
                return jnp.where(lo, both[0], both[1]) * scale

            dk_full = fold(dk_acc, A_HEAD_DIM ** -0.5)
            dv_full = fold(dv_acc, 1.0)
            dk_cur, dv_cur = dk_full[BLOCK:] + carry_k, dv_full[BLOCK:] + carry_v
            carry_k, carry_v = dk_full[:BLOCK], dv_full[:BLOCK]
            c, s1, s2 = cos_ref[rows, :], sa_ref[rows, :], sb_ref[rows, :]
            dqkv_ref[rows, 0:512] = _rope(jnp.concatenate(dq_pairs, axis=1), c, s1, s2, -1.0).astype(_ST)
            dqkv_ref[rows, 512:640] = _rope(dk_cur, c, s1, s2, -1.0).astype(_ST)
            dqkv_ref[rows, 640:768] = dv_cur.astype(_ST)
        ck_ref[...] = carry_k
        cv_ref[...] = carry_v

    cur = lambda w: pl.BlockSpec((tile, w), lambda s, i: (s * nt + nt - 1 - i, 0))
    return pl.pallas_call(
        body, name="attn_bwd", grid=(nseq, nt),
        in_specs=[pl.BlockSpec(memory_space=pltpu.SMEM), cur(768),
                  pl.BlockSpec((BLOCK, 256),
                               lambda s, i: (ATTN_TILE * (s * nt + nt - 1 - i) - jnp.minimum(nt - 1 - i, 1), 2)),
                  cur(512), cur(512), cur(LANE), cur(LANE), cur(LANE)],
        out_specs=[cur(768), cur(512), pl.BlockSpec((8, LANE), lambda s, i: (0, 0))],
        out_shape=[jax.ShapeDtypeStruct((t, 768), _ST), jax.ShapeDtypeStruct((t, 512), _ST),
                   jax.ShapeDtypeStruct((8, LANE), _F32)],
        scratch_shapes=[pltpu.VMEM((BLOCK, A_KV_WIDTH), _F32), pltpu.VMEM((BLOCK, A_KV_WIDTH), _F32)],
        compiler_params=_cparams(("arbitrary", "arbitrary")),
    )(sinks, qkv, qkv, za, doa, cos, sa, sb)


def _gla_chunk_terms(la, qkb_ref, r0):
    g = la[r0:r0 + B_CHUNK, :]
    ri = lax.broadcasted_iota(jnp.int32, (B_CHUNK, B_CHUNK), 0)
    ci = lax.broadcasted_iota(jnp.int32, (B_CHUNK, B_CHUNK), 1)
    cum = _dot_ones((ri >= ci).astype(_F32), g)
    last = cum[B_CHUNK - 1:B_CHUNK, :]
    mid = cum[B_CHUNK // 2 - 1:B_CHUNK // 2, :]
    q = qkb_ref[r0:r0 + B_CHUNK, 0:B_KEY_WIDTH].astype(_F32) * (B_KEY_DIM ** -0.5)
    k = qkb_ref[r0:r0 + B_CHUNK, B_KEY_WIDTH:2 * B_KEY_WIDTH].astype(_F32)
    e_q, e_k, e_l, e_c = jnp.exp(cum - mid), jnp.exp(mid - cum), jnp.exp(last - cum), jnp.exp(cum)
    dec_col = jnp.exp(jnp.sum(g.T, axis=1, keepdims=True))
    return dict(qm=q * e_q, km=k * e_k, kl=k * e_l, qc=q * e_c, e_q=e_q, e_k=e_k, e_l=e_l, e_c=e_c,
                dec_col=dec_col, dec_row=jnp.exp(last), causal=ri >= ci, ri=ri)


def _gate_logits(alr_ref, wup_ref, b_ref):
    return _dot(alr_ref[...], wup_ref[...]) + b_ref[...]


def _gla_fwd(qkb, vb, zb, alr, wup, b_alpha, gn, nseq):
    t = qkb.shape[0]
    tb = min(GLA_BLOCK, t // nseq)
    nblk = t // nseq // tb
    cpb = tb // B_CHUNK

    def body(qkb_ref, vb_ref, zb_ref, alr_ref, wup_ref, b_ref, gn_ref, ob_ref, oraw_ref, sst_ref, s_ref):
        @pl.when(pl.program_id(1) == 0)
        def _():
            s_ref[...] = jnp.zeros_like(s_ref)

        la = _log_sigmoid(_gate_logits(alr_ref, wup_ref, b_ref)) * (1.0 / B_GATE_TEMP)
        terms = [_gla_chunk_terms(la, qkb_ref, c * B_CHUNK) for c in range(cpb)]
        o_intra, inc = {}, {}
        for c, tm in enumerate(terms):
            for h in range(B_HEADS):
                kl_, vl_ = slice(h * 64, (h + 1) * 64), slice(h * 128, (h + 1) * 128)
                v = vb_ref[c * B_CHUNK:(c + 1) * B_CHUNK, vl_]
                a = jnp.where(tm["causal"], _dot_nt(tm["qm"][:, kl_], tm["km"][:, kl_]), 0.0)
                o_intra[c, h] = _dot(a, v)
                inc[c, h] = _dot_tn(tm["kl"][:, kl_], v)
        o_heads = {}
        for h in range(B_HEADS):
            kl_ = slice(h * 64, (h + 1) * 64)
            st = s_ref[kl_, :]
            for c, tm in enumerate(terms):
                sst_ref[c, kl_, :] = st
                o_heads[c, h] = o_intra[c, h] + _dot(tm["qc"][:, kl_], st)
                st = tm["dec_col"][kl_, :] * st + inc[c, h]
            s_ref[kl_, :] = st
        o = jnp.concatenate([jnp.concatenate([o_heads[c, h] for h in range(B_HEADS)], axis=1)
                             for c in range(cpb)], axis=0)
        oraw_ref[...] = o
        z = zb_ref[...].astype(_F32)
        gate = z * _sigmoid(z)
        for h in range(B_HEADS):
            vl_ = slice(h * 128, (h + 1) * 128)
            oh = o[:, vl_]
            r = lax.rsqrt(jnp.mean(oh * oh, axis=-1, keepdims=True) + NORM_EPS)
            ob_ref[:, vl_] = ((oh * r) * gn_ref[:, vl_] * gate[:, vl_]).astype(_ST)

    rows = lambda w: pl.BlockSpec((tb, w), lambda s, i: (s * nblk + i, 0))
    full = lambda a, b: pl.BlockSpec((a, b), lambda s, i: (0, 0))
    return pl.pallas_call(
        body, name="gla_fwd", grid=(nseq, nblk),
        in_specs=[rows(512), rows(512), rows(512), rows(LANE), full(LANE, B_KEY_WIDTH),
                  full(1, B_KEY_WIDTH), full(1, B_WIDTH)],
        out_specs=[rows(512), rows(512),
                   pl.BlockSpec((cpb, B_KEY_WIDTH, B_VAL_DIM), lambda s, i: (s * nblk + i, 0, 0))],
        out_shape=[jax.ShapeDtypeStruct((t, B_WIDTH), _ST), jax.ShapeDtypeStruct((t, B_WIDTH), _F32),
                   jax.ShapeDtypeStruct((t // B_CHUNK, B_KEY_WIDTH, B_VAL_DIM), _F32)],
        scratch_shapes=[pltpu.VMEM((B_KEY_WIDTH, B_VAL_DIM), _F32)],
        compiler_params=_cparams(("parallel", "arbitrary")),
    )(qkb, vb, zb, alr, wup, b_alpha, gn)


def _gla_bwd(qkb, vb, zb, alr, oraw, dob, sst, wup, b_alpha, gn, nseq):
    t = qkb.shape[0]
    tb = min(GLA_BLOCK, t // nseq)
    nblk = t // nseq // tb
    cpb = tb // B_CHUNK

    def body(qkb_ref, vb_ref, zb_ref, alr_ref, oraw_ref, dob_ref, sst_ref, wup_ref, b_ref, gn_ref,
             dqkb_ref, dvb_ref, dzb_ref, dalr_ref, dwup_ref, db_ref, dgn_ref, ds_ref):
        s_id, i = pl.program_id(0), pl.program_id(1)

        @pl.when((s_id == 0) & (i == 0))
        def _():
            dwup_ref[...] = jnp.zeros_like(dwup_ref)
            db_ref[...] = jnp.zeros_like(db_ref)
            dgn_ref[...] = jnp.zeros_like(dgn_ref)

        @pl.when(i == 0)
        def _():
            ds_ref[...] = jnp.zeros_like(ds_ref)

        a_pre = _gate_logits(alr_ref, wup_ref, b_ref)
        la = _log_sigmoid(a_pre) * (1.0 / B_GATE_TEMP)

        z = zb_ref[...].astype(_F32)
        sz = _sigmoid(z)
        d_ob = dob_ref[...].astype(_F32)
        tg = d_ob * (z * sz)
        dsilu = sz * (1.0 + z * (1.0 - sz))
        do_cols, dgn_cols = [], []
        for h in range(B_HEADS):
            vl_ = slice(h * 128, (h + 1) * 128)
            oh = oraw_ref[:, vl_].astype(_F32)
            r = lax.rsqrt(jnp.mean(oh * oh, axis=-1, keepdims=True) + NORM_EPS)
            on = oh * r
            gnh = gn_ref[:, vl_]
            dzb_ref[:, vl_] = (d_ob[:, vl_] * (on * gnh) * dsilu[:, vl_]).astype(_ST)
            dgn_cols.append(jnp.sum(tg[:, vl_] * on, axis=0, keepdims=True))
            do_cols.append(_rms_bwd(tg[:, vl_] * gnh, on, r))
        dgn_ref[...] = dgn_ref[...] + jnp.concatenate(dgn_cols, axis=1)
        d_o = jnp.concatenate(do_cols, axis=1)

        ri = lax.broadcasted_iota(jnp.int32, (tb, tb), 0)
        ci = lax.broadcasted_iota(jnp.int32, (tb, tb), 1)
        same = (ri // B_CHUNK) == (ci // B_CHUNK)
        low = same & (ri >= ci)
        upto_mid = same & ((ci % B_CHUNK) < B_CHUNK // 2)
        sums = _dot_ones(jnp.concatenate([m.astype(_F32) for m in (low, same, upto_mid)], axis=0), la)
        cum, last, mid = sums[0:tb], sums[tb:2 * tb], sums[2 * tb:3 * tb]
        e_q, e_k, e_l, e_c = jnp.exp(cum - mid), jnp.exp(mid - cum), jnp.exp(last - cum), jnp.exp(cum)
        q = qkb_ref[:, 0:B_KEY_WIDTH] * (B_KEY_DIM ** -0.5)
        k = qkb_ref[:, B_KEY_WIDTH:2 * B_KEY_WIDTH]
        qm, km, kl, qc = q * e_q, k * e_k, k * e_l, q * e_c
        lane_head = lax.broadcasted_iota(jnp.int32, (1, B_KEY_WIDTH), 1) // B_KEY_DIM
        d_o_mx = d_o.astype(_MX)

        def on_diagonal(st):
            z = jnp.zeros((B_KEY_DIM, B_VAL_DIM), st.dtype)
            return jnp.concatenate([jnp.concatenate(
                [st[h * B_KEY_DIM:(h + 1) * B_KEY_DIM] if g == h else z for g in range(B_HEADS)], axis=1)
                for h in range(B_HEADS)], axis=0)

        def diagonal_of(full):
            return jnp.concatenate([full[h * B_KEY_DIM:(h + 1) * B_KEY_DIM, h * B_VAL_DIM:(h + 1) * B_VAL_DIM]
                                    for h in range(B_HEADS)], axis=0)

        dqm, dkm, dv_cols = None, None, []
        for h in range(B_HEADS):
            vl_ = slice(h * B_VAL_DIM, (h + 1) * B_VAL_DIM)
            mine = lane_head == h
            qz, kz = jnp.where(mine, qm, 0.0).astype(_MX), jnp.where(mine, km, 0.0).astype(_MX)
            a = jnp.where(low, _dot_nt(qz, kz), 0.0).astype(_MX)
            da = jnp.where(low, _dot_nt(d_o_mx[:, vl_], vb_ref[:, vl_]), 0.0).astype(_MX)
            dqm_h, dkm_h = _dot(da, kz), _dot_tn(da, qz)
            dqm = dqm_h if dqm is None else dqm + dqm_h
            dkm = dkm_h if dkm is None else dkm + dkm_h
            dv_cols.append(_dot_tn(a, d_o_mx[:, vl_]))
        dv = jnp.concatenate(dv_cols, axis=1)

        chunk = [slice(c * B_CHUNK, (c + 1) * B_CHUNK) for c in range(cpb)]
        dqc_rows, g_loc = [], []
        for c in range(cpb):
            dqc_rows.append(_dot_nt(d_o_mx[chunk[c]], on_diagonal(sst_ref[c].astype(_MX))))
            g_loc.append(diagonal_of(_dot_tn(qc[chunk[c]], d_o_mx[chunk[c]])))
        cur = ds_ref[...]
        d_state = [None] * cpb
        for c in reversed(range(cpb)):
            d_state[c] = cur
            cur = g_loc[c] + jnp.exp(jnp.sum(la[chunk[c]].T, axis=1, keepdims=True)) * cur
        ds_ref[...] = cur
        dkl_rows, dv_rows, dlast_rows = [], [], []
        ones8 = jnp.ones((8, B_VAL_DIM), _F32)
        for c in range(cpb):
            dsd = on_diagonal(d_state[c].astype(_MX))
            dkl_c = _dot_nt(vb_ref[chunk[c], :], dsd)
            dkl_rows.append(dkl_c)
            dv_rows.append(_dot(kl[chunk[c]], dsd))
            prod = d_state[c] * sst_ref[c]
            p0 = prod.astype(jnp.bfloat16)
            p1 = (prod - p0.astype(_F32)).astype(jnp.bfloat16)
            p2 = (prod - p0.astype(_F32) - p1.astype(_F32)).astype(jnp.bfloat16)
            ddec = (_dot_nt(ones8, p0) + _dot_nt(ones8, p1) + _dot_nt(ones8, p2))[0:1]
            r_last = c * B_CHUNK + B_CHUNK - 1
            dlast = jnp.sum(dkl_c * kl[chunk[c]], axis=0, keepdims=True) + ddec * jnp.exp(last[r_last:r_last + 1])
            dlast_rows.append(jnp.broadcast_to(dlast, (B_CHUNK, B_KEY_WIDTH)))
        dqc, dkl = jnp.concatenate(dqc_rows, axis=0), jnp.concatenate(dkl_rows, axis=0)
        dqkb_ref[:, 0:B_KEY_WIDTH] = ((dqm * e_q + dqc * e_c) * (B_KEY_DIM ** -0.5)).astype(_ST)
        dqkb_ref[:, B_KEY_WIDTH:2 * B_KEY_WIDTH] = (dkm * e_k + dkl * e_l).astype(_ST)
        dvb_ref[...] = (dv + jnp.concatenate(dv_rows, axis=0)).astype(_ST)
        dcum = dqm * qm - dkm * km + dqc * qc - dkl * kl
        row = lax.broadcasted_iota(jnp.int32, (tb, B_KEY_WIDTH), 0)
        dcum = jnp.where(row % B_CHUNK == B_CHUNK - 1, dcum + jnp.concatenate(dlast_rows, axis=0), dcum)
        dla = _dot_ones((same & (ri <= ci)).astype(_F32), dcum)

        da_pre = dla * (1.0 / B_GATE_TEMP) * (1.0 - _sigmoid(a_pre))
        dalr_ref[...] = _dot_nt(da_pre, wup_ref[...]).astype(_ST)
        dwup_ref[...] = dwup_ref[...] + _dot_tn(alr_ref[...], da_pre)
        db_ref[...] = db_ref[...] + jnp.sum(da_pre, axis=0, keepdims=True)

    blk = lambda s, i: s * nblk + nblk - 1 - i
    rows = lambda w: pl.BlockSpec((tb, w), lambda s, i: (blk(s, i), 0))
    full = lambda a, b: pl.BlockSpec((a, b), lambda s, i: (0, 0))
    act = lambda w: jax.ShapeDtypeStruct((t, w), _ST)
    return pl.pallas_call(
        body, name="gla_bwd", grid=(nseq, nblk),
        in_specs=[rows(512), rows(512), rows(512), rows(LANE), rows(512), rows(512),
                  pl.BlockSpec((cpb, B_KEY_WIDTH, B_VAL_DIM), lambda s, i: (blk(s, i), 0, 0)),
                  full(LANE, B_KEY_WIDTH), full(1, B_KEY_WIDTH), full(1, B_WIDTH)],
        out_specs=[rows(512), rows(512), rows(512), rows(LANE), full(LANE, B_KEY_WIDTH),
                   full(1, B_KEY_WIDTH), full(1, B_WIDTH)],
        out_shape=[act(512), act(512), act(512), act(LANE),
                   jax.ShapeDtypeStruct((LANE, B_KEY_WIDTH), _F32),
                   jax.ShapeDtypeStruct((1, B_KEY_WIDTH), _F32), jax.ShapeDtypeStruct((1, B_WIDTH), _F32)],
        scratch_shapes=[pltpu.VMEM((B_KEY_WIDTH, B_VAL_DIM), _F32)],
        compiler_params=_cparams(("arbitrary", "arbitrary")),
    )(qkb, vb, zb, alr, oraw, dob, sst, wup, b_alpha, gn)


def _merge_loss(oa, ob, ga, gb, x2, tgt, wa, wb, wo, g_final):
    t = x2.shape[0]
    tm = min(t, 512)
    nt = t // tm

    def body(oa_ref, ob_ref, ga_ref, gb_ref, x_ref, t_ref, wa_ref, wb_ref, wo_ref, gf_ref,
             dh_ref, doa_ref, dob_ref, dga_ref, dgb_ref, dwa_ref, dwb_ref, dwo_ref, dgf_ref, loss_ref,
             ya_s, yb_s, out_s, dmer_s, mrg_s, dya_s, dyb_s):
        first = pl.program_id(0) == 0
        so_far = lambda ref: jnp.where(first, 0.0, ref[...])

        slabs = [slice(s, s + MERGE_SLAB) for s in range(0, tm, MERGE_SLAB)]
        fold = lambda a: a[0:8] + a[8:16]
        ya_s[...] = _dot(oa_ref[...], wa_ref[...])
        yb_s[...] = _dot(ob_ref[...], wb_ref[...])
        for rows_ in slabs:
            sga, sgb = _sigmoid(ga_ref[rows_, :].astype(_F32)), _sigmoid(gb_ref[rows_, :].astype(_F32))
            mrg_s[rows_, :] = (sga * ya_s[rows_, :] + sgb * yb_s[rows_, :]).astype(_MX)
        out_s[...] = x_ref[...] + _dot(mrg_s[...], wo_ref[...])
        gf = gf_ref[...]
        loss8 = jnp.zeros((8, D_MODEL), _F32)
        dgf8 = jnp.zeros((8, D_MODEL), _F32)
        for rows_ in slabs:
            out = out_s[rows_, :]
            r = lax.rsqrt(jnp.mean(out * out, axis=-1, keepdims=True) + NORM_EPS)
            nrm = out * r
            err = nrm * gf - t_ref[rows_, :]
            loss8 = loss8 + fold(err * err)
            dy = err * (1.0 / D_MODEL)
            dgf8 = dgf8 + fold(dy * nrm)
            dh = _rms_bwd(dy * gf, nrm, r)
            dh_ref[rows_, :] = dh.astype(_ST)
        loss_ref[...] = so_far(loss_ref) + (0.5 / D_MODEL) * jnp.sum(loss8, axis=0, keepdims=True)
        dgf_ref[...] = so_far(dgf_ref) + jnp.sum(dgf8, axis=0, keepdims=True)
        dmer_s[...] = _dot_nt(dh_ref[...], wo_ref[...])
        dwo_ref[...] = so_far(dwo_ref) + _dot_tn(mrg_s[...], dh_ref[...])
        for rows_ in slabs:
            sga, sgb = _sigmoid(ga_ref[rows_, :].astype(_F32)), _sigmoid(gb_ref[rows_, :].astype(_F32))
            dmer = dmer_s[rows_, :]
            da, db = dmer * sga, dmer * sgb
            dya_s[rows_, :] = da.astype(_MX)
            dyb_s[rows_, :] = db.astype(_MX)
            dga_ref[rows_, :] = (da * ya_s[rows_, :] * (1.0 - sga)).astype(_ST)
            dgb_ref[rows_, :] = (db * yb_s[rows_, :] * (1.0 - sgb)).astype(_ST)
        doa_ref[...] = _dot_nt(dya_s[...], wa_ref[...]).astype(_ST)
        dob_ref[...] = _dot_nt(dyb_s[...], wb_ref[...]).astype(_ST)
        dwa_ref[...] = so_far(dwa_ref) + _dot_tn(oa_ref[...], dya_s[...])
        dwb_ref[...] = so_far(dwb_ref) + _dot_tn(ob_ref[...], dyb_s[...])

    rows = lambda w: pl.BlockSpec((tm, w), lambda i: (i, 0))
    full = lambda a, b: pl.BlockSpec((a, b), lambda i: (0, 0), pipeline_mode=pl.Buffered(1))
    return pl.pallas_call(
        body, name="merge_loss", grid=(nt,),
        in_specs=[rows(512), rows(512), rows(D_MODEL), rows(D_MODEL), rows(D_MODEL), rows(D_MODEL),
                  full(A_WIDTH, D_MODEL), full(B_WIDTH, D_MODEL), full(D_MODEL, D_MODEL), full(1, D_MODEL)],
        out_specs=[rows(D_MODEL), rows(512), rows(512), rows(D_MODEL), rows(D_MODEL),
                   full(A_WIDTH, D_MODEL), full(B_WIDTH, D_MODEL), full(D_MODEL, D_MODEL),
                   full(1, D_MODEL), full(1, D_MODEL)],
        out_shape=[jax.ShapeDtypeStruct((t, D_MODEL), _ST), jax.ShapeDtypeStruct((t, 512), _ST),
                   jax.ShapeDtypeStruct((t, 512), _ST), jax.ShapeDtypeStruct((t, D_MODEL), _ST),
                   jax.ShapeDtypeStruct((t, D_MODEL), _ST),
                   jax.ShapeDtypeStruct((A_WIDTH, D_MODEL), _F32), jax.ShapeDtypeStruct((B_WIDTH, D_MODEL), _F32),
                   jax.ShapeDtypeStruct((D_MODEL, D_MODEL), _F32), jax.ShapeDtypeStruct((1, D_MODEL), _F32),
                   jax.ShapeDtypeStruct((1, D_MODEL), _F32)],
        scratch_shapes=[pltpu.VMEM((tm, D_MODEL), _F32)] * 4 + [pltpu.VMEM((tm, D_MODEL), _MX)] * 3,
        compiler_params=_cparams(("arbitrary",), VMEM_LIMIT),
    )(oa, ob, ga, gb, x2, tgt, wa, wb, wo, g_final)


def _in_proj_bwd_x(dpieces, wt, x2, dh2, g_in):
    t = x2.shape[0]
    tm = min(t, 512)
    np_ = len(PIECES)

    def body(*refs):
        dp_refs = refs[:np_]
        w_ref, x_ref, dh2_ref, g_ref, gx_ref, dg_ref = refs[np_:]

        @pl.when(pl.program_id(0) == 0)
        def _():
            dg_ref[...] = jnp.zeros_like(dg_ref)

        dh = None
        for (name, a, b), dp in zip(PIECES, dp_refs):
            part = _dot(dp[...], w_ref[a:b, :])
            dh = part if dh is None else dh + part
        xv = x_ref[...]
        r = lax.rsqrt(jnp.mean(xv * xv, axis=-1, keepdims=True) + NORM_EPS)
        nrm = xv * r
        dg_ref[...] = dg_ref[...] + jnp.sum(dh * nrm, axis=0, keepdims=True)
        gx_ref[...] = dh2_ref[...].astype(_F32) + _rms_bwd(dh * g_ref[...], nrm, r)

    rows = lambda w: pl.BlockSpec((tm, w), lambda i: (i, 0))
    full = lambda a, b: pl.BlockSpec((a, b), lambda i: (0, 0), pipeline_mode=pl.Buffered(1))
    return pl.pallas_call(
        body, name="in_proj_bwd_x", grid=(t // tm,),
        in_specs=[rows(b - a) for _, a, b in PIECES] + [full(D_IN, D_MODEL), rows(D_MODEL), rows(D_MODEL),
                                                          full(1, D_MODEL)],
        out_specs=[rows(D_MODEL), full(1, D_MODEL)],
        out_shape=[jax.ShapeDtypeStruct((t, D_MODEL), _F32), jax.ShapeDtypeStruct((1, D_MODEL), _F32)],
        compiler_params=_cparams(("arbitrary",), VMEM_LIMIT),
    )(*dpieces, wt, x2, dh2, g_in)


def _in_proj_bwd_w(h, dpieces):
    t = h.shape[0]
    tm = min(t, 1024)
    nt = t // tm
    np_ = len(PIECES)

    def body(*refs):
        h_ref, dp_refs, out_ref = refs[0], refs[1:1 + np_], refs[1 + np_]
        acc_ref, sem = refs[2 + np_:]
        i = pl.program_id(0)
        hv = h_ref[...]
        writes = []
        for j, ((name, a, b), dp) in enumerate(zip(PIECES, dp_refs)):
            part = _dot_tn(dp[...], hv)
            if name == "alr":
                b = a + B_GATE_RANK
                part = part[0:B_GATE_RANK]
            acc_ref[a:b, :] = jnp.where(i == 0, 0.0, acc_ref[a:b, :]) + part
            writes.append(pltpu.make_async_copy(acc_ref.at[a:b], out_ref.at[a:b], sem.at[j]))

            @pl.when(i == nt - 1)
            def _(cp=writes[-1]):
                cp.start()

        @pl.when(i == nt - 1)
        def _():
            for cp in writes:
                cp.wait()

    rows = lambda w: pl.BlockSpec((tm, w), lambda i: (i, 0))
    return pl.pallas_call(
        body, name="in_proj_bwd_w", grid=(nt,),
        in_specs=[rows(D_MODEL)] + [rows(b - a) for _, a, b in PIECES],
        out_specs=_ANY, out_shape=jax.ShapeDtypeStruct((D_IN, D_MODEL), _F32),
        scratch_shapes=[pltpu.VMEM((D_IN, D_MODEL), _F32), pltpu.SemaphoreType.DMA((np_,))],
        compiler_params=_cparams(("arbitrary",), VMEM_LIMIT),
    )(h, *dpieces)


def _place():
    return lax.axis_index("x"), lax.axis_index("y"), lax.axis_index("c")


def _other_chips(x, y):
    return [(1 - x, y), (x, 1 - y), (1 - x, 1 - y)]


class _Split(NamedTuple):
    by_rows: bool
    step: int
    size: int

    def half(self, ref, c):
        r, n = ref.shape[-2:]
        if self.by_rows:
            return ref.at[:, pl.ds(pl.multiple_of(c * (n // 2), LANE), n // 2)]
        return ref.at[pl.ds(pl.multiple_of(c * (r // 2), 16), r // 2), :]

    def chip_part(self, ref, k):
        if self.by_rows:
            return ref.at[pl.ds(pl.multiple_of(k * self.step, 16), self.size), :]
        return ref.at[:, pl.ds(pl.multiple_of(k * self.size, LANE), self.size)]

    def half_shape(self, shape):
        r, n = shape
        return (r, n // 2) if self.by_rows else (r // 2, n)

    def part_shape(self, shape):
        r, n = shape
        return (self.size, n) if self.by_rows else (r, self.size)


SPLIT_W_IN_T = _Split(True, WINDOW_STEP, WINDOW_ROWS)
SPLIT_W_O = _Split(True, 256, 256)
SPLIT_W_OUT = _Split(False, 256, 256)


def _gather_weights(shards, splits, fulls, pos_f):
    nw = len(shards)
    t = pos_f.shape[0]

    def body(*refs):
        ins, (pos_ref, c_ref) = refs[:nw], refs[nw:nw + 2]
        outs, tables = refs[nw + 2:2 * nw + 2], refs[2 * nw + 2:2 * nw + 5]
        send_a, recv_a, send_b, recv_b = refs[2 * nw + 5:]
        x, y, c = _place()
        me = 2 * x + y
        peers = _other_chips(x, y)

        def place(i, k, half):
            if splits[i] is None:
                return outs[i].at[k]
            if fulls[i][0] == 4 and len(fulls[i]) == 3:
                whole = outs[i].at[k]
            else:
                whole = splits[i].chip_part(outs[i], k)
            return splits[i].half(whole, half)

        first, passed = [], []
        for i in range(nw):
            src = ins[i] if splits[i] is None else splits[i].half(ins[i], c)
            for j, (px, py) in enumerate(peers):
                cp = pltpu.make_async_remote_copy(
                    src_ref=src, dst_ref=place(i, me, c), send_sem=send_a.at[3 * i + j],
                    recv_sem=recv_a.at[3 * i + j], device_id=(px, py, c), device_id_type=_MESH)
                cp.start()
                first.append(cp)
        _rope_tables_into(pos_ref, c_ref, *tables)
        for i in range(nw):
            for j, (px, py) in enumerate(peers):
                landed = place(i, 2 * px + py, c)
                pltpu.make_async_remote_copy(
                    src_ref=landed, dst_ref=landed, send_sem=send_a.at[3 * i + j], recv_sem=recv_a.at[3 * i + j],
                    device_id=(px, py, c), device_id_type=_MESH).wait_recv()
                if splits[i] is not None:
                    cp = pltpu.make_async_remote_copy(
                        src_ref=landed, dst_ref=landed, send_sem=send_b.at[3 * i + j], recv_sem=recv_b.at[3 * i + j],
                        device_id=(x, y, 1 - c), device_id_type=_MESH)
                    cp.start()
                    passed.append(cp)
        for i in range(nw):
            if splits[i] is None:
                continue
            for j, (px, py) in enumerate(peers):
                theirs = place(i, 2 * px + py, 1 - c)
                pltpu.make_async_remote_copy(
                    src_ref=theirs, dst_ref=theirs, send_sem=send_b.at[3 * i + j], recv_sem=recv_b.at[3 * i + j],
                    device_id=(x, y, 1 - c), device_id_type=_MESH).wait_recv()
        for cp in first + passed:
            cp.wait_send()

    vm = pl.BlockSpec(memory_space=pltpu.VMEM)
    tab = jax.ShapeDtypeStruct((t, LANE), _F32)
    return pl.pallas_call(
        body, name="gather_weights",
        in_specs=[_ANY] * nw + [vm, vm], out_specs=[_ANY] * nw + [vm] * 3,
        out_shape=[jax.ShapeDtypeStruct(f, s.dtype) for f, s in zip(fulls, shards)] + [tab] * 3,
        scratch_shapes=[pltpu.SemaphoreType.DMA((3 * nw,)) for _ in range(4)],
        compiler_params=_cparams(None, VMEM_LIMIT),
    )(*shards, pos_f, _rope_consts())


def _assemble_w_in_t(slots):
    bw = 256
    ov = WINDOW_ROWS - WINDOW_STEP

    def body(s_ref, o_ref):
        for k in range(4):
            base = k * WINDOW_STEP
            lo = 0 if k == 0 else ov
            if k > 0:
                o_ref[base:base + ov, :] = s_ref[k - 1, WINDOW_STEP:WINDOW_ROWS, :] + s_ref[k, 0:ov, :]
            hi = WINDOW_ROWS if k == 3 else WINDOW_STEP
            o_ref[base + lo:base + hi, :] = s_ref[k, lo:hi, :]

    return pl.pallas_call(
        body, name="assemble_w_in_t", grid=(D_MODEL // bw,),
        in_specs=[pl.BlockSpec((4, WINDOW_ROWS, bw), lambda i: (0, 0, i))],
        out_specs=pl.BlockSpec((D_IN, bw), lambda i: (0, i)),
        out_shape=jax.ShapeDtypeStruct((D_IN, D_MODEL), slots.dtype),
        compiler_params=_cparams(("parallel",)),
    )(slots)


def _pair_exchange(grads, splits):
    nw = len(grads)

    def body(*refs):
        ins, outs = refs[:nw], refs[nw:2 * nw]
        send, recv = refs[2 * nw:]
        x, y, c = _place()
        copies = []
        for i in range(nw):
            cp = pltpu.make_async_remote_copy(
                src_ref=splits[i].half(ins[i], 1 - c), dst_ref=outs[i], send_sem=send.at[i], recv_sem=recv.at[i],
                device_id=(x, y, 1 - c), device_id_type=_MESH)
            cp.start()
            copies.append(cp)
        for cp in copies:
            cp.wait()

    return pl.pallas_call(
        body, name="grad_pair_exchange",
        in_specs=[_ANY] * nw, out_specs=[_ANY] * nw,
        out_shape=[jax.ShapeDtypeStruct(sp.half_shape(g.shape), g.dtype) for g, sp in zip(grads, splits)],
        scratch_shapes=[pltpu.SemaphoreType.DMA((nw,)), pltpu.SemaphoreType.DMA((nw,))],
    )(*grads)


def _row_block(rows):
    for cand in (976, 176, 256, 128):
        if rows % cand == 0:
            return cand
    return rows


def _pair_sum(g, r, split, c_arr, name):
    hr, hn = r.shape
    br = _row_block(hr)
    if split.by_rows:
        g_spec = pl.BlockSpec((br, hn), lambda i, c_ref: (i, c_ref[0]))
    else:
        g_spec = pl.BlockSpec((br, hn), lambda i, c_ref: (c_ref[0] * (hr // br) + i, 0))

    def body(c_ref, g_ref, r_ref, o_ref):
        o_ref[...] = (g_ref[...] + r_ref[...]).astype(o_ref.dtype)

    return pl.pallas_call(
        body, name=name,
        grid_spec=pltpu.PrefetchScalarGridSpec(
            num_scalar_prefetch=1, grid=(hr // br,),
            in_specs=[g_spec, pl.BlockSpec((br, hn), lambda i, c_ref: (i, 0))],
            out_specs=pl.BlockSpec((br, hn), lambda i, c_ref: (i, 0))),
        out_shape=jax.ShapeDtypeStruct(r.shape, _MX),
        compiler_params=_cparams(("parallel",)),
    )(c_arr, g, r)


def _chip_exchange(parts, splits, small):
    nw = len(parts)

    def body(*refs):
        ins, small_ref, outs, all_ref = refs[:nw], refs[nw], refs[nw + 1:2 * nw + 1], refs[2 * nw + 1]
        send, recv, lsem, s_send, s_recv = refs[2 * nw + 2:]
        x, y, c = _place()
        me = 2 * x + y
        dev = 2 * me + c
        copies = [pltpu.make_async_copy(small_ref, all_ref.at[dev], lsem.at[nw])]
        copies[0].start()
        for r in range(1, 8):
            peer = (1 - x if r & 4 else x, 1 - y if r & 2 else y, 1 - c if r & 1 else c)
            cp = pltpu.make_async_remote_copy(
                src_ref=small_ref, dst_ref=all_ref.at[dev], send_sem=s_send.at[r - 1], recv_sem=s_recv.at[r - 1],
                device_id=peer, device_id_type=_MESH)
            cp.start()
            copies.append(cp)
        for i in range(nw):
            mine = pltpu.make_async_copy(splits[i].chip_part(ins[i], me), outs[i].at[me], lsem.at[i])
            mine.start()
            copies.append(mine)
            for j, (px, py) in enumerate(_other_chips(x, y)):
                cp = pltpu.make_async_remote_copy(
                    src_ref=splits[i].chip_part(ins[i], 2 * px + py), dst_ref=outs[i].at[me],
                    send_sem=send.at[3 * i + j], recv_sem=recv.at[3 * i + j],
                    device_id=(px, py, c), device_id_type=_MESH)
                cp.start()
                copies.append(cp)
        for cp in copies:
            cp.wait()

    return pl.pallas_call(
        body, name="grad_chip_exchange",
        in_specs=[_ANY] * (nw + 1), out_specs=[_ANY] * (nw + 1),
        out_shape=[jax.ShapeDtypeStruct((4,) + sp.part_shape(p.shape), p.dtype) for p, sp in zip(parts, splits)]
        + [jax.ShapeDtypeStruct((8,) + small.shape, small.dtype)],
        scratch_shapes=[pltpu.SemaphoreType.DMA((3 * nw,)), pltpu.SemaphoreType.DMA((3 * nw,)),
                        pltpu.SemaphoreType.DMA((nw + 1,)), pltpu.SemaphoreType.DMA((7,)),
                        pltpu.SemaphoreType.DMA((7,))],
    )(*parts, small)


def _sum_chips(q, split, c_arr, name):
    _, hr, hn = q.shape
    br = _row_block(hr)
    if split.by_rows:
        out_shape = (hr, 2 * hn)
        o_spec = pl.BlockSpec((br, hn), lambda i, c_ref: (i, c_ref[0]))
    else:
        out_shape = (2 * hr, hn)
        o_spec = pl.BlockSpec((br, hn), lambda i, c_ref: (c_ref[0] * (hr // br) + i, 0))

    def body(c_ref, q_ref, o_ref):
        f = lambda k: q_ref[k].astype(_F32)
        o_ref[...] = ((f(0) + f(1)) + f(2)) + f(3)

    return pl.pallas_call(
        body, name=name,
        grid_spec=pltpu.PrefetchScalarGridSpec(
            num_scalar_prefetch=1, grid=(hr // br,),
            in_specs=[pl.BlockSpec((4, br, hn), lambda i, c_ref: (0, i, 0))], out_specs=o_spec),
        out_shape=jax.ShapeDtypeStruct(out_shape, _F32),
        compiler_params=_cparams(("parallel",)),
    )(c_arr, q)


def _pair_share(bufs, splits):
    nw = len(bufs)

    def body(*refs):
        ins, outs = refs[:nw], refs[nw:2 * nw]
        send, recv = refs[2 * nw:]
        x, y, c = _place()
        copies = []
        for i in range(nw):
            cp = pltpu.make_async_remote_copy(
                src_ref=splits[i].half(ins[i], c), dst_ref=splits[i].half(outs[i], c), send_sem=send.at[i],
                recv_sem=recv.at[i], device_id=(x, y, 1 - c), device_id_type=_MESH)
            cp.start()
            copies.append(cp)
        for cp in copies:
            cp.wait()

    return pl.pallas_call(
        body, name="grad_pair_share",
        in_specs=[_ANY] * nw, out_specs=[_ANY] * nw,
        out_shape=[jax.ShapeDtypeStruct(b.shape, b.dtype) for b in bufs],
        input_output_aliases={i: i for i in range(nw)},
        scratch_shapes=[pltpu.SemaphoreType.DMA((nw,)), pltpu.SemaphoreType.DMA((nw,))],
    )(*bufs)


def _sum_devices(parts):
    def body(p_ref, tot_ref):
        acc = p_ref[0]
        for d in range(1, 8):
            acc = acc + p_ref[d]
        tot_ref[...] = acc

    vm = pl.BlockSpec(memory_space=pltpu.VMEM)
    return pl.pallas_call(
        body, name="small_sum", in_specs=[vm], out_specs=vm,
        out_shape=jax.ShapeDtypeStruct(parts.shape[1:], parts.dtype),
    )(parts)


def _adamw(w, g, m, v, name):
    lead = w.shape[0] != 1
    r, n = (w.shape[0], w.shape[2]) if lead else w.shape[1:]
    br = r
    for cand in (256, 244, 128):
        if r > cand and r % cand == 0:
            br = cand
            break

    def body(w_ref, g_ref, m_ref, v_ref, d_ref, nm_ref, nv_ref):
        gv = g_ref[...]
        m2 = ADAM_B1 * m_ref[...] + (1.0 - ADAM_B1) * gv
        v2 = ADAM_B2 * v_ref[...] + (1.0 - ADAM_B2) * (gv * gv)
        m_hat = m2 / (1.0 - ADAM_B1 ** ADAM_STEP)
        v_hat = v2 / (1.0 - ADAM_B2 ** ADAM_STEP)
        d_ref[...] = -ADAM_LR * (m_hat / (jnp.sqrt(v_hat) + ADAM_EPS) + ADAM_WD * w_ref[...])
        nm_ref[...] = m2
        nv_ref[...] = v2

    blk = pl.BlockSpec((br, 1, n), lambda i: (i, 0, 0)) if lead else pl.BlockSpec((None, br, n), lambda i: (0, i, 0))
    shp = jax.ShapeDtypeStruct(w.shape, _F32)
    return pl.pallas_call(
        body, name=name, grid=(r // br,),
        in_specs=[blk] * 4, out_specs=[blk] * 3, out_shape=[shp] * 3,
        compiler_params=_cparams(("parallel",)),
    )(w, g, m, v)


def kernel(x, positions, g_in, w_in, w_alpha_up, b_alpha, attn_sinks, g_gla_norm, w_out_a, w_out_b, w_o, g_final, loss_target, m_g_in, m_w_in, m_w_alpha_up, m_b_alpha, m_attn_sinks, m_g_gla_norm, m_w_out_a, m_w_out_b, m_w_o, m_g_final, v_g_in, v_w_in, v_w_alpha_up, v_b_alpha, v_attn_sinks, v_g_gla_norm, v_w_out_a, v_w_out_b, v_w_o, v_g_final):
    nseq, seq, _ = x.shape
    t = nseq * seq
    cx, cy, cc = _place()
    chip = 2 * cx + cy
    c_arr = jnp.reshape(cc, (1,)).astype(jnp.int32)

    tr = lambda w: jnp.transpose(w, (2, 0, 1))
    w_in_t = tr(w_in).reshape(SHARD, D_MODEL).astype(_MX)
    pad = WINDOW_ROWS - SHARD
    window = lax.switch(chip, [lambda w, k=k: jnp.pad(w, ((4 * k, pad - 4 * k), (0, 0))) for k in range(4)], w_in_t)
    shards = [window, w_out_a[0].astype(_MX), w_out_b[0].astype(_MX), w_o[0].astype(_MX), w_alpha_up[0].astype(_MX)]
    splits = [SPLIT_W_IN_T, SPLIT_W_OUT, SPLIT_W_OUT, SPLIT_W_O, None]
    fulls = [(4, WINDOW_ROWS, D_MODEL), (A_WIDTH, D_MODEL), (B_WIDTH, D_MODEL), (D_MODEL, D_MODEL),
             (4, B_GATE_RANK, B_KEY_WIDTH // 4)]
    pos_f = positions.astype(_F32).reshape(t, 1)
    win_g, wa, wb, wo, wup_g, cos, sa, sb = _gather_weights(shards, splits, fulls, pos_f)
    nsh = D_MODEL // 4
    win_g = lax.dynamic_update_slice(win_g, window[None], (chip, 0, 0))
    wa = lax.dynamic_update_slice(wa, shards[1], (0, nsh * chip))
    wb = lax.dynamic_update_slice(wb, shards[2], (0, nsh * chip))
    wo = lax.dynamic_update_slice(wo, shards[3], (nsh * chip, 0))
    wup_g = lax.dynamic_update_slice(wup_g, shards[4][None], (chip, 0, 0))
    wt = _assemble_w_in_t(win_g)
    wup = jnp.concatenate([jnp.transpose(wup_g, (1, 0, 2)).reshape(B_GATE_RANK, B_KEY_WIDTH),
                           jnp.zeros((LANE - B_GATE_RANK, B_KEY_WIDTH), _MX)], axis=0)

    x2 = x.reshape(t, D_MODEL)
    tgt = loss_target.reshape(t, D_MODEL)
    sinks = attn_sinks.reshape(A_HEADS)
    gf = g_final.reshape(1, D_MODEL)

    h, qkv, za, qkb, vb, zb, alr, ga, gb = _in_proj(x2, g_in, wt, cos, sa, sb)
    oa = _attn_fwd(qkv, za, sinks, nseq)
    ob, oraw, sst = _gla_fwd(qkb, vb, zb, alr, wup, b_alpha, g_gla_norm, nseq)

    dh2, doa, dob, dga, dgb, dwa, dwb, dwo, dgf, lossv = _merge_loss(oa, ob, ga, gb, x2, tgt, wa, wb, wo, gf)

    dqkv, dza, dsink = _attn_bwd(qkv, za, doa, sinks, cos, sa, sb, nseq)
    dqkb, dvb, dzb, dalr, dwup, dba, dgn = _gla_bwd(qkb, vb, zb, alr, oraw, dob, sst, wup, b_alpha, g_gla_norm, nseq)
    dpieces = [dqkv, dza, dqkb, dvb, dzb, dalr, dga, dgb]
    grad_x2, dgin = _in_proj_bwd_x(dpieces, wt, x2, dh2, g_in)
    dwin_t = _in_proj_bwd_w(h, dpieces)

    grads = [dwin_t, dwa, dwb, dwo]
    gsplits = [SPLIT_W_IN_T, SPLIT_W_OUT, SPLIT_W_OUT, SPLIT_W_O]
    names = ("w_in", "w_out_a", "w_out_b", "w_o")
    from_sibling = _pair_exchange(grads, gsplits)
    pair_sums = [_pair_sum(g, r, sp, c_arr, "pair_sum_" + nm)
                 for g, r, sp, nm in zip(grads, from_sibling, gsplits, names)]
    small = jnp.concatenate([
        dgin, dgf, dgn, dba,
        jnp.pad(dsink[:, 0].reshape(1, A_HEADS), ((0, 0), (0, LANE - A_HEADS))),
        jnp.pad(jnp.sum(lossv, axis=1, keepdims=True), ((0, 0), (0, LANE - 1))),
        dwup[:B_GATE_RANK].reshape(1, B_GATE_RANK * B_KEY_WIDTH)], axis=1)
    *from_chips, small_parts = _chip_exchange(pair_sums, gsplits, small.reshape(-1, LANE))
    reduced = [_sum_chips(q, sp, c_arr, "chip_sum_" + nm) for q, sp, nm in zip(from_chips, gsplits, names)]
    g_window, g_wa, g_wb, g_wo = _pair_share(reduced, gsplits)
    g_win_t = lax.switch(chip, [lambda w, k=k: w[4 * k:4 * k + SHARD].reshape(SHARD, 1, D_MODEL) for k in range(4)],
                         g_window)
    tot = _sum_devices(small_parts).reshape(1, -1)
    o = 0
    def take(n):
        nonlocal o
        o += n
        return tot[:, o - n:o]
    g_gin, g_gf, g_gn, g_ba = take(D_MODEL), take(D_MODEL), take(B_WIDTH), take(B_KEY_WIDTH)
    g_sink = take(LANE)[:, :A_HEADS]
    loss = take(LANE)[0, 0]
    g_wup_full = take(B_GATE_RANK * B_KEY_WIDTH).reshape(B_GATE_RANK, B_KEY_WIDTH)
    nup = B_KEY_WIDTH // 4
    g_wup = lax.dynamic_slice(g_wup_full, (0, chip * nup), (B_GATE_RANK, nup))

    def pack(*parts):
        return jnp.concatenate([p.reshape(1, -1) for p in parts], axis=1)

    sm_w = pack(g_in, g_final, g_gla_norm, b_alpha, attn_sinks, w_alpha_up)
    sm_g = pack(g_gin, g_gf, g_gn, g_ba, g_sink, g_wup)
    sm_m = pack(m_g_in, m_g_final, m_g_gla_norm, m_b_alpha, m_attn_sinks, m_w_alpha_up)
    sm_v = pack(v_g_in, v_g_final, v_g_gla_norm, v_b_alpha, v_attn_sinks, v_w_alpha_up)
    sm_out = [p[0] for p in _adamw(sm_w[None], sm_g[None], sm_m[None], sm_v[None], "adamw_small")]

    def unpack(p):
        sizes = (D_MODEL, D_MODEL, B_WIDTH, B_KEY_WIDTH, A_HEADS, B_GATE_RANK * nup)
        outs, at = [], 0
        for s in sizes:
            outs.append(p[:, at:at + s])
            at += s
        gi, gfin, gnn, ba, sk, wu = outs
        return dict(g_in=gi, g_final=gfin.reshape(D_MODEL), g_gla_norm=gnn, b_alpha=ba, attn_sinks=sk,
                    w_alpha_up=wu.reshape(1, B_GATE_RANK, nup))

    untr = lambda a: jnp.transpose(a, (1, 2, 0))
    big = dict(w_in=tuple(untr(a) for a in (g_win_t,) + tuple(_adamw(tr(w_in), g_win_t, tr(m_w_in), tr(v_w_in), "adamw_w_in"))))
    for nm, w, g, m, v in (("w_out_a", w_out_a, g_wa, m_w_out_a, v_w_out_a),
                           ("w_out_b", w_out_b, g_wb, m_w_out_b, v_w_out_b), ("w_o", w_o, g_wo, m_w_o, v_w_o)):
        big[nm] = (g[None],) + tuple(_adamw(w, g[None], m, v, "adamw_" + nm))

    order = ("g_in", "w_in", "w_alpha_up", "b_alpha", "attn_sinks", "g_gla_norm", "w_out_a", "w_out_b", "w_o", "g_final")
    small_sets = [unpack(sm_g)] + [unpack(p) for p in sm_out]
    outs = []
    for kind in range(4):
        for nm in order:
            outs.append(big[nm][kind] if nm in big else small_sets[kind][nm])
    return (loss, grad_x2.reshape(x.shape), *outs)
```

```python
import math
from typing import NamedTuple

import numpy as np
import jax
import jax.numpy as jnp
from jax import lax
from jax.experimental import pallas as pl
from jax.experimental.pallas import tpu as pltpu

D_MODEL = 1024
A_HEADS, A_KV_HEADS, A_HEAD_DIM = 8, 2, 64
A_GROUP = A_HEADS // A_KV_HEADS
A_WIDTH, A_KV_WIDTH = 512, 128
BLOCK = 128
ROPE_THETA = 500000.0
ROPE_DIM = 16
B_HEADS, B_KEY_DIM, B_VAL_DIM = 4, 64, 128
B_KEY_WIDTH, B_WIDTH = 256, 512
B_GATE_RANK = 16
B_GATE_TEMP = 16.0
B_CHUNK = 64
NORM_EPS = 1e-6
NEG_BIG = -1e30
D_IN = 4880

ADAM_LR, ADAM_B1, ADAM_B2, ADAM_EPS, ADAM_WD, ADAM_STEP = 0.001, 0.9, 0.999, 1e-08, 0.01, 10

LANE = 128
ALR_AT = 2816
PIECES = (("qkv", 0, 768), ("za", 768, 1280), ("qkb", 1280, 1792), ("vb", 1792, 2304),
          ("zb", 2304, 2816), ("alr", ALR_AT, ALR_AT + LANE), ("ga", 2832, 3856), ("gb", 3856, 4880))
SHARD = D_IN // 4
WINDOW_STEP = 1216
WINDOW_ROWS = 1232

GLA_BLOCK = 256
MERGE_SLAB = 16
VMEM_LIMIT = 56 * 1024 * 1024

_F32 = jnp.float32
_MX = jnp.bfloat16
_ST = jnp.bfloat16

_MESH = pl.DeviceIdType.MESH
_ANY = pl.BlockSpec(memory_space=pl.ANY)


def _cparams(sem=None, vmem=None):
    return pltpu.CompilerParams(dimension_semantics=sem, vmem_limit_bytes=vmem)


def _dot(a, b):
    return jnp.dot(a.astype(_MX), b.astype(_MX), preferred_element_type=_F32)


def _dot_nt(a, b):
    return lax.dot_general(a.astype(_MX), b.astype(_MX), (((1,), (1,)), ((), ())),
                           preferred_element_type=_F32)


def _dot_tn(a, b):
    return lax.dot_general(a.astype(_MX), b.astype(_MX), (((0,), (0,)), ((), ())),
                           preferred_element_type=_F32)


def _dot_ones(ones_mat, v):
    o = ones_mat.astype(jnp.bfloat16)
    v0 = v.astype(jnp.bfloat16)
    r1 = v - v0.astype(_F32)
    v1 = r1.astype(jnp.bfloat16)
    v2 = (r1 - v1.astype(_F32)).astype(jnp.bfloat16)
    d = lambda t: jnp.dot(o, t, preferred_element_type=_F32)
    return d(v0) + d(v1) + d(v2)


def _sigmoid(x):
    return 0.5 * jnp.tanh(0.5 * x) + 0.5


def _log_sigmoid(x):
    return jnp.minimum(x, 0.0) - jnp.log(1.0 + jnp.exp(-jnp.abs(x)))


def _lane_tile(t, width):
    reps = width // t.shape[1]
    return t if reps == 1 else jnp.tile(t, (1, reps))


def _rope(t, cos, sa, sb, sign):
    w = t.shape[1]
    rot = pltpu.roll(t, w - 8, 1) * _lane_tile(sa, w) + pltpu.roll(t, 8, 1) * _lane_tile(sb, w)
    return t * _lane_tile(cos, w) + sign * rot


def _rms_bwd(dy_g, n, r):
    return r * (dy_g - n * jnp.mean(dy_g * n, axis=-1, keepdims=True))


ROPE_ROWS = 256


def _rope_consts():
    lane = np.arange(LANE) % A_HEAD_DIM
    half = ROPE_DIM // 2
    inv = np.exp((np.float32(-math.log(ROPE_THETA)) * np.arange(half, dtype=np.float32)) * np.float32(2.0 / ROPE_DIM))
    consts = np.zeros((8, LANE), np.float32)
    consts[0] = np.where(lane < ROPE_DIM, inv[lane % half], 0.0)
    consts[1] = np.where(lane < half, -1.0, 0.0)
    consts[2] = np.where((lane >= half) & (lane < ROPE_DIM), 1.0, 0.0)
    return jnp.asarray(consts)


def _rope_tables_into(pos_ref, c_ref, cos_ref, sa_ref, sb_ref):
    def rows_of(b, carry):
        rows = pl.ds(pl.multiple_of(b * ROPE_ROWS, ROPE_ROWS), ROPE_ROWS)
        ang = pos_ref[rows, :] * c_ref[0:1, :]
        s = jnp.sin(ang)
        cos_ref[rows, :] = jnp.cos(ang)
        sa_ref[rows, :] = s * c_ref[1:2, :]
        sb_ref[rows, :] = s * c_ref[2:3, :]
        return carry

    lax.fori_loop(0, pos_ref.shape[0] // ROPE_ROWS, rows_of, 0)


def _in_proj(x2, g_in, wt, cos, sa, sb):
    t = x2.shape[0]
    tm = min(t, 512)

    def body(x_ref, g_ref, w_ref, cos_ref, sa_ref, sb_ref, h_ref, qkv_ref, za_ref, qkb_ref,
             vb_ref, zb_ref, alr_ref, ga_ref, gb_ref):
        xv = x_ref[...]
        r = lax.rsqrt(jnp.mean(xv * xv, axis=-1, keepdims=True) + NORM_EPS)
        h = (xv * r * g_ref[...]).astype(_MX)
        h_ref[...] = h.astype(_ST)
        outs = dict(za=za_ref, qkb=qkb_ref, vb=vb_ref, zb=zb_ref, alr=alr_ref, ga=ga_ref, gb=gb_ref)
        for name, a, b in PIECES:
            p = _dot_nt(h, w_ref[a:b, :])
            if name == "qkv":
                c, s1, s2 = cos_ref[...], sa_ref[...], sb_ref[...]
                qkv_ref[:, 0:512] = _rope(p[:, 0:512], c, s1, s2, 1.0).astype(_ST)
                qkv_ref[:, 512:640] = _rope(p[:, 512:640], c, s1, s2, 1.0).astype(_ST)
                qkv_ref[:, 640:768] = p[:, 640:768].astype(_ST)
            else:
                outs[name][...] = p.astype(outs[name].dtype)

    rows = lambda w: pl.BlockSpec((tm, w), lambda i: (i, 0))
    shp = lambda name, w: jax.ShapeDtypeStruct((t, w), _F32 if name == "qkb" else _ST)
    widths = [D_MODEL] + [b - a for _, a, b in PIECES]
    return pl.pallas_call(
        body, name="in_proj", grid=(t // tm,),
        in_specs=[rows(D_MODEL), pl.BlockSpec((1, D_MODEL), lambda i: (0, 0)),
                  pl.BlockSpec((D_IN, D_MODEL), lambda i: (0, 0), pipeline_mode=pl.Buffered(1)),
                  rows(LANE), rows(LANE), rows(LANE)],
        out_specs=[rows(w) for w in widths],
        out_shape=[shp(n, w) for n, w in zip(["h"] + [p[0] for p in PIECES], widths)],
        compiler_params=_cparams(("parallel",), VMEM_LIMIT),
    )(x2, g_in, wt, cos, sa, sb)


def _attn_operands(k_prev, k_cur, v_prev, v_cur, want_bwd):
    kf = jnp.concatenate([k_prev, k_cur], axis=0).astype(_F32) * (A_HEAD_DIM ** -0.5)
    vf = jnp.concatenate([v_prev, v_cur], axis=0).astype(_F32)
    lo = lax.broadcasted_iota(jnp.int32, (1, LANE), 1) < 64

    def on_lanes(a):
        sw = pltpu.roll(a, 64, 1)
        z = jnp.zeros_like(a)
        return [[jnp.where(lo, a, z).astype(_MX), jnp.where(lo, z, sw).astype(_MX)],
                [jnp.where(lo, sw, z).astype(_MX), jnp.where(lo, z, a).astype(_MX)]]

    def on_rows(a):
        at = a.T.astype(_MX)
        z = jnp.zeros((64, at.shape[1]), _MX)
        top, bot = at[0:64], at[64:128]
        return [[jnp.concatenate([top, z], axis=0), jnp.concatenate([z, top], axis=0)],
                [jnp.concatenate([bot, z], axis=0), jnp.concatenate([z, bot], axis=0)]]

    ops = dict(k_lanes=on_lanes(kf), v_rows=on_rows(vf), lo=lo)
    if want_bwd:
        ops.update(v_lanes=on_lanes(vf), k_rows=on_rows(kf))
    return ops


def _attn_valid(n):
    kj = lax.broadcasted_iota(jnp.int32, (2 * BLOCK, 2 * BLOCK), 0) - BLOCK
    qi = lax.broadcasted_iota(jnp.int32, (2 * BLOCK, 2 * BLOCK), 1) & (BLOCK - 1)
    return (kj <= qi) & (qi - kj < BLOCK) & ((n > 0) | (kj >= 0))


def _attn_sinks(sink_ref, h_a, h_b):
    first = lax.broadcasted_iota(jnp.int32, (1, 2 * BLOCK), 1) < BLOCK
    return jnp.where(first, sink_ref[h_a], sink_ref[h_b])


def _attn_softmax_t(k_lanes, q_pair, valid, sink):
    s = jnp.where(valid, _dot_nt(k_lanes, q_pair), NEG_BIG)
    m = jnp.maximum(jnp.max(s, axis=0, keepdims=True), sink)
    e = jnp.exp(s - m)
    e_sink = jnp.exp(sink - m)
    inv = 1.0 / (jnp.sum(e, axis=0, keepdims=True) + e_sink)
    return e, e_sink, inv


ATTN_TILE = 4


def _attn_kv(qkv_ref, kvp_ref, j):
    rows = slice(j * BLOCK, (j + 1) * BLOCK)
    if j == 0:
        k_prev, v_prev = kvp_ref[:, 0:128], kvp_ref[:, 128:256]
    else:
        before = slice((j - 1) * BLOCK, j * BLOCK)
        k_prev, v_prev = qkv_ref[before, 512:640], qkv_ref[before, 640:768]
    return k_prev, qkv_ref[rows, 512:640], v_prev, qkv_ref[rows, 640:768]


def _attn_fwd(qkv, za, sinks, nseq):
    t = qkv.shape[0]
    tile = ATTN_TILE * BLOCK
    nt = t // nseq // tile

    def body(sink_ref, qkv_ref, kvp_ref, za_ref, oa_ref):
        for j in range(ATTN_TILE):
            rows = slice(j * BLOCK, (j + 1) * BLOCK)
            ops = _attn_operands(*_attn_kv(qkv_ref, kvp_ref, j), False)
            valid = _attn_valid(ATTN_TILE * pl.program_id(1) + j)[:, 0:BLOCK]
            for pr in range(A_HEADS // 2):
                lanes = slice(pr * LANE, (pr + 1) * LANE)
                g = pr // (A_GROUP // 2)
                q_pair = qkv_ref[rows, lanes]
                ot = None
                for half in range(2):
                    e, _, inv = _attn_softmax_t(ops["k_lanes"][g][half], q_pair, valid, sink_ref[2 * pr + half])
                    part = _dot(ops["v_rows"][g][half], e) * inv
                    ot = part if ot is None else ot + part
                z = za_ref[rows, lanes].astype(_F32)
                oa_ref[rows, lanes] = (ot.T * (z * _sigmoid(z))).astype(_ST)

    cur = lambda w: pl.BlockSpec((tile, w), lambda s, n: (s * nt + n, 0))
    return pl.pallas_call(
        body, name="attn_fwd", grid=(nseq, nt),
        in_specs=[pl.BlockSpec(memory_space=pltpu.SMEM), cur(768),
                  pl.BlockSpec((BLOCK, 256), lambda s, n: (ATTN_TILE * (s * nt + n) - jnp.minimum(n, 1), 2)),
                  cur(512)],
        out_specs=cur(512), out_shape=jax.ShapeDtypeStruct((t, A_WIDTH), _ST),
        compiler_params=_cparams(("parallel", "arbitrary")),
    )(sinks, qkv, qkv, za)


def _attn_bwd(qkv, za, doa, sinks, cos, sa, sb, nseq):
    t = qkv.shape[0]
    tile = ATTN_TILE * BLOCK
    nt = t // nseq // tile

    def body(sink_ref, qkv_ref, kvp_ref, za_ref, doa_ref, cos_ref, sa_ref, sb_ref,
             dqkv_ref, dza_ref, dsink_ref, ck_ref, cv_ref):
        s_id, i = pl.program_id(0), pl.program_id(1)

        @pl.when((s_id == 0) & (i == 0))
        def _():
            dsink_ref[...] = jnp.zeros_like(dsink_ref)

        @pl.when(i == 0)
        def _():
            ck_ref[...] = jnp.zeros_like(ck_ref)
            cv_ref[...] = jnp.zeros_like(cv_ref)

        carry_k, carry_v = ck_ref[...], cv_ref[...]
        for j in reversed(range(ATTN_TILE)):
            rows = slice(j * BLOCK, (j + 1) * BLOCK)
            ops = _attn_operands(*_attn_kv(qkv_ref, kvp_ref, j), True)
            lo = ops["lo"]
            valid = _attn_valid(ATTN_TILE * (nt - 1 - i) + j)
            dk_acc, dv_acc, dq_pairs = [], [], []
            for g in range(A_KV_HEADS):
                pairs = [slice((2 * g + p) * LANE, (2 * g + p + 1) * LANE) for p in range(2)]
                q_both = jnp.concatenate([qkv_ref[rows, p] for p in pairs], axis=0)
                q_f = q_both.astype(_F32)
                z = [za_ref[rows, p].astype(_F32) for p in pairs]
                sz = [_sigmoid(v) for v in z]
                d_oa = [doa_ref[rows, p].astype(_F32) for p in pairs]
                d_att = jnp.concatenate([d_oa[p] * (z[p] * sz[p]) for p in range(2)], axis=0)
                zero = jnp.zeros_like(d_att)
                ot, dqt, ds_all, pn_all, qz_all, daz_all = None, None, [], [], [], []
                for half in range(2):
                    heads = (4 * g + half, 4 * g + 2 + half)
                    e, e_sink, inv = _attn_softmax_t(ops["k_lanes"][g][half], q_both, valid,
                                                     _attn_sinks(sink_ref, *heads))
                    pn = e * inv
                    dpt = _dot_nt(ops["v_lanes"][g][half], d_att)
                    delta = jnp.sum(pn * dpt, axis=0, keepdims=True)
                    ds = (pn * (dpt - delta)).astype(_MX)
                    pn = pn.astype(_MX)
                    d_sink = e_sink * inv * delta
                    for p, h in enumerate(heads):
                        dsink_ref[h:h + 1, :] = dsink_ref[h:h + 1, :] - jnp.sum(d_sink[:, p * BLOCK:(p + 1) * BLOCK])
                    o_part = _dot(ops["v_rows"][g][half], pn)
                    dq_part = _dot(ops["k_rows"][g][half], ds)
                    ot = o_part if ot is None else ot + o_part
                    dqt = dq_part if dqt is None else dqt + dq_part
                    mine = lo if half == 0 else jnp.logical_not(lo)
                    ds_all.append(ds)
                    pn_all.append(pn)
                    qz_all.append(jnp.where(mine, q_f, zero).astype(_MX))
                    daz_all.append(jnp.where(mine, d_att, zero).astype(_MX))
                dk_acc.append(_dot(jnp.concatenate(ds_all, axis=1), jnp.concatenate(qz_all, axis=0)))
                dv_acc.append(_dot(jnp.concatenate(pn_all, axis=1), jnp.concatenate(daz_all, axis=0)))
                for p, lanes in enumerate(pairs):
                    cols = slice(p * BLOCK, (p + 1) * BLOCK)
                    dza_ref[rows, lanes] = (d_oa[p] * ot[:, cols].T * (sz[p] * (1.0 + z[p] * (1.0 - sz[p])))).astype(_ST)
                    dq_pairs.append(dqt[:, cols].T)

            def fold(acc, scale):
                both = [a + pltpu.roll(a, 64, 1) for a in acc]
                return jnp.where(lo, both[0], both[1]) * scale

            dk_full = fold(dk_acc, A_HEAD_DIM ** -0.5)
            dv_full = fold(dv_acc, 1.0)
            dk_cur, dv_cur = dk_full[BLOCK:] + carry_k, dv_full[BLOCK:] + carry_v
            carry_k, carry_v = dk_full[:BLOCK], dv_full[:BLOCK]
            c, s1, s2 = cos_ref[rows, :], sa_ref[rows, :], sb_ref[rows, :]
            dqkv_ref[rows, 0:512] = _rope(jnp.concatenate(dq_pairs, axis=1), c, s1, s2, -1.0).astype(_ST)
            dqkv_ref[rows, 512:640] = _rope(dk_cur, c, s1, s2, -1.0).astype(_ST)
            dqkv_ref[rows, 640:768] = dv_cur.astype(_ST)
        ck_ref[...] = carry_k
        cv_ref[...] = carry_v

    cur = lambda w: pl.BlockSpec((tile, w), lambda s, i: (s * nt + nt - 1 - i, 0))
    return pl.pallas_call(
        body, name="attn_bwd", grid=(nseq, nt),
        in_specs=[pl.BlockSpec(memory_space=pltpu.SMEM), cur(768),
                  pl.BlockSpec((BLOCK, 256),
                               lambda s, i: (ATTN_TILE * (s * nt + nt - 1 - i) - jnp.minimum(nt - 1 - i, 1), 2)),
                  cur(512), cur(512), cur(LANE), cur(LANE), cur(LANE)],
        out_specs=[cur(768), cur(512), pl.BlockSpec((8, LANE), lambda s, i: (0, 0))],
        out_shape=[jax.ShapeDtypeStruct((t, 768), _ST), jax.ShapeDtypeStruct((t, 512), _ST),
                   jax.ShapeDtypeStruct((8, LANE), _F32)],
        scratch_shapes=[pltpu.VMEM((BLOCK, A_KV_WIDTH), _F32), pltpu.VMEM((BLOCK, A_KV_WIDTH), _F32)],
        compiler_params=_cparams(("arbitrary", "arbitrary")),
    )(sinks, qkv, qkv, za, doa, cos, sa, sb)


def _gla_chunk_terms(la, qkb_ref, r0):
    g = la[r0:r0 + B_CHUNK, :]
    ri = lax.broadcasted_iota(jnp.int32, (B_CHUNK, B_CHUNK), 0)
    ci = lax.broadcasted_iota(jnp.int32, (B_CHUNK, B_CHUNK), 1)
    cum = _dot_ones((ri >= ci).astype(_F32), g)
    last = cum[B_CHUNK - 1:B_CHUNK, :]
    mid = cum[B_CHUNK // 2 - 1:B_CHUNK // 2, :]
    q = qkb_ref[r0:r0 + B_CHUNK, 0:B_KEY_WIDTH].astype(_F32) * (B_KEY_DIM ** -0.5)
    k = qkb_ref[r0:r0 + B_CHUNK, B_KEY_WIDTH:2 * B_KEY_WIDTH].astype(_F32)
    e_q, e_k, e_l, e_c = jnp.exp(cum - mid), jnp.exp(mid - cum), jnp.exp(last - cum), jnp.exp(cum)
    dec_col = jnp.exp(jnp.sum(g.T, axis=1, keepdims=True))
    return dict(qm=q * e_q, km=k * e_k, kl=k * e_l, qc=q * e_c, e_q=e_q, e_k=e_k, e_l=e_l, e_c=e_c,
                dec_col=dec_col, dec_row=jnp.exp(last), causal=ri >= ci, ri=ri)


def _gate_logits(alr_ref, wup_ref, b_ref):
    return _dot(alr_ref[...], wup_ref[...]) + b_ref[...]


def _gla_fwd(qkb, vb, zb, alr, wup, b_alpha, gn, nseq):
    t = qkb.shape[0]
    tb = min(GLA_BLOCK, t // nseq)
    nblk = t // nseq // tb
    cpb = tb // B_CHUNK

    def body(qkb_ref, vb_ref, zb_ref, alr_ref, wup_ref, b_ref, gn_ref, ob_ref, oraw_ref, sst_ref, s_ref):
        @pl.when(pl.program_id(1) == 0)
        def _():
            s_ref[...] = jnp.zeros_like(s_ref)

        la = _log_sigmoid(_gate_logits(alr_ref, wup_ref, b_ref)) * (1.0 / B_GATE_TEMP)
        terms = [_gla_chunk_terms(la, qkb_ref, c * B_CHUNK) for c in range(cpb)]
        o_intra, inc = {}, {}
        for c, tm in enumerate(terms):
            for h in range(B_HEADS):
                kl_, vl_ = slice(h * 64, (h + 1) * 64), slice(h * 128, (h + 1) * 128)
                v = vb_ref[c * B_CHUNK:(c + 1) * B_CHUNK, vl_]
                a = jnp.where(tm["causal"], _dot_nt(tm["qm"][:, kl_], tm["km"][:, kl_]), 0.0)
                o_intra[c, h] = _dot(a, v)
                inc[c, h] = _dot_tn(tm["kl"][:, kl_], v)
        o_heads = {}
        for h in range(B_HEADS):
            kl_ = slice(h * 64, (h + 1) * 64)
            st = s_ref[kl_, :]
            for c, tm in enumerate(terms):
                sst_ref[c, kl_, :] = st
                o_heads[c, h] = o_intra[c, h] + _dot(tm["qc"][:, kl_], st)
                st = tm["dec_col"][kl_, :] * st + inc[c, h]
            s_ref[kl_, :] = st
        o = jnp.concatenate([jnp.concatenate([o_heads[c, h] for h in range(B_HEADS)], axis=1)
                             for c in range(cpb)], axis=0)
        oraw_ref[...] = o
        z = zb_ref[...].astype(_F32)
        gate = z * _sigmoid(z)
        for h in range(B_HEADS):
            vl_ = slice(h * 128, (h + 1) * 128)
            oh = o[:, vl_]
            r = lax.rsqrt(jnp.mean(oh * oh, axis=-1, keepdims=True) + NORM_EPS)
            ob_ref[:, vl_] = ((oh * r) * gn_ref[:, vl_] * gate[:, vl_]).astype(_ST)

    rows = lambda w: pl.BlockSpec((tb, w), lambda s, i: (s * nblk + i, 0))
    full = lambda a, b: pl.BlockSpec((a, b), lambda s, i: (0, 0))
    return pl.pallas_call(
        body, name="gla_fwd", grid=(nseq, nblk),
        in_specs=[rows(512), rows(512), rows(512), rows(LANE), full(LANE, B_KEY_WIDTH),
                  full(1, B_KEY_WIDTH), full(1, B_WIDTH)],
        out_specs=[rows(512), rows(512),
                   pl.BlockSpec((cpb, B_KEY_WIDTH, B_VAL_DIM), lambda s, i: (s * nblk + i, 0, 0))],
        out_shape=[jax.ShapeDtypeStruct((t, B_WIDTH), _ST), jax.ShapeDtypeStruct((t, B_WIDTH), _F32),
                   jax.ShapeDtypeStruct((t // B_CHUNK, B_KEY_WIDTH, B_VAL_DIM), _F32)],
        scratch_shapes=[pltpu.VMEM((B_KEY_WIDTH, B_VAL_DIM), _F32)],
        compiler_params=_cparams(("parallel", "arbitrary")),
    )(qkb, vb, zb, alr, wup, b_alpha, gn)


def _gla_bwd(qkb, vb, zb, alr, oraw, dob, sst, wup, b_alpha, gn, nseq):
    t = qkb.shape[0]
    tb = min(GLA_BLOCK, t // nseq)
    nblk = t // nseq // tb
    cpb = tb // B_CHUNK

    def body(qkb_ref, vb_ref, zb_ref, alr_ref, oraw_ref, dob_ref, sst_ref, wup_ref, b_ref, gn_ref,
             dqkb_ref, dvb_ref, dzb_ref, dalr_ref, dwup_ref, db_ref, dgn_ref, ds_ref):
        s_id, i = pl.program_id(0), pl.program_id(1)

        @pl.when((s_id == 0) & (i == 0))
        def _():
            dwup_ref[...] = jnp.zeros_like(dwup_ref)
            db_ref[...] = jnp.zeros_like(db_ref)
            dgn_ref[...] = jnp.zeros_like(dgn_ref)

        @pl.when(i == 0)
        def _():
            ds_ref[...] = jnp.zeros_like(ds_ref)

        a_pre = _gate_logits(alr_ref, wup_ref, b_ref)
        la = _log_sigmoid(a_pre) * (1.0 / B_GATE_TEMP)

        z = zb_ref[...].astype(_F32)
        sz = _sigmoid(z)
        d_ob = dob_ref[...].astype(_F32)
        tg = d_ob * (z * sz)
        dsilu = sz * (1.0 + z * (1.0 - sz))
        do_cols, dgn_cols = [], []
        for h in range(B_HEADS):
            vl_ = slice(h * 128, (h + 1) * 128)
            oh = oraw_ref[:, vl_].astype(_F32)
            r = lax.rsqrt(jnp.mean(oh * oh, axis=-1, keepdims=True) + NORM_EPS)
            on = oh * r
            gnh = gn_ref[:, vl_]
            dzb_ref[:, vl_] = (d_ob[:, vl_] * (on * gnh) * dsilu[:, vl_]).astype(_ST)
            dgn_cols.append(jnp.sum(tg[:, vl_] * on, axis=0, keepdims=True))
            do_cols.append(_rms_bwd(tg[:, vl_] * gnh, on, r))
        dgn_ref[...] = dgn_ref[...] + jnp.concatenate(dgn_cols, axis=1)
        d_o = jnp.concatenate(do_cols, axis=1)

        ri = lax.broadcasted_iota(jnp.int32, (tb, tb), 0)
        ci = lax.broadcasted_iota(jnp.int32, (tb, tb), 1)
        same = (ri // B_CHUNK) == (ci // B_CHUNK)
        low = same & (ri >= ci)
        upto_mid = same & ((ci % B_CHUNK) < B_CHUNK // 2)
        sums = _dot_ones(jnp.concatenate([m.astype(_F32) for m in (low, same, upto_mid)], axis=0), la)
        cum, last, mid = sums[0:tb], sums[tb:2 * tb], sums[2 * tb:3 * tb]
        e_q, e_k, e_l, e_c = jnp.exp(cum - mid), jnp.exp(mid - cum), jnp.exp(last - cum), jnp.exp(cum)
        q = qkb_ref[:, 0:B_KEY_WIDTH] * (B_KEY_DIM ** -0.5)
        k = qkb_ref[:, B_KEY_WIDTH:2 * B_KEY_WIDTH]
        qm, km, kl, qc = q * e_q, k * e_k, k * e_l, q * e_c
        lane_head = lax.broadcasted_iota(jnp.int32, (1, B_KEY_WIDTH), 1) // B_KEY_DIM
        d_o_mx = d_o.astype(_MX)

        def on_diagonal(st):
            z = jnp.zeros((B_KEY_DIM, B_VAL_DIM), st.dtype)
            return jnp.concatenate([jnp.concatenate(
                [st[h * B_KEY_DIM:(h + 1) * B_KEY_DIM] if g == h else z for g in range(B_HEADS)], axis=1)
                for h in range(B_HEADS)], axis=0)

        def diagonal_of(full):
            return jnp.concatenate([full[h * B_KEY_DIM:(h + 1) * B_KEY_DIM, h * B_VAL_DIM:(h + 1) * B_VAL_DIM]
                                    for h in range(B_HEADS)], axis=0)

        dqm, dkm, dv_cols = None, None, []
        for h in range(B_HEADS):
            vl_ = slice(h * B_VAL_DIM, (h + 1) * B_VAL_DIM)
            mine = lane_head == h
            qz, kz = jnp.where(mine, qm, 0.0).astype(_MX), jnp.where(mine, km, 0.0).astype(_MX)
            a = jnp.where(low, _dot_nt(qz, kz), 0.0).astype(_MX)
            da = jnp.where(low, _dot_nt(d_o_mx[:, vl_], vb_ref[:, vl_]), 0.0).astype(_MX)
            dqm_h, dkm_h = _dot(da, kz), _dot_tn(da, qz)
            dqm = dqm_h if dqm is None else dqm + dqm_h
            dkm = dkm_h if dkm is None else dkm + dkm_h
            dv_cols.append(_dot_tn(a, d_o_mx[:, vl_]))
        dv = jnp.concatenate(dv_cols, axis=1)

        chunk = [slice(c * B_CHUNK, (c + 1) * B_CHUNK) for c in range(cpb)]
        dqc_rows, g_loc = [], []
        for c in range(cpb):
            dqc_rows.append(_dot_nt(d_o_mx[chunk[c]], on_diagonal(sst_ref[c].astype(_MX))))
            g_loc.append(diagonal_of(_dot_tn(qc[chunk[c]], d_o_mx[chunk[c]])))
        cur = ds_ref[...]
        d_state = [None] * cpb
        for c in reversed(range(cpb)):
            d_state[c] = cur
            cur = g_loc[c] + jnp.exp(jnp.sum(la[chunk[c]].T, axis=1, keepdims=True)) * cur
        ds_ref[...] = cur
        dkl_rows, dv_rows, dlast_rows = [], [], []
        ones8 = jnp.ones((8, B_VAL_DIM), _F32)
        for c in range(cpb):
            dsd = on_diagonal(d_state[c].astype(_MX))
            dkl_c = _dot_nt(vb_ref[chunk[c], :], dsd)
            dkl_rows.append(dkl_c)
            dv_rows.append(_dot(kl[chunk[c]], dsd))
            prod = d_state[c] * sst_ref[c]
            p0 = prod.astype(jnp.bfloat16)
            p1 = (prod - p0.astype(_F32)).astype(jnp.bfloat16)
            p2 = (prod - p0.astype(_F32) - p1.astype(_F32)).astype(jnp.bfloat16)
            ddec = (_dot_nt(ones8, p0) + _dot_nt(ones8, p1) + _dot_nt(ones8, p2))[0:1]
            r_last = c * B_CHUNK + B_CHUNK - 1
            dlast = jnp.sum(dkl_c * kl[chunk[c]], axis=0, keepdims=True) + ddec * jnp.exp(last[r_last:r_last + 1])
            dlast_rows.append(jnp.broadcast_to(dlast, (B_CHUNK, B_KEY_WIDTH)))
        dqc, dkl = jnp.concatenate(dqc_rows, axis=0), jnp.concatenate(dkl_rows, axis=0)
        dqkb_ref[:, 0:B_KEY_WIDTH] = ((dqm * e_q + dqc * e_c) * (B_KEY_DIM ** -0.5)).astype(_ST)
        dqkb_ref[:, B_KEY_WIDTH:2 * B_KEY_WIDTH] = (dkm * e_k + dkl * e_l).astype(_ST)
        dvb_ref[...] = (dv + jnp.concatenate(dv_rows, axis=0)).astype(_ST)
        dcum = dqm * qm - dkm * km + dqc * qc - dkl * kl
        row = lax.broadcasted_iota(jnp.int32, (tb, B_KEY_WIDTH), 0)
        dcum = jnp.where(row % B_CHUNK == B_CHUNK - 1, dcum + jnp.concatenate(dlast_rows, axis=0), dcum)
        dla = _dot_ones((same & (ri <= ci)).astype(_F32), dcum)

        da_pre = dla * (1.0 / B_GATE_TEMP) * (1.0 - _sigmoid(a_pre))
        dalr_ref[...] = _dot_nt(da_pre, wup_ref[...]).astype(_ST)
        dwup_ref[...] = dwup_ref[...] + _dot_tn(alr_ref[...], da_pre)
        db_ref[...] = db_ref[...] + jnp.sum(da_pre, axis=0, keepdims=True)

    blk = lambda s, i: s * nblk + nblk - 1 - i
    rows = lambda w: pl.BlockSpec((tb, w), lambda s, i: (blk(s, i), 0))
    full = lambda a, b: pl.BlockSpec((a, b), lambda s, i: (0, 0))
    act = lambda w: jax.ShapeDtypeStruct((t, w), _ST)
    return pl.pallas_call(
        body, name="gla_bwd", grid=(nseq, nblk),
        in_specs=[rows(512), rows(512), rows(512), rows(LANE), rows(512), rows(512),
                  pl.BlockSpec((cpb, B_KEY_WIDTH, B_VAL_DIM), lambda s, i: (blk(s, i), 0, 0)),
                  full(LANE, B_KEY_WIDTH), full(1, B_KEY_WIDTH), full(1, B_WIDTH)],
        out_specs=[rows(512), rows(512), rows(512), rows(LANE), full(LANE, B_KEY_WIDTH),
                   full(1, B_KEY_WIDTH), full(1, B_WIDTH)],
        out_shape=[act(512), act(512), act(512), act(LANE),
                   jax.ShapeDtypeStruct((LANE, B_KEY_WIDTH), _F32),
                   jax.ShapeDtypeStruct((1, B_KEY_WIDTH), _F32), jax.ShapeDtypeStruct((1, B_WIDTH), _F32)],
        scratch_shapes=[pltpu.VMEM((B_KEY_WIDTH, B_VAL_DIM), _F32)],
        compiler_params=_cparams(("arbitrary", "arbitrary")),
    )(qkb, vb, zb, alr, oraw, dob, sst, wup, b_alpha, gn)


def _merge_loss(oa, ob, ga, gb, x2, tgt, wa, wb, wo, g_final):
    t = x2.shape[0]
    tm = min(t, 512)
    nt = t // tm

    def body(oa_ref, ob_ref, ga_ref, gb_ref, x_ref, t_ref, wa_ref, wb_ref, wo_ref, gf_ref,
             dh_ref, doa_ref, dob_ref, dga_ref, dgb_ref, dwa_ref, dwb_ref, dwo_ref, dgf_ref, loss_ref,
             ya_s, yb_s, out_s, dmer_s, mrg_s, dya_s, dyb_s):
        first = pl.program_id(0) == 0
        so_far = lambda ref: jnp.where(first, 0.0, ref[...])

        slabs = [slice(s, s + MERGE_SLAB) for s in range(0, tm, MERGE_SLAB)]
        fold = lambda a: a[0:8] + a[8:16]
        ya_s[...] = _dot(oa_ref[...], wa_ref[...])
        yb_s[...] = _dot(ob_ref[...], wb_ref[...])
        for rows_ in slabs:
            sga, sgb = _sigmoid(ga_ref[rows_, :].astype(_F32)), _sigmoid(gb_ref[rows_, :].astype(_F32))
            mrg_s[rows_, :] = (sga * ya_s[rows_, :] + sgb * yb_s[rows_, :]).astype(_MX)
        out_s[...] = x_ref[...] + _dot(mrg_s[...], wo_ref[...])
        gf = gf_ref[...]
        loss8 = jnp.zeros((8, D_MODEL), _F32)
        dgf8 = jnp.zeros((8, D_MODEL), _F32)
        for rows_ in slabs:
            out = out_s[rows_, :]
            r = lax.rsqrt(jnp.mean(out * out, axis=-1, keepdims=True) + NORM_EPS)
            nrm = out * r
            err = nrm * gf - t_ref[rows_, :]
            loss8 = loss8 + fold(err * err)
            dy = err * (1.0 / D_MODEL)
            dgf8 = dgf8 + fold(dy * nrm)
            dh = _rms_bwd(dy * gf, nrm, r)
            dh_ref[rows_, :] = dh.astype(_ST)
        loss_ref[...] = so_far(loss_ref) + (0.5 / D_MODEL) * jnp.sum(loss8, axis=0, keepdims=True)
        dgf_ref[...] = so_far(dgf_ref) + jnp.sum(dgf8, axis=0, keepdims=True)
        dmer_s[...] = _dot_nt(dh_ref[...], wo_ref[...])
        dwo_ref[...] = so_far(dwo_ref) + _dot_tn(mrg_s[...], dh_ref[...])
        for rows_ in slabs:
            sga, sgb = _sigmoid(ga_ref[rows_, :].astype(_F32)), _sigmoid(gb_ref[rows_, :].astype(_F32))
            dmer = dmer_s[rows_, :]
            da, db = dmer * sga, dmer * sgb
            dya_s[rows_, :] = da.astype(_MX)
            dyb_s[rows_, :] = db.astype(_MX)
            dga_ref[rows_, :] = (da * ya_s[rows_, :] * (1.0 - sga)).astype(_ST)
            dgb_ref[rows_, :] = (db * yb_s[rows_, :] * (1.0 - sgb)).astype(_ST)
        doa_ref[...] = _dot_nt(dya_s[...], wa_ref[...]).astype(_ST)
        dob_ref[...] = _dot_nt(dyb_s[...], wb_ref[...]).astype(_ST)
        dwa_ref[...] = so_far(dwa_ref) + _dot_tn(oa_ref[...], dya_s[...])
        dwb_ref[...] = so_far(dwb_ref) + _dot_tn(ob_ref[...], dyb_s[...])

    rows = lambda w: pl.BlockSpec((tm, w), lambda i: (i, 0))
    full = lambda a, b: pl.BlockSpec((a, b), lambda i: (0, 0), pipeline_mode=pl.Buffered(1))
    return pl.pallas_call(
        body, name="merge_loss", grid=(nt,),
        in_specs=[rows(512), rows(512), rows(D_MODEL), rows(D_MODEL), rows(D_MODEL), rows(D_MODEL),
                  full(A_WIDTH, D_MODEL), full(B_WIDTH, D_MODEL), full(D_MODEL, D_MODEL), full(1, D_MODEL)],
        out_specs=[rows(D_MODEL), rows(512), rows(512), rows(D_MODEL), rows(D_MODEL),
                   full(A_WIDTH, D_MODEL), full(B_WIDTH, D_MODEL), full(D_MODEL, D_MODEL),
                   full(1, D_MODEL), full(1, D_MODEL)],
        out_shape=[jax.ShapeDtypeStruct((t, D_MODEL), _ST), jax.ShapeDtypeStruct((t, 512), _ST),
                   jax.ShapeDtypeStruct((t, 512), _ST), jax.ShapeDtypeStruct((t, D_MODEL), _ST),
                   jax.ShapeDtypeStruct((t, D_MODEL), _ST),
                   jax.ShapeDtypeStruct((A_WIDTH, D_MODEL), _F32), jax.ShapeDtypeStruct((B_WIDTH, D_MODEL), _F32),
                   jax.ShapeDtypeStruct((D_MODEL, D_MODEL), _F32), jax.ShapeDtypeStruct((1, D_MODEL), _F32),
                   jax.ShapeDtypeStruct((1, D_MODEL), _F32)],
        scratch_shapes=[pltpu.VMEM((tm, D_MODEL), _F32)] * 4 + [pltpu.VMEM((tm, D_MODEL), _MX)] * 3,
        compiler_params=_cparams(("arbitrary",), VMEM_LIMIT),
    )(oa, ob, ga, gb, x2, tgt, wa, wb, wo, g_final)


def _in_proj_bwd_x(dpieces, wt, x2, dh2, g_in):
    t = x2.shape[0]
    tm = min(t, 512)
    np_ = len(PIECES)

    def body(*refs):
        dp_refs = refs[:np_]
        w_ref, x_ref, dh2_ref, g_ref, gx_ref, dg_ref = refs[np_:]

        @pl.when(pl.program_id(0) == 0)
        def _():
            dg_ref[...] = jnp.zeros_like(dg_ref)

        dh = None
        for (name, a, b), dp in zip(PIECES, dp_refs):
            part = _dot(dp[...], w_ref[a:b, :])
            dh = part if dh is None else dh + part
        xv = x_ref[...]
        r = lax.rsqrt(jnp.mean(xv * xv, axis=-1, keepdims=True) + NORM_EPS)
        nrm = xv * r
        dg_ref[...] = dg_ref[...] + jnp.sum(dh * nrm, axis=0, keepdims=True)
        gx_ref[...] = dh2_ref[...].astype(_F32) + _rms_bwd(dh * g_ref[...], nrm, r)

    rows = lambda w: pl.BlockSpec((tm, w), lambda i: (i, 0))
    full = lambda a, b: pl.BlockSpec((a, b), lambda i: (0, 0), pipeline_mode=pl.Buffered(1))
    return pl.pallas_call(
        body, name="in_proj_bwd_x", grid=(t // tm,),
        in_specs=[rows(b - a) for _, a, b in PIECES] + [full(D_IN, D_MODEL), rows(D_MODEL), rows(D_MODEL),
                                                          full(1, D_MODEL)],
        out_specs=[rows(D_MODEL), full(1, D_MODEL)],
        out_shape=[jax.ShapeDtypeStruct((t, D_MODEL), _F32), jax.ShapeDtypeStruct((1, D_MODEL), _F32)],
        compiler_params=_cparams(("arbitrary",), VMEM_LIMIT),
    )(*dpieces, wt, x2, dh2, g_in)


def _in_proj_bwd_w(h, dpieces):
    t = h.shape[0]
    tm = min(t, 1024)
    nt = t // tm
    np_ = len(PIECES)

    def body(*refs):
        h_ref, dp_refs, out_ref = refs[0], refs[1:1 + np_], refs[1 + np_]
        acc_ref, sem = refs[2 + np_:]
        i = pl.program_id(0)
        hv = h_ref[...]
        writes = []
        for j, ((name, a, b), dp) in enumerate(zip(PIECES, dp_refs)):
            part = _dot_tn(dp[...], hv)
            if name == "alr":
                b = a + B_GATE_RANK
                part = part[0:B_GATE_RANK]
            acc_ref[a:b, :] = jnp.where(i == 0, 0.0, acc_ref[a:b, :]) + part
            writes.append(pltpu.make_async_copy(acc_ref.at[a:b], out_ref.at[a:b], sem.at[j]))

            @pl.when(i == nt - 1)
            def _(cp=writes[-1]):
                cp.start()

        @pl.when(i == nt - 1)
        def _():
            for cp in writes:
                cp.wait()

    rows = lambda w: pl.BlockSpec((tm, w), lambda i: (i, 0))
    return pl.pallas_call(
        body, name="in_proj_bwd_w", grid=(nt,),
        in_specs=[rows(D_MODEL)] + [rows(b - a) for _, a, b in PIECES],
        out_specs=_ANY, out_shape=jax.ShapeDtypeStruct((D_IN, D_MODEL), _F32),
        scratch_shapes=[pltpu.VMEM((D_IN, D_MODEL), _F32), pltpu.SemaphoreType.DMA((np_,))],
        compiler_params=_cparams(("arbitrary",), VMEM_LIMIT),
    )(h, *dpieces)


def _place():
    return lax.axis_index("x"), lax.axis_index("y"), lax.axis_index("c")


def _other_chips(x, y):
    return [(1 - x, y), (x, 1 - y), (1 - x, 1 - y)]


class _Split(NamedTuple):
    by_rows: bool
    step: int
    size: int

    def half(self, ref, c):
        r, n = ref.shape[-2:]
        if self.by_rows:
            return ref.at[:, pl.ds(pl.multiple_of(c * (n // 2), LANE), n // 2)]
        return ref.at[pl.ds(pl.multiple_of(c * (r // 2), 16), r // 2), :]

    def chip_part(self, ref, k):
        if self.by_rows:
            return ref.at[pl.ds(pl.multiple_of(k * self.step, 16), self.size), :]
        return ref.at[:, pl.ds(pl.multiple_of(k * self.size, LANE), self.size)]

    def half_shape(self, shape):
        r, n = shape
        return (r, n // 2) if self.by_rows else (r // 2, n)

    def part_shape(self, shape):
        r, n = shape
        return (self.size, n) if self.by_rows else (r, self.size)


SPLIT_W_IN_T = _Split(True, WINDOW_STEP, WINDOW_ROWS)
SPLIT_W_O = _Split(True, 256, 256)
SPLIT_W_OUT = _Split(False, 256, 256)


def _gather_weights(shards, splits, fulls, pos_f):
    nw = len(shards)
    t = pos_f.shape[0]

    def body(*refs):
        ins, (pos_ref, c_ref) = refs[:nw], refs[nw:nw + 2]
        outs, tables = refs[nw + 2:2 * nw + 2], refs[2 * nw + 2:2 * nw + 5]
        send_a, recv_a, send_b, recv_b = refs[2 * nw + 5:]
        x, y, c = _place()
        me = 2 * x + y
        peers = _other_chips(x, y)

        def place(i, k, half):
            if splits[i] is None:
                return outs[i].at[k]
            if fulls[i][0] == 4 and len(fulls[i]) == 3:
                whole = outs[i].at[k]
            else:
                whole = splits[i].chip_part(outs[i], k)
            return splits[i].half(whole, half)

        first, passed = [], []
        for i in range(nw):
            src = ins[i] if splits[i] is None else splits[i].half(ins[i], c)
            for j, (px, py) in enumerate(peers):
                cp = pltpu.make_async_remote_copy(
                    src_ref=src, dst_ref=place(i, me, c), send_sem=send_a.at[3 * i + j],
                    recv_sem=recv_a.at[3 * i + j], device_id=(px, py, c), device_id_type=_MESH)
                cp.start()
                first.append(cp)
        _rope_tables_into(pos_ref, c_ref, *tables)
        for i in range(nw):
            for j, (px, py) in enumerate(peers):
                landed = place(i, 2 * px + py, c)
                pltpu.make_async_remote_copy(
                    src_ref=landed, dst_ref=landed, send_sem=send_a.at[3 * i + j], recv_sem=recv_a.at[3 * i + j],
                    device_id=(px, py, c), device_id_type=_MESH).wait_recv()
                if splits[i] is not None:
                    cp = pltpu.make_async_remote_copy(
                        src_ref=landed, dst_ref=landed, send_sem=send_b.at[3 * i + j], recv_sem=recv_b.at[3 * i + j],
                        device_id=(x, y, 1 - c), device_id_type=_MESH)
                    cp.start()
                    passed.append(cp)
        for i in range(nw):
            if splits[i] is None:
                continue
            for j, (px, py) in enumerate(peers):
                theirs = place(i, 2 * px + py, 1 - c)
                pltpu.make_async_remote_copy(
                    src_ref=theirs, dst_ref=theirs, send_sem=send_b.at[3 * i + j], recv_sem=recv_b.at[3 * i + j],
                    device_id=(x, y, 1 - c), device_id_type=_MESH).wait_recv()
        for cp in first + passed:
            cp.wait_send()

    vm = pl.BlockSpec(memory_space=pltpu.VMEM)
    tab = jax.ShapeDtypeStruct((t, LANE), _F32)
    return pl.pallas_call(
        body, name="gather_weights",
        in_specs=[_ANY] * nw + [vm, vm], out_specs=[_ANY] * nw + [vm] * 3,
        out_shape=[jax.ShapeDtypeStruct(f, s.dtype) for f, s in zip(fulls, shards)] + [tab] * 3,
        scratch_shapes=[pltpu.SemaphoreType.DMA((3 * nw,)) for _ in range(4)],
        compiler_params=_cparams(None, VMEM_LIMIT),
    )(*shards, pos_f, _rope_consts())


def _assemble_w_in_t(slots):
    bw = 256
    ov = WINDOW_ROWS - WINDOW_STEP

    def body(s_ref, o_ref):
        for k in range(4):
            base = k * WINDOW_STEP
            lo = 0 if k == 0 else ov
            if k > 0:
                o_ref[base:base + ov, :] = s_ref[k - 1, WINDOW_STEP:WINDOW_ROWS, :] + s_ref[k, 0:ov, :]
            hi = WINDOW_ROWS if k == 3 else WINDOW_STEP
            o_ref[base + lo:base + hi, :] = s_ref[k, lo:hi, :]

    return pl.pallas_call(
        body, name="assemble_w_in_t", grid=(D_MODEL // bw,),
        in_specs=[pl.BlockSpec((4, WINDOW_ROWS, bw), lambda i: (0, 0, i))],
        out_specs=pl.BlockSpec((D_IN, bw), lambda i: (0, i)),
        out_shape=jax.ShapeDtypeStruct((D_IN, D_MODEL), slots.dtype),
        compiler_params=_cparams(("parallel",)),
    )(slots)


def _pair_exchange(grads, splits):
    nw = len(grads)

    def body(*refs):
        ins, outs = refs[:nw], refs[nw:2 * nw]
        send, recv = refs[2 * nw:]
        x, y, c = _place()
        copies = []
        for i in range(nw):
            cp = pltpu.make_async_remote_copy(
                src_ref=splits[i].half(ins[i], 1 - c), dst_ref=outs[i], send_sem=send.at[i], recv_sem=recv.at[i],
                device_id=(x, y, 1 - c), device_id_type=_MESH)
            cp.start()
            copies.append(cp)
        for cp in copies:
            cp.wait()

    return pl.pallas_call(
        body, name="grad_pair_exchange",
        in_specs=[_ANY] * nw, out_specs=[_ANY] * nw,
        out_shape=[jax.ShapeDtypeStruct(sp.half_shape(g.shape), g.dtype) for g, sp in zip(grads, splits)],
        scratch_shapes=[pltpu.SemaphoreType.DMA((nw,)), pltpu.SemaphoreType.DMA((nw,))],
    )(*grads)


def _row_block(rows):
    for cand in (976, 176, 256, 128):
        if rows % cand == 0:
            return cand
    return rows


def _pair_sum(g, r, split, c_arr, name):
    hr, hn = r.shape
    br = _row_block(hr)
    if split.by_rows:
        g_spec = pl.BlockSpec((br, hn), lambda i, c_ref: (i, c_ref[0]))
    else:
        g_spec = pl.BlockSpec((br, hn), lambda i, c_ref: (c_ref[0] * (hr // br) + i, 0))

    def body(c_ref, g_ref, r_ref, o_ref):
        o_ref[...] = (g_ref[...] + r_ref[...]).astype(o_ref.dtype)

    return pl.pallas_call(
        body, name=name,
        grid_spec=pltpu.PrefetchScalarGridSpec(
            num_scalar_prefetch=1, grid=(hr // br,),
            in_specs=[g_spec, pl.BlockSpec((br, hn), lambda i, c_ref: (i, 0))],
            out_specs=pl.BlockSpec((br, hn), lambda i, c_ref: (i, 0))),
        out_shape=jax.ShapeDtypeStruct(r.shape, _MX),
        compiler_params=_cparams(("parallel",)),
    )(c_arr, g, r)


_HBM = pl.BlockSpec(memory_space=pltpu.HBM)
_SEM = pl.BlockSpec(memory_space=pltpu.SEMAPHORE)
_FLOWS = pltpu.SideEffectType.DATAFLOW_SIDE_EFFECTING


def _chip_exchange_copies(parts, lands, send, recv, splits):
    x, y, c = _place()
    me = 2 * x + y
    copies = []
    for i in range(len(parts)):
        for px, py in _other_chips(x, y):
            copies.append((splits[i].chip_part(parts[i], 2 * px + py), lands[i].at[me], (px, py, c)))
    return [pltpu.make_async_remote_copy(src_ref=src, dst_ref=dst, send_sem=send.at[k], recv_sem=recv.at[k],
                                         device_id=peer, device_id_type=_MESH)
            for k, (src, dst, peer) in enumerate(copies)]


def _chip_exchange_start(parts, splits):
    nw = len(parts)
    n_copies = 3 * nw
    flying = list(parts) + [lax.empty((4,) + sp.part_shape(p.shape), p.dtype) for p, sp in zip(parts, splits)]

    def body(*refs):
        ins = refs[:len(flying)]
        send, recv = refs[len(flying):len(flying) + 2]
        token = refs[-1]
        for cp in _chip_exchange_copies(ins[:nw], ins[nw:], send, recv, splits):
            cp.start()
        token[...] = jnp.zeros_like(token)

    outs = pl.pallas_call(
        body, name="grad_chip_exchange_start",
        in_specs=[_HBM] * len(flying),
        out_specs=[_SEM, _SEM] + [_HBM] * len(flying) + [pl.BlockSpec(memory_space=pltpu.VMEM)],
        out_shape=[pltpu.SemaphoreType.DMA((n_copies,)), pltpu.SemaphoreType.DMA((n_copies,))]
        + [pltpu.HBM(f.shape, f.dtype) for f in flying] + [jax.ShapeDtypeStruct((8, LANE), _F32)],
        input_output_aliases={i: 2 + i for i in range(len(flying))},
        compiler_params=pltpu.CompilerParams(has_side_effects=_FLOWS),
    )(*[pltpu.with_memory_space_constraint(f, pltpu.HBM) for f in flying])
    return outs[0], outs[1], outs[2:2 + len(flying)], outs[-1]


def _chip_exchange_wait(send, recv, flying, splits, after):
    nw = len(flying) // 2

    def body(*refs):
        ins = refs[:len(flying)]
        send_ref, recv_ref = refs[len(flying):len(flying) + 2]
        for cp in _chip_exchange_copies(ins[:nw], ins[nw:], send_ref, recv_ref, splits):
            cp.wait_send()
            cp.wait_recv()

    outs = pl.pallas_call(
        body, name="grad_chip_exchange_wait",
        in_specs=[_HBM] * len(flying) + [_SEM, _SEM, _ANY],
        out_specs=[_HBM] * len(flying),
        out_shape=[pltpu.HBM(f.shape, f.dtype) for f in flying],
        input_output_aliases={i: i for i in range(len(flying))},
        compiler_params=pltpu.CompilerParams(has_side_effects=_FLOWS),
    )(*flying, send, recv, after)
    return outs[nw:]


def _sum_chips(q, p, split, place_arr, name):
    _, hr, hn = q.shape
    if split.by_rows:
        out_shape = (hr, 2 * hn)
        o_spec = pl.BlockSpec((hr, hn), lambda i, pr: (0, pr[0]))
        p_spec = pl.BlockSpec((pl.Element(hr), pl.Element(hn)), lambda i, pr: (pr[1] * split.step, 0))
    else:
        out_shape = (2 * hr, hn)
        o_spec = pl.BlockSpec((hr, hn), lambda i, pr: (pr[0], 0))
        p_spec = pl.BlockSpec((hr, hn), lambda i, pr: (0, pr[1]))

    def body(pr, q_ref, p_ref, o_ref):
        f = lambda k: jnp.where(pr[1] == k, p_ref[...], q_ref[k]).astype(_F32)
        o_ref[...] = ((f(0) + f(1)) + f(2)) + f(3)

    return pl.pallas_call(
        body, name=name,
        grid_spec=pltpu.PrefetchScalarGridSpec(
            num_scalar_prefetch=1, grid=(1,),
            in_specs=[pl.BlockSpec((4, hr, hn), lambda i, pr: (0, 0, 0)), p_spec], out_specs=o_spec),
        out_shape=jax.ShapeDtypeStruct(out_shape, _F32),
        compiler_params=_cparams(("arbitrary",), VMEM_LIMIT),
    )(place_arr, q, p)


def _pair_share(bufs, splits, small):
    nw = len(bufs)

    def body(*refs):
        ins, small_ref, outs, all_ref = refs[:nw], refs[nw], refs[nw + 1:2 * nw + 1], refs[2 * nw + 1]
        send, recv, s_send, s_recv = refs[2 * nw + 2:]
        x, y, c = _place()
        copies = []
        for r in range(1, 8):
            peer = (1 - x if r & 4 else x, 1 - y if r & 2 else y, 1 - c if r & 1 else c)
            cp = pltpu.make_async_remote_copy(
                src_ref=small_ref, dst_ref=all_ref.at[4 * x + 2 * y + c], send_sem=s_send.at[r - 1],
                recv_sem=s_recv.at[r - 1], device_id=peer, device_id_type=_MESH)
            cp.start()
            copies.append(cp)
        for i in range(nw):
            cp = pltpu.make_async_remote_copy(
                src_ref=splits[i].half(ins[i], c), dst_ref=splits[i].half(outs[i], c), send_sem=send.at[i],
                recv_sem=recv.at[i], device_id=(x, y, 1 - c), device_id_type=_MESH)
            cp.start()
            copies.append(cp)
        for cp in copies:
            cp.wait()

    return pl.pallas_call(
        body, name="grad_pair_share",
        in_specs=[_ANY] * (nw + 1), out_specs=[_ANY] * (nw + 1),
        out_shape=[jax.ShapeDtypeStruct(b.shape, b.dtype) for b in bufs]
        + [jax.ShapeDtypeStruct((8,) + small.shape, small.dtype)],
        input_output_aliases={i: i for i in range(nw)},
        scratch_shapes=[pltpu.SemaphoreType.DMA((nw,)), pltpu.SemaphoreType.DMA((nw,)),
                        pltpu.SemaphoreType.DMA((7,)), pltpu.SemaphoreType.DMA((7,))],
    )(*bufs, small)


def _sum_devices(parts, own, dev_arr):
    def body(dev, p_ref, own_ref, tot_ref):
        f = lambda d: jnp.where(dev[0] == d, own_ref[...], p_ref[d])
        acc = f(0)
        for d in range(1, 8):
            acc = acc + f(d)
        tot_ref[...] = acc

    return pl.pallas_call(
        body, name="small_sum",
        grid_spec=pltpu.PrefetchScalarGridSpec(
            num_scalar_prefetch=1, grid=(1,),
            in_specs=[pl.BlockSpec(parts.shape, lambda i, dev: (0, 0, 0)), pl.BlockSpec(own.shape, lambda i, dev: (0, 0))],
            out_specs=pl.BlockSpec(own.shape, lambda i, dev: (0, 0))),
        out_shape=jax.ShapeDtypeStruct(own.shape, own.dtype),
    )(dev_arr, parts, own)


def _adamw(w, g, m, v, name):
    lead = w.shape[0] != 1
    r, n = (w.shape[0], w.shape[2]) if lead else w.shape[1:]
    br = r
    for cand in (256, 244, 128):
        if r > cand and r % cand == 0:
            br = cand
            break

    def body(w_ref, g_ref, m_ref, v_ref, d_ref, nm_ref, nv_ref):
        gv = g_ref[...]
        m2 = ADAM_B1 * m_ref[...] + (1.0 - ADAM_B1) * gv
        v2 = ADAM_B2 * v_ref[...] + (1.0 - ADAM_B2) * (gv * gv)
        m_hat = m2 / (1.0 - ADAM_B1 ** ADAM_STEP)
        v_hat = v2 / (1.0 - ADAM_B2 ** ADAM_STEP)
        d_ref[...] = -ADAM_LR * (m_hat / (jnp.sqrt(v_hat) + ADAM_EPS) + ADAM_WD * w_ref[...])
        nm_ref[...] = m2
        nv_ref[...] = v2

    blk = pl.BlockSpec((br, 1, n), lambda i: (i, 0, 0)) if lead else pl.BlockSpec((None, br, n), lambda i: (0, i, 0))
    shp = jax.ShapeDtypeStruct(w.shape, _F32)
    return pl.pallas_call(
        body, name=name, grid=(r // br,),
        in_specs=[blk] * 4, out_specs=[blk] * 3, out_shape=[shp] * 3,
        compiler_params=_cparams(("parallel",)),
    )(w, g, m, v)


def kernel(x, positions, g_in, w_in, w_alpha_up, b_alpha, attn_sinks, g_gla_norm, w_out_a, w_out_b, w_o, g_final, loss_target, m_g_in, m_w_in, m_w_alpha_up, m_b_alpha, m_attn_sinks, m_g_gla_norm, m_w_out_a, m_w_out_b, m_w_o, m_g_final, v_g_in, v_w_in, v_w_alpha_up, v_b_alpha, v_attn_sinks, v_g_gla_norm, v_w_out_a, v_w_out_b, v_w_o, v_g_final):
    nseq, seq, _ = x.shape
    t = nseq * seq
    cx, cy, cc = _place()
    chip = 2 * cx + cy
    c_arr = jnp.reshape(cc, (1,)).astype(jnp.int32)

    tr = lambda w: jnp.transpose(w, (2, 0, 1))
    w_in_t = tr(w_in).reshape(SHARD, D_MODEL).astype(_MX)
    pad = WINDOW_ROWS - SHARD
    window = lax.switch(chip, [lambda w, k=k: jnp.pad(w, ((4 * k, pad - 4 * k), (0, 0))) for k in range(4)], w_in_t)
    shards = [window, w_out_a[0].astype(_MX), w_out_b[0].astype(_MX), w_o[0].astype(_MX), w_alpha_up[0].astype(_MX)]
    splits = [SPLIT_W_IN_T, SPLIT_W_OUT, SPLIT_W_OUT, SPLIT_W_O, None]
    fulls = [(4, WINDOW_ROWS, D_MODEL), (A_WIDTH, D_MODEL), (B_WIDTH, D_MODEL), (D_MODEL, D_MODEL),
             (4, B_GATE_RANK, B_KEY_WIDTH // 4)]
    pos_f = positions.astype(_F32).reshape(t, 1)
    win_g, wa, wb, wo, wup_g, cos, sa, sb = _gather_weights(shards, splits, fulls, pos_f)
    nsh = D_MODEL // 4
    win_g = lax.dynamic_update_slice(win_g, window[None], (chip, 0, 0))
    wa = lax.dynamic_update_slice(wa, shards[1], (0, nsh * chip))
    wb = lax.dynamic_update_slice(wb, shards[2], (0, nsh * chip))
    wo = lax.dynamic_update_slice(wo, shards[3], (nsh * chip, 0))
    wup_g = lax.dynamic_update_slice(wup_g, shards[4][None], (chip, 0, 0))
    wt = _assemble_w_in_t(win_g)
    wup = jnp.concatenate([jnp.transpose(wup_g, (1, 0, 2)).reshape(B_GATE_RANK, B_KEY_WIDTH),
                           jnp.zeros((LANE - B_GATE_RANK, B_KEY_WIDTH), _MX)], axis=0)

    x2 = x.reshape(t, D_MODEL)
    tgt = loss_target.reshape(t, D_MODEL)
    sinks = attn_sinks.reshape(A_HEADS)
    gf = g_final.reshape(1, D_MODEL)

    h, qkv, za, qkb, vb, zb, alr, ga, gb = _in_proj(x2, g_in, wt, cos, sa, sb)
    oa = _attn_fwd(qkv, za, sinks, nseq)
    ob, oraw, sst = _gla_fwd(qkb, vb, zb, alr, wup, b_alpha, g_gla_norm, nseq)

    dh2, doa, dob, dga, dgb, dwa, dwb, dwo, dgf, lossv = _merge_loss(oa, ob, ga, gb, x2, tgt, wa, wb, wo, gf)

    dqkv, dza, dsink = _attn_bwd(qkv, za, doa, sinks, cos, sa, sb, nseq)
    dqkb, dvb, dzb, dalr, dwup, dba, dgn = _gla_bwd(qkb, vb, zb, alr, oraw, dob, sst, wup, b_alpha, g_gla_norm, nseq)
    dpieces = [dqkv, dza, dqkb, dvb, dzb, dalr, dga, dgb]
    dwin_t = _in_proj_bwd_w(h, dpieces)

    grads = [dwin_t, dwa, dwb, dwo]
    gsplits = [SPLIT_W_IN_T, SPLIT_W_OUT, SPLIT_W_OUT, SPLIT_W_O]
    names = ("w_in", "w_out_a", "w_out_b", "w_o")
    from_sibling = _pair_exchange(grads, gsplits)
    pair_sums = [_pair_sum(g, r, sp, c_arr, "pair_sum_" + nm)
                 for g, r, sp, nm in zip(grads, from_sibling, gsplits, names)]
    send, recv, flying, token = _chip_exchange_start(pair_sums, gsplits)
    grad_x2, dgin = _in_proj_bwd_x(dpieces, wt, x2, dh2, g_in + token[0, 0])
    from_chips = _chip_exchange_wait(send, recv, flying, gsplits, grad_x2)
    place_arr = jnp.stack([cc, chip]).astype(jnp.int32)
    reduced = [_sum_chips(q, p, sp, place_arr, "chip_sum_" + nm)
               for q, p, sp, nm in zip(from_chips, pair_sums, gsplits, names)]
    small = jnp.concatenate([
        dgin, dgf, dgn, dba,
        jnp.pad(dsink[:, 0].reshape(1, A_HEADS), ((0, 0), (0, LANE - A_HEADS))),
        jnp.pad(jnp.sum(lossv, axis=1, keepdims=True), ((0, 0), (0, LANE - 1))),
        dwup[:B_GATE_RANK].reshape(1, B_GATE_RANK * B_KEY_WIDTH)], axis=1).reshape(-1, LANE)
    g_window, g_wa, g_wb, g_wo, small_parts = _pair_share(reduced, gsplits, small)
    g_win_t = lax.switch(chip, [lambda w, k=k: w[4 * k:4 * k + SHARD].reshape(SHARD, 1, D_MODEL) for k in range(4)],
                         g_window)
    dev_arr = jnp.reshape(2 * chip + cc, (1,)).astype(jnp.int32)
    tot = _sum_devices(small_parts, small, dev_arr).reshape(1, -1)
    o = 0
    def take(n):
        nonlocal o
        o += n
        return tot[:, o - n:o]
    g_gin, g_gf, g_gn, g_ba = take(D_MODEL), take(D_MODEL), take(B_WIDTH), take(B_KEY_WIDTH)
    g_sink = take(LANE)[:, :A_HEADS]
    loss = take(LANE)[0, 0]
    g_wup_full = take(B_GATE_RANK * B_KEY_WIDTH).reshape(B_GATE_RANK, B_KEY_WIDTH)
    nup = B_KEY_WIDTH // 4
    g_wup = lax.dynamic_slice(g_wup_full, (0, chip * nup), (B_GATE_RANK, nup))

    def pack(*parts):
        return jnp.concatenate([p.reshape(1, -1) for p in parts], axis=1)

    sm_w = pack(g_in, g_final, g_gla_norm, b_alpha, attn_sinks, w_alpha_up)
    sm_g = pack(g_gin, g_gf, g_gn, g_ba, g_sink, g_wup)
    sm_m = pack(m_g_in, m_g_final, m_g_gla_norm, m_b_alpha, m_attn_sinks, m_w_alpha_up)
    sm_v = pack(v_g_in, v_g_final, v_g_gla_norm, v_b_alpha, v_attn_sinks, v_w_alpha_up)
    sm_out = [p[0] for p in _adamw(sm_w[None], sm_g[None], sm_m[None], sm_v[None], "adamw_small")]

    def unpack(p):
        sizes = (D_MODEL, D_MODEL, B_WIDTH, B_KEY_WIDTH, A_HEADS, B_GATE_RANK * nup)
        outs, at = [], 0
        for s in sizes:
            outs.append(p[:, at:at + s])
            at += s
        gi, gfin, gnn, ba, sk, wu = outs
        return dict(g_in=gi, g_final=gfin.reshape(D_MODEL), g_gla_norm=gnn, b_alpha=ba, attn_sinks=sk,
                    w_alpha_up=wu.reshape(1, B_GATE_RANK, nup))

    untr = lambda a: jnp.transpose(a, (1, 2, 0))
    big = dict(w_in=tuple(untr(a) for a in (g_win_t,) + tuple(_adamw(tr(w_in), g_win_t, tr(m_w_in), tr(v_w_in), "adamw_w_in"))))
    for nm, w, g, m, v in (("w_out_a", w_out_a, g_wa, m_w_out_a, v_w_out_a),
                           ("w_out_b", w_out_b, g_wb, m_w_out_b, v_w_out_b), ("w_o", w_o, g_wo, m_w_o, v_w_o)):
        big[nm] = (g[None],) + tuple(_adamw(w, g[None], m, v, "adamw_" + nm))

    order = ("g_in", "w_in", "w_alpha_up", "b_alpha", "attn_sinks", "g_gla_norm", "w_out_a", "w_out_b", "w_o", "g_final")
    small_sets = [unpack(sm_g)] + [unpack(p) for p in sm_out]
    outs = []
    for kind in range(4):
        for nm in order:
            outs.append(big[nm][kind] if nm in big else small_sets[kind][nm])
    return (loss, grad_x2.reshape(x.shape), *outs)
```

```python
import math
from typing import NamedTuple

import numpy as np
import jax
import jax.numpy as jnp
from jax import lax
from jax.experimental import pallas as pl
from jax.experimental.pallas import tpu as pltpu

D_MODEL = 1024
A_HEADS, A_KV_HEADS, A_HEAD_DIM = 8, 2, 64
A_GROUP = A_HEADS // A_KV_HEADS
A_WIDTH, A_KV_WIDTH = 512, 128
BLOCK = 128
ROPE_THETA = 500000.0
ROPE_DIM = 16
B_HEADS, B_KEY_DIM, B_VAL_DIM = 4, 64, 128
B_KEY_WIDTH, B_WIDTH = 256, 512
B_GATE_RANK = 16
B_GATE_TEMP = 16.0
B_CHUNK = 64
NORM_EPS = 1e-6
NEG_BIG = -1e30
D_IN = 4880

ADAM_LR, ADAM_B1, ADAM_B2, ADAM_EPS, ADAM_WD, ADAM_STEP = 0.001, 0.9, 0.999, 1e-08, 0.01, 10

LANE = 128
ALR_AT = 2816
PIECES = (("qkv", 0, 768), ("za", 768, 1280), ("qkb", 1280, 1792), ("vb", 1792, 2304),
          ("zb", 2304, 2816), ("alr", ALR_AT, ALR_AT + LANE), ("ga", 2832, 3856), ("gb", 3856, 4880))
SHARD = D_IN // 4
WINDOW_STEP = 1216
WINDOW_ROWS = 1232

GLA_BLOCK = 256
MERGE_SLAB = 16
VMEM_LIMIT = 56 * 1024 * 1024

_F32 = jnp.float32
_MX = jnp.bfloat16
_ST = jnp.bfloat16

_MESH = pl.DeviceIdType.MESH
_ANY = pl.BlockSpec(memory_space=pl.ANY)


def _cparams(sem=None, vmem=None):
    return pltpu.CompilerParams(dimension_semantics=sem, vmem_limit_bytes=vmem)


def _dot(a, b):
    return jnp.dot(a.astype(_MX), b.astype(_MX), preferred_element_type=_F32)


def _dot_nt(a, b):
    return lax.dot_general(a.astype(_MX), b.astype(_MX), (((1,), (1,)), ((), ())),
                           preferred_element_type=_F32)


def _dot_tn(a, b):
    return lax.dot_general(a.astype(_MX), b.astype(_MX), (((0,), (0,)), ((), ())),
                           preferred_element_type=_F32)


def _dot_ones(ones_mat, v):
    o = ones_mat.astype(jnp.bfloat16)
    v0 = v.astype(jnp.bfloat16)
    r1 = v - v0.astype(_F32)
    v1 = r1.astype(jnp.bfloat16)
    v2 = (r1 - v1.astype(_F32)).astype(jnp.bfloat16)
    d = lambda t: jnp.dot(o, t, preferred_element_type=_F32)
    return d(v0) + d(v1) + d(v2)


def _sigmoid(x):
    return 0.5 * jnp.tanh(0.5 * x) + 0.5


def _log_sigmoid(x):
    return jnp.minimum(x, 0.0) - jnp.log(1.0 + jnp.exp(-jnp.abs(x)))


def _lane_tile(t, width):
    reps = width // t.shape[1]
    return t if reps == 1 else jnp.tile(t, (1, reps))


def _rope(t, cos, sa, sb, sign):
    w = t.shape[1]
    rot = pltpu.roll(t, w - 8, 1) * _lane_tile(sa, w) + pltpu.roll(t, 8, 1) * _lane_tile(sb, w)
    return t * _lane_tile(cos, w) + sign * rot


def _rms_bwd(dy_g, n, r):
    return r * (dy_g - n * jnp.mean(dy_g * n, axis=-1, keepdims=True))


ROPE_ROWS = 256


def _rope_consts():
    lane = np.arange(LANE) % A_HEAD_DIM
    half = ROPE_DIM // 2
    inv = np.exp((np.float32(-math.log(ROPE_THETA)) * np.arange(half, dtype=np.float32)) * np.float32(2.0 / ROPE_DIM))
    consts = np.zeros((8, LANE), np.float32)
    consts[0] = np.where(lane < ROPE_DIM, inv[lane % half], 0.0)
    consts[1] = np.where(lane < half, -1.0, 0.0)
    consts[2] = np.where((lane >= half) & (lane < ROPE_DIM), 1.0, 0.0)
    return jnp.asarray(consts)


def _rope_tables_into(pos_ref, c_ref, cos_ref, sa_ref, sb_ref):
    def rows_of(b, carry):
        rows = pl.ds(pl.multiple_of(b * ROPE_ROWS, ROPE_ROWS), ROPE_ROWS)
        ang = pos_ref[rows, :] * c_ref[0:1, :]
        s = jnp.sin(ang)
        cos_ref[rows, :] = jnp.cos(ang)
        sa_ref[rows, :] = s * c_ref[1:2, :]
        sb_ref[rows, :] = s * c_ref[2:3, :]
        return carry

    lax.fori_loop(0, pos_ref.shape[0] // ROPE_ROWS, rows_of, 0)


def _in_proj(x2, g_in, wt, cos, sa, sb):
    t = x2.shape[0]
    tm = min(t, 512)

    def body(x_ref, g_ref, w_ref, cos_ref, sa_ref, sb_ref, h_ref, qkv_ref, za_ref, qkb_ref,
             vb_ref, zb_ref, alr_ref, ga_ref, gb_ref):
        xv = x_ref[...]
        r = lax.rsqrt(jnp.mean(xv * xv, axis=-1, keepdims=True) + NORM_EPS)
        h = (xv * r * g_ref[...]).astype(_MX)
        h_ref[...] = h.astype(_ST)
        outs = dict(za=za_ref, qkb=qkb_ref, vb=vb_ref, zb=zb_ref, alr=alr_ref, ga=ga_ref, gb=gb_ref)
        for name, a, b in PIECES:
            p = _dot_nt(h, w_ref[a:b, :])
            if name == "qkv":
                c, s1, s2 = cos_ref[...], sa_ref[...], sb_ref[...]
                qkv_ref[:, 0:512] = _rope(p[:, 0:512], c, s1, s2, 1.0).astype(_ST)
                qkv_ref[:, 512:640] = _rope(p[:, 512:640], c, s1, s2, 1.0).astype(_ST)
                qkv_ref[:, 640:768] = p[:, 640:768].astype(_ST)
            else:
                outs[name][...] = p.astype(outs[name].dtype)

    rows = lambda w: pl.BlockSpec((tm, w), lambda i: (i, 0))
    shp = lambda name, w: jax.ShapeDtypeStruct((t, w), _F32 if name == "qkb" else _ST)
    widths = [D_MODEL] + [b - a for _, a, b in PIECES]
    return pl.pallas_call(
        body, name="in_proj", grid=(t // tm,),
        in_specs=[rows(D_MODEL), pl.BlockSpec((1, D_MODEL), lambda i: (0, 0)),
                  pl.BlockSpec((D_IN, D_MODEL), lambda i: (0, 0), pipeline_mode=pl.Buffered(1)),
                  rows(LANE), rows(LANE), rows(LANE)],
        out_specs=[rows(w) for w in widths],
        out_shape=[shp(n, w) for n, w in zip(["h"] + [p[0] for p in PIECES], widths)],
        compiler_params=_cparams(("parallel",), VMEM_LIMIT),
    )(x2, g_in, wt, cos, sa, sb)


def _attn_operands(k_prev, k_cur, v_prev, v_cur, want_bwd):
    kf = jnp.concatenate([k_prev, k_cur], axis=0).astype(_F32) * (A_HEAD_DIM ** -0.5)
    vf = jnp.concatenate([v_prev, v_cur], axis=0).astype(_F32)
    lo = lax.broadcasted_iota(jnp.int32, (1, LANE), 1) < 64

    def on_lanes(a):
        sw = pltpu.roll(a, 64, 1)
        z = jnp.zeros_like(a)
        return [[jnp.where(lo, a, z).astype(_MX), jnp.where(lo, z, sw).astype(_MX)],
                [jnp.where(lo, sw, z).astype(_MX), jnp.where(lo, z, a).astype(_MX)]]

    def on_rows(a):
        at = a.T.astype(_MX)
        z = jnp.zeros((64, at.shape[1]), _MX)
        top, bot = at[0:64], at[64:128]
        return [[jnp.concatenate([top, z], axis=0), jnp.concatenate([z, top], axis=0)],
                [jnp.concatenate([bot, z], axis=0), jnp.concatenate([z, bot], axis=0)]]

    ops = dict(k_lanes=on_lanes(kf), v_rows=on_rows(vf), lo=lo)
    if want_bwd:
        ops.update(v_lanes=on_lanes(vf), k_rows=on_rows(kf))
    return ops


def _attn_valid(n):
    kj = lax.broadcasted_iota(jnp.int32, (2 * BLOCK, 2 * BLOCK), 0) - BLOCK
    qi = lax.broadcasted_iota(jnp.int32, (2 * BLOCK, 2 * BLOCK), 1) & (BLOCK - 1)
    return (kj <= qi) & (qi - kj < BLOCK) & ((n > 0) | (kj >= 0))


def _attn_sinks(sink_ref, h_a, h_b):
    first = lax.broadcasted_iota(jnp.int32, (1, 2 * BLOCK), 1) < BLOCK
    return jnp.where(first, sink_ref[h_a], sink_ref[h_b])


def _attn_softmax_t(k_lanes, q_pair, valid, sink):
    s = jnp.where(valid, _dot_nt(k_lanes, q_pair), NEG_BIG)
    m = jnp.maximum(jnp.max(s, axis=0, keepdims=True), sink)
    e = jnp.exp(s - m)
    e_sink = jnp.exp(sink - m)
    inv = 1.0 / (jnp.sum(e, axis=0, keepdims=True) + e_sink)
    return e, e_sink, inv


ATTN_TILE = 4


def _attn_kv(qkv_ref, kvp_ref, j):
    rows = slice(j * BLOCK, (j + 1) * BLOCK)
    if j == 0:
        k_prev, v_prev = kvp_ref[:, 0:128], kvp_ref[:, 128:256]
    else:
        before = slice((j - 1) * BLOCK, j * BLOCK)
        k_prev, v_prev = qkv_ref[before, 512:640], qkv_ref[before, 640:768]
    return k_prev, qkv_ref[rows, 512:640], v_prev, qkv_ref[rows, 640:768]


def _attn_fwd(qkv, za, sinks, nseq):
    t = qkv.shape[0]
    tile = ATTN_TILE * BLOCK
    nt = t // nseq // tile

    def body(sink_ref, qkv_ref, kvp_ref, za_ref, oa_ref):
        for j in range(ATTN_TILE):
            rows = slice(j * BLOCK, (j + 1) * BLOCK)
            ops = _attn_operands(*_attn_kv(qkv_ref, kvp_ref, j), False)
            valid = _attn_valid(ATTN_TILE * pl.program_id(1) + j)[:, 0:BLOCK]
            for pr in range(A_HEADS // 2):
                lanes = slice(pr * LANE, (pr + 1) * LANE)
                g = pr // (A_GROUP // 2)
                q_pair = qkv_ref[rows, lanes]
                ot = None
                for half in range(2):
                    e, _, inv = _attn_softmax_t(ops["k_lanes"][g][half], q_pair, valid, sink_ref[2 * pr + half])
                    part = _dot(ops["v_rows"][g][half], e) * inv
                    ot = part if ot is None else ot + part
                z = za_ref[rows, lanes].astype(_F32)
                oa_ref[rows, lanes] = (ot.T * (z * _sigmoid(z))).astype(_ST)

    cur = lambda w: pl.BlockSpec((tile, w), lambda s, n: (s * nt + n, 0))
    return pl.pallas_call(
        body, name="attn_fwd", grid=(nseq, nt),
        in_specs=[pl.BlockSpec(memory_space=pltpu.SMEM), cur(768),
                  pl.BlockSpec((BLOCK, 256), lambda s, n: (ATTN_TILE * (s * nt + n) - jnp.minimum(n, 1), 2)),
                  cur(512)],
        out_specs=cur(512), out_shape=jax.ShapeDtypeStruct((t, A_WIDTH), _ST),
        compiler_params=_cparams(("parallel", "arbitrary")),
    )(sinks, qkv, qkv, za)


def _attn_bwd(qkv, za, doa, sinks, cos, sa, sb, nseq):
    t = qkv.shape[0]
    tile = ATTN_TILE * BLOCK
    nt = t // nseq // tile

    def body(sink_ref, qkv_ref, kvp_ref, za_ref, doa_ref, cos_ref, sa_ref, sb_ref,
             dqkv_ref, dza_ref, dsink_ref, ck_ref, cv_ref):
        s_id, i = pl.program_id(0), pl.program_id(1)

        @pl.when((s_id == 0) & (i == 0))
        def _():
            dsink_ref[...] = jnp.zeros_like(dsink_ref)

        @pl.when(i == 0)
        def _():
            ck_ref[...] = jnp.zeros_like(ck_ref)
            cv_ref[...] = jnp.zeros_like(cv_ref)

        carry_k, carry_v = ck_ref[...], cv_ref[...]
        for j in reversed(range(ATTN_TILE)):
            rows = slice(j * BLOCK, (j + 1) * BLOCK)
            ops = _attn_operands(*_attn_kv(qkv_ref, kvp_ref, j), True)
            lo = ops["lo"]
            valid = _attn_valid(ATTN_TILE * (nt - 1 - i) + j)
            dk_acc, dv_acc, dq_pairs = [], [], []
            for g in range(A_KV_HEADS):
                pairs = [slice((2 * g + p) * LANE, (2 * g + p + 1) * LANE) for p in range(2)]
                q_both = jnp.concatenate([qkv_ref[rows, p] for p in pairs], axis=0)
                q_f = q_both.astype(_F32)
                z = [za_ref[rows, p].astype(_F32) for p in pairs]
                sz = [_sigmoid(v) for v in z]
                d_oa = [doa_ref[rows, p].astype(_F32) for p in pairs]
                d_att = jnp.concatenate([d_oa[p] * (z[p] * sz[p]) for p in range(2)], axis=0)
                zero = jnp.zeros_like(d_att)
                ot, dqt, ds_all, pn_all, qz_all, daz_all = None, None, [], [], [], []
                for half in range(2):
                    heads = (4 * g + half, 4 * g + 2 + half)
                    e, e_sink, inv = _attn_softmax_t(ops["k_lanes"][g][half], q_both, valid,
                                                     _attn_sinks(sink_ref, *heads))
                    pn = e * inv
                    dpt = _dot_nt(ops["v_lanes"][g][half], d_att)
                    delta = jnp.sum(pn * dpt, axis=0, keepdims=True)
                    ds = (pn * (dpt - delta)).astype(_MX)
                    pn = pn.astype(_MX)
                    d_sink = e_sink * inv * delta
                    for p, h in enumerate(heads):
                        dsink_ref[h:h + 1, :] = dsink_ref[h:h + 1, :] - jnp.sum(d_sink[:, p * BLOCK:(p + 1) * BLOCK])
                    o_part = _dot(ops["v_rows"][g][half], pn)
                    dq_part = _dot(ops["k_rows"][g][half], ds)
                    ot = o_part if ot is None else ot + o_part
                    dqt = dq_part if dqt is None else dqt + dq_part
                    mine = lo if half == 0 else jnp.logical_not(lo)
                    ds_all.append(ds)
                    pn_all.append(pn)
                    qz_all.append(jnp.where(mine, q_f, zero).astype(_MX))
                    daz_all.append(jnp.where(mine, d_att, zero).astype(_MX))
                dk_acc.append(_dot(jnp.concatenate(ds_all, axis=1), jnp.concatenate(qz_all, axis=0)))
                dv_acc.append(_dot(jnp.concatenate(pn_all, axis=1), jnp.concatenate(daz_all, axis=0)))
                for p, lanes in enumerate(pairs):
                    cols = slice(p * BLOCK, (p + 1) * BLOCK)
                    dza_ref[rows, lanes] = (d_oa[p] * ot[:, cols].T * (sz[p] * (1.0 + z[p] * (1.0 - sz[p])))).astype(_ST)
                    dq_pairs.append(dqt[:, cols].T)

            def fold(acc, scale):
                both = [a + pltpu.roll(a, 64, 1) for a in acc]
                return jnp.where(lo, both[0], both[1]) * scale

            dk_full = fold(dk_acc, A_HEAD_DIM ** -0.5)
            dv_full = fold(dv_acc, 1.0)
            dk_cur, dv_cur = dk_full[BLOCK:] + carry_k, dv_full[BLOCK:] + carry_v
            carry_k, carry_v = dk_full[:BLOCK], dv_full[:BLOCK]
            c, s1, s2 = cos_ref[rows, :], sa_ref[rows, :], sb_ref[rows, :]
            dqkv_ref[rows, 0:512] = _rope(jnp.concatenate(dq_pairs, axis=1), c, s1, s2, -1.0).astype(_ST)
            dqkv_ref[rows, 512:640] = _rope(dk_cur, c, s1, s2, -1.0).astype(_ST)
            dqkv_ref[rows, 640:768] = dv_cur.astype(_ST)
        ck_ref[...] = carry_k
        cv_ref[...] = carry_v

    cur = lambda w: pl.BlockSpec((tile, w), lambda s, i: (s * nt + nt - 1 - i, 0))
    return pl.pallas_call(
        body, name="attn_bwd", grid=(nseq, nt),
        in_specs=[pl.BlockSpec(memory_space=pltpu.SMEM), cur(768),
                  pl.BlockSpec((BLOCK, 256),
                               lambda s, i: (ATTN_TILE * (s * nt + nt - 1 - i) - jnp.minimum(nt - 1 - i, 1), 2)),
                  cur(512), cur(512), cur(LANE), cur(LANE), cur(LANE)],
        out_specs=[cur(768), cur(512), pl.BlockSpec((8, LANE), lambda s, i: (0, 0))],
        out_shape=[jax.ShapeDtypeStruct((t, 768), _ST), jax.ShapeDtypeStruct((t, 512), _ST),
                   jax.ShapeDtypeStruct((8, LANE), _F32)],
        scratch_shapes=[pltpu.VMEM((BLOCK, A_KV_WIDTH), _F32), pltpu.VMEM((BLOCK, A_KV_WIDTH), _F32)],
        compiler_params=_cparams(("arbitrary", "arbitrary")),
    )(sinks, qkv, qkv, za, doa, cos, sa, sb)


def _gla_chunk_terms(la, qkb_ref, r0):
    g = la[r0:r0 + B_CHUNK, :]
    ri = lax.broadcasted_iota(jnp.int32, (B_CHUNK, B_CHUNK), 0)
    ci = lax.broadcasted_iota(jnp.int32, (B_CHUNK, B_CHUNK), 1)
    cum = _dot_ones((ri >= ci).astype(_F32), g)
    last = cum[B_CHUNK - 1:B_CHUNK, :]
    mid = cum[B_CHUNK // 2 - 1:B_CHUNK // 2, :]
    q = qkb_ref[r0:r0 + B_CHUNK, 0:B_KEY_WIDTH].astype(_F32) * (B_KEY_DIM ** -0.5)
    k = qkb_ref[r0:r0 + B_CHUNK, B_KEY_WIDTH:2 * B_KEY_WIDTH].astype(_F32)
    e_q, e_k, e_l, e_c = jnp.exp(cum - mid), jnp.exp(mid - cum), jnp.exp(last - cum), jnp.exp(cum)
    dec_col = jnp.exp(jnp.sum(g.T, axis=1, keepdims=True))
    return dict(qm=q * e_q, km=k * e_k, kl=k * e_l, qc=q * e_c, e_q=e_q, e_k=e_k, e_l=e_l, e_c=e_c,
                dec_col=dec_col, dec_row=jnp.exp(last), causal=ri >= ci, ri=ri)


def _gate_logits(alr_ref, wup_ref, b_ref):
    return _dot(alr_ref[...], wup_ref[...]) + b_ref[...]


def _gla_fwd(qkb, vb, zb, alr, wup, b_alpha, gn, nseq):
    t = qkb.shape[0]
    tb = min(GLA_BLOCK, t // nseq)
    nblk = t // nseq // tb
    cpb = tb // B_CHUNK

    def body(qkb_ref, vb_ref, zb_ref, alr_ref, wup_ref, b_ref, gn_ref, ob_ref, oraw_ref, sst_ref, s_ref):
        @pl.when(pl.program_id(1) == 0)
        def _():
            s_ref[...] = jnp.zeros_like(s_ref)

        la = _log_sigmoid(_gate_logits(alr_ref, wup_ref, b_ref)) * (1.0 / B_GATE_TEMP)
        terms = [_gla_chunk_terms(la, qkb_ref, c * B_CHUNK) for c in range(cpb)]
        o_intra, inc = {}, {}
        for c, tm in enumerate(terms):
            for h in range(B_HEADS):
                kl_, vl_ = slice(h * 64, (h + 1) * 64), slice(h * 128, (h + 1) * 128)
                v = vb_ref[c * B_CHUNK:(c + 1) * B_CHUNK, vl_]
                a = jnp.where(tm["causal"], _dot_nt(tm["qm"][:, kl_], tm["km"][:, kl_]), 0.0)
                o_intra[c, h] = _dot(a, v)
                inc[c, h] = _dot_tn(tm["kl"][:, kl_], v)
        o_heads = {}
        for h in range(B_HEADS):
            kl_ = slice(h * 64, (h + 1) * 64)
            st = s_ref[kl_, :]
            for c, tm in enumerate(terms):
                sst_ref[c, kl_, :] = st
                o_heads[c, h] = o_intra[c, h] + _dot(tm["qc"][:, kl_], st)
                st = tm["dec_col"][kl_, :] * st + inc[c, h]
            s_ref[kl_, :] = st
        o = jnp.concatenate([jnp.concatenate([o_heads[c, h] for h in range(B_HEADS)], axis=1)
                             for c in range(cpb)], axis=0)
        oraw_ref[...] = o
        z = zb_ref[...].astype(_F32)
        gate = z * _sigmoid(z)
        for h in range(B_HEADS):
            vl_ = slice(h * 128, (h + 1) * 128)
            oh = o[:, vl_]
            r = lax.rsqrt(jnp.mean(oh * oh, axis=-1, keepdims=True) + NORM_EPS)
            ob_ref[:, vl_] = ((oh * r) * gn_ref[:, vl_] * gate[:, vl_]).astype(_ST)

    rows = lambda w: pl.BlockSpec((tb, w), lambda s, i: (s * nblk + i, 0))
    full = lambda a, b: pl.BlockSpec((a, b), lambda s, i: (0, 0))
    return pl.pallas_call(
        body, name="gla_fwd", grid=(nseq, nblk),
        in_specs=[rows(512), rows(512), rows(512), rows(LANE), full(LANE, B_KEY_WIDTH),
                  full(1, B_KEY_WIDTH), full(1, B_WIDTH)],
        out_specs=[rows(512), rows(512),
                   pl.BlockSpec((cpb, B_KEY_WIDTH, B_VAL_DIM), lambda s, i: (s * nblk + i, 0, 0))],
        out_shape=[jax.ShapeDtypeStruct((t, B_WIDTH), _ST), jax.ShapeDtypeStruct((t, B_WIDTH), _F32),
                   jax.ShapeDtypeStruct((t // B_CHUNK, B_KEY_WIDTH, B_VAL_DIM), _F32)],
        scratch_shapes=[pltpu.VMEM((B_KEY_WIDTH, B_VAL_DIM), _F32)],
        compiler_params=_cparams(("parallel", "arbitrary")),
    )(qkb, vb, zb, alr, wup, b_alpha, gn)


def _gla_bwd(qkb, vb, zb, alr, oraw, dob, sst, wup, b_alpha, gn, nseq):
    t = qkb.shape[0]
    tb = min(GLA_BLOCK, t // nseq)
    nblk = t // nseq // tb
    cpb = tb // B_CHUNK

    def body(qkb_ref, vb_ref, zb_ref, alr_ref, oraw_ref, dob_ref, sst_ref, wup_ref, b_ref, gn_ref,
             dqkb_ref, dvb_ref, dzb_ref, dalr_ref, dwup_ref, db_ref, dgn_ref, ds_ref):
        s_id, i = pl.program_id(0), pl.program_id(1)

        @pl.when((s_id == 0) & (i == 0))
        def _():
            dwup_ref[...] = jnp.zeros_like(dwup_ref)
            db_ref[...] = jnp.zeros_like(db_ref)
            dgn_ref[...] = jnp.zeros_like(dgn_ref)

        @pl.when(i == 0)
        def _():
            ds_ref[...] = jnp.zeros_like(ds_ref)

        a_pre = _gate_logits(alr_ref, wup_ref, b_ref)
        la = _log_sigmoid(a_pre) * (1.0 / B_GATE_TEMP)

        z = zb_ref[...].astype(_F32)
        sz = _sigmoid(z)
        d_ob = dob_ref[...].astype(_F32)
        tg = d_ob * (z * sz)
        dsilu = sz * (1.0 + z * (1.0 - sz))
        do_cols, dgn_cols = [], []
        for h in range(B_HEADS):
            vl_ = slice(h * 128, (h + 1) * 128)
            oh = oraw_ref[:, vl_].astype(_F32)
            r = lax.rsqrt(jnp.mean(oh * oh, axis=-1, keepdims=True) + NORM_EPS)
            on = oh * r
            gnh = gn_ref[:, vl_]
            dzb_ref[:, vl_] = (d_ob[:, vl_] * (on * gnh) * dsilu[:, vl_]).astype(_ST)
            dgn_cols.append(jnp.sum(tg[:, vl_] * on, axis=0, keepdims=True))
            do_cols.append(_rms_bwd(tg[:, vl_] * gnh, on, r))
        dgn_ref[...] = dgn_ref[...] + jnp.concatenate(dgn_cols, axis=1)
        d_o = jnp.concatenate(do_cols, axis=1)

        ri = lax.broadcasted_iota(jnp.int32, (tb, tb), 0)
        ci = lax.broadcasted_iota(jnp.int32, (tb, tb), 1)
        same = (ri // B_CHUNK) == (ci // B_CHUNK)
        low = same & (ri >= ci)
        upto_mid = same & ((ci % B_CHUNK) < B_CHUNK // 2)
        sums = _dot_ones(jnp.concatenate([m.astype(_F32) for m in (low, same, upto_mid)], axis=0), la)
        cum, last, mid = sums[0:tb], sums[tb:2 * tb], sums[2 * tb:3 * tb]
        e_q, e_k, e_l, e_c = jnp.exp(cum - mid), jnp.exp(mid - cum), jnp.exp(last - cum), jnp.exp(cum)
        q = qkb_ref[:, 0:B_KEY_WIDTH] * (B_KEY_DIM ** -0.5)
        k = qkb_ref[:, B_KEY_WIDTH:2 * B_KEY_WIDTH]
        qm, km, kl, qc = q * e_q, k * e_k, k * e_l, q * e_c
        lane_head = lax.broadcasted_iota(jnp.int32, (1, B_KEY_WIDTH), 1) // B_KEY_DIM
        d_o_mx = d_o.astype(_MX)

        def on_diagonal(st):
            z = jnp.zeros((B_KEY_DIM, B_VAL_DIM), st.dtype)
            return jnp.concatenate([jnp.concatenate(
                [st[h * B_KEY_DIM:(h + 1) * B_KEY_DIM] if g == h else z for g in range(B_HEADS)], axis=1)
                for h in range(B_HEADS)], axis=0)

        def diagonal_of(full):
            return jnp.concatenate([full[h * B_KEY_DIM:(h + 1) * B_KEY_DIM, h * B_VAL_DIM:(h + 1) * B_VAL_DIM]
                                    for h in range(B_HEADS)], axis=0)

        dqm, dkm, dv_cols = None, None, []
        for h in range(B_HEADS):
            vl_ = slice(h * B_VAL_DIM, (h + 1) * B_VAL_DIM)
            mine = lane_head == h
            qz, kz = jnp.where(mine, qm, 0.0).astype(_MX), jnp.where(mine, km, 0.0).astype(_MX)
            a = jnp.where(low, _dot_nt(qz, kz), 0.0).astype(_MX)
            da = jnp.where(low, _dot_nt(d_o_mx[:, vl_], vb_ref[:, vl_]), 0.0).astype(_MX)
            dqm_h, dkm_h = _dot(da, kz), _dot_tn(da, qz)
            dqm = dqm_h if dqm is None else dqm + dqm_h
            dkm = dkm_h if dkm is None else dkm + dkm_h
            dv_cols.append(_dot_tn(a, d_o_mx[:, vl_]))
        dv = jnp.concatenate(dv_cols, axis=1)

        chunk = [slice(c * B_CHUNK, (c + 1) * B_CHUNK) for c in range(cpb)]
        dqc_rows, g_loc = [], []
        for c in range(cpb):
            dqc_rows.append(_dot_nt(d_o_mx[chunk[c]], on_diagonal(sst_ref[c].astype(_MX))))
            g_loc.append(diagonal_of(_dot_tn(qc[chunk[c]], d_o_mx[chunk[c]])))
        cur = ds_ref[...]
        d_state = [None] * cpb
        for c in reversed(range(cpb)):
            d_state[c] = cur
            cur = g_loc[c] + jnp.exp(jnp.sum(la[chunk[c]].T, axis=1, keepdims=True)) * cur
        ds_ref[...] = cur
        dkl_rows, dv_rows, dlast_rows = [], [], []
        ones8 = jnp.ones((8, B_VAL_DIM), _F32)
        for c in range(cpb):
            dsd = on_diagonal(d_state[c].astype(_MX))
            dkl_c = _dot_nt(vb_ref[chunk[c], :], dsd)
            dkl_rows.append(dkl_c)
            dv_rows.append(_dot(kl[chunk[c]], dsd))
            prod = d_state[c] * sst_ref[c]
            p0 = prod.astype(jnp.bfloat16)
            p1 = (prod - p0.astype(_F32)).astype(jnp.bfloat16)
            p2 = (prod - p0.astype(_F32) - p1.astype(_F32)).astype(jnp.bfloat16)
            ddec = (_dot_nt(ones8, p0) + _dot_nt(ones8, p1) + _dot_nt(ones8, p2))[0:1]
            r_last = c * B_CHUNK + B_CHUNK - 1
            dlast = jnp.sum(dkl_c * kl[chunk[c]], axis=0, keepdims=True) + ddec * jnp.exp(last[r_last:r_last + 1])
            dlast_rows.append(jnp.broadcast_to(dlast, (B_CHUNK, B_KEY_WIDTH)))
        dqc, dkl = jnp.concatenate(dqc_rows, axis=0), jnp.concatenate(dkl_rows, axis=0)
        dqkb_ref[:, 0:B_KEY_WIDTH] = ((dqm * e_q + dqc * e_c) * (B_KEY_DIM ** -0.5)).astype(_ST)
        dqkb_ref[:, B_KEY_WIDTH:2 * B_KEY_WIDTH] = (dkm * e_k + dkl * e_l).astype(_ST)
        dvb_ref[...] = (dv + jnp.concatenate(dv_rows, axis=0)).astype(_ST)
        dcum = dqm * qm - dkm * km + dqc * qc - dkl * kl
        row = lax.broadcasted_iota(jnp.int32, (tb, B_KEY_WIDTH), 0)
        dcum = jnp.where(row % B_CHUNK == B_CHUNK - 1, dcum + jnp.concatenate(dlast_rows, axis=0), dcum)
        dla = _dot_ones((same & (ri <= ci)).astype(_F32), dcum)

        da_pre = dla * (1.0 / B_GATE_TEMP) * (1.0 - _sigmoid(a_pre))
        dalr_ref[...] = _dot_nt(da_pre, wup_ref[...]).astype(_ST)
        dwup_ref[...] = dwup_ref[...] + _dot_tn(alr_ref[...], da_pre)
        db_ref[...] = db_ref[...] + jnp.sum(da_pre, axis=0, keepdims=True)

    blk = lambda s, i: s * nblk + nblk - 1 - i
    rows = lambda w: pl.BlockSpec((tb, w), lambda s, i: (blk(s, i), 0))
    full = lambda a, b: pl.BlockSpec((a, b), lambda s, i: (0, 0))
    act = lambda w: jax.ShapeDtypeStruct((t, w), _ST)
    return pl.pallas_call(
        body, name="gla_bwd", grid=(nseq, nblk),
        in_specs=[rows(512), rows(512), rows(512), rows(LANE), rows(512), rows(512),
                  pl.BlockSpec((cpb, B_KEY_WIDTH, B_VAL_DIM), lambda s, i: (blk(s, i), 0, 0)),
                  full(LANE, B_KEY_WIDTH), full(1, B_KEY_WIDTH), full(1, B_WIDTH)],
        out_specs=[rows(512), rows(512), rows(512), rows(LANE), full(LANE, B_KEY_WIDTH),
                   full(1, B_KEY_WIDTH), full(1, B_WIDTH)],
        out_shape=[act(512), act(512), act(512), act(LANE),
                   jax.ShapeDtypeStruct((LANE, B_KEY_WIDTH), _F32),
                   jax.ShapeDtypeStruct((1, B_KEY_WIDTH), _F32), jax.ShapeDtypeStruct((1, B_WIDTH), _F32)],
        scratch_shapes=[pltpu.VMEM((B_KEY_WIDTH, B_VAL_DIM), _F32)],
        compiler_params=_cparams(("arbitrary", "arbitrary")),
    )(qkb, vb, zb, alr, oraw, dob, sst, wup, b_alpha, gn)


def _merge_loss(oa, ob, ga, gb, x2, tgt, wa, wb, wo, g_final):
    t = x2.shape[0]
    tm = min(t, 512)
    nt = t // tm

    def body(oa_ref, ob_ref, ga_ref, gb_ref, x_ref, t_ref, wa_ref, wb_ref, wo_ref, gf_ref,
             dh_ref, doa_ref, dob_ref, dga_ref, dgb_ref, dwa_ref, dwb_ref, dwo_ref, dgf_ref, loss_ref,
             ya_s, yb_s, out_s, dmer_s, mrg_s, dya_s, dyb_s):
        first = pl.program_id(0) == 0
        so_far = lambda ref: jnp.where(first, 0.0, ref[...])

        slabs = [slice(s, s + MERGE_SLAB) for s in range(0, tm, MERGE_SLAB)]
        fold = lambda a: a[0:8] + a[8:16]
        ya_s[...] = _dot(oa_ref[...], wa_ref[...])
        yb_s[...] = _dot(ob_ref[...], wb_ref[...])
        for rows_ in slabs:
            sga, sgb = _sigmoid(ga_ref[rows_, :].astype(_F32)), _sigmoid(gb_ref[rows_, :].astype(_F32))
            mrg_s[rows_, :] = (sga * ya_s[rows_, :] + sgb * yb_s[rows_, :]).astype(_MX)
        out_s[...] = x_ref[...] + _dot(mrg_s[...], wo_ref[...])
        gf = gf_ref[...]
        loss8 = jnp.zeros((8, D_MODEL), _F32)
        dgf8 = jnp.zeros((8, D_MODEL), _F32)
        for rows_ in slabs:
            out = out_s[rows_, :]
            r = lax.rsqrt(jnp.mean(out * out, axis=-1, keepdims=True) + NORM_EPS)
            nrm = out * r
            err = nrm * gf - t_ref[rows_, :]
            loss8 = loss8 + fold(err * err)
            dy = err * (1.0 / D_MODEL)
            dgf8 = dgf8 + fold(dy * nrm)
            dh = _rms_bwd(dy * gf, nrm, r)
            dh_ref[rows_, :] = dh.astype(_ST)
        loss_ref[...] = so_far(loss_ref) + (0.5 / D_MODEL) * jnp.sum(loss8, axis=0, keepdims=True)
        dgf_ref[...] = so_far(dgf_ref) + jnp.sum(dgf8, axis=0, keepdims=True)
        dmer_s[...] = _dot_nt(dh_ref[...], wo_ref[...])
        dwo_ref[...] = so_far(dwo_ref) + _dot_tn(mrg_s[...], dh_ref[...])
        for rows_ in slabs:
            sga, sgb = _sigmoid(ga_ref[rows_, :].astype(_F32)), _sigmoid(gb_ref[rows_, :].astype(_F32))
            dmer = dmer_s[rows_, :]
            da, db = dmer * sga, dmer * sgb
            dya_s[rows_, :] = da.astype(_MX)
            dyb_s[rows_, :] = db.astype(_MX)
            dga_ref[rows_, :] = (da * ya_s[rows_, :] * (1.0 - sga)).astype(_ST)
            dgb_ref[rows_, :] = (db * yb_s[rows_, :] * (1.0 - sgb)).astype(_ST)
        doa_ref[...] = _dot_nt(dya_s[...], wa_ref[...]).astype(_ST)
        dob_ref[...] = _dot_nt(dyb_s[...], wb_ref[...]).astype(_ST)
        dwa_ref[...] = so_far(dwa_ref) + _dot_tn(oa_ref[...], dya_s[...])
        dwb_ref[...] = so_far(dwb_ref) + _dot_tn(ob_ref[...], dyb_s[...])

    rows = lambda w: pl.BlockSpec((tm, w), lambda i: (i, 0))
    full = lambda a, b: pl.BlockSpec((a, b), lambda i: (0, 0), pipeline_mode=pl.Buffered(1))
    return pl.pallas_call(
        body, name="merge_loss", grid=(nt,),
        in_specs=[rows(512), rows(512), rows(D_MODEL), rows(D_MODEL), rows(D_MODEL), rows(D_MODEL),
                  full(A_WIDTH, D_MODEL), full(B_WIDTH, D_MODEL), full(D_MODEL, D_MODEL), full(1, D_MODEL)],
        out_specs=[rows(D_MODEL), rows(512), rows(512), rows(D_MODEL), rows(D_MODEL),
                   full(A_WIDTH, D_MODEL), full(B_WIDTH, D_MODEL), full(D_MODEL, D_MODEL),
                   full(1, D_MODEL), full(1, D_MODEL)],
        out_shape=[jax.ShapeDtypeStruct((t, D_MODEL), _ST), jax.ShapeDtypeStruct((t, 512), _ST),
                   jax.ShapeDtypeStruct((t, 512), _ST), jax.ShapeDtypeStruct((t, D_MODEL), _ST),
                   jax.ShapeDtypeStruct((t, D_MODEL), _ST),
                   jax.ShapeDtypeStruct((A_WIDTH, D_MODEL), _F32), jax.ShapeDtypeStruct((B_WIDTH, D_MODEL), _F32),
                   jax.ShapeDtypeStruct((D_MODEL, D_MODEL), _F32), jax.ShapeDtypeStruct((1, D_MODEL), _F32),
                   jax.ShapeDtypeStruct((1, D_MODEL), _F32)],
        scratch_shapes=[pltpu.VMEM((tm, D_MODEL), _F32)] * 4 + [pltpu.VMEM((tm, D_MODEL), _MX)] * 3,
        compiler_params=_cparams(("arbitrary",), VMEM_LIMIT),
    )(oa, ob, ga, gb, x2, tgt, wa, wb, wo, g_final)


def _in_proj_bwd_x(dpieces, wt, x2, dh2, g_in):
    t = x2.shape[0]
    tm = min(t, 512)
    np_ = len(PIECES)

    def body(*refs):
        dp_refs = refs[:np_]
        w_ref, x_ref, dh2_ref, g_ref, gx_ref, dg_ref = refs[np_:]

        @pl.when(pl.program_id(0) == 0)
        def _():
            dg_ref[...] = jnp.zeros_like(dg_ref)

        dh = None
        for (name, a, b), dp in zip(PIECES, dp_refs):
            part = _dot(dp[...], w_ref[a:b, :])
            dh = part if dh is None else dh + part
        xv = x_ref[...]
        r = lax.rsqrt(jnp.mean(xv * xv, axis=-1, keepdims=True) + NORM_EPS)
        nrm = xv * r
        dg_ref[...] = dg_ref[...] + jnp.sum(dh * nrm, axis=0, keepdims=True)
        gx_ref[...] = dh2_ref[...].astype(_F32) + _rms_bwd(dh * g_ref[...], nrm, r)

    rows = lambda w: pl.BlockSpec((tm, w), lambda i: (i, 0))
    full = lambda a, b: pl.BlockSpec((a, b), lambda i: (0, 0), pipeline_mode=pl.Buffered(1))
    return pl.pallas_call(
        body, name="in_proj_bwd_x", grid=(t // tm,),
        in_specs=[rows(b - a) for _, a, b in PIECES] + [full(D_IN, D_MODEL), rows(D_MODEL), rows(D_MODEL),
                                                          full(1, D_MODEL)],
        out_specs=[rows(D_MODEL), full(1, D_MODEL)],
        out_shape=[jax.ShapeDtypeStruct((t, D_MODEL), _F32), jax.ShapeDtypeStruct((1, D_MODEL), _F32)],
        compiler_params=_cparams(("arbitrary",), VMEM_LIMIT),
    )(*dpieces, wt, x2, dh2, g_in)


def _in_proj_bwd_w(h, dpieces):
    t = h.shape[0]
    tm = min(t, 1024)
    nt = t // tm
    np_ = len(PIECES)

    def body(*refs):
        h_ref, dp_refs, out_ref = refs[0], refs[1:1 + np_], refs[1 + np_]
        acc_ref, sem = refs[2 + np_:]
        i = pl.program_id(0)
        hv = h_ref[...]
        writes = []
        for j, ((name, a, b), dp) in enumerate(zip(PIECES, dp_refs)):
            part = _dot_tn(dp[...], hv)
            if name == "alr":
                b = a + B_GATE_RANK
                part = part[0:B_GATE_RANK]
            acc_ref[a:b, :] = jnp.where(i == 0, 0.0, acc_ref[a:b, :]) + part
            writes.append(pltpu.make_async_copy(acc_ref.at[a:b], out_ref.at[a:b], sem.at[j]))

            @pl.when(i == nt - 1)
            def _(cp=writes[-1]):
                cp.start()

        @pl.when(i == nt - 1)
        def _():
            for cp in writes:
                cp.wait()

    rows = lambda w: pl.BlockSpec((tm, w), lambda i: (i, 0))
    return pl.pallas_call(
        body, name="in_proj_bwd_w", grid=(nt,),
        in_specs=[rows(D_MODEL)] + [rows(b - a) for _, a, b in PIECES],
        out_specs=_ANY, out_shape=jax.ShapeDtypeStruct((D_IN, D_MODEL), _F32),
        scratch_shapes=[pltpu.VMEM((D_IN, D_MODEL), _F32), pltpu.SemaphoreType.DMA((np_,))],
        compiler_params=_cparams(("arbitrary",), VMEM_LIMIT),
    )(h, *dpieces)


def _place():
    return lax.axis_index("x"), lax.axis_index("y"), lax.axis_index("c")


def _other_chips(x, y):
    return [(1 - x, y), (x, 1 - y), (1 - x, 1 - y)]


class _Split(NamedTuple):
    by_rows: bool
    step: int
    size: int

    def half(self, ref, c):
        r, n = ref.shape[-2:]
        if self.by_rows:
            return ref.at[:, pl.ds(pl.multiple_of(c * (n // 2), LANE), n // 2)]
        return ref.at[pl.ds(pl.multiple_of(c * (r // 2), 16), r // 2), :]

    def chip_part(self, ref, k):
        if self.by_rows:
            return ref.at[pl.ds(pl.multiple_of(k * self.step, 16), self.size), :]
        return ref.at[:, pl.ds(pl.multiple_of(k * self.size, LANE), self.size)]

    def half_shape(self, shape):
        r, n = shape
        return (r, n // 2) if self.by_rows else (r // 2, n)

    def part_shape(self, shape):
        r, n = shape
        return (self.size, n) if self.by_rows else (r, self.size)


SPLIT_W_IN_T = _Split(True, WINDOW_STEP, WINDOW_ROWS)
SPLIT_W_O = _Split(True, 256, 256)
SPLIT_W_OUT = _Split(False, 256, 256)


def _gather_weights(shards, splits, fulls, pos_f):
    nw = len(shards)
    t = pos_f.shape[0]

    def body(*refs):
        ins, (pos_ref, c_ref) = refs[:nw], refs[nw:nw + 2]
        outs, tables = refs[nw + 2:2 * nw + 2], refs[2 * nw + 2:2 * nw + 5]
        send_a, recv_a, send_b, recv_b = refs[2 * nw + 5:]
        x, y, c = _place()
        me = 2 * x + y
        peers = _other_chips(x, y)

        def place(i, k, half):
            if splits[i] is None:
                return outs[i].at[k]
            if fulls[i][0] == 4 and len(fulls[i]) == 3:
                whole = outs[i].at[k]
            else:
                whole = splits[i].chip_part(outs[i], k)
            return splits[i].half(whole, half)

        first, passed = [], []
        for i in range(nw):
            src = ins[i] if splits[i] is None else splits[i].half(ins[i], c)
            for j, (px, py) in enumerate(peers):
                cp = pltpu.make_async_remote_copy(
                    src_ref=src, dst_ref=place(i, me, c), send_sem=send_a.at[3 * i + j],
                    recv_sem=recv_a.at[3 * i + j], device_id=(px, py, c), device_id_type=_MESH)
                cp.start()
                first.append(cp)
        _rope_tables_into(pos_ref, c_ref, *tables)
        for i in range(nw):
            for j, (px, py) in enumerate(peers):
                landed = place(i, 2 * px + py, c)
                pltpu.make_async_remote_copy(
                    src_ref=landed, dst_ref=landed, send_sem=send_a.at[3 * i + j], recv_sem=recv_a.at[3 * i + j],
                    device_id=(px, py, c), device_id_type=_MESH).wait_recv()
                if splits[i] is not None:
                    cp = pltpu.make_async_remote_copy(
                        src_ref=landed, dst_ref=landed, send_sem=send_b.at[3 * i + j], recv_sem=recv_b.at[3 * i + j],
                        device_id=(x, y, 1 - c), device_id_type=_MESH)
                    cp.start()
                    passed.append(cp)
        for i in range(nw):
            if splits[i] is None:
                continue
            for j, (px, py) in enumerate(peers):
                theirs = place(i, 2 * px + py, 1 - c)
                pltpu.make_async_remote_copy(
                    src_ref=theirs, dst_ref=theirs, send_sem=send_b.at[3 * i + j], recv_sem=recv_b.at[3 * i + j],
                    device_id=(x, y, 1 - c), device_id_type=_MESH).wait_recv()
        for cp in first + passed:
            cp.wait_send()

    vm = pl.BlockSpec(memory_space=pltpu.VMEM)
    tab = jax.ShapeDtypeStruct((t, LANE), _F32)
    return pl.pallas_call(
        body, name="gather_weights",
        in_specs=[_ANY] * nw + [vm, vm], out_specs=[_ANY] * nw + [vm] * 3,
        out_shape=[jax.ShapeDtypeStruct(f, s.dtype) for f, s in zip(fulls, shards)] + [tab] * 3,
        scratch_shapes=[pltpu.SemaphoreType.DMA((3 * nw,)) for _ in range(4)],
        compiler_params=_cparams(None, VMEM_LIMIT),
    )(*shards, pos_f, _rope_consts())


def _assemble_w_in_t(slots):
    bw = 256
    ov = WINDOW_ROWS - WINDOW_STEP

    def body(s_ref, o_ref):
        for k in range(4):
            base = k * WINDOW_STEP
            lo = 0 if k == 0 else ov
            if k > 0:
                o_ref[base:base + ov, :] = s_ref[k - 1, WINDOW_STEP:WINDOW_ROWS, :] + s_ref[k, 0:ov, :]
            hi = WINDOW_ROWS if k == 3 else WINDOW_STEP
            o_ref[base + lo:base + hi, :] = s_ref[k, lo:hi, :]

    return pl.pallas_call(
        body, name="assemble_w_in_t", grid=(D_MODEL // bw,),
        in_specs=[pl.BlockSpec((4, WINDOW_ROWS, bw), lambda i: (0, 0, i))],
        out_specs=pl.BlockSpec((D_IN, bw), lambda i: (0, i)),
        out_shape=jax.ShapeDtypeStruct((D_IN, D_MODEL), slots.dtype),
        compiler_params=_cparams(("parallel",)),
    )(slots)


def _pair_exchange(grads, splits):
    nw = len(grads)

    def body(*refs):
        ins, outs = refs[:nw], refs[nw:2 * nw]
        send, recv = refs[2 * nw:]
        x, y, c = _place()
        copies = []
        for i in range(nw):
            cp = pltpu.make_async_remote_copy(
                src_ref=splits[i].half(ins[i], 1 - c), dst_ref=outs[i], send_sem=send.at[i], recv_sem=recv.at[i],
                device_id=(x, y, 1 - c), device_id_type=_MESH)
            cp.start()
            copies.append(cp)
        for cp in copies:
            cp.wait()

    return pl.pallas_call(
        body, name="grad_pair_exchange",
        in_specs=[_ANY] * nw, out_specs=[_ANY] * nw,
        out_shape=[jax.ShapeDtypeStruct(sp.half_shape(g.shape), g.dtype) for g, sp in zip(grads, splits)],
        scratch_shapes=[pltpu.SemaphoreType.DMA((nw,)), pltpu.SemaphoreType.DMA((nw,))],
    )(*grads)


def _row_block(rows):
    for cand in (976, 176, 256, 128):
        if rows % cand == 0:
            return cand
    return rows


def _pair_sum(g, r, split, c_arr, name):
    hr, hn = r.shape
    br = _row_block(hr)
    if split.by_rows:
        g_spec = pl.BlockSpec((br, hn), lambda i, c_ref: (i, c_ref[0]))
    else:
        g_spec = pl.BlockSpec((br, hn), lambda i, c_ref: (c_ref[0] * (hr // br) + i, 0))

    def body(c_ref, g_ref, r_ref, o_ref):
        o_ref[...] = (g_ref[...] + r_ref[...]).astype(o_ref.dtype)

    return pl.pallas_call(
        body, name=name,
        grid_spec=pltpu.PrefetchScalarGridSpec(
            num_scalar_prefetch=1, grid=(hr // br,),
            in_specs=[g_spec, pl.BlockSpec((br, hn), lambda i, c_ref: (i, 0))],
            out_specs=pl.BlockSpec((br, hn), lambda i, c_ref: (i, 0))),
        out_shape=jax.ShapeDtypeStruct(r.shape, _MX),
        compiler_params=_cparams(("parallel",)),
    )(c_arr, g, r)


_HBM = pl.BlockSpec(memory_space=pltpu.HBM)
_SEM = pl.BlockSpec(memory_space=pltpu.SEMAPHORE)
_FLOWS = pltpu.SideEffectType.DATAFLOW_SIDE_EFFECTING


def _chip_exchange_copies(refs, send, recv, splits):
    nw = len(refs) // 2
    x, y, c = _place()
    me = 2 * x + y
    copies = []
    for i in range(nw):
        for px, py in _other_chips(x, y):
            copies.append((splits[i].chip_part(refs[i], 2 * px + py), refs[nw + i].at[me], (px, py, c)))
    return [pltpu.make_async_remote_copy(src_ref=src, dst_ref=dst, send_sem=send.at[k], recv_sem=recv.at[k],
                                         device_id=peer, device_id_type=_MESH)
            for k, (src, dst, peer) in enumerate(copies)]


def _late_gather_copies(refs, send, recv, splits):
    nw = len(refs) // 2
    x, y, c = _place()
    me = 2 * x + y
    copies = []
    for i in range(nw):
        for px, py in _other_chips(x, y):
            copies.append((refs[i], splits[i].chip_part(refs[nw + i], me), (px, py, c)))
    return [pltpu.make_async_remote_copy(src_ref=src, dst_ref=dst, send_sem=send.at[k], recv_sem=recv.at[k],
                                         device_id=peer, device_id_type=_MESH)
            for k, (src, dst, peer) in enumerate(copies)]


def _start_copies(name, flying, copies_of, n_copies, after=None):
    first = [] if after is None else [after]

    def body(*refs):
        ins = refs[:len(flying)]
        send, recv = refs[len(flying) + len(first):len(flying) + len(first) + 2]
        token = refs[-1]
        for cp in copies_of(ins, send, recv):
            cp.start()
        token[...] = jnp.zeros_like(token)

    outs = pl.pallas_call(
        body, name=name,
        in_specs=[_HBM] * len(flying) + [_ANY] * len(first),
        out_specs=[_SEM, _SEM] + [_HBM] * len(flying) + [pl.BlockSpec(memory_space=pltpu.VMEM)],
        out_shape=[pltpu.SemaphoreType.DMA((n_copies,)), pltpu.SemaphoreType.DMA((n_copies,))]
        + [pltpu.HBM(f.shape, f.dtype) for f in flying] + [jax.ShapeDtypeStruct((8, LANE), _F32)],
        input_output_aliases={i: 2 + i for i in range(len(flying))},
        compiler_params=pltpu.CompilerParams(has_side_effects=_FLOWS),
    )(*[pltpu.with_memory_space_constraint(f, pltpu.HBM) for f in flying], *first)
    return outs[0], outs[1], outs[2:2 + len(flying)], outs[-1]


def _wait_copies(name, send, recv, flying, copies_of, after):
    def body(*refs):
        ins = refs[:len(flying)]
        send_ref, recv_ref = refs[len(flying):len(flying) + 2]
        for cp in copies_of(ins, send_ref, recv_ref):
            cp.wait_send()
            cp.wait_recv()

    return pl.pallas_call(
        body, name=name,
        in_specs=[_HBM] * len(flying) + [_SEM, _SEM, _ANY],
        out_specs=[_HBM] * len(flying),
        out_shape=[pltpu.HBM(f.shape, f.dtype) for f in flying],
        input_output_aliases={i: i for i in range(len(flying))},
        compiler_params=pltpu.CompilerParams(has_side_effects=_FLOWS),
    )(*flying, send, recv, after)


def _sum_chips(q, p, split, place_arr, name):
    _, hr, hn = q.shape
    if split.by_rows:
        out_shape = (hr, 2 * hn)
        o_spec = pl.BlockSpec((hr, hn), lambda i, pr: (0, pr[0]))
        p_spec = pl.BlockSpec((pl.Element(hr), pl.Element(hn)), lambda i, pr: (pr[1] * split.step, 0))
    else:
        out_shape = (2 * hr, hn)
        o_spec = pl.BlockSpec((hr, hn), lambda i, pr: (pr[0], 0))
        p_spec = pl.BlockSpec((hr, hn), lambda i, pr: (0, pr[1]))

    def body(pr, q_ref, p_ref, o_ref):
        f = lambda k: jnp.where(pr[1] == k, p_ref[...], q_ref[k]).astype(_F32)
        o_ref[...] = ((f(0) + f(1)) + f(2)) + f(3)

    return pl.pallas_call(
        body, name=name,
        grid_spec=pltpu.PrefetchScalarGridSpec(
            num_scalar_prefetch=1, grid=(1,),
            in_specs=[pl.BlockSpec((4, hr, hn), lambda i, pr: (0, 0, 0)), p_spec], out_specs=o_spec),
        out_shape=jax.ShapeDtypeStruct(out_shape, _F32),
        compiler_params=_cparams(("arbitrary",), VMEM_LIMIT),
    )(place_arr, q, p)


def _pair_share(bufs, splits, small):
    nw = len(bufs)

    def body(*refs):
        ins, small_ref, outs, all_ref = refs[:nw], refs[nw], refs[nw + 1:2 * nw + 1], refs[2 * nw + 1]
        send, recv, s_send, s_recv = refs[2 * nw + 2:]
        x, y, c = _place()
        copies = []
        for r in range(1, 8):
            peer = (1 - x if r & 4 else x, 1 - y if r & 2 else y, 1 - c if r & 1 else c)
            cp = pltpu.make_async_remote_copy(
                src_ref=small_ref, dst_ref=all_ref.at[4 * x + 2 * y + c], send_sem=s_send.at[r - 1],
                recv_sem=s_recv.at[r - 1], device_id=peer, device_id_type=_MESH)
            cp.start()
            copies.append(cp)
        for i in range(nw):
            cp = pltpu.make_async_remote_copy(
                src_ref=splits[i].half(ins[i], c), dst_ref=splits[i].half(outs[i], c), send_sem=send.at[i],
                recv_sem=recv.at[i], device_id=(x, y, 1 - c), device_id_type=_MESH)
            cp.start()
            copies.append(cp)
        for cp in copies:
            cp.wait()

    return pl.pallas_call(
        body, name="grad_pair_share",
        in_specs=[_ANY] * (nw + 1), out_specs=[_ANY] * (nw + 1),
        out_shape=[jax.ShapeDtypeStruct(b.shape, b.dtype) for b in bufs]
        + [jax.ShapeDtypeStruct((8,) + small.shape, small.dtype)],
        input_output_aliases={i: i for i in range(nw)},
        scratch_shapes=[pltpu.SemaphoreType.DMA((nw,)), pltpu.SemaphoreType.DMA((nw,)),
                        pltpu.SemaphoreType.DMA((7,)), pltpu.SemaphoreType.DMA((7,))],
    )(*bufs, small)


def _sum_devices(parts, own, dev_arr):
    def body(dev, p_ref, own_ref, tot_ref):
        f = lambda d: jnp.where(dev[0] == d, own_ref[...], p_ref[d])
        acc = f(0)
        for d in range(1, 8):
            acc = acc + f(d)
        tot_ref[...] = acc

    return pl.pallas_call(
        body, name="small_sum",
        grid_spec=pltpu.PrefetchScalarGridSpec(
            num_scalar_prefetch=1, grid=(1,),
            in_specs=[pl.BlockSpec(parts.shape, lambda i, dev: (0, 0, 0)), pl.BlockSpec(own.shape, lambda i, dev: (0, 0))],
            out_specs=pl.BlockSpec(own.shape, lambda i, dev: (0, 0))),
        out_shape=jax.ShapeDtypeStruct(own.shape, own.dtype),
    )(dev_arr, parts, own)


def _adamw(w, g, m, v, name):
    lead = w.shape[0] != 1
    r, n = (w.shape[0], w.shape[2]) if lead else w.shape[1:]
    br = r
    for cand in (256, 244, 128):
        if r > cand and r % cand == 0:
            br = cand
            break

    def body(w_ref, g_ref, m_ref, v_ref, d_ref, nm_ref, nv_ref):
        gv = g_ref[...]
        m2 = ADAM_B1 * m_ref[...] + (1.0 - ADAM_B1) * gv
        v2 = ADAM_B2 * v_ref[...] + (1.0 - ADAM_B2) * (gv * gv)
        m_hat = m2 / (1.0 - ADAM_B1 ** ADAM_STEP)
        v_hat = v2 / (1.0 - ADAM_B2 ** ADAM_STEP)
        d_ref[...] = -ADAM_LR * (m_hat / (jnp.sqrt(v_hat) + ADAM_EPS) + ADAM_WD * w_ref[...])
        nm_ref[...] = m2
        nv_ref[...] = v2

    blk = pl.BlockSpec((br, 1, n), lambda i: (i, 0, 0)) if lead else pl.BlockSpec((None, br, n), lambda i: (0, i, 0))
    shp = jax.ShapeDtypeStruct(w.shape, _F32)
    return pl.pallas_call(
        body, name=name, grid=(r // br,),
        in_specs=[blk] * 4, out_specs=[blk] * 3, out_shape=[shp] * 3,
        compiler_params=_cparams(("parallel",)),
    )(w, g, m, v)


def kernel(x, positions, g_in, w_in, w_alpha_up, b_alpha, attn_sinks, g_gla_norm, w_out_a, w_out_b, w_o, g_final, loss_target, m_g_in, m_w_in, m_w_alpha_up, m_b_alpha, m_attn_sinks, m_g_gla_norm, m_w_out_a, m_w_out_b, m_w_o, m_g_final, v_g_in, v_w_in, v_w_alpha_up, v_b_alpha, v_attn_sinks, v_g_gla_norm, v_w_out_a, v_w_out_b, v_w_o, v_g_final):
    nseq, seq, _ = x.shape
    t = nseq * seq
    cx, cy, cc = _place()
    chip = 2 * cx + cy
    c_arr = jnp.reshape(cc, (1,)).astype(jnp.int32)

    tr = lambda w: jnp.transpose(w, (2, 0, 1))
    w_in_t = tr(w_in).reshape(SHARD, D_MODEL).astype(_MX)
    pad = WINDOW_ROWS - SHARD
    window = lax.switch(chip, [lambda w, k=k: jnp.pad(w, ((4 * k, pad - 4 * k), (0, 0))) for k in range(4)], w_in_t)
    shards = [window, w_alpha_up[0].astype(_MX)]
    late = [w_out_a[0].astype(_MX), w_out_b[0].astype(_MX), w_o[0].astype(_MX)]
    late_splits = [SPLIT_W_OUT, SPLIT_W_OUT, SPLIT_W_O]
    splits = [SPLIT_W_IN_T, None]
    fulls = [(4, WINDOW_ROWS, D_MODEL), (4, B_GATE_RANK, B_KEY_WIDTH // 4)]
    pos_f = positions.astype(_F32).reshape(t, 1)
    win_g, wup_g, cos, sa, sb = _gather_weights(shards, splits, fulls, pos_f)
    late_copies = lambda refs, send, recv: _late_gather_copies(refs, send, recv, late_splits)
    late_full = [lax.empty(shape, _MX) for shape in ((A_WIDTH, D_MODEL), (B_WIDTH, D_MODEL), (D_MODEL, D_MODEL))]
    l_send, l_recv, l_flying, l_token = _start_copies("late_gather_start", late + late_full, late_copies,
                                                      3 * len(late), after=win_g)
    nsh = D_MODEL // 4
    win_g = lax.dynamic_update_slice(win_g, window[None], (chip, 0, 0))
    wup_g = lax.dynamic_update_slice(wup_g, shards[1][None], (chip, 0, 0))
    wt = _assemble_w_in_t(win_g)
    wup = jnp.concatenate([jnp.transpose(wup_g, (1, 0, 2)).reshape(B_GATE_RANK, B_KEY_WIDTH),
                           jnp.zeros((LANE - B_GATE_RANK, B_KEY_WIDTH), _MX)], axis=0)

    x2 = x.reshape(t, D_MODEL)
    tgt = loss_target.reshape(t, D_MODEL)
    sinks = attn_sinks.reshape(A_HEADS)
    gf = g_final.reshape(1, D_MODEL)

    h, qkv, za, qkb, vb, zb, alr, ga, gb = _in_proj(x2, g_in + l_token[0, 0], wt, cos, sa, sb)
    oa = _attn_fwd(qkv, za, sinks, nseq)
    ob, oraw, sst = _gla_fwd(qkb, vb, zb, alr, wup, b_alpha, g_gla_norm, nseq)

    wa, wb, wo = _wait_copies("late_gather_wait", l_send, l_recv, l_flying, late_copies, ob)[len(late):]
    wa = lax.dynamic_update_slice(wa, late[0], (0, nsh * chip))
    wb = lax.dynamic_update_slice(wb, late[1], (0, nsh * chip))
    wo = lax.dynamic_update_slice(wo, late[2], (nsh * chip, 0))
    dh2, doa, dob, dga, dgb, dwa, dwb, dwo, dgf, lossv = _merge_loss(oa, ob, ga, gb, x2, tgt, wa, wb, wo, gf)

    dqkv, dza, dsink = _attn_bwd(qkv, za, doa, sinks, cos, sa, sb, nseq)
    dqkb, dvb, dzb, dalr, dwup, dba, dgn = _gla_bwd(qkb, vb, zb, alr, oraw, dob, sst, wup, b_alpha, g_gla_norm, nseq)
    dpieces = [dqkv, dza, dqkb, dvb, dzb, dalr, dga, dgb]
    dwin_t = _in_proj_bwd_w(h, dpieces)

    grads = [dwin_t, dwa, dwb, dwo]
    gsplits = [SPLIT_W_IN_T, SPLIT_W_OUT, SPLIT_W_OUT, SPLIT_W_O]
    names = ("w_in", "w_out_a", "w_out_b", "w_o")
    from_sibling = _pair_exchange(grads, gsplits)
    pair_sums = [_pair_sum(g, r, sp, c_arr, "pair_sum_" + nm)
                 for g, r, sp, nm in zip(grads, from_sibling, gsplits, names)]
    exchange = lambda refs, send, recv: _chip_exchange_copies(refs, send, recv, gsplits)
    lands = [lax.empty((4,) + sp.part_shape(p.shape), p.dtype) for p, sp in zip(pair_sums, gsplits)]
    send, recv, flying, token = _start_copies("grad_chip_exchange_start", pair_sums + lands, exchange, 3 * len(lands))
    grad_x2, dgin = _in_proj_bwd_x(dpieces, wt, x2, dh2, g_in + token[0, 0])
    from_chips = _wait_copies("grad_chip_exchange_wait", send, recv, flying, exchange, grad_x2)[len(lands):]
    place_arr = jnp.stack([cc, chip]).astype(jnp.int32)
    reduced = [_sum_chips(q, p, sp, place_arr, "chip_sum_" + nm)
               for q, p, sp, nm in zip(from_chips, pair_sums, gsplits, names)]
    small = jnp.concatenate([
        dgin, dgf, dgn, dba,
        jnp.pad(dsink[:, 0].reshape(1, A_HEADS), ((0, 0), (0, LANE - A_HEADS))),
        jnp.pad(jnp.sum(lossv, axis=1, keepdims=True), ((0, 0), (0, LANE - 1))),
        dwup[:B_GATE_RANK].reshape(1, B_GATE_RANK * B_KEY_WIDTH)], axis=1).reshape(-1, LANE)
    g_window, g_wa, g_wb, g_wo, small_parts = _pair_share(reduced, gsplits, small)
    g_win_t = lax.switch(chip, [lambda w, k=k: w[4 * k:4 * k + SHARD].reshape(SHARD, 1, D_MODEL) for k in range(4)],
                         g_window)
    dev_arr = jnp.reshape(2 * chip + cc, (1,)).astype(jnp.int32)
    tot = _sum_devices(small_parts, small, dev_arr).reshape(1, -1)
    o = 0
    def take(n):
        nonlocal o
        o += n
        return tot[:, o - n:o]
    g_gin, g_gf, g_gn, g_ba = take(D_MODEL), take(D_MODEL), take(B_WIDTH), take(B_KEY_WIDTH)
    g_sink = take(LANE)[:, :A_HEADS]
    loss = take(LANE)[0, 0]
    g_wup_full = take(B_GATE_RANK * B_KEY_WIDTH).reshape(B_GATE_RANK, B_KEY_WIDTH)
    nup = B_KEY_WIDTH // 4
    g_wup = lax.dynamic_slice(g_wup_full, (0, chip * nup), (B_GATE_RANK, nup))

    def pack(*parts):
        return jnp.concatenate([p.reshape(1, -1) for p in parts], axis=1)

    sm_w = pack(g_in, g_final, g_gla_norm, b_alpha, attn_sinks, w_alpha_up)
    sm_g = pack(g_gin, g_gf, g_gn, g_ba, g_sink, g_wup)
    sm_m = pack(m_g_in, m_g_final, m_g_gla_norm, m_b_alpha, m_attn_sinks, m_w_alpha_up)
    sm_v = pack(v_g_in, v_g_final, v_g_gla_norm, v_b_alpha, v_attn_sinks, v_w_alpha_up)
    sm_out = [p[0] for p in _adamw(sm_w[None], sm_g[None], sm_m[None], sm_v[None], "adamw_small")]

    def unpack(p):
        sizes = (D_MODEL, D_MODEL, B_WIDTH, B_KEY_WIDTH, A_HEADS, B_GATE_RANK * nup)
        outs, at = [], 0
        for s in sizes:
            outs.append(p[:, at:at + s])
            at += s
        gi, gfin, gnn, ba, sk, wu = outs
        return dict(g_in=gi, g_final=gfin.reshape(D_MODEL), g_gla_norm=gnn, b_alpha=ba, attn_sinks=sk,
                    w_alpha_up=wu.reshape(1, B_GATE_RANK, nup))

    untr = lambda a: jnp.transpose(a, (1, 2, 0))
    big = dict(w_in=tuple(untr(a) for a in (g_win_t,) + tuple(_adamw(tr(w_in), g_win_t, tr(m_w_in), tr(v_w_in), "adamw_w_in"))))
    for nm, w, g, m, v in (("w_out_a", w_out_a, g_wa, m_w_out_a, v_w_out_a),
                           ("w_out_b", w_out_b, g_wb, m_w_out_b, v_w_out_b), ("w_o", w_o, g_wo, m_w_o, v_w_o)):
        big[nm] = (g[None],) + tuple(_adamw(w, g[None], m, v, "adamw_" + nm))

    order = ("g_in", "w_in", "w_alpha_up", "b_alpha", "attn_sinks", "g_gla_norm", "w_out_a", "w_out_b", "w_o", "g_final")
    small_sets = [unpack(sm_g)] + [unpack(p) for p in sm_out]
    outs = []
    for kind in range(4):
        for nm in order:
            outs.append(big[nm][kind] if nm in big else small_sets[kind][nm])
    return (loss, grad_x2.reshape(x.shape), *outs)
```

```python
import math
from typing import NamedTuple

import numpy as np
import jax
import jax.numpy as jnp
from jax import lax
from jax.experimental import pallas as pl
from jax.experimental.pallas import tpu as pltpu

D_MODEL = 1024
A_HEADS, A_KV_HEADS, A_HEAD_DIM = 8, 2, 64
A_GROUP = A_HEADS // A_KV_HEADS
A_WIDTH, A_KV_WIDTH = 512, 128
BLOCK = 128
ROPE_THETA = 500000.0
ROPE_DIM = 16
B_HEADS, B_KEY_DIM, B_VAL_DIM = 4, 64, 128
B_KEY_WIDTH, B_WIDTH = 256, 512
B_GATE_RANK = 16
B_GATE_TEMP = 16.0
B_CHUNK = 64
NORM_EPS = 1e-6
NEG_BIG = -1e30
D_IN = 4880

ADAM_LR, ADAM_B1, ADAM_B2, ADAM_EPS, ADAM_WD, ADAM_STEP = 0.001, 0.9, 0.999, 1e-08, 0.01, 10

LANE = 128
ALR_AT = 2816
PIECES = (("qkv", 0, 768), ("za", 768, 1280), ("qkb", 1280, 1792), ("vb", 1792, 2304),
          ("zb", 2304, 2816), ("alr", ALR_AT, ALR_AT + LANE), ("ga", 2832, 3856), ("gb", 3856, 4880))
SHARD = D_IN // 4
WINDOW_STEP = 1216
WINDOW_ROWS = 1232

GLA_BLOCK = 256
MERGE_SLAB = 16
VMEM_LIMIT = 56 * 1024 * 1024

_F32 = jnp.float32
_MX = jnp.bfloat16
_ST = jnp.bfloat16

_MESH = pl.DeviceIdType.MESH
_ANY = pl.BlockSpec(memory_space=pl.ANY)


def _cparams(sem=None, vmem=None):
    return pltpu.CompilerParams(dimension_semantics=sem, vmem_limit_bytes=vmem)


def _dot(a, b):
    return jnp.dot(a.astype(_MX), b.astype(_MX), preferred_element_type=_F32)


def _dot_nt(a, b):
    return lax.dot_general(a.astype(_MX), b.astype(_MX), (((1,), (1,)), ((), ())),
                           preferred_element_type=_F32)


def _dot_tn(a, b):
    return lax.dot_general(a.astype(_MX), b.astype(_MX), (((0,), (0,)), ((), ())),
                           preferred_element_type=_F32)


def _dot_ones(ones_mat, v):
    o = ones_mat.astype(jnp.bfloat16)
    v0 = v.astype(jnp.bfloat16)
    r1 = v - v0.astype(_F32)
    v1 = r1.astype(jnp.bfloat16)
    v2 = (r1 - v1.astype(_F32)).astype(jnp.bfloat16)
    d = lambda t: jnp.dot(o, t, preferred_element_type=_F32)
    return d(v0) + d(v1) + d(v2)


def _sigmoid(x):
    return 0.5 * jnp.tanh(0.5 * x) + 0.5


def _log_sigmoid(x):
    return jnp.minimum(x, 0.0) - jnp.log(1.0 + jnp.exp(-jnp.abs(x)))


def _lane_tile(t, width):
    reps = width // t.shape[1]
    return t if reps == 1 else jnp.tile(t, (1, reps))


def _rope(t, cos, sa, sb, sign):
    w = t.shape[1]
    rot = pltpu.roll(t, w - 8, 1) * _lane_tile(sa, w) + pltpu.roll(t, 8, 1) * _lane_tile(sb, w)
    return t * _lane_tile(cos, w) + sign * rot


def _rms_bwd(dy_g, n, r):
    return r * (dy_g - n * jnp.mean(dy_g * n, axis=-1, keepdims=True))


ROPE_ROWS = 256


def _rope_consts():
    lane = np.arange(LANE) % A_HEAD_DIM
    half = ROPE_DIM // 2
    inv = np.exp((np.float32(-math.log(ROPE_THETA)) * np.arange(half, dtype=np.float32)) * np.float32(2.0 / ROPE_DIM))
    consts = np.zeros((8, LANE), np.float32)
    consts[0] = np.where(lane < ROPE_DIM, inv[lane % half], 0.0)
    consts[1] = np.where(lane < half, -1.0, 0.0)
    consts[2] = np.where((lane >= half) & (lane < ROPE_DIM), 1.0, 0.0)
    return jnp.asarray(consts)


def _rope_tables_into(pos_ref, c_ref, cos_ref, sa_ref, sb_ref):
    def rows_of(b, carry):
        rows = pl.ds(pl.multiple_of(b * ROPE_ROWS, ROPE_ROWS), ROPE_ROWS)
        ang = pos_ref[rows, :] * c_ref[0:1, :]
        s = jnp.sin(ang)
        cos_ref[rows, :] = jnp.cos(ang)
        sa_ref[rows, :] = s * c_ref[1:2, :]
        sb_ref[rows, :] = s * c_ref[2:3, :]
        return carry

    lax.fori_loop(0, pos_ref.shape[0] // ROPE_ROWS, rows_of, 0)


def _in_proj(x2, g_in, wt, cos, sa, sb):
    t = x2.shape[0]
    tm = min(t, 512)

    def body(x_ref, g_ref, w_ref, cos_ref, sa_ref, sb_ref, h_ref, qkv_ref, za_ref, qkb_ref,
             vb_ref, zb_ref, alr_ref, ga_ref, gb_ref):
        xv = x_ref[...]
        r = lax.rsqrt(jnp.mean(xv * xv, axis=-1, keepdims=True) + NORM_EPS)
        h = (xv * r * g_ref[...]).astype(_MX)
        h_ref[...] = h.astype(_ST)
        outs = dict(za=za_ref, qkb=qkb_ref, vb=vb_ref, zb=zb_ref, alr=alr_ref, ga=ga_ref, gb=gb_ref)
        for name, a, b in PIECES:
            p = _dot_nt(h, w_ref[a:b, :])
            if name == "qkv":
                c, s1, s2 = cos_ref[...], sa_ref[...], sb_ref[...]
                qkv_ref[:, 0:512] = _rope(p[:, 0:512], c, s1, s2, 1.0).astype(_ST)
                qkv_ref[:, 512:640] = _rope(p[:, 512:640], c, s1, s2, 1.0).astype(_ST)
                qkv_ref[:, 640:768] = p[:, 640:768].astype(_ST)
            else:
                outs[name][...] = p.astype(outs[name].dtype)

    rows = lambda w: pl.BlockSpec((tm, w), lambda i: (i, 0))
    shp = lambda name, w: jax.ShapeDtypeStruct((t, w), _F32 if name == "qkb" else _ST)
    widths = [D_MODEL] + [b - a for _, a, b in PIECES]
    return pl.pallas_call(
        body, name="in_proj", grid=(t // tm,),
        in_specs=[rows(D_MODEL), pl.BlockSpec((1, D_MODEL), lambda i: (0, 0)),
                  pl.BlockSpec((D_IN, D_MODEL), lambda i: (0, 0), pipeline_mode=pl.Buffered(1)),
                  rows(LANE), rows(LANE), rows(LANE)],
        out_specs=[rows(w) for w in widths],
        out_shape=[shp(n, w) for n, w in zip(["h"] + [p[0] for p in PIECES], widths)],
        compiler_params=_cparams(("parallel",), VMEM_LIMIT),
    )(x2, g_in, wt, cos, sa, sb)


def _attn_operands(k_prev, k_cur, v_prev, v_cur, want_bwd):
    kf = jnp.concatenate([k_prev, k_cur], axis=0).astype(_F32) * (A_HEAD_DIM ** -0.5)
    vf = jnp.concatenate([v_prev, v_cur], axis=0).astype(_F32)
    lo = lax.broadcasted_iota(jnp.int32, (1, LANE), 1) < 64

    def on_lanes(a):
        sw = pltpu.roll(a, 64, 1)
        z = jnp.zeros_like(a)
        return [[jnp.where(lo, a, z).astype(_MX), jnp.where(lo, z, sw).astype(_MX)],
                [jnp.where(lo, sw, z).astype(_MX), jnp.where(lo, z, a).astype(_MX)]]

    def on_rows(a):
        at = a.T.astype(_MX)
        z = jnp.zeros((64, at.shape[1]), _MX)
        top, bot = at[0:64], at[64:128]
        return [[jnp.concatenate([top, z], axis=0), jnp.concatenate([z, top], axis=0)],
                [jnp.concatenate([bot, z], axis=0), jnp.concatenate([z, bot], axis=0)]]

    ops = dict(k_lanes=on_lanes(kf), v_rows=on_rows(vf), lo=lo)
    if want_bwd:
        ops.update(v_lanes=on_lanes(vf), k_rows=on_rows(kf))
    return ops


def _attn_valid(n):
    kj = lax.broadcasted_iota(jnp.int32, (2 * BLOCK, 2 * BLOCK), 0) - BLOCK
    qi = lax.broadcasted_iota(jnp.int32, (2 * BLOCK, 2 * BLOCK), 1) & (BLOCK - 1)
    return (kj <= qi) & (qi - kj < BLOCK) & ((n > 0) | (kj >= 0))


def _attn_sinks(sink_ref, h_a, h_b):
    first = lax.broadcasted_iota(jnp.int32, (1, 2 * BLOCK), 1) < BLOCK
    return jnp.where(first, sink_ref[h_a], sink_ref[h_b])


def _attn_softmax_t(k_lanes, q_pair, valid, sink):
    s = jnp.where(valid, _dot_nt(k_lanes, q_pair), NEG_BIG)
    m = jnp.maximum(jnp.max(s, axis=0, keepdims=True), sink)
    e = jnp.exp(s - m)
    e_sink = jnp.exp(sink - m)
    inv = 1.0 / (jnp.sum(e, axis=0, keepdims=True) + e_sink)
    return e, e_sink, inv


ATTN_TILE = 4


def _attn_kv(qkv_ref, kvp_ref, j):
    rows = slice(j * BLOCK, (j + 1) * BLOCK)
    if j == 0:
        k_prev, v_prev = kvp_ref[:, 0:128], kvp_ref[:, 128:256]
    else:
        before = slice((j - 1) * BLOCK, j * BLOCK)
        k_prev, v_prev = qkv_ref[before, 512:640], qkv_ref[before, 640:768]
    return k_prev, qkv_ref[rows, 512:640], v_prev, qkv_ref[rows, 640:768]


def _attn_fwd(qkv, za, sinks, nseq):
    t = qkv.shape[0]
    tile = ATTN_TILE * BLOCK
    nt = t // nseq // tile

    def body(sink_ref, qkv_ref, kvp_ref, za_ref, oa_ref):
        for j in range(ATTN_TILE):
            rows = slice(j * BLOCK, (j + 1) * BLOCK)
            ops = _attn_operands(*_attn_kv(qkv_ref, kvp_ref, j), False)
            valid = _attn_valid(ATTN_TILE * pl.program_id(1) + j)[:, 0:BLOCK]
            for pr in range(A_HEADS // 2):
                lanes = slice(pr * LANE, (pr + 1) * LANE)
                g = pr // (A_GROUP // 2)
                q_pair = qkv_ref[rows, lanes]
                ot = None
                for half in range(2):
                    e, _, inv = _attn_softmax_t(ops["k_lanes"][g][half], q_pair, valid, sink_ref[2 * pr + half])
                    part = _dot(ops["v_rows"][g][half], e) * inv
                    ot = part if ot is None else ot + part
                z = za_ref[rows, lanes].astype(_F32)
                oa_ref[rows, lanes] = (ot.T * (z * _sigmoid(z))).astype(_ST)

    cur = lambda w: pl.BlockSpec((tile, w), lambda s, n: (s * nt + n, 0))
    return pl.pallas_call(
        body, name="attn_fwd", grid=(nseq, nt),
        in_specs=[pl.BlockSpec(memory_space=pltpu.SMEM), cur(768),
                  pl.BlockSpec((BLOCK, 256), lambda s, n: (ATTN_TILE * (s * nt + n) - jnp.minimum(n, 1), 2)),
                  cur(512)],
        out_specs=cur(512), out_shape=jax.ShapeDtypeStruct((t, A_WIDTH), _ST),
        compiler_params=_cparams(("parallel", "arbitrary")),
    )(sinks, qkv, qkv, za)


def _attn_bwd(qkv, za, doa, sinks, cos, sa, sb, nseq):
    t = qkv.shape[0]
    tile = ATTN_TILE * BLOCK
    nt = t // nseq // tile

    def body(sink_ref, qkv_ref, kvp_ref, za_ref, doa_ref, cos_ref, sa_ref, sb_ref,
             dqkv_ref, dza_ref, dsink_ref, ck_ref, cv_ref):
        s_id, i = pl.program_id(0), pl.program_id(1)

        @pl.when((s_id == 0) & (i == 0))
        def _():
            dsink_ref[...] = jnp.zeros_like(dsink_ref)

        @pl.when(i == 0)
        def _():
            ck_ref[...] = jnp.zeros_like(ck_ref)
            cv_ref[...] = jnp.zeros_like(cv_ref)

        carry_k, carry_v = ck_ref[...], cv_ref[...]
        for j in reversed(range(ATTN_TILE)):
            rows = slice(j * BLOCK, (j + 1) * BLOCK)
            ops = _attn_operands(*_attn_kv(qkv_ref, kvp_ref, j), True)
            lo = ops["lo"]
            valid = _attn_valid(ATTN_TILE * (nt - 1 - i) + j)
            dk_acc, dv_acc, dq_pairs = [], [], []
            for g in range(A_KV_HEADS):
                pairs = [slice((2 * g + p) * LANE, (2 * g + p + 1) * LANE) for p in range(2)]
                q_both = jnp.concatenate([qkv_ref[rows, p] for p in pairs], axis=0)
                q_f = q_both.astype(_F32)
                z = [za_ref[rows, p].astype(_F32) for p in pairs]
                sz = [_sigmoid(v) for v in z]
                d_oa = [doa_ref[rows, p].astype(_F32) for p in pairs]
                d_att = jnp.concatenate([d_oa[p] * (z[p] * sz[p]) for p in range(2)], axis=0)
                zero = jnp.zeros_like(d_att)
                ot, dqt, ds_all, pn_all, qz_all, daz_all = None, None, [], [], [], []
                for half in range(2):
                    heads = (4 * g + half, 4 * g + 2 + half)
                    e, e_sink, inv = _attn_softmax_t(ops["k_lanes"][g][half], q_both, valid,
                                                     _attn_sinks(sink_ref, *heads))
                    pn = e * inv
                    dpt = _dot_nt(ops["v_lanes"][g][half], d_att)
                    delta = jnp.sum(pn * dpt, axis=0, keepdims=True)
                    ds = (pn * (dpt - delta)).astype(_MX)
                    pn = pn.astype(_MX)
                    d_sink = e_sink * inv * delta
                    for p, h in enumerate(heads):
                        dsink_ref[h:h + 1, :] = dsink_ref[h:h + 1, :] - jnp.sum(d_sink[:, p * BLOCK:(p + 1) * BLOCK])
                    o_part = _dot(ops["v_rows"][g][half], pn)
                    dq_part = _dot(ops["k_rows"][g][half], ds)
                    ot = o_part if ot is None else ot + o_part
                    dqt = dq_part if dqt is None else dqt + dq_part
                    mine = lo if half == 0 else jnp.logical_not(lo)
                    ds_all.append(ds)
                    pn_all.append(pn)
                    qz_all.append(jnp.where(mine, q_f, zero).astype(_MX))
                    daz_all.append(jnp.where(mine, d_att, zero).astype(_MX))
                dk_acc.append(_dot(jnp.concatenate(ds_all, axis=1), jnp.concatenate(qz_all, axis=0)))
                dv_acc.append(_dot(jnp.concatenate(pn_all, axis=1), jnp.concatenate(daz_all, axis=0)))
                for p, lanes in enumerate(pairs):
                    cols = slice(p * BLOCK, (p + 1) * BLOCK)
                    dza_ref[rows, lanes] = (d_oa[p] * ot[:, cols].T * (sz[p] * (1.0 + z[p] * (1.0 - sz[p])))).astype(_ST)
                    dq_pairs.append(dqt[:, cols].T)

            def fold(acc, scale):
                both = [a + pltpu.roll(a, 64, 1) for a in acc]
                return jnp.where(lo, both[0], both[1]) * scale

            dk_full = fold(dk_acc, A_HEAD_DIM ** -0.5)
            dv_full = fold(dv_acc, 1.0)
            dk_cur, dv_cur = dk_full[BLOCK:] + carry_k, dv_full[BLOCK:] + carry_v
            carry_k, carry_v = dk_full[:BLOCK], dv_full[:BLOCK]
            c, s1, s2 = cos_ref[rows, :], sa_ref[rows, :], sb_ref[rows, :]
            dqkv_ref[rows, 0:512] = _rope(jnp.concatenate(dq_pairs, axis=1), c, s1, s2, -1.0).astype(_ST)
            dqkv_ref[rows, 512:640] = _rope(dk_cur, c, s1, s2, -1.0).astype(_ST)
            dqkv_ref[rows, 640:768] = dv_cur.astype(_ST)
        ck_ref[...] = carry_k
        cv_ref[...] = carry_v

    cur = lambda w: pl.BlockSpec((tile, w), lambda s, i: (s * nt + nt - 1 - i, 0))
    return pl.pallas_call(
        body, name="attn_bwd", grid=(nseq, nt),
        in_specs=[pl.BlockSpec(memory_space=pltpu.SMEM), cur(768),
                  pl.BlockSpec((BLOCK, 256),
                               lambda s, i: (ATTN_TILE * (s * nt + nt - 1 - i) - jnp.minimum(nt - 1 - i, 1), 2)),
                  cur(512), cur(512), cur(LANE), cur(LANE), cur(LANE)],
        out_specs=[cur(768), cur(512), pl.BlockSpec((8, LANE), lambda s, i: (0, 0))],
        out_shape=[jax.ShapeDtypeStruct((t, 768), _ST), jax.ShapeDtypeStruct((t, 512), _ST),
                   jax.ShapeDtypeStruct((8, LANE), _F32)],
        scratch_shapes=[pltpu.VMEM((BLOCK, A_KV_WIDTH), _F32), pltpu.VMEM((BLOCK, A_KV_WIDTH), _F32)],
        compiler_params=_cparams(("arbitrary", "arbitrary")),
    )(sinks, qkv, qkv, za, doa, cos, sa, sb)


def _gla_chunk_terms(la, qkb_ref, r0):
    g = la[r0:r0 + B_CHUNK, :]
    ri = lax.broadcasted_iota(jnp.int32, (B_CHUNK, B_CHUNK), 0)
    ci = lax.broadcasted_iota(jnp.int32, (B_CHUNK, B_CHUNK), 1)
    cum = _dot_ones((ri >= ci).astype(_F32), g)
    last = cum[B_CHUNK - 1:B_CHUNK, :]
    mid = cum[B_CHUNK // 2 - 1:B_CHUNK // 2, :]
    q = qkb_ref[r0:r0 + B_CHUNK, 0:B_KEY_WIDTH].astype(_F32) * (B_KEY_DIM ** -0.5)
    k = qkb_ref[r0:r0 + B_CHUNK, B_KEY_WIDTH:2 * B_KEY_WIDTH].astype(_F32)
    e_q, e_k, e_l, e_c = jnp.exp(cum - mid), jnp.exp(mid - cum), jnp.exp(last - cum), jnp.exp(cum)
    dec_col = jnp.exp(jnp.sum(g.T, axis=1, keepdims=True))
    return dict(qm=q * e_q, km=k * e_k, kl=k * e_l, qc=q * e_c, e_q=e_q, e_k=e_k, e_l=e_l, e_c=e_c,
                dec_col=dec_col, dec_row=jnp.exp(last), causal=ri >= ci, ri=ri)


def _gate_logits(alr_ref, wup_ref, b_ref):
    return _dot(alr_ref[...], wup_ref[...]) + b_ref[...]


def _gla_fwd(qkb, vb, zb, alr, wup, b_alpha, gn, nseq):
    t = qkb.shape[0]
    tb = min(GLA_BLOCK, t // nseq)
    nblk = t // nseq // tb
    cpb = tb // B_CHUNK

    def body(qkb_ref, vb_ref, zb_ref, alr_ref, wup_ref, b_ref, gn_ref, ob_ref, oraw_ref, sst_ref, s_ref):
        @pl.when(pl.program_id(1) == 0)
        def _():
            s_ref[...] = jnp.zeros_like(s_ref)

        la = _log_sigmoid(_gate_logits(alr_ref, wup_ref, b_ref)) * (1.0 / B_GATE_TEMP)
        terms = [_gla_chunk_terms(la, qkb_ref, c * B_CHUNK) for c in range(cpb)]
        o_intra, inc = {}, {}
        for c, tm in enumerate(terms):
            for h in range(B_HEADS):
                kl_, vl_ = slice(h * 64, (h + 1) * 64), slice(h * 128, (h + 1) * 128)
                v = vb_ref[c * B_CHUNK:(c + 1) * B_CHUNK, vl_]
                a = jnp.where(tm["causal"], _dot_nt(tm["qm"][:, kl_], tm["km"][:, kl_]), 0.0)
                o_intra[c, h] = _dot(a, v)
                inc[c, h] = _dot_tn(tm["kl"][:, kl_], v)
        o_heads = {}
        for h in range(B_HEADS):
            kl_ = slice(h * 64, (h + 1) * 64)
            st = s_ref[kl_, :]
            for c, tm in enumerate(terms):
                sst_ref[c, kl_, :] = st
                o_heads[c, h] = o_intra[c, h] + _dot(tm["qc"][:, kl_], st)
                st = tm["dec_col"][kl_, :] * st + inc[c, h]
            s_ref[kl_, :] = st
        o = jnp.concatenate([jnp.concatenate([o_heads[c, h] for h in range(B_HEADS)], axis=1)
                             for c in range(cpb)], axis=0)
        oraw_ref[...] = o
        z = zb_ref[...].astype(_F32)
        gate = z * _sigmoid(z)
        for h in range(B_HEADS):
            vl_ = slice(h * 128, (h + 1) * 128)
            oh = o[:, vl_]
            r = lax.rsqrt(jnp.mean(oh * oh, axis=-1, keepdims=True) + NORM_EPS)
            ob_ref[:, vl_] = ((oh * r) * gn_ref[:, vl_] * gate[:, vl_]).astype(_ST)

    rows = lambda w: pl.BlockSpec((tb, w), lambda s, i: (s * nblk + i, 0))
    full = lambda a, b: pl.BlockSpec((a, b), lambda s, i: (0, 0))
    return pl.pallas_call(
        body, name="gla_fwd", grid=(nseq, nblk),
        in_specs=[rows(512), rows(512), rows(512), rows(LANE), full(LANE, B_KEY_WIDTH),
                  full(1, B_KEY_WIDTH), full(1, B_WIDTH)],
        out_specs=[rows(512), rows(512),
                   pl.BlockSpec((cpb, B_KEY_WIDTH, B_VAL_DIM), lambda s, i: (s * nblk + i, 0, 0))],
        out_shape=[jax.ShapeDtypeStruct((t, B_WIDTH), _ST), jax.ShapeDtypeStruct((t, B_WIDTH), _F32),
                   jax.ShapeDtypeStruct((t // B_CHUNK, B_KEY_WIDTH, B_VAL_DIM), _F32)],
        scratch_shapes=[pltpu.VMEM((B_KEY_WIDTH, B_VAL_DIM), _F32)],
        compiler_params=_cparams(("parallel", "arbitrary")),
    )(qkb, vb, zb, alr, wup, b_alpha, gn)


def _gla_bwd(qkb, vb, zb, alr, oraw, dob, sst, wup, b_alpha, gn, nseq):
    t = qkb.shape[0]
    tb = min(GLA_BLOCK, t // nseq)
    nblk = t // nseq // tb
    cpb = tb // B_CHUNK

    def body(qkb_ref, vb_ref, zb_ref, alr_ref, oraw_ref, dob_ref, sst_ref, wup_ref, b_ref, gn_ref,
             dqkb_ref, dvb_ref, dzb_ref, dalr_ref, dwup_ref, db_ref, dgn_ref, ds_ref):
        s_id, i = pl.program_id(0), pl.program_id(1)

        @pl.when((s_id == 0) & (i == 0))
        def _():
            dwup_ref[...] = jnp.zeros_like(dwup_ref)
            db_ref[...] = jnp.zeros_like(db_ref)
            dgn_ref[...] = jnp.zeros_like(dgn_ref)

        @pl.when(i == 0)
        def _():
            ds_ref[...] = jnp.zeros_like(ds_ref)

        a_pre = _gate_logits(alr_ref, wup_ref, b_ref)
        la = _log_sigmoid(a_pre) * (1.0 / B_GATE_TEMP)

        z = zb_ref[...].astype(_F32)
        sz = _sigmoid(z)
        d_ob = dob_ref[...].astype(_F32)
        tg = d_ob * (z * sz)
        dsilu = sz * (1.0 + z * (1.0 - sz))
        do_cols, dgn_cols = [], []
        for h in range(B_HEADS):
            vl_ = slice(h * 128, (h + 1) * 128)
            oh = oraw_ref[:, vl_].astype(_F32)
            r = lax.rsqrt(jnp.mean(oh * oh, axis=-1, keepdims=True) + NORM_EPS)
            on = oh * r
            gnh = gn_ref[:, vl_]
            dzb_ref[:, vl_] = (d_ob[:, vl_] * (on * gnh) * dsilu[:, vl_]).astype(_ST)
            dgn_cols.append(jnp.sum(tg[:, vl_] * on, axis=0, keepdims=True))
            do_cols.append(_rms_bwd(tg[:, vl_] * gnh, on, r))
        dgn_ref[...] = dgn_ref[...] + jnp.concatenate(dgn_cols, axis=1)
        d_o = jnp.concatenate(do_cols, axis=1)

        ri = lax.broadcasted_iota(jnp.int32, (tb, tb), 0)
        ci = lax.broadcasted_iota(jnp.int32, (tb, tb), 1)
        same = (ri // B_CHUNK) == (ci // B_CHUNK)
        low = same & (ri >= ci)
        upto_mid = same & ((ci % B_CHUNK) < B_CHUNK // 2)
        sums = _dot_ones(jnp.concatenate([m.astype(_F32) for m in (low, same, upto_mid)], axis=0), la)
        cum, last, mid = sums[0:tb], sums[tb:2 * tb], sums[2 * tb:3 * tb]
        e_q, e_k, e_l, e_c = jnp.exp(cum - mid), jnp.exp(mid - cum), jnp.exp(last - cum), jnp.exp(cum)
        q = qkb_ref[:, 0:B_KEY_WIDTH] * (B_KEY_DIM ** -0.5)
        k = qkb_ref[:, B_KEY_WIDTH:2 * B_KEY_WIDTH]
        qm, km, kl, qc = q * e_q, k * e_k, k * e_l, q * e_c
        lane_head = lax.broadcasted_iota(jnp.int32, (1, B_KEY_WIDTH), 1) // B_KEY_DIM
        d_o_mx = d_o.astype(_MX)

        def on_diagonal(st):
            z = jnp.zeros((B_KEY_DIM, B_VAL_DIM), st.dtype)
            return jnp.concatenate([jnp.concatenate(
                [st[h * B_KEY_DIM:(h + 1) * B_KEY_DIM] if g == h else z for g in range(B_HEADS)], axis=1)
                for h in range(B_HEADS)], axis=0)

        def diagonal_of(full):
            return jnp.concatenate([full[h * B_KEY_DIM:(h + 1) * B_KEY_DIM, h * B_VAL_DIM:(h + 1) * B_VAL_DIM]
                                    for h in range(B_HEADS)], axis=0)

        dqm, dkm, dv_cols = None, None, []
        for h in range(B_HEADS):
            vl_ = slice(h * B_VAL_DIM, (h + 1) * B_VAL_DIM)
            mine = lane_head == h
            qz, kz = jnp.where(mine, qm, 0.0).astype(_MX), jnp.where(mine, km, 0.0).astype(_MX)
            a = jnp.where(low, _dot_nt(qz, kz), 0.0).astype(_MX)
            da = jnp.where(low, _dot_nt(d_o_mx[:, vl_], vb_ref[:, vl_]), 0.0).astype(_MX)
            dqm_h, dkm_h = _dot(da, kz), _dot_tn(da, qz)
            dqm = dqm_h if dqm is None else dqm + dqm_h
            dkm = dkm_h if dkm is None else dkm + dkm_h
            dv_cols.append(_dot_tn(a, d_o_mx[:, vl_]))
        dv = jnp.concatenate(dv_cols, axis=1)

        chunk = [slice(c * B_CHUNK, (c + 1) * B_CHUNK) for c in range(cpb)]
        dqc_rows, g_loc = [], []
        for c in range(cpb):
            dqc_rows.append(_dot_nt(d_o_mx[chunk[c]], on_diagonal(sst_ref[c].astype(_MX))))
            g_loc.append(diagonal_of(_dot_tn(qc[chunk[c]], d_o_mx[chunk[c]])))
        cur = ds_ref[...]
        d_state = [None] * cpb
        for c in reversed(range(cpb)):
            d_state[c] = cur
            cur = g_loc[c] + jnp.exp(jnp.sum(la[chunk[c]].T, axis=1, keepdims=True)) * cur
        ds_ref[...] = cur
        dkl_rows, dv_rows, dlast_rows = [], [], []
        ones8 = jnp.ones((8, B_VAL_DIM), _F32)
        for c in range(cpb):
            dsd = on_diagonal(d_state[c].astype(_MX))
            dkl_c = _dot_nt(vb_ref[chunk[c], :], dsd)
            dkl_rows.append(dkl_c)
            dv_rows.append(_dot(kl[chunk[c]], dsd))
            prod = d_state[c] * sst_ref[c]
            p0 = prod.astype(jnp.bfloat16)
            p1 = (prod - p0.astype(_F32)).astype(jnp.bfloat16)
            p2 = (prod - p0.astype(_F32) - p1.astype(_F32)).astype(jnp.bfloat16)
            ddec = (_dot_nt(ones8, p0) + _dot_nt(ones8, p1) + _dot_nt(ones8, p2))[0:1]
            r_last = c * B_CHUNK + B_CHUNK - 1
            dlast = jnp.sum(dkl_c * kl[chunk[c]], axis=0, keepdims=True) + ddec * jnp.exp(last[r_last:r_last + 1])
            dlast_rows.append(jnp.broadcast_to(dlast, (B_CHUNK, B_KEY_WIDTH)))
        dqc, dkl = jnp.concatenate(dqc_rows, axis=0), jnp.concatenate(dkl_rows, axis=0)
        dqkb_ref[:, 0:B_KEY_WIDTH] = ((dqm * e_q + dqc * e_c) * (B_KEY_DIM ** -0.5)).astype(_ST)
        dqkb_ref[:, B_KEY_WIDTH:2 * B_KEY_WIDTH] = (dkm * e_k + dkl * e_l).astype(_ST)
        dvb_ref[...] = (dv + jnp.concatenate(dv_rows, axis=0)).astype(_ST)
        dcum = dqm * qm - dkm * km + dqc * qc - dkl * kl
        row = lax.broadcasted_iota(jnp.int32, (tb, B_KEY_WIDTH), 0)
        dcum = jnp.where(row % B_CHUNK == B_CHUNK - 1, dcum + jnp.concatenate(dlast_rows, axis=0), dcum)
        dla = _dot_ones((same & (ri <= ci)).astype(_F32), dcum)

        da_pre = dla * (1.0 / B_GATE_TEMP) * (1.0 - _sigmoid(a_pre))
        dalr_ref[...] = _dot_nt(da_pre, wup_ref[...]).astype(_ST)
        dwup_ref[...] = dwup_ref[...] + _dot_tn(alr_ref[...], da_pre)
        db_ref[...] = db_ref[...] + jnp.sum(da_pre, axis=0, keepdims=True)

    blk = lambda s, i: s * nblk + nblk - 1 - i
    rows = lambda w: pl.BlockSpec((tb, w), lambda s, i: (blk(s, i), 0))
    full = lambda a, b: pl.BlockSpec((a, b), lambda s, i: (0, 0))
    act = lambda w: jax.ShapeDtypeStruct((t, w), _ST)
    return pl.pallas_call(
        body, name="gla_bwd", grid=(nseq, nblk),
        in_specs=[rows(512), rows(512), rows(512), rows(LANE), rows(512), rows(512),
                  pl.BlockSpec((cpb, B_KEY_WIDTH, B_VAL_DIM), lambda s, i: (blk(s, i), 0, 0)),
                  full(LANE, B_KEY_WIDTH), full(1, B_KEY_WIDTH), full(1, B_WIDTH)],
        out_specs=[rows(512), rows(512), rows(512), rows(LANE), full(LANE, B_KEY_WIDTH),
                   full(1, B_KEY_WIDTH), full(1, B_WIDTH)],
        out_shape=[act(512), act(512), act(512), act(LANE),
                   jax.ShapeDtypeStruct((LANE, B_KEY_WIDTH), _F32),
                   jax.ShapeDtypeStruct((1, B_KEY_WIDTH), _F32), jax.ShapeDtypeStruct((1, B_WIDTH), _F32)],
        scratch_shapes=[pltpu.VMEM((B_KEY_WIDTH, B_VAL_DIM), _F32)],
        compiler_params=_cparams(("arbitrary", "arbitrary")),
    )(qkb, vb, zb, alr, oraw, dob, sst, wup, b_alpha, gn)


def _merge_loss(oa, ob, ga, gb, x2, tgt, wa, wb, wo, g_final):
    t = x2.shape[0]
    tm = min(t, 512)
    nt = t // tm

    def body(oa_ref, ob_ref, ga_ref, gb_ref, x_ref, t_ref, wa_ref, wb_ref, wo_ref, gf_ref,
             dh_ref, doa_ref, dob_ref, dga_ref, dgb_ref, dwa_ref, dwb_ref, dwo_ref, dgf_ref, loss_ref,
             ya_s, yb_s, out_s, dmer_s, mrg_s, dya_s, dyb_s):
        first = pl.program_id(0) == 0
        so_far = lambda ref: jnp.where(first, 0.0, ref[...])

        slabs = [slice(s, s + MERGE_SLAB) for s in range(0, tm, MERGE_SLAB)]
        fold = lambda a: a[0:8] + a[8:16]
        ya_s[...] = _dot(oa_ref[...], wa_ref[...])
        yb_s[...] = _dot(ob_ref[...], wb_ref[...])
        for rows_ in slabs:
            sga, sgb = _sigmoid(ga_ref[rows_, :].astype(_F32)), _sigmoid(gb_ref[rows_, :].astype(_F32))
            mrg_s[rows_, :] = (sga * ya_s[rows_, :] + sgb * yb_s[rows_, :]).astype(_MX)
        out_s[...] = x_ref[...] + _dot(mrg_s[...], wo_ref[...])
        gf = gf_ref[...]
        loss8 = jnp.zeros((8, D_MODEL), _F32)
        dgf8 = jnp.zeros((8, D_MODEL), _F32)
        for rows_ in slabs:
            out = out_s[rows_, :]
            r = lax.rsqrt(jnp.mean(out * out, axis=-1, keepdims=True) + NORM_EPS)
            nrm = out * r
            err = nrm * gf - t_ref[rows_, :]
            loss8 = loss8 + fold(err * err)
            dy = err * (1.0 / D_MODEL)
            dgf8 = dgf8 + fold(dy * nrm)
            dh = _rms_bwd(dy * gf, nrm, r)
            dh_ref[rows_, :] = dh.astype(_ST)
        loss_ref[...] = so_far(loss_ref) + (0.5 / D_MODEL) * jnp.sum(loss8, axis=0, keepdims=True)
        dgf_ref[...] = so_far(dgf_ref) + jnp.sum(dgf8, axis=0, keepdims=True)
        dmer_s[...] = _dot_nt(dh_ref[...], wo_ref[...])
        dwo_ref[...] = so_far(dwo_ref) + _dot_tn(mrg_s[...], dh_ref[...])
        for rows_ in slabs:
            sga, sgb = _sigmoid(ga_ref[rows_, :].astype(_F32)), _sigmoid(gb_ref[rows_, :].astype(_F32))
            dmer = dmer_s[rows_, :]
            da, db = dmer * sga, dmer * sgb
            dya_s[rows_, :] = da.astype(_MX)
            dyb_s[rows_, :] = db.astype(_MX)
            dga_ref[rows_, :] = (da * ya_s[rows_, :] * (1.0 - sga)).astype(_ST)
            dgb_ref[rows_, :] = (db * yb_s[rows_, :] * (1.0 - sgb)).astype(_ST)
        doa_ref[...] = _dot_nt(dya_s[...], wa_ref[...]).astype(_ST)
        dob_ref[...] = _dot_nt(dyb_s[...], wb_ref[...]).astype(_ST)
        dwa_ref[...] = so_far(dwa_ref) + _dot_tn(oa_ref[...], dya_s[...])
        dwb_ref[...] = so_far(dwb_ref) + _dot_tn(ob_ref[...], dyb_s[...])

    rows = lambda w: pl.BlockSpec((tm, w), lambda i: (i, 0))
    full = lambda a, b: pl.BlockSpec((a, b), lambda i: (0, 0), pipeline_mode=pl.Buffered(1))
    return pl.pallas_call(
        body, name="merge_loss", grid=(nt,),
        in_specs=[rows(512), rows(512), rows(D_MODEL), rows(D_MODEL), rows(D_MODEL), rows(D_MODEL),
                  full(A_WIDTH, D_MODEL), full(B_WIDTH, D_MODEL), full(D_MODEL, D_MODEL), full(1, D_MODEL)],
        out_specs=[rows(D_MODEL), rows(512), rows(512), rows(D_MODEL), rows(D_MODEL),
                   full(A_WIDTH, D_MODEL), full(B_WIDTH, D_MODEL), full(D_MODEL, D_MODEL),
                   full(1, D_MODEL), full(1, D_MODEL)],
        out_shape=[jax.ShapeDtypeStruct((t, D_MODEL), _ST), jax.ShapeDtypeStruct((t, 512), _ST),
                   jax.ShapeDtypeStruct((t, 512), _ST), jax.ShapeDtypeStruct((t, D_MODEL), _ST),
                   jax.ShapeDtypeStruct((t, D_MODEL), _ST),
                   jax.ShapeDtypeStruct((A_WIDTH, D_MODEL), _F32), jax.ShapeDtypeStruct((B_WIDTH, D_MODEL), _F32),
                   jax.ShapeDtypeStruct((D_MODEL, D_MODEL), _F32), jax.ShapeDtypeStruct((1, D_MODEL), _F32),
                   jax.ShapeDtypeStruct((1, D_MODEL), _F32)],
        scratch_shapes=[pltpu.VMEM((tm, D_MODEL), _F32)] * 4 + [pltpu.VMEM((tm, D_MODEL), _MX)] * 3,
        compiler_params=_cparams(("arbitrary",), VMEM_LIMIT),
    )(oa, ob, ga, gb, x2, tgt, wa, wb, wo, g_final)


def _in_proj_bwd_x(dpieces, wt, x2, dh2, g_in):
    t = x2.shape[0]
    tm = min(t, 512)
    np_ = len(PIECES)

    def body(*refs):
        dp_refs = refs[:np_]
        w_ref, x_ref, dh2_ref, g_ref, gx_ref, dg_ref = refs[np_:]

        @pl.when(pl.program_id(0) == 0)
        def _():
            dg_ref[...] = jnp.zeros_like(dg_ref)

        dh = None
        for (name, a, b), dp in zip(PIECES, dp_refs):
            part = _dot(dp[...], w_ref[a:b, :])
            dh = part if dh is None else dh + part
        xv = x_ref[...]
        r = lax.rsqrt(jnp.mean(xv * xv, axis=-1, keepdims=True) + NORM_EPS)
        nrm = xv * r
        dg_ref[...] = dg_ref[...] + jnp.sum(dh * nrm, axis=0, keepdims=True)
        gx_ref[...] = dh2_ref[...].astype(_F32) + _rms_bwd(dh * g_ref[...], nrm, r)

    rows = lambda w: pl.BlockSpec((tm, w), lambda i: (i, 0))
    full = lambda a, b: pl.BlockSpec((a, b), lambda i: (0, 0), pipeline_mode=pl.Buffered(1))
    return pl.pallas_call(
        body, name="in_proj_bwd_x", grid=(t // tm,),
        in_specs=[rows(b - a) for _, a, b in PIECES] + [full(D_IN, D_MODEL), rows(D_MODEL), rows(D_MODEL),
                                                          full(1, D_MODEL)],
        out_specs=[rows(D_MODEL), full(1, D_MODEL)],
        out_shape=[jax.ShapeDtypeStruct((t, D_MODEL), _F32), jax.ShapeDtypeStruct((1, D_MODEL), _F32)],
        compiler_params=_cparams(("arbitrary",), VMEM_LIMIT),
    )(*dpieces, wt, x2, dh2, g_in)


def _in_proj_bwd_w(h, dpieces):
    t = h.shape[0]
    tm = min(t, 1024)
    nt = t // tm
    np_ = len(PIECES)

    def body(*refs):
        h_ref, dp_refs, out_ref = refs[0], refs[1:1 + np_], refs[1 + np_]
        acc_ref, sem = refs[2 + np_:]
        i = pl.program_id(0)
        hv = h_ref[...]
        writes = []
        for j, ((name, a, b), dp) in enumerate(zip(PIECES, dp_refs)):
            part = _dot_tn(dp[...], hv)
            if name == "alr":
                b = a + B_GATE_RANK
                part = part[0:B_GATE_RANK]
            acc_ref[a:b, :] = jnp.where(i == 0, 0.0, acc_ref[a:b, :]) + part
            writes.append(pltpu.make_async_copy(acc_ref.at[a:b], out_ref.at[a:b], sem.at[j]))

            @pl.when(i == nt - 1)
            def _(cp=writes[-1]):
                cp.start()

        @pl.when(i == nt - 1)
        def _():
            for cp in writes:
                cp.wait()

    rows = lambda w: pl.BlockSpec((tm, w), lambda i: (i, 0))
    return pl.pallas_call(
        body, name="in_proj_bwd_w", grid=(nt,),
        in_specs=[rows(D_MODEL)] + [rows(b - a) for _, a, b in PIECES],
        out_specs=_ANY, out_shape=jax.ShapeDtypeStruct((D_IN, D_MODEL), _F32),
        scratch_shapes=[pltpu.VMEM((D_IN, D_MODEL), _F32), pltpu.SemaphoreType.DMA((np_,))],
        compiler_params=_cparams(("arbitrary",), VMEM_LIMIT),
    )(h, *dpieces)


def _place():
    return lax.axis_index("x"), lax.axis_index("y"), lax.axis_index("c")


def _other_chips(x, y):
    return [(1 - x, y), (x, 1 - y), (1 - x, 1 - y)]


class _Split(NamedTuple):
    by_rows: bool
    step: int
    size: int

    def half(self, ref, c):
        r, n = ref.shape[-2:]
        if self.by_rows:
            return ref.at[:, pl.ds(pl.multiple_of(c * (n // 2), LANE), n // 2)]
        return ref.at[pl.ds(pl.multiple_of(c * (r // 2), 16), r // 2), :]

    def chip_part(self, ref, k):
        if self.by_rows:
            return ref.at[pl.ds(pl.multiple_of(k * self.step, 16), self.size), :]
        return ref.at[:, pl.ds(pl.multiple_of(k * self.size, LANE), self.size)]

    def half_shape(self, shape):
        r, n = shape
        return (r, n // 2) if self.by_rows else (r // 2, n)

    def part_shape(self, shape):
        r, n = shape
        return (self.size, n) if self.by_rows else (r, self.size)


SPLIT_W_IN_T = _Split(True, WINDOW_STEP, WINDOW_ROWS)
SPLIT_W_O = _Split(True, 256, 256)
SPLIT_W_OUT = _Split(False, 256, 256)


def _gather_weights(shards, splits, fulls, pos_f):
    nw = len(shards)
    t = pos_f.shape[0]

    def body(*refs):
        ins, (pos_ref, c_ref) = refs[:nw], refs[nw:nw + 2]
        outs, tables = refs[nw + 2:2 * nw + 2], refs[2 * nw + 2:2 * nw + 5]
        send_a, recv_a, send_b, recv_b = refs[2 * nw + 5:]
        x, y, c = _place()
        me = 2 * x + y
        peers = _other_chips(x, y)

        def place(i, k, half):
            if splits[i] is None:
                return outs[i].at[k]
            if fulls[i][0] == 4 and len(fulls[i]) == 3:
                whole = outs[i].at[k]
            else:
                whole = splits[i].chip_part(outs[i], k)
            return splits[i].half(whole, half)

        first, passed = [], []
        for i in range(nw):
            src = ins[i] if splits[i] is None else splits[i].half(ins[i], c)
            for j, (px, py) in enumerate(peers):
                cp = pltpu.make_async_remote_copy(
                    src_ref=src, dst_ref=place(i, me, c), send_sem=send_a.at[3 * i + j],
                    recv_sem=recv_a.at[3 * i + j], device_id=(px, py, c), device_id_type=_MESH)
                cp.start()
                first.append(cp)
        _rope_tables_into(pos_ref, c_ref, *tables)
        for i in range(nw):
            for j, (px, py) in enumerate(peers):
                landed = place(i, 2 * px + py, c)
                pltpu.make_async_remote_copy(
                    src_ref=landed, dst_ref=landed, send_sem=send_a.at[3 * i + j], recv_sem=recv_a.at[3 * i + j],
                    device_id=(px, py, c), device_id_type=_MESH).wait_recv()
                if splits[i] is not None:
                    cp = pltpu.make_async_remote_copy(
                        src_ref=landed, dst_ref=landed, send_sem=send_b.at[3 * i + j], recv_sem=recv_b.at[3 * i + j],
                        device_id=(x, y, 1 - c), device_id_type=_MESH)
                    cp.start()
                    passed.append(cp)
        for i in range(nw):
            if splits[i] is None:
                continue
            for j, (px, py) in enumerate(peers):
                theirs = place(i, 2 * px + py, 1 - c)
                pltpu.make_async_remote_copy(
                    src_ref=theirs, dst_ref=theirs, send_sem=send_b.at[3 * i + j], recv_sem=recv_b.at[3 * i + j],
                    device_id=(x, y, 1 - c), device_id_type=_MESH).wait_recv()
        for cp in first + passed:
            cp.wait_send()

    vm = pl.BlockSpec(memory_space=pltpu.VMEM)
    tab = jax.ShapeDtypeStruct((t, LANE), _F32)
    return pl.pallas_call(
        body, name="gather_weights",
        in_specs=[_ANY] * nw + [vm, vm], out_specs=[_ANY] * nw + [vm] * 3,
        out_shape=[jax.ShapeDtypeStruct(f, s.dtype) for f, s in zip(fulls, shards)] + [tab] * 3,
        scratch_shapes=[pltpu.SemaphoreType.DMA((3 * nw,)) for _ in range(4)],
        compiler_params=_cparams(None, VMEM_LIMIT),
    )(*shards, pos_f, _rope_consts())


def _assemble_w_in_t(slots):
    bw = 256
    ov = WINDOW_ROWS - WINDOW_STEP

    def body(s_ref, o_ref):
        for k in range(4):
            base = k * WINDOW_STEP
            lo = 0 if k == 0 else ov
            if k > 0:
                o_ref[base:base + ov, :] = s_ref[k - 1, WINDOW_STEP:WINDOW_ROWS, :] + s_ref[k, 0:ov, :]
            hi = WINDOW_ROWS if k == 3 else WINDOW_STEP
            o_ref[base + lo:base + hi, :] = s_ref[k, lo:hi, :]

    return pl.pallas_call(
        body, name="assemble_w_in_t", grid=(D_MODEL // bw,),
        in_specs=[pl.BlockSpec((4, WINDOW_ROWS, bw), lambda i: (0, 0, i))],
        out_specs=pl.BlockSpec((D_IN, bw), lambda i: (0, i)),
        out_shape=jax.ShapeDtypeStruct((D_IN, D_MODEL), slots.dtype),
        compiler_params=_cparams(("parallel",)),
    )(slots)


def _pair_exchange(grads, splits):
    nw = len(grads)

    def body(*refs):
        ins, outs = refs[:nw], refs[nw:2 * nw]
        send, recv = refs[2 * nw:]
        x, y, c = _place()
        copies = []
        for i in range(nw):
            cp = pltpu.make_async_remote_copy(
                src_ref=splits[i].half(ins[i], 1 - c), dst_ref=outs[i], send_sem=send.at[i], recv_sem=recv.at[i],
                device_id=(x, y, 1 - c), device_id_type=_MESH)
            cp.start()
            copies.append(cp)
        for cp in copies:
            cp.wait()

    return pl.pallas_call(
        body, name="grad_pair_exchange",
        in_specs=[_ANY] * nw, out_specs=[_ANY] * nw,
        out_shape=[jax.ShapeDtypeStruct(sp.half_shape(g.shape), g.dtype) for g, sp in zip(grads, splits)],
        scratch_shapes=[pltpu.SemaphoreType.DMA((nw,)), pltpu.SemaphoreType.DMA((nw,))],
    )(*grads)


def _row_block(rows):
    for cand in (976, 176, 256, 128):
        if rows % cand == 0:
            return cand
    return rows


def _pair_sum(g, r, split, c_arr, name):
    hr, hn = r.shape
    br = _row_block(hr)
    if split.by_rows:
        g_spec = pl.BlockSpec((br, hn), lambda i, c_ref: (i, c_ref[0]))
    else:
        g_spec = pl.BlockSpec((br, hn), lambda i, c_ref: (c_ref[0] * (hr // br) + i, 0))

    def body(c_ref, g_ref, r_ref, o_ref):
        o_ref[...] = (g_ref[...] + r_ref[...]).astype(o_ref.dtype)

    return pl.pallas_call(
        body, name=name,
        grid_spec=pltpu.PrefetchScalarGridSpec(
            num_scalar_prefetch=1, grid=(hr // br,),
            in_specs=[g_spec, pl.BlockSpec((br, hn), lambda i, c_ref: (i, 0))],
            out_specs=pl.BlockSpec((br, hn), lambda i, c_ref: (i, 0))),
        out_shape=jax.ShapeDtypeStruct(r.shape, _MX),
        compiler_params=_cparams(("parallel",)),
    )(c_arr, g, r)


_HBM = pl.BlockSpec(memory_space=pltpu.HBM)
_SEM = pl.BlockSpec(memory_space=pltpu.SEMAPHORE)
_FLOWS = pltpu.SideEffectType.DATAFLOW_SIDE_EFFECTING


def _chip_exchange_copies(refs, send, recv, splits):
    nw = len(refs) // 2
    x, y, c = _place()
    me = 2 * x + y
    copies = []
    for i in range(nw):
        for px, py in _other_chips(x, y):
            copies.append((splits[i].chip_part(refs[i], 2 * px + py), refs[nw + i].at[me], (px, py, c)))
    return [pltpu.make_async_remote_copy(src_ref=src, dst_ref=dst, send_sem=send.at[k], recv_sem=recv.at[k],
                                         device_id=peer, device_id_type=_MESH)
            for k, (src, dst, peer) in enumerate(copies)]


def _late_gather_copies(refs, send, recv, splits):
    nw = len(refs) // 2
    x, y, c = _place()
    me = 2 * x + y
    copies = []
    for i in range(nw):
        for px, py in _other_chips(x, y):
            copies.append((refs[i], splits[i].chip_part(refs[nw + i], me), (px, py, c)))
    return [pltpu.make_async_remote_copy(src_ref=src, dst_ref=dst, send_sem=send.at[k], recv_sem=recv.at[k],
                                         device_id=peer, device_id_type=_MESH)
            for k, (src, dst, peer) in enumerate(copies)]


def _start_copies(name, flying, copies_of, n_copies, after=None):
    first = [] if after is None else [after]

    def body(*refs):
        ins = refs[:len(flying)]
        send, recv = refs[len(flying) + len(first):len(flying) + len(first) + 2]
        token = refs[-1]
        for cp in copies_of(ins, send, recv):
            cp.start()
        token[...] = jnp.zeros_like(token)

    outs = pl.pallas_call(
        body, name=name,
        in_specs=[_HBM] * len(flying) + [_ANY] * len(first),
        out_specs=[_SEM, _SEM] + [_HBM] * len(flying) + [pl.BlockSpec(memory_space=pltpu.VMEM)],
        out_shape=[pltpu.SemaphoreType.DMA((n_copies,)), pltpu.SemaphoreType.DMA((n_copies,))]
        + [pltpu.HBM(f.shape, f.dtype) for f in flying] + [jax.ShapeDtypeStruct((8, LANE), _F32)],
        input_output_aliases={i: 2 + i for i in range(len(flying))},
        compiler_params=pltpu.CompilerParams(has_side_effects=_FLOWS),
    )(*[pltpu.with_memory_space_constraint(f, pltpu.HBM) for f in flying], *first)
    return outs[0], outs[1], outs[2:2 + len(flying)], outs[-1]


def _wait_copies(name, send, recv, flying, copies_of, after):
    def body(*refs):
        ins = refs[:len(flying)]
        send_ref, recv_ref = refs[len(flying):len(flying) + 2]
        for cp in copies_of(ins, send_ref, recv_ref):
            cp.wait_send()
            cp.wait_recv()

    return pl.pallas_call(
        body, name=name,
        in_specs=[_HBM] * len(flying) + [_SEM, _SEM, _ANY],
        out_specs=[_HBM] * len(flying),
        out_shape=[pltpu.HBM(f.shape, f.dtype) for f in flying],
        input_output_aliases={i: i for i in range(len(flying))},
        compiler_params=pltpu.CompilerParams(has_side_effects=_FLOWS),
    )(*flying, send, recv, after)


def _sum_chips(q, p, split, place_arr, name):
    _, hr, hn = q.shape
    if split.by_rows:
        out_shape = (hr, 2 * hn)
        o_spec = pl.BlockSpec((hr, hn), lambda i, pr: (0, pr[0]))
        p_spec = pl.BlockSpec((pl.Element(hr), pl.Element(hn)), lambda i, pr: (pr[1] * split.step, 0))
    else:
        out_shape = (2 * hr, hn)
        o_spec = pl.BlockSpec((hr, hn), lambda i, pr: (pr[0], 0))
        p_spec = pl.BlockSpec((hr, hn), lambda i, pr: (0, pr[1]))

    def body(pr, q_ref, p_ref, o_ref):
        f = lambda k: jnp.where(pr[1] == k, p_ref[...], q_ref[k]).astype(_F32)
        o_ref[...] = ((f(0) + f(1)) + f(2)) + f(3)

    return pl.pallas_call(
        body, name=name,
        grid_spec=pltpu.PrefetchScalarGridSpec(
            num_scalar_prefetch=1, grid=(1,),
            in_specs=[pl.BlockSpec((4, hr, hn), lambda i, pr: (0, 0, 0)), p_spec], out_specs=o_spec),
        out_shape=jax.ShapeDtypeStruct(out_shape, _F32),
        compiler_params=_cparams(("arbitrary",), VMEM_LIMIT),
    )(place_arr, q, p)


def _pair_share(bufs, splits, small):
    nw = len(bufs)

    def body(*refs):
        ins, small_ref, outs, all_ref = refs[:nw], refs[nw], refs[nw + 1:2 * nw + 1], refs[2 * nw + 1]
        send, recv, s_send, s_recv = refs[2 * nw + 2:]
        x, y, c = _place()
        copies = []
        for r in range(1, 8):
            peer = (1 - x if r & 4 else x, 1 - y if r & 2 else y, 1 - c if r & 1 else c)
            cp = pltpu.make_async_remote_copy(
                src_ref=small_ref, dst_ref=all_ref.at[4 * x + 2 * y + c], send_sem=s_send.at[r - 1],
                recv_sem=s_recv.at[r - 1], device_id=peer, device_id_type=_MESH)
            cp.start()
            copies.append(cp)
        for i in range(nw):
            cp = pltpu.make_async_remote_copy(
                src_ref=splits[i].half(ins[i], c), dst_ref=splits[i].half(outs[i], c), send_sem=send.at[i],
                recv_sem=recv.at[i], device_id=(x, y, 1 - c), device_id_type=_MESH)
            cp.start()
            copies.append(cp)
        for cp in copies:
            cp.wait()

    return pl.pallas_call(
        body, name="grad_pair_share",
        in_specs=[_ANY] * (nw + 1), out_specs=[_ANY] * (nw + 1),
        out_shape=[jax.ShapeDtypeStruct(b.shape, b.dtype) for b in bufs]
        + [jax.ShapeDtypeStruct((8,) + small.shape, small.dtype)],
        input_output_aliases={i: i for i in range(nw)},
        scratch_shapes=[pltpu.SemaphoreType.DMA((nw,)), pltpu.SemaphoreType.DMA((nw,)),
                        pltpu.SemaphoreType.DMA((7,)), pltpu.SemaphoreType.DMA((7,))],
    )(*bufs, small)


def _sum_devices(parts, own, dev_arr):
    def body(dev, p_ref, own_ref, tot_ref):
        f = lambda d: jnp.where(dev[0] == d, own_ref[...], p_ref[d])
        acc = f(0)
        for d in range(1, 8):
            acc = acc + f(d)
        tot_ref[...] = acc

    return pl.pallas_call(
        body, name="small_sum",
        grid_spec=pltpu.PrefetchScalarGridSpec(
            num_scalar_prefetch=1, grid=(1,),
            in_specs=[pl.BlockSpec(parts.shape, lambda i, dev: (0, 0, 0)), pl.BlockSpec(own.shape, lambda i, dev: (0, 0))],
            out_specs=pl.BlockSpec(own.shape, lambda i, dev: (0, 0))),
        out_shape=jax.ShapeDtypeStruct(own.shape, own.dtype),
    )(dev_arr, parts, own)


def _adam_update(w, g, m, v):
    m2 = ADAM_B1 * m + (1.0 - ADAM_B1) * g
    v2 = ADAM_B2 * v + (1.0 - ADAM_B2) * (g * g)
    m_hat = m2 / (1.0 - ADAM_B1 ** ADAM_STEP)
    v_hat = v2 / (1.0 - ADAM_B2 ** ADAM_STEP)
    return -ADAM_LR * (m_hat / (jnp.sqrt(v_hat) + ADAM_EPS) + ADAM_WD * w), m2, v2


SMALL_AT = dict(g_in=(0, 0), g_final=(1, 0), g_gla_norm=(2, 0), b_alpha=(2, B_WIDTH), attn_sinks=(2, B_WIDTH + B_KEY_WIDTH))
LOSS_AT = (2, B_WIDTH + B_KEY_WIDTH + LANE)
WUP_ROWS = (3, 7)


def _adamw_small(tot, g_wup, params):
    names = list(params)

    def body(*refs):
        tot_ref, gw_ref = refs[0], refs[1]
        ins = refs[2:2 + 3 * len(names)]
        outs = refs[2 + 3 * len(names):]
        for i, nm in enumerate(names):
            w_ref, m_ref, v_ref = ins[3 * i:3 * i + 3]
            if nm in SMALL_AT:
                r, a = SMALL_AT[nm]
                g = tot_ref[r:r + 1, a:a + w_ref.shape[1]]
            else:
                g = gw_ref[...]
            d, m2, v2 = _adam_update(w_ref[...], g, m_ref[...], v_ref[...])
            for o_ref, val in zip(outs[4 * i:4 * i + 4], (g, d, m2, v2)):
                o_ref[...] = val

    vm = pl.BlockSpec(memory_space=pltpu.VMEM)
    flat = [a for nm in names for a in params[nm]]
    out_shape = [jax.ShapeDtypeStruct(params[nm][0].shape, _F32) for nm in names for _ in range(4)]
    outs = pl.pallas_call(
        body, name="adamw_small", in_specs=[vm] * (2 + len(flat)), out_specs=[vm] * len(out_shape), out_shape=out_shape,
    )(tot, g_wup, *flat)
    return {nm: tuple(outs[4 * i:4 * i + 4]) for i, nm in enumerate(names)}


def _adamw(w, g, m, v, name):
    lead = w.shape[0] != 1
    r, n = (w.shape[0], w.shape[2]) if lead else w.shape[1:]
    br = r
    for cand in (256, 244, 128):
        if r > cand and r % cand == 0:
            br = cand
            break

    def body(w_ref, g_ref, m_ref, v_ref, d_ref, nm_ref, nv_ref):
        d_ref[...], nm_ref[...], nv_ref[...] = _adam_update(w_ref[...], g_ref[...], m_ref[...], v_ref[...])

    blk = pl.BlockSpec((br, 1, n), lambda i: (i, 0, 0)) if lead else pl.BlockSpec((None, br, n), lambda i: (0, i, 0))
    shp = jax.ShapeDtypeStruct(w.shape, _F32)
    return pl.pallas_call(
        body, name=name, grid=(r // br,),
        in_specs=[blk] * 4, out_specs=[blk] * 3, out_shape=[shp] * 3,
        compiler_params=_cparams(("parallel",)),
    )(w, g, m, v)


def kernel(x, positions, g_in, w_in, w_alpha_up, b_alpha, attn_sinks, g_gla_norm, w_out_a, w_out_b, w_o, g_final, loss_target, m_g_in, m_w_in, m_w_alpha_up, m_b_alpha, m_attn_sinks, m_g_gla_norm, m_w_out_a, m_w_out_b, m_w_o, m_g_final, v_g_in, v_w_in, v_w_alpha_up, v_b_alpha, v_attn_sinks, v_g_gla_norm, v_w_out_a, v_w_out_b, v_w_o, v_g_final):
    nseq, seq, _ = x.shape
    t = nseq * seq
    cx, cy, cc = _place()
    chip = 2 * cx + cy
    c_arr = jnp.reshape(cc, (1,)).astype(jnp.int32)

    tr = lambda w: jnp.transpose(w, (2, 0, 1))
    w_in_t = tr(w_in).reshape(SHARD, D_MODEL).astype(_MX)
    pad = WINDOW_ROWS - SHARD
    window = lax.switch(chip, [lambda w, k=k: jnp.pad(w, ((4 * k, pad - 4 * k), (0, 0))) for k in range(4)], w_in_t)
    shards = [window, w_alpha_up[0].astype(_MX)]
    late = [w_out_a[0].astype(_MX), w_out_b[0].astype(_MX), w_o[0].astype(_MX)]
    late_splits = [SPLIT_W_OUT, SPLIT_W_OUT, SPLIT_W_O]
    splits = [SPLIT_W_IN_T, None]
    fulls = [(4, WINDOW_ROWS, D_MODEL), (4, B_GATE_RANK, B_KEY_WIDTH // 4)]
    pos_f = positions.astype(_F32).reshape(t, 1)
    win_g, wup_g, cos, sa, sb = _gather_weights(shards, splits, fulls, pos_f)
    late_copies = lambda refs, send, recv: _late_gather_copies(refs, send, recv, late_splits)
    late_full = [lax.empty(shape, _MX) for shape in ((A_WIDTH, D_MODEL), (B_WIDTH, D_MODEL), (D_MODEL, D_MODEL))]
    l_send, l_recv, l_flying, l_token = _start_copies("late_gather_start", late + late_full, late_copies,
                                                      3 * len(late), after=win_g)
    nsh = D_MODEL // 4
    win_g = lax.dynamic_update_slice(win_g, window[None], (chip, 0, 0))
    wup_g = lax.dynamic_update_slice(wup_g, shards[1][None], (chip, 0, 0))
    wt = _assemble_w_in_t(win_g)
    wup = jnp.concatenate([jnp.transpose(wup_g, (1, 0, 2)).reshape(B_GATE_RANK, B_KEY_WIDTH),
                           jnp.zeros((LANE - B_GATE_RANK, B_KEY_WIDTH), _MX)], axis=0)

    x2 = x.reshape(t, D_MODEL)
    tgt = loss_target.reshape(t, D_MODEL)
    sinks = attn_sinks.reshape(A_HEADS)
    gf = g_final.reshape(1, D_MODEL)

    h, qkv, za, qkb, vb, zb, alr, ga, gb = _in_proj(x2, g_in + l_token[0, 0], wt, cos, sa, sb)
    oa = _attn_fwd(qkv, za, sinks, nseq)
    ob, oraw, sst = _gla_fwd(qkb, vb, zb, alr, wup, b_alpha, g_gla_norm, nseq)

    mine_a, mine_b, mine_o, wa, wb, wo = _wait_copies("late_gather_wait", l_send, l_recv, l_flying, late_copies, ob)
    wa = lax.dynamic_update_slice(wa, mine_a, (0, nsh * chip))
    wb = lax.dynamic_update_slice(wb, mine_b, (0, nsh * chip))
    wo = lax.dynamic_update_slice(wo, mine_o, (nsh * chip, 0))
    dh2, doa, dob, dga, dgb, dwa, dwb, dwo, dgf, lossv = _merge_loss(oa, ob, ga, gb, x2, tgt, wa, wb, wo, gf)

    dqkv, dza, dsink = _attn_bwd(qkv, za, doa, sinks, cos, sa, sb, nseq)
    dqkb, dvb, dzb, dalr, dwup, dba, dgn = _gla_bwd(qkb, vb, zb, alr, oraw, dob, sst, wup, b_alpha, g_gla_norm, nseq)
    dpieces = [dqkv, dza, dqkb, dvb, dzb, dalr, dga, dgb]
    dwin_t = _in_proj_bwd_w(h, dpieces)

    grads = [dwin_t, dwa, dwb, dwo]
    gsplits = [SPLIT_W_IN_T, SPLIT_W_OUT, SPLIT_W_OUT, SPLIT_W_O]
    names = ("w_in", "w_out_a", "w_out_b", "w_o")
    from_sibling = _pair_exchange(grads, gsplits)
    pair_sums = [_pair_sum(g, r, sp, c_arr, "pair_sum_" + nm)
                 for g, r, sp, nm in zip(grads, from_sibling, gsplits, names)]
    exchange = lambda refs, send, recv: _chip_exchange_copies(refs, send, recv, gsplits)
    lands = [lax.empty((4,) + sp.part_shape(p.shape), p.dtype) for p, sp in zip(pair_sums, gsplits)]
    send, recv, flying, token = _start_copies("grad_chip_exchange_start", pair_sums + lands, exchange, 3 * len(lands))
    grad_x2, dgin = _in_proj_bwd_x(dpieces, wt, x2, dh2, g_in + token[0, 0])
    landed = _wait_copies("grad_chip_exchange_wait", send, recv, flying, exchange, grad_x2)
    place_arr = jnp.stack([cc, chip]).astype(jnp.int32)
    reduced = [_sum_chips(q, p, sp, place_arr, "chip_sum_" + nm)
               for q, p, sp, nm in zip(landed[len(lands):], landed[:len(lands)], gsplits, names)]
    row2 = jnp.concatenate([dgn, dba, jnp.pad(dsink[:, 0].reshape(1, A_HEADS), ((0, 0), (0, LANE - A_HEADS))),
                            jnp.pad(jnp.sum(lossv, axis=1, keepdims=True), ((0, 0), (0, LANE - 1)))], axis=1)
    small = jnp.concatenate([dgin, dgf, row2, dwup[:B_GATE_RANK].reshape(WUP_ROWS[1] - WUP_ROWS[0], D_MODEL),
                             jnp.zeros((1, D_MODEL), _F32)], axis=0)
    g_window, g_wa, g_wb, g_wo, small_parts = _pair_share(reduced, gsplits, small)
    g_win_t = lax.switch(chip, [lambda w, k=k: w[4 * k:4 * k + SHARD].reshape(SHARD, 1, D_MODEL) for k in range(4)],
                         g_window)
    dev_arr = jnp.reshape(2 * chip + cc, (1,)).astype(jnp.int32)
    tot = _sum_devices(small_parts, small, dev_arr)
    loss = tot[LOSS_AT]
    nup = B_KEY_WIDTH // 4
    g_wup = lax.dynamic_slice(tot[WUP_ROWS[0]:WUP_ROWS[1]].reshape(B_GATE_RANK, B_KEY_WIDTH), (0, chip * nup),
                              (B_GATE_RANK, nup))

    row = lambda a: a.reshape(1, -1)
    sm = _adamw_small(tot, g_wup, dict(
        g_in=(g_in, m_g_in, v_g_in), g_final=(row(g_final), row(m_g_final), row(v_g_final)),
        g_gla_norm=(g_gla_norm, m_g_gla_norm, v_g_gla_norm), b_alpha=(b_alpha, m_b_alpha, v_b_alpha),
        attn_sinks=(attn_sinks, m_attn_sinks, v_attn_sinks),
        w_alpha_up=(w_alpha_up[0], m_w_alpha_up[0], v_w_alpha_up[0])))
    sm["g_final"] = tuple(a.reshape(D_MODEL) for a in sm["g_final"])
    sm["w_alpha_up"] = tuple(a[None] for a in sm["w_alpha_up"])

    untr = lambda a: jnp.transpose(a, (1, 2, 0))
    big = dict(w_in=tuple(untr(a) for a in (g_win_t,) + tuple(_adamw(tr(w_in), g_win_t, tr(m_w_in), tr(v_w_in), "adamw_w_in"))))
    for nm, w, g, m, v in (("w_out_a", w_out_a, g_wa, m_w_out_a, v_w_out_a),
                           ("w_out_b", w_out_b, g_wb, m_w_out_b, v_w_out_b), ("w_o", w_o, g_wo, m_w_o, v_w_o)):
        big[nm] = (g[None],) + tuple(_adamw(w, g[None], m, v, "adamw_" + nm))

    order = ("g_in", "w_in", "w_alpha_up", "b_alpha", "attn_sinks", "g_gla_norm", "w_out_a", "w_out_b", "w_o", "g_final")
    outs = [big[nm][kind] if nm in big else sm[nm][kind] for kind in range(4) for nm in order]
    return (loss, grad_x2.reshape(x.shape), *outs)
```

```python
import math
from typing import NamedTuple

import numpy as np
import jax
import jax.numpy as jnp
from jax import lax
from jax.experimental import pallas as pl
from jax.experimental.pallas import tpu as pltpu

D_MODEL = 1024
A_HEADS, A_KV_HEADS, A_HEAD_DIM = 8, 2, 64
A_GROUP = A_HEADS // A_KV_HEADS
A_WIDTH, A_KV_WIDTH = 512, 128
BLOCK = 128
ROPE_THETA = 500000.0
ROPE_DIM = 16
B_HEADS, B_KEY_DIM, B_VAL_DIM = 4, 64, 128
B_KEY_WIDTH, B_WIDTH = 256, 512
B_GATE_RANK = 16
B_GATE_TEMP = 16.0
B_CHUNK = 64
NORM_EPS = 1e-6
NEG_BIG = -1e30
D_IN = 4880

ADAM_LR, ADAM_B1, ADAM_B2, ADAM_EPS, ADAM_WD, ADAM_STEP = 0.001, 0.9, 0.999, 1e-08, 0.01, 10

LANE = 128
ALR_AT = 2816
PIECES = (("qkv", 0, 768), ("za", 768, 1280), ("qkb", 1280, 1792), ("vb", 1792, 2304),
          ("zb", 2304, 2816), ("alr", ALR_AT, ALR_AT + LANE), ("ga", 2832, 3856), ("gb", 3856, 4880))
SHARD = D_IN // 4
WINDOW_STEP = 1216
WINDOW_ROWS = 1232

GLA_BLOCK = 256
MERGE_SLAB = 16
VMEM_LIMIT = 56 * 1024 * 1024

_F32 = jnp.float32
_MX = jnp.bfloat16
_ST = jnp.bfloat16

_MESH = pl.DeviceIdType.MESH
_ANY = pl.BlockSpec(memory_space=pl.ANY)


def _cparams(sem=None, vmem=None):
    return pltpu.CompilerParams(dimension_semantics=sem, vmem_limit_bytes=vmem)


def _dot(a, b):
    return jnp.dot(a.astype(_MX), b.astype(_MX), preferred_element_type=_F32)


def _dot_nt(a, b):
    return lax.dot_general(a.astype(_MX), b.astype(_MX), (((1,), (1,)), ((), ())),
                           preferred_element_type=_F32)


def _dot_tn(a, b):
    return lax.dot_general(a.astype(_MX), b.astype(_MX), (((0,), (0,)), ((), ())),
                           preferred_element_type=_F32)


def _dot_ones(ones_mat, v):
    o = ones_mat.astype(jnp.bfloat16)
    v0 = v.astype(jnp.bfloat16)
    r1 = v - v0.astype(_F32)
    v1 = r1.astype(jnp.bfloat16)
    v2 = (r1 - v1.astype(_F32)).astype(jnp.bfloat16)
    d = lambda t: jnp.dot(o, t, preferred_element_type=_F32)
    return d(v0) + d(v1) + d(v2)


def _sigmoid(x):
    return 0.5 * jnp.tanh(0.5 * x) + 0.5


def _log_sigmoid(x):
    return jnp.minimum(x, 0.0) - jnp.log(1.0 + jnp.exp(-jnp.abs(x)))


def _lane_tile(t, width):
    reps = width // t.shape[1]
    return t if reps == 1 else jnp.tile(t, (1, reps))


def _rope(t, cos, sa, sb, sign):
    w = t.shape[1]
    rot = pltpu.roll(t, w - 8, 1) * _lane_tile(sa, w) + pltpu.roll(t, 8, 1) * _lane_tile(sb, w)
    return t * _lane_tile(cos, w) + sign * rot


def _rms_bwd(dy_g, n, r):
    return r * (dy_g - n * jnp.mean(dy_g * n, axis=-1, keepdims=True))


ROPE_ROWS = 256


def _rope_consts():
    lane = np.arange(LANE) % A_HEAD_DIM
    half = ROPE_DIM // 2
    inv = np.exp((np.float32(-math.log(ROPE_THETA)) * np.arange(half, dtype=np.float32)) * np.float32(2.0 / ROPE_DIM))
    consts = np.zeros((8, LANE), np.float32)
    consts[0] = np.where(lane < ROPE_DIM, inv[lane % half], 0.0)
    consts[1] = np.where(lane < half, -1.0, 0.0)
    consts[2] = np.where((lane >= half) & (lane < ROPE_DIM), 1.0, 0.0)
    return jnp.asarray(consts)


def _rope_tables_into(pos_ref, c_ref, cos_ref, sa_ref, sb_ref):
    def rows_of(b, carry):
        rows = pl.ds(pl.multiple_of(b * ROPE_ROWS, ROPE_ROWS), ROPE_ROWS)
        ang = pos_ref[rows, :] * c_ref[0:1, :]
        s = jnp.sin(ang)
        cos_ref[rows, :] = jnp.cos(ang)
        sa_ref[rows, :] = s * c_ref[1:2, :]
        sb_ref[rows, :] = s * c_ref[2:3, :]
        return carry

    lax.fori_loop(0, pos_ref.shape[0] // ROPE_ROWS, rows_of, 0)


def _in_proj(x2, g_in, wt, cos, sa, sb):
    t = x2.shape[0]
    tm = min(t, 512)

    def body(x_ref, g_ref, w_ref, cos_ref, sa_ref, sb_ref, h_ref, qkv_ref, za_ref, qkb_ref,
             vb_ref, zb_ref, alr_ref, ga_ref, gb_ref):
        xv = x_ref[...]
        r = lax.rsqrt(jnp.mean(xv * xv, axis=-1, keepdims=True) + NORM_EPS)
        h = (xv * r * g_ref[...]).astype(_MX)
        h_ref[...] = h.astype(_ST)
        outs = dict(za=za_ref, qkb=qkb_ref, vb=vb_ref, zb=zb_ref, alr=alr_ref, ga=ga_ref, gb=gb_ref)
        for name, a, b in PIECES:
            p = _dot_nt(h, w_ref[a:b, :])
            if name == "qkv":
                c, s1, s2 = cos_ref[...], sa_ref[...], sb_ref[...]
                qkv_ref[:, 0:512] = _rope(p[:, 0:512], c, s1, s2, 1.0).astype(_ST)
                qkv_ref[:, 512:640] = _rope(p[:, 512:640], c, s1, s2, 1.0).astype(_ST)
                qkv_ref[:, 640:768] = p[:, 640:768].astype(_ST)
            else:
                outs[name][...] = p.astype(outs[name].dtype)

    rows = lambda w: pl.BlockSpec((tm, w), lambda i: (i, 0))
    shp = lambda name, w: jax.ShapeDtypeStruct((t, w), _F32 if name == "qkb" else _ST)
    widths = [D_MODEL] + [b - a for _, a, b in PIECES]
    return pl.pallas_call(
        body, name="in_proj", grid=(t // tm,),
        in_specs=[rows(D_MODEL), pl.BlockSpec((1, D_MODEL), lambda i: (0, 0)),
                  pl.BlockSpec((D_IN, D_MODEL), lambda i: (0, 0), pipeline_mode=pl.Buffered(1)),
                  rows(LANE), rows(LANE), rows(LANE)],
        out_specs=[rows(w) for w in widths],
        out_shape=[shp(n, w) for n, w in zip(["h"] + [p[0] for p in PIECES], widths)],
        compiler_params=_cparams(("parallel",), VMEM_LIMIT),
    )(x2, g_in, wt, cos, sa, sb)


def _attn_operands(k_prev, k_cur, v_prev, v_cur, want_bwd):
    kf = jnp.concatenate([k_prev, k_cur], axis=0).astype(_F32) * (A_HEAD_DIM ** -0.5)
    vf = jnp.concatenate([v_prev, v_cur], axis=0).astype(_F32)
    lo = lax.broadcasted_iota(jnp.int32, (1, LANE), 1) < 64

    def on_lanes(a):
        sw = pltpu.roll(a, 64, 1)
        z = jnp.zeros_like(a)
        return [[jnp.where(lo, a, z).astype(_MX), jnp.where(lo, z, sw).astype(_MX)],
                [jnp.where(lo, sw, z).astype(_MX), jnp.where(lo, z, a).astype(_MX)]]

    def on_rows(a):
        at = a.T.astype(_MX)
        z = jnp.zeros((64, at.shape[1]), _MX)
        top, bot = at[0:64], at[64:128]
        return [[jnp.concatenate([top, z], axis=0), jnp.concatenate([z, top], axis=0)],
                [jnp.concatenate([bot, z], axis=0), jnp.concatenate([z, bot], axis=0)]]

    ops = dict(k_lanes=on_lanes(kf), v_rows=on_rows(vf), lo=lo)
    if want_bwd:
        ops.update(v_lanes=on_lanes(vf), k_rows=on_rows(kf))
    return ops


def _attn_valid(n):
    kj = lax.broadcasted_iota(jnp.int32, (2 * BLOCK, 2 * BLOCK), 0) - BLOCK
    qi = lax.broadcasted_iota(jnp.int32, (2 * BLOCK, 2 * BLOCK), 1) & (BLOCK - 1)
    return (kj <= qi) & (qi - kj < BLOCK) & ((n > 0) | (kj >= 0))


def _attn_sinks(sink_ref, h_a, h_b):
    first = lax.broadcasted_iota(jnp.int32, (1, 2 * BLOCK), 1) < BLOCK
    return jnp.where(first, sink_ref[h_a], sink_ref[h_b])


def _attn_softmax_t(k_lanes, q_pair, valid, sink):
    s = jnp.where(valid, _dot_nt(k_lanes, q_pair), NEG_BIG)
    m = jnp.maximum(jnp.max(s, axis=0, keepdims=True), sink)
    e = jnp.exp(s - m)
    e_sink = jnp.exp(sink - m)
    inv = 1.0 / (jnp.sum(e, axis=0, keepdims=True) + e_sink)
    return e, e_sink, inv


ATTN_TILE = 4


def _attn_kv(qkv_ref, kvp_ref, j):
    rows = slice(j * BLOCK, (j + 1) * BLOCK)
    if j == 0:
        k_prev, v_prev = kvp_ref[:, 0:128], kvp_ref[:, 128:256]
    else:
        before = slice((j - 1) * BLOCK, j * BLOCK)
        k_prev, v_prev = qkv_ref[before, 512:640], qkv_ref[before, 640:768]
    return k_prev, qkv_ref[rows, 512:640], v_prev, qkv_ref[rows, 640:768]


def _attn_fwd(qkv, za, sinks, nseq):
    t = qkv.shape[0]
    tile = ATTN_TILE * BLOCK
    nt = t // nseq // tile

    def body(sink_ref, qkv_ref, kvp_ref, za_ref, oa_ref):
        for j in range(ATTN_TILE):
            rows = slice(j * BLOCK, (j + 1) * BLOCK)
            ops = _attn_operands(*_attn_kv(qkv_ref, kvp_ref, j), False)
            valid = _attn_valid(ATTN_TILE * pl.program_id(1) + j)[:, 0:BLOCK]
            for pr in range(A_HEADS // 2):
                lanes = slice(pr * LANE, (pr + 1) * LANE)
                g = pr // (A_GROUP // 2)
                q_pair = qkv_ref[rows, lanes]
                ot = None
                for half in range(2):
                    e, _, inv = _attn_softmax_t(ops["k_lanes"][g][half], q_pair, valid, sink_ref[2 * pr + half])
                    part = _dot(ops["v_rows"][g][half], e) * inv
                    ot = part if ot is None else ot + part
                z = za_ref[rows, lanes].astype(_F32)
                oa_ref[rows, lanes] = (ot.T * (z * _sigmoid(z))).astype(_ST)

    cur = lambda w: pl.BlockSpec((tile, w), lambda s, n: (s * nt + n, 0))
    return pl.pallas_call(
        body, name="attn_fwd", grid=(nseq, nt),
        in_specs=[pl.BlockSpec(memory_space=pltpu.SMEM), cur(768),
                  pl.BlockSpec((BLOCK, 256), lambda s, n: (ATTN_TILE * (s * nt + n) - jnp.minimum(n, 1), 2)),
                  cur(512)],
        out_specs=cur(512), out_shape=jax.ShapeDtypeStruct((t, A_WIDTH), _ST),
        compiler_params=_cparams(("parallel", "arbitrary")),
    )(sinks, qkv, qkv, za)


def _attn_bwd(qkv, za, doa, sinks, cos, sa, sb, nseq):
    t = qkv.shape[0]
    tile = ATTN_TILE * BLOCK
    nt = t // nseq // tile

    def body(sink_ref, qkv_ref, kvp_ref, za_ref, doa_ref, cos_ref, sa_ref, sb_ref,
             dqkv_ref, dza_ref, dsink_ref, ck_ref, cv_ref):
        s_id, i = pl.program_id(0), pl.program_id(1)

        @pl.when((s_id == 0) & (i == 0))
        def _():
            dsink_ref[...] = jnp.zeros_like(dsink_ref)

        @pl.when(i == 0)
        def _():
            ck_ref[...] = jnp.zeros_like(ck_ref)
            cv_ref[...] = jnp.zeros_like(cv_ref)

        carry_k, carry_v = ck_ref[...], cv_ref[...]
        for j in reversed(range(ATTN_TILE)):
            rows = slice(j * BLOCK, (j + 1) * BLOCK)
            ops = _attn_operands(*_attn_kv(qkv_ref, kvp_ref, j), True)
            lo = ops["lo"]
            valid = _attn_valid(ATTN_TILE * (nt - 1 - i) + j)
            dk_acc, dv_acc, dq_pairs = [], [], []
            for g in range(A_KV_HEADS):
                pairs = [slice((2 * g + p) * LANE, (2 * g + p + 1) * LANE) for p in range(2)]
                q_both = jnp.concatenate([qkv_ref[rows, p] for p in pairs], axis=0)
                q_f = q_both.astype(_F32)
                z = [za_ref[rows, p].astype(_F32) for p in pairs]
                sz = [_sigmoid(v) for v in z]
                d_oa = [doa_ref[rows, p].astype(_F32) for p in pairs]
                d_att = jnp.concatenate([d_oa[p] * (z[p] * sz[p]) for p in range(2)], axis=0)
                zero = jnp.zeros_like(d_att)
                ot, dqt, ds_all, pn_all, qz_all, daz_all = None, None, [], [], [], []
                for half in range(2):
                    heads = (4 * g + half, 4 * g + 2 + half)
                    e, e_sink, inv = _attn_softmax_t(ops["k_lanes"][g][half], q_both, valid,
                                                     _attn_sinks(sink_ref, *heads))
                    pn = e * inv
                    dpt = _dot_nt(ops["v_lanes"][g][half], d_att)
                    delta = jnp.sum(pn * dpt, axis=0, keepdims=True)
                    ds = (pn * (dpt - delta)).astype(_MX)
                    pn = pn.astype(_MX)
                    d_sink = e_sink * inv * delta
                    for p, h in enumerate(heads):
                        dsink_ref[h:h + 1, :] = dsink_ref[h:h + 1, :] - jnp.sum(d_sink[:, p * BLOCK:(p + 1) * BLOCK])
                    o_part = _dot(ops["v_rows"][g][half], pn)
                    dq_part = _dot(ops["k_rows"][g][half], ds)
                    ot = o_part if ot is None else ot + o_part
                    dqt = dq_part if dqt is None else dqt + dq_part
                    mine = lo if half == 0 else jnp.logical_not(lo)
                    ds_all.append(ds)
                    pn_all.append(pn)
                    qz_all.append(jnp.where(mine, q_f, zero).astype(_MX))
                    daz_all.append(jnp.where(mine, d_att, zero).astype(_MX))
                dk_acc.append(_dot(jnp.concatenate(ds_all, axis=1), jnp.concatenate(qz_all, axis=0)))
                dv_acc.append(_dot(jnp.concatenate(pn_all, axis=1), jnp.concatenate(daz_all, axis=0)))
                for p, lanes in enumerate(pairs):
                    cols = slice(p * BLOCK, (p + 1) * BLOCK)
                    dza_ref[rows, lanes] = (d_oa[p] * ot[:, cols].T * (sz[p] * (1.0 + z[p] * (1.0 - sz[p])))).astype(_ST)
                    dq_pairs.append(dqt[:, cols].T)

            def fold(acc, scale):
                both = [a + pltpu.roll(a, 64, 1) for a in acc]
                return jnp.where(lo, both[0], both[1]) * scale

            dk_full = fold(dk_acc, A_HEAD_DIM ** -0.5)
            dv_full = fold(dv_acc, 1.0)
            dk_cur, dv_cur = dk_full[BLOCK:] + carry_k, dv_full[BLOCK:] + carry_v
            carry_k, carry_v = dk_full[:BLOCK], dv_full[:BLOCK]
            c, s1, s2 = cos_ref[rows, :], sa_ref[rows, :], sb_ref[rows, :]
            dqkv_ref[rows, 0:512] = _rope(jnp.concatenate(dq_pairs, axis=1), c, s1, s2, -1.0).astype(_ST)
            dqkv_ref[rows, 512:640] = _rope(dk_cur, c, s1, s2, -1.0).astype(_ST)
            dqkv_ref[rows, 640:768] = dv_cur.astype(_ST)
        ck_ref[...] = carry_k
        cv_ref[...] = carry_v

    cur = lambda w: pl.BlockSpec((tile, w), lambda s, i: (s * nt + nt - 1 - i, 0))
    return pl.pallas_call(
        body, name="attn_bwd", grid=(nseq, nt),
        in_specs=[pl.BlockSpec(memory_space=pltpu.SMEM), cur(768),
                  pl.BlockSpec((BLOCK, 256),
                               lambda s, i: (ATTN_TILE * (s * nt + nt - 1 - i) - jnp.minimum(nt - 1 - i, 1), 2)),
                  cur(512), cur(512), cur(LANE), cur(LANE), cur(LANE)],
        out_specs=[cur(768), cur(512), pl.BlockSpec((8, LANE), lambda s, i: (0, 0))],
        out_shape=[jax.ShapeDtypeStruct((t, 768), _ST), jax.ShapeDtypeStruct((t, 512), _ST),
                   jax.ShapeDtypeStruct((8, LANE), _F32)],
        scratch_shapes=[pltpu.VMEM((BLOCK, A_KV_WIDTH), _F32), pltpu.VMEM((BLOCK, A_KV_WIDTH), _F32)],
        compiler_params=_cparams(("arbitrary", "arbitrary")),
    )(sinks, qkv, qkv, za, doa, cos, sa, sb)


def _gla_chunk_terms(la, qkb_ref, r0):
    g = la[r0:r0 + B_CHUNK, :]
    ri = lax.broadcasted_iota(jnp.int32, (B_CHUNK, B_CHUNK), 0)
    ci = lax.broadcasted_iota(jnp.int32, (B_CHUNK, B_CHUNK), 1)
    cum = _dot_ones((ri >= ci).astype(_F32), g)
    last = cum[B_CHUNK - 1:B_CHUNK, :]
    mid = cum[B_CHUNK // 2 - 1:B_CHUNK // 2, :]
    q = qkb_ref[r0:r0 + B_CHUNK, 0:B_KEY_WIDTH].astype(_F32) * (B_KEY_DIM ** -0.5)
    k = qkb_ref[r0:r0 + B_CHUNK, B_KEY_WIDTH:2 * B_KEY_WIDTH].astype(_F32)
    e_q, e_k, e_l, e_c = jnp.exp(cum - mid), jnp.exp(mid - cum), jnp.exp(last - cum), jnp.exp(cum)
    dec_col = jnp.exp(jnp.sum(g.T, axis=1, keepdims=True))
    return dict(qm=q * e_q, km=k * e_k, kl=k * e_l, qc=q * e_c, e_q=e_q, e_k=e_k, e_l=e_l, e_c=e_c,
                dec_col=dec_col, dec_row=jnp.exp(last), causal=ri >= ci, ri=ri)


def _gate_logits(alr_ref, wup_ref, b_ref):
    return _dot(alr_ref[...], wup_ref[...]) + b_ref[...]


def _gla_fwd(qkb, vb, zb, alr, wup, b_alpha, gn, nseq):
    t = qkb.shape[0]
    tb = min(GLA_BLOCK, t // nseq)
    nblk = t // nseq // tb
    cpb = tb // B_CHUNK

    def body(qkb_ref, vb_ref, zb_ref, alr_ref, wup_ref, b_ref, gn_ref, ob_ref, oraw_ref, sst_ref, s_ref):
        @pl.when(pl.program_id(1) == 0)
        def _():
            s_ref[...] = jnp.zeros_like(s_ref)

        la = _log_sigmoid(_gate_logits(alr_ref, wup_ref, b_ref)) * (1.0 / B_GATE_TEMP)
        terms = [_gla_chunk_terms(la, qkb_ref, c * B_CHUNK) for c in range(cpb)]
        o_intra, inc = {}, {}
        for c, tm in enumerate(terms):
            for h in range(B_HEADS):
                kl_, vl_ = slice(h * 64, (h + 1) * 64), slice(h * 128, (h + 1) * 128)
                v = vb_ref[c * B_CHUNK:(c + 1) * B_CHUNK, vl_]
                a = jnp.where(tm["causal"], _dot_nt(tm["qm"][:, kl_], tm["km"][:, kl_]), 0.0)
                o_intra[c, h] = _dot(a, v)
                inc[c, h] = _dot_tn(tm["kl"][:, kl_], v)
        o_heads = {}
        for h in range(B_HEADS):
            kl_ = slice(h * 64, (h + 1) * 64)
            st = s_ref[kl_, :]
            for c, tm in enumerate(terms):
                sst_ref[c, kl_, :] = st
                o_heads[c, h] = o_intra[c, h] + _dot(tm["qc"][:, kl_], st)
                st = tm["dec_col"][kl_, :] * st + inc[c, h]
            s_ref[kl_, :] = st
        o = jnp.concatenate([jnp.concatenate([o_heads[c, h] for h in range(B_HEADS)], axis=1)
                             for c in range(cpb)], axis=0)
        oraw_ref[...] = o
        z = zb_ref[...].astype(_F32)
        gate = z * _sigmoid(z)
        for h in range(B_HEADS):
            vl_ = slice(h * 128, (h + 1) * 128)
            oh = o[:, vl_]
            r = lax.rsqrt(jnp.mean(oh * oh, axis=-1, keepdims=True) + NORM_EPS)
            ob_ref[:, vl_] = ((oh * r) * gn_ref[:, vl_] * gate[:, vl_]).astype(_ST)

    rows = lambda w: pl.BlockSpec((tb, w), lambda s, i: (s * nblk + i, 0))
    full = lambda a, b: pl.BlockSpec((a, b), lambda s, i: (0, 0))
    return pl.pallas_call(
        body, name="gla_fwd", grid=(nseq, nblk),
        in_specs=[rows(512), rows(512), rows(512), rows(LANE), full(LANE, B_KEY_WIDTH),
                  full(1, B_KEY_WIDTH), full(1, B_WIDTH)],
        out_specs=[rows(512), rows(512),
                   pl.BlockSpec((cpb, B_KEY_WIDTH, B_VAL_DIM), lambda s, i: (s * nblk + i, 0, 0))],
        out_shape=[jax.ShapeDtypeStruct((t, B_WIDTH), _ST), jax.ShapeDtypeStruct((t, B_WIDTH), _F32),
                   jax.ShapeDtypeStruct((t // B_CHUNK, B_KEY_WIDTH, B_VAL_DIM), _F32)],
        scratch_shapes=[pltpu.VMEM((B_KEY_WIDTH, B_VAL_DIM), _F32)],
        compiler_params=_cparams(("parallel", "arbitrary")),
    )(qkb, vb, zb, alr, wup, b_alpha, gn)


def _gla_bwd(qkb, vb, zb, alr, oraw, dob, sst, wup, b_alpha, gn, nseq):
    t = qkb.shape[0]
    tb = min(GLA_BLOCK, t // nseq)
    nblk = t // nseq // tb
    cpb = tb // B_CHUNK

    def body(qkb_ref, vb_ref, zb_ref, alr_ref, oraw_ref, dob_ref, sst_ref, wup_ref, b_ref, gn_ref,
             dqkb_ref, dvb_ref, dzb_ref, dalr_ref, dwup_ref, db_ref, dgn_ref, ds_ref):
        s_id, i = pl.program_id(0), pl.program_id(1)

        @pl.when((s_id == 0) & (i == 0))
        def _():
            dwup_ref[...] = jnp.zeros_like(dwup_ref)
            db_ref[...] = jnp.zeros_like(db_ref)
            dgn_ref[...] = jnp.zeros_like(dgn_ref)

        @pl.when(i == 0)
        def _():
            ds_ref[...] = jnp.zeros_like(ds_ref)

        a_pre = _gate_logits(alr_ref, wup_ref, b_ref)
        la = _log_sigmoid(a_pre) * (1.0 / B_GATE_TEMP)

        z = zb_ref[...].astype(_F32)
        sz = _sigmoid(z)
        d_ob = dob_ref[...].astype(_F32)
        tg = d_ob * (z * sz)
        dsilu = sz * (1.0 + z * (1.0 - sz))
        do_cols, dgn_cols = [], []
        for h in range(B_HEADS):
            vl_ = slice(h * 128, (h + 1) * 128)
            oh = oraw_ref[:, vl_].astype(_F32)
            r = lax.rsqrt(jnp.mean(oh * oh, axis=-1, keepdims=True) + NORM_EPS)
            on = oh * r
            gnh = gn_ref[:, vl_]
            dzb_ref[:, vl_] = (d_ob[:, vl_] * (on * gnh) * dsilu[:, vl_]).astype(_ST)
            dgn_cols.append(jnp.sum(tg[:, vl_] * on, axis=0, keepdims=True))
            do_cols.append(_rms_bwd(tg[:, vl_] * gnh, on, r))
        dgn_ref[...] = dgn_ref[...] + jnp.concatenate(dgn_cols, axis=1)
        d_o = jnp.concatenate(do_cols, axis=1)

        ri = lax.broadcasted_iota(jnp.int32, (tb, tb), 0)
        ci = lax.broadcasted_iota(jnp.int32, (tb, tb), 1)
        same = (ri // B_CHUNK) == (ci // B_CHUNK)
        low = same & (ri >= ci)
        upto_mid = same & ((ci % B_CHUNK) < B_CHUNK // 2)
        sums = _dot_ones(jnp.concatenate([m.astype(_F32) for m in (low, same, upto_mid)], axis=0), la)
        cum, last, mid = sums[0:tb], sums[tb:2 * tb], sums[2 * tb:3 * tb]
        e_q, e_k, e_l, e_c = jnp.exp(cum - mid), jnp.exp(mid - cum), jnp.exp(last - cum), jnp.exp(cum)
        q = qkb_ref[:, 0:B_KEY_WIDTH] * (B_KEY_DIM ** -0.5)
        k = qkb_ref[:, B_KEY_WIDTH:2 * B_KEY_WIDTH]
        qm, km, kl, qc = q * e_q, k * e_k, k * e_l, q * e_c
        lane_head = lax.broadcasted_iota(jnp.int32, (1, B_KEY_WIDTH), 1) // B_KEY_DIM
        d_o_mx = d_o.astype(_MX)

        def on_diagonal(st):
            z = jnp.zeros((B_KEY_DIM, B_VAL_DIM), st.dtype)
            return jnp.concatenate([jnp.concatenate(
                [st[h * B_KEY_DIM:(h + 1) * B_KEY_DIM] if g == h else z for g in range(B_HEADS)], axis=1)
                for h in range(B_HEADS)], axis=0)

        def diagonal_of(full):
            return jnp.concatenate([full[h * B_KEY_DIM:(h + 1) * B_KEY_DIM, h * B_VAL_DIM:(h + 1) * B_VAL_DIM]
                                    for h in range(B_HEADS)], axis=0)

        dqm, dkm, dv_cols = None, None, []
        for h in range(B_HEADS):
            vl_ = slice(h * B_VAL_DIM, (h + 1) * B_VAL_DIM)
            mine = lane_head == h
            qz, kz = jnp.where(mine, qm, 0.0).astype(_MX), jnp.where(mine, km, 0.0).astype(_MX)
            a = jnp.where(low, _dot_nt(qz, kz), 0.0).astype(_MX)
            da = jnp.where(low, _dot_nt(d_o_mx[:, vl_], vb_ref[:, vl_]), 0.0).astype(_MX)
            dqm_h, dkm_h = _dot(da, kz), _dot_tn(da, qz)
            dqm = dqm_h if dqm is None else dqm + dqm_h
            dkm = dkm_h if dkm is None else dkm + dkm_h
            dv_cols.append(_dot_tn(a, d_o_mx[:, vl_]))
        dv = jnp.concatenate(dv_cols, axis=1)

        chunk = [slice(c * B_CHUNK, (c + 1) * B_CHUNK) for c in range(cpb)]
        dqc_rows, g_loc = [], []
        for c in range(cpb):
            dqc_rows.append(_dot_nt(d_o_mx[chunk[c]], on_diagonal(sst_ref[c].astype(_MX))))
            g_loc.append(diagonal_of(_dot_tn(qc[chunk[c]], d_o_mx[chunk[c]])))
        cur = ds_ref[...]
        d_state = [None] * cpb
        for c in reversed(range(cpb)):
            d_state[c] = cur
            cur = g_loc[c] + jnp.exp(jnp.sum(la[chunk[c]].T, axis=1, keepdims=True)) * cur
        ds_ref[...] = cur
        dkl_rows, dv_rows, dlast_rows = [], [], []
        ones8 = jnp.ones((8, B_VAL_DIM), _F32)
        for c in range(cpb):
            dsd = on_diagonal(d_state[c].astype(_MX))
            dkl_c = _dot_nt(vb_ref[chunk[c], :], dsd)
            dkl_rows.append(dkl_c)
            dv_rows.append(_dot(kl[chunk[c]], dsd))
            prod = d_state[c] * sst_ref[c]
            p0 = prod.astype(jnp.bfloat16)
            p1 = (prod - p0.astype(_F32)).astype(jnp.bfloat16)
            p2 = (prod - p0.astype(_F32) - p1.astype(_F32)).astype(jnp.bfloat16)
            ddec = (_dot_nt(ones8, p0) + _dot_nt(ones8, p1) + _dot_nt(ones8, p2))[0:1]
            r_last = c * B_CHUNK + B_CHUNK - 1
            dlast = jnp.sum(dkl_c * kl[chunk[c]], axis=0, keepdims=True) + ddec * jnp.exp(last[r_last:r_last + 1])
            dlast_rows.append(jnp.broadcast_to(dlast, (B_CHUNK, B_KEY_WIDTH)))
        dqc, dkl = jnp.concatenate(dqc_rows, axis=0), jnp.concatenate(dkl_rows, axis=0)
        dqkb_ref[:, 0:B_KEY_WIDTH] = ((dqm * e_q + dqc * e_c) * (B_KEY_DIM ** -0.5)).astype(_ST)
        dqkb_ref[:, B_KEY_WIDTH:2 * B_KEY_WIDTH] = (dkm * e_k + dkl * e_l).astype(_ST)
        dvb_ref[...] = (dv + jnp.concatenate(dv_rows, axis=0)).astype(_ST)
        dcum = dqm * qm - dkm * km + dqc * qc - dkl * kl
        row = lax.broadcasted_iota(jnp.int32, (tb, B_KEY_WIDTH), 0)
        dcum = jnp.where(row % B_CHUNK == B_CHUNK - 1, dcum + jnp.concatenate(dlast_rows, axis=0), dcum)
        dla = _dot_ones((same & (ri <= ci)).astype(_F32), dcum)

        da_pre = dla * (1.0 / B_GATE_TEMP) * (1.0 - _sigmoid(a_pre))
        dalr_ref[...] = _dot_nt(da_pre, wup_ref[...]).astype(_ST)
        dwup_ref[...] = dwup_ref[...] + _dot_tn(alr_ref[...], da_pre)
        db_ref[...] = db_ref[...] + jnp.sum(da_pre, axis=0, keepdims=True)

    blk = lambda s, i: s * nblk + nblk - 1 - i
    rows = lambda w: pl.BlockSpec((tb, w), lambda s, i: (blk(s, i), 0))
    full = lambda a, b: pl.BlockSpec((a, b), lambda s, i: (0, 0))
    act = lambda w: jax.ShapeDtypeStruct((t, w), _ST)
    return pl.pallas_call(
        body, name="gla_bwd", grid=(nseq, nblk),
        in_specs=[rows(512), rows(512), rows(512), rows(LANE), rows(512), rows(512),
                  pl.BlockSpec((cpb, B_KEY_WIDTH, B_VAL_DIM), lambda s, i: (blk(s, i), 0, 0)),
                  full(LANE, B_KEY_WIDTH), full(1, B_KEY_WIDTH), full(1, B_WIDTH)],
        out_specs=[rows(512), rows(512), rows(512), rows(LANE), full(LANE, B_KEY_WIDTH),
                   full(1, B_KEY_WIDTH), full(1, B_WIDTH)],
        out_shape=[act(512), act(512), act(512), act(LANE),
                   jax.ShapeDtypeStruct((LANE, B_KEY_WIDTH), _F32),
                   jax.ShapeDtypeStruct((1, B_KEY_WIDTH), _F32), jax.ShapeDtypeStruct((1, B_WIDTH), _F32)],
        scratch_shapes=[pltpu.VMEM((B_KEY_WIDTH, B_VAL_DIM), _F32)],
        compiler_params=_cparams(("arbitrary", "arbitrary")),
    )(qkb, vb, zb, alr, oraw, dob, sst, wup, b_alpha, gn)


def _merge_loss(oa, ob, ga, gb, x2, tgt, wa, wb, wo, g_final):
    t = x2.shape[0]
    tm = min(t, 512)
    nt = t // tm

    def body(oa_ref, ob_ref, ga_ref, gb_ref, x_ref, t_ref, wa_ref, wb_ref, wo_ref, gf_ref,
             dh_ref, doa_ref, dob_ref, dga_ref, dgb_ref, dwa_ref, dwb_ref, dwo_ref, dgf_ref, loss_ref,
             ya_s, yb_s, out_s, dmer_s, mrg_s, dya_s, dyb_s):
        first = pl.program_id(0) == 0
        so_far = lambda ref: jnp.where(first, 0.0, ref[...])

        slabs = [slice(s, s + MERGE_SLAB) for s in range(0, tm, MERGE_SLAB)]
        fold = lambda a: a[0:8] + a[8:16]
        ya_s[...] = _dot(oa_ref[...], wa_ref[...])
        yb_s[...] = _dot(ob_ref[...], wb_ref[...])
        for rows_ in slabs:
            sga, sgb = _sigmoid(ga_ref[rows_, :].astype(_F32)), _sigmoid(gb_ref[rows_, :].astype(_F32))
            mrg_s[rows_, :] = (sga * ya_s[rows_, :] + sgb * yb_s[rows_, :]).astype(_MX)
        out_s[...] = x_ref[...] + _dot(mrg_s[...], wo_ref[...])
        gf = gf_ref[...]
        loss8 = jnp.zeros((8, D_MODEL), _F32)
        dgf8 = jnp.zeros((8, D_MODEL), _F32)
        for rows_ in slabs:
            out = out_s[rows_, :]
            r = lax.rsqrt(jnp.mean(out * out, axis=-1, keepdims=True) + NORM_EPS)
            nrm = out * r
            err = nrm * gf - t_ref[rows_, :]
            loss8 = loss8 + fold(err * err)
            dy = err * (1.0 / D_MODEL)
            dgf8 = dgf8 + fold(dy * nrm)
            dh = _rms_bwd(dy * gf, nrm, r)
            dh_ref[rows_, :] = dh.astype(_ST)
        loss_ref[...] = so_far(loss_ref) + (0.5 / D_MODEL) * jnp.sum(loss8, axis=0, keepdims=True)
        dgf_ref[...] = so_far(dgf_ref) + jnp.sum(dgf8, axis=0, keepdims=True)
        dmer_s[...] = _dot_nt(dh_ref[...], wo_ref[...])
        dwo_ref[...] = so_far(dwo_ref) + _dot_tn(mrg_s[...], dh_ref[...])
        for rows_ in slabs:
            sga, sgb = _sigmoid(ga_ref[rows_, :].astype(_F32)), _sigmoid(gb_ref[rows_, :].astype(_F32))
            dmer = dmer_s[rows_, :]
            da, db = dmer * sga, dmer * sgb
            dya_s[rows_, :] = da.astype(_MX)
            dyb_s[rows_, :] = db.astype(_MX)
            dga_ref[rows_, :] = (da * ya_s[rows_, :] * (1.0 - sga)).astype(_ST)
            dgb_ref[rows_, :] = (db * yb_s[rows_, :] * (1.0 - sgb)).astype(_ST)
        doa_ref[...] = _dot_nt(dya_s[...], wa_ref[...]).astype(_ST)
        dob_ref[...] = _dot_nt(dyb_s[...], wb_ref[...]).astype(_ST)
        dwa_ref[...] = so_far(dwa_ref) + _dot_tn(oa_ref[...], dya_s[...])
        dwb_ref[...] = so_far(dwb_ref) + _dot_tn(ob_ref[...], dyb_s[...])

    rows = lambda w: pl.BlockSpec((tm, w), lambda i: (i, 0))
    full = lambda a, b: pl.BlockSpec((a, b), lambda i: (0, 0), pipeline_mode=pl.Buffered(1))
    return pl.pallas_call(
        body, name="merge_loss", grid=(nt,),
        in_specs=[rows(512), rows(512), rows(D_MODEL), rows(D_MODEL), rows(D_MODEL), rows(D_MODEL),
                  full(A_WIDTH, D_MODEL), full(B_WIDTH, D_MODEL), full(D_MODEL, D_MODEL), full(1, D_MODEL)],
        out_specs=[rows(D_MODEL), rows(512), rows(512), rows(D_MODEL), rows(D_MODEL),
                   full(A_WIDTH, D_MODEL), full(B_WIDTH, D_MODEL), full(D_MODEL, D_MODEL),
                   full(1, D_MODEL), full(1, D_MODEL)],
        out_shape=[jax.ShapeDtypeStruct((t, D_MODEL), _ST), jax.ShapeDtypeStruct((t, 512), _ST),
                   jax.ShapeDtypeStruct((t, 512), _ST), jax.ShapeDtypeStruct((t, D_MODEL), _ST),
                   jax.ShapeDtypeStruct((t, D_MODEL), _ST),
                   jax.ShapeDtypeStruct((A_WIDTH, D_MODEL), _F32), jax.ShapeDtypeStruct((B_WIDTH, D_MODEL), _F32),
                   jax.ShapeDtypeStruct((D_MODEL, D_MODEL), _F32), jax.ShapeDtypeStruct((1, D_MODEL), _F32),
                   jax.ShapeDtypeStruct((1, D_MODEL), _F32)],
        scratch_shapes=[pltpu.VMEM((tm, D_MODEL), _F32)] * 4 + [pltpu.VMEM((tm, D_MODEL), _MX)] * 3,
        compiler_params=_cparams(("arbitrary",), VMEM_LIMIT),
    )(oa, ob, ga, gb, x2, tgt, wa, wb, wo, g_final)


def _in_proj_bwd_x(dpieces, wt, x2, dh2, g_in):
    t = x2.shape[0]
    tm = min(t, 512)
    np_ = len(PIECES)

    def body(*refs):
        dp_refs = refs[:np_]
        w_ref, x_ref, dh2_ref, g_ref, gx_ref, dg_ref = refs[np_:]

        @pl.when(pl.program_id(0) == 0)
        def _():
            dg_ref[...] = jnp.zeros_like(dg_ref)

        dh = None
        for (name, a, b), dp in zip(PIECES, dp_refs):
            part = _dot(dp[...], w_ref[a:b, :])
            dh = part if dh is None else dh + part
        xv = x_ref[...]
        r = lax.rsqrt(jnp.mean(xv * xv, axis=-1, keepdims=True) + NORM_EPS)
        nrm = xv * r
        dg_ref[...] = dg_ref[...] + jnp.sum(dh * nrm, axis=0, keepdims=True)
        gx_ref[...] = dh2_ref[...].astype(_F32) + _rms_bwd(dh * g_ref[...], nrm, r)

    rows = lambda w: pl.BlockSpec((tm, w), lambda i: (i, 0))
    full = lambda a, b: pl.BlockSpec((a, b), lambda i: (0, 0), pipeline_mode=pl.Buffered(1))
    return pl.pallas_call(
        body, name="in_proj_bwd_x", grid=(t // tm,),
        in_specs=[rows(b - a) for _, a, b in PIECES] + [full(D_IN, D_MODEL), rows(D_MODEL), rows(D_MODEL),
                                                          full(1, D_MODEL)],
        out_specs=[rows(D_MODEL), full(1, D_MODEL)],
        out_shape=[jax.ShapeDtypeStruct((t, D_MODEL), _F32), jax.ShapeDtypeStruct((1, D_MODEL), _F32)],
        compiler_params=_cparams(("arbitrary",), VMEM_LIMIT),
    )(*dpieces, wt, x2, dh2, g_in)


def _in_proj_bwd_w(h, dpieces, others, osplits):
    t = h.shape[0]
    tm = min(t, 1024)
    nt = t // tm
    np_, no = len(PIECES), len(others)
    half = D_MODEL // 2

    def body(*refs):
        h_ref, dp_refs, o_refs = refs[0], refs[1:1 + np_], refs[1 + np_:1 + np_ + no]
        mine_ref, theirs_ref = refs[1 + np_ + no:3 + np_ + no]
        r_refs = refs[3 + np_ + no:3 + np_ + 2 * no]
        acc_ref, keep_sem, send, recv, o_send, o_recv = refs[3 + np_ + 2 * no:]
        i = pl.program_id(0)
        x, y, c = _place()
        sibling = (x, y, 1 - c)
        early = [pltpu.make_async_remote_copy(
            src_ref=osplits[k].half(o_refs[k], 1 - c), dst_ref=r_refs[k], send_sem=o_send.at[k], recv_sem=o_recv.at[k],
            device_id=sibling, device_id_type=_MESH) for k in range(no)]

        @pl.when(i == 0)
        def _():
            for cp in early:
                cp.start()

        hv = h_ref[...]
        cols = lambda core: pl.ds(pl.multiple_of(core * half, LANE), half)
        writes = []
        for j, ((name, a, b), dp) in enumerate(zip(PIECES, dp_refs)):
            part = _dot_tn(dp[...], hv)
            if name == "alr":
                b = a + B_GATE_RANK
                part = part[0:B_GATE_RANK]
            acc_ref[a:b, :] = jnp.where(i == 0, 0.0, acc_ref[a:b, :]) + part
            keep = pltpu.make_async_copy(acc_ref.at[a:b, cols(c)], mine_ref.at[a:b], keep_sem.at[j])
            give = pltpu.make_async_remote_copy(
                src_ref=acc_ref.at[a:b, cols(1 - c)], dst_ref=theirs_ref.at[a:b], send_sem=send.at[j],
                recv_sem=recv.at[j], device_id=sibling, device_id_type=_MESH)
            writes += [keep, give]

            @pl.when(i == nt - 1)
            def _(keep=keep, give=give):
                keep.start()
                give.start()

        @pl.when(i == nt - 1)
        def _():
            for cp in writes + early:
                cp.wait()

    rows = lambda w: pl.BlockSpec((tm, w), lambda i: (i, 0))
    halves = [jax.ShapeDtypeStruct((D_IN, half), _F32)] * 2
    return pl.pallas_call(
        body, name="in_proj_bwd_w", grid=(nt,),
        in_specs=[rows(D_MODEL)] + [rows(b - a) for _, a, b in PIECES] + [_ANY] * no,
        out_specs=[_ANY] * (2 + no),
        out_shape=halves + [jax.ShapeDtypeStruct(sp.half_shape(g.shape), g.dtype) for g, sp in zip(others, osplits)],
        scratch_shapes=[pltpu.VMEM((D_IN, D_MODEL), _F32), pltpu.SemaphoreType.DMA((np_,)),
                        pltpu.SemaphoreType.DMA((np_,)), pltpu.SemaphoreType.DMA((np_,)),
                        pltpu.SemaphoreType.DMA((no,)), pltpu.SemaphoreType.DMA((no,))],
        compiler_params=_cparams(("arbitrary",), VMEM_LIMIT),
    )(h, *dpieces, *others)


def _place():
    return lax.axis_index("x"), lax.axis_index("y"), lax.axis_index("c")


def _other_chips(x, y):
    return [(1 - x, y), (x, 1 - y), (1 - x, 1 - y)]


class _Split(NamedTuple):
    by_rows: bool
    step: int
    size: int

    def half(self, ref, c):
        r, n = ref.shape[-2:]
        if self.by_rows:
            return ref.at[:, pl.ds(pl.multiple_of(c * (n // 2), LANE), n // 2)]
        return ref.at[pl.ds(pl.multiple_of(c * (r // 2), 16), r // 2), :]

    def chip_part(self, ref, k):
        if self.by_rows:
            return ref.at[pl.ds(pl.multiple_of(k * self.step, 16), self.size), :]
        return ref.at[:, pl.ds(pl.multiple_of(k * self.size, LANE), self.size)]

    def half_shape(self, shape):
        r, n = shape
        return (r, n // 2) if self.by_rows else (r // 2, n)

    def part_shape(self, shape):
        r, n = shape
        return (self.size, n) if self.by_rows else (r, self.size)


SPLIT_W_IN_T = _Split(True, WINDOW_STEP, WINDOW_ROWS)
SPLIT_W_O = _Split(True, 256, 256)
SPLIT_W_OUT = _Split(False, 256, 256)


def _gather_weights(shards, splits, fulls, pos_f):
    nw = len(shards)
    t = pos_f.shape[0]

    def body(*refs):
        ins, (pos_ref, c_ref) = refs[:nw], refs[nw:nw + 2]
        outs, tables = refs[nw + 2:2 * nw + 2], refs[2 * nw + 2:2 * nw + 5]
        send_a, recv_a, send_b, recv_b = refs[2 * nw + 5:]
        x, y, c = _place()
        me = 2 * x + y
        peers = _other_chips(x, y)

        def place(i, k, half):
            if splits[i] is None:
                return outs[i].at[k]
            if fulls[i][0] == 4 and len(fulls[i]) == 3:
                whole = outs[i].at[k]
            else:
                whole = splits[i].chip_part(outs[i], k)
            return splits[i].half(whole, half)

        first, passed = [], []
        for i in range(nw):
            src = ins[i] if splits[i] is None else splits[i].half(ins[i], c)
            for j, (px, py) in enumerate(peers):
                cp = pltpu.make_async_remote_copy(
                    src_ref=src, dst_ref=place(i, me, c), send_sem=send_a.at[3 * i + j],
                    recv_sem=recv_a.at[3 * i + j], device_id=(px, py, c), device_id_type=_MESH)
                cp.start()
                first.append(cp)
        _rope_tables_into(pos_ref, c_ref, *tables)
        for i in range(nw):
            for j, (px, py) in enumerate(peers):
                landed = place(i, 2 * px + py, c)
                pltpu.make_async_remote_copy(
                    src_ref=landed, dst_ref=landed, send_sem=send_a.at[3 * i + j], recv_sem=recv_a.at[3 * i + j],
                    device_id=(px, py, c), device_id_type=_MESH).wait_recv()
                if splits[i] is not None:
                    cp = pltpu.make_async_remote_copy(
                        src_ref=landed, dst_ref=landed, send_sem=send_b.at[3 * i + j], recv_sem=recv_b.at[3 * i + j],
                        device_id=(x, y, 1 - c), device_id_type=_MESH)
                    cp.start()
                    passed.append(cp)
        for i in range(nw):
            if splits[i] is None:
                continue
            for j, (px, py) in enumerate(peers):
                theirs = place(i, 2 * px + py, 1 - c)
                pltpu.make_async_remote_copy(
                    src_ref=theirs, dst_ref=theirs, send_sem=send_b.at[3 * i + j], recv_sem=recv_b.at[3 * i + j],
                    device_id=(x, y, 1 - c), device_id_type=_MESH).wait_recv()
        for cp in first + passed:
            cp.wait_send()

    vm = pl.BlockSpec(memory_space=pltpu.VMEM)
    tab = jax.ShapeDtypeStruct((t, LANE), _F32)
    return pl.pallas_call(
        body, name="gather_weights",
        in_specs=[_ANY] * nw + [vm, vm], out_specs=[_ANY] * nw + [vm] * 3,
        out_shape=[jax.ShapeDtypeStruct(f, s.dtype) for f, s in zip(fulls, shards)] + [tab] * 3,
        scratch_shapes=[pltpu.SemaphoreType.DMA((3 * nw,)) for _ in range(4)],
        compiler_params=_cparams(None, VMEM_LIMIT),
    )(*shards, pos_f, _rope_consts())


def _assemble_w_in_t(slots):
    bw = 256
    ov = WINDOW_ROWS - WINDOW_STEP

    def body(s_ref, o_ref):
        for k in range(4):
            base = k * WINDOW_STEP
            lo = 0 if k == 0 else ov
            if k > 0:
                o_ref[base:base + ov, :] = s_ref[k - 1, WINDOW_STEP:WINDOW_ROWS, :] + s_ref[k, 0:ov, :]
            hi = WINDOW_ROWS if k == 3 else WINDOW_STEP
            o_ref[base + lo:base + hi, :] = s_ref[k, lo:hi, :]

    return pl.pallas_call(
        body, name="assemble_w_in_t", grid=(D_MODEL // bw,),
        in_specs=[pl.BlockSpec((4, WINDOW_ROWS, bw), lambda i: (0, 0, i))],
        out_specs=pl.BlockSpec((D_IN, bw), lambda i: (0, i)),
        out_shape=jax.ShapeDtypeStruct((D_IN, D_MODEL), slots.dtype),
        compiler_params=_cparams(("parallel",)),
    )(slots)


def _row_block(rows):
    for cand in (976, 176, 256, 128):
        if rows % cand == 0:
            return cand
    return rows


def _pair_sum(g, r, split, c_arr, name):
    hr, hn = r.shape
    br = _row_block(hr)
    if split is None:
        g_spec = pl.BlockSpec((br, hn), lambda i, c_ref: (i, 0))
    elif split.by_rows:
        g_spec = pl.BlockSpec((br, hn), lambda i, c_ref: (i, c_ref[0]))
    else:
        g_spec = pl.BlockSpec((br, hn), lambda i, c_ref: (c_ref[0] * (hr // br) + i, 0))

    def body(c_ref, g_ref, r_ref, o_ref):
        o_ref[...] = (g_ref[...] + r_ref[...]).astype(o_ref.dtype)

    return pl.pallas_call(
        body, name=name,
        grid_spec=pltpu.PrefetchScalarGridSpec(
            num_scalar_prefetch=1, grid=(hr // br,),
            in_specs=[g_spec, pl.BlockSpec((br, hn), lambda i, c_ref: (i, 0))],
            out_specs=pl.BlockSpec((br, hn), lambda i, c_ref: (i, 0))),
        out_shape=jax.ShapeDtypeStruct(r.shape, _MX),
        compiler_params=_cparams(("parallel",)),
    )(c_arr, g, r)


_HBM = pl.BlockSpec(memory_space=pltpu.HBM)
_SEM = pl.BlockSpec(memory_space=pltpu.SEMAPHORE)
_FLOWS = pltpu.SideEffectType.DATAFLOW_SIDE_EFFECTING


def _chip_exchange_copies(refs, send, recv, splits):
    nw = len(refs) // 2
    x, y, c = _place()
    me = 2 * x + y
    copies = []
    for i in range(nw):
        for px, py in _other_chips(x, y):
            copies.append((splits[i].chip_part(refs[i], 2 * px + py), refs[nw + i].at[me], (px, py, c)))
    return [pltpu.make_async_remote_copy(src_ref=src, dst_ref=dst, send_sem=send.at[k], recv_sem=recv.at[k],
                                         device_id=peer, device_id_type=_MESH)
            for k, (src, dst, peer) in enumerate(copies)]


def _late_gather_copies(refs, send, recv, splits):
    nw = len(refs) // 2
    x, y, c = _place()
    me = 2 * x + y
    copies = []
    for i in range(nw):
        for px, py in _other_chips(x, y):
            copies.append((refs[i], splits[i].chip_part(refs[nw + i], me), (px, py, c)))
    return [pltpu.make_async_remote_copy(src_ref=src, dst_ref=dst, send_sem=send.at[k], recv_sem=recv.at[k],
                                         device_id=peer, device_id_type=_MESH)
            for k, (src, dst, peer) in enumerate(copies)]


def _start_copies(name, flying, copies_of, n_copies, after=None):
    first = [] if after is None else [after]

    def body(*refs):
        ins = refs[:len(flying)]
        send, recv = refs[len(flying) + len(first):len(flying) + len(first) + 2]
        token = refs[-1]
        for cp in copies_of(ins, send, recv):
            cp.start()
        token[...] = jnp.zeros_like(token)

    outs = pl.pallas_call(
        body, name=name,
        in_specs=[_HBM] * len(flying) + [_ANY] * len(first),
        out_specs=[_SEM, _SEM] + [_HBM] * len(flying) + [pl.BlockSpec(memory_space=pltpu.VMEM)],
        out_shape=[pltpu.SemaphoreType.DMA((n_copies,)), pltpu.SemaphoreType.DMA((n_copies,))]
        + [pltpu.HBM(f.shape, f.dtype) for f in flying] + [jax.ShapeDtypeStruct((8, LANE), _F32)],
        input_output_aliases={i: 2 + i for i in range(len(flying))},
        compiler_params=pltpu.CompilerParams(has_side_effects=_FLOWS),
    )(*[pltpu.with_memory_space_constraint(f, pltpu.HBM) for f in flying], *first)
    return outs[0], outs[1], outs[2:2 + len(flying)], outs[-1]


def _wait_copies(name, send, recv, flying, copies_of, after):
    def body(*refs):
        ins = refs[:len(flying)]
        send_ref, recv_ref = refs[len(flying):len(flying) + 2]
        for cp in copies_of(ins, send_ref, recv_ref):
            cp.wait_send()
            cp.wait_recv()

    return pl.pallas_call(
        body, name=name,
        in_specs=[_HBM] * len(flying) + [_SEM, _SEM, _ANY],
        out_specs=[_HBM] * len(flying),
        out_shape=[pltpu.HBM(f.shape, f.dtype) for f in flying],
        input_output_aliases={i: i for i in range(len(flying))},
        compiler_params=pltpu.CompilerParams(has_side_effects=_FLOWS),
    )(*flying, send, recv, after)


def _sum_chips(q, p, split, place_arr, name):
    _, hr, hn = q.shape
    if split.by_rows:
        out_shape = (hr, 2 * hn)
        o_spec = pl.BlockSpec((hr, hn), lambda i, pr: (0, pr[0]))
        p_spec = pl.BlockSpec((pl.Element(hr), pl.Element(hn)), lambda i, pr: (pr[1] * split.step, 0))
    else:
        out_shape = (2 * hr, hn)
        o_spec = pl.BlockSpec((hr, hn), lambda i, pr: (pr[0], 0))
        p_spec = pl.BlockSpec((hr, hn), lambda i, pr: (0, pr[1]))

    def body(pr, q_ref, p_ref, o_ref):
        f = lambda k: jnp.where(pr[1] == k, p_ref[...], q_ref[k]).astype(_F32)
        o_ref[...] = ((f(0) + f(1)) + f(2)) + f(3)

    return pl.pallas_call(
        body, name=name,
        grid_spec=pltpu.PrefetchScalarGridSpec(
            num_scalar_prefetch=1, grid=(1,),
            in_specs=[pl.BlockSpec((4, hr, hn), lambda i, pr: (0, 0, 0)), p_spec], out_specs=o_spec),
        out_shape=jax.ShapeDtypeStruct(out_shape, _F32),
        compiler_params=_cparams(("arbitrary",), VMEM_LIMIT),
    )(place_arr, q, p)


def _pair_share(bufs, splits, small):
    nw = len(bufs)

    def body(*refs):
        ins, small_ref, outs, all_ref = refs[:nw], refs[nw], refs[nw + 1:2 * nw + 1], refs[2 * nw + 1]
        send, recv, s_send, s_recv = refs[2 * nw + 2:]
        x, y, c = _place()
        copies = []
        for r in range(1, 8):
            peer = (1 - x if r & 4 else x, 1 - y if r & 2 else y, 1 - c if r & 1 else c)
            cp = pltpu.make_async_remote_copy(
                src_ref=small_ref, dst_ref=all_ref.at[4 * x + 2 * y + c], send_sem=s_send.at[r - 1],
                recv_sem=s_recv.at[r - 1], device_id=peer, device_id_type=_MESH)
            cp.start()
            copies.append(cp)
        for i in range(nw):
            cp = pltpu.make_async_remote_copy(
                src_ref=splits[i].half(ins[i], c), dst_ref=splits[i].half(outs[i], c), send_sem=send.at[i],
                recv_sem=recv.at[i], device_id=(x, y, 1 - c), device_id_type=_MESH)
            cp.start()
            copies.append(cp)
        for cp in copies:
            cp.wait()

    return pl.pallas_call(
        body, name="grad_pair_share",
        in_specs=[_ANY] * (nw + 1), out_specs=[_ANY] * (nw + 1),
        out_shape=[jax.ShapeDtypeStruct(b.shape, b.dtype) for b in bufs]
        + [jax.ShapeDtypeStruct((8,) + small.shape, small.dtype)],
        input_output_aliases={i: i for i in range(nw)},
        scratch_shapes=[pltpu.SemaphoreType.DMA((nw,)), pltpu.SemaphoreType.DMA((nw,)),
                        pltpu.SemaphoreType.DMA((7,)), pltpu.SemaphoreType.DMA((7,))],
    )(*bufs, small)


def _sum_devices(parts, own, dev_arr):
    def body(dev, p_ref, own_ref, tot_ref):
        f = lambda d: jnp.where(dev[0] == d, own_ref[...], p_ref[d])
        acc = f(0)
        for d in range(1, 8):
            acc = acc + f(d)
        tot_ref[...] = acc

    return pl.pallas_call(
        body, name="small_sum",
        grid_spec=pltpu.PrefetchScalarGridSpec(
            num_scalar_prefetch=1, grid=(1,),
            in_specs=[pl.BlockSpec(parts.shape, lambda i, dev: (0, 0, 0)), pl.BlockSpec(own.shape, lambda i, dev: (0, 0))],
            out_specs=pl.BlockSpec(own.shape, lambda i, dev: (0, 0))),
        out_shape=jax.ShapeDtypeStruct(own.shape, own.dtype),
    )(dev_arr, parts, own)


def _adam_update(w, g, m, v):
    m2 = ADAM_B1 * m + (1.0 - ADAM_B1) * g
    v2 = ADAM_B2 * v + (1.0 - ADAM_B2) * (g * g)
    m_hat = m2 / (1.0 - ADAM_B1 ** ADAM_STEP)
    v_hat = v2 / (1.0 - ADAM_B2 ** ADAM_STEP)
    return -ADAM_LR * (m_hat / (jnp.sqrt(v_hat) + ADAM_EPS) + ADAM_WD * w), m2, v2


SMALL_AT = dict(g_in=(0, 0), g_final=(1, 0), g_gla_norm=(2, 0), b_alpha=(2, B_WIDTH), attn_sinks=(2, B_WIDTH + B_KEY_WIDTH))
LOSS_AT = (2, B_WIDTH + B_KEY_WIDTH + LANE)
WUP_ROWS = (3, 7)


def _adamw_small(tot, g_wup, params):
    names = list(params)

    def body(*refs):
        tot_ref, gw_ref = refs[0], refs[1]
        ins = refs[2:2 + 3 * len(names)]
        outs = refs[2 + 3 * len(names):]
        for i, nm in enumerate(names):
            w_ref, m_ref, v_ref = ins[3 * i:3 * i + 3]
            if nm in SMALL_AT:
                r, a = SMALL_AT[nm]
                g = tot_ref[r:r + 1, a:a + w_ref.shape[1]]
            else:
                g = gw_ref[...]
            d, m2, v2 = _adam_update(w_ref[...], g, m_ref[...], v_ref[...])
            for o_ref, val in zip(outs[4 * i:4 * i + 4], (g, d, m2, v2)):
                o_ref[...] = val

    vm = pl.BlockSpec(memory_space=pltpu.VMEM)
    flat = [a for nm in names for a in params[nm]]
    out_shape = [jax.ShapeDtypeStruct(params[nm][0].shape, _F32) for nm in names for _ in range(4)]
    outs = pl.pallas_call(
        body, name="adamw_small", in_specs=[vm] * (2 + len(flat)), out_specs=[vm] * len(out_shape), out_shape=out_shape,
    )(tot, g_wup, *flat)
    return {nm: tuple(outs[4 * i:4 * i + 4]) for i, nm in enumerate(names)}


def _adamw(w, g, m, v, name):
    lead = w.shape[0] != 1
    r, n = (w.shape[0], w.shape[2]) if lead else w.shape[1:]
    br = r
    for cand in (256, 244, 128):
        if r > cand and r % cand == 0:
            br = cand
            break

    def body(w_ref, g_ref, m_ref, v_ref, d_ref, nm_ref, nv_ref):
        d_ref[...], nm_ref[...], nv_ref[...] = _adam_update(w_ref[...], g_ref[...], m_ref[...], v_ref[...])

    blk = pl.BlockSpec((br, 1, n), lambda i: (i, 0, 0)) if lead else pl.BlockSpec((None, br, n), lambda i: (0, i, 0))
    shp = jax.ShapeDtypeStruct(w.shape, _F32)
    return pl.pallas_call(
        body, name=name, grid=(r // br,),
        in_specs=[blk] * 4, out_specs=[blk] * 3, out_shape=[shp] * 3,
        compiler_params=_cparams(("parallel",)),
    )(w, g, m, v)


def kernel(x, positions, g_in, w_in, w_alpha_up, b_alpha, attn_sinks, g_gla_norm, w_out_a, w_out_b, w_o, g_final, loss_target, m_g_in, m_w_in, m_w_alpha_up, m_b_alpha, m_attn_sinks, m_g_gla_norm, m_w_out_a, m_w_out_b, m_w_o, m_g_final, v_g_in, v_w_in, v_w_alpha_up, v_b_alpha, v_attn_sinks, v_g_gla_norm, v_w_out_a, v_w_out_b, v_w_o, v_g_final):
    nseq, seq, _ = x.shape
    t = nseq * seq
    cx, cy, cc = _place()
    chip = 2 * cx + cy
    c_arr = jnp.reshape(cc, (1,)).astype(jnp.int32)

    tr = lambda w: jnp.transpose(w, (2, 0, 1))
    w_in_t = tr(w_in).reshape(SHARD, D_MODEL).astype(_MX)
    pad = WINDOW_ROWS - SHARD
    window = lax.switch(chip, [lambda w, k=k: jnp.pad(w, ((4 * k, pad - 4 * k), (0, 0))) for k in range(4)], w_in_t)
    shards = [window, w_alpha_up[0].astype(_MX)]
    late = [w_out_a[0].astype(_MX), w_out_b[0].astype(_MX), w_o[0].astype(_MX)]
    late_splits = [SPLIT_W_OUT, SPLIT_W_OUT, SPLIT_W_O]
    splits = [SPLIT_W_IN_T, None]
    fulls = [(4, WINDOW_ROWS, D_MODEL), (4, B_GATE_RANK, B_KEY_WIDTH // 4)]
    pos_f = positions.astype(_F32).reshape(t, 1)
    win_g, wup_g, cos, sa, sb = _gather_weights(shards, splits, fulls, pos_f)
    late_copies = lambda refs, send, recv: _late_gather_copies(refs, send, recv, late_splits)
    late_full = [lax.empty(shape, _MX) for shape in ((A_WIDTH, D_MODEL), (B_WIDTH, D_MODEL), (D_MODEL, D_MODEL))]
    l_send, l_recv, l_flying, l_token = _start_copies("late_gather_start", late + late_full, late_copies,
                                                      3 * len(late), after=win_g)
    nsh = D_MODEL // 4
    win_g = lax.dynamic_update_slice(win_g, window[None], (chip, 0, 0))
    wup_g = lax.dynamic_update_slice(wup_g, shards[1][None], (chip, 0, 0))
    wt = _assemble_w_in_t(win_g)
    wup = jnp.concatenate([jnp.transpose(wup_g, (1, 0, 2)).reshape(B_GATE_RANK, B_KEY_WIDTH),
                           jnp.zeros((LANE - B_GATE_RANK, B_KEY_WIDTH), _MX)], axis=0)

    x2 = x.reshape(t, D_MODEL)
    tgt = loss_target.reshape(t, D_MODEL)
    sinks = attn_sinks.reshape(A_HEADS)
    gf = g_final.reshape(1, D_MODEL)

    h, qkv, za, qkb, vb, zb, alr, ga, gb = _in_proj(x2, g_in + l_token[0, 0], wt, cos, sa, sb)
    oa = _attn_fwd(qkv, za, sinks, nseq)
    ob, oraw, sst = _gla_fwd(qkb, vb, zb, alr, wup, b_alpha, g_gla_norm, nseq)

    mine_a, mine_b, mine_o, wa, wb, wo = _wait_copies("late_gather_wait", l_send, l_recv, l_flying, late_copies, ob)
    wa = lax.dynamic_update_slice(wa, mine_a, (0, nsh * chip))
    wb = lax.dynamic_update_slice(wb, mine_b, (0, nsh * chip))
    wo = lax.dynamic_update_slice(wo, mine_o, (nsh * chip, 0))
    dh2, doa, dob, dga, dgb, dwa, dwb, dwo, dgf, lossv = _merge_loss(oa, ob, ga, gb, x2, tgt, wa, wb, wo, gf)

    dqkv, dza, dsink = _attn_bwd(qkv, za, doa, sinks, cos, sa, sb, nseq)
    dqkb, dvb, dzb, dalr, dwup, dba, dgn = _gla_bwd(qkb, vb, zb, alr, oraw, dob, sst, wup, b_alpha, g_gla_norm, nseq)
    dpieces = [dqkv, dza, dqkb, dvb, dzb, dalr, dga, dgb]
    gsplits = [SPLIT_W_IN_T, SPLIT_W_OUT, SPLIT_W_OUT, SPLIT_W_O]
    names = ("w_in", "w_out_a", "w_out_b", "w_o")
    dwin_mine, *from_sibling = _in_proj_bwd_w(h, dpieces, [dwa, dwb, dwo], gsplits[1:])
    pair_sums = [_pair_sum(g, r, sp, c_arr, "pair_sum_" + nm)
                 for g, r, sp, nm in zip([dwin_mine, dwa, dwb, dwo], from_sibling, [None] + gsplits[1:], names)]
    exchange = lambda refs, send, recv: _chip_exchange_copies(refs, send, recv, gsplits)
    lands = [lax.empty((4,) + sp.part_shape(p.shape), p.dtype) for p, sp in zip(pair_sums, gsplits)]
    send, recv, flying, token = _start_copies("grad_chip_exchange_start", pair_sums + lands, exchange, 3 * len(lands))
    grad_x2, dgin = _in_proj_bwd_x(dpieces, wt, x2, dh2, g_in + token[0, 0])
    landed = _wait_copies("grad_chip_exchange_wait", send, recv, flying, exchange, grad_x2)
    place_arr = jnp.stack([cc, chip]).astype(jnp.int32)
    reduced = [_sum_chips(q, p, sp, place_arr, "chip_sum_" + nm)
               for q, p, sp, nm in zip(landed[len(lands):], landed[:len(lands)], gsplits, names)]
    row2 = jnp.concatenate([dgn, dba, jnp.pad(dsink[:, 0].reshape(1, A_HEADS), ((0, 0), (0, LANE - A_HEADS))),
                            jnp.pad(jnp.sum(lossv, axis=1, keepdims=True), ((0, 0), (0, LANE - 1)))], axis=1)
    small = jnp.concatenate([dgin, dgf, row2, dwup[:B_GATE_RANK].reshape(WUP_ROWS[1] - WUP_ROWS[0], D_MODEL),
                             jnp.zeros((1, D_MODEL), _F32)], axis=0)
    g_window, g_wa, g_wb, g_wo, small_parts = _pair_share(reduced, gsplits, small)
    g_win_t = lax.switch(chip, [lambda w, k=k: w[4 * k:4 * k + SHARD].reshape(SHARD, 1, D_MODEL) for k in range(4)],
                         g_window)
    dev_arr = jnp.reshape(2 * chip + cc, (1,)).astype(jnp.int32)
    tot = _sum_devices(small_parts, small, dev_arr)
    loss = tot[LOSS_AT]
    nup = B_KEY_WIDTH // 4
    g_wup = lax.dynamic_slice(tot[WUP_ROWS[0]:WUP_ROWS[1]].reshape(B_GATE_RANK, B_KEY_WIDTH), (0, chip * nup),
                              (B_GATE_RANK, nup))

    row = lambda a: a.reshape(1, -1)
    sm = _adamw_small(tot, g_wup, dict(
        g_in=(g_in, m_g_in, v_g_in), g_final=(row(g_final), row(m_g_final), row(v_g_final)),
        g_gla_norm=(g_gla_norm, m_g_gla_norm, v_g_gla_norm), b_alpha=(b_alpha, m_b_alpha, v_b_alpha),
        attn_sinks=(attn_sinks, m_attn_sinks, v_attn_sinks),
        w_alpha_up=(w_alpha_up[0], m_w_alpha_up[0], v_w_alpha_up[0])))
    sm["g_final"] = tuple(a.reshape(D_MODEL) for a in sm["g_final"])
    sm["w_alpha_up"] = tuple(a[None] for a in sm["w_alpha_up"])

    untr = lambda a: jnp.transpose(a, (1, 2, 0))
    big = dict(w_in=tuple(untr(a) for a in (g_win_t,) + tuple(_adamw(tr(w_in), g_win_t, tr(m_w_in), tr(v_w_in), "adamw_w_in"))))
    for nm, w, g, m, v in (("w_out_a", w_out_a, g_wa, m_w_out_a, v_w_out_a),
                           ("w_out_b", w_out_b, g_wb, m_w_out_b, v_w_out_b), ("w_o", w_o, g_wo, m_w_o, v_w_o)):
        big[nm] = (g[None],) + tuple(_adamw(w, g[None], m, v, "adamw_" + nm))

    order = ("g_in", "w_in", "w_alpha_up", "b_alpha", "attn_sinks", "g_gla_norm", "w_out_a", "w_out_b", "w_o", "g_final")
    outs = [big[nm][kind] if nm in big else sm[nm][kind] for kind in range(4) for nm in order]
    return (loss, grad_x2.reshape(x.shape), *outs)
```

```python
import math
from typing import NamedTuple

import numpy as np
import jax
import jax.numpy as jnp
from jax import lax
from jax.experimental import pallas as pl
from jax.experimental.pallas import tpu as pltpu

D_MODEL = 1024
A_HEADS, A_KV_HEADS, A_HEAD_DIM = 8, 2, 64
A_GROUP = A_HEADS // A_KV_HEADS
A_WIDTH, A_KV_WIDTH = 512, 128
BLOCK = 128
ROPE_THETA = 500000.0
ROPE_DIM = 16
B_HEADS, B_KEY_DIM, B_VAL_DIM = 4, 64, 128
B_KEY_WIDTH, B_WIDTH = 256, 512
B_GATE_RANK = 16
B_GATE_TEMP = 16.0
B_CHUNK = 64
NORM_EPS = 1e-6
NEG_BIG = -1e30
D_IN = 4880

ADAM_LR, ADAM_B1, ADAM_B2, ADAM_EPS, ADAM_WD, ADAM_STEP = 0.001, 0.9, 0.999, 1e-08, 0.01, 10

LANE = 128
ALR_AT = 2816
PIECES = (("qkv", 0, 768), ("za", 768, 1280), ("qkb", 1280, 1792), ("vb", 1792, 2304),
          ("zb", 2304, 2816), ("alr", ALR_AT, ALR_AT + LANE), ("ga", 2832, 3856), ("gb", 3856, 4880))
SHARD = D_IN // 4
WINDOW_STEP = 1216
WINDOW_ROWS = 1232

GLA_BLOCK = 256
MERGE_SLAB = 16
VMEM_LIMIT = 56 * 1024 * 1024

_F32 = jnp.float32
_MX = jnp.bfloat16
_ST = jnp.bfloat16

_MESH = pl.DeviceIdType.MESH
_ANY = pl.BlockSpec(memory_space=pl.ANY)


def _cparams(sem=None, vmem=None):
    return pltpu.CompilerParams(dimension_semantics=sem, vmem_limit_bytes=vmem)


def _dot(a, b):
    return jnp.dot(a.astype(_MX), b.astype(_MX), preferred_element_type=_F32)


def _dot_nt(a, b):
    return lax.dot_general(a.astype(_MX), b.astype(_MX), (((1,), (1,)), ((), ())),
                           preferred_element_type=_F32)


def _dot_tn(a, b):
    return lax.dot_general(a.astype(_MX), b.astype(_MX), (((0,), (0,)), ((), ())),
                           preferred_element_type=_F32)


def _dot_ones(ones_mat, v):
    o = ones_mat.astype(jnp.bfloat16)
    v0 = v.astype(jnp.bfloat16)
    r1 = v - v0.astype(_F32)
    v1 = r1.astype(jnp.bfloat16)
    v2 = (r1 - v1.astype(_F32)).astype(jnp.bfloat16)
    d = lambda t: jnp.dot(o, t, preferred_element_type=_F32)
    return d(v0) + d(v1) + d(v2)


def _sigmoid(x):
    return 0.5 * jnp.tanh(0.5 * x) + 0.5


def _log_sigmoid(x):
    return jnp.minimum(x, 0.0) - jnp.log(1.0 + jnp.exp(-jnp.abs(x)))


def _lane_tile(t, width):
    reps = width // t.shape[1]
    return t if reps == 1 else jnp.tile(t, (1, reps))


def _rope(t, cos, sa, sb, sign):
    w = t.shape[1]
    rot = pltpu.roll(t, w - 8, 1) * _lane_tile(sa, w) + pltpu.roll(t, 8, 1) * _lane_tile(sb, w)
    return t * _lane_tile(cos, w) + sign * rot


def _rms_bwd(dy_g, n, r):
    return r * (dy_g - n * jnp.mean(dy_g * n, axis=-1, keepdims=True))


ROPE_ROWS = 256


def _rope_consts():
    lane = np.arange(LANE) % A_HEAD_DIM
    half = ROPE_DIM // 2
    inv = np.exp((np.float32(-math.log(ROPE_THETA)) * np.arange(half, dtype=np.float32)) * np.float32(2.0 / ROPE_DIM))
    consts = np.zeros((8, LANE), np.float32)
    consts[0] = np.where(lane < ROPE_DIM, inv[lane % half], 0.0)
    consts[1] = np.where(lane < half, -1.0, 0.0)
    consts[2] = np.where((lane >= half) & (lane < ROPE_DIM), 1.0, 0.0)
    return jnp.asarray(consts)


def _rope_tables_into(pos_ref, c_ref, cos_ref, sa_ref, sb_ref):
    def rows_of(b, carry):
        rows = pl.ds(pl.multiple_of(b * ROPE_ROWS, ROPE_ROWS), ROPE_ROWS)
        ang = pos_ref[rows, :] * c_ref[0:1, :]
        s = jnp.sin(ang)
        cos_ref[rows, :] = jnp.cos(ang)
        sa_ref[rows, :] = s * c_ref[1:2, :]
        sb_ref[rows, :] = s * c_ref[2:3, :]
        return carry

    lax.fori_loop(0, pos_ref.shape[0] // ROPE_ROWS, rows_of, 0)


def _in_proj(x2, g_in, wt, cos, sa, sb):
    t = x2.shape[0]
    tm = min(t, 512)

    def body(x_ref, g_ref, w_ref, cos_ref, sa_ref, sb_ref, h_ref, qkv_ref, za_ref, qkb_ref,
             vb_ref, zb_ref, alr_ref, ga_ref, gb_ref):
        xv = x_ref[...]
        r = lax.rsqrt(jnp.mean(xv * xv, axis=-1, keepdims=True) + NORM_EPS)
        h = (xv * r * g_ref[...]).astype(_MX)
        h_ref[...] = h.astype(_ST)
        outs = dict(za=za_ref, qkb=qkb_ref, vb=vb_ref, zb=zb_ref, alr=alr_ref, ga=ga_ref, gb=gb_ref)
        for name, a, b in PIECES:
            p = _dot_nt(h, w_ref[a:b, :])
            if name == "qkv":
                c, s1, s2 = cos_ref[...], sa_ref[...], sb_ref[...]
                qkv_ref[:, 0:512] = _rope(p[:, 0:512], c, s1, s2, 1.0).astype(_ST)
                qkv_ref[:, 512:640] = _rope(p[:, 512:640], c, s1, s2, 1.0).astype(_ST)
                qkv_ref[:, 640:768] = p[:, 640:768].astype(_ST)
            else:
                outs[name][...] = p.astype(outs[name].dtype)

    rows = lambda w: pl.BlockSpec((tm, w), lambda i: (i, 0))
    shp = lambda name, w: jax.ShapeDtypeStruct((t, w), _F32 if name == "qkb" else _ST)
    widths = [D_MODEL] + [b - a for _, a, b in PIECES]
    return pl.pallas_call(
        body, name="in_proj", grid=(t // tm,),
        in_specs=[rows(D_MODEL), pl.BlockSpec((1, D_MODEL), lambda i: (0, 0)),
                  pl.BlockSpec((D_IN, D_MODEL), lambda i: (0, 0), pipeline_mode=pl.Buffered(1)),
                  rows(LANE), rows(LANE), rows(LANE)],
        out_specs=[rows(w) for w in widths],
        out_shape=[shp(n, w) for n, w in zip(["h"] + [p[0] for p in PIECES], widths)],
        compiler_params=_cparams(("parallel",), VMEM_LIMIT),
    )(x2, g_in, wt, cos, sa, sb)


def _attn_operands(k_prev, k_cur, v_prev, v_cur, want_bwd):
    kf = jnp.concatenate([k_prev, k_cur], axis=0).astype(_F32) * (A_HEAD_DIM ** -0.5)
    vf = jnp.concatenate([v_prev, v_cur], axis=0).astype(_F32)
    lo = lax.broadcasted_iota(jnp.int32, (1, LANE), 1) < 64

    def on_lanes(a):
        sw = pltpu.roll(a, 64, 1)
        z = jnp.zeros_like(a)
        return [[jnp.where(lo, a, z).astype(_MX), jnp.where(lo, z, sw).astype(_MX)],
                [jnp.where(lo, sw, z).astype(_MX), jnp.where(lo, z, a).astype(_MX)]]

    def on_rows(a):
        at = a.T.astype(_MX)
        z = jnp.zeros((64, at.shape[1]), _MX)
        top, bot = at[0:64], at[64:128]
        return [[jnp.concatenate([top, z], axis=0), jnp.concatenate([z, top], axis=0)],
                [jnp.concatenate([bot, z], axis=0), jnp.concatenate([z, bot], axis=0)]]

    ops = dict(k_lanes=on_lanes(kf), v_rows=on_rows(vf), lo=lo)
    if want_bwd:
        ops.update(v_lanes=on_lanes(vf), k_rows=on_rows(kf))
    return ops


def _attn_valid(n):
    kj = lax.broadcasted_iota(jnp.int32, (2 * BLOCK, 2 * BLOCK), 0) - BLOCK
    qi = lax.broadcasted_iota(jnp.int32, (2 * BLOCK, 2 * BLOCK), 1) & (BLOCK - 1)
    return (kj <= qi) & (qi - kj < BLOCK) & ((n > 0) | (kj >= 0))


def _attn_sinks(sink_ref, h_a, h_b):
    first = lax.broadcasted_iota(jnp.int32, (1, 2 * BLOCK), 1) < BLOCK
    return jnp.where(first, sink_ref[h_a], sink_ref[h_b])


def _attn_softmax_t(k_lanes, q_pair, valid, sink):
    s = jnp.where(valid, _dot_nt(k_lanes, q_pair), NEG_BIG)
    m = jnp.maximum(jnp.max(s, axis=0, keepdims=True), sink)
    e = jnp.exp(s - m)
    e_sink = jnp.exp(sink - m)
    inv = 1.0 / (jnp.sum(e, axis=0, keepdims=True) + e_sink)
    return e, e_sink, inv


ATTN_TILE = 8


def _attn_kv(qkv_ref, kvp_ref, j):
    rows = slice(j * BLOCK, (j + 1) * BLOCK)
    if j == 0:
        k_prev, v_prev = kvp_ref[:, 0:128], kvp_ref[:, 128:256]
    else:
        before = slice((j - 1) * BLOCK, j * BLOCK)
        k_prev, v_prev = qkv_ref[before, 512:640], qkv_ref[before, 640:768]
    return k_prev, qkv_ref[rows, 512:640], v_prev, qkv_ref[rows, 640:768]


def _attn_fwd(qkv, za, sinks, nseq):
    t = qkv.shape[0]
    tile = ATTN_TILE * BLOCK
    nt = t // nseq // tile

    def body(sink_ref, qkv_ref, kvp_ref, za_ref, oa_ref):
        for j in range(ATTN_TILE):
            rows = slice(j * BLOCK, (j + 1) * BLOCK)
            ops = _attn_operands(*_attn_kv(qkv_ref, kvp_ref, j), False)
            valid = _attn_valid(ATTN_TILE * pl.program_id(1) + j)[:, 0:BLOCK]
            for pr in range(A_HEADS // 2):
                lanes = slice(pr * LANE, (pr + 1) * LANE)
                g = pr // (A_GROUP // 2)
                q_pair = qkv_ref[rows, lanes]
                ot = None
                for half in range(2):
                    e, _, inv = _attn_softmax_t(ops["k_lanes"][g][half], q_pair, valid, sink_ref[2 * pr + half])
                    part = _dot(ops["v_rows"][g][half], e) * inv
                    ot = part if ot is None else ot + part
                z = za_ref[rows, lanes].astype(_F32)
                oa_ref[rows, lanes] = (ot.T * (z * _sigmoid(z))).astype(_ST)

    cur = lambda w: pl.BlockSpec((tile, w), lambda s, n: (s * nt + n, 0))
    return pl.pallas_call(
        body, name="attn_fwd", grid=(nseq, nt),
        in_specs=[pl.BlockSpec(memory_space=pltpu.SMEM), cur(768),
                  pl.BlockSpec((BLOCK, 256), lambda s, n: (ATTN_TILE * (s * nt + n) - jnp.minimum(n, 1), 2)),
                  cur(512)],
        out_specs=cur(512), out_shape=jax.ShapeDtypeStruct((t, A_WIDTH), _ST),
        compiler_params=_cparams(("parallel", "arbitrary")),
    )(sinks, qkv, qkv, za)


def _attn_bwd(qkv, za, doa, sinks, cos, sa, sb, nseq):
    t = qkv.shape[0]
    tile = ATTN_TILE * BLOCK
    nt = t // nseq // tile

    def body(sink_ref, qkv_ref, kvp_ref, za_ref, doa_ref, cos_ref, sa_ref, sb_ref,
             dqkv_ref, dza_ref, dsink_ref, ck_ref, cv_ref):
        s_id, i = pl.program_id(0), pl.program_id(1)

        @pl.when((s_id == 0) & (i == 0))
        def _():
            dsink_ref[...] = jnp.zeros_like(dsink_ref)

        @pl.when(i == 0)
        def _():
            ck_ref[...] = jnp.zeros_like(ck_ref)
            cv_ref[...] = jnp.zeros_like(cv_ref)

        carry_k, carry_v = ck_ref[...], cv_ref[...]
        for j in reversed(range(ATTN_TILE)):
            rows = slice(j * BLOCK, (j + 1) * BLOCK)
            ops = _attn_operands(*_attn_kv(qkv_ref, kvp_ref, j), True)
            lo = ops["lo"]
            valid = _attn_valid(ATTN_TILE * (nt - 1 - i) + j)
            dk_acc, dv_acc, dq_pairs = [], [], []
            for g in range(A_KV_HEADS):
                pairs = [slice((2 * g + p) * LANE, (2 * g + p + 1) * LANE) for p in range(2)]
                q_both = jnp.concatenate([qkv_ref[rows, p] for p in pairs], axis=0)
                q_f = q_both.astype(_F32)
                z = [za_ref[rows, p].astype(_F32) for p in pairs]
                sz = [_sigmoid(v) for v in z]
                d_oa = [doa_ref[rows, p].astype(_F32) for p in pairs]
                d_att = jnp.concatenate([d_oa[p] * (z[p] * sz[p]) for p in range(2)], axis=0)
                zero = jnp.zeros_like(d_att)
                ot, dqt, ds_all, pn_all, qz_all, daz_all = None, None, [], [], [], []
                for half in range(2):
                    heads = (4 * g + half, 4 * g + 2 + half)
                    e, e_sink, inv = _attn_softmax_t(ops["k_lanes"][g][half], q_both, valid,
                                                     _attn_sinks(sink_ref, *heads))
                    pn = e * inv
                    dpt = _dot_nt(ops["v_lanes"][g][half], d_att)
                    delta = jnp.sum(pn * dpt, axis=0, keepdims=True)
                    ds = (pn * (dpt - delta)).astype(_MX)
                    pn = pn.astype(_MX)
                    d_sink = e_sink * inv * delta
                    for p, h in enumerate(heads):
                        dsink_ref[h:h + 1, :] = dsink_ref[h:h + 1, :] - jnp.sum(d_sink[:, p * BLOCK:(p + 1) * BLOCK])
                    o_part = _dot(ops["v_rows"][g][half], pn)
                    dq_part = _dot(ops["k_rows"][g][half], ds)
                    ot = o_part if ot is None else ot + o_part
                    dqt = dq_part if dqt is None else dqt + dq_part
                    mine = lo if half == 0 else jnp.logical_not(lo)
                    ds_all.append(ds)
                    pn_all.append(pn)
                    qz_all.append(jnp.where(mine, q_f, zero).astype(_MX))
                    daz_all.append(jnp.where(mine, d_att, zero).astype(_MX))
                dk_acc.append(_dot(jnp.concatenate(ds_all, axis=1), jnp.concatenate(qz_all, axis=0)))
                dv_acc.append(_dot(jnp.concatenate(pn_all, axis=1), jnp.concatenate(daz_all, axis=0)))
                for p, lanes in enumerate(pairs):
                    cols = slice(p * BLOCK, (p + 1) * BLOCK)
                    dza_ref[rows, lanes] = (d_oa[p] * ot[:, cols].T * (sz[p] * (1.0 + z[p] * (1.0 - sz[p])))).astype(_ST)
                    dq_pairs.append(dqt[:, cols].T)

            def fold(acc, scale):
                both = [a + pltpu.roll(a, 64, 1) for a in acc]
                return jnp.where(lo, both[0], both[1]) * scale

            dk_full = fold(dk_acc, A_HEAD_DIM ** -0.5)
            dv_full = fold(dv_acc, 1.0)
            dk_cur, dv_cur = dk_full[BLOCK:] + carry_k, dv_full[BLOCK:] + carry_v
            carry_k, carry_v = dk_full[:BLOCK], dv_full[:BLOCK]
            c, s1, s2 = cos_ref[rows, :], sa_ref[rows, :], sb_ref[rows, :]
            dqkv_ref[rows, 0:512] = _rope(jnp.concatenate(dq_pairs, axis=1), c, s1, s2, -1.0).astype(_ST)
            dqkv_ref[rows, 512:640] = _rope(dk_cur, c, s1, s2, -1.0).astype(_ST)
            dqkv_ref[rows, 640:768] = dv_cur.astype(_ST)
        ck_ref[...] = carry_k
        cv_ref[...] = carry_v

    cur = lambda w: pl.BlockSpec((tile, w), lambda s, i: (s * nt + nt - 1 - i, 0))
    return pl.pallas_call(
        body, name="attn_bwd", grid=(nseq, nt),
        in_specs=[pl.BlockSpec(memory_space=pltpu.SMEM), cur(768),
                  pl.BlockSpec((BLOCK, 256),
                               lambda s, i: (ATTN_TILE * (s * nt + nt - 1 - i) - jnp.minimum(nt - 1 - i, 1), 2)),
                  cur(512), cur(512), cur(LANE), cur(LANE), cur(LANE)],
        out_specs=[cur(768), cur(512), pl.BlockSpec((8, LANE), lambda s, i: (0, 0))],
        out_shape=[jax.ShapeDtypeStruct((t, 768), _ST), jax.ShapeDtypeStruct((t, 512), _ST),
                   jax.ShapeDtypeStruct((8, LANE), _F32)],
        scratch_shapes=[pltpu.VMEM((BLOCK, A_KV_WIDTH), _F32), pltpu.VMEM((BLOCK, A_KV_WIDTH), _F32)],
        compiler_params=_cparams(("arbitrary", "arbitrary")),
    )(sinks, qkv, qkv, za, doa, cos, sa, sb)


def _gla_chunk_terms(la, qkb_ref, r0):
    g = la[r0:r0 + B_CHUNK, :]
    ri = lax.broadcasted_iota(jnp.int32, (B_CHUNK, B_CHUNK), 0)
    ci = lax.broadcasted_iota(jnp.int32, (B_CHUNK, B_CHUNK), 1)
    cum = _dot_ones((ri >= ci).astype(_F32), g)
    last = cum[B_CHUNK - 1:B_CHUNK, :]
    mid = cum[B_CHUNK // 2 - 1:B_CHUNK // 2, :]
    q = qkb_ref[r0:r0 + B_CHUNK, 0:B_KEY_WIDTH].astype(_F32) * (B_KEY_DIM ** -0.5)
    k = qkb_ref[r0:r0 + B_CHUNK, B_KEY_WIDTH:2 * B_KEY_WIDTH].astype(_F32)
    e_q, e_k, e_l, e_c = jnp.exp(cum - mid), jnp.exp(mid - cum), jnp.exp(last - cum), jnp.exp(cum)
    dec_col = jnp.exp(jnp.sum(g.T, axis=1, keepdims=True))
    return dict(qm=q * e_q, km=k * e_k, kl=k * e_l, qc=q * e_c, e_q=e_q, e_k=e_k, e_l=e_l, e_c=e_c,
                dec_col=dec_col, dec_row=jnp.exp(last), causal=ri >= ci, ri=ri)


def _gate_logits(alr_ref, wup_ref, b_ref):
    return _dot(alr_ref[...], wup_ref[...]) + b_ref[...]


def _gla_fwd(qkb, vb, zb, alr, wup, b_alpha, gn, nseq):
    t = qkb.shape[0]
    tb = min(GLA_BLOCK, t // nseq)
    nblk = t // nseq // tb
    cpb = tb // B_CHUNK

    def body(qkb_ref, vb_ref, zb_ref, alr_ref, wup_ref, b_ref, gn_ref, ob_ref, oraw_ref, sst_ref, s_ref):
        @pl.when(pl.program_id(1) == 0)
        def _():
            s_ref[...] = jnp.zeros_like(s_ref)

        la = _log_sigmoid(_gate_logits(alr_ref, wup_ref, b_ref)) * (1.0 / B_GATE_TEMP)
        terms = [_gla_chunk_terms(la, qkb_ref, c * B_CHUNK) for c in range(cpb)]
        o_intra, inc = {}, {}
        for c, tm in enumerate(terms):
            for h in range(B_HEADS):
                kl_, vl_ = slice(h * 64, (h + 1) * 64), slice(h * 128, (h + 1) * 128)
                v = vb_ref[c * B_CHUNK:(c + 1) * B_CHUNK, vl_]
                a = jnp.where(tm["causal"], _dot_nt(tm["qm"][:, kl_], tm["km"][:, kl_]), 0.0)
                o_intra[c, h] = _dot(a, v)
                inc[c, h] = _dot_tn(tm["kl"][:, kl_], v)
        o_heads = {}
        for h in range(B_HEADS):
            kl_ = slice(h * 64, (h + 1) * 64)
            st = s_ref[kl_, :]
            for c, tm in enumerate(terms):
                sst_ref[c, kl_, :] = st
                o_heads[c, h] = o_intra[c, h] + _dot(tm["qc"][:, kl_], st)
                st = tm["dec_col"][kl_, :] * st + inc[c, h]
            s_ref[kl_, :] = st
        o = jnp.concatenate([jnp.concatenate([o_heads[c, h] for h in range(B_HEADS)], axis=1)
                             for c in range(cpb)], axis=0)
        oraw_ref[...] = o
        z = zb_ref[...].astype(_F32)
        gate = z * _sigmoid(z)
        for h in range(B_HEADS):
            vl_ = slice(h * 128, (h + 1) * 128)
            oh = o[:, vl_]
            r = lax.rsqrt(jnp.mean(oh * oh, axis=-1, keepdims=True) + NORM_EPS)
            ob_ref[:, vl_] = ((oh * r) * gn_ref[:, vl_] * gate[:, vl_]).astype(_ST)

    rows = lambda w: pl.BlockSpec((tb, w), lambda s, i: (s * nblk + i, 0))
    full = lambda a, b: pl.BlockSpec((a, b), lambda s, i: (0, 0))
    return pl.pallas_call(
        body, name="gla_fwd", grid=(nseq, nblk),
        in_specs=[rows(512), rows(512), rows(512), rows(LANE), full(LANE, B_KEY_WIDTH),
                  full(1, B_KEY_WIDTH), full(1, B_WIDTH)],
        out_specs=[rows(512), rows(512),
                   pl.BlockSpec((cpb, B_KEY_WIDTH, B_VAL_DIM), lambda s, i: (s * nblk + i, 0, 0))],
        out_shape=[jax.ShapeDtypeStruct((t, B_WIDTH), _ST), jax.ShapeDtypeStruct((t, B_WIDTH), _F32),
                   jax.ShapeDtypeStruct((t // B_CHUNK, B_KEY_WIDTH, B_VAL_DIM), _F32)],
        scratch_shapes=[pltpu.VMEM((B_KEY_WIDTH, B_VAL_DIM), _F32)],
        compiler_params=_cparams(("parallel", "arbitrary")),
    )(qkb, vb, zb, alr, wup, b_alpha, gn)


def _gla_bwd(qkb, vb, zb, alr, oraw, dob, sst, wup, b_alpha, gn, nseq):
    t = qkb.shape[0]
    tb = min(GLA_BLOCK, t // nseq)
    nblk = t // nseq // tb
    cpb = tb // B_CHUNK

    def body(qkb_ref, vb_ref, zb_ref, alr_ref, oraw_ref, dob_ref, sst_ref, wup_ref, b_ref, gn_ref,
             dqkb_ref, dvb_ref, dzb_ref, dalr_ref, dwup_ref, db_ref, dgn_ref, ds_ref):
        s_id, i = pl.program_id(0), pl.program_id(1)

        @pl.when((s_id == 0) & (i == 0))
        def _():
            dwup_ref[...] = jnp.zeros_like(dwup_ref)
            db_ref[...] = jnp.zeros_like(db_ref)
            dgn_ref[...] = jnp.zeros_like(dgn_ref)

        @pl.when(i == 0)
        def _():
            ds_ref[...] = jnp.zeros_like(ds_ref)

        a_pre = _gate_logits(alr_ref, wup_ref, b_ref)
        la = _log_sigmoid(a_pre) * (1.0 / B_GATE_TEMP)

        z = zb_ref[...].astype(_F32)
        sz = _sigmoid(z)
        d_ob = dob_ref[...].astype(_F32)
        tg = d_ob * (z * sz)
        dsilu = sz * (1.0 + z * (1.0 - sz))
        do_cols, dgn_cols = [], []
        for h in range(B_HEADS):
            vl_ = slice(h * 128, (h + 1) * 128)
            oh = oraw_ref[:, vl_].astype(_F32)
            r = lax.rsqrt(jnp.mean(oh * oh, axis=-1, keepdims=True) + NORM_EPS)
            on = oh * r
            gnh = gn_ref[:, vl_]
            dzb_ref[:, vl_] = (d_ob[:, vl_] * (on * gnh) * dsilu[:, vl_]).astype(_ST)
            dgn_cols.append(jnp.sum(tg[:, vl_] * on, axis=0, keepdims=True))
            do_cols.append(_rms_bwd(tg[:, vl_] * gnh, on, r))
        dgn_ref[...] = dgn_ref[...] + jnp.concatenate(dgn_cols, axis=1)
        d_o = jnp.concatenate(do_cols, axis=1)

        ri = lax.broadcasted_iota(jnp.int32, (tb, tb), 0)
        ci = lax.broadcasted_iota(jnp.int32, (tb, tb), 1)
        same = (ri // B_CHUNK) == (ci // B_CHUNK)
        low = same & (ri >= ci)
        upto_mid = same & ((ci % B_CHUNK) < B_CHUNK // 2)
        sums = _dot_ones(jnp.concatenate([m.astype(_F32) for m in (low, same, upto_mid)], axis=0), la)
        cum, last, mid = sums[0:tb], sums[tb:2 * tb], sums[2 * tb:3 * tb]
        e_q, e_k, e_l, e_c = jnp.exp(cum - mid), jnp.exp(mid - cum), jnp.exp(last - cum), jnp.exp(cum)
        q = qkb_ref[:, 0:B_KEY_WIDTH] * (B_KEY_DIM ** -0.5)
        k = qkb_ref[:, B_KEY_WIDTH:2 * B_KEY_WIDTH]
        qm, km, kl, qc = q * e_q, k * e_k, k * e_l, q * e_c
        lane_head = lax.broadcasted_iota(jnp.int32, (1, B_KEY_WIDTH), 1) // B_KEY_DIM
        d_o_mx = d_o.astype(_MX)

        def on_diagonal(st):
            z = jnp.zeros((B_KEY_DIM, B_VAL_DIM), st.dtype)
            return jnp.concatenate([jnp.concatenate(
                [st[h * B_KEY_DIM:(h + 1) * B_KEY_DIM] if g == h else z for g in range(B_HEADS)], axis=1)
                for h in range(B_HEADS)], axis=0)

        def diagonal_of(full):
            return jnp.concatenate([full[h * B_KEY_DIM:(h + 1) * B_KEY_DIM, h * B_VAL_DIM:(h + 1) * B_VAL_DIM]
                                    for h in range(B_HEADS)], axis=0)

        dqm, dkm, dv_cols = None, None, []
        for h in range(B_HEADS):
            vl_ = slice(h * B_VAL_DIM, (h + 1) * B_VAL_DIM)
            mine = lane_head == h
            qz, kz = jnp.where(mine, qm, 0.0).astype(_MX), jnp.where(mine, km, 0.0).astype(_MX)
            a = jnp.where(low, _dot_nt(qz, kz), 0.0).astype(_MX)
            da = jnp.where(low, _dot_nt(d_o_mx[:, vl_], vb_ref[:, vl_]), 0.0).astype(_MX)
            dqm_h, dkm_h = _dot(da, kz), _dot_tn(da, qz)
            dqm = dqm_h if dqm is None else dqm + dqm_h
            dkm = dkm_h if dkm is None else dkm + dkm_h
            dv_cols.append(_dot_tn(a, d_o_mx[:, vl_]))
        dv = jnp.concatenate(dv_cols, axis=1)

        chunk = [slice(c * B_CHUNK, (c + 1) * B_CHUNK) for c in range(cpb)]
        dqc_rows, g_loc = [], []
        for c in range(cpb):
            dqc_rows.append(_dot_nt(d_o_mx[chunk[c]], on_diagonal(sst_ref[c].astype(_MX))))
            g_loc.append(diagonal_of(_dot_tn(qc[chunk[c]], d_o_mx[chunk[c]])))
        cur = ds_ref[...]
        d_state = [None] * cpb
        for c in reversed(range(cpb)):
            d_state[c] = cur
            cur = g_loc[c] + jnp.exp(jnp.sum(la[chunk[c]].T, axis=1, keepdims=True)) * cur
        ds_ref[...] = cur
        dkl_rows, dv_rows, dlast_rows = [], [], []
        ones8 = jnp.ones((8, B_VAL_DIM), _F32)
        for c in range(cpb):
            dsd = on_diagonal(d_state[c].astype(_MX))
            dkl_c = _dot_nt(vb_ref[chunk[c], :], dsd)
            dkl_rows.append(dkl_c)
            dv_rows.append(_dot(kl[chunk[c]], dsd))
            prod = d_state[c] * sst_ref[c]
            p0 = prod.astype(jnp.bfloat16)
            p1 = (prod - p0.astype(_F32)).astype(jnp.bfloat16)
            p2 = (prod - p0.astype(_F32) - p1.astype(_F32)).astype(jnp.bfloat16)
            ddec = (_dot_nt(ones8, p0) + _dot_nt(ones8, p1) + _dot_nt(ones8, p2))[0:1]
            r_last = c * B_CHUNK + B_CHUNK - 1
            dlast = jnp.sum(dkl_c * kl[chunk[c]], axis=0, keepdims=True) + ddec * jnp.exp(last[r_last:r_last + 1])
            dlast_rows.append(jnp.broadcast_to(dlast, (B_CHUNK, B_KEY_WIDTH)))
        dqc, dkl = jnp.concatenate(dqc_rows, axis=0), jnp.concatenate(dkl_rows, axis=0)
        dqkb_ref[:, 0:B_KEY_WIDTH] = ((dqm * e_q + dqc * e_c) * (B_KEY_DIM ** -0.5)).astype(_ST)
        dqkb_ref[:, B_KEY_WIDTH:2 * B_KEY_WIDTH] = (dkm * e_k + dkl * e_l).astype(_ST)
        dvb_ref[...] = (dv + jnp.concatenate(dv_rows, axis=0)).astype(_ST)
        dcum = dqm * qm - dkm * km + dqc * qc - dkl * kl
        row = lax.broadcasted_iota(jnp.int32, (tb, B_KEY_WIDTH), 0)
        dcum = jnp.where(row % B_CHUNK == B_CHUNK - 1, dcum + jnp.concatenate(dlast_rows, axis=0), dcum)
        dla = _dot_ones((same & (ri <= ci)).astype(_F32), dcum)

        da_pre = dla * (1.0 / B_GATE_TEMP) * (1.0 - _sigmoid(a_pre))
        dalr_ref[...] = _dot_nt(da_pre, wup_ref[...]).astype(_ST)
        dwup_ref[...] = dwup_ref[...] + _dot_tn(alr_ref[...], da_pre)
        db_ref[...] = db_ref[...] + jnp.sum(da_pre, axis=0, keepdims=True)

    blk = lambda s, i: s * nblk + nblk - 1 - i
    rows = lambda w: pl.BlockSpec((tb, w), lambda s, i: (blk(s, i), 0))
    full = lambda a, b: pl.BlockSpec((a, b), lambda s, i: (0, 0))
    act = lambda w: jax.ShapeDtypeStruct((t, w), _ST)
    return pl.pallas_call(
        body, name="gla_bwd", grid=(nseq, nblk),
        in_specs=[rows(512), rows(512), rows(512), rows(LANE), rows(512), rows(512),
                  pl.BlockSpec((cpb, B_KEY_WIDTH, B_VAL_DIM), lambda s, i: (blk(s, i), 0, 0)),
                  full(LANE, B_KEY_WIDTH), full(1, B_KEY_WIDTH), full(1, B_WIDTH)],
        out_specs=[rows(512), rows(512), rows(512), rows(LANE), full(LANE, B_KEY_WIDTH),
                   full(1, B_KEY_WIDTH), full(1, B_WIDTH)],
        out_shape=[act(512), act(512), act(512), act(LANE),
                   jax.ShapeDtypeStruct((LANE, B_KEY_WIDTH), _F32),
                   jax.ShapeDtypeStruct((1, B_KEY_WIDTH), _F32), jax.ShapeDtypeStruct((1, B_WIDTH), _F32)],
        scratch_shapes=[pltpu.VMEM((B_KEY_WIDTH, B_VAL_DIM), _F32)],
        compiler_params=_cparams(("arbitrary", "arbitrary")),
    )(qkb, vb, zb, alr, oraw, dob, sst, wup, b_alpha, gn)


def _merge_loss(oa, ob, ga, gb, x2, tgt, wa, wb, wo, g_final):
    t = x2.shape[0]
    tm = min(t, 512)
    nt = t // tm

    def body(oa_ref, ob_ref, ga_ref, gb_ref, x_ref, t_ref, wa_ref, wb_ref, wo_ref, gf_ref,
             dh_ref, doa_ref, dob_ref, dga_ref, dgb_ref, dwa_ref, dwb_ref, dwo_ref, dgf_ref, loss_ref,
             ya_s, yb_s, out_s, dmer_s, mrg_s, dya_s, dyb_s):
        first = pl.program_id(0) == 0
        so_far = lambda ref: jnp.where(first, 0.0, ref[...])

        slabs = [slice(s, s + MERGE_SLAB) for s in range(0, tm, MERGE_SLAB)]
        fold = lambda a: a[0:8] + a[8:16]
        ya_s[...] = _dot(oa_ref[...], wa_ref[...])
        yb_s[...] = _dot(ob_ref[...], wb_ref[...])
        for rows_ in slabs:
            sga, sgb = _sigmoid(ga_ref[rows_, :].astype(_F32)), _sigmoid(gb_ref[rows_, :].astype(_F32))
            mrg_s[rows_, :] = (sga * ya_s[rows_, :] + sgb * yb_s[rows_, :]).astype(_MX)
        out_s[...] = x_ref[...] + _dot(mrg_s[...], wo_ref[...])
        gf = gf_ref[...]
        loss8 = jnp.zeros((8, D_MODEL), _F32)
        dgf8 = jnp.zeros((8, D_MODEL), _F32)
        for rows_ in slabs:
            out = out_s[rows_, :]
            r = lax.rsqrt(jnp.mean(out * out, axis=-1, keepdims=True) + NORM_EPS)
            nrm = out * r
            err = nrm * gf - t_ref[rows_, :]
            loss8 = loss8 + fold(err * err)
            dy = err * (1.0 / D_MODEL)
            dgf8 = dgf8 + fold(dy * nrm)
            dh = _rms_bwd(dy * gf, nrm, r)
            dh_ref[rows_, :] = dh.astype(_ST)
        loss_ref[...] = so_far(loss_ref) + (0.5 / D_MODEL) * jnp.sum(loss8, axis=0, keepdims=True)
        dgf_ref[...] = so_far(dgf_ref) + jnp.sum(dgf8, axis=0, keepdims=True)
        dmer_s[...] = _dot_nt(dh_ref[...], wo_ref[...])
        dwo_ref[...] = so_far(dwo_ref) + _dot_tn(mrg_s[...], dh_ref[...])
        for rows_ in slabs:
            sga, sgb = _sigmoid(ga_ref[rows_, :].astype(_F32)), _sigmoid(gb_ref[rows_, :].astype(_F32))
            dmer = dmer_s[rows_, :]
            da, db = dmer * sga, dmer * sgb
            dya_s[rows_, :] = da.astype(_MX)
            dyb_s[rows_, :] = db.astype(_MX)
            dga_ref[rows_, :] = (da * ya_s[rows_, :] * (1.0 - sga)).astype(_ST)
            dgb_ref[rows_, :] = (db * yb_s[rows_, :] * (1.0 - sgb)).astype(_ST)
        doa_ref[...] = _dot_nt(dya_s[...], wa_ref[...]).astype(_ST)
        dob_ref[...] = _dot_nt(dyb_s[...], wb_ref[...]).astype(_ST)
        dwa_ref[...] = so_far(dwa_ref) + _dot_tn(oa_ref[...], dya_s[...])
        dwb_ref[...] = so_far(dwb_ref) + _dot_tn(ob_ref[...], dyb_s[...])

    rows = lambda w: pl.BlockSpec((tm, w), lambda i: (i, 0))
    full = lambda a, b: pl.BlockSpec((a, b), lambda i: (0, 0), pipeline_mode=pl.Buffered(1))
    return pl.pallas_call(
        body, name="merge_loss", grid=(nt,),
        in_specs=[rows(512), rows(512), rows(D_MODEL), rows(D_MODEL), rows(D_MODEL), rows(D_MODEL),
                  full(A_WIDTH, D_MODEL), full(B_WIDTH, D_MODEL), full(D_MODEL, D_MODEL), full(1, D_MODEL)],
        out_specs=[rows(D_MODEL), rows(512), rows(512), rows(D_MODEL), rows(D_MODEL),
                   full(A_WIDTH, D_MODEL), full(B_WIDTH, D_MODEL), full(D_MODEL, D_MODEL),
                   full(1, D_MODEL), full(1, D_MODEL)],
        out_shape=[jax.ShapeDtypeStruct((t, D_MODEL), _ST), jax.ShapeDtypeStruct((t, 512), _ST),
                   jax.ShapeDtypeStruct((t, 512), _ST), jax.ShapeDtypeStruct((t, D_MODEL), _ST),
                   jax.ShapeDtypeStruct((t, D_MODEL), _ST),
                   jax.ShapeDtypeStruct((A_WIDTH, D_MODEL), _F32), jax.ShapeDtypeStruct((B_WIDTH, D_MODEL), _F32),
                   jax.ShapeDtypeStruct((D_MODEL, D_MODEL), _F32), jax.ShapeDtypeStruct((1, D_MODEL), _F32),
                   jax.ShapeDtypeStruct((1, D_MODEL), _F32)],
        scratch_shapes=[pltpu.VMEM((tm, D_MODEL), _F32)] * 4 + [pltpu.VMEM((tm, D_MODEL), _MX)] * 3,
        compiler_params=_cparams(("arbitrary",), VMEM_LIMIT),
    )(oa, ob, ga, gb, x2, tgt, wa, wb, wo, g_final)


def _in_proj_bwd_x(dpieces, wt, x2, dh2, g_in):
    t = x2.shape[0]
    tm = min(t, 512)
    np_ = len(PIECES)

    def body(*refs):
        dp_refs = refs[:np_]
        w_ref, x_ref, dh2_ref, g_ref, gx_ref, dg_ref = refs[np_:]

        @pl.when(pl.program_id(0) == 0)
        def _():
            dg_ref[...] = jnp.zeros_like(dg_ref)

        dh = None
        for (name, a, b), dp in zip(PIECES, dp_refs):
            part = _dot(dp[...], w_ref[a:b, :])
            dh = part if dh is None else dh + part
        xv = x_ref[...]
        r = lax.rsqrt(jnp.mean(xv * xv, axis=-1, keepdims=True) + NORM_EPS)
        nrm = xv * r
        dg_ref[...] = dg_ref[...] + jnp.sum(dh * nrm, axis=0, keepdims=True)
        gx_ref[...] = dh2_ref[...].astype(_F32) + _rms_bwd(dh * g_ref[...], nrm, r)

    rows = lambda w: pl.BlockSpec((tm, w), lambda i: (i, 0))
    full = lambda a, b: pl.BlockSpec((a, b), lambda i: (0, 0), pipeline_mode=pl.Buffered(1))
    return pl.pallas_call(
        body, name="in_proj_bwd_x", grid=(t // tm,),
        in_specs=[rows(b - a) for _, a, b in PIECES] + [full(D_IN, D_MODEL), rows(D_MODEL), rows(D_MODEL),
                                                          full(1, D_MODEL)],
        out_specs=[rows(D_MODEL), full(1, D_MODEL)],
        out_shape=[jax.ShapeDtypeStruct((t, D_MODEL), _F32), jax.ShapeDtypeStruct((1, D_MODEL), _F32)],
        compiler_params=_cparams(("arbitrary",), VMEM_LIMIT),
    )(*dpieces, wt, x2, dh2, g_in)


def _in_proj_bwd_w(h, dpieces, others, osplits):
    t = h.shape[0]
    tm = min(t, 1024)
    nt = t // tm
    np_, no = len(PIECES), len(others)
    half = D_MODEL // 2

    def body(*refs):
        h_ref, dp_refs, o_refs = refs[0], refs[1:1 + np_], refs[1 + np_:1 + np_ + no]
        mine_ref, theirs_ref = refs[1 + np_ + no:3 + np_ + no]
        r_refs = refs[3 + np_ + no:3 + np_ + 2 * no]
        acc_ref, keep_sem, send, recv, o_send, o_recv = refs[3 + np_ + 2 * no:]
        i = pl.program_id(0)
        x, y, c = _place()
        sibling = (x, y, 1 - c)
        early = [pltpu.make_async_remote_copy(
            src_ref=osplits[k].half(o_refs[k], 1 - c), dst_ref=r_refs[k], send_sem=o_send.at[k], recv_sem=o_recv.at[k],
            device_id=sibling, device_id_type=_MESH) for k in range(no)]

        @pl.when(i == 0)
        def _():
            for cp in early:
                cp.start()

        hv = h_ref[...]
        cols = lambda core: pl.ds(pl.multiple_of(core * half, LANE), half)
        writes = []
        for j, ((name, a, b), dp) in enumerate(zip(PIECES, dp_refs)):
            part = _dot_tn(dp[...], hv)
            if name == "alr":
                b = a + B_GATE_RANK
                part = part[0:B_GATE_RANK]
            acc_ref[a:b, :] = jnp.where(i == 0, 0.0, acc_ref[a:b, :]) + part
            keep = pltpu.make_async_copy(acc_ref.at[a:b, cols(c)], mine_ref.at[a:b], keep_sem.at[j])
            give = pltpu.make_async_remote_copy(
                src_ref=acc_ref.at[a:b, cols(1 - c)], dst_ref=theirs_ref.at[a:b], send_sem=send.at[j],
                recv_sem=recv.at[j], device_id=sibling, device_id_type=_MESH)
            writes += [keep, give]

            @pl.when(i == nt - 1)
            def _(keep=keep, give=give):
                keep.start()
                give.start()

        @pl.when(i == nt - 1)
        def _():
            for cp in writes + early:
                cp.wait()

    rows = lambda w: pl.BlockSpec((tm, w), lambda i: (i, 0))
    halves = [jax.ShapeDtypeStruct((D_IN, half), _F32)] * 2
    return pl.pallas_call(
        body, name="in_proj_bwd_w", grid=(nt,),
        in_specs=[rows(D_MODEL)] + [rows(b - a) for _, a, b in PIECES] + [_ANY] * no,
        out_specs=[_ANY] * (2 + no),
        out_shape=halves + [jax.ShapeDtypeStruct(sp.half_shape(g.shape), g.dtype) for g, sp in zip(others, osplits)],
        scratch_shapes=[pltpu.VMEM((D_IN, D_MODEL), _F32), pltpu.SemaphoreType.DMA((np_,)),
                        pltpu.SemaphoreType.DMA((np_,)), pltpu.SemaphoreType.DMA((np_,)),
                        pltpu.SemaphoreType.DMA((no,)), pltpu.SemaphoreType.DMA((no,))],
        compiler_params=_cparams(("arbitrary",), VMEM_LIMIT),
    )(h, *dpieces, *others)


def _place():
    return lax.axis_index("x"), lax.axis_index("y"), lax.axis_index("c")


def _other_chips(x, y):
    return [(1 - x, y), (x, 1 - y), (1 - x, 1 - y)]


class _Split(NamedTuple):
    by_rows: bool
    step: int
    size: int

    def half(self, ref, c):
        r, n = ref.shape[-2:]
        if self.by_rows:
            return ref.at[:, pl.ds(pl.multiple_of(c * (n // 2), LANE), n // 2)]
        return ref.at[pl.ds(pl.multiple_of(c * (r // 2), 16), r // 2), :]

    def chip_part(self, ref, k):
        if self.by_rows:
            return ref.at[pl.ds(pl.multiple_of(k * self.step, 16), self.size), :]
        return ref.at[:, pl.ds(pl.multiple_of(k * self.size, LANE), self.size)]

    def half_shape(self, shape):
        r, n = shape
        return (r, n // 2) if self.by_rows else (r // 2, n)

    def part_shape(self, shape):
        r, n = shape
        return (self.size, n) if self.by_rows else (r, self.size)


SPLIT_W_IN_T = _Split(True, WINDOW_STEP, WINDOW_ROWS)
SPLIT_W_O = _Split(True, 256, 256)
SPLIT_W_OUT = _Split(False, 256, 256)


def _gather_weights(shards, splits, fulls, pos_f):
    nw = len(shards)
    t = pos_f.shape[0]

    def body(*refs):
        ins, (pos_ref, c_ref) = refs[:nw], refs[nw:nw + 2]
        outs, tables = refs[nw + 2:2 * nw + 2], refs[2 * nw + 2:2 * nw + 5]
        send_a, recv_a, send_b, recv_b = refs[2 * nw + 5:]
        x, y, c = _place()
        me = 2 * x + y
        peers = _other_chips(x, y)

        def place(i, k, half):
            if splits[i] is None:
                return outs[i].at[k]
            if fulls[i][0] == 4 and len(fulls[i]) == 3:
                whole = outs[i].at[k]
            else:
                whole = splits[i].chip_part(outs[i], k)
            return splits[i].half(whole, half)

        first, passed = [], []
        for i in range(nw):
            src = ins[i] if splits[i] is None else splits[i].half(ins[i], c)
            for j, (px, py) in enumerate(peers):
                cp = pltpu.make_async_remote_copy(
                    src_ref=src, dst_ref=place(i, me, c), send_sem=send_a.at[3 * i + j],
                    recv_sem=recv_a.at[3 * i + j], device_id=(px, py, c), device_id_type=_MESH)
                cp.start()
                first.append(cp)
        _rope_tables_into(pos_ref, c_ref, *tables)
        for i in range(nw):
            for j, (px, py) in enumerate(peers):
                landed = place(i, 2 * px + py, c)
                pltpu.make_async_remote_copy(
                    src_ref=landed, dst_ref=landed, send_sem=send_a.at[3 * i + j], recv_sem=recv_a.at[3 * i + j],
                    device_id=(px, py, c), device_id_type=_MESH).wait_recv()
                if splits[i] is not None:
                    cp = pltpu.make_async_remote_copy(
                        src_ref=landed, dst_ref=landed, send_sem=send_b.at[3 * i + j], recv_sem=recv_b.at[3 * i + j],
                        device_id=(x, y, 1 - c), device_id_type=_MESH)
                    cp.start()
                    passed.append(cp)
        for i in range(nw):
            if splits[i] is None:
                continue
            for j, (px, py) in enumerate(peers):
                theirs = place(i, 2 * px + py, 1 - c)
                pltpu.make_async_remote_copy(
                    src_ref=theirs, dst_ref=theirs, send_sem=send_b.at[3 * i + j], recv_sem=recv_b.at[3 * i + j],
                    device_id=(x, y, 1 - c), device_id_type=_MESH).wait_recv()
        for cp in first + passed:
            cp.wait_send()

    vm = pl.BlockSpec(memory_space=pltpu.VMEM)
    tab = jax.ShapeDtypeStruct((t, LANE), _F32)
    return pl.pallas_call(
        body, name="gather_weights",
        in_specs=[_ANY] * nw + [vm, vm], out_specs=[_ANY] * nw + [vm] * 3,
        out_shape=[jax.ShapeDtypeStruct(f, s.dtype) for f, s in zip(fulls, shards)] + [tab] * 3,
        scratch_shapes=[pltpu.SemaphoreType.DMA((3 * nw,)) for _ in range(4)],
        compiler_params=_cparams(None, VMEM_LIMIT),
    )(*shards, pos_f, _rope_consts())


def _assemble_w_in_t(slots):
    bw = 256
    ov = WINDOW_ROWS - WINDOW_STEP

    def body(s_ref, o_ref):
        for k in range(4):
            base = k * WINDOW_STEP
            lo = 0 if k == 0 else ov
            if k > 0:
                o_ref[base:base + ov, :] = s_ref[k - 1, WINDOW_STEP:WINDOW_ROWS, :] + s_ref[k, 0:ov, :]
            hi = WINDOW_ROWS if k == 3 else WINDOW_STEP
            o_ref[base + lo:base + hi, :] = s_ref[k, lo:hi, :]

    return pl.pallas_call(
        body, name="assemble_w_in_t", grid=(D_MODEL // bw,),
        in_specs=[pl.BlockSpec((4, WINDOW_ROWS, bw), lambda i: (0, 0, i))],
        out_specs=pl.BlockSpec((D_IN, bw), lambda i: (0, i)),
        out_shape=jax.ShapeDtypeStruct((D_IN, D_MODEL), slots.dtype),
        compiler_params=_cparams(("parallel",)),
    )(slots)


def _row_block(rows):
    for cand in (976, 176, 256, 128):
        if rows % cand == 0:
            return cand
    return rows


def _pair_sum(g, r, split, c_arr, name):
    hr, hn = r.shape
    br = _row_block(hr)
    if split is None:
        g_spec = pl.BlockSpec((br, hn), lambda i, c_ref: (i, 0))
    elif split.by_rows:
        g_spec = pl.BlockSpec((br, hn), lambda i, c_ref: (i, c_ref[0]))
    else:
        g_spec = pl.BlockSpec((br, hn), lambda i, c_ref: (c_ref[0] * (hr // br) + i, 0))

    def body(c_ref, g_ref, r_ref, o_ref):
        o_ref[...] = (g_ref[...] + r_ref[...]).astype(o_ref.dtype)

    return pl.pallas_call(
        body, name=name,
        grid_spec=pltpu.PrefetchScalarGridSpec(
            num_scalar_prefetch=1, grid=(hr // br,),
            in_specs=[g_spec, pl.BlockSpec((br, hn), lambda i, c_ref: (i, 0))],
            out_specs=pl.BlockSpec((br, hn), lambda i, c_ref: (i, 0))),
        out_shape=jax.ShapeDtypeStruct(r.shape, _MX),
        compiler_params=_cparams(("parallel",)),
    )(c_arr, g, r)


_HBM = pl.BlockSpec(memory_space=pltpu.HBM)
_SEM = pl.BlockSpec(memory_space=pltpu.SEMAPHORE)
_FLOWS = pltpu.SideEffectType.DATAFLOW_SIDE_EFFECTING


def _chip_exchange_copies(refs, send, recv, splits):
    nw = len(refs) // 2
    x, y, c = _place()
    me = 2 * x + y
    copies = []
    for i in range(nw):
        for px, py in _other_chips(x, y):
            copies.append((splits[i].chip_part(refs[i], 2 * px + py), refs[nw + i].at[me], (px, py, c)))
    return [pltpu.make_async_remote_copy(src_ref=src, dst_ref=dst, send_sem=send.at[k], recv_sem=recv.at[k],
                                         device_id=peer, device_id_type=_MESH)
            for k, (src, dst, peer) in enumerate(copies)]


def _late_gather_copies(refs, send, recv, splits):
    nw = len(refs) // 2
    x, y, c = _place()
    me = 2 * x + y
    copies = []
    for i in range(nw):
        for px, py in [(x, y)] + _other_chips(x, y):
            copies.append((refs[i], splits[i].chip_part(refs[nw + i], me), (px, py, c)))
    return [pltpu.make_async_remote_copy(src_ref=src, dst_ref=dst, send_sem=send.at[k], recv_sem=recv.at[k],
                                         device_id=peer, device_id_type=_MESH)
            for k, (src, dst, peer) in enumerate(copies)]


def _start_copies(name, flying, copies_of, n_copies, after=None):
    first = [] if after is None else [after]

    def body(*refs):
        ins = refs[:len(flying)]
        send, recv = refs[len(flying) + len(first):len(flying) + len(first) + 2]
        token = refs[-1]
        for cp in copies_of(ins, send, recv):
            cp.start()
        token[...] = jnp.zeros_like(token)

    outs = pl.pallas_call(
        body, name=name,
        in_specs=[_HBM] * len(flying) + [_ANY] * len(first),
        out_specs=[_SEM, _SEM] + [_HBM] * len(flying) + [pl.BlockSpec(memory_space=pltpu.VMEM)],
        out_shape=[pltpu.SemaphoreType.DMA((n_copies,)), pltpu.SemaphoreType.DMA((n_copies,))]
        + [pltpu.HBM(f.shape, f.dtype) for f in flying] + [jax.ShapeDtypeStruct((8, LANE), _F32)],
        input_output_aliases={i: 2 + i for i in range(len(flying))},
        compiler_params=pltpu.CompilerParams(has_side_effects=_FLOWS),
    )(*[pltpu.with_memory_space_constraint(f, pltpu.HBM) for f in flying], *first)
    return outs[0], outs[1], outs[2:2 + len(flying)], outs[-1]


def _wait_copies(name, send, recv, flying, copies_of, after):
    def body(*refs):
        ins = refs[:len(flying)]
        send_ref, recv_ref = refs[len(flying):len(flying) + 2]
        for cp in copies_of(ins, send_ref, recv_ref):
            cp.wait_send()
            cp.wait_recv()

    return pl.pallas_call(
        body, name=name,
        in_specs=[_HBM] * len(flying) + [_SEM, _SEM, _ANY],
        out_specs=[_HBM] * len(flying),
        out_shape=[pltpu.HBM(f.shape, f.dtype) for f in flying],
        input_output_aliases={i: i for i in range(len(flying))},
        compiler_params=pltpu.CompilerParams(has_side_effects=_FLOWS),
    )(*flying, send, recv, after)


def _sum_chips(q, p, split, place_arr, name):
    _, hr, hn = q.shape
    if split.by_rows:
        out_shape = (hr, 2 * hn)
        o_spec = pl.BlockSpec((hr, hn), lambda i, pr: (0, pr[0]))
        p_spec = pl.BlockSpec((pl.Element(hr), pl.Element(hn)), lambda i, pr: (pr[1] * split.step, 0))
    else:
        out_shape = (2 * hr, hn)
        o_spec = pl.BlockSpec((hr, hn), lambda i, pr: (pr[0], 0))
        p_spec = pl.BlockSpec((hr, hn), lambda i, pr: (0, pr[1]))

    def body(pr, q_ref, p_ref, o_ref):
        f = lambda k: jnp.where(pr[1] == k, p_ref[...], q_ref[k]).astype(_F32)
        o_ref[...] = ((f(0) + f(1)) + f(2)) + f(3)

    return pl.pallas_call(
        body, name=name,
        grid_spec=pltpu.PrefetchScalarGridSpec(
            num_scalar_prefetch=1, grid=(1,),
            in_specs=[pl.BlockSpec((4, hr, hn), lambda i, pr: (0, 0, 0)), p_spec], out_specs=o_spec),
        out_shape=jax.ShapeDtypeStruct(out_shape, _F32),
        compiler_params=_cparams(("arbitrary",), VMEM_LIMIT),
    )(place_arr, q, p)


def _pair_share(bufs, splits, small):
    nw = len(bufs)

    def body(*refs):
        ins, small_ref, outs, all_ref = refs[:nw], refs[nw], refs[nw + 1:2 * nw + 1], refs[2 * nw + 1]
        send, recv, s_send, s_recv = refs[2 * nw + 2:]
        x, y, c = _place()
        copies = []
        for r in range(1, 8):
            peer = (1 - x if r & 4 else x, 1 - y if r & 2 else y, 1 - c if r & 1 else c)
            cp = pltpu.make_async_remote_copy(
                src_ref=small_ref, dst_ref=all_ref.at[4 * x + 2 * y + c], send_sem=s_send.at[r - 1],
                recv_sem=s_recv.at[r - 1], device_id=peer, device_id_type=_MESH)
            cp.start()
            copies.append(cp)
        for i in range(nw):
            cp = pltpu.make_async_remote_copy(
                src_ref=splits[i].half(ins[i], c), dst_ref=splits[i].half(outs[i], c), send_sem=send.at[i],
                recv_sem=recv.at[i], device_id=(x, y, 1 - c), device_id_type=_MESH)
            cp.start()
            copies.append(cp)
        for cp in copies:
            cp.wait()

    return pl.pallas_call(
        body, name="grad_pair_share",
        in_specs=[_ANY] * (nw + 1), out_specs=[_ANY] * (nw + 1),
        out_shape=[jax.ShapeDtypeStruct(b.shape, b.dtype) for b in bufs]
        + [jax.ShapeDtypeStruct((8,) + small.shape, small.dtype)],
        input_output_aliases={i: i for i in range(nw)},
        scratch_shapes=[pltpu.SemaphoreType.DMA((nw,)), pltpu.SemaphoreType.DMA((nw,)),
                        pltpu.SemaphoreType.DMA((7,)), pltpu.SemaphoreType.DMA((7,))],
    )(*bufs, small)


def _sum_devices(parts, own, dev_arr):
    def body(dev, p_ref, own_ref, tot_ref):
        f = lambda d: jnp.where(dev[0] == d, own_ref[...], p_ref[d])
        acc = f(0)
        for d in range(1, 8):
            acc = acc + f(d)
        tot_ref[...] = acc

    return pl.pallas_call(
        body, name="small_sum",
        grid_spec=pltpu.PrefetchScalarGridSpec(
            num_scalar_prefetch=1, grid=(1,),
            in_specs=[pl.BlockSpec(parts.shape, lambda i, dev: (0, 0, 0)), pl.BlockSpec(own.shape, lambda i, dev: (0, 0))],
            out_specs=pl.BlockSpec(own.shape, lambda i, dev: (0, 0))),
        out_shape=jax.ShapeDtypeStruct(own.shape, own.dtype),
    )(dev_arr, parts, own)


def _adam_update(w, g, m, v):
    m2 = ADAM_B1 * m + (1.0 - ADAM_B1) * g
    v2 = ADAM_B2 * v + (1.0 - ADAM_B2) * (g * g)
    m_hat = m2 / (1.0 - ADAM_B1 ** ADAM_STEP)
    v_hat = v2 / (1.0 - ADAM_B2 ** ADAM_STEP)
    return -ADAM_LR * (m_hat / (jnp.sqrt(v_hat) + ADAM_EPS) + ADAM_WD * w), m2, v2


SMALL_AT = dict(g_in=(0, 0), g_final=(1, 0), g_gla_norm=(2, 0), b_alpha=(2, B_WIDTH), attn_sinks=(2, B_WIDTH + B_KEY_WIDTH))
LOSS_AT = (2, B_WIDTH + B_KEY_WIDTH + LANE)
WUP_ROWS = (3, 7)


def _adamw_small(tot, g_wup, params):
    names = list(params)

    def body(*refs):
        tot_ref, gw_ref = refs[0], refs[1]
        ins = refs[2:2 + 3 * len(names)]
        outs = refs[2 + 3 * len(names):]
        for i, nm in enumerate(names):
            w_ref, m_ref, v_ref = ins[3 * i:3 * i + 3]
            if nm in SMALL_AT:
                r, a = SMALL_AT[nm]
                g = tot_ref[r:r + 1, a:a + w_ref.shape[1]]
            else:
                g = gw_ref[...]
            d, m2, v2 = _adam_update(w_ref[...], g, m_ref[...], v_ref[...])
            for o_ref, val in zip(outs[4 * i:4 * i + 4], (g, d, m2, v2)):
                o_ref[...] = val

    vm = pl.BlockSpec(memory_space=pltpu.VMEM)
    flat = [a for nm in names for a in params[nm]]
    out_shape = [jax.ShapeDtypeStruct(params[nm][0].shape, _F32) for nm in names for _ in range(4)]
    outs = pl.pallas_call(
        body, name="adamw_small", in_specs=[vm] * (2 + len(flat)), out_specs=[vm] * len(out_shape), out_shape=out_shape,
    )(tot, g_wup, *flat)
    return {nm: tuple(outs[4 * i:4 * i + 4]) for i, nm in enumerate(names)}


def _adamw(w, g, m, v, name):
    lead = w.shape[0] != 1
    r, n = (w.shape[0], w.shape[2]) if lead else w.shape[1:]
    br = r
    for cand in (256, 244, 128):
        if r > cand and r % cand == 0:
            br = cand
            break

    def body(w_ref, g_ref, m_ref, v_ref, d_ref, nm_ref, nv_ref):
        d_ref[...], nm_ref[...], nv_ref[...] = _adam_update(w_ref[...], g_ref[...], m_ref[...], v_ref[...])

    blk = pl.BlockSpec((br, 1, n), lambda i: (i, 0, 0)) if lead else pl.BlockSpec((None, br, n), lambda i: (0, i, 0))
    shp = jax.ShapeDtypeStruct(w.shape, _F32)
    return pl.pallas_call(
        body, name=name, grid=(r // br,),
        in_specs=[blk] * 4, out_specs=[blk] * 3, out_shape=[shp] * 3,
        compiler_params=_cparams(("parallel",)),
    )(w, g, m, v)


def kernel(x, positions, g_in, w_in, w_alpha_up, b_alpha, attn_sinks, g_gla_norm, w_out_a, w_out_b, w_o, g_final, loss_target, m_g_in, m_w_in, m_w_alpha_up, m_b_alpha, m_attn_sinks, m_g_gla_norm, m_w_out_a, m_w_out_b, m_w_o, m_g_final, v_g_in, v_w_in, v_w_alpha_up, v_b_alpha, v_attn_sinks, v_g_gla_norm, v_w_out_a, v_w_out_b, v_w_o, v_g_final):
    nseq, seq, _ = x.shape
    t = nseq * seq
    cx, cy, cc = _place()
    chip = 2 * cx + cy
    c_arr = jnp.reshape(cc, (1,)).astype(jnp.int32)

    tr = lambda w: jnp.transpose(w, (2, 0, 1))
    w_in_t = tr(w_in).reshape(SHARD, D_MODEL).astype(_MX)
    pad = WINDOW_ROWS - SHARD
    window = lax.switch(chip, [lambda w, k=k: jnp.pad(w, ((4 * k, pad - 4 * k), (0, 0))) for k in range(4)], w_in_t)
    shards = [window, w_alpha_up[0].astype(_MX)]
    late = [w_out_a[0].astype(_MX), w_out_b[0].astype(_MX), w_o[0].astype(_MX)]
    late_splits = [SPLIT_W_OUT, SPLIT_W_OUT, SPLIT_W_O]
    splits = [SPLIT_W_IN_T, None]
    fulls = [(4, WINDOW_ROWS, D_MODEL), (4, B_GATE_RANK, B_KEY_WIDTH // 4)]
    pos_f = positions.astype(_F32).reshape(t, 1)
    win_g, wup_g, cos, sa, sb = _gather_weights(shards, splits, fulls, pos_f)
    late_copies = lambda refs, send, recv: _late_gather_copies(refs, send, recv, late_splits)
    late_full = [lax.empty(shape, _MX) for shape in ((A_WIDTH, D_MODEL), (B_WIDTH, D_MODEL), (D_MODEL, D_MODEL))]
    l_send, l_recv, l_flying, l_token = _start_copies("late_gather_start", late + late_full, late_copies,
                                                      4 * len(late), after=win_g)
    win_g = lax.dynamic_update_slice(win_g, window[None], (chip, 0, 0))
    wup_g = lax.dynamic_update_slice(wup_g, shards[1][None], (chip, 0, 0))
    wt = _assemble_w_in_t(win_g)
    wup = jnp.concatenate([jnp.transpose(wup_g, (1, 0, 2)).reshape(B_GATE_RANK, B_KEY_WIDTH),
                           jnp.zeros((LANE - B_GATE_RANK, B_KEY_WIDTH), _MX)], axis=0)

    x2 = x.reshape(t, D_MODEL)
    tgt = loss_target.reshape(t, D_MODEL)
    sinks = attn_sinks.reshape(A_HEADS)
    gf = g_final.reshape(1, D_MODEL)

    h, qkv, za, qkb, vb, zb, alr, ga, gb = _in_proj(x2, g_in + l_token[0, 0], wt, cos, sa, sb)
    oa = _attn_fwd(qkv, za, sinks, nseq)
    ob, oraw, sst = _gla_fwd(qkb, vb, zb, alr, wup, b_alpha, g_gla_norm, nseq)

    wa, wb, wo = _wait_copies("late_gather_wait", l_send, l_recv, l_flying, late_copies, ob)[len(late):]
    dh2, doa, dob, dga, dgb, dwa, dwb, dwo, dgf, lossv = _merge_loss(oa, ob, ga, gb, x2, tgt, wa, wb, wo, gf)

    dqkv, dza, dsink = _attn_bwd(qkv, za, doa, sinks, cos, sa, sb, nseq)
    dqkb, dvb, dzb, dalr, dwup, dba, dgn = _gla_bwd(qkb, vb, zb, alr, oraw, dob, sst, wup, b_alpha, g_gla_norm, nseq)
    dpieces = [dqkv, dza, dqkb, dvb, dzb, dalr, dga, dgb]
    gsplits = [SPLIT_W_IN_T, SPLIT_W_OUT, SPLIT_W_OUT, SPLIT_W_O]
    names = ("w_in", "w_out_a", "w_out_b", "w_o")
    dwin_mine, *from_sibling = _in_proj_bwd_w(h, dpieces, [dwa, dwb, dwo], gsplits[1:])
    pair_sums = [_pair_sum(g, r, sp, c_arr, "pair_sum_" + nm)
                 for g, r, sp, nm in zip([dwin_mine, dwa, dwb, dwo], from_sibling, [None] + gsplits[1:], names)]
    exchange = lambda refs, send, recv: _chip_exchange_copies(refs, send, recv, gsplits)
    lands = [lax.empty((4,) + sp.part_shape(p.shape), p.dtype) for p, sp in zip(pair_sums, gsplits)]
    send, recv, flying, token = _start_copies("grad_chip_exchange_start", pair_sums + lands, exchange, 3 * len(lands))
    grad_x2, dgin = _in_proj_bwd_x(dpieces, wt, x2, dh2, g_in + token[0, 0])
    landed = _wait_copies("grad_chip_exchange_wait", send, recv, flying, exchange, grad_x2)
    place_arr = jnp.stack([cc, chip]).astype(jnp.int32)
    reduced = [_sum_chips(q, p, sp, place_arr, "chip_sum_" + nm)
               for q, p, sp, nm in zip(landed[len(lands):], landed[:len(lands)], gsplits, names)]
    row2 = jnp.concatenate([dgn, dba, jnp.pad(dsink[:, 0].reshape(1, A_HEADS), ((0, 0), (0, LANE - A_HEADS))),
                            jnp.pad(jnp.sum(lossv, axis=1, keepdims=True), ((0, 0), (0, LANE - 1)))], axis=1)
    small = jnp.concatenate([dgin, dgf, row2, dwup[:B_GATE_RANK].reshape(WUP_ROWS[1] - WUP_ROWS[0], D_MODEL),
                             jnp.zeros((1, D_MODEL), _F32)], axis=0)
    g_window, g_wa, g_wb, g_wo, small_parts = _pair_share(reduced, gsplits, small)
    g_win_t = lax.switch(chip, [lambda w, k=k: w[4 * k:4 * k + SHARD].reshape(SHARD, 1, D_MODEL) for k in range(4)],
                         g_window)
    dev_arr = jnp.reshape(2 * chip + cc, (1,)).astype(jnp.int32)
    tot = _sum_devices(small_parts, small, dev_arr)
    loss = tot[LOSS_AT]
    nup = B_KEY_WIDTH // 4
    g_wup = lax.dynamic_slice(tot[WUP_ROWS[0]:WUP_ROWS[1]].reshape(B_GATE_RANK, B_KEY_WIDTH), (0, chip * nup),
                              (B_GATE_RANK, nup))

    row = lambda a: a.reshape(1, -1)
    sm = _adamw_small(tot, g_wup, dict(
        g_in=(g_in, m_g_in, v_g_in), g_final=(row(g_final), row(m_g_final), row(v_g_final)),
        g_gla_norm=(g_gla_norm, m_g_gla_norm, v_g_gla_norm), b_alpha=(b_alpha, m_b_alpha, v_b_alpha),
        attn_sinks=(attn_sinks, m_attn_sinks, v_attn_sinks),
        w_alpha_up=(w_alpha_up[0], m_w_alpha_up[0], v_w_alpha_up[0])))
    sm["g_final"] = tuple(a.reshape(D_MODEL) for a in sm["g_final"])
    sm["w_alpha_up"] = tuple(a[None] for a in sm["w_alpha_up"])

    untr = lambda a: jnp.transpose(a, (1, 2, 0))
    big = dict(w_in=tuple(untr(a) for a in (g_win_t,) + tuple(_adamw(tr(w_in), g_win_t, tr(m_w_in), tr(v_w_in), "adamw_w_in"))))
    for nm, w, g, m, v in (("w_out_a", w_out_a, g_wa, m_w_out_a, v_w_out_a),
                           ("w_out_b", w_out_b, g_wb, m_w_out_b, v_w_out_b), ("w_o", w_o, g_wo, m_w_o, v_w_o)):
        big[nm] = (g[None],) + tuple(_adamw(w, g[None], m, v, "adamw_" + nm))

    order = ("g_in", "w_in", "w_alpha_up", "b_alpha", "attn_sinks", "g_gla_norm", "w_out_a", "w_out_b", "w_o", "g_final")
    outs = [big[nm][kind] if nm in big else sm[nm][kind] for kind in range(4) for nm in order]
    return (loss, grad_x2.reshape(x.shape), *outs)
```

```python
import math
from typing import NamedTuple

import numpy as np
import jax
import jax.numpy as jnp
from jax import lax
from jax.experimental import pallas as pl
from jax.experimental.pallas import tpu as pltpu

D_MODEL = 1024
A_HEADS, A_KV_HEADS, A_HEAD_DIM = 8, 2, 64
A_GROUP = A_HEADS // A_KV_HEADS
A_WIDTH, A_KV_WIDTH = 512, 128
BLOCK = 128
ROPE_THETA = 500000.0
ROPE_DIM = 16
B_HEADS, B_KEY_DIM, B_VAL_DIM = 4, 64, 128
B_KEY_WIDTH, B_WIDTH = 256, 512
B_GATE_RANK = 16
B_GATE_TEMP = 16.0
B_CHUNK = 64
NORM_EPS = 1e-6
NEG_BIG = -1e30
D_IN = 4880

ADAM_LR, ADAM_B1, ADAM_B2, ADAM_EPS, ADAM_WD, ADAM_STEP = 0.001, 0.9, 0.999, 1e-08, 0.01, 10

LANE = 128
ALR_AT = 2816
PIECES = (("qkv", 0, 768), ("za", 768, 1280), ("qkb", 1280, 1792), ("vb", 1792, 2304),
          ("zb", 2304, 2816), ("alr", ALR_AT, ALR_AT + LANE), ("ga", 2832, 3856), ("gb", 3856, 4880))
SHARD = D_IN // 4
WINDOW_STEP = 1216
WINDOW_ROWS = 1232

GLA_BLOCK = 256
GLA_FWD_BLOCK = 1024
MERGE_SLAB = 16
VMEM_LIMIT = 56 * 1024 * 1024

_F32 = jnp.float32
_MX = jnp.bfloat16
_ST = jnp.bfloat16

_MESH = pl.DeviceIdType.MESH
_ANY = pl.BlockSpec(memory_space=pl.ANY)


def _cparams(sem=None, vmem=None):
    return pltpu.CompilerParams(dimension_semantics=sem, vmem_limit_bytes=vmem)


def _dot(a, b):
    return jnp.dot(a.astype(_MX), b.astype(_MX), preferred_element_type=_F32)


def _dot_nt(a, b):
    return lax.dot_general(a.astype(_MX), b.astype(_MX), (((1,), (1,)), ((), ())),
                           preferred_element_type=_F32)


def _dot_tn(a, b):
    return lax.dot_general(a.astype(_MX), b.astype(_MX), (((0,), (0,)), ((), ())),
                           preferred_element_type=_F32)


def _dot_ones(ones_mat, v):
    o = ones_mat.astype(jnp.bfloat16)
    v0 = v.astype(jnp.bfloat16)
    r1 = v - v0.astype(_F32)
    v1 = r1.astype(jnp.bfloat16)
    v2 = (r1 - v1.astype(_F32)).astype(jnp.bfloat16)
    d = lambda t: jnp.dot(o, t, preferred_element_type=_F32)
    return d(v0) + d(v1) + d(v2)


def _sigmoid(x):
    return 0.5 * jnp.tanh(0.5 * x) + 0.5


def _log_sigmoid(x):
    return jnp.minimum(x, 0.0) - jnp.log(1.0 + jnp.exp(-jnp.abs(x)))


def _lane_tile(t, width):
    reps = width // t.shape[1]
    return t if reps == 1 else jnp.tile(t, (1, reps))


def _rope(t, cos, sa, sb, sign):
    w = t.shape[1]
    rot = pltpu.roll(t, w - 8, 1) * _lane_tile(sa, w) + pltpu.roll(t, 8, 1) * _lane_tile(sb, w)
    return t * _lane_tile(cos, w) + sign * rot


def _rms_bwd(dy_g, n, r):
    return r * (dy_g - n * jnp.mean(dy_g * n, axis=-1, keepdims=True))


ROPE_ROWS = 256


def _rope_consts():
    lane = np.arange(LANE) % A_HEAD_DIM
    half = ROPE_DIM // 2
    inv = np.exp((np.float32(-math.log(ROPE_THETA)) * np.arange(half, dtype=np.float32)) * np.float32(2.0 / ROPE_DIM))
    consts = np.zeros((8, LANE), np.float32)
    consts[0] = np.where(lane < ROPE_DIM, inv[lane % half], 0.0)
    consts[1] = np.where(lane < half, -1.0, 0.0)
    consts[2] = np.where((lane >= half) & (lane < ROPE_DIM), 1.0, 0.0)
    return jnp.asarray(consts)


def _rope_tables_into(pos_ref, c_ref, cos_ref, sa_ref, sb_ref):
    def rows_of(b, carry):
        rows = pl.ds(pl.multiple_of(b * ROPE_ROWS, ROPE_ROWS), ROPE_ROWS)
        ang = pos_ref[rows, :] * c_ref[0:1, :]
        s = jnp.sin(ang)
        cos_ref[rows, :] = jnp.cos(ang)
        sa_ref[rows, :] = s * c_ref[1:2, :]
        sb_ref[rows, :] = s * c_ref[2:3, :]
        return carry

    lax.fori_loop(0, pos_ref.shape[0] // ROPE_ROWS, rows_of, 0)


def _in_proj(x2, g_in, wt, cos, sa, sb):
    t = x2.shape[0]
    tm = min(t, 512)

    def body(x_ref, g_ref, w_ref, cos_ref, sa_ref, sb_ref, h_ref, qkv_ref, za_ref, qkb_ref,
             vb_ref, zb_ref, alr_ref, ga_ref, gb_ref):
        xv = x_ref[...]
        r = lax.rsqrt(jnp.mean(xv * xv, axis=-1, keepdims=True) + NORM_EPS)
        h = (xv * r * g_ref[...]).astype(_MX)
        h_ref[...] = h.astype(_ST)
        outs = dict(za=za_ref, qkb=qkb_ref, vb=vb_ref, zb=zb_ref, alr=alr_ref, ga=ga_ref, gb=gb_ref)
        for name, a, b in PIECES:
            p = _dot_nt(h, w_ref[a:b, :])
            if name == "qkv":
                c, s1, s2 = cos_ref[...], sa_ref[...], sb_ref[...]
                qkv_ref[:, 0:512] = _rope(p[:, 0:512], c, s1, s2, 1.0).astype(_ST)
                qkv_ref[:, 512:640] = _rope(p[:, 512:640], c, s1, s2, 1.0).astype(_ST)
                qkv_ref[:, 640:768] = p[:, 640:768].astype(_ST)
            else:
                outs[name][...] = p.astype(outs[name].dtype)

    rows = lambda w: pl.BlockSpec((tm, w), lambda i: (i, 0))
    shp = lambda name, w: jax.ShapeDtypeStruct((t, w), _F32 if name == "qkb" else _ST)
    widths = [D_MODEL] + [b - a for _, a, b in PIECES]
    return pl.pallas_call(
        body, name="in_proj", grid=(t // tm,),
        in_specs=[rows(D_MODEL), pl.BlockSpec((1, D_MODEL), lambda i: (0, 0)),
                  pl.BlockSpec((D_IN, D_MODEL), lambda i: (0, 0), pipeline_mode=pl.Buffered(1)),
                  rows(LANE), rows(LANE), rows(LANE)],
        out_specs=[rows(w) for w in widths],
        out_shape=[shp(n, w) for n, w in zip(["h"] + [p[0] for p in PIECES], widths)],
        compiler_params=_cparams(("parallel",), VMEM_LIMIT),
    )(x2, g_in, wt, cos, sa, sb)


def _attn_operands(k_prev, k_cur, v_prev, v_cur, want_bwd):
    kf = jnp.concatenate([k_prev, k_cur], axis=0).astype(_F32) * (A_HEAD_DIM ** -0.5)
    vf = jnp.concatenate([v_prev, v_cur], axis=0).astype(_F32)
    lo = lax.broadcasted_iota(jnp.int32, (1, LANE), 1) < 64

    def on_lanes(a):
        sw = pltpu.roll(a, 64, 1)
        z = jnp.zeros_like(a)
        return [[jnp.where(lo, a, z).astype(_MX), jnp.where(lo, z, sw).astype(_MX)],
                [jnp.where(lo, sw, z).astype(_MX), jnp.where(lo, z, a).astype(_MX)]]

    def on_rows(a):
        at = a.T.astype(_MX)
        z = jnp.zeros((64, at.shape[1]), _MX)
        top, bot = at[0:64], at[64:128]
        return [[jnp.concatenate([top, z], axis=0), jnp.concatenate([z, top], axis=0)],
                [jnp.concatenate([bot, z], axis=0), jnp.concatenate([z, bot], axis=0)]]

    ops = dict(k_lanes=on_lanes(kf), v_rows=on_rows(vf), lo=lo)
    if want_bwd:
        ops.update(v_lanes=on_lanes(vf), k_rows=on_rows(kf))
    return ops


def _attn_valid(n):
    kj = lax.broadcasted_iota(jnp.int32, (2 * BLOCK, 2 * BLOCK), 0) - BLOCK
    qi = lax.broadcasted_iota(jnp.int32, (2 * BLOCK, 2 * BLOCK), 1) & (BLOCK - 1)
    return (kj <= qi) & (qi - kj < BLOCK) & ((n > 0) | (kj >= 0))


def _attn_sinks(sink_ref, h_a, h_b):
    first = lax.broadcasted_iota(jnp.int32, (1, 2 * BLOCK), 1) < BLOCK
    return jnp.where(first, sink_ref[h_a], sink_ref[h_b])


def _attn_softmax_t(k_lanes, q_pair, valid, sink):
    s = jnp.where(valid, _dot_nt(k_lanes, q_pair), NEG_BIG)
    m = jnp.maximum(jnp.max(s, axis=0, keepdims=True), sink)
    e = jnp.exp(s - m)
    e_sink = jnp.exp(sink - m)
    inv = 1.0 / (jnp.sum(e, axis=0, keepdims=True) + e_sink)
    return e, e_sink, inv


ATTN_TILE = 8


def _attn_kv(qkv_ref, kvp_ref, j):
    rows = slice(j * BLOCK, (j + 1) * BLOCK)
    if j == 0:
        k_prev, v_prev = kvp_ref[:, 0:128], kvp_ref[:, 128:256]
    else:
        before = slice((j - 1) * BLOCK, j * BLOCK)
        k_prev, v_prev = qkv_ref[before, 512:640], qkv_ref[before, 640:768]
    return k_prev, qkv_ref[rows, 512:640], v_prev, qkv_ref[rows, 640:768]


def _attn_fwd(qkv, za, sinks, nseq):
    t = qkv.shape[0]
    nblk = min(ATTN_TILE, t // nseq // BLOCK)
    tile = nblk * BLOCK
    nt = t // nseq // tile

    def body(sink_ref, qkv_ref, kvp_ref, za_ref, oa_ref):
        for j in range(nblk):
            rows = slice(j * BLOCK, (j + 1) * BLOCK)
            ops = _attn_operands(*_attn_kv(qkv_ref, kvp_ref, j), False)
            valid = _attn_valid(nblk * pl.program_id(1) + j)[:, 0:BLOCK]
            for pr in range(A_HEADS // 2):
                lanes = slice(pr * LANE, (pr + 1) * LANE)
                g = pr // (A_GROUP // 2)
                q_pair = qkv_ref[rows, lanes]
                ot = None
                for half in range(2):
                    e, _, inv = _attn_softmax_t(ops["k_lanes"][g][half], q_pair, valid, sink_ref[2 * pr + half])
                    part = _dot(ops["v_rows"][g][half], e) * inv
                    ot = part if ot is None else ot + part
                z = za_ref[rows, lanes].astype(_F32)
                oa_ref[rows, lanes] = (ot.T * (z * _sigmoid(z))).astype(_ST)

    cur = lambda w: pl.BlockSpec((tile, w), lambda s, n: (s * nt + n, 0))
    return pl.pallas_call(
        body, name="attn_fwd", grid=(nseq, nt),
        in_specs=[pl.BlockSpec(memory_space=pltpu.SMEM), cur(768),
                  pl.BlockSpec((BLOCK, 256), lambda s, n: (nblk * (s * nt + n) - jnp.minimum(n, 1), 2)),
                  cur(512)],
        out_specs=cur(512), out_shape=jax.ShapeDtypeStruct((t, A_WIDTH), _ST),
        compiler_params=_cparams(("parallel", "arbitrary")),
    )(sinks, qkv, qkv, za)


def _attn_bwd(qkv, za, doa, sinks, cos, sa, sb, nseq):
    t = qkv.shape[0]
    nblk = min(ATTN_TILE, t // nseq // BLOCK)
    tile = nblk * BLOCK
    nt = t // nseq // tile

    def body(sink_ref, qkv_ref, kvp_ref, za_ref, doa_ref, cos_ref, sa_ref, sb_ref,
             dqkv_ref, dza_ref, dsink_ref, ck_ref, cv_ref):
        s_id, i = pl.program_id(0), pl.program_id(1)

        @pl.when((s_id == 0) & (i == 0))
        def _():
            dsink_ref[...] = jnp.zeros_like(dsink_ref)

        @pl.when(i == 0)
        def _():
            ck_ref[...] = jnp.zeros_like(ck_ref)
            cv_ref[...] = jnp.zeros_like(cv_ref)

        carry_k, carry_v = ck_ref[...], cv_ref[...]
        for j in reversed(range(nblk)):
            rows = slice(j * BLOCK, (j + 1) * BLOCK)
            ops = _attn_operands(*_attn_kv(qkv_ref, kvp_ref, j), True)
            lo = ops["lo"]
            valid = _attn_valid(nblk * (nt - 1 - i) + j)
            dk_acc, dv_acc, dq_pairs = [], [], []
            for g in range(A_KV_HEADS):
                pairs = [slice((2 * g + p) * LANE, (2 * g + p + 1) * LANE) for p in range(2)]
                q_both = jnp.concatenate([qkv_ref[rows, p] for p in pairs], axis=0)
                q_f = q_both.astype(_F32)
                z = [za_ref[rows, p].astype(_F32) for p in pairs]
                sz = [_sigmoid(v) for v in z]
                d_oa = [doa_ref[rows, p].astype(_F32) for p in pairs]
                d_att = jnp.concatenate([d_oa[p] * (z[p] * sz[p]) for p in range(2)], axis=0)
                zero = jnp.zeros_like(d_att)
                ot, dqt, ds_all, pn_all, qz_all, daz_all = None, None, [], [], [], []
                for half in range(2):
                    heads = (4 * g + half, 4 * g + 2 + half)
                    e, e_sink, inv = _attn_softmax_t(ops["k_lanes"][g][half], q_both, valid,
                                                     _attn_sinks(sink_ref, *heads))
                    pn = e * inv
                    dpt = _dot_nt(ops["v_lanes"][g][half], d_att)
                    delta = jnp.sum(pn * dpt, axis=0, keepdims=True)
                    ds = (pn * (dpt - delta)).astype(_MX)
                    pn = pn.astype(_MX)
                    d_sink = e_sink * inv * delta
                    for p, h in enumerate(heads):
                        dsink_ref[h:h + 1, :] = dsink_ref[h:h + 1, :] - jnp.sum(d_sink[:, p * BLOCK:(p + 1) * BLOCK])
                    o_part = _dot(ops["v_rows"][g][half], pn)
                    dq_part = _dot(ops["k_rows"][g][half], ds)
                    ot = o_part if ot is None else ot + o_part
                    dqt = dq_part if dqt is None else dqt + dq_part
                    mine = lo if half == 0 else jnp.logical_not(lo)
                    ds_all.append(ds)
                    pn_all.append(pn)
                    qz_all.append(jnp.where(mine, q_f, zero).astype(_MX))
                    daz_all.append(jnp.where(mine, d_att, zero).astype(_MX))
                dk_acc.append(_dot(jnp.concatenate(ds_all, axis=1), jnp.concatenate(qz_all, axis=0)))
                dv_acc.append(_dot(jnp.concatenate(pn_all, axis=1), jnp.concatenate(daz_all, axis=0)))
                for p, lanes in enumerate(pairs):
                    cols = slice(p * BLOCK, (p + 1) * BLOCK)
                    dza_ref[rows, lanes] = (d_oa[p] * ot[:, cols].T * (sz[p] * (1.0 + z[p] * (1.0 - sz[p])))).astype(_ST)
                    dq_pairs.append(dqt[:, cols].T)

            def fold(acc, scale):
                both = [a + pltpu.roll(a, 64, 1) for a in acc]
                return jnp.where(lo, both[0], both[1]) * scale

            dk_full = fold(dk_acc, A_HEAD_DIM ** -0.5)
            dv_full = fold(dv_acc, 1.0)
            dk_cur, dv_cur = dk_full[BLOCK:] + carry_k, dv_full[BLOCK:] + carry_v
            carry_k, carry_v = dk_full[:BLOCK], dv_full[:BLOCK]
            c, s1, s2 = cos_ref[rows, :], sa_ref[rows, :], sb_ref[rows, :]
            dqkv_ref[rows, 0:512] = _rope(jnp.concatenate(dq_pairs, axis=1), c, s1, s2, -1.0).astype(_ST)
            dqkv_ref[rows, 512:640] = _rope(dk_cur, c, s1, s2, -1.0).astype(_ST)
            dqkv_ref[rows, 640:768] = dv_cur.astype(_ST)
        ck_ref[...] = carry_k
        cv_ref[...] = carry_v

    cur = lambda w: pl.BlockSpec((tile, w), lambda s, i: (s * nt + nt - 1 - i, 0))
    return pl.pallas_call(
        body, name="attn_bwd", grid=(nseq, nt),
        in_specs=[pl.BlockSpec(memory_space=pltpu.SMEM), cur(768),
                  pl.BlockSpec((BLOCK, 256),
                               lambda s, i: (nblk * (s * nt + nt - 1 - i) - jnp.minimum(nt - 1 - i, 1), 2)),
                  cur(512), cur(512), cur(LANE), cur(LANE), cur(LANE)],
        out_specs=[cur(768), cur(512), pl.BlockSpec((8, LANE), lambda s, i: (0, 0))],
        out_shape=[jax.ShapeDtypeStruct((t, 768), _ST), jax.ShapeDtypeStruct((t, 512), _ST),
                   jax.ShapeDtypeStruct((8, LANE), _F32)],
        scratch_shapes=[pltpu.VMEM((BLOCK, A_KV_WIDTH), _F32), pltpu.VMEM((BLOCK, A_KV_WIDTH), _F32)],
        compiler_params=_cparams(("arbitrary", "arbitrary")),
    )(sinks, qkv, qkv, za, doa, cos, sa, sb)


def _gla_chunk_terms(la, qkb_ref, r0):
    g = la[r0:r0 + B_CHUNK, :]
    ri = lax.broadcasted_iota(jnp.int32, (B_CHUNK, B_CHUNK), 0)
    ci = lax.broadcasted_iota(jnp.int32, (B_CHUNK, B_CHUNK), 1)
    cum = _dot_ones((ri >= ci).astype(_F32), g)
    last = cum[B_CHUNK - 1:B_CHUNK, :]
    mid = cum[B_CHUNK // 2 - 1:B_CHUNK // 2, :]
    q = qkb_ref[r0:r0 + B_CHUNK, 0:B_KEY_WIDTH].astype(_F32) * (B_KEY_DIM ** -0.5)
    k = qkb_ref[r0:r0 + B_CHUNK, B_KEY_WIDTH:2 * B_KEY_WIDTH].astype(_F32)
    e_q, e_k, e_l, e_c = jnp.exp(cum - mid), jnp.exp(mid - cum), jnp.exp(last - cum), jnp.exp(cum)
    dec_col = jnp.exp(jnp.sum(g.T, axis=1, keepdims=True))
    return dict(qm=q * e_q, km=k * e_k, kl=k * e_l, qc=q * e_c, e_q=e_q, e_k=e_k, e_l=e_l, e_c=e_c,
                dec_col=dec_col, dec_row=jnp.exp(last), causal=ri >= ci, ri=ri)


def _gate_logits(alr_ref, wup_ref, b_ref):
    return _dot(alr_ref[...], wup_ref[...]) + b_ref[...]


def _gla_fwd(qkb, vb, zb, alr, wup, b_alpha, gn, nseq):
    t = qkb.shape[0]
    tb = min(GLA_FWD_BLOCK, t // nseq)
    nblk = t // nseq // tb
    cpb = tb // B_CHUNK

    def body(qkb_ref, vb_ref, zb_ref, alr_ref, wup_ref, b_ref, gn_ref, ob_ref, oraw_ref, sst_ref, s_ref):
        @pl.when(pl.program_id(1) == 0)
        def _():
            s_ref[...] = jnp.zeros_like(s_ref)

        la = _log_sigmoid(_gate_logits(alr_ref, wup_ref, b_ref)) * (1.0 / B_GATE_TEMP)
        terms = [_gla_chunk_terms(la, qkb_ref, c * B_CHUNK) for c in range(cpb)]
        o_intra, inc = {}, {}
        for c, tm in enumerate(terms):
            for h in range(B_HEADS):
                kl_, vl_ = slice(h * 64, (h + 1) * 64), slice(h * 128, (h + 1) * 128)
                v = vb_ref[c * B_CHUNK:(c + 1) * B_CHUNK, vl_]
                a = jnp.where(tm["causal"], _dot_nt(tm["qm"][:, kl_], tm["km"][:, kl_]), 0.0)
                o_intra[c, h] = _dot(a, v)
                inc[c, h] = _dot_tn(tm["kl"][:, kl_], v)
        o_heads = {}
        for h in range(B_HEADS):
            kl_ = slice(h * 64, (h + 1) * 64)
            st = s_ref[kl_, :]
            for c, tm in enumerate(terms):
                sst_ref[c, kl_, :] = st
                o_heads[c, h] = o_intra[c, h] + _dot(tm["qc"][:, kl_], st)
                st = tm["dec_col"][kl_, :] * st + inc[c, h]
            s_ref[kl_, :] = st
        o = jnp.concatenate([jnp.concatenate([o_heads[c, h] for h in range(B_HEADS)], axis=1)
                             for c in range(cpb)], axis=0)
        oraw_ref[...] = o
        z = zb_ref[...].astype(_F32)
        gate = z * _sigmoid(z)
        for h in range(B_HEADS):
            vl_ = slice(h * 128, (h + 1) * 128)
            oh = o[:, vl_]
            r = lax.rsqrt(jnp.mean(oh * oh, axis=-1, keepdims=True) + NORM_EPS)
            ob_ref[:, vl_] = ((oh * r) * gn_ref[:, vl_] * gate[:, vl_]).astype(_ST)

    rows = lambda w: pl.BlockSpec((tb, w), lambda s, i: (s * nblk + i, 0))
    full = lambda a, b: pl.BlockSpec((a, b), lambda s, i: (0, 0))
    return pl.pallas_call(
        body, name="gla_fwd", grid=(nseq, nblk),
        in_specs=[rows(512), rows(512), rows(512), rows(LANE), full(LANE, B_KEY_WIDTH),
                  full(1, B_KEY_WIDTH), full(1, B_WIDTH)],
        out_specs=[rows(512), rows(512),
                   pl.BlockSpec((cpb, B_KEY_WIDTH, B_VAL_DIM), lambda s, i: (s * nblk + i, 0, 0))],
        out_shape=[jax.ShapeDtypeStruct((t, B_WIDTH), _ST), jax.ShapeDtypeStruct((t, B_WIDTH), _F32),
                   jax.ShapeDtypeStruct((t // B_CHUNK, B_KEY_WIDTH, B_VAL_DIM), _F32)],
        scratch_shapes=[pltpu.VMEM((B_KEY_WIDTH, B_VAL_DIM), _F32)],
        compiler_params=_cparams(("parallel", "arbitrary")),
    )(qkb, vb, zb, alr, wup, b_alpha, gn)


def _gla_bwd(qkb, vb, zb, alr, oraw, dob, sst, wup, b_alpha, gn, nseq):
    t = qkb.shape[0]
    tb = min(GLA_BLOCK, t // nseq)
    nblk = t // nseq // tb
    cpb = tb // B_CHUNK

    def body(qkb_ref, vb_ref, zb_ref, alr_ref, oraw_ref, dob_ref, sst_ref, wup_ref, b_ref, gn_ref,
             dqkb_ref, dvb_ref, dzb_ref, dalr_ref, dwup_ref, db_ref, dgn_ref, ds_ref):
        s_id, i = pl.program_id(0), pl.program_id(1)

        @pl.when((s_id == 0) & (i == 0))
        def _():
            dwup_ref[...] = jnp.zeros_like(dwup_ref)
            db_ref[...] = jnp.zeros_like(db_ref)
            dgn_ref[...] = jnp.zeros_like(dgn_ref)

        @pl.when(i == 0)
        def _():
            ds_ref[...] = jnp.zeros_like(ds_ref)

        a_pre = _gate_logits(alr_ref, wup_ref, b_ref)
        la = _log_sigmoid(a_pre) * (1.0 / B_GATE_TEMP)

        z = zb_ref[...].astype(_F32)
        sz = _sigmoid(z)
        d_ob = dob_ref[...].astype(_F32)
        tg = d_ob * (z * sz)
        dsilu = sz * (1.0 + z * (1.0 - sz))
        do_cols, dgn_cols = [], []
        for h in range(B_HEADS):
            vl_ = slice(h * 128, (h + 1) * 128)
            oh = oraw_ref[:, vl_].astype(_F32)
            r = lax.rsqrt(jnp.mean(oh * oh, axis=-1, keepdims=True) + NORM_EPS)
            on = oh * r
            gnh = gn_ref[:, vl_]
            dzb_ref[:, vl_] = (d_ob[:, vl_] * (on * gnh) * dsilu[:, vl_]).astype(_ST)
            dgn_cols.append(jnp.sum(tg[:, vl_] * on, axis=0, keepdims=True))
            do_cols.append(_rms_bwd(tg[:, vl_] * gnh, on, r))
        dgn_ref[...] = dgn_ref[...] + jnp.concatenate(dgn_cols, axis=1)
        d_o = jnp.concatenate(do_cols, axis=1)

        ri = lax.broadcasted_iota(jnp.int32, (tb, tb), 0)
        ci = lax.broadcasted_iota(jnp.int32, (tb, tb), 1)
        same = (ri // B_CHUNK) == (ci // B_CHUNK)
        low = same & (ri >= ci)
        upto_mid = same & ((ci % B_CHUNK) < B_CHUNK // 2)
        sums = _dot_ones(jnp.concatenate([m.astype(_F32) for m in (low, same, upto_mid)], axis=0), la)
        cum, last, mid = sums[0:tb], sums[tb:2 * tb], sums[2 * tb:3 * tb]
        e_q, e_k, e_l, e_c = jnp.exp(cum - mid), jnp.exp(mid - cum), jnp.exp(last - cum), jnp.exp(cum)
        q = qkb_ref[:, 0:B_KEY_WIDTH] * (B_KEY_DIM ** -0.5)
        k = qkb_ref[:, B_KEY_WIDTH:2 * B_KEY_WIDTH]
        qm, km, kl, qc = q * e_q, k * e_k, k * e_l, q * e_c
        lane_head = lax.broadcasted_iota(jnp.int32, (1, B_KEY_WIDTH), 1) // B_KEY_DIM
        d_o_mx = d_o.astype(_MX)

        def on_diagonal(st):
            z = jnp.zeros((B_KEY_DIM, B_VAL_DIM), st.dtype)
            return jnp.concatenate([jnp.concatenate(
                [st[h * B_KEY_DIM:(h + 1) * B_KEY_DIM] if g == h else z for g in range(B_HEADS)], axis=1)
                for h in range(B_HEADS)], axis=0)

        def diagonal_of(full):
            return jnp.concatenate([full[h * B_KEY_DIM:(h + 1) * B_KEY_DIM, h * B_VAL_DIM:(h + 1) * B_VAL_DIM]
                                    for h in range(B_HEADS)], axis=0)

        dqm, dkm, dv_cols = None, None, []
        for h in range(B_HEADS):
            vl_ = slice(h * B_VAL_DIM, (h + 1) * B_VAL_DIM)
            mine = lane_head == h
            qz, kz = jnp.where(mine, qm, 0.0).astype(_MX), jnp.where(mine, km, 0.0).astype(_MX)
            a = jnp.where(low, _dot_nt(qz, kz), 0.0).astype(_MX)
            da = jnp.where(low, _dot_nt(d_o_mx[:, vl_], vb_ref[:, vl_]), 0.0).astype(_MX)
            dqm_h, dkm_h = _dot(da, kz), _dot_tn(da, qz)
            dqm = dqm_h if dqm is None else dqm + dqm_h
            dkm = dkm_h if dkm is None else dkm + dkm_h
            dv_cols.append(_dot_tn(a, d_o_mx[:, vl_]))
        dv = jnp.concatenate(dv_cols, axis=1)

        chunk = [slice(c * B_CHUNK, (c + 1) * B_CHUNK) for c in range(cpb)]
        dqc_rows, g_loc = [], []
        for c in range(cpb):
            dqc_rows.append(_dot_nt(d_o_mx[chunk[c]], on_diagonal(sst_ref[c].astype(_MX))))
            g_loc.append(diagonal_of(_dot_tn(qc[chunk[c]], d_o_mx[chunk[c]])))
        cur = ds_ref[...]
        d_state = [None] * cpb
        for c in reversed(range(cpb)):
            d_state[c] = cur
            cur = g_loc[c] + jnp.exp(jnp.sum(la[chunk[c]].T, axis=1, keepdims=True)) * cur
        ds_ref[...] = cur
        dkl_rows, dv_rows, dlast_rows = [], [], []
        ones8 = jnp.ones((8, B_VAL_DIM), _F32)
        for c in range(cpb):
            dsd = on_diagonal(d_state[c].astype(_MX))
            dkl_c = _dot_nt(vb_ref[chunk[c], :], dsd)
            dkl_rows.append(dkl_c)
            dv_rows.append(_dot(kl[chunk[c]], dsd))
            prod = d_state[c] * sst_ref[c]
            p0 = prod.astype(jnp.bfloat16)
            p1 = (prod - p0.astype(_F32)).astype(jnp.bfloat16)
            p2 = (prod - p0.astype(_F32) - p1.astype(_F32)).astype(jnp.bfloat16)
            ddec = (_dot_nt(ones8, p0) + _dot_nt(ones8, p1) + _dot_nt(ones8, p2))[0:1]
            r_last = c * B_CHUNK + B_CHUNK - 1
            dlast = jnp.sum(dkl_c * kl[chunk[c]], axis=0, keepdims=True) + ddec * jnp.exp(last[r_last:r_last + 1])
            dlast_rows.append(jnp.broadcast_to(dlast, (B_CHUNK, B_KEY_WIDTH)))
        dqc, dkl = jnp.concatenate(dqc_rows, axis=0), jnp.concatenate(dkl_rows, axis=0)
        dqkb_ref[:, 0:B_KEY_WIDTH] = ((dqm * e_q + dqc * e_c) * (B_KEY_DIM ** -0.5)).astype(_ST)
        dqkb_ref[:, B_KEY_WIDTH:2 * B_KEY_WIDTH] = (dkm * e_k + dkl * e_l).astype(_ST)
        dvb_ref[...] = (dv + jnp.concatenate(dv_rows, axis=0)).astype(_ST)
        dcum = dqm * qm - dkm * km + dqc * qc - dkl * kl
        row = lax.broadcasted_iota(jnp.int32, (tb, B_KEY_WIDTH), 0)
        dcum = jnp.where(row % B_CHUNK == B_CHUNK - 1, dcum + jnp.concatenate(dlast_rows, axis=0), dcum)
        dla = _dot_ones((same & (ri <= ci)).astype(_F32), dcum)

        da_pre = dla * (1.0 / B_GATE_TEMP) * (1.0 - _sigmoid(a_pre))
        dalr_ref[...] = _dot_nt(da_pre, wup_ref[...]).astype(_ST)
        dwup_ref[...] = dwup_ref[...] + _dot_tn(alr_ref[...], da_pre)
        db_ref[...] = db_ref[...] + jnp.sum(da_pre, axis=0, keepdims=True)

    blk = lambda s, i: s * nblk + nblk - 1 - i
    rows = lambda w: pl.BlockSpec((tb, w), lambda s, i: (blk(s, i), 0))
    full = lambda a, b: pl.BlockSpec((a, b), lambda s, i: (0, 0))
    act = lambda w: jax.ShapeDtypeStruct((t, w), _ST)
    return pl.pallas_call(
        body, name="gla_bwd", grid=(nseq, nblk),
        in_specs=[rows(512), rows(512), rows(512), rows(LANE), rows(512), rows(512),
                  pl.BlockSpec((cpb, B_KEY_WIDTH, B_VAL_DIM), lambda s, i: (blk(s, i), 0, 0)),
                  full(LANE, B_KEY_WIDTH), full(1, B_KEY_WIDTH), full(1, B_WIDTH)],
        out_specs=[rows(512), rows(512), rows(512), rows(LANE), full(LANE, B_KEY_WIDTH),
                   full(1, B_KEY_WIDTH), full(1, B_WIDTH)],
        out_shape=[act(512), act(512), act(512), act(LANE),
                   jax.ShapeDtypeStruct((LANE, B_KEY_WIDTH), _F32),
                   jax.ShapeDtypeStruct((1, B_KEY_WIDTH), _F32), jax.ShapeDtypeStruct((1, B_WIDTH), _F32)],
        scratch_shapes=[pltpu.VMEM((B_KEY_WIDTH, B_VAL_DIM), _F32)],
        compiler_params=_cparams(("arbitrary", "arbitrary")),
    )(qkb, vb, zb, alr, oraw, dob, sst, wup, b_alpha, gn)


def _merge_loss(oa, ob, ga, gb, x2, tgt, wa, wb, wo, g_final):
    t = x2.shape[0]
    tm = min(t, 512)
    nt = t // tm

    def body(oa_ref, ob_ref, ga_ref, gb_ref, x_ref, t_ref, wa_ref, wb_ref, wo_ref, gf_ref,
             dh_ref, doa_ref, dob_ref, dga_ref, dgb_ref, dwa_ref, dwb_ref, dwo_ref, dgf_ref, loss_ref,
             ya_s, yb_s, out_s, dmer_s, mrg_s, dya_s, dyb_s):
        first = pl.program_id(0) == 0
        so_far = lambda ref: jnp.where(first, 0.0, ref[...])

        slabs = [slice(s, s + MERGE_SLAB) for s in range(0, tm, MERGE_SLAB)]
        fold = lambda a: a[0:8] + a[8:16]
        ya_s[...] = _dot(oa_ref[...], wa_ref[...])
        yb_s[...] = _dot(ob_ref[...], wb_ref[...])
        for rows_ in slabs:
            sga, sgb = _sigmoid(ga_ref[rows_, :].astype(_F32)), _sigmoid(gb_ref[rows_, :].astype(_F32))
            mrg_s[rows_, :] = (sga * ya_s[rows_, :] + sgb * yb_s[rows_, :]).astype(_MX)
        out_s[...] = x_ref[...] + _dot(mrg_s[...], wo_ref[...])
        gf = gf_ref[...]
        loss8 = jnp.zeros((8, D_MODEL), _F32)
        dgf8 = jnp.zeros((8, D_MODEL), _F32)
        for rows_ in slabs:
            out = out_s[rows_, :]
            r = lax.rsqrt(jnp.mean(out * out, axis=-1, keepdims=True) + NORM_EPS)
            nrm = out * r
            err = nrm * gf - t_ref[rows_, :]
            loss8 = loss8 + fold(err * err)
            dy = err * (1.0 / D_MODEL)
            dgf8 = dgf8 + fold(dy * nrm)
            dh = _rms_bwd(dy * gf, nrm, r)
            dh_ref[rows_, :] = dh.astype(_ST)
        loss_ref[...] = so_far(loss_ref) + (0.5 / D_MODEL) * jnp.sum(loss8, axis=0, keepdims=True)
        dgf_ref[...] = so_far(dgf_ref) + jnp.sum(dgf8, axis=0, keepdims=True)
        dmer_s[...] = _dot_nt(dh_ref[...], wo_ref[...])
        dwo_ref[...] = so_far(dwo_ref) + _dot_tn(mrg_s[...], dh_ref[...])
        for rows_ in slabs:
            sga, sgb = _sigmoid(ga_ref[rows_, :].astype(_F32)), _sigmoid(gb_ref[rows_, :].astype(_F32))
            dmer = dmer_s[rows_, :]
            da, db = dmer * sga, dmer * sgb
            dya_s[rows_, :] = da.astype(_MX)
            dyb_s[rows_, :] = db.astype(_MX)
            dga_ref[rows_, :] = (da * ya_s[rows_, :] * (1.0 - sga)).astype(_ST)
            dgb_ref[rows_, :] = (db * yb_s[rows_, :] * (1.0 - sgb)).astype(_ST)
        doa_ref[...] = _dot_nt(dya_s[...], wa_ref[...]).astype(_ST)
        dob_ref[...] = _dot_nt(dyb_s[...], wb_ref[...]).astype(_ST)
        dwa_ref[...] = so_far(dwa_ref) + _dot_tn(oa_ref[...], dya_s[...])
        dwb_ref[...] = so_far(dwb_ref) + _dot_tn(ob_ref[...], dyb_s[...])

    rows = lambda w: pl.BlockSpec((tm, w), lambda i: (i, 0))
    full = lambda a, b: pl.BlockSpec((a, b), lambda i: (0, 0), pipeline_mode=pl.Buffered(1))
    return pl.pallas_call(
        body, name="merge_loss", grid=(nt,),
        in_specs=[rows(512), rows(512), rows(D_MODEL), rows(D_MODEL), rows(D_MODEL), rows(D_MODEL),
                  full(A_WIDTH, D_MODEL), full(B_WIDTH, D_MODEL), full(D_MODEL, D_MODEL), full(1, D_MODEL)],
        out_specs=[rows(D_MODEL), rows(512), rows(512), rows(D_MODEL), rows(D_MODEL),
                   full(A_WIDTH, D_MODEL), full(B_WIDTH, D_MODEL), full(D_MODEL, D_MODEL),
                   full(1, D_MODEL), full(1, D_MODEL)],
        out_shape=[jax.ShapeDtypeStruct((t, D_MODEL), _ST), jax.ShapeDtypeStruct((t, 512), _ST),
                   jax.ShapeDtypeStruct((t, 512), _ST), jax.ShapeDtypeStruct((t, D_MODEL), _ST),
                   jax.ShapeDtypeStruct((t, D_MODEL), _ST),
                   jax.ShapeDtypeStruct((A_WIDTH, D_MODEL), _F32), jax.ShapeDtypeStruct((B_WIDTH, D_MODEL), _F32),
                   jax.ShapeDtypeStruct((D_MODEL, D_MODEL), _F32), jax.ShapeDtypeStruct((1, D_MODEL), _F32),
                   jax.ShapeDtypeStruct((1, D_MODEL), _F32)],
        scratch_shapes=[pltpu.VMEM((tm, D_MODEL), _F32)] * 4 + [pltpu.VMEM((tm, D_MODEL), _MX)] * 3,
        compiler_params=_cparams(("arbitrary",), VMEM_LIMIT),
    )(oa, ob, ga, gb, x2, tgt, wa, wb, wo, g_final)


def _in_proj_bwd_x(dpieces, wt, x2, dh2, g_in):
    t = x2.shape[0]
    tm = min(t, 512)
    np_ = len(PIECES)

    def body(*refs):
        dp_refs = refs[:np_]
        w_ref, x_ref, dh2_ref, g_ref, gx_ref, dg_ref = refs[np_:]

        @pl.when(pl.program_id(0) == 0)
        def _():
            dg_ref[...] = jnp.zeros_like(dg_ref)

        dh = None
        for (name, a, b), dp in zip(PIECES, dp_refs):
            part = _dot(dp[...], w_ref[a:b, :])
            dh = part if dh is None else dh + part
        xv = x_ref[...]
        r = lax.rsqrt(jnp.mean(xv * xv, axis=-1, keepdims=True) + NORM_EPS)
        nrm = xv * r
        dg_ref[...] = dg_ref[...] + jnp.sum(dh * nrm, axis=0, keepdims=True)
        gx_ref[...] = dh2_ref[...].astype(_F32) + _rms_bwd(dh * g_ref[...], nrm, r)

    rows = lambda w: pl.BlockSpec((tm, w), lambda i: (i, 0))
    full = lambda a, b: pl.BlockSpec((a, b), lambda i: (0, 0), pipeline_mode=pl.Buffered(1))
    return pl.pallas_call(
        body, name="in_proj_bwd_x", grid=(t // tm,),
        in_specs=[rows(b - a) for _, a, b in PIECES] + [full(D_IN, D_MODEL), rows(D_MODEL), rows(D_MODEL),
                                                          full(1, D_MODEL)],
        out_specs=[rows(D_MODEL), full(1, D_MODEL)],
        out_shape=[jax.ShapeDtypeStruct((t, D_MODEL), _F32), jax.ShapeDtypeStruct((1, D_MODEL), _F32)],
        compiler_params=_cparams(("arbitrary",), VMEM_LIMIT),
    )(*dpieces, wt, x2, dh2, g_in)


def _in_proj_bwd_w(h, dpieces, others, osplits):
    t = h.shape[0]
    tm = min(t, 1024)
    nt = t // tm
    np_, no = len(PIECES), len(others)
    half = D_MODEL // 2

    def body(*refs):
        h_ref, dp_refs, o_refs = refs[0], refs[1:1 + np_], refs[1 + np_:1 + np_ + no]
        mine_ref, theirs_ref = refs[1 + np_ + no:3 + np_ + no]
        r_refs = refs[3 + np_ + no:3 + np_ + 2 * no]
        acc_ref, keep_sem, send, recv, o_send, o_recv = refs[3 + np_ + 2 * no:]
        i = pl.program_id(0)
        x, y, c = _place()
        sibling = (x, y, 1 - c)
        early = [pltpu.make_async_remote_copy(
            src_ref=osplits[k].half(o_refs[k], 1 - c), dst_ref=r_refs[k], send_sem=o_send.at[k], recv_sem=o_recv.at[k],
            device_id=sibling, device_id_type=_MESH) for k in range(no)]

        @pl.when(i == 0)
        def _():
            for cp in early:
                cp.start()

        hv = h_ref[...]
        cols = lambda core: pl.ds(pl.multiple_of(core * half, LANE), half)
        writes = []
        by_size = sorted(range(np_), key=lambda j: PIECES[j][1] - PIECES[j][2])
        for j, ((name, a, b), dp) in [(j, (PIECES[j], dp_refs[j])) for j in by_size]:
            part = _dot_tn(dp[...], hv)
            if name == "alr":
                b = a + B_GATE_RANK
                part = part[0:B_GATE_RANK]
            acc_ref[a:b, :] = jnp.where(i == 0, 0.0, acc_ref[a:b, :]) + part
            keep = pltpu.make_async_copy(acc_ref.at[a:b, cols(c)], mine_ref.at[a:b], keep_sem.at[j])
            give = pltpu.make_async_remote_copy(
                src_ref=acc_ref.at[a:b, cols(1 - c)], dst_ref=theirs_ref.at[a:b], send_sem=send.at[j],
                recv_sem=recv.at[j], device_id=sibling, device_id_type=_MESH)
            writes += [keep, give]

            @pl.when(i == nt - 1)
            def _(keep=keep, give=give):
                keep.start()
                give.start()

        @pl.when(i == nt - 1)
        def _():
            for cp in writes + early:
                cp.wait()

    rows = lambda w: pl.BlockSpec((tm, w), lambda i: (i, 0))
    halves = [jax.ShapeDtypeStruct((D_IN, half), _F32)] * 2
    return pl.pallas_call(
        body, name="in_proj_bwd_w", grid=(nt,),
        in_specs=[rows(D_MODEL)] + [rows(b - a) for _, a, b in PIECES] + [_ANY] * no,
        out_specs=[_ANY] * (2 + no),
        out_shape=halves + [jax.ShapeDtypeStruct(sp.half_shape(g.shape), g.dtype) for g, sp in zip(others, osplits)],
        scratch_shapes=[pltpu.VMEM((D_IN, D_MODEL), _F32), pltpu.SemaphoreType.DMA((np_,)),
                        pltpu.SemaphoreType.DMA((np_,)), pltpu.SemaphoreType.DMA((np_,)),
                        pltpu.SemaphoreType.DMA((no,)), pltpu.SemaphoreType.DMA((no,))],
        compiler_params=_cparams(("arbitrary",), VMEM_LIMIT),
    )(h, *dpieces, *others)


def _place():
    return lax.axis_index("x"), lax.axis_index("y"), lax.axis_index("c")


def _other_chips(x, y):
    return [(1 - x, y), (x, 1 - y), (1 - x, 1 - y)]


class _Split(NamedTuple):
    by_rows: bool
    step: int
    size: int

    def half(self, ref, c):
        r, n = ref.shape[-2:]
        if self.by_rows:
            return ref.at[:, pl.ds(pl.multiple_of(c * (n // 2), LANE), n // 2)]
        return ref.at[pl.ds(pl.multiple_of(c * (r // 2), 16), r // 2), :]

    def chip_part(self, ref, k):
        if self.by_rows:
            return ref.at[pl.ds(pl.multiple_of(k * self.step, 16), self.size), :]
        return ref.at[:, pl.ds(pl.multiple_of(k * self.size, LANE), self.size)]

    def half_shape(self, shape):
        r, n = shape
        return (r, n // 2) if self.by_rows else (r // 2, n)

    def part_shape(self, shape):
        r, n = shape
        return (self.size, n) if self.by_rows else (r, self.size)


SPLIT_W_IN_T = _Split(True, WINDOW_STEP, WINDOW_ROWS)
SPLIT_W_O = _Split(True, 256, 256)
SPLIT_W_OUT = _Split(False, 256, 256)


def _gather_weights(shards, splits, fulls, pos_f):
    nw = len(shards)
    t = pos_f.shape[0]

    def body(*refs):
        ins, (pos_ref, c_ref) = refs[:nw], refs[nw:nw + 2]
        outs, tables = refs[nw + 2:2 * nw + 2], refs[2 * nw + 2:2 * nw + 5]
        send_a, recv_a, send_b, recv_b = refs[2 * nw + 5:]
        x, y, c = _place()
        me = 2 * x + y
        peers = _other_chips(x, y)

        def place(i, k, half):
            if splits[i] is None:
                return outs[i].at[k]
            if fulls[i][0] == 4 and len(fulls[i]) == 3:
                whole = outs[i].at[k]
            else:
                whole = splits[i].chip_part(outs[i], k)
            return splits[i].half(whole, half)

        first, passed = [], []
        for i in range(nw):
            src = ins[i] if splits[i] is None else splits[i].half(ins[i], c)
            for j, (px, py) in enumerate(peers):
                cp = pltpu.make_async_remote_copy(
                    src_ref=src, dst_ref=place(i, me, c), send_sem=send_a.at[3 * i + j],
                    recv_sem=recv_a.at[3 * i + j], device_id=(px, py, c), device_id_type=_MESH)
                cp.start()
                first.append(cp)
        _rope_tables_into(pos_ref, c_ref, *tables)
        for i in range(nw):
            for j, (px, py) in enumerate(peers):
                landed = place(i, 2 * px + py, c)
                pltpu.make_async_remote_copy(
                    src_ref=landed, dst_ref=landed, send_sem=send_a.at[3 * i + j], recv_sem=recv_a.at[3 * i + j],
                    device_id=(px, py, c), device_id_type=_MESH).wait_recv()
                if splits[i] is not None:
                    cp = pltpu.make_async_remote_copy(
                        src_ref=landed, dst_ref=landed, send_sem=send_b.at[3 * i + j], recv_sem=recv_b.at[3 * i + j],
                        device_id=(x, y, 1 - c), device_id_type=_MESH)
                    cp.start()
                    passed.append(cp)
        for i in range(nw):
            if splits[i] is None:
                continue
            for j, (px, py) in enumerate(peers):
                theirs = place(i, 2 * px + py, 1 - c)
                pltpu.make_async_remote_copy(
                    src_ref=theirs, dst_ref=theirs, send_sem=send_b.at[3 * i + j], recv_sem=recv_b.at[3 * i + j],
                    device_id=(x, y, 1 - c), device_id_type=_MESH).wait_recv()
        for cp in first + passed:
            cp.wait_send()

    vm = pl.BlockSpec(memory_space=pltpu.VMEM)
    tab = jax.ShapeDtypeStruct((t, LANE), _F32)
    return pl.pallas_call(
        body, name="gather_weights",
        in_specs=[_ANY] * nw + [vm, vm], out_specs=[_ANY] * nw + [vm] * 3,
        out_shape=[jax.ShapeDtypeStruct(f, s.dtype) for f, s in zip(fulls, shards)] + [tab] * 3,
        scratch_shapes=[pltpu.SemaphoreType.DMA((3 * nw,)) for _ in range(4)],
        compiler_params=_cparams(None, VMEM_LIMIT),
    )(*shards, pos_f, _rope_consts())


def _assemble_w_in_t(slots):
    bw = 256
    ov = WINDOW_ROWS - WINDOW_STEP

    def body(s_ref, o_ref):
        for k in range(4):
            base = k * WINDOW_STEP
            lo = 0 if k == 0 else ov
            if k > 0:
                o_ref[base:base + ov, :] = s_ref[k - 1, WINDOW_STEP:WINDOW_ROWS, :] + s_ref[k, 0:ov, :]
            hi = WINDOW_ROWS if k == 3 else WINDOW_STEP
            o_ref[base + lo:base + hi, :] = s_ref[k, lo:hi, :]

    return pl.pallas_call(
        body, name="assemble_w_in_t", grid=(D_MODEL // bw,),
        in_specs=[pl.BlockSpec((4, WINDOW_ROWS, bw), lambda i: (0, 0, i))],
        out_specs=pl.BlockSpec((D_IN, bw), lambda i: (0, i)),
        out_shape=jax.ShapeDtypeStruct((D_IN, D_MODEL), slots.dtype),
        compiler_params=_cparams(("parallel",)),
    )(slots)


def _row_block(rows):
    for cand in (976, 176, 256, 128):
        if rows % cand == 0:
            return cand
    return rows


def _pair_sum(g, r, split, c_arr, name):
    hr, hn = r.shape
    br = _row_block(hr)
    if split is None:
        g_spec = pl.BlockSpec((br, hn), lambda i, c_ref: (i, 0))
    elif split.by_rows:
        g_spec = pl.BlockSpec((br, hn), lambda i, c_ref: (i, c_ref[0]))
    else:
        g_spec = pl.BlockSpec((br, hn), lambda i, c_ref: (c_ref[0] * (hr // br) + i, 0))

    def body(c_ref, g_ref, r_ref, o_ref):
        o_ref[...] = (g_ref[...] + r_ref[...]).astype(o_ref.dtype)

    return pl.pallas_call(
        body, name=name,
        grid_spec=pltpu.PrefetchScalarGridSpec(
            num_scalar_prefetch=1, grid=(hr // br,),
            in_specs=[g_spec, pl.BlockSpec((br, hn), lambda i, c_ref: (i, 0))],
            out_specs=pl.BlockSpec((br, hn), lambda i, c_ref: (i, 0))),
        out_shape=jax.ShapeDtypeStruct(r.shape, _MX),
        compiler_params=_cparams(("parallel",)),
    )(c_arr, g, r)


_HBM = pl.BlockSpec(memory_space=pltpu.HBM)
_SEM = pl.BlockSpec(memory_space=pltpu.SEMAPHORE)
_FLOWS = pltpu.SideEffectType.DATAFLOW_SIDE_EFFECTING


def _chip_exchange_copies(refs, send, recv, splits):
    nw = len(refs) // 2
    x, y, c = _place()
    me = 2 * x + y
    copies = []
    for i in range(nw):
        for px, py in _other_chips(x, y):
            copies.append((splits[i].chip_part(refs[i], 2 * px + py), refs[nw + i].at[me], (px, py, c)))
    return [pltpu.make_async_remote_copy(src_ref=src, dst_ref=dst, send_sem=send.at[k], recv_sem=recv.at[k],
                                         device_id=peer, device_id_type=_MESH)
            for k, (src, dst, peer) in enumerate(copies)]


def _late_gather_copies(refs, send, recv, splits):
    nw = len(refs) // 2
    x, y, c = _place()
    me = 2 * x + y
    copies = []
    for i in range(nw):
        for px, py in [(x, y)] + _other_chips(x, y):
            copies.append((refs[i], splits[i].chip_part(refs[nw + i], me), (px, py, c)))
    return [pltpu.make_async_remote_copy(src_ref=src, dst_ref=dst, send_sem=send.at[k], recv_sem=recv.at[k],
                                         device_id=peer, device_id_type=_MESH)
            for k, (src, dst, peer) in enumerate(copies)]


def _start_copies(name, flying, copies_of, n_copies, after=None):
    first = [] if after is None else [after]

    def body(*refs):
        ins = refs[:len(flying)]
        send, recv = refs[len(flying) + len(first):len(flying) + len(first) + 2]
        token = refs[-1]
        for cp in copies_of(ins, send, recv):
            cp.start()
        token[...] = jnp.zeros_like(token)

    outs = pl.pallas_call(
        body, name=name,
        in_specs=[_HBM] * len(flying) + [_ANY] * len(first),
        out_specs=[_SEM, _SEM] + [_HBM] * len(flying) + [pl.BlockSpec(memory_space=pltpu.VMEM)],
        out_shape=[pltpu.SemaphoreType.DMA((n_copies,)), pltpu.SemaphoreType.DMA((n_copies,))]
        + [pltpu.HBM(f.shape, f.dtype) for f in flying] + [jax.ShapeDtypeStruct((8, LANE), _F32)],
        input_output_aliases={i: 2 + i for i in range(len(flying))},
        compiler_params=pltpu.CompilerParams(has_side_effects=_FLOWS),
    )(*[pltpu.with_memory_space_constraint(f, pltpu.HBM) for f in flying], *first)
    return outs[0], outs[1], outs[2:2 + len(flying)], outs[-1]


def _wait_copies(name, send, recv, flying, copies_of, after):
    def body(*refs):
        ins = refs[:len(flying)]
        send_ref, recv_ref = refs[len(flying):len(flying) + 2]
        for cp in copies_of(ins, send_ref, recv_ref):
            cp.wait_send()
            cp.wait_recv()

    return pl.pallas_call(
        body, name=name,
        in_specs=[_HBM] * len(flying) + [_SEM, _SEM, _ANY],
        out_specs=[_HBM] * len(flying),
        out_shape=[pltpu.HBM(f.shape, f.dtype) for f in flying],
        input_output_aliases={i: i for i in range(len(flying))},
        compiler_params=pltpu.CompilerParams(has_side_effects=_FLOWS),
    )(*flying, send, recv, after)


def _sum_chips(q, p, split, place_arr, name):
    _, hr, hn = q.shape
    if split.by_rows:
        out_shape = (hr, 2 * hn)
        o_spec = pl.BlockSpec((hr, hn), lambda i, pr: (0, pr[0]))
        p_spec = pl.BlockSpec((pl.Element(hr), pl.Element(hn)), lambda i, pr: (pr[1] * split.step, 0))
    else:
        out_shape = (2 * hr, hn)
        o_spec = pl.BlockSpec((hr, hn), lambda i, pr: (pr[0], 0))
        p_spec = pl.BlockSpec((hr, hn), lambda i, pr: (0, pr[1]))

    def body(pr, q_ref, p_ref, o_ref):
        f = lambda k: jnp.where(pr[1] == k, p_ref[...], q_ref[k]).astype(_F32)
        o_ref[...] = ((f(0) + f(1)) + f(2)) + f(3)

    return pl.pallas_call(
        body, name=name,
        grid_spec=pltpu.PrefetchScalarGridSpec(
            num_scalar_prefetch=1, grid=(1,),
            in_specs=[pl.BlockSpec((4, hr, hn), lambda i, pr: (0, 0, 0)), p_spec], out_specs=o_spec),
        out_shape=jax.ShapeDtypeStruct(out_shape, _F32),
        compiler_params=_cparams(("arbitrary",), VMEM_LIMIT),
    )(place_arr, q, p)


def _pair_share(bufs, splits, small):
    nw = len(bufs)

    def body(*refs):
        ins, small_ref, outs, all_ref = refs[:nw], refs[nw], refs[nw + 1:2 * nw + 1], refs[2 * nw + 1]
        send, recv, s_send, s_recv = refs[2 * nw + 2:]
        x, y, c = _place()
        copies = []
        for r in range(1, 8):
            peer = (1 - x if r & 4 else x, 1 - y if r & 2 else y, 1 - c if r & 1 else c)
            cp = pltpu.make_async_remote_copy(
                src_ref=small_ref, dst_ref=all_ref.at[4 * x + 2 * y + c], send_sem=s_send.at[r - 1],
                recv_sem=s_recv.at[r - 1], device_id=peer, device_id_type=_MESH)
            cp.start()
            copies.append(cp)
        for i in range(nw):
            cp = pltpu.make_async_remote_copy(
                src_ref=splits[i].half(ins[i], c), dst_ref=splits[i].half(outs[i], c), send_sem=send.at[i],
                recv_sem=recv.at[i], device_id=(x, y, 1 - c), device_id_type=_MESH)
            cp.start()
            copies.append(cp)
        for cp in copies:
            cp.wait()

    return pl.pallas_call(
        body, name="grad_pair_share",
        in_specs=[_ANY] * (nw + 1), out_specs=[_ANY] * (nw + 1),
        out_shape=[jax.ShapeDtypeStruct(b.shape, b.dtype) for b in bufs]
        + [jax.ShapeDtypeStruct((8,) + small.shape, small.dtype)],
        input_output_aliases={i: i for i in range(nw)},
        scratch_shapes=[pltpu.SemaphoreType.DMA((nw,)), pltpu.SemaphoreType.DMA((nw,)),
                        pltpu.SemaphoreType.DMA((7,)), pltpu.SemaphoreType.DMA((7,))],
    )(*bufs, small)


def _sum_devices(parts, own, dev_arr):
    def body(dev, p_ref, own_ref, tot_ref):
        f = lambda d: jnp.where(dev[0] == d, own_ref[...], p_ref[d])
        acc = f(0)
        for d in range(1, 8):
            acc = acc + f(d)
        tot_ref[...] = acc

    return pl.pallas_call(
        body, name="small_sum",
        grid_spec=pltpu.PrefetchScalarGridSpec(
            num_scalar_prefetch=1, grid=(1,),
            in_specs=[pl.BlockSpec(parts.shape, lambda i, dev: (0, 0, 0)), pl.BlockSpec(own.shape, lambda i, dev: (0, 0))],
            out_specs=pl.BlockSpec(own.shape, lambda i, dev: (0, 0))),
        out_shape=jax.ShapeDtypeStruct(own.shape, own.dtype),
    )(dev_arr, parts, own)


def _adam_update(w, g, m, v):
    m2 = ADAM_B1 * m + (1.0 - ADAM_B1) * g
    v2 = ADAM_B2 * v + (1.0 - ADAM_B2) * (g * g)
    m_hat = m2 / (1.0 - ADAM_B1 ** ADAM_STEP)
    v_hat = v2 / (1.0 - ADAM_B2 ** ADAM_STEP)
    return -ADAM_LR * (m_hat / (jnp.sqrt(v_hat) + ADAM_EPS) + ADAM_WD * w), m2, v2


SMALL_AT = dict(g_in=(0, 0), g_final=(1, 0), g_gla_norm=(2, 0), b_alpha=(2, B_WIDTH), attn_sinks=(2, B_WIDTH + B_KEY_WIDTH))
LOSS_AT = (2, B_WIDTH + B_KEY_WIDTH + LANE)
WUP_ROWS = (3, 7)


def _adamw_small(tot, g_wup, params):
    names = list(params)

    def body(*refs):
        tot_ref, gw_ref = refs[0], refs[1]
        ins = refs[2:2 + 3 * len(names)]
        outs = refs[2 + 3 * len(names):]
        for i, nm in enumerate(names):
            w_ref, m_ref, v_ref = ins[3 * i:3 * i + 3]
            if nm in SMALL_AT:
                r, a = SMALL_AT[nm]
                g = tot_ref[r:r + 1, a:a + w_ref.shape[1]]
            else:
                g = gw_ref[...]
            d, m2, v2 = _adam_update(w_ref[...], g, m_ref[...], v_ref[...])
            for o_ref, val in zip(outs[4 * i:4 * i + 4], (g, d, m2, v2)):
                o_ref[...] = val

    vm = pl.BlockSpec(memory_space=pltpu.VMEM)
    flat = [a for nm in names for a in params[nm]]
    out_shape = [jax.ShapeDtypeStruct(params[nm][0].shape, _F32) for nm in names for _ in range(4)]
    outs = pl.pallas_call(
        body, name="adamw_small", in_specs=[vm] * (2 + len(flat)), out_specs=[vm] * len(out_shape), out_shape=out_shape,
    )(tot, g_wup, *flat)
    return {nm: tuple(outs[4 * i:4 * i + 4]) for i, nm in enumerate(names)}


def _adamw(w, g, m, v, name):
    lead = w.shape[0] != 1
    r, n = (w.shape[0], w.shape[2]) if lead else w.shape[1:]
    br = r
    for cand in (256, 244, 128):
        if r > cand and r % cand == 0:
            br = cand
            break

    def body(w_ref, g_ref, m_ref, v_ref, d_ref, nm_ref, nv_ref):
        d_ref[...], nm_ref[...], nv_ref[...] = _adam_update(w_ref[...], g_ref[...], m_ref[...], v_ref[...])

    blk = pl.BlockSpec((br, 1, n), lambda i: (i, 0, 0)) if lead else pl.BlockSpec((None, br, n), lambda i: (0, i, 0))
    shp = jax.ShapeDtypeStruct(w.shape, _F32)
    return pl.pallas_call(
        body, name=name, grid=(r // br,),
        in_specs=[blk] * 4, out_specs=[blk] * 3, out_shape=[shp] * 3,
        compiler_params=_cparams(("parallel",)),
    )(w, g, m, v)


def kernel(x, positions, g_in, w_in, w_alpha_up, b_alpha, attn_sinks, g_gla_norm, w_out_a, w_out_b, w_o, g_final, loss_target, m_g_in, m_w_in, m_w_alpha_up, m_b_alpha, m_attn_sinks, m_g_gla_norm, m_w_out_a, m_w_out_b, m_w_o, m_g_final, v_g_in, v_w_in, v_w_alpha_up, v_b_alpha, v_attn_sinks, v_g_gla_norm, v_w_out_a, v_w_out_b, v_w_o, v_g_final):
    nseq, seq, _ = x.shape
    t = nseq * seq
    cx, cy, cc = _place()
    chip = 2 * cx + cy
    c_arr = jnp.reshape(cc, (1,)).astype(jnp.int32)

    tr = lambda w: jnp.transpose(w, (2, 0, 1))
    w_in_t = tr(w_in).reshape(SHARD, D_MODEL).astype(_MX)
    pad = WINDOW_ROWS - SHARD
    window = lax.switch(chip, [lambda w, k=k: jnp.pad(w, ((4 * k, pad - 4 * k), (0, 0))) for k in range(4)], w_in_t)
    shards = [window, w_alpha_up[0].astype(_MX)]
    late = [w_out_a[0].astype(_MX), w_out_b[0].astype(_MX), w_o[0].astype(_MX)]
    late_splits = [SPLIT_W_OUT, SPLIT_W_OUT, SPLIT_W_O]
    splits = [SPLIT_W_IN_T, None]
    fulls = [(4, WINDOW_ROWS, D_MODEL), (4, B_GATE_RANK, B_KEY_WIDTH // 4)]
    pos_f = positions.astype(_F32).reshape(t, 1)
    win_g, wup_g, cos, sa, sb = _gather_weights(shards, splits, fulls, pos_f)
    late_copies = lambda refs, send, recv: _late_gather_copies(refs, send, recv, late_splits)
    late_full = [lax.empty(shape, _MX) for shape in ((A_WIDTH, D_MODEL), (B_WIDTH, D_MODEL), (D_MODEL, D_MODEL))]
    l_send, l_recv, l_flying, l_token = _start_copies("late_gather_start", late + late_full, late_copies,
                                                      4 * len(late), after=win_g)
    win_g = lax.dynamic_update_slice(win_g, window[None], (chip, 0, 0))
    wup_g = lax.dynamic_update_slice(wup_g, shards[1][None], (chip, 0, 0))
    wt = _assemble_w_in_t(win_g)
    wup = jnp.concatenate([jnp.transpose(wup_g, (1, 0, 2)).reshape(B_GATE_RANK, B_KEY_WIDTH),
                           jnp.zeros((LANE - B_GATE_RANK, B_KEY_WIDTH), _MX)], axis=0)

    x2 = x.reshape(t, D_MODEL)
    tgt = loss_target.reshape(t, D_MODEL)
    sinks = attn_sinks.reshape(A_HEADS)
    gf = g_final.reshape(1, D_MODEL)

    h, qkv, za, qkb, vb, zb, alr, ga, gb = _in_proj(x2, g_in + l_token[0, 0], wt, cos, sa, sb)
    oa = _attn_fwd(qkv, za, sinks, nseq)
    ob, oraw, sst = _gla_fwd(qkb, vb, zb, alr, wup, b_alpha, g_gla_norm, nseq)

    wa, wb, wo = _wait_copies("late_gather_wait", l_send, l_recv, l_flying, late_copies, ob)[len(late):]
    dh2, doa, dob, dga, dgb, dwa, dwb, dwo, dgf, lossv = _merge_loss(oa, ob, ga, gb, x2, tgt, wa, wb, wo, gf)

    dqkv, dza, dsink = _attn_bwd(qkv, za, doa, sinks, cos, sa, sb, nseq)
    dqkb, dvb, dzb, dalr, dwup, dba, dgn = _gla_bwd(qkb, vb, zb, alr, oraw, dob, sst, wup, b_alpha, g_gla_norm, nseq)
    dpieces = [dqkv, dza, dqkb, dvb, dzb, dalr, dga, dgb]
    gsplits = [SPLIT_W_IN_T, SPLIT_W_OUT, SPLIT_W_OUT, SPLIT_W_O]
    names = ("w_in", "w_out_a", "w_out_b", "w_o")
    dwin_mine, *from_sibling = _in_proj_bwd_w(h, dpieces, [dwa, dwb, dwo], gsplits[1:])
    pair_sums = [_pair_sum(g, r, sp, c_arr, "pair_sum_" + nm)
                 for g, r, sp, nm in zip([dwin_mine, dwa, dwb, dwo], from_sibling, [None] + gsplits[1:], names)]
    exchange = lambda refs, send, recv: _chip_exchange_copies(refs, send, recv, gsplits)
    lands = [lax.empty((4,) + sp.part_shape(p.shape), p.dtype) for p, sp in zip(pair_sums, gsplits)]
    send, recv, flying, token = _start_copies("grad_chip_exchange_start", pair_sums + lands, exchange, 3 * len(lands))
    grad_x2, dgin = _in_proj_bwd_x(dpieces, wt, x2, dh2, g_in + token[0, 0])
    landed = _wait_copies("grad_chip_exchange_wait", send, recv, flying, exchange, grad_x2)
    place_arr = jnp.stack([cc, chip]).astype(jnp.int32)
    reduced = [_sum_chips(q, p, sp, place_arr, "chip_sum_" + nm)
               for q, p, sp, nm in zip(landed[len(lands):], landed[:len(lands)], gsplits, names)]
    row2 = jnp.concatenate([dgn, dba, jnp.pad(dsink[:, 0].reshape(1, A_HEADS), ((0, 0), (0, LANE - A_HEADS))),
                            jnp.pad(jnp.sum(lossv, axis=1, keepdims=True), ((0, 0), (0, LANE - 1)))], axis=1)
    small = jnp.concatenate([dgin, dgf, row2, dwup[:B_GATE_RANK].reshape(WUP_ROWS[1] - WUP_ROWS[0], D_MODEL),
                             jnp.zeros((1, D_MODEL), _F32)], axis=0)
    g_window, g_wa, g_wb, g_wo, small_parts = _pair_share(reduced, gsplits, small)
    g_win_t = lax.switch(chip, [lambda w, k=k: w[4 * k:4 * k + SHARD].reshape(SHARD, 1, D_MODEL) for k in range(4)],
                         g_window)
    dev_arr = jnp.reshape(2 * chip + cc, (1,)).astype(jnp.int32)
    tot = _sum_devices(small_parts, small, dev_arr)
    loss = tot[LOSS_AT]
    nup = B_KEY_WIDTH // 4
    g_wup = lax.dynamic_slice(tot[WUP_ROWS[0]:WUP_ROWS[1]].reshape(B_GATE_RANK, B_KEY_WIDTH), (0, chip * nup),
                              (B_GATE_RANK, nup))

    row = lambda a: a.reshape(1, -1)
    sm = _adamw_small(tot, g_wup, dict(
        g_in=(g_in, m_g_in, v_g_in), g_final=(row(g_final), row(m_g_final), row(v_g_final)),
        g_gla_norm=(g_gla_norm, m_g_gla_norm, v_g_gla_norm), b_alpha=(b_alpha, m_b_alpha, v_b_alpha),
        attn_sinks=(attn_sinks, m_attn_sinks, v_attn_sinks),
        w_alpha_up=(w_alpha_up[0], m_w_alpha_up[0], v_w_alpha_up[0])))
    sm["g_final"] = tuple(a.reshape(D_MODEL) for a in sm["g_final"])
    sm["w_alpha_up"] = tuple(a[None] for a in sm["w_alpha_up"])

    untr = lambda a: jnp.transpose(a, (1, 2, 0))
    big = dict(w_in=tuple(untr(a) for a in (g_win_t,) + tuple(_adamw(tr(w_in), g_win_t, tr(m_w_in), tr(v_w_in), "adamw_w_in"))))
    for nm, w, g, m, v in (("w_out_a", w_out_a, g_wa, m_w_out_a, v_w_out_a),
                           ("w_out_b", w_out_b, g_wb, m_w_out_b, v_w_out_b), ("w_o", w_o, g_wo, m_w_o, v_w_o)):
        big[nm] = (g[None],) + tuple(_adamw(w, g[None], m, v, "adamw_" + nm))

    order = ("g_in", "w_in", "w_alpha_up", "b_alpha", "attn_sinks", "g_gla_norm", "w_out_a", "w_out_b", "w_o", "g_final")
    outs = [big[nm][kind] if nm in big else sm[nm][kind] for kind in range(4) for nm in order]
    return (loss, grad_x2.reshape(x.shape), *outs)
```

```python
import math
from typing import NamedTuple

import numpy as np
import jax
import jax.numpy as jnp
from jax import lax
from jax.experimental import pallas as pl
from jax.experimental.pallas import tpu as pltpu

D_MODEL = 1024
A_HEADS, A_KV_HEADS, A_HEAD_DIM = 8, 2, 64
A_GROUP = A_HEADS // A_KV_HEADS
A_WIDTH, A_KV_WIDTH = 512, 128
BLOCK = 128
ROPE_THETA = 500000.0
ROPE_DIM = 16
B_HEADS, B_KEY_DIM, B_VAL_DIM = 4, 64, 128
B_KEY_WIDTH, B_WIDTH = 256, 512
B_GATE_RANK = 16
B_GATE_TEMP = 16.0
B_CHUNK = 64
NORM_EPS = 1e-6
NEG_BIG = -1e30
D_IN = 4880

ADAM_LR, ADAM_B1, ADAM_B2, ADAM_EPS, ADAM_WD, ADAM_STEP = 0.001, 0.9, 0.999, 1e-08, 0.01, 10

LANE = 128
ALR_AT = 2816
PIECES = (("qkv", 0, 768), ("za", 768, 1280), ("qkb", 1280, 1792), ("vb", 1792, 2304),
          ("zb", 2304, 2816), ("alr", ALR_AT, ALR_AT + LANE), ("ga", 2832, 3856), ("gb", 3856, 4880))
SHARD = D_IN // 4
WINDOW_STEP = 1216
WINDOW_ROWS = 1232

GLA_BLOCK = 256
GLA_FWD_BLOCK = 1024
MERGE_SLAB = 16
VMEM_LIMIT = 56 * 1024 * 1024

_F32 = jnp.float32
_MX = jnp.bfloat16
_ST = jnp.bfloat16

_MESH = pl.DeviceIdType.MESH
_ANY = pl.BlockSpec(memory_space=pl.ANY)


def _cparams(sem=None, vmem=None):
    return pltpu.CompilerParams(dimension_semantics=sem, vmem_limit_bytes=vmem)


def _dot(a, b):
    return jnp.dot(a.astype(_MX), b.astype(_MX), preferred_element_type=_F32)


def _dot_nt(a, b):
    return lax.dot_general(a.astype(_MX), b.astype(_MX), (((1,), (1,)), ((), ())),
                           preferred_element_type=_F32)


def _dot_tn(a, b):
    return lax.dot_general(a.astype(_MX), b.astype(_MX), (((0,), (0,)), ((), ())),
                           preferred_element_type=_F32)


def _dot_ones(ones_mat, v):
    o = ones_mat.astype(jnp.bfloat16)
    v0 = v.astype(jnp.bfloat16)
    v1 = (v - v0.astype(_F32)).astype(jnp.bfloat16)
    d = lambda t: jnp.dot(o, t, preferred_element_type=_F32)
    return d(v0) + d(v1)


def _sigmoid(x):
    return 0.5 * jnp.tanh(0.5 * x) + 0.5


def _log_sigmoid(x):
    return jnp.minimum(x, 0.0) - jnp.log(1.0 + jnp.exp(-jnp.abs(x)))


def _lane_tile(t, width):
    reps = width // t.shape[1]
    return t if reps == 1 else jnp.tile(t, (1, reps))


def _rope(t, cos, sa, sb, sign):
    w = t.shape[1]
    rot = pltpu.roll(t, w - 8, 1) * _lane_tile(sa, w) + pltpu.roll(t, 8, 1) * _lane_tile(sb, w)
    return t * _lane_tile(cos, w) + sign * rot


def _rms_bwd(dy_g, n, r):
    return r * (dy_g - n * jnp.mean(dy_g * n, axis=-1, keepdims=True))


ROPE_ROWS = 256


def _rope_consts():
    lane = np.arange(LANE) % A_HEAD_DIM
    half = ROPE_DIM // 2
    inv = np.exp((np.float32(-math.log(ROPE_THETA)) * np.arange(half, dtype=np.float32)) * np.float32(2.0 / ROPE_DIM))
    consts = np.zeros((8, LANE), np.float32)
    consts[0] = np.where(lane < ROPE_DIM, inv[lane % half], 0.0)
    consts[1] = np.where(lane < half, -1.0, 0.0)
    consts[2] = np.where((lane >= half) & (lane < ROPE_DIM), 1.0, 0.0)
    return jnp.asarray(consts)


def _rope_tables_into(pos_ref, c_ref, cos_ref, sa_ref, sb_ref):
    def rows_of(b, carry):
        rows = pl.ds(pl.multiple_of(b * ROPE_ROWS, ROPE_ROWS), ROPE_ROWS)
        ang = pos_ref[rows, :] * c_ref[0:1, :]
        s = jnp.sin(ang)
        cos_ref[rows, :] = jnp.cos(ang)
        sa_ref[rows, :] = s * c_ref[1:2, :]
        sb_ref[rows, :] = s * c_ref[2:3, :]
        return carry

    lax.fori_loop(0, pos_ref.shape[0] // ROPE_ROWS, rows_of, 0)


def _in_proj(x2, g_in, wt, cos, sa, sb):
    t = x2.shape[0]
    tm = min(t, 512)

    def body(x_ref, g_ref, w_ref, cos_ref, sa_ref, sb_ref, h_ref, qkv_ref, za_ref, qkb_ref,
             vb_ref, zb_ref, alr_ref, ga_ref, gb_ref):
        xv = x_ref[...]
        r = lax.rsqrt(jnp.mean(xv * xv, axis=-1, keepdims=True) + NORM_EPS)
        h = (xv * r * g_ref[...]).astype(_MX)
        h_ref[...] = h.astype(_ST)
        outs = dict(za=za_ref, qkb=qkb_ref, vb=vb_ref, zb=zb_ref, alr=alr_ref, ga=ga_ref, gb=gb_ref)
        for name, a, b in PIECES:
            p = _dot_nt(h, w_ref[a:b, :])
            if name == "qkv":
                c, s1, s2 = cos_ref[...], sa_ref[...], sb_ref[...]
                qkv_ref[:, 0:512] = _rope(p[:, 0:512], c, s1, s2, 1.0).astype(_ST)
                qkv_ref[:, 512:640] = _rope(p[:, 512:640], c, s1, s2, 1.0).astype(_ST)
                qkv_ref[:, 640:768] = p[:, 640:768].astype(_ST)
            else:
                outs[name][...] = p.astype(outs[name].dtype)

    rows = lambda w: pl.BlockSpec((tm, w), lambda i: (i, 0))
    shp = lambda name, w: jax.ShapeDtypeStruct((t, w), _F32 if name == "qkb" else _ST)
    widths = [D_MODEL] + [b - a for _, a, b in PIECES]
    return pl.pallas_call(
        body, name="in_proj", grid=(t // tm,),
        in_specs=[rows(D_MODEL), pl.BlockSpec((1, D_MODEL), lambda i: (0, 0)),
                  pl.BlockSpec((D_IN, D_MODEL), lambda i: (0, 0), pipeline_mode=pl.Buffered(1)),
                  rows(LANE), rows(LANE), rows(LANE)],
        out_specs=[rows(w) for w in widths],
        out_shape=[shp(n, w) for n, w in zip(["h"] + [p[0] for p in PIECES], widths)],
        compiler_params=_cparams(("parallel",), VMEM_LIMIT),
    )(x2, g_in, wt, cos, sa, sb)


def _attn_operands(k_prev, k_cur, v_prev, v_cur, want_bwd):
    kf = jnp.concatenate([k_prev, k_cur], axis=0).astype(_F32) * (A_HEAD_DIM ** -0.5)
    vf = jnp.concatenate([v_prev, v_cur], axis=0).astype(_F32)
    lo = lax.broadcasted_iota(jnp.int32, (1, LANE), 1) < 64

    def on_lanes(a):
        sw = pltpu.roll(a, 64, 1)
        z = jnp.zeros_like(a)
        return [[jnp.where(lo, a, z).astype(_MX), jnp.where(lo, z, sw).astype(_MX)],
                [jnp.where(lo, sw, z).astype(_MX), jnp.where(lo, z, a).astype(_MX)]]

    def on_rows(a):
        at = a.T.astype(_MX)
        z = jnp.zeros((64, at.shape[1]), _MX)
        top, bot = at[0:64], at[64:128]
        return [[jnp.concatenate([top, z], axis=0), jnp.concatenate([z, top], axis=0)],
                [jnp.concatenate([bot, z], axis=0), jnp.concatenate([z, bot], axis=0)]]

    ops = dict(k_lanes=on_lanes(kf), v_rows=on_rows(vf), lo=lo)
    if want_bwd:
        ops.update(v_lanes=on_lanes(vf), k_rows=on_rows(kf))
    return ops


def _attn_valid(n):
    kj = lax.broadcasted_iota(jnp.int32, (2 * BLOCK, 2 * BLOCK), 0) - BLOCK
    qi = lax.broadcasted_iota(jnp.int32, (2 * BLOCK, 2 * BLOCK), 1) & (BLOCK - 1)
    return (kj <= qi) & (qi - kj < BLOCK) & ((n > 0) | (kj >= 0))


def _attn_sinks(sink_ref, h_a, h_b):
    first = lax.broadcasted_iota(jnp.int32, (1, 2 * BLOCK), 1) < BLOCK
    return jnp.where(first, sink_ref[h_a], sink_ref[h_b])


def _attn_softmax_t(k_lanes, q_pair, valid, sink):
    s = jnp.where(valid, _dot_nt(k_lanes, q_pair), NEG_BIG)
    m = jnp.maximum(jnp.max(s, axis=0, keepdims=True), sink)
    e = jnp.exp(s - m)
    e_sink = jnp.exp(sink - m)
    inv = 1.0 / (jnp.sum(e, axis=0, keepdims=True) + e_sink)
    return e, e_sink, inv


ATTN_TILE = 8


def _attn_kv(qkv_ref, kvp_ref, j):
    rows = slice(j * BLOCK, (j + 1) * BLOCK)
    if j == 0:
        k_prev, v_prev = kvp_ref[:, 0:128], kvp_ref[:, 128:256]
    else:
        before = slice((j - 1) * BLOCK, j * BLOCK)
        k_prev, v_prev = qkv_ref[before, 512:640], qkv_ref[before, 640:768]
    return k_prev, qkv_ref[rows, 512:640], v_prev, qkv_ref[rows, 640:768]


def _attn_fwd(qkv, za, sinks, nseq):
    t = qkv.shape[0]
    nblk = min(ATTN_TILE, t // nseq // BLOCK)
    tile = nblk * BLOCK
    nt = t // nseq // tile

    def body(sink_ref, qkv_ref, kvp_ref, za_ref, oa_ref):
        for j in range(nblk):
            rows = slice(j * BLOCK, (j + 1) * BLOCK)
            ops = _attn_operands(*_attn_kv(qkv_ref, kvp_ref, j), False)
            valid = _attn_valid(nblk * pl.program_id(1) + j)[:, 0:BLOCK]
            for pr in range(A_HEADS // 2):
                lanes = slice(pr * LANE, (pr + 1) * LANE)
                g = pr // (A_GROUP // 2)
                q_pair = qkv_ref[rows, lanes]
                ot = None
                for half in range(2):
                    e, _, inv = _attn_softmax_t(ops["k_lanes"][g][half], q_pair, valid, sink_ref[2 * pr + half])
                    part = _dot(ops["v_rows"][g][half], e) * inv
                    ot = part if ot is None else ot + part
                z = za_ref[rows, lanes].astype(_F32)
                oa_ref[rows, lanes] = (ot.T * (z * _sigmoid(z))).astype(_ST)

    cur = lambda w: pl.BlockSpec((tile, w), lambda s, n: (s * nt + n, 0))
    return pl.pallas_call(
        body, name="attn_fwd", grid=(nseq, nt),
        in_specs=[pl.BlockSpec(memory_space=pltpu.SMEM), cur(768),
                  pl.BlockSpec((BLOCK, 256), lambda s, n: (nblk * (s * nt + n) - jnp.minimum(n, 1), 2)),
                  cur(512)],
        out_specs=cur(512), out_shape=jax.ShapeDtypeStruct((t, A_WIDTH), _ST),
        compiler_params=_cparams(("parallel", "arbitrary")),
    )(sinks, qkv, qkv, za)


def _attn_bwd(qkv, za, doa, sinks, cos, sa, sb, nseq):
    t = qkv.shape[0]
    nblk = min(ATTN_TILE, t // nseq // BLOCK)
    tile = nblk * BLOCK
    nt = t // nseq // tile

    def body(sink_ref, qkv_ref, kvp_ref, za_ref, doa_ref, cos_ref, sa_ref, sb_ref,
             dqkv_ref, dza_ref, dsink_ref, ck_ref, cv_ref):
        s_id, i = pl.program_id(0), pl.program_id(1)

        @pl.when((s_id == 0) & (i == 0))
        def _():
            dsink_ref[...] = jnp.zeros_like(dsink_ref)

        @pl.when(i == 0)
        def _():
            ck_ref[...] = jnp.zeros_like(ck_ref)
            cv_ref[...] = jnp.zeros_like(cv_ref)

        carry_k, carry_v = ck_ref[...], cv_ref[...]
        for j in reversed(range(nblk)):
            rows = slice(j * BLOCK, (j + 1) * BLOCK)
            ops = _attn_operands(*_attn_kv(qkv_ref, kvp_ref, j), True)
            lo = ops["lo"]
            valid = _attn_valid(nblk * (nt - 1 - i) + j)
            dk_acc, dv_acc, dq_pairs = [], [], []
            for g in range(A_KV_HEADS):
                pairs = [slice((2 * g + p) * LANE, (2 * g + p + 1) * LANE) for p in range(2)]
                q_both = jnp.concatenate([qkv_ref[rows, p] for p in pairs], axis=0)
                q_f = q_both.astype(_F32)
                z = [za_ref[rows, p].astype(_F32) for p in pairs]
                sz = [_sigmoid(v) for v in z]
                d_oa = [doa_ref[rows, p].astype(_F32) for p in pairs]
                d_att = jnp.concatenate([d_oa[p] * (z[p] * sz[p]) for p in range(2)], axis=0)
                zero = jnp.zeros_like(d_att)
                ot, dqt, ds_all, pn_all, qz_all, daz_all = None, None, [], [], [], []
                for half in range(2):
                    heads = (4 * g + half, 4 * g + 2 + half)
                    e, e_sink, inv = _attn_softmax_t(ops["k_lanes"][g][half], q_both, valid,
                                                     _attn_sinks(sink_ref, *heads))
                    pn = e * inv
                    dpt = _dot_nt(ops["v_lanes"][g][half], d_att)
                    delta = jnp.sum(pn * dpt, axis=0, keepdims=True)
                    ds = (pn * (dpt - delta)).astype(_MX)
                    pn = pn.astype(_MX)
                    d_sink = e_sink * inv * delta
                    for p, h in enumerate(heads):
                        dsink_ref[h:h + 1, :] = dsink_ref[h:h + 1, :] - jnp.sum(d_sink[:, p * BLOCK:(p + 1) * BLOCK])
                    o_part = _dot(ops["v_rows"][g][half], pn)
                    dq_part = _dot(ops["k_rows"][g][half], ds)
                    ot = o_part if ot is None else ot + o_part
                    dqt = dq_part if dqt is None else dqt + dq_part
                    mine = lo if half == 0 else jnp.logical_not(lo)
                    ds_all.append(ds)
                    pn_all.append(pn)
                    qz_all.append(jnp.where(mine, q_f, zero).astype(_MX))
                    daz_all.append(jnp.where(mine, d_att, zero).astype(_MX))
                dk_acc.append(_dot(jnp.concatenate(ds_all, axis=1), jnp.concatenate(qz_all, axis=0)))
                dv_acc.append(_dot(jnp.concatenate(pn_all, axis=1), jnp.concatenate(daz_all, axis=0)))
                for p, lanes in enumerate(pairs):
                    cols = slice(p * BLOCK, (p + 1) * BLOCK)
                    dza_ref[rows, lanes] = (d_oa[p] * ot[:, cols].T * (sz[p] * (1.0 + z[p] * (1.0 - sz[p])))).astype(_ST)
                    dq_pairs.append(dqt[:, cols].T)

            def fold(acc, scale):
                both = [a + pltpu.roll(a, 64, 1) for a in acc]
                return jnp.where(lo, both[0], both[1]) * scale

            dk_full = fold(dk_acc, A_HEAD_DIM ** -0.5)
            dv_full = fold(dv_acc, 1.0)
            dk_cur, dv_cur = dk_full[BLOCK:] + carry_k, dv_full[BLOCK:] + carry_v
            carry_k, carry_v = dk_full[:BLOCK], dv_full[:BLOCK]
            c, s1, s2 = cos_ref[rows, :], sa_ref[rows, :], sb_ref[rows, :]
            dqkv_ref[rows, 0:512] = _rope(jnp.concatenate(dq_pairs, axis=1), c, s1, s2, -1.0).astype(_ST)
            dqkv_ref[rows, 512:640] = _rope(dk_cur, c, s1, s2, -1.0).astype(_ST)
            dqkv_ref[rows, 640:768] = dv_cur.astype(_ST)
        ck_ref[...] = carry_k
        cv_ref[...] = carry_v

    cur = lambda w: pl.BlockSpec((tile, w), lambda s, i: (s * nt + nt - 1 - i, 0))
    return pl.pallas_call(
        body, name="attn_bwd", grid=(nseq, nt),
        in_specs=[pl.BlockSpec(memory_space=pltpu.SMEM), cur(768),
                  pl.BlockSpec((BLOCK, 256),
                               lambda s, i: (nblk * (s * nt + nt - 1 - i) - jnp.minimum(nt - 1 - i, 1), 2)),
                  cur(512), cur(512), cur(LANE), cur(LANE), cur(LANE)],
        out_specs=[cur(768), cur(512), pl.BlockSpec((8, LANE), lambda s, i: (0, 0))],
        out_shape=[jax.ShapeDtypeStruct((t, 768), _ST), jax.ShapeDtypeStruct((t, 512), _ST),
                   jax.ShapeDtypeStruct((8, LANE), _F32)],
        scratch_shapes=[pltpu.VMEM((BLOCK, A_KV_WIDTH), _F32), pltpu.VMEM((BLOCK, A_KV_WIDTH), _F32)],
        compiler_params=_cparams(("arbitrary", "arbitrary")),
    )(sinks, qkv, qkv, za, doa, cos, sa, sb)


def _gla_chunk_terms(la, qkb_ref, r0):
    g = la[r0:r0 + B_CHUNK, :]
    ri = lax.broadcasted_iota(jnp.int32, (B_CHUNK, B_CHUNK), 0)
    ci = lax.broadcasted_iota(jnp.int32, (B_CHUNK, B_CHUNK), 1)
    cum = _dot_ones((ri >= ci).astype(_F32), g)
    last = cum[B_CHUNK - 1:B_CHUNK, :]
    mid = cum[B_CHUNK // 2 - 1:B_CHUNK // 2, :]
    q = qkb_ref[r0:r0 + B_CHUNK, 0:B_KEY_WIDTH].astype(_F32) * (B_KEY_DIM ** -0.5)
    k = qkb_ref[r0:r0 + B_CHUNK, B_KEY_WIDTH:2 * B_KEY_WIDTH].astype(_F32)
    e_q, e_k, e_l, e_c = jnp.exp(cum - mid), jnp.exp(mid - cum), jnp.exp(last - cum), jnp.exp(cum)
    dec_col = jnp.exp(jnp.sum(g.T, axis=1, keepdims=True))
    return dict(qm=q * e_q, km=k * e_k, kl=k * e_l, qc=q * e_c, e_q=e_q, e_k=e_k, e_l=e_l, e_c=e_c,
                dec_col=dec_col, dec_row=jnp.exp(last), causal=ri >= ci, ri=ri)


def _gate_logits(alr_ref, wup_ref, b_ref):
    return _dot(alr_ref[...], wup_ref[...]) + b_ref[...]


def _gla_fwd(qkb, vb, zb, alr, wup, b_alpha, gn, nseq):
    t = qkb.shape[0]
    tb = min(GLA_FWD_BLOCK, t // nseq)
    nblk = t // nseq // tb
    cpb = tb // B_CHUNK

    def body(qkb_ref, vb_ref, zb_ref, alr_ref, wup_ref, b_ref, gn_ref, ob_ref, oraw_ref, sst_ref, s_ref):
        @pl.when(pl.program_id(1) == 0)
        def _():
            s_ref[...] = jnp.zeros_like(s_ref)

        la = _log_sigmoid(_gate_logits(alr_ref, wup_ref, b_ref)) * (1.0 / B_GATE_TEMP)
        terms = [_gla_chunk_terms(la, qkb_ref, c * B_CHUNK) for c in range(cpb)]
        o_intra, inc = {}, {}
        for c, tm in enumerate(terms):
            for h in range(B_HEADS):
                kl_, vl_ = slice(h * 64, (h + 1) * 64), slice(h * 128, (h + 1) * 128)
                v = vb_ref[c * B_CHUNK:(c + 1) * B_CHUNK, vl_]
                a = jnp.where(tm["causal"], _dot_nt(tm["qm"][:, kl_], tm["km"][:, kl_]), 0.0)
                o_intra[c, h] = _dot(a, v)
                inc[c, h] = _dot_tn(tm["kl"][:, kl_], v)
        o_heads = {}
        for h in range(B_HEADS):
            kl_ = slice(h * 64, (h + 1) * 64)
            st = s_ref[kl_, :]
            for c, tm in enumerate(terms):
                sst_ref[c, kl_, :] = st
                o_heads[c, h] = o_intra[c, h] + _dot(tm["qc"][:, kl_], st)
                st = tm["dec_col"][kl_, :] * st + inc[c, h]
            s_ref[kl_, :] = st
        o = jnp.concatenate([jnp.concatenate([o_heads[c, h] for h in range(B_HEADS)], axis=1)
                             for c in range(cpb)], axis=0)
        oraw_ref[...] = o
        z = zb_ref[...].astype(_F32)
        gate = z * _sigmoid(z)
        for h in range(B_HEADS):
            vl_ = slice(h * 128, (h + 1) * 128)
            oh = o[:, vl_]
            r = lax.rsqrt(jnp.mean(oh * oh, axis=-1, keepdims=True) + NORM_EPS)
            ob_ref[:, vl_] = ((oh * r) * gn_ref[:, vl_] * gate[:, vl_]).astype(_ST)

    rows = lambda w: pl.BlockSpec((tb, w), lambda s, i: (s * nblk + i, 0))
    full = lambda a, b: pl.BlockSpec((a, b), lambda s, i: (0, 0))
    return pl.pallas_call(
        body, name="gla_fwd", grid=(nseq, nblk),
        in_specs=[rows(512), rows(512), rows(512), rows(LANE), full(LANE, B_KEY_WIDTH),
                  full(1, B_KEY_WIDTH), full(1, B_WIDTH)],
        out_specs=[rows(512), rows(512),
                   pl.BlockSpec((cpb, B_KEY_WIDTH, B_VAL_DIM), lambda s, i: (s * nblk + i, 0, 0))],
        out_shape=[jax.ShapeDtypeStruct((t, B_WIDTH), _ST), jax.ShapeDtypeStruct((t, B_WIDTH), _F32),
                   jax.ShapeDtypeStruct((t // B_CHUNK, B_KEY_WIDTH, B_VAL_DIM), _F32)],
        scratch_shapes=[pltpu.VMEM((B_KEY_WIDTH, B_VAL_DIM), _F32)],
        compiler_params=_cparams(("parallel", "arbitrary")),
    )(qkb, vb, zb, alr, wup, b_alpha, gn)


def _gla_bwd(qkb, vb, zb, alr, oraw, dob, sst, wup, b_alpha, gn, nseq):
    t = qkb.shape[0]
    tb = min(GLA_BLOCK, t // nseq)
    nblk = t // nseq // tb
    cpb = tb // B_CHUNK

    def body(qkb_ref, vb_ref, zb_ref, alr_ref, oraw_ref, dob_ref, sst_ref, wup_ref, b_ref, gn_ref,
             dqkb_ref, dvb_ref, dzb_ref, dalr_ref, dwup_ref, db_ref, dgn_ref, ds_ref):
        s_id, i = pl.program_id(0), pl.program_id(1)

        @pl.when((s_id == 0) & (i == 0))
        def _():
            dwup_ref[...] = jnp.zeros_like(dwup_ref)
            db_ref[...] = jnp.zeros_like(db_ref)
            dgn_ref[...] = jnp.zeros_like(dgn_ref)

        @pl.when(i == 0)
        def _():
            ds_ref[...] = jnp.zeros_like(ds_ref)

        a_pre = _gate_logits(alr_ref, wup_ref, b_ref)
        la = _log_sigmoid(a_pre) * (1.0 / B_GATE_TEMP)

        z = zb_ref[...].astype(_F32)
        sz = _sigmoid(z)
        d_ob = dob_ref[...].astype(_F32)
        tg = d_ob * (z * sz)
        dsilu = sz * (1.0 + z * (1.0 - sz))
        do_cols, dgn_cols = [], []
        for h in range(B_HEADS):
            vl_ = slice(h * 128, (h + 1) * 128)
            oh = oraw_ref[:, vl_].astype(_F32)
            r = lax.rsqrt(jnp.mean(oh * oh, axis=-1, keepdims=True) + NORM_EPS)
            on = oh * r
            gnh = gn_ref[:, vl_]
            dzb_ref[:, vl_] = (d_ob[:, vl_] * (on * gnh) * dsilu[:, vl_]).astype(_ST)
            dgn_cols.append(jnp.sum(tg[:, vl_] * on, axis=0, keepdims=True))
            do_cols.append(_rms_bwd(tg[:, vl_] * gnh, on, r))
        dgn_ref[...] = dgn_ref[...] + jnp.concatenate(dgn_cols, axis=1)
        d_o = jnp.concatenate(do_cols, axis=1)

        ri = lax.broadcasted_iota(jnp.int32, (tb, tb), 0)
        ci = lax.broadcasted_iota(jnp.int32, (tb, tb), 1)
        same = (ri // B_CHUNK) == (ci // B_CHUNK)
        low = same & (ri >= ci)
        cum = _dot_ones(low.astype(_F32), la)
        at_row = lambda r: jnp.concatenate([jnp.broadcast_to(cum[c * B_CHUNK + r:c * B_CHUNK + r + 1], (B_CHUNK, B_KEY_WIDTH))
                                            for c in range(cpb)], axis=0)
        last, mid = at_row(B_CHUNK - 1), at_row(B_CHUNK // 2 - 1)
        e_q, e_k, e_l, e_c = jnp.exp(cum - mid), jnp.exp(mid - cum), jnp.exp(last - cum), jnp.exp(cum)
        q = qkb_ref[:, 0:B_KEY_WIDTH] * (B_KEY_DIM ** -0.5)
        k = qkb_ref[:, B_KEY_WIDTH:2 * B_KEY_WIDTH]
        qm, km, kl, qc = q * e_q, k * e_k, k * e_l, q * e_c
        lane_head = lax.broadcasted_iota(jnp.int32, (1, B_KEY_WIDTH), 1) // B_KEY_DIM
        d_o_mx = d_o.astype(_MX)

        def on_diagonal(st):
            z = jnp.zeros((B_KEY_DIM, B_VAL_DIM), st.dtype)
            return jnp.concatenate([jnp.concatenate(
                [st[h * B_KEY_DIM:(h + 1) * B_KEY_DIM] if g == h else z for g in range(B_HEADS)], axis=1)
                for h in range(B_HEADS)], axis=0)

        def diagonal_of(full):
            return jnp.concatenate([full[h * B_KEY_DIM:(h + 1) * B_KEY_DIM, h * B_VAL_DIM:(h + 1) * B_VAL_DIM]
                                    for h in range(B_HEADS)], axis=0)

        dqm, dkm, dv_cols = None, None, []
        for h in range(B_HEADS):
            vl_ = slice(h * B_VAL_DIM, (h + 1) * B_VAL_DIM)
            mine = lane_head == h
            qz, kz = jnp.where(mine, qm, 0.0).astype(_MX), jnp.where(mine, km, 0.0).astype(_MX)
            a = jnp.where(low, _dot_nt(qz, kz), 0.0).astype(_MX)
            da = jnp.where(low, _dot_nt(d_o_mx[:, vl_], vb_ref[:, vl_]), 0.0).astype(_MX)
            dqm_h, dkm_h = _dot(da, kz), _dot_tn(da, qz)
            dqm = dqm_h if dqm is None else dqm + dqm_h
            dkm = dkm_h if dkm is None else dkm + dkm_h
            dv_cols.append(_dot_tn(a, d_o_mx[:, vl_]))
        dv = jnp.concatenate(dv_cols, axis=1)

        chunk = [slice(c * B_CHUNK, (c + 1) * B_CHUNK) for c in range(cpb)]
        dqc_rows, g_loc = [], []
        for c in range(cpb):
            dqc_rows.append(_dot_nt(d_o_mx[chunk[c]], on_diagonal(sst_ref[c].astype(_MX))))
            g_loc.append(diagonal_of(_dot_tn(qc[chunk[c]], d_o_mx[chunk[c]])))
        cur = ds_ref[...]
        d_state = [None] * cpb
        for c in reversed(range(cpb)):
            d_state[c] = cur
            cur = g_loc[c] + jnp.exp(jnp.sum(la[chunk[c]].T, axis=1, keepdims=True)) * cur
        ds_ref[...] = cur
        dkl_rows, dv_rows, dlast_rows = [], [], []
        ones8 = jnp.ones((8, B_VAL_DIM), _F32)
        for c in range(cpb):
            dsd = on_diagonal(d_state[c].astype(_MX))
            dkl_c = _dot_nt(vb_ref[chunk[c], :], dsd)
            dkl_rows.append(dkl_c)
            dv_rows.append(_dot(kl[chunk[c]], dsd))
            prod = d_state[c] * sst_ref[c]
            p0 = prod.astype(jnp.bfloat16)
            p1 = (prod - p0.astype(_F32)).astype(jnp.bfloat16)
            ddec = (_dot_nt(ones8, p0) + _dot_nt(ones8, p1))[0:1]
            r_last = c * B_CHUNK + B_CHUNK - 1
            dlast = jnp.sum(dkl_c * kl[chunk[c]], axis=0, keepdims=True) + ddec * jnp.exp(last[r_last:r_last + 1])
            dlast_rows.append(jnp.broadcast_to(dlast, (B_CHUNK, B_KEY_WIDTH)))
        dqc, dkl = jnp.concatenate(dqc_rows, axis=0), jnp.concatenate(dkl_rows, axis=0)
        dqkb_ref[:, 0:B_KEY_WIDTH] = ((dqm * e_q + dqc * e_c) * (B_KEY_DIM ** -0.5)).astype(_ST)
        dqkb_ref[:, B_KEY_WIDTH:2 * B_KEY_WIDTH] = (dkm * e_k + dkl * e_l).astype(_ST)
        dvb_ref[...] = (dv + jnp.concatenate(dv_rows, axis=0)).astype(_ST)
        dcum = dqm * qm - dkm * km + dqc * qc - dkl * kl
        row = lax.broadcasted_iota(jnp.int32, (tb, B_KEY_WIDTH), 0)
        dcum = jnp.where(row % B_CHUNK == B_CHUNK - 1, dcum + jnp.concatenate(dlast_rows, axis=0), dcum)
        dla = _dot_ones((same & (ri <= ci)).astype(_F32), dcum)

        da_pre = dla * (1.0 / B_GATE_TEMP) * (1.0 - _sigmoid(a_pre))
        dalr_ref[...] = _dot_nt(da_pre, wup_ref[...]).astype(_ST)
        dwup_ref[...] = dwup_ref[...] + _dot_tn(alr_ref[...], da_pre)
        db_ref[...] = db_ref[...] + jnp.sum(da_pre, axis=0, keepdims=True)

    blk = lambda s, i: s * nblk + nblk - 1 - i
    rows = lambda w: pl.BlockSpec((tb, w), lambda s, i: (blk(s, i), 0))
    full = lambda a, b: pl.BlockSpec((a, b), lambda s, i: (0, 0))
    act = lambda w: jax.ShapeDtypeStruct((t, w), _ST)
    return pl.pallas_call(
        body, name="gla_bwd", grid=(nseq, nblk),
        in_specs=[rows(512), rows(512), rows(512), rows(LANE), rows(512), rows(512),
                  pl.BlockSpec((cpb, B_KEY_WIDTH, B_VAL_DIM), lambda s, i: (blk(s, i), 0, 0)),
                  full(LANE, B_KEY_WIDTH), full(1, B_KEY_WIDTH), full(1, B_WIDTH)],
        out_specs=[rows(512), rows(512), rows(512), rows(LANE), full(LANE, B_KEY_WIDTH),
                   full(1, B_KEY_WIDTH), full(1, B_WIDTH)],
        out_shape=[act(512), act(512), act(512), act(LANE),
                   jax.ShapeDtypeStruct((LANE, B_KEY_WIDTH), _F32),
                   jax.ShapeDtypeStruct((1, B_KEY_WIDTH), _F32), jax.ShapeDtypeStruct((1, B_WIDTH), _F32)],
        scratch_shapes=[pltpu.VMEM((B_KEY_WIDTH, B_VAL_DIM), _F32)],
        compiler_params=_cparams(("arbitrary", "arbitrary")),
    )(qkb, vb, zb, alr, oraw, dob, sst, wup, b_alpha, gn)


def _merge_loss(oa, ob, ga, gb, x2, tgt, wa, wb, wo, g_final):
    t = x2.shape[0]
    tm = min(t, 512)
    nt = t // tm

    def body(oa_ref, ob_ref, ga_ref, gb_ref, x_ref, t_ref, wa_ref, wb_ref, wo_ref, gf_ref,
             dh_ref, doa_ref, dob_ref, dga_ref, dgb_ref, dwa_ref, dwb_ref, dwo_ref, dgf_ref, loss_ref,
             ya_s, yb_s, out_s, dmer_s, mrg_s, dya_s, dyb_s):
        first = pl.program_id(0) == 0
        so_far = lambda ref: jnp.where(first, 0.0, ref[...])

        slabs = [slice(s, s + MERGE_SLAB) for s in range(0, tm, MERGE_SLAB)]
        fold = lambda a: a[0:8] + a[8:16]
        ya_s[...] = _dot(oa_ref[...], wa_ref[...])
        yb_s[...] = _dot(ob_ref[...], wb_ref[...])
        for rows_ in slabs:
            sga, sgb = _sigmoid(ga_ref[rows_, :].astype(_F32)), _sigmoid(gb_ref[rows_, :].astype(_F32))
            mrg_s[rows_, :] = (sga * ya_s[rows_, :] + sgb * yb_s[rows_, :]).astype(_MX)
        out_s[...] = x_ref[...] + _dot(mrg_s[...], wo_ref[...])
        gf = gf_ref[...]
        loss8 = jnp.zeros((8, D_MODEL), _F32)
        dgf8 = jnp.zeros((8, D_MODEL), _F32)
        for rows_ in slabs:
            out = out_s[rows_, :]
            r = lax.rsqrt(jnp.mean(out * out, axis=-1, keepdims=True) + NORM_EPS)
            nrm = out * r
            err = nrm * gf - t_ref[rows_, :]
            loss8 = loss8 + fold(err * err)
            dy = err * (1.0 / D_MODEL)
            dgf8 = dgf8 + fold(dy * nrm)
            dh = _rms_bwd(dy * gf, nrm, r)
            dh_ref[rows_, :] = dh.astype(_ST)
        loss_ref[...] = so_far(loss_ref) + (0.5 / D_MODEL) * jnp.sum(loss8, axis=0, keepdims=True)
        dgf_ref[...] = so_far(dgf_ref) + jnp.sum(dgf8, axis=0, keepdims=True)
        dmer_s[...] = _dot_nt(dh_ref[...], wo_ref[...])
        dwo_ref[...] = so_far(dwo_ref) + _dot_tn(mrg_s[...], dh_ref[...])
        for rows_ in slabs:
            sga, sgb = _sigmoid(ga_ref[rows_, :].astype(_F32)), _sigmoid(gb_ref[rows_, :].astype(_F32))
            dmer = dmer_s[rows_, :]
            da, db = dmer * sga, dmer * sgb
            dya_s[rows_, :] = da.astype(_MX)
            dyb_s[rows_, :] = db.astype(_MX)
            dga_ref[rows_, :] = (da * ya_s[rows_, :] * (1.0 - sga)).astype(_ST)
            dgb_ref[rows_, :] = (db * yb_s[rows_, :] * (1.0 - sgb)).astype(_ST)
        doa_ref[...] = _dot_nt(dya_s[...], wa_ref[...]).astype(_ST)
        dob_ref[...] = _dot_nt(dyb_s[...], wb_ref[...]).astype(_ST)
        dwa_ref[...] = so_far(dwa_ref) + _dot_tn(oa_ref[...], dya_s[...])
        dwb_ref[...] = so_far(dwb_ref) + _dot_tn(ob_ref[...], dyb_s[...])

    rows = lambda w: pl.BlockSpec((tm, w), lambda i: (i, 0))
    full = lambda a, b: pl.BlockSpec((a, b), lambda i: (0, 0), pipeline_mode=pl.Buffered(1))
    return pl.pallas_call(
        body, name="merge_loss", grid=(nt,),
        in_specs=[rows(512), rows(512), rows(D_MODEL), rows(D_MODEL), rows(D_MODEL), rows(D_MODEL),
                  full(A_WIDTH, D_MODEL), full(B_WIDTH, D_MODEL), full(D_MODEL, D_MODEL), full(1, D_MODEL)],
        out_specs=[rows(D_MODEL), rows(512), rows(512), rows(D_MODEL), rows(D_MODEL),
                   full(A_WIDTH, D_MODEL), full(B_WIDTH, D_MODEL), full(D_MODEL, D_MODEL),
                   full(1, D_MODEL), full(1, D_MODEL)],
        out_shape=[jax.ShapeDtypeStruct((t, D_MODEL), _ST), jax.ShapeDtypeStruct((t, 512), _ST),
                   jax.ShapeDtypeStruct((t, 512), _ST), jax.ShapeDtypeStruct((t, D_MODEL), _ST),
                   jax.ShapeDtypeStruct((t, D_MODEL), _ST),
                   jax.ShapeDtypeStruct((A_WIDTH, D_MODEL), _F32), jax.ShapeDtypeStruct((B_WIDTH, D_MODEL), _F32),
                   jax.ShapeDtypeStruct((D_MODEL, D_MODEL), _F32), jax.ShapeDtypeStruct((1, D_MODEL), _F32),
                   jax.ShapeDtypeStruct((1, D_MODEL), _F32)],
        scratch_shapes=[pltpu.VMEM((tm, D_MODEL), _F32)] * 4 + [pltpu.VMEM((tm, D_MODEL), _MX)] * 3,
        compiler_params=_cparams(("arbitrary",), VMEM_LIMIT),
    )(oa, ob, ga, gb, x2, tgt, wa, wb, wo, g_final)


def _in_proj_bwd_x(dpieces, wt, x2, dh2, g_in):
    t = x2.shape[0]
    tm = min(t, 512)
    np_ = len(PIECES)

    def body(*refs):
        dp_refs = refs[:np_]
        w_ref, x_ref, dh2_ref, g_ref, gx_ref, dg_ref = refs[np_:]

        @pl.when(pl.program_id(0) == 0)
        def _():
            dg_ref[...] = jnp.zeros_like(dg_ref)

        dh = None
        for (name, a, b), dp in zip(PIECES, dp_refs):
            part = _dot(dp[...], w_ref[a:b, :])
            dh = part if dh is None else dh + part
        xv = x_ref[...]
        r = lax.rsqrt(jnp.mean(xv * xv, axis=-1, keepdims=True) + NORM_EPS)
        nrm = xv * r
        dg_ref[...] = dg_ref[...] + jnp.sum(dh * nrm, axis=0, keepdims=True)
        gx_ref[...] = dh2_ref[...].astype(_F32) + _rms_bwd(dh * g_ref[...], nrm, r)

    rows = lambda w: pl.BlockSpec((tm, w), lambda i: (i, 0))
    full = lambda a, b: pl.BlockSpec((a, b), lambda i: (0, 0), pipeline_mode=pl.Buffered(1))
    return pl.pallas_call(
        body, name="in_proj_bwd_x", grid=(t // tm,),
        in_specs=[rows(b - a) for _, a, b in PIECES] + [full(D_IN, D_MODEL), rows(D_MODEL), rows(D_MODEL),
                                                          full(1, D_MODEL)],
        out_specs=[rows(D_MODEL), full(1, D_MODEL)],
        out_shape=[jax.ShapeDtypeStruct((t, D_MODEL), _F32), jax.ShapeDtypeStruct((1, D_MODEL), _F32)],
        compiler_params=_cparams(("arbitrary",), VMEM_LIMIT),
    )(*dpieces, wt, x2, dh2, g_in)


def _in_proj_bwd_w(h, dpieces, others, osplits):
    t = h.shape[0]
    tm = min(t, 1024)
    nt = t // tm
    np_, no = len(PIECES), len(others)
    half = D_MODEL // 2

    def body(*refs):
        h_ref, dp_refs, o_refs = refs[0], refs[1:1 + np_], refs[1 + np_:1 + np_ + no]
        mine_ref, theirs_ref = refs[1 + np_ + no:3 + np_ + no]
        r_refs = refs[3 + np_ + no:3 + np_ + 2 * no]
        acc_ref, keep_sem, send, recv, o_send, o_recv = refs[3 + np_ + 2 * no:]
        i = pl.program_id(0)
        x, y, c = _place()
        sibling = (x, y, 1 - c)
        early = [pltpu.make_async_remote_copy(
            src_ref=osplits[k].half(o_refs[k], 1 - c), dst_ref=r_refs[k], send_sem=o_send.at[k], recv_sem=o_recv.at[k],
            device_id=sibling, device_id_type=_MESH) for k in range(no)]

        @pl.when(i == 0)
        def _():
            for cp in early:
                cp.start()

        hv = h_ref[...]
        cols = lambda core: pl.ds(pl.multiple_of(core * half, LANE), half)
        writes = []
        by_size = sorted(range(np_), key=lambda j: PIECES[j][1] - PIECES[j][2])
        for j, ((name, a, b), dp) in [(j, (PIECES[j], dp_refs[j])) for j in by_size]:
            part = _dot_tn(dp[...], hv)
            if name == "alr":
                b = a + B_GATE_RANK
                part = part[0:B_GATE_RANK]
            acc_ref[a:b, :] = jnp.where(i == 0, 0.0, acc_ref[a:b, :]) + part
            keep = pltpu.make_async_copy(acc_ref.at[a:b, cols(c)], mine_ref.at[a:b], keep_sem.at[j])
            give = pltpu.make_async_remote_copy(
                src_ref=acc_ref.at[a:b, cols(1 - c)], dst_ref=theirs_ref.at[a:b], send_sem=send.at[j],
                recv_sem=recv.at[j], device_id=sibling, device_id_type=_MESH)
            writes += [keep, give]

            @pl.when(i == nt - 1)
            def _(keep=keep, give=give):
                keep.start()
                give.start()

        @pl.when(i == nt - 1)
        def _():
            for cp in writes + early:
                cp.wait()

    rows = lambda w: pl.BlockSpec((tm, w), lambda i: (i, 0))
    halves = [jax.ShapeDtypeStruct((D_IN, half), _F32)] * 2
    return pl.pallas_call(
        body, name="in_proj_bwd_w", grid=(nt,),
        in_specs=[rows(D_MODEL)] + [rows(b - a) for _, a, b in PIECES] + [_ANY] * no,
        out_specs=[_ANY] * (2 + no),
        out_shape=halves + [jax.ShapeDtypeStruct(sp.half_shape(g.shape), g.dtype) for g, sp in zip(others, osplits)],
        scratch_shapes=[pltpu.VMEM((D_IN, D_MODEL), _F32), pltpu.SemaphoreType.DMA((np_,)),
                        pltpu.SemaphoreType.DMA((np_,)), pltpu.SemaphoreType.DMA((np_,)),
                        pltpu.SemaphoreType.DMA((no,)), pltpu.SemaphoreType.DMA((no,))],
        compiler_params=_cparams(("arbitrary",), VMEM_LIMIT),
    )(h, *dpieces, *others)


def _place():
    return lax.axis_index("x"), lax.axis_index("y"), lax.axis_index("c")


def _other_chips(x, y):
    return [(1 - x, y), (x, 1 - y), (1 - x, 1 - y)]


class _Split(NamedTuple):
    by_rows: bool
    step: int
    size: int

    def half(self, ref, c):
        r, n = ref.shape[-2:]
        if self.by_rows:
            return ref.at[:, pl.ds(pl.multiple_of(c * (n // 2), LANE), n // 2)]
        return ref.at[pl.ds(pl.multiple_of(c * (r // 2), 16), r // 2), :]

    def chip_part(self, ref, k):
        if self.by_rows:
            return ref.at[pl.ds(pl.multiple_of(k * self.step, 16), self.size), :]
        return ref.at[:, pl.ds(pl.multiple_of(k * self.size, LANE), self.size)]

    def half_shape(self, shape):
        r, n = shape
        return (r, n // 2) if self.by_rows else (r // 2, n)

    def part_shape(self, shape):
        r, n = shape
        return (self.size, n) if self.by_rows else (r, self.size)


SPLIT_W_IN_T = _Split(True, WINDOW_STEP, WINDOW_ROWS)
SPLIT_W_O = _Split(True, 256, 256)
SPLIT_W_OUT = _Split(False, 256, 256)


def _gather_weights(shards, splits, fulls, pos_f):
    nw = len(shards)
    t = pos_f.shape[0]

    def body(*refs):
        ins, (pos_ref, c_ref) = refs[:nw], refs[nw:nw + 2]
        outs, tables = refs[nw + 2:2 * nw + 2], refs[2 * nw + 2:2 * nw + 5]
        send_a, recv_a, send_b, recv_b = refs[2 * nw + 5:]
        x, y, c = _place()
        me = 2 * x + y
        peers = _other_chips(x, y)

        def place(i, k, half):
            if splits[i] is None:
                return outs[i].at[k]
            if fulls[i][0] == 4 and len(fulls[i]) == 3:
                whole = outs[i].at[k]
            else:
                whole = splits[i].chip_part(outs[i], k)
            return splits[i].half(whole, half)

        first, passed = [], []
        for i in range(nw):
            src = ins[i] if splits[i] is None else splits[i].half(ins[i], c)
            for j, (px, py) in enumerate(peers):
                cp = pltpu.make_async_remote_copy(
                    src_ref=src, dst_ref=place(i, me, c), send_sem=send_a.at[3 * i + j],
                    recv_sem=recv_a.at[3 * i + j], device_id=(px, py, c), device_id_type=_MESH)
                cp.start()
                first.append(cp)
        _rope_tables_into(pos_ref, c_ref, *tables)
        for i in range(nw):
            for j, (px, py) in enumerate(peers):
                landed = place(i, 2 * px + py, c)
                pltpu.make_async_remote_copy(
                    src_ref=landed, dst_ref=landed, send_sem=send_a.at[3 * i + j], recv_sem=recv_a.at[3 * i + j],
                    device_id=(px, py, c), device_id_type=_MESH).wait_recv()
                if splits[i] is not None:
                    cp = pltpu.make_async_remote_copy(
                        src_ref=landed, dst_ref=landed, send_sem=send_b.at[3 * i + j], recv_sem=recv_b.at[3 * i + j],
                        device_id=(x, y, 1 - c), device_id_type=_MESH)
                    cp.start()
                    passed.append(cp)
        for i in range(nw):
            if splits[i] is None:
                continue
            for j, (px, py) in enumerate(peers):
                theirs = place(i, 2 * px + py, 1 - c)
                pltpu.make_async_remote_copy(
                    src_ref=theirs, dst_ref=theirs, send_sem=send_b.at[3 * i + j], recv_sem=recv_b.at[3 * i + j],
                    device_id=(x, y, 1 - c), device_id_type=_MESH).wait_recv()
        for cp in first + passed:
            cp.wait_send()

    vm = pl.BlockSpec(memory_space=pltpu.VMEM)
    tab = jax.ShapeDtypeStruct((t, LANE), _F32)
    return pl.pallas_call(
        body, name="gather_weights",
        in_specs=[_ANY] * nw + [vm, vm], out_specs=[_ANY] * nw + [vm] * 3,
        out_shape=[jax.ShapeDtypeStruct(f, s.dtype) for f, s in zip(fulls, shards)] + [tab] * 3,
        scratch_shapes=[pltpu.SemaphoreType.DMA((3 * nw,)) for _ in range(4)],
        compiler_params=_cparams(None, VMEM_LIMIT),
    )(*shards, pos_f, _rope_consts())


def _assemble_w_in_t(slots):
    bw = 256
    ov = WINDOW_ROWS - WINDOW_STEP

    def body(s_ref, o_ref):
        for k in range(4):
            base = k * WINDOW_STEP
            lo = 0 if k == 0 else ov
            if k > 0:
                o_ref[base:base + ov, :] = s_ref[k - 1, WINDOW_STEP:WINDOW_ROWS, :] + s_ref[k, 0:ov, :]
            hi = WINDOW_ROWS if k == 3 else WINDOW_STEP
            o_ref[base + lo:base + hi, :] = s_ref[k, lo:hi, :]

    return pl.pallas_call(
        body, name="assemble_w_in_t", grid=(D_MODEL // bw,),
        in_specs=[pl.BlockSpec((4, WINDOW_ROWS, bw), lambda i: (0, 0, i))],
        out_specs=pl.BlockSpec((D_IN, bw), lambda i: (0, i)),
        out_shape=jax.ShapeDtypeStruct((D_IN, D_MODEL), slots.dtype),
        compiler_params=_cparams(("parallel",)),
    )(slots)


def _row_block(rows):
    for cand in (976, 176, 256, 128):
        if rows % cand == 0:
            return cand
    return rows


def _pair_sum(g, r, split, c_arr, name):
    hr, hn = r.shape
    br = _row_block(hr)
    if split is None:
        g_spec = pl.BlockSpec((br, hn), lambda i, c_ref: (i, 0))
    elif split.by_rows:
        g_spec = pl.BlockSpec((br, hn), lambda i, c_ref: (i, c_ref[0]))
    else:
        g_spec = pl.BlockSpec((br, hn), lambda i, c_ref: (c_ref[0] * (hr // br) + i, 0))

    def body(c_ref, g_ref, r_ref, o_ref):
        o_ref[...] = (g_ref[...] + r_ref[...]).astype(o_ref.dtype)

    return pl.pallas_call(
        body, name=name,
        grid_spec=pltpu.PrefetchScalarGridSpec(
            num_scalar_prefetch=1, grid=(hr // br,),
            in_specs=[g_spec, pl.BlockSpec((br, hn), lambda i, c_ref: (i, 0))],
            out_specs=pl.BlockSpec((br, hn), lambda i, c_ref: (i, 0))),
        out_shape=jax.ShapeDtypeStruct(r.shape, _MX),
        compiler_params=_cparams(("parallel",)),
    )(c_arr, g, r)


_HBM = pl.BlockSpec(memory_space=pltpu.HBM)
_SEM = pl.BlockSpec(memory_space=pltpu.SEMAPHORE)
_FLOWS = pltpu.SideEffectType.DATAFLOW_SIDE_EFFECTING


def _chip_exchange_copies(refs, send, recv, splits):
    nw = len(refs) // 2
    x, y, c = _place()
    me = 2 * x + y
    copies = []
    for i in range(nw):
        for px, py in _other_chips(x, y):
            copies.append((splits[i].chip_part(refs[i], 2 * px + py), refs[nw + i].at[me], (px, py, c)))
    return [pltpu.make_async_remote_copy(src_ref=src, dst_ref=dst, send_sem=send.at[k], recv_sem=recv.at[k],
                                         device_id=peer, device_id_type=_MESH)
            for k, (src, dst, peer) in enumerate(copies)]


def _late_gather_copies(refs, send, recv, splits):
    nw = len(refs) // 2
    x, y, c = _place()
    me = 2 * x + y
    copies = []
    for i in range(nw):
        for px, py in [(x, y)] + _other_chips(x, y):
            copies.append((refs[i], splits[i].chip_part(refs[nw + i], me), (px, py, c)))
    return [pltpu.make_async_remote_copy(src_ref=src, dst_ref=dst, send_sem=send.at[k], recv_sem=recv.at[k],
                                         device_id=peer, device_id_type=_MESH)
            for k, (src, dst, peer) in enumerate(copies)]


def _start_copies(name, flying, copies_of, n_copies, after=None):
    first = [] if after is None else [after]

    def body(*refs):
        ins = refs[:len(flying)]
        send, recv = refs[len(flying) + len(first):len(flying) + len(first) + 2]
        token = refs[-1]
        for cp in copies_of(ins, send, recv):
            cp.start()
        token[...] = jnp.zeros_like(token)

    outs = pl.pallas_call(
        body, name=name,
        in_specs=[_HBM] * len(flying) + [_ANY] * len(first),
        out_specs=[_SEM, _SEM] + [_HBM] * len(flying) + [pl.BlockSpec(memory_space=pltpu.VMEM)],
        out_shape=[pltpu.SemaphoreType.DMA((n_copies,)), pltpu.SemaphoreType.DMA((n_copies,))]
        + [pltpu.HBM(f.shape, f.dtype) for f in flying] + [jax.ShapeDtypeStruct((8, LANE), _F32)],
        input_output_aliases={i: 2 + i for i in range(len(flying))},
        compiler_params=pltpu.CompilerParams(has_side_effects=_FLOWS),
    )(*[pltpu.with_memory_space_constraint(f, pltpu.HBM) for f in flying], *first)
    return outs[0], outs[1], outs[2:2 + len(flying)], outs[-1]


def _wait_copies(name, send, recv, flying, copies_of, after):
    def body(*refs):
        ins = refs[:len(flying)]
        send_ref, recv_ref = refs[len(flying):len(flying) + 2]
        for cp in copies_of(ins, send_ref, recv_ref):
            cp.wait_send()
            cp.wait_recv()

    return pl.pallas_call(
        body, name=name,
        in_specs=[_HBM] * len(flying) + [_SEM, _SEM, _ANY],
        out_specs=[_HBM] * len(flying),
        out_shape=[pltpu.HBM(f.shape, f.dtype) for f in flying],
        input_output_aliases={i: i for i in range(len(flying))},
        compiler_params=pltpu.CompilerParams(has_side_effects=_FLOWS),
    )(*flying, send, recv, after)


def _sum_chips(q, p, split, place_arr, name):
    _, hr, hn = q.shape
    if split.by_rows:
        out_shape = (hr, 2 * hn)
        o_spec = pl.BlockSpec((hr, hn), lambda i, pr: (0, pr[0]))
        p_spec = pl.BlockSpec((pl.Element(hr), pl.Element(hn)), lambda i, pr: (pr[1] * split.step, 0))
    else:
        out_shape = (2 * hr, hn)
        o_spec = pl.BlockSpec((hr, hn), lambda i, pr: (pr[0], 0))
        p_spec = pl.BlockSpec((hr, hn), lambda i, pr: (0, pr[1]))

    def body(pr, q_ref, p_ref, o_ref):
        f = lambda k: jnp.where(pr[1] == k, p_ref[...], q_ref[k]).astype(_F32)
        o_ref[...] = ((f(0) + f(1)) + f(2)) + f(3)

    return pl.pallas_call(
        body, name=name,
        grid_spec=pltpu.PrefetchScalarGridSpec(
            num_scalar_prefetch=1, grid=(1,),
            in_specs=[pl.BlockSpec((4, hr, hn), lambda i, pr: (0, 0, 0)), p_spec], out_specs=o_spec),
        out_shape=jax.ShapeDtypeStruct(out_shape, _F32),
        compiler_params=_cparams(("arbitrary",), VMEM_LIMIT),
    )(place_arr, q, p)


def _pair_share(bufs, splits, small):
    nw = len(bufs)

    def body(*refs):
        ins, small_ref, outs, all_ref = refs[:nw], refs[nw], refs[nw + 1:2 * nw + 1], refs[2 * nw + 1]
        send, recv, s_send, s_recv = refs[2 * nw + 2:]
        x, y, c = _place()
        copies = []
        for r in range(1, 8):
            peer = (1 - x if r & 4 else x, 1 - y if r & 2 else y, 1 - c if r & 1 else c)
            cp = pltpu.make_async_remote_copy(
                src_ref=small_ref, dst_ref=all_ref.at[4 * x + 2 * y + c], send_sem=s_send.at[r - 1],
                recv_sem=s_recv.at[r - 1], device_id=peer, device_id_type=_MESH)
            cp.start()
            copies.append(cp)
        for i in range(nw):
            cp = pltpu.make_async_remote_copy(
                src_ref=splits[i].half(ins[i], c), dst_ref=splits[i].half(outs[i], c), send_sem=send.at[i],
                recv_sem=recv.at[i], device_id=(x, y, 1 - c), device_id_type=_MESH)
            cp.start()
            copies.append(cp)
        for cp in copies:
            cp.wait()

    return pl.pallas_call(
        body, name="grad_pair_share",
        in_specs=[_ANY] * (nw + 1), out_specs=[_ANY] * (nw + 1),
        out_shape=[jax.ShapeDtypeStruct(b.shape, b.dtype) for b in bufs]
        + [jax.ShapeDtypeStruct((8,) + small.shape, small.dtype)],
        input_output_aliases={i: i for i in range(nw)},
        scratch_shapes=[pltpu.SemaphoreType.DMA((nw,)), pltpu.SemaphoreType.DMA((nw,)),
                        pltpu.SemaphoreType.DMA((7,)), pltpu.SemaphoreType.DMA((7,))],
    )(*bufs, small)


def _sum_devices(parts, own, dev_arr):
    def body(dev, p_ref, own_ref, tot_ref):
        f = lambda d: jnp.where(dev[0] == d, own_ref[...], p_ref[d])
        acc = f(0)
        for d in range(1, 8):
            acc = acc + f(d)
        tot_ref[...] = acc

    return pl.pallas_call(
        body, name="small_sum",
        grid_spec=pltpu.PrefetchScalarGridSpec(
            num_scalar_prefetch=1, grid=(1,),
            in_specs=[pl.BlockSpec(parts.shape, lambda i, dev: (0, 0, 0)), pl.BlockSpec(own.shape, lambda i, dev: (0, 0))],
            out_specs=pl.BlockSpec(own.shape, lambda i, dev: (0, 0))),
        out_shape=jax.ShapeDtypeStruct(own.shape, own.dtype),
    )(dev_arr, parts, own)


def _adam_update(w, g, m, v):
    m2 = ADAM_B1 * m + (1.0 - ADAM_B1) * g
    v2 = ADAM_B2 * v + (1.0 - ADAM_B2) * (g * g)
    m_hat = m2 / (1.0 - ADAM_B1 ** ADAM_STEP)
    v_hat = v2 / (1.0 - ADAM_B2 ** ADAM_STEP)
    return -ADAM_LR * (m_hat / (jnp.sqrt(v_hat) + ADAM_EPS) + ADAM_WD * w), m2, v2


SMALL_AT = dict(g_in=(0, 0), g_final=(1, 0), g_gla_norm=(2, 0), b_alpha=(2, B_WIDTH), attn_sinks=(2, B_WIDTH + B_KEY_WIDTH))
LOSS_AT = (2, B_WIDTH + B_KEY_WIDTH + LANE)
WUP_ROWS = (3, 7)


def _adamw_small(tot, g_wup, params):
    names = list(params)

    def body(*refs):
        tot_ref, gw_ref = refs[0], refs[1]
        ins = refs[2:2 + 3 * len(names)]
        outs = refs[2 + 3 * len(names):]
        for i, nm in enumerate(names):
            w_ref, m_ref, v_ref = ins[3 * i:3 * i + 3]
            if nm in SMALL_AT:
                r, a = SMALL_AT[nm]
                g = tot_ref[r:r + 1, a:a + w_ref.shape[1]]
            else:
                g = gw_ref[...]
            d, m2, v2 = _adam_update(w_ref[...], g, m_ref[...], v_ref[...])
            for o_ref, val in zip(outs[4 * i:4 * i + 4], (g, d, m2, v2)):
                o_ref[...] = val

    vm = pl.BlockSpec(memory_space=pltpu.VMEM)
    flat = [a for nm in names for a in params[nm]]
    out_shape = [jax.ShapeDtypeStruct(params[nm][0].shape, _F32) for nm in names for _ in range(4)]
    outs = pl.pallas_call(
        body, name="adamw_small", in_specs=[vm] * (2 + len(flat)), out_specs=[vm] * len(out_shape), out_shape=out_shape,
    )(tot, g_wup, *flat)
    return {nm: tuple(outs[4 * i:4 * i + 4]) for i, nm in enumerate(names)}


def _adamw(w, g, m, v, name):
    lead = w.shape[0] != 1
    r, n = (w.shape[0], w.shape[2]) if lead else w.shape[1:]
    br = r
    for cand in (256, 244, 128):
        if r > cand and r % cand == 0:
            br = cand
            break

    def body(w_ref, g_ref, m_ref, v_ref, d_ref, nm_ref, nv_ref):
        d_ref[...], nm_ref[...], nv_ref[...] = _adam_update(w_ref[...], g_ref[...], m_ref[...], v_ref[...])

    blk = pl.BlockSpec((br, 1, n), lambda i: (i, 0, 0)) if lead else pl.BlockSpec((None, br, n), lambda i: (0, i, 0))
    shp = jax.ShapeDtypeStruct(w.shape, _F32)
    return pl.pallas_call(
        body, name=name, grid=(r // br,),
        in_specs=[blk] * 4, out_specs=[blk] * 3, out_shape=[shp] * 3,
        compiler_params=_cparams(("parallel",)),
    )(w, g, m, v)


def kernel(x, positions, g_in, w_in, w_alpha_up, b_alpha, attn_sinks, g_gla_norm, w_out_a, w_out_b, w_o, g_final, loss_target, m_g_in, m_w_in, m_w_alpha_up, m_b_alpha, m_attn_sinks, m_g_gla_norm, m_w_out_a, m_w_out_b, m_w_o, m_g_final, v_g_in, v_w_in, v_w_alpha_up, v_b_alpha, v_attn_sinks, v_g_gla_norm, v_w_out_a, v_w_out_b, v_w_o, v_g_final):
    nseq, seq, _ = x.shape
    t = nseq * seq
    cx, cy, cc = _place()
    chip = 2 * cx + cy
    c_arr = jnp.reshape(cc, (1,)).astype(jnp.int32)

    tr = lambda w: jnp.transpose(w, (2, 0, 1))
    w_in_t = tr(w_in).reshape(SHARD, D_MODEL).astype(_MX)
    pad = WINDOW_ROWS - SHARD
    window = lax.switch(chip, [lambda w, k=k: jnp.pad(w, ((4 * k, pad - 4 * k), (0, 0))) for k in range(4)], w_in_t)
    shards = [window, w_alpha_up[0].astype(_MX)]
    late = [w_out_a[0].astype(_MX), w_out_b[0].astype(_MX), w_o[0].astype(_MX)]
    late_splits = [SPLIT_W_OUT, SPLIT_W_OUT, SPLIT_W_O]
    splits = [SPLIT_W_IN_T, None]
    fulls = [(4, WINDOW_ROWS, D_MODEL), (4, B_GATE_RANK, B_KEY_WIDTH // 4)]
    pos_f = positions.astype(_F32).reshape(t, 1)
    win_g, wup_g, cos, sa, sb = _gather_weights(shards, splits, fulls, pos_f)
    late_copies = lambda refs, send, recv: _late_gather_copies(refs, send, recv, late_splits)
    late_full = [lax.empty(shape, _MX) for shape in ((A_WIDTH, D_MODEL), (B_WIDTH, D_MODEL), (D_MODEL, D_MODEL))]
    l_send, l_recv, l_flying, l_token = _start_copies("late_gather_start", late + late_full, late_copies,
                                                      4 * len(late), after=win_g)
    win_g = lax.dynamic_update_slice(win_g, window[None], (chip, 0, 0))
    wup_g = lax.dynamic_update_slice(wup_g, shards[1][None], (chip, 0, 0))
    wt = _assemble_w_in_t(win_g)
    wup = jnp.concatenate([jnp.transpose(wup_g, (1, 0, 2)).reshape(B_GATE_RANK, B_KEY_WIDTH),
                           jnp.zeros((LANE - B_GATE_RANK, B_KEY_WIDTH), _MX)], axis=0)

    x2 = x.reshape(t, D_MODEL)
    tgt = loss_target.reshape(t, D_MODEL)
    sinks = attn_sinks.reshape(A_HEADS)
    gf = g_final.reshape(1, D_MODEL)

    h, qkv, za, qkb, vb, zb, alr, ga, gb = _in_proj(x2, g_in + l_token[0, 0], wt, cos, sa, sb)
    oa = _attn_fwd(qkv, za, sinks, nseq)
    ob, oraw, sst = _gla_fwd(qkb, vb, zb, alr, wup, b_alpha, g_gla_norm, nseq)

    wa, wb, wo = _wait_copies("late_gather_wait", l_send, l_recv, l_flying, late_copies, ob)[len(late):]
    dh2, doa, dob, dga, dgb, dwa, dwb, dwo, dgf, lossv = _merge_loss(oa, ob, ga, gb, x2, tgt, wa, wb, wo, gf)

    dqkv, dza, dsink = _attn_bwd(qkv, za, doa, sinks, cos, sa, sb, nseq)
    dqkb, dvb, dzb, dalr, dwup, dba, dgn = _gla_bwd(qkb, vb, zb, alr, oraw, dob, sst, wup, b_alpha, g_gla_norm, nseq)
    dpieces = [dqkv, dza, dqkb, dvb, dzb, dalr, dga, dgb]
    gsplits = [SPLIT_W_IN_T, SPLIT_W_OUT, SPLIT_W_OUT, SPLIT_W_O]
    names = ("w_in", "w_out_a", "w_out_b", "w_o")
    dwin_mine, *from_sibling = _in_proj_bwd_w(h, dpieces, [dwa, dwb, dwo], gsplits[1:])
    pair_sums = [_pair_sum(g, r, sp, c_arr, "pair_sum_" + nm)
                 for g, r, sp, nm in zip([dwin_mine, dwa, dwb, dwo], from_sibling, [None] + gsplits[1:], names)]
    exchange = lambda refs, send, recv: _chip_exchange_copies(refs, send, recv, gsplits)
    lands = [lax.empty((4,) + sp.part_shape(p.shape), p.dtype) for p, sp in zip(pair_sums, gsplits)]
    send, recv, flying, token = _start_copies("grad_chip_exchange_start", pair_sums + lands, exchange, 3 * len(lands))
    grad_x2, dgin = _in_proj_bwd_x(dpieces, wt, x2, dh2, g_in + token[0, 0])
    landed = _wait_copies("grad_chip_exchange_wait", send, recv, flying, exchange, grad_x2)
    place_arr = jnp.stack([cc, chip]).astype(jnp.int32)
    reduced = [_sum_chips(q, p, sp, place_arr, "chip_sum_" + nm)
               for q, p, sp, nm in zip(landed[len(lands):], landed[:len(lands)], gsplits, names)]
    row2 = jnp.concatenate([dgn, dba, jnp.pad(dsink[:, 0].reshape(1, A_HEADS), ((0, 0), (0, LANE - A_HEADS))),
                            jnp.pad(jnp.sum(lossv, axis=1, keepdims=True), ((0, 0), (0, LANE - 1)))], axis=1)
    small = jnp.concatenate([dgin, dgf, row2, dwup[:B_GATE_RANK].reshape(WUP_ROWS[1] - WUP_ROWS[0], D_MODEL),
                             jnp.zeros((1, D_MODEL), _F32)], axis=0)
    g_window, g_wa, g_wb, g_wo, small_parts = _pair_share(reduced, gsplits, small)
    g_win_t = lax.switch(chip, [lambda w, k=k: w[4 * k:4 * k + SHARD].reshape(SHARD, 1, D_MODEL) for k in range(4)],
                         g_window)
    dev_arr = jnp.reshape(2 * chip + cc, (1,)).astype(jnp.int32)
    tot = _sum_devices(small_parts, small, dev_arr)
    loss = tot[LOSS_AT]
    nup = B_KEY_WIDTH // 4
    g_wup = lax.dynamic_slice(tot[WUP_ROWS[0]:WUP_ROWS[1]].reshape(B_GATE_RANK, B_KEY_WIDTH), (0, chip * nup),
                              (B_GATE_RANK, nup))

    row = lambda a: a.reshape(1, -1)
    sm = _adamw_small(tot, g_wup, dict(
        g_in=(g_in, m_g_in, v_g_in), g_final=(row(g_final), row(m_g_final), row(v_g_final)),
        g_gla_norm=(g_gla_norm, m_g_gla_norm, v_g_gla_norm), b_alpha=(b_alpha, m_b_alpha, v_b_alpha),
        attn_sinks=(attn_sinks, m_attn_sinks, v_attn_sinks),
        w_alpha_up=(w_alpha_up[0], m_w_alpha_up[0], v_w_alpha_up[0])))
    sm["g_final"] = tuple(a.reshape(D_MODEL) for a in sm["g_final"])
    sm["w_alpha_up"] = tuple(a[None] for a in sm["w_alpha_up"])

    untr = lambda a: jnp.transpose(a, (1, 2, 0))
    big = dict(w_in=tuple(untr(a) for a in (g_win_t,) + tuple(_adamw(tr(w_in), g_win_t, tr(m_w_in), tr(v_w_in), "adamw_w_in"))))
    for nm, w, g, m, v in (("w_out_a", w_out_a, g_wa, m_w_out_a, v_w_out_a),
                           ("w_out_b", w_out_b, g_wb, m_w_out_b, v_w_out_b), ("w_o", w_o, g_wo, m_w_o, v_w_o)):
        big[nm] = (g[None],) + tuple(_adamw(w, g[None], m, v, "adamw_" + nm))

    order = ("g_in", "w_in", "w_alpha_up", "b_alpha", "attn_sinks", "g_gla_norm", "w_out_a", "w_out_b", "w_o", "g_final")
    outs = [big[nm][kind] if nm in big else sm[nm][kind] for kind in range(4) for nm in order]
    return (loss, grad_x2.reshape(x.shape), *outs)
```

```python
import math
from typing import NamedTuple

import numpy as np
import jax
import jax.numpy as jnp
from jax import lax
from jax.experimental import pallas as pl
from jax.experimental.pallas import tpu as pltpu

D_MODEL = 1024
A_HEADS, A_KV_HEADS, A_HEAD_DIM = 8, 2, 64
A_GROUP = A_HEADS // A_KV_HEADS
A_WIDTH, A_KV_WIDTH = 512, 128
BLOCK = 128
ROPE_THETA = 500000.0
ROPE_DIM = 16
B_HEADS, B_KEY_DIM, B_VAL_DIM = 4, 64, 128
B_KEY_WIDTH, B_WIDTH = 256, 512
B_GATE_RANK = 16
B_GATE_TEMP = 16.0
B_CHUNK = 64
NORM_EPS = 1e-6
NEG_BIG = -1e30
D_IN = 4880

ADAM_LR, ADAM_B1, ADAM_B2, ADAM_EPS, ADAM_WD, ADAM_STEP = 0.001, 0.9, 0.999, 1e-08, 0.01, 10

LANE = 128
ALR_AT = 2816
PIECES = (("qkv", 0, 768), ("za", 768, 1280), ("qkb", 1280, 1792), ("vb", 1792, 2304),
          ("zb", 2304, 2816), ("alr", ALR_AT, ALR_AT + LANE), ("ga", 2832, 3856), ("gb", 3856, 4880))
SHARD = D_IN // 4
WINDOW_STEP = 1216
WINDOW_ROWS = 1232

GLA_BLOCK = 256
GLA_FWD_BLOCK = 2048
MERGE_SLAB = 16
VMEM_LIMIT = 56 * 1024 * 1024

_F32 = jnp.float32
_MX = jnp.bfloat16
_ST = jnp.bfloat16

_MESH = pl.DeviceIdType.MESH
_ANY = pl.BlockSpec(memory_space=pl.ANY)


def _cparams(sem=None, vmem=None):
    return pltpu.CompilerParams(dimension_semantics=sem, vmem_limit_bytes=vmem)


def _dot(a, b):
    return jnp.dot(a.astype(_MX), b.astype(_MX), preferred_element_type=_F32)


def _dot_nt(a, b):
    return lax.dot_general(a.astype(_MX), b.astype(_MX), (((1,), (1,)), ((), ())),
                           preferred_element_type=_F32)


def _dot_tn(a, b):
    return lax.dot_general(a.astype(_MX), b.astype(_MX), (((0,), (0,)), ((), ())),
                           preferred_element_type=_F32)


def _dot_ones(ones_mat, v):
    o = ones_mat.astype(jnp.bfloat16)
    v0 = v.astype(jnp.bfloat16)
    v1 = (v - v0.astype(_F32)).astype(jnp.bfloat16)
    d = lambda t: jnp.dot(o, t, preferred_element_type=_F32)
    return d(v0) + d(v1)


def _sigmoid(x):
    return 0.5 * jnp.tanh(0.5 * x) + 0.5


def _log_sigmoid(x):
    return jnp.minimum(x, 0.0) - jnp.log(1.0 + jnp.exp(-jnp.abs(x)))


def _lane_tile(t, width):
    reps = width // t.shape[1]
    return t if reps == 1 else jnp.tile(t, (1, reps))


def _rope(t, cos, sa, sb, sign):
    w = t.shape[1]
    rot = pltpu.roll(t, w - 8, 1) * _lane_tile(sa, w) + pltpu.roll(t, 8, 1) * _lane_tile(sb, w)
    return t * _lane_tile(cos, w) + sign * rot


def _rms_bwd(dy_g, n, r):
    return r * (dy_g - n * jnp.mean(dy_g * n, axis=-1, keepdims=True))


ROPE_ROWS = 256


def _rope_consts():
    lane = np.arange(LANE) % A_HEAD_DIM
    half = ROPE_DIM // 2
    inv = np.exp((np.float32(-math.log(ROPE_THETA)) * np.arange(half, dtype=np.float32)) * np.float32(2.0 / ROPE_DIM))
    consts = np.zeros((8, LANE), np.float32)
    consts[0] = np.where(lane < ROPE_DIM, inv[lane % half], 0.0)
    consts[1] = np.where(lane < half, -1.0, 0.0)
    consts[2] = np.where((lane >= half) & (lane < ROPE_DIM), 1.0, 0.0)
    return jnp.asarray(consts)


def _rope_tables_into(pos_ref, c_ref, cos_ref, sa_ref, sb_ref):
    def rows_of(b, carry):
        rows = pl.ds(pl.multiple_of(b * ROPE_ROWS, ROPE_ROWS), ROPE_ROWS)
        ang = pos_ref[rows, :] * c_ref[0:1, :]
        s = jnp.sin(ang)
        cos_ref[rows, :] = jnp.cos(ang)
        sa_ref[rows, :] = s * c_ref[1:2, :]
        sb_ref[rows, :] = s * c_ref[2:3, :]
        return carry

    lax.fori_loop(0, pos_ref.shape[0] // ROPE_ROWS, rows_of, 0)


def _in_proj(x2, g_in, wt, cos, sa, sb):
    t = x2.shape[0]
    tm = min(t, 512)

    def body(x_ref, g_ref, w_ref, cos_ref, sa_ref, sb_ref, h_ref, qkv_ref, za_ref, qkb_ref,
             vb_ref, zb_ref, alr_ref, ga_ref, gb_ref):
        xv = x_ref[...]
        r = lax.rsqrt(jnp.mean(xv * xv, axis=-1, keepdims=True) + NORM_EPS)
        h = (xv * r * g_ref[...]).astype(_MX)
        h_ref[...] = h.astype(_ST)
        outs = dict(za=za_ref, qkb=qkb_ref, vb=vb_ref, zb=zb_ref, alr=alr_ref, ga=ga_ref, gb=gb_ref)
        for name, a, b in PIECES:
            p = _dot_nt(h, w_ref[a:b, :])
            if name == "qkv":
                c, s1, s2 = cos_ref[...], sa_ref[...], sb_ref[...]
                qkv_ref[:, 0:512] = _rope(p[:, 0:512], c, s1, s2, 1.0).astype(_ST)
                qkv_ref[:, 512:640] = _rope(p[:, 512:640], c, s1, s2, 1.0).astype(_ST)
                qkv_ref[:, 640:768] = p[:, 640:768].astype(_ST)
            else:
                outs[name][...] = p.astype(outs[name].dtype)

    rows = lambda w: pl.BlockSpec((tm, w), lambda i: (i, 0))
    shp = lambda name, w: jax.ShapeDtypeStruct((t, w), _F32 if name == "qkb" else _ST)
    widths = [D_MODEL] + [b - a for _, a, b in PIECES]
    return pl.pallas_call(
        body, name="in_proj", grid=(t // tm,),
        in_specs=[rows(D_MODEL), pl.BlockSpec((1, D_MODEL), lambda i: (0, 0)),
                  pl.BlockSpec((D_IN, D_MODEL), lambda i: (0, 0), pipeline_mode=pl.Buffered(1)),
                  rows(LANE), rows(LANE), rows(LANE)],
        out_specs=[rows(w) for w in widths],
        out_shape=[shp(n, w) for n, w in zip(["h"] + [p[0] for p in PIECES], widths)],
        compiler_params=_cparams(("parallel",), VMEM_LIMIT),
    )(x2, g_in, wt, cos, sa, sb)


def _attn_operands(k_prev, k_cur, v_prev, v_cur, want_bwd):
    kf = jnp.concatenate([k_prev, k_cur], axis=0).astype(_F32) * (A_HEAD_DIM ** -0.5)
    vf = jnp.concatenate([v_prev, v_cur], axis=0).astype(_F32)
    lo = lax.broadcasted_iota(jnp.int32, (1, LANE), 1) < 64

    def on_lanes(a):
        sw = pltpu.roll(a, 64, 1)
        z = jnp.zeros_like(a)
        return [[jnp.where(lo, a, z).astype(_MX), jnp.where(lo, z, sw).astype(_MX)],
                [jnp.where(lo, sw, z).astype(_MX), jnp.where(lo, z, a).astype(_MX)]]

    def on_rows(a):
        at = a.T.astype(_MX)
        z = jnp.zeros((64, at.shape[1]), _MX)
        top, bot = at[0:64], at[64:128]
        return [[jnp.concatenate([top, z], axis=0), jnp.concatenate([z, top], axis=0)],
                [jnp.concatenate([bot, z], axis=0), jnp.concatenate([z, bot], axis=0)]]

    ops = dict(k_lanes=on_lanes(kf), v_rows=on_rows(vf), lo=lo)
    if want_bwd:
        ops.update(v_lanes=on_lanes(vf), k_rows=on_rows(kf))
    return ops


def _attn_valid(n):
    kj = lax.broadcasted_iota(jnp.int32, (2 * BLOCK, 2 * BLOCK), 0) - BLOCK
    qi = lax.broadcasted_iota(jnp.int32, (2 * BLOCK, 2 * BLOCK), 1) & (BLOCK - 1)
    return (kj <= qi) & (qi - kj < BLOCK) & ((n > 0) | (kj >= 0))


def _attn_sinks(sink_ref, h_a, h_b):
    first = lax.broadcasted_iota(jnp.int32, (1, 2 * BLOCK), 1) < BLOCK
    return jnp.where(first, sink_ref[h_a], sink_ref[h_b])


def _attn_softmax_t(k_lanes, q_pair, valid, sink):
    s = jnp.where(valid, _dot_nt(k_lanes, q_pair), NEG_BIG)
    m = jnp.maximum(jnp.max(s, axis=0, keepdims=True), sink)
    e = jnp.exp(s - m)
    e_sink = jnp.exp(sink - m)
    inv = 1.0 / (jnp.sum(e, axis=0, keepdims=True) + e_sink)
    return e, e_sink, inv


ATTN_TILE = 16


def _attn_kv(qkv_ref, kvp_ref, j):
    rows = slice(j * BLOCK, (j + 1) * BLOCK)
    if j == 0:
        k_prev, v_prev = kvp_ref[:, 0:128], kvp_ref[:, 128:256]
    else:
        before = slice((j - 1) * BLOCK, j * BLOCK)
        k_prev, v_prev = qkv_ref[before, 512:640], qkv_ref[before, 640:768]
    return k_prev, qkv_ref[rows, 512:640], v_prev, qkv_ref[rows, 640:768]


def _attn_fwd(qkv, za, sinks, nseq):
    t = qkv.shape[0]
    nblk = min(ATTN_TILE, t // nseq // BLOCK)
    tile = nblk * BLOCK
    nt = t // nseq // tile

    def body(sink_ref, qkv_ref, kvp_ref, za_ref, oa_ref):
        for j in range(nblk):
            rows = slice(j * BLOCK, (j + 1) * BLOCK)
            ops = _attn_operands(*_attn_kv(qkv_ref, kvp_ref, j), False)
            valid = _attn_valid(nblk * pl.program_id(1) + j)[:, 0:BLOCK]
            for pr in range(A_HEADS // 2):
                lanes = slice(pr * LANE, (pr + 1) * LANE)
                g = pr // (A_GROUP // 2)
                q_pair = qkv_ref[rows, lanes]
                ot = None
                for half in range(2):
                    e, _, inv = _attn_softmax_t(ops["k_lanes"][g][half], q_pair, valid, sink_ref[2 * pr + half])
                    part = _dot(ops["v_rows"][g][half], e) * inv
                    ot = part if ot is None else ot + part
                z = za_ref[rows, lanes].astype(_F32)
                oa_ref[rows, lanes] = (ot.T * (z * _sigmoid(z))).astype(_ST)

    cur = lambda w: pl.BlockSpec((tile, w), lambda s, n: (s * nt + n, 0))
    return pl.pallas_call(
        body, name="attn_fwd", grid=(nseq, nt),
        in_specs=[pl.BlockSpec(memory_space=pltpu.SMEM), cur(768),
                  pl.BlockSpec((BLOCK, 256), lambda s, n: (nblk * (s * nt + n) - jnp.minimum(n, 1), 2)),
                  cur(512)],
        out_specs=cur(512), out_shape=jax.ShapeDtypeStruct((t, A_WIDTH), _ST),
        compiler_params=_cparams(("parallel", "arbitrary")),
    )(sinks, qkv, qkv, za)


def _attn_bwd(qkv, za, doa, sinks, cos, sa, sb, nseq):
    t = qkv.shape[0]
    nblk = min(ATTN_TILE, t // nseq // BLOCK)
    tile = nblk * BLOCK
    nt = t // nseq // tile

    def body(sink_ref, qkv_ref, kvp_ref, za_ref, doa_ref, cos_ref, sa_ref, sb_ref,
             dqkv_ref, dza_ref, dsink_ref, ck_ref, cv_ref):
        s_id, i = pl.program_id(0), pl.program_id(1)

        @pl.when((s_id == 0) & (i == 0))
        def _():
            dsink_ref[...] = jnp.zeros_like(dsink_ref)

        @pl.when(i == 0)
        def _():
            ck_ref[...] = jnp.zeros_like(ck_ref)
            cv_ref[...] = jnp.zeros_like(cv_ref)

        carry_k, carry_v = ck_ref[...], cv_ref[...]
        for j in reversed(range(nblk)):
            rows = slice(j * BLOCK, (j + 1) * BLOCK)
            ops = _attn_operands(*_attn_kv(qkv_ref, kvp_ref, j), True)
            lo = ops["lo"]
            valid = _attn_valid(nblk * (nt - 1 - i) + j)
            dk_acc, dv_acc, dq_pairs = [], [], []
            for g in range(A_KV_HEADS):
                pairs = [slice((2 * g + p) * LANE, (2 * g + p + 1) * LANE) for p in range(2)]
                q_both = jnp.concatenate([qkv_ref[rows, p] for p in pairs], axis=0)
                q_f = q_both.astype(_F32)
                z = [za_ref[rows, p].astype(_F32) for p in pairs]
                sz = [_sigmoid(v) for v in z]
                d_oa = [doa_ref[rows, p].astype(_F32) for p in pairs]
                d_att = jnp.concatenate([d_oa[p] * (z[p] * sz[p]) for p in range(2)], axis=0)
                zero = jnp.zeros_like(d_att)
                ot, dqt, ds_all, pn_all, qz_all, daz_all = None, None, [], [], [], []
                for half in range(2):
                    heads = (4 * g + half, 4 * g + 2 + half)
                    e, e_sink, inv = _attn_softmax_t(ops["k_lanes"][g][half], q_both, valid,
                                                     _attn_sinks(sink_ref, *heads))
                    pn = e * inv
                    dpt = _dot_nt(ops["v_lanes"][g][half], d_att)
                    delta = jnp.sum(pn * dpt, axis=0, keepdims=True)
                    ds = (pn * (dpt - delta)).astype(_MX)
                    pn = pn.astype(_MX)
                    d_sink = e_sink * inv * delta
                    for p, h in enumerate(heads):
                        dsink_ref[h:h + 1, :] = dsink_ref[h:h + 1, :] - jnp.sum(d_sink[:, p * BLOCK:(p + 1) * BLOCK])
                    o_part = _dot(ops["v_rows"][g][half], pn)
                    dq_part = _dot(ops["k_rows"][g][half], ds)
                    ot = o_part if ot is None else ot + o_part
                    dqt = dq_part if dqt is None else dqt + dq_part
                    mine = lo if half == 0 else jnp.logical_not(lo)
                    ds_all.append(ds)
                    pn_all.append(pn)
                    qz_all.append(jnp.where(mine, q_f, zero).astype(_MX))
                    daz_all.append(jnp.where(mine, d_att, zero).astype(_MX))
                dk_acc.append(_dot(jnp.concatenate(ds_all, axis=1), jnp.concatenate(qz_all, axis=0)))
                dv_acc.append(_dot(jnp.concatenate(pn_all, axis=1), jnp.concatenate(daz_all, axis=0)))
                for p, lanes in enumerate(pairs):
                    cols = slice(p * BLOCK, (p + 1) * BLOCK)
                    dza_ref[rows, lanes] = (d_oa[p] * ot[:, cols].T * (sz[p] * (1.0 + z[p] * (1.0 - sz[p])))).astype(_ST)
                    dq_pairs.append(dqt[:, cols].T)

            def fold(acc, scale):
                both = [a + pltpu.roll(a, 64, 1) for a in acc]
                return jnp.where(lo, both[0], both[1]) * scale

            dk_full = fold(dk_acc, A_HEAD_DIM ** -0.5)
            dv_full = fold(dv_acc, 1.0)
            dk_cur, dv_cur = dk_full[BLOCK:] + carry_k, dv_full[BLOCK:] + carry_v
            carry_k, carry_v = dk_full[:BLOCK], dv_full[:BLOCK]
            c, s1, s2 = cos_ref[rows, :], sa_ref[rows, :], sb_ref[rows, :]
            dqkv_ref[rows, 0:512] = _rope(jnp.concatenate(dq_pairs, axis=1), c, s1, s2, -1.0).astype(_ST)
            dqkv_ref[rows, 512:640] = _rope(dk_cur, c, s1, s2, -1.0).astype(_ST)
            dqkv_ref[rows, 640:768] = dv_cur.astype(_ST)
        ck_ref[...] = carry_k
        cv_ref[...] = carry_v

    cur = lambda w: pl.BlockSpec((tile, w), lambda s, i: (s * nt + nt - 1 - i, 0))
    return pl.pallas_call(
        body, name="attn_bwd", grid=(nseq, nt),
        in_specs=[pl.BlockSpec(memory_space=pltpu.SMEM), cur(768),
                  pl.BlockSpec((BLOCK, 256),
                               lambda s, i: (nblk * (s * nt + nt - 1 - i) - jnp.minimum(nt - 1 - i, 1), 2)),
                  cur(512), cur(512), cur(LANE), cur(LANE), cur(LANE)],
        out_specs=[cur(768), cur(512), pl.BlockSpec((8, LANE), lambda s, i: (0, 0))],
        out_shape=[jax.ShapeDtypeStruct((t, 768), _ST), jax.ShapeDtypeStruct((t, 512), _ST),
                   jax.ShapeDtypeStruct((8, LANE), _F32)],
        scratch_shapes=[pltpu.VMEM((BLOCK, A_KV_WIDTH), _F32), pltpu.VMEM((BLOCK, A_KV_WIDTH), _F32)],
        compiler_params=_cparams(("arbitrary", "arbitrary")),
    )(sinks, qkv, qkv, za, doa, cos, sa, sb)


def _gla_chunk_terms(la, qkb_ref, r0):
    g = la[r0:r0 + B_CHUNK, :]
    ri = lax.broadcasted_iota(jnp.int32, (B_CHUNK, B_CHUNK), 0)
    ci = lax.broadcasted_iota(jnp.int32, (B_CHUNK, B_CHUNK), 1)
    cum = _dot_ones((ri >= ci).astype(_F32), g)
    last = cum[B_CHUNK - 1:B_CHUNK, :]
    mid = cum[B_CHUNK // 2 - 1:B_CHUNK // 2, :]
    q = qkb_ref[r0:r0 + B_CHUNK, 0:B_KEY_WIDTH].astype(_F32) * (B_KEY_DIM ** -0.5)
    k = qkb_ref[r0:r0 + B_CHUNK, B_KEY_WIDTH:2 * B_KEY_WIDTH].astype(_F32)
    e_q, e_k, e_l, e_c = jnp.exp(cum - mid), jnp.exp(mid - cum), jnp.exp(last - cum), jnp.exp(cum)
    dec_col = jnp.exp(jnp.sum(g.T, axis=1, keepdims=True))
    return dict(qm=q * e_q, km=k * e_k, kl=k * e_l, qc=q * e_c, e_q=e_q, e_k=e_k, e_l=e_l, e_c=e_c,
                dec_col=dec_col, dec_row=jnp.exp(last), causal=ri >= ci, ri=ri)


def _gate_logits(alr_ref, wup_ref, b_ref):
    return _dot(alr_ref[...], wup_ref[...]) + b_ref[...]


def _gla_fwd(qkb, vb, zb, alr, wup, b_alpha, gn, nseq):
    t = qkb.shape[0]
    tb = min(GLA_FWD_BLOCK, t // nseq)
    nblk = t // nseq // tb
    cpb = tb // B_CHUNK

    def body(qkb_ref, vb_ref, zb_ref, alr_ref, wup_ref, b_ref, gn_ref, ob_ref, oraw_ref, sst_ref, s_ref):
        @pl.when(pl.program_id(1) == 0)
        def _():
            s_ref[...] = jnp.zeros_like(s_ref)

        la = _log_sigmoid(_gate_logits(alr_ref, wup_ref, b_ref)) * (1.0 / B_GATE_TEMP)
        terms = [_gla_chunk_terms(la, qkb_ref, c * B_CHUNK) for c in range(cpb)]
        o_intra, inc = {}, {}
        for c, tm in enumerate(terms):
            for h in range(B_HEADS):
                kl_, vl_ = slice(h * 64, (h + 1) * 64), slice(h * 128, (h + 1) * 128)
                v = vb_ref[c * B_CHUNK:(c + 1) * B_CHUNK, vl_]
                a = jnp.where(tm["causal"], _dot_nt(tm["qm"][:, kl_], tm["km"][:, kl_]), 0.0)
                o_intra[c, h] = _dot(a, v)
                inc[c, h] = _dot_tn(tm["kl"][:, kl_], v)
        o_heads = {}
        for h in range(B_HEADS):
            kl_ = slice(h * 64, (h + 1) * 64)
            st = s_ref[kl_, :]
            for c, tm in enumerate(terms):
                sst_ref[c, kl_, :] = st
                o_heads[c, h] = o_intra[c, h] + _dot(tm["qc"][:, kl_], st)
                st = tm["dec_col"][kl_, :] * st + inc[c, h]
            s_ref[kl_, :] = st
        o = jnp.concatenate([jnp.concatenate([o_heads[c, h] for h in range(B_HEADS)], axis=1)
                             for c in range(cpb)], axis=0)
        oraw_ref[...] = o
        z = zb_ref[...].astype(_F32)
        gate = z * _sigmoid(z)
        for h in range(B_HEADS):
            vl_ = slice(h * 128, (h + 1) * 128)
            oh = o[:, vl_]
            r = lax.rsqrt(jnp.mean(oh * oh, axis=-1, keepdims=True) + NORM_EPS)
            ob_ref[:, vl_] = ((oh * r) * gn_ref[:, vl_] * gate[:, vl_]).astype(_ST)

    rows = lambda w: pl.BlockSpec((tb, w), lambda s, i: (s * nblk + i, 0))
    full = lambda a, b: pl.BlockSpec((a, b), lambda s, i: (0, 0))
    return pl.pallas_call(
        body, name="gla_fwd", grid=(nseq, nblk),
        in_specs=[rows(512), rows(512), rows(512), rows(LANE), full(LANE, B_KEY_WIDTH),
                  full(1, B_KEY_WIDTH), full(1, B_WIDTH)],
        out_specs=[rows(512), rows(512),
                   pl.BlockSpec((cpb, B_KEY_WIDTH, B_VAL_DIM), lambda s, i: (s * nblk + i, 0, 0))],
        out_shape=[jax.ShapeDtypeStruct((t, B_WIDTH), _ST), jax.ShapeDtypeStruct((t, B_WIDTH), _F32),
                   jax.ShapeDtypeStruct((t // B_CHUNK, B_KEY_WIDTH, B_VAL_DIM), _F32)],
        scratch_shapes=[pltpu.VMEM((B_KEY_WIDTH, B_VAL_DIM), _F32)],
        compiler_params=_cparams(("parallel", "arbitrary")),
    )(qkb, vb, zb, alr, wup, b_alpha, gn)


def _gla_bwd(qkb, vb, zb, alr, oraw, dob, sst, wup, b_alpha, gn, nseq):
    t = qkb.shape[0]
    tb = min(GLA_BLOCK, t // nseq)
    nblk = t // nseq // tb
    cpb = tb // B_CHUNK

    def body(qkb_ref, vb_ref, zb_ref, alr_ref, oraw_ref, dob_ref, sst_ref, wup_ref, b_ref, gn_ref,
             dqkb_ref, dvb_ref, dzb_ref, dalr_ref, dwup_ref, db_ref, dgn_ref, ds_ref):
        s_id, i = pl.program_id(0), pl.program_id(1)

        @pl.when((s_id == 0) & (i == 0))
        def _():
            dwup_ref[...] = jnp.zeros_like(dwup_ref)
            db_ref[...] = jnp.zeros_like(db_ref)
            dgn_ref[...] = jnp.zeros_like(dgn_ref)

        @pl.when(i == 0)
        def _():
            ds_ref[...] = jnp.zeros_like(ds_ref)

        a_pre = _gate_logits(alr_ref, wup_ref, b_ref)
        la = _log_sigmoid(a_pre) * (1.0 / B_GATE_TEMP)

        z = zb_ref[...].astype(_F32)
        sz = _sigmoid(z)
        d_ob = dob_ref[...].astype(_F32)
        tg = d_ob * (z * sz)
        dsilu = sz * (1.0 + z * (1.0 - sz))
        do_cols, dgn_cols = [], []
        for h in range(B_HEADS):
            vl_ = slice(h * 128, (h + 1) * 128)
            oh = oraw_ref[:, vl_].astype(_F32)
            r = lax.rsqrt(jnp.mean(oh * oh, axis=-1, keepdims=True) + NORM_EPS)
            on = oh * r
            gnh = gn_ref[:, vl_]
            dzb_ref[:, vl_] = (d_ob[:, vl_] * (on * gnh) * dsilu[:, vl_]).astype(_ST)
            dgn_cols.append(jnp.sum(tg[:, vl_] * on, axis=0, keepdims=True))
            do_cols.append(_rms_bwd(tg[:, vl_] * gnh, on, r))
        dgn_ref[...] = dgn_ref[...] + jnp.concatenate(dgn_cols, axis=1)
        d_o = jnp.concatenate(do_cols, axis=1)

        ri = lax.broadcasted_iota(jnp.int32, (tb, tb), 0)
        ci = lax.broadcasted_iota(jnp.int32, (tb, tb), 1)
        same = (ri // B_CHUNK) == (ci // B_CHUNK)
        low = same & (ri >= ci)
        cum = _dot_ones(low.astype(_F32), la)
        at_row = lambda r: jnp.concatenate([jnp.broadcast_to(cum[c * B_CHUNK + r:c * B_CHUNK + r + 1], (B_CHUNK, B_KEY_WIDTH))
                                            for c in range(cpb)], axis=0)
        last, mid = at_row(B_CHUNK - 1), at_row(B_CHUNK // 2 - 1)
        e_q, e_k, e_l, e_c = jnp.exp(cum - mid), jnp.exp(mid - cum), jnp.exp(last - cum), jnp.exp(cum)
        q = qkb_ref[:, 0:B_KEY_WIDTH] * (B_KEY_DIM ** -0.5)
        k = qkb_ref[:, B_KEY_WIDTH:2 * B_KEY_WIDTH]
        qm, km, kl, qc = q * e_q, k * e_k, k * e_l, q * e_c
        lane_head = lax.broadcasted_iota(jnp.int32, (1, B_KEY_WIDTH), 1) // B_KEY_DIM
        d_o_mx = d_o.astype(_MX)

        def on_diagonal(st):
            z = jnp.zeros((B_KEY_DIM, B_VAL_DIM), st.dtype)
            return jnp.concatenate([jnp.concatenate(
                [st[h * B_KEY_DIM:(h + 1) * B_KEY_DIM] if g == h else z for g in range(B_HEADS)], axis=1)
                for h in range(B_HEADS)], axis=0)

        def diagonal_of(full):
            return jnp.concatenate([full[h * B_KEY_DIM:(h + 1) * B_KEY_DIM, h * B_VAL_DIM:(h + 1) * B_VAL_DIM]
                                    for h in range(B_HEADS)], axis=0)

        dqm, dkm, dv_cols = None, None, []
        for h in range(B_HEADS):
            vl_ = slice(h * B_VAL_DIM, (h + 1) * B_VAL_DIM)
            mine = lane_head == h
            qz, kz = jnp.where(mine, qm, 0.0).astype(_MX), jnp.where(mine, km, 0.0).astype(_MX)
            a = jnp.where(low, _dot_nt(qz, kz), 0.0).astype(_MX)
            da = jnp.where(low, _dot_nt(d_o_mx[:, vl_], vb_ref[:, vl_]), 0.0).astype(_MX)
            dqm_h, dkm_h = _dot(da, kz), _dot_tn(da, qz)
            dqm = dqm_h if dqm is None else dqm + dqm_h
            dkm = dkm_h if dkm is None else dkm + dkm_h
            dv_cols.append(_dot_tn(a, d_o_mx[:, vl_]))
        dv = jnp.concatenate(dv_cols, axis=1)

        chunk = [slice(c * B_CHUNK, (c + 1) * B_CHUNK) for c in range(cpb)]
        dqc_rows, g_loc = [], []
        for c in range(cpb):
            dqc_rows.append(_dot_nt(d_o_mx[chunk[c]], on_diagonal(sst_ref[c].astype(_MX))))
            g_loc.append(diagonal_of(_dot_tn(qc[chunk[c]], d_o_mx[chunk[c]])))
        cur = ds_ref[...]
        d_state = [None] * cpb
        for c in reversed(range(cpb)):
            d_state[c] = cur
            cur = g_loc[c] + jnp.exp(jnp.sum(la[chunk[c]].T, axis=1, keepdims=True)) * cur
        ds_ref[...] = cur
        dkl_rows, dv_rows, dlast_rows = [], [], []
        ones8 = jnp.ones((8, B_VAL_DIM), _F32)
        for c in range(cpb):
            dsd = on_diagonal(d_state[c].astype(_MX))
            dkl_c = _dot_nt(vb_ref[chunk[c], :], dsd)
            dkl_rows.append(dkl_c)
            dv_rows.append(_dot(kl[chunk[c]], dsd))
            prod = d_state[c] * sst_ref[c]
            p0 = prod.astype(jnp.bfloat16)
            p1 = (prod - p0.astype(_F32)).astype(jnp.bfloat16)
            ddec = (_dot_nt(ones8, p0) + _dot_nt(ones8, p1))[0:1]
            r_last = c * B_CHUNK + B_CHUNK - 1
            dlast = jnp.sum(dkl_c * kl[chunk[c]], axis=0, keepdims=True) + ddec * jnp.exp(last[r_last:r_last + 1])
            dlast_rows.append(jnp.broadcast_to(dlast, (B_CHUNK, B_KEY_WIDTH)))
        dqc, dkl = jnp.concatenate(dqc_rows, axis=0), jnp.concatenate(dkl_rows, axis=0)
        dqkb_ref[:, 0:B_KEY_WIDTH] = ((dqm * e_q + dqc * e_c) * (B_KEY_DIM ** -0.5)).astype(_ST)
        dqkb_ref[:, B_KEY_WIDTH:2 * B_KEY_WIDTH] = (dkm * e_k + dkl * e_l).astype(_ST)
        dvb_ref[...] = (dv + jnp.concatenate(dv_rows, axis=0)).astype(_ST)
        dcum = dqm * qm - dkm * km + dqc * qc - dkl * kl
        row = lax.broadcasted_iota(jnp.int32, (tb, B_KEY_WIDTH), 0)
        dcum = jnp.where(row % B_CHUNK == B_CHUNK - 1, dcum + jnp.concatenate(dlast_rows, axis=0), dcum)
        dla = _dot_ones((same & (ri <= ci)).astype(_F32), dcum)

        da_pre = dla * (1.0 / B_GATE_TEMP) * (1.0 - _sigmoid(a_pre))
        dalr_ref[...] = _dot_nt(da_pre, wup_ref[...]).astype(_ST)
        dwup_ref[...] = dwup_ref[...] + _dot_tn(alr_ref[...], da_pre)
        db_ref[...] = db_ref[...] + jnp.sum(da_pre, axis=0, keepdims=True)

    blk = lambda s, i: s * nblk + nblk - 1 - i
    rows = lambda w: pl.BlockSpec((tb, w), lambda s, i: (blk(s, i), 0))
    full = lambda a, b: pl.BlockSpec((a, b), lambda s, i: (0, 0))
    act = lambda w: jax.ShapeDtypeStruct((t, w), _ST)
    return pl.pallas_call(
        body, name="gla_bwd", grid=(nseq, nblk),
        in_specs=[rows(512), rows(512), rows(512), rows(LANE), rows(512), rows(512),
                  pl.BlockSpec((cpb, B_KEY_WIDTH, B_VAL_DIM), lambda s, i: (blk(s, i), 0, 0)),
                  full(LANE, B_KEY_WIDTH), full(1, B_KEY_WIDTH), full(1, B_WIDTH)],
        out_specs=[rows(512), rows(512), rows(512), rows(LANE), full(LANE, B_KEY_WIDTH),
                   full(1, B_KEY_WIDTH), full(1, B_WIDTH)],
        out_shape=[act(512), act(512), act(512), act(LANE),
                   jax.ShapeDtypeStruct((LANE, B_KEY_WIDTH), _F32),
                   jax.ShapeDtypeStruct((1, B_KEY_WIDTH), _F32), jax.ShapeDtypeStruct((1, B_WIDTH), _F32)],
        scratch_shapes=[pltpu.VMEM((B_KEY_WIDTH, B_VAL_DIM), _F32)],
        compiler_params=_cparams(("arbitrary", "arbitrary")),
    )(qkb, vb, zb, alr, oraw, dob, sst, wup, b_alpha, gn)


def _merge_loss(oa, ob, ga, gb, x2, tgt, wa, wb, wo, g_final):
    t = x2.shape[0]
    tm = min(t, 512)
    nt = t // tm

    def body(oa_ref, ob_ref, ga_ref, gb_ref, x_ref, t_ref, wa_ref, wb_ref, wo_ref, gf_ref,
             dh_ref, doa_ref, dob_ref, dga_ref, dgb_ref, dwa_ref, dwb_ref, dwo_ref, dgf_ref, loss_ref,
             ya_s, yb_s, out_s, dmer_s, mrg_s, dya_s, dyb_s):
        first = pl.program_id(0) == 0
        so_far = lambda ref: jnp.where(first, 0.0, ref[...])

        slabs = [slice(s, s + MERGE_SLAB) for s in range(0, tm, MERGE_SLAB)]
        fold = lambda a: a[0:8] + a[8:16]
        ya_s[...] = _dot(oa_ref[...], wa_ref[...])
        yb_s[...] = _dot(ob_ref[...], wb_ref[...])
        for rows_ in slabs:
            sga, sgb = _sigmoid(ga_ref[rows_, :].astype(_F32)), _sigmoid(gb_ref[rows_, :].astype(_F32))
            mrg_s[rows_, :] = (sga * ya_s[rows_, :] + sgb * yb_s[rows_, :]).astype(_MX)
        out_s[...] = x_ref[...] + _dot(mrg_s[...], wo_ref[...])
        gf = gf_ref[...]
        loss8 = jnp.zeros((8, D_MODEL), _F32)
        dgf8 = jnp.zeros((8, D_MODEL), _F32)
        for rows_ in slabs:
            out = out_s[rows_, :]
            r = lax.rsqrt(jnp.mean(out * out, axis=-1, keepdims=True) + NORM_EPS)
            nrm = out * r
            err = nrm * gf - t_ref[rows_, :]
            loss8 = loss8 + fold(err * err)
            dy = err * (1.0 / D_MODEL)
            dgf8 = dgf8 + fold(dy * nrm)
            dh = _rms_bwd(dy * gf, nrm, r)
            dh_ref[rows_, :] = dh.astype(_ST)
        loss_ref[...] = so_far(loss_ref) + (0.5 / D_MODEL) * jnp.sum(loss8, axis=0, keepdims=True)
        dgf_ref[...] = so_far(dgf_ref) + jnp.sum(dgf8, axis=0, keepdims=True)
        dmer_s[...] = _dot_nt(dh_ref[...], wo_ref[...])
        dwo_ref[...] = so_far(dwo_ref) + _dot_tn(mrg_s[...], dh_ref[...])
        for rows_ in slabs:
            sga, sgb = _sigmoid(ga_ref[rows_, :].astype(_F32)), _sigmoid(gb_ref[rows_, :].astype(_F32))
            dmer = dmer_s[rows_, :]
            da, db = dmer * sga, dmer * sgb
            dya_s[rows_, :] = da.astype(_MX)
            dyb_s[rows_, :] = db.astype(_MX)
            dga_ref[rows_, :] = (da * ya_s[rows_, :] * (1.0 - sga)).astype(_ST)
            dgb_ref[rows_, :] = (db * yb_s[rows_, :] * (1.0 - sgb)).astype(_ST)
        doa_ref[...] = _dot_nt(dya_s[...], wa_ref[...]).astype(_ST)
        dob_ref[...] = _dot_nt(dyb_s[...], wb_ref[...]).astype(_ST)
        dwa_ref[...] = so_far(dwa_ref) + _dot_tn(oa_ref[...], dya_s[...])
        dwb_ref[...] = so_far(dwb_ref) + _dot_tn(ob_ref[...], dyb_s[...])

    rows = lambda w: pl.BlockSpec((tm, w), lambda i: (i, 0))
    full = lambda a, b: pl.BlockSpec((a, b), lambda i: (0, 0), pipeline_mode=pl.Buffered(1))
    return pl.pallas_call(
        body, name="merge_loss", grid=(nt,),
        in_specs=[rows(512), rows(512), rows(D_MODEL), rows(D_MODEL), rows(D_MODEL), rows(D_MODEL),
                  full(A_WIDTH, D_MODEL), full(B_WIDTH, D_MODEL), full(D_MODEL, D_MODEL), full(1, D_MODEL)],
        out_specs=[rows(D_MODEL), rows(512), rows(512), rows(D_MODEL), rows(D_MODEL),
                   full(A_WIDTH, D_MODEL), full(B_WIDTH, D_MODEL), full(D_MODEL, D_MODEL),
                   full(1, D_MODEL), full(1, D_MODEL)],
        out_shape=[jax.ShapeDtypeStruct((t, D_MODEL), _ST), jax.ShapeDtypeStruct((t, 512), _ST),
                   jax.ShapeDtypeStruct((t, 512), _ST), jax.ShapeDtypeStruct((t, D_MODEL), _ST),
                   jax.ShapeDtypeStruct((t, D_MODEL), _ST),
                   jax.ShapeDtypeStruct((A_WIDTH, D_MODEL), _F32), jax.ShapeDtypeStruct((B_WIDTH, D_MODEL), _F32),
                   jax.ShapeDtypeStruct((D_MODEL, D_MODEL), _F32), jax.ShapeDtypeStruct((1, D_MODEL), _F32),
                   jax.ShapeDtypeStruct((1, D_MODEL), _F32)],
        scratch_shapes=[pltpu.VMEM((tm, D_MODEL), _F32)] * 4 + [pltpu.VMEM((tm, D_MODEL), _MX)] * 3,
        compiler_params=_cparams(("arbitrary",), VMEM_LIMIT),
    )(oa, ob, ga, gb, x2, tgt, wa, wb, wo, g_final)


def _in_proj_bwd_x(dpieces, wt, x2, dh2, g_in):
    t = x2.shape[0]
    tm = min(t, 512)
    np_ = len(PIECES)

    def body(*refs):
        dp_refs = refs[:np_]
        w_ref, x_ref, dh2_ref, g_ref, gx_ref, dg_ref = refs[np_:]

        @pl.when(pl.program_id(0) == 0)
        def _():
            dg_ref[...] = jnp.zeros_like(dg_ref)

        dh = None
        for (name, a, b), dp in zip(PIECES, dp_refs):
            part = _dot(dp[...], w_ref[a:b, :])
            dh = part if dh is None else dh + part
        xv = x_ref[...]
        r = lax.rsqrt(jnp.mean(xv * xv, axis=-1, keepdims=True) + NORM_EPS)
        nrm = xv * r
        dg_ref[...] = dg_ref[...] + jnp.sum(dh * nrm, axis=0, keepdims=True)
        gx_ref[...] = dh2_ref[...].astype(_F32) + _rms_bwd(dh * g_ref[...], nrm, r)

    rows = lambda w: pl.BlockSpec((tm, w), lambda i: (i, 0))
    full = lambda a, b: pl.BlockSpec((a, b), lambda i: (0, 0), pipeline_mode=pl.Buffered(1))
    return pl.pallas_call(
        body, name="in_proj_bwd_x", grid=(t // tm,),
        in_specs=[rows(b - a) for _, a, b in PIECES] + [full(D_IN, D_MODEL), rows(D_MODEL), rows(D_MODEL),
                                                          full(1, D_MODEL)],
        out_specs=[rows(D_MODEL), full(1, D_MODEL)],
        out_shape=[jax.ShapeDtypeStruct((t, D_MODEL), _F32), jax.ShapeDtypeStruct((1, D_MODEL), _F32)],
        compiler_params=_cparams(("arbitrary",), VMEM_LIMIT),
    )(*dpieces, wt, x2, dh2, g_in)


def _in_proj_bwd_w(h, dpieces, others, osplits):
    t = h.shape[0]
    tm = min(t, 1024)
    nt = t // tm
    np_, no = len(PIECES), len(others)
    half = D_MODEL // 2

    def body(*refs):
        h_ref, dp_refs, o_refs = refs[0], refs[1:1 + np_], refs[1 + np_:1 + np_ + no]
        mine_ref, theirs_ref = refs[1 + np_ + no:3 + np_ + no]
        r_refs = refs[3 + np_ + no:3 + np_ + 2 * no]
        acc_ref, keep_sem, send, recv, o_send, o_recv = refs[3 + np_ + 2 * no:]
        i = pl.program_id(0)
        x, y, c = _place()
        sibling = (x, y, 1 - c)
        early = [pltpu.make_async_remote_copy(
            src_ref=osplits[k].half(o_refs[k], 1 - c), dst_ref=r_refs[k], send_sem=o_send.at[k], recv_sem=o_recv.at[k],
            device_id=sibling, device_id_type=_MESH) for k in range(no)]

        @pl.when(i == 0)
        def _():
            for cp in early:
                cp.start()

        hv = h_ref[...]
        cols = lambda core: pl.ds(pl.multiple_of(core * half, LANE), half)
        writes = []
        by_size = sorted(range(np_), key=lambda j: PIECES[j][1] - PIECES[j][2])
        for j, ((name, a, b), dp) in [(j, (PIECES[j], dp_refs[j])) for j in by_size]:
            part = _dot_tn(dp[...], hv)
            if name == "alr":
                b = a + B_GATE_RANK
                part = part[0:B_GATE_RANK]
            acc_ref[a:b, :] = jnp.where(i == 0, 0.0, acc_ref[a:b, :]) + part
            keep = pltpu.make_async_copy(acc_ref.at[a:b, cols(c)], mine_ref.at[a:b], keep_sem.at[j])
            give = pltpu.make_async_remote_copy(
                src_ref=acc_ref.at[a:b, cols(1 - c)], dst_ref=theirs_ref.at[a:b], send_sem=send.at[j],
                recv_sem=recv.at[j], device_id=sibling, device_id_type=_MESH)
            writes += [keep, give]

            @pl.when(i == nt - 1)
            def _(keep=keep, give=give):
                keep.start()
                give.start()

        @pl.when(i == nt - 1)
        def _():
            for cp in writes + early:
                cp.wait()

    rows = lambda w: pl.BlockSpec((tm, w), lambda i: (i, 0))
    halves = [jax.ShapeDtypeStruct((D_IN, half), _F32)] * 2
    return pl.pallas_call(
        body, name="in_proj_bwd_w", grid=(nt,),
        in_specs=[rows(D_MODEL)] + [rows(b - a) for _, a, b in PIECES] + [_ANY] * no,
        out_specs=[_ANY] * (2 + no),
        out_shape=halves + [jax.ShapeDtypeStruct(sp.half_shape(g.shape), g.dtype) for g, sp in zip(others, osplits)],
        scratch_shapes=[pltpu.VMEM((D_IN, D_MODEL), _F32), pltpu.SemaphoreType.DMA((np_,)),
                        pltpu.SemaphoreType.DMA((np_,)), pltpu.SemaphoreType.DMA((np_,)),
                        pltpu.SemaphoreType.DMA((no,)), pltpu.SemaphoreType.DMA((no,))],
        compiler_params=_cparams(("arbitrary",), VMEM_LIMIT),
    )(h, *dpieces, *others)


def _place():
    return lax.axis_index("x"), lax.axis_index("y"), lax.axis_index("c")


def _other_chips(x, y):
    return [(1 - x, y), (x, 1 - y), (1 - x, 1 - y)]


class _Split(NamedTuple):
    by_rows: bool
    step: int
    size: int

    def half(self, ref, c):
        r, n = ref.shape[-2:]
        if self.by_rows:
            return ref.at[:, pl.ds(pl.multiple_of(c * (n // 2), LANE), n // 2)]
        return ref.at[pl.ds(pl.multiple_of(c * (r // 2), 16), r // 2), :]

    def chip_part(self, ref, k):
        if self.by_rows:
            return ref.at[pl.ds(pl.multiple_of(k * self.step, 16), self.size), :]
        return ref.at[:, pl.ds(pl.multiple_of(k * self.size, LANE), self.size)]

    def half_shape(self, shape):
        r, n = shape
        return (r, n // 2) if self.by_rows else (r // 2, n)

    def part_shape(self, shape):
        r, n = shape
        return (self.size, n) if self.by_rows else (r, self.size)


SPLIT_W_IN_T = _Split(True, WINDOW_STEP, WINDOW_ROWS)
SPLIT_W_O = _Split(True, 256, 256)
SPLIT_W_OUT = _Split(False, 256, 256)


def _gather_weights(shards, splits, fulls, pos_f):
    nw = len(shards)
    t = pos_f.shape[0]

    def body(*refs):
        ins, (pos_ref, c_ref) = refs[:nw], refs[nw:nw + 2]
        outs, tables = refs[nw + 2:2 * nw + 2], refs[2 * nw + 2:2 * nw + 5]
        send_a, recv_a, send_b, recv_b = refs[2 * nw + 5:]
        x, y, c = _place()
        me = 2 * x + y
        peers = _other_chips(x, y)

        def place(i, k, half):
            if splits[i] is None:
                return outs[i].at[k]
            if fulls[i][0] == 4 and len(fulls[i]) == 3:
                whole = outs[i].at[k]
            else:
                whole = splits[i].chip_part(outs[i], k)
            return splits[i].half(whole, half)

        first, passed = [], []
        for i in range(nw):
            src = ins[i] if splits[i] is None else splits[i].half(ins[i], c)
            for j, (px, py) in enumerate(peers):
                cp = pltpu.make_async_remote_copy(
                    src_ref=src, dst_ref=place(i, me, c), send_sem=send_a.at[3 * i + j],
                    recv_sem=recv_a.at[3 * i + j], device_id=(px, py, c), device_id_type=_MESH)
                cp.start()
                first.append(cp)
        _rope_tables_into(pos_ref, c_ref, *tables)
        for i in range(nw):
            for j, (px, py) in enumerate(peers):
                landed = place(i, 2 * px + py, c)
                pltpu.make_async_remote_copy(
                    src_ref=landed, dst_ref=landed, send_sem=send_a.at[3 * i + j], recv_sem=recv_a.at[3 * i + j],
                    device_id=(px, py, c), device_id_type=_MESH).wait_recv()
                if splits[i] is not None:
                    cp = pltpu.make_async_remote_copy(
                        src_ref=landed, dst_ref=landed, send_sem=send_b.at[3 * i + j], recv_sem=recv_b.at[3 * i + j],
                        device_id=(x, y, 1 - c), device_id_type=_MESH)
                    cp.start()
                    passed.append(cp)
        for i in range(nw):
            if splits[i] is None:
                continue
            for j, (px, py) in enumerate(peers):
                theirs = place(i, 2 * px + py, 1 - c)
                pltpu.make_async_remote_copy(
                    src_ref=theirs, dst_ref=theirs, send_sem=send_b.at[3 * i + j], recv_sem=recv_b.at[3 * i + j],
                    device_id=(x, y, 1 - c), device_id_type=_MESH).wait_recv()
        for cp in first + passed:
            cp.wait_send()

    vm = pl.BlockSpec(memory_space=pltpu.VMEM)
    tab = jax.ShapeDtypeStruct((t, LANE), _F32)
    return pl.pallas_call(
        body, name="gather_weights",
        in_specs=[_ANY] * nw + [vm, vm], out_specs=[_ANY] * nw + [vm] * 3,
        out_shape=[jax.ShapeDtypeStruct(f, s.dtype) for f, s in zip(fulls, shards)] + [tab] * 3,
        scratch_shapes=[pltpu.SemaphoreType.DMA((3 * nw,)) for _ in range(4)],
        compiler_params=_cparams(None, VMEM_LIMIT),
    )(*shards, pos_f, _rope_consts())


def _assemble_w_in_t(slots):
    bw = 256
    ov = WINDOW_ROWS - WINDOW_STEP

    def body(s_ref, o_ref):
        for k in range(4):
            base = k * WINDOW_STEP
            lo = 0 if k == 0 else ov
            if k > 0:
                o_ref[base:base + ov, :] = s_ref[k - 1, WINDOW_STEP:WINDOW_ROWS, :] + s_ref[k, 0:ov, :]
            hi = WINDOW_ROWS if k == 3 else WINDOW_STEP
            o_ref[base + lo:base + hi, :] = s_ref[k, lo:hi, :]

    return pl.pallas_call(
        body, name="assemble_w_in_t", grid=(D_MODEL // bw,),
        in_specs=[pl.BlockSpec((4, WINDOW_ROWS, bw), lambda i: (0, 0, i))],
        out_specs=pl.BlockSpec((D_IN, bw), lambda i: (0, i)),
        out_shape=jax.ShapeDtypeStruct((D_IN, D_MODEL), slots.dtype),
        compiler_params=_cparams(("parallel",)),
    )(slots)


def _row_block(rows):
    for cand in (976, 176, 256, 128):
        if rows % cand == 0:
            return cand
    return rows


def _pair_sum(g, r, split, c_arr, name):
    hr, hn = r.shape
    br = _row_block(hr)
    if split is None:
        g_spec = pl.BlockSpec((br, hn), lambda i, c_ref: (i, 0))
    elif split.by_rows:
        g_spec = pl.BlockSpec((br, hn), lambda i, c_ref: (i, c_ref[0]))
    else:
        g_spec = pl.BlockSpec((br, hn), lambda i, c_ref: (c_ref[0] * (hr // br) + i, 0))

    def body(c_ref, g_ref, r_ref, o_ref):
        o_ref[...] = (g_ref[...] + r_ref[...]).astype(o_ref.dtype)

    return pl.pallas_call(
        body, name=name,
        grid_spec=pltpu.PrefetchScalarGridSpec(
            num_scalar_prefetch=1, grid=(hr // br,),
            in_specs=[g_spec, pl.BlockSpec((br, hn), lambda i, c_ref: (i, 0))],
            out_specs=pl.BlockSpec((br, hn), lambda i, c_ref: (i, 0))),
        out_shape=jax.ShapeDtypeStruct(r.shape, _MX),
        compiler_params=_cparams(("parallel",)),
    )(c_arr, g, r)


_HBM = pl.BlockSpec(memory_space=pltpu.HBM)
_SEM = pl.BlockSpec(memory_space=pltpu.SEMAPHORE)
_FLOWS = pltpu.SideEffectType.DATAFLOW_SIDE_EFFECTING


def _chip_exchange_copies(refs, send, recv, splits):
    nw = len(refs) // 2
    x, y, c = _place()
    me = 2 * x + y
    copies = []
    for i in range(nw):
        for px, py in _other_chips(x, y):
            copies.append((splits[i].chip_part(refs[i], 2 * px + py), refs[nw + i].at[me], (px, py, c)))
    return [pltpu.make_async_remote_copy(src_ref=src, dst_ref=dst, send_sem=send.at[k], recv_sem=recv.at[k],
                                         device_id=peer, device_id_type=_MESH)
            for k, (src, dst, peer) in enumerate(copies)]


def _late_gather_copies(refs, send, recv, splits):
    nw = len(refs) // 2
    x, y, c = _place()
    me = 2 * x + y
    copies = []
    for i in range(nw):
        for px, py in [(x, y)] + _other_chips(x, y):
            copies.append((refs[i], splits[i].chip_part(refs[nw + i], me), (px, py, c)))
    return [pltpu.make_async_remote_copy(src_ref=src, dst_ref=dst, send_sem=send.at[k], recv_sem=recv.at[k],
                                         device_id=peer, device_id_type=_MESH)
            for k, (src, dst, peer) in enumerate(copies)]


def _start_copies(name, flying, copies_of, n_copies, after=None):
    first = [] if after is None else [after]

    def body(*refs):
        ins = refs[:len(flying)]
        send, recv = refs[len(flying) + len(first):len(flying) + len(first) + 2]
        token = refs[-1]
        for cp in copies_of(ins, send, recv):
            cp.start()
        token[...] = jnp.zeros_like(token)

    outs = pl.pallas_call(
        body, name=name,
        in_specs=[_HBM] * len(flying) + [_ANY] * len(first),
        out_specs=[_SEM, _SEM] + [_HBM] * len(flying) + [pl.BlockSpec(memory_space=pltpu.VMEM)],
        out_shape=[pltpu.SemaphoreType.DMA((n_copies,)), pltpu.SemaphoreType.DMA((n_copies,))]
        + [pltpu.HBM(f.shape, f.dtype) for f in flying] + [jax.ShapeDtypeStruct((8, LANE), _F32)],
        input_output_aliases={i: 2 + i for i in range(len(flying))},
        compiler_params=pltpu.CompilerParams(has_side_effects=_FLOWS),
    )(*[pltpu.with_memory_space_constraint(f, pltpu.HBM) for f in flying], *first)
    return outs[0], outs[1], outs[2:2 + len(flying)], outs[-1]


def _wait_copies(name, send, recv, flying, copies_of, after):
    def body(*refs):
        ins = refs[:len(flying)]
        send_ref, recv_ref = refs[len(flying):len(flying) + 2]
        for cp in copies_of(ins, send_ref, recv_ref):
            cp.wait_send()
            cp.wait_recv()

    return pl.pallas_call(
        body, name=name,
        in_specs=[_HBM] * len(flying) + [_SEM, _SEM, _ANY],
        out_specs=[_HBM] * len(flying),
        out_shape=[pltpu.HBM(f.shape, f.dtype) for f in flying],
        input_output_aliases={i: i for i in range(len(flying))},
        compiler_params=pltpu.CompilerParams(has_side_effects=_FLOWS),
    )(*flying, send, recv, after)


def _sum_chips(q, p, split, place_arr, name):
    _, hr, hn = q.shape
    if split.by_rows:
        out_shape = (hr, 2 * hn)
        o_spec = pl.BlockSpec((hr, hn), lambda i, pr: (0, pr[0]))
        p_spec = pl.BlockSpec((pl.Element(hr), pl.Element(hn)), lambda i, pr: (pr[1] * split.step, 0))
    else:
        out_shape = (2 * hr, hn)
        o_spec = pl.BlockSpec((hr, hn), lambda i, pr: (pr[0], 0))
        p_spec = pl.BlockSpec((hr, hn), lambda i, pr: (0, pr[1]))

    def body(pr, q_ref, p_ref, o_ref):
        f = lambda k: jnp.where(pr[1] == k, p_ref[...], q_ref[k]).astype(_F32)
        o_ref[...] = ((f(0) + f(1)) + f(2)) + f(3)

    return pl.pallas_call(
        body, name=name,
        grid_spec=pltpu.PrefetchScalarGridSpec(
            num_scalar_prefetch=1, grid=(1,),
            in_specs=[pl.BlockSpec((4, hr, hn), lambda i, pr: (0, 0, 0)), p_spec], out_specs=o_spec),
        out_shape=jax.ShapeDtypeStruct(out_shape, _F32),
        compiler_params=_cparams(("arbitrary",), VMEM_LIMIT),
    )(place_arr, q, p)


def _pair_share(bufs, splits, small):
    nw = len(bufs)

    def body(*refs):
        ins, small_ref, outs, all_ref = refs[:nw], refs[nw], refs[nw + 1:2 * nw + 1], refs[2 * nw + 1]
        send, recv, s_send, s_recv = refs[2 * nw + 2:]
        x, y, c = _place()
        copies = []
        for r in range(1, 8):
            peer = (1 - x if r & 4 else x, 1 - y if r & 2 else y, 1 - c if r & 1 else c)
            cp = pltpu.make_async_remote_copy(
                src_ref=small_ref, dst_ref=all_ref.at[4 * x + 2 * y + c], send_sem=s_send.at[r - 1],
                recv_sem=s_recv.at[r - 1], device_id=peer, device_id_type=_MESH)
            cp.start()
            copies.append(cp)
        for i in range(nw):
            cp = pltpu.make_async_remote_copy(
                src_ref=splits[i].half(ins[i], c), dst_ref=splits[i].half(outs[i], c), send_sem=send.at[i],
                recv_sem=recv.at[i], device_id=(x, y, 1 - c), device_id_type=_MESH)
            cp.start()
            copies.append(cp)
        for cp in copies:
            cp.wait()

    return pl.pallas_call(
        body, name="grad_pair_share",
        in_specs=[_ANY] * (nw + 1), out_specs=[_ANY] * (nw + 1),
        out_shape=[jax.ShapeDtypeStruct(b.shape, b.dtype) for b in bufs]
        + [jax.ShapeDtypeStruct((8,) + small.shape, small.dtype)],
        input_output_aliases={i: i for i in range(nw)},
        scratch_shapes=[pltpu.SemaphoreType.DMA((nw,)), pltpu.SemaphoreType.DMA((nw,)),
                        pltpu.SemaphoreType.DMA((7,)), pltpu.SemaphoreType.DMA((7,))],
    )(*bufs, small)


def _sum_devices(parts, own, dev_arr):
    def body(dev, p_ref, own_ref, tot_ref):
        f = lambda d: jnp.where(dev[0] == d, own_ref[...], p_ref[d])
        acc = f(0)
        for d in range(1, 8):
            acc = acc + f(d)
        tot_ref[...] = acc

    return pl.pallas_call(
        body, name="small_sum",
        grid_spec=pltpu.PrefetchScalarGridSpec(
            num_scalar_prefetch=1, grid=(1,),
            in_specs=[pl.BlockSpec(parts.shape, lambda i, dev: (0, 0, 0)), pl.BlockSpec(own.shape, lambda i, dev: (0, 0))],
            out_specs=pl.BlockSpec(own.shape, lambda i, dev: (0, 0))),
        out_shape=jax.ShapeDtypeStruct(own.shape, own.dtype),
    )(dev_arr, parts, own)


def _adam_update(w, g, m, v):
    m2 = ADAM_B1 * m + (1.0 - ADAM_B1) * g
    v2 = ADAM_B2 * v + (1.0 - ADAM_B2) * (g * g)
    m_hat = m2 / (1.0 - ADAM_B1 ** ADAM_STEP)
    v_hat = v2 / (1.0 - ADAM_B2 ** ADAM_STEP)
    return -ADAM_LR * (m_hat / (jnp.sqrt(v_hat) + ADAM_EPS) + ADAM_WD * w), m2, v2


SMALL_AT = dict(g_in=(0, 0), g_final=(1, 0), g_gla_norm=(2, 0), b_alpha=(2, B_WIDTH), attn_sinks=(2, B_WIDTH + B_KEY_WIDTH))
LOSS_AT = (2, B_WIDTH + B_KEY_WIDTH + LANE)
WUP_ROWS = (3, 7)


def _adamw_small(tot, g_wup, params):
    names = list(params)

    def body(*refs):
        tot_ref, gw_ref = refs[0], refs[1]
        ins = refs[2:2 + 3 * len(names)]
        outs = refs[2 + 3 * len(names):]
        for i, nm in enumerate(names):
            w_ref, m_ref, v_ref = ins[3 * i:3 * i + 3]
            if nm in SMALL_AT:
                r, a = SMALL_AT[nm]
                g = tot_ref[r:r + 1, a:a + w_ref.shape[1]]
            else:
                g = gw_ref[...]
            d, m2, v2 = _adam_update(w_ref[...], g, m_ref[...], v_ref[...])
            for o_ref, val in zip(outs[4 * i:4 * i + 4], (g, d, m2, v2)):
                o_ref[...] = val

    vm = pl.BlockSpec(memory_space=pltpu.VMEM)
    flat = [a for nm in names for a in params[nm]]
    out_shape = [jax.ShapeDtypeStruct(params[nm][0].shape, _F32) for nm in names for _ in range(4)]
    outs = pl.pallas_call(
        body, name="adamw_small", in_specs=[vm] * (2 + len(flat)), out_specs=[vm] * len(out_shape), out_shape=out_shape,
    )(tot, g_wup, *flat)
    return {nm: tuple(outs[4 * i:4 * i + 4]) for i, nm in enumerate(names)}


def _adamw(w, g, m, v, name):
    lead = w.shape[0] != 1
    r, n = (w.shape[0], w.shape[2]) if lead else w.shape[1:]
    br = r
    for cand in (256, 244, 128):
        if r > cand and r % cand == 0:
            br = cand
            break

    def body(w_ref, g_ref, m_ref, v_ref, d_ref, nm_ref, nv_ref):
        d_ref[...], nm_ref[...], nv_ref[...] = _adam_update(w_ref[...], g_ref[...], m_ref[...], v_ref[...])

    blk = pl.BlockSpec((br, 1, n), lambda i: (i, 0, 0)) if lead else pl.BlockSpec((None, br, n), lambda i: (0, i, 0))
    shp = jax.ShapeDtypeStruct(w.shape, _F32)
    return pl.pallas_call(
        body, name=name, grid=(r // br,),
        in_specs=[blk] * 4, out_specs=[blk] * 3, out_shape=[shp] * 3,
        compiler_params=_cparams(("parallel",)),
    )(w, g, m, v)


def kernel(x, positions, g_in, w_in, w_alpha_up, b_alpha, attn_sinks, g_gla_norm, w_out_a, w_out_b, w_o, g_final, loss_target, m_g_in, m_w_in, m_w_alpha_up, m_b_alpha, m_attn_sinks, m_g_gla_norm, m_w_out_a, m_w_out_b, m_w_o, m_g_final, v_g_in, v_w_in, v_w_alpha_up, v_b_alpha, v_attn_sinks, v_g_gla_norm, v_w_out_a, v_w_out_b, v_w_o, v_g_final):
    nseq, seq, _ = x.shape
    t = nseq * seq
    cx, cy, cc = _place()
    chip = 2 * cx + cy
    c_arr = jnp.reshape(cc, (1,)).astype(jnp.int32)

    tr = lambda w: jnp.transpose(w, (2, 0, 1))
    w_in_t = tr(w_in).reshape(SHARD, D_MODEL).astype(_MX)
    pad = WINDOW_ROWS - SHARD
    window = lax.switch(chip, [lambda w, k=k: jnp.pad(w, ((4 * k, pad - 4 * k), (0, 0))) for k in range(4)], w_in_t)
    shards = [window, w_alpha_up[0].astype(_MX)]
    late = [w_out_a[0].astype(_MX), w_out_b[0].astype(_MX), w_o[0].astype(_MX)]
    late_splits = [SPLIT_W_OUT, SPLIT_W_OUT, SPLIT_W_O]
    splits = [SPLIT_W_IN_T, None]
    fulls = [(4, WINDOW_ROWS, D_MODEL), (4, B_GATE_RANK, B_KEY_WIDTH // 4)]
    pos_f = positions.astype(_F32).reshape(t, 1)
    win_g, wup_g, cos, sa, sb = _gather_weights(shards, splits, fulls, pos_f)
    late_copies = lambda refs, send, recv: _late_gather_copies(refs, send, recv, late_splits)
    late_full = [lax.empty(shape, _MX) for shape in ((A_WIDTH, D_MODEL), (B_WIDTH, D_MODEL), (D_MODEL, D_MODEL))]
    l_send, l_recv, l_flying, l_token = _start_copies("late_gather_start", late + late_full, late_copies,
                                                      4 * len(late), after=win_g)
    win_g = lax.dynamic_update_slice(win_g, window[None], (chip, 0, 0))
    wup_g = lax.dynamic_update_slice(wup_g, shards[1][None], (chip, 0, 0))
    wt = _assemble_w_in_t(win_g)
    wup = jnp.concatenate([jnp.transpose(wup_g, (1, 0, 2)).reshape(B_GATE_RANK, B_KEY_WIDTH),
                           jnp.zeros((LANE - B_GATE_RANK, B_KEY_WIDTH), _MX)], axis=0)

    x2 = x.reshape(t, D_MODEL)
    tgt = loss_target.reshape(t, D_MODEL)
    sinks = attn_sinks.reshape(A_HEADS)
    gf = g_final.reshape(1, D_MODEL)

    h, qkv, za, qkb, vb, zb, alr, ga, gb = _in_proj(x2, g_in + l_token[0, 0], wt, cos, sa, sb)
    oa = _attn_fwd(qkv, za, sinks, nseq)
    ob, oraw, sst = _gla_fwd(qkb, vb, zb, alr, wup, b_alpha, g_gla_norm, nseq)

    wa, wb, wo = _wait_copies("late_gather_wait", l_send, l_recv, l_flying, late_copies, ob)[len(late):]
    dh2, doa, dob, dga, dgb, dwa, dwb, dwo, dgf, lossv = _merge_loss(oa, ob, ga, gb, x2, tgt, wa, wb, wo, gf)

    dqkv, dza, dsink = _attn_bwd(qkv, za, doa, sinks, cos, sa, sb, nseq)
    dqkb, dvb, dzb, dalr, dwup, dba, dgn = _gla_bwd(qkb, vb, zb, alr, oraw, dob, sst, wup, b_alpha, g_gla_norm, nseq)
    dpieces = [dqkv, dza, dqkb, dvb, dzb, dalr, dga, dgb]
    gsplits = [SPLIT_W_IN_T, SPLIT_W_OUT, SPLIT_W_OUT, SPLIT_W_O]
    names = ("w_in", "w_out_a", "w_out_b", "w_o")
    dwin_mine, *from_sibling = _in_proj_bwd_w(h, dpieces, [dwa, dwb, dwo], gsplits[1:])
    pair_sums = [_pair_sum(g, r, sp, c_arr, "pair_sum_" + nm)
                 for g, r, sp, nm in zip([dwin_mine, dwa, dwb, dwo], from_sibling, [None] + gsplits[1:], names)]
    exchange = lambda refs, send, recv: _chip_exchange_copies(refs, send, recv, gsplits)
    lands = [lax.empty((4,) + sp.part_shape(p.shape), p.dtype) for p, sp in zip(pair_sums, gsplits)]
    send, recv, flying, token = _start_copies("grad_chip_exchange_start", pair_sums + lands, exchange, 3 * len(lands))
    grad_x2, dgin = _in_proj_bwd_x(dpieces, wt, x2, dh2, g_in + token[0, 0])
    landed = _wait_copies("grad_chip_exchange_wait", send, recv, flying, exchange, grad_x2)
    place_arr = jnp.stack([cc, chip]).astype(jnp.int32)
    reduced = [_sum_chips(q, p, sp, place_arr, "chip_sum_" + nm)
               for q, p, sp, nm in zip(landed[len(lands):], landed[:len(lands)], gsplits, names)]
    row2 = jnp.concatenate([dgn, dba, jnp.pad(dsink[:, 0].reshape(1, A_HEADS), ((0, 0), (0, LANE - A_HEADS))),
                            jnp.pad(jnp.sum(lossv, axis=1, keepdims=True), ((0, 0), (0, LANE - 1)))], axis=1)
    small = jnp.concatenate([dgin, dgf, row2, dwup[:B_GATE_RANK].reshape(WUP_ROWS[1] - WUP_ROWS[0], D_MODEL),
                             jnp.zeros((1, D_MODEL), _F32)], axis=0)
    g_window, g_wa, g_wb, g_wo, small_parts = _pair_share(reduced, gsplits, small)
    g_win_t = lax.switch(chip, [lambda w, k=k: w[4 * k:4 * k + SHARD].reshape(SHARD, 1, D_MODEL) for k in range(4)],
                         g_window)
    dev_arr = jnp.reshape(2 * chip + cc, (1,)).astype(jnp.int32)
    tot = _sum_devices(small_parts, small, dev_arr)
    loss = tot[LOSS_AT]
    nup = B_KEY_WIDTH // 4
    g_wup = lax.dynamic_slice(tot[WUP_ROWS[0]:WUP_ROWS[1]].reshape(B_GATE_RANK, B_KEY_WIDTH), (0, chip * nup),
                              (B_GATE_RANK, nup))

    row = lambda a: a.reshape(1, -1)
    sm = _adamw_small(tot, g_wup, dict(
        g_in=(g_in, m_g_in, v_g_in), g_final=(row(g_final), row(m_g_final), row(v_g_final)),
        g_gla_norm=(g_gla_norm, m_g_gla_norm, v_g_gla_norm), b_alpha=(b_alpha, m_b_alpha, v_b_alpha),
        attn_sinks=(attn_sinks, m_attn_sinks, v_attn_sinks),
        w_alpha_up=(w_alpha_up[0], m_w_alpha_up[0], v_w_alpha_up[0])))
    sm["g_final"] = tuple(a.reshape(D_MODEL) for a in sm["g_final"])
    sm["w_alpha_up"] = tuple(a[None] for a in sm["w_alpha_up"])

    untr = lambda a: jnp.transpose(a, (1, 2, 0))
    big = dict(w_in=tuple(untr(a) for a in (g_win_t,) + tuple(_adamw(tr(w_in), g_win_t, tr(m_w_in), tr(v_w_in), "adamw_w_in"))))
    for nm, w, g, m, v in (("w_out_a", w_out_a, g_wa, m_w_out_a, v_w_out_a),
                           ("w_out_b", w_out_b, g_wb, m_w_out_b, v_w_out_b), ("w_o", w_o, g_wo, m_w_o, v_w_o)):
        big[nm] = (g[None],) + tuple(_adamw(w, g[None], m, v, "adamw_" + nm))

    order = ("g_in", "w_in", "w_alpha_up", "b_alpha", "attn_sinks", "g_gla_norm", "w_out_a", "w_out_b", "w_o", "g_final")
    outs = [big[nm][kind] if nm in big else sm[nm][kind] for kind in range(4) for nm in order]
    return (loss, grad_x2.reshape(x.shape), *outs)
```

```python
import math
from typing import NamedTuple

import numpy as np
import jax
import jax.numpy as jnp
from jax import lax
from jax.experimental import pallas as pl
from jax.experimental.pallas import tpu as pltpu

D_MODEL = 1024
A_HEADS, A_KV_HEADS, A_HEAD_DIM = 8, 2, 64
A_GROUP = A_HEADS // A_KV_HEADS
A_WIDTH, A_KV_WIDTH = 512, 128
BLOCK = 128
ROPE_THETA = 500000.0
ROPE_DIM = 16
B_HEADS, B_KEY_DIM, B_VAL_DIM = 4, 64, 128
B_KEY_WIDTH, B_WIDTH = 256, 512
B_GATE_RANK = 16
B_GATE_TEMP = 16.0
B_CHUNK = 64
NORM_EPS = 1e-6
NEG_BIG = -1e30
D_IN = 4880

ADAM_LR, ADAM_B1, ADAM_B2, ADAM_EPS, ADAM_WD, ADAM_STEP = 0.001, 0.9, 0.999, 1e-08, 0.01, 10

LANE = 128
ALR_AT = 2816
PIECES = (("qkv", 0, 768), ("za", 768, 1280), ("qkb", 1280, 1792), ("vb", 1792, 2304),
          ("zb", 2304, 2816), ("alr", ALR_AT, ALR_AT + LANE), ("ga", 2832, 3856), ("gb", 3856, 4880))
SHARD = D_IN // 4
WINDOW_STEP = 1216
WINDOW_ROWS = 1232

GLA_BLOCK = 256
GLA_FWD_BLOCK = 1024
MERGE_SLAB = 16
VMEM_LIMIT = 56 * 1024 * 1024

_F32 = jnp.float32
_MX = jnp.bfloat16
_ST = jnp.bfloat16

_MESH = pl.DeviceIdType.MESH
_ANY = pl.BlockSpec(memory_space=pl.ANY)


def _cparams(sem=None, vmem=None):
    return pltpu.CompilerParams(dimension_semantics=sem, vmem_limit_bytes=vmem)


def _dot(a, b):
    return jnp.dot(a.astype(_MX), b.astype(_MX), preferred_element_type=_F32)


def _dot_nt(a, b):
    return lax.dot_general(a.astype(_MX), b.astype(_MX), (((1,), (1,)), ((), ())),
                           preferred_element_type=_F32)


def _dot_tn(a, b):
    return lax.dot_general(a.astype(_MX), b.astype(_MX), (((0,), (0,)), ((), ())),
                           preferred_element_type=_F32)


def _dot_ones(ones_mat, v):
    o = ones_mat.astype(jnp.bfloat16)
    v0 = v.astype(jnp.bfloat16)
    v1 = (v - v0.astype(_F32)).astype(jnp.bfloat16)
    d = lambda t: jnp.dot(o, t, preferred_element_type=_F32)
    return d(v0) + d(v1)


def _sigmoid(x):
    return 0.5 * jnp.tanh(0.5 * x) + 0.5


def _log_sigmoid(x):
    return jnp.minimum(x, 0.0) - jnp.log(1.0 + jnp.exp(-jnp.abs(x)))


def _lane_tile(t, width):
    reps = width // t.shape[1]
    return t if reps == 1 else jnp.tile(t, (1, reps))


def _rope(t, cos, sa, sb, sign):
    w = t.shape[1]
    rot = pltpu.roll(t, w - 8, 1) * _lane_tile(sa, w) + pltpu.roll(t, 8, 1) * _lane_tile(sb, w)
    return t * _lane_tile(cos, w) + sign * rot


def _rms_bwd(dy_g, n, r):
    return r * (dy_g - n * jnp.mean(dy_g * n, axis=-1, keepdims=True))


ROPE_ROWS = 256


def _rope_consts():
    lane = np.arange(LANE) % A_HEAD_DIM
    half = ROPE_DIM // 2
    inv = np.exp((np.float32(-math.log(ROPE_THETA)) * np.arange(half, dtype=np.float32)) * np.float32(2.0 / ROPE_DIM))
    consts = np.zeros((8, LANE), np.float32)
    consts[0] = np.where(lane < ROPE_DIM, inv[lane % half], 0.0)
    consts[1] = np.where(lane < half, -1.0, 0.0)
    consts[2] = np.where((lane >= half) & (lane < ROPE_DIM), 1.0, 0.0)
    return jnp.asarray(consts)


def _rope_tables_into(pos_ref, c_ref, cos_ref, sa_ref, sb_ref):
    def rows_of(b, carry):
        rows = pl.ds(pl.multiple_of(b * ROPE_ROWS, ROPE_ROWS), ROPE_ROWS)
        ang = pos_ref[rows, :] * c_ref[0:1, :]
        s = jnp.sin(ang)
        cos_ref[rows, :] = jnp.cos(ang)
        sa_ref[rows, :] = s * c_ref[1:2, :]
        sb_ref[rows, :] = s * c_ref[2:3, :]
        return carry

    lax.fori_loop(0, pos_ref.shape[0] // ROPE_ROWS, rows_of, 0)


def _in_proj(x2, g_in, wt, cos, sa, sb):
    t = x2.shape[0]
    tm = min(t, 512)

    def body(x_ref, g_ref, w_ref, cos_ref, sa_ref, sb_ref, h_ref, qkv_ref, za_ref, qkb_ref,
             vb_ref, zb_ref, alr_ref, ga_ref, gb_ref):
        xv = x_ref[...]
        r = lax.rsqrt(jnp.mean(xv * xv, axis=-1, keepdims=True) + NORM_EPS)
        h = (xv * r * g_ref[...]).astype(_MX)
        h_ref[...] = h.astype(_ST)
        outs = dict(za=za_ref, qkb=qkb_ref, vb=vb_ref, zb=zb_ref, alr=alr_ref, ga=ga_ref, gb=gb_ref)
        for name, a, b in PIECES:
            p = _dot_nt(h, w_ref[a:b, :])
            if name == "qkv":
                c, s1, s2 = cos_ref[...], sa_ref[...], sb_ref[...]
                qkv_ref[:, 0:512] = _rope(p[:, 0:512], c, s1, s2, 1.0).astype(_ST)
                qkv_ref[:, 512:640] = _rope(p[:, 512:640], c, s1, s2, 1.0).astype(_ST)
                qkv_ref[:, 640:768] = p[:, 640:768].astype(_ST)
            else:
                outs[name][...] = p.astype(outs[name].dtype)

    rows = lambda w: pl.BlockSpec((tm, w), lambda i: (i, 0))
    shp = lambda name, w: jax.ShapeDtypeStruct((t, w), _F32 if name == "qkb" else _ST)
    widths = [D_MODEL] + [b - a for _, a, b in PIECES]
    return pl.pallas_call(
        body, name="in_proj", grid=(t // tm,),
        in_specs=[rows(D_MODEL), pl.BlockSpec((1, D_MODEL), lambda i: (0, 0)),
                  pl.BlockSpec((D_IN, D_MODEL), lambda i: (0, 0), pipeline_mode=pl.Buffered(1)),
                  rows(LANE), rows(LANE), rows(LANE)],
        out_specs=[rows(w) for w in widths],
        out_shape=[shp(n, w) for n, w in zip(["h"] + [p[0] for p in PIECES], widths)],
        compiler_params=_cparams(("parallel",), VMEM_LIMIT),
    )(x2, g_in, wt, cos, sa, sb)


def _attn_operands(k_prev, k_cur, v_prev, v_cur, want_bwd):
    kf = jnp.concatenate([k_prev, k_cur], axis=0).astype(_F32) * (A_HEAD_DIM ** -0.5)
    vf = jnp.concatenate([v_prev, v_cur], axis=0).astype(_F32)
    lo = lax.broadcasted_iota(jnp.int32, (1, LANE), 1) < 64

    def on_lanes(a):
        sw = pltpu.roll(a, 64, 1)
        z = jnp.zeros_like(a)
        return [[jnp.where(lo, a, z).astype(_MX), jnp.where(lo, z, sw).astype(_MX)],
                [jnp.where(lo, sw, z).astype(_MX), jnp.where(lo, z, a).astype(_MX)]]

    def on_rows(a):
        at = a.T.astype(_MX)
        z = jnp.zeros((64, at.shape[1]), _MX)
        top, bot = at[0:64], at[64:128]
        return [[jnp.concatenate([top, z], axis=0), jnp.concatenate([z, top], axis=0)],
                [jnp.concatenate([bot, z], axis=0), jnp.concatenate([z, bot], axis=0)]]

    ops = dict(k_lanes=on_lanes(kf), v_rows=on_rows(vf), lo=lo)
    if want_bwd:
        ops.update(v_lanes=on_lanes(vf), k_rows=on_rows(kf))
    return ops


def _attn_valid(n):
    kj = lax.broadcasted_iota(jnp.int32, (2 * BLOCK, 2 * BLOCK), 0) - BLOCK
    qi = lax.broadcasted_iota(jnp.int32, (2 * BLOCK, 2 * BLOCK), 1) & (BLOCK - 1)
    return (kj <= qi) & (qi - kj < BLOCK) & ((n > 0) | (kj >= 0))


def _attn_sinks(sink_ref, h_a, h_b):
    first = lax.broadcasted_iota(jnp.int32, (1, 2 * BLOCK), 1) < BLOCK
    return jnp.where(first, sink_ref[h_a], sink_ref[h_b])


def _attn_softmax_t(k_lanes, q_pair, valid, sink):
    s = jnp.where(valid, _dot_nt(k_lanes, q_pair), NEG_BIG)
    m = jnp.maximum(jnp.max(s, axis=0, keepdims=True), sink)
    e = jnp.exp(s - m)
    e_sink = jnp.exp(sink - m)
    inv = 1.0 / (jnp.sum(e, axis=0, keepdims=True) + e_sink)
    return e, e_sink, inv


ATTN_TILE = 8


def _attn_kv(qkv_ref, kvp_ref, j):
    rows = slice(j * BLOCK, (j + 1) * BLOCK)
    if j == 0:
        k_prev, v_prev = kvp_ref[:, 0:128], kvp_ref[:, 128:256]
    else:
        before = slice((j - 1) * BLOCK, j * BLOCK)
        k_prev, v_prev = qkv_ref[before, 512:640], qkv_ref[before, 640:768]
    return k_prev, qkv_ref[rows, 512:640], v_prev, qkv_ref[rows, 640:768]


def _attn_fwd(qkv, za, sinks, nseq):
    t = qkv.shape[0]
    nblk = min(ATTN_TILE, t // nseq // BLOCK)
    tile = nblk * BLOCK
    nt = t // nseq // tile

    def body(sink_ref, qkv_ref, kvp_ref, za_ref, oa_ref):
        for j in range(nblk):
            rows = slice(j * BLOCK, (j + 1) * BLOCK)
            ops = _attn_operands(*_attn_kv(qkv_ref, kvp_ref, j), False)
            valid = _attn_valid(nblk * pl.program_id(1) + j)[:, 0:BLOCK]
            for pr in range(A_HEADS // 2):
                lanes = slice(pr * LANE, (pr + 1) * LANE)
                g = pr // (A_GROUP // 2)
                q_pair = qkv_ref[rows, lanes]
                ot = None
                for half in range(2):
                    e, _, inv = _attn_softmax_t(ops["k_lanes"][g][half], q_pair, valid, sink_ref[2 * pr + half])
                    part = _dot(ops["v_rows"][g][half], e) * inv
                    ot = part if ot is None else ot + part
                z = za_ref[rows, lanes].astype(_F32)
                oa_ref[rows, lanes] = (ot.T * (z * _sigmoid(z))).astype(_ST)

    cur = lambda w: pl.BlockSpec((tile, w), lambda s, n: (s * nt + n, 0))
    return pl.pallas_call(
        body, name="attn_fwd", grid=(nseq, nt),
        in_specs=[pl.BlockSpec(memory_space=pltpu.SMEM), cur(768),
                  pl.BlockSpec((BLOCK, 256), lambda s, n: (nblk * (s * nt + n) - jnp.minimum(n, 1), 2)),
                  cur(512)],
        out_specs=cur(512), out_shape=jax.ShapeDtypeStruct((t, A_WIDTH), _ST),
        compiler_params=_cparams(("parallel", "arbitrary")),
    )(sinks, qkv, qkv, za)


def _attn_bwd(qkv, za, doa, sinks, cos, sa, sb, nseq):
    t = qkv.shape[0]
    nblk = min(ATTN_TILE, t // nseq // BLOCK)
    tile = nblk * BLOCK
    nt = t // nseq // tile

    def body(sink_ref, qkv_ref, kvp_ref, za_ref, doa_ref, cos_ref, sa_ref, sb_ref,
             dqkv_ref, dza_ref, dsink_ref, ck_ref, cv_ref):
        s_id, i = pl.program_id(0), pl.program_id(1)

        @pl.when((s_id == 0) & (i == 0))
        def _():
            dsink_ref[...] = jnp.zeros_like(dsink_ref)

        @pl.when(i == 0)
        def _():
            ck_ref[...] = jnp.zeros_like(ck_ref)
            cv_ref[...] = jnp.zeros_like(cv_ref)

        carry_k, carry_v = ck_ref[...], cv_ref[...]
        for j in reversed(range(nblk)):
            rows = slice(j * BLOCK, (j + 1) * BLOCK)
            ops = _attn_operands(*_attn_kv(qkv_ref, kvp_ref, j), True)
            lo = ops["lo"]
            valid = _attn_valid(nblk * (nt - 1 - i) + j)
            dk_acc, dv_acc, dq_pairs = [], [], []
            for g in range(A_KV_HEADS):
                pairs = [slice((2 * g + p) * LANE, (2 * g + p + 1) * LANE) for p in range(2)]
                q_both = jnp.concatenate([qkv_ref[rows, p] for p in pairs], axis=0)
                q_f = q_both.astype(_F32)
                z = [za_ref[rows, p].astype(_F32) for p in pairs]
                sz = [_sigmoid(v) for v in z]
                d_oa = [doa_ref[rows, p].astype(_F32) for p in pairs]
                d_att = jnp.concatenate([d_oa[p] * (z[p] * sz[p]) for p in range(2)], axis=0)
                zero = jnp.zeros_like(d_att)
                ot, dqt, ds_all, pn_all, qz_all, daz_all = None, None, [], [], [], []
                for half in range(2):
                    heads = (4 * g + half, 4 * g + 2 + half)
                    e, e_sink, inv = _attn_softmax_t(ops["k_lanes"][g][half], q_both, valid,
                                                     _attn_sinks(sink_ref, *heads))
                    pn = e * inv
                    dpt = _dot_nt(ops["v_lanes"][g][half], d_att)
                    delta = jnp.sum(pn * dpt, axis=0, keepdims=True)
                    ds = (pn * (dpt - delta)).astype(_MX)
                    pn = pn.astype(_MX)
                    d_sink = e_sink * inv * delta
                    for p, h in enumerate(heads):
                        dsink_ref[h:h + 1, :] = dsink_ref[h:h + 1, :] - jnp.sum(d_sink[:, p * BLOCK:(p + 1) * BLOCK])
                    o_part = _dot(ops["v_rows"][g][half], pn)
                    dq_part = _dot(ops["k_rows"][g][half], ds)
                    ot = o_part if ot is None else ot + o_part
                    dqt = dq_part if dqt is None else dqt + dq_part
                    mine = lo if half == 0 else jnp.logical_not(lo)
                    ds_all.append(ds)
                    pn_all.append(pn)
                    qz_all.append(jnp.where(mine, q_f, zero).astype(_MX))
                    daz_all.append(jnp.where(mine, d_att, zero).astype(_MX))
                dk_acc.append(_dot(jnp.concatenate(ds_all, axis=1), jnp.concatenate(qz_all, axis=0)))
                dv_acc.append(_dot(jnp.concatenate(pn_all, axis=1), jnp.concatenate(daz_all, axis=0)))
                for p, lanes in enumerate(pairs):
                    cols = slice(p * BLOCK, (p + 1) * BLOCK)
                    dza_ref[rows, lanes] = (d_oa[p] * ot[:, cols].T * (sz[p] * (1.0 + z[p] * (1.0 - sz[p])))).astype(_ST)
                    dq_pairs.append(dqt[:, cols].T)

            def fold(acc, scale):
                both = [a + pltpu.roll(a, 64, 1) for a in acc]
                return jnp.where(lo, both[0], both[1]) * scale

            dk_full = fold(dk_acc, A_HEAD_DIM ** -0.5)
            dv_full = fold(dv_acc, 1.0)
            dk_cur, dv_cur = dk_full[BLOCK:] + carry_k, dv_full[BLOCK:] + carry_v
            carry_k, carry_v = dk_full[:BLOCK], dv_full[:BLOCK]
            c, s1, s2 = cos_ref[rows, :], sa_ref[rows, :], sb_ref[rows, :]
            dqkv_ref[rows, 0:512] = _rope(jnp.concatenate(dq_pairs, axis=1), c, s1, s2, -1.0).astype(_ST)
            dqkv_ref[rows, 512:640] = _rope(dk_cur, c, s1, s2, -1.0).astype(_ST)
            dqkv_ref[rows, 640:768] = dv_cur.astype(_ST)
        ck_ref[...] = carry_k
        cv_ref[...] = carry_v

    cur = lambda w: pl.BlockSpec((tile, w), lambda s, i: (s * nt + nt - 1 - i, 0))
    return pl.pallas_call(
        body, name="attn_bwd", grid=(nseq, nt),
        in_specs=[pl.BlockSpec(memory_space=pltpu.SMEM), cur(768),
                  pl.BlockSpec((BLOCK, 256),
                               lambda s, i: (nblk * (s * nt + nt - 1 - i) - jnp.minimum(nt - 1 - i, 1), 2)),
                  cur(512), cur(512), cur(LANE), cur(LANE), cur(LANE)],
        out_specs=[cur(768), cur(512), pl.BlockSpec((8, LANE), lambda s, i: (0, 0))],
        out_shape=[jax.ShapeDtypeStruct((t, 768), _ST), jax.ShapeDtypeStruct((t, 512), _ST),
                   jax.ShapeDtypeStruct((8, LANE), _F32)],
        scratch_shapes=[pltpu.VMEM((BLOCK, A_KV_WIDTH), _F32), pltpu.VMEM((BLOCK, A_KV_WIDTH), _F32)],
        compiler_params=_cparams(("arbitrary", "arbitrary")),
    )(sinks, qkv, qkv, za, doa, cos, sa, sb)


def _gla_chunk_terms(la, qkb_ref, r0):
    g = la[r0:r0 + B_CHUNK, :]
    ri = lax.broadcasted_iota(jnp.int32, (B_CHUNK, B_CHUNK), 0)
    ci = lax.broadcasted_iota(jnp.int32, (B_CHUNK, B_CHUNK), 1)
    cum = _dot_ones((ri >= ci).astype(_F32), g)
    last = cum[B_CHUNK - 1:B_CHUNK, :]
    mid = cum[B_CHUNK // 2 - 1:B_CHUNK // 2, :]
    q = qkb_ref[r0:r0 + B_CHUNK, 0:B_KEY_WIDTH].astype(_F32) * (B_KEY_DIM ** -0.5)
    k = qkb_ref[r0:r0 + B_CHUNK, B_KEY_WIDTH:2 * B_KEY_WIDTH].astype(_F32)
    e_q, e_k, e_l, e_c = jnp.exp(cum - mid), jnp.exp(mid - cum), jnp.exp(last - cum), jnp.exp(cum)
    dec_col = jnp.exp(jnp.sum(g.T, axis=1, keepdims=True))
    return dict(qm=q * e_q, km=k * e_k, kl=k * e_l, qc=q * e_c, e_q=e_q, e_k=e_k, e_l=e_l, e_c=e_c,
                dec_col=dec_col, dec_row=jnp.exp(last), causal=ri >= ci, ri=ri)


def _gate_logits(alr_ref, wup_ref, b_ref):
    return _dot(alr_ref[...], wup_ref[...]) + b_ref[...]


def _gla_fwd(qkb, vb, zb, alr, wup, b_alpha, gn, nseq):
    t = qkb.shape[0]
    tb = min(GLA_FWD_BLOCK, t // nseq)
    nblk = t // nseq // tb
    cpb = tb // B_CHUNK

    def body(qkb_ref, vb_ref, zb_ref, alr_ref, wup_ref, b_ref, gn_ref, ob_ref, oraw_ref, sst_ref, s_ref):
        @pl.when(pl.program_id(1) == 0)
        def _():
            s_ref[...] = jnp.zeros_like(s_ref)

        la = _log_sigmoid(_gate_logits(alr_ref, wup_ref, b_ref)) * (1.0 / B_GATE_TEMP)
        terms = [_gla_chunk_terms(la, qkb_ref, c * B_CHUNK) for c in range(cpb)]
        o_intra, inc = {}, {}
        for c, tm in enumerate(terms):
            for h in range(B_HEADS):
                kl_, vl_ = slice(h * 64, (h + 1) * 64), slice(h * 128, (h + 1) * 128)
                v = vb_ref[c * B_CHUNK:(c + 1) * B_CHUNK, vl_]
                a = jnp.where(tm["causal"], _dot_nt(tm["qm"][:, kl_], tm["km"][:, kl_]), 0.0)
                o_intra[c, h] = _dot(a, v)
                inc[c, h] = _dot_tn(tm["kl"][:, kl_], v)
        o_heads = {}
        for h in range(B_HEADS):
            kl_ = slice(h * 64, (h + 1) * 64)
            st = s_ref[kl_, :]
            for c, tm in enumerate(terms):
                sst_ref[c, kl_, :] = st
                o_heads[c, h] = o_intra[c, h] + _dot(tm["qc"][:, kl_], st)
                st = tm["dec_col"][kl_, :] * st + inc[c, h]
            s_ref[kl_, :] = st
        o = jnp.concatenate([jnp.concatenate([o_heads[c, h] for h in range(B_HEADS)], axis=1)
                             for c in range(cpb)], axis=0)
        oraw_ref[...] = o
        z = zb_ref[...].astype(_F32)
        gate = z * _sigmoid(z)
        for h in range(B_HEADS):
            vl_ = slice(h * 128, (h + 1) * 128)
            oh = o[:, vl_]
            r = lax.rsqrt(jnp.mean(oh * oh, axis=-1, keepdims=True) + NORM_EPS)
            ob_ref[:, vl_] = ((oh * r) * gn_ref[:, vl_] * gate[:, vl_]).astype(_ST)

    rows = lambda w: pl.BlockSpec((tb, w), lambda s, i: (s * nblk + i, 0))
    full = lambda a, b: pl.BlockSpec((a, b), lambda s, i: (0, 0))
    return pl.pallas_call(
        body, name="gla_fwd", grid=(nseq, nblk),
        in_specs=[rows(512), rows(512), rows(512), rows(LANE), full(LANE, B_KEY_WIDTH),
                  full(1, B_KEY_WIDTH), full(1, B_WIDTH)],
        out_specs=[rows(512), rows(512),
                   pl.BlockSpec((cpb, B_KEY_WIDTH, B_VAL_DIM), lambda s, i: (s * nblk + i, 0, 0))],
        out_shape=[jax.ShapeDtypeStruct((t, B_WIDTH), _ST), jax.ShapeDtypeStruct((t, B_WIDTH), _F32),
                   jax.ShapeDtypeStruct((t // B_CHUNK, B_KEY_WIDTH, B_VAL_DIM), _F32)],
        scratch_shapes=[pltpu.VMEM((B_KEY_WIDTH, B_VAL_DIM), _F32)],
        compiler_params=_cparams(("parallel", "arbitrary")),
    )(qkb, vb, zb, alr, wup, b_alpha, gn)


def _gla_bwd(qkb, vb, zb, alr, oraw, dob, sst, wup, b_alpha, gn, nseq):
    t = qkb.shape[0]
    tb = min(GLA_BLOCK, t // nseq)
    nblk = t // nseq // tb
    cpb = tb // B_CHUNK

    def body(qkb_ref, vb_ref, zb_ref, alr_ref, oraw_ref, dob_ref, sst_ref, wup_ref, b_ref, gn_ref,
             dqkb_ref, dvb_ref, dzb_ref, dalr_ref, dwup_ref, db_ref, dgn_ref, ds_ref):
        s_id, i = pl.program_id(0), pl.program_id(1)

        @pl.when((s_id == 0) & (i == 0))
        def _():
            dwup_ref[...] = jnp.zeros_like(dwup_ref)
            db_ref[...] = jnp.zeros_like(db_ref)
            dgn_ref[...] = jnp.zeros_like(dgn_ref)

        @pl.when(i == 0)
        def _():
            ds_ref[...] = jnp.zeros_like(ds_ref)

        a_pre = _gate_logits(alr_ref, wup_ref, b_ref)
        la = _log_sigmoid(a_pre) * (1.0 / B_GATE_TEMP)

        z = zb_ref[...].astype(_F32)
        sz = _sigmoid(z)
        d_ob = dob_ref[...].astype(_F32)
        tg = d_ob * (z * sz)
        dsilu = sz * (1.0 + z * (1.0 - sz))
        do_cols, dgn_cols = [], []
        for h in range(B_HEADS):
            vl_ = slice(h * 128, (h + 1) * 128)
            oh = oraw_ref[:, vl_].astype(_F32)
            r = lax.rsqrt(jnp.mean(oh * oh, axis=-1, keepdims=True) + NORM_EPS)
            on = oh * r
            gnh = gn_ref[:, vl_]
            dzb_ref[:, vl_] = (d_ob[:, vl_] * (on * gnh) * dsilu[:, vl_]).astype(_ST)
            dgn_cols.append(jnp.sum(tg[:, vl_] * on, axis=0, keepdims=True))
            do_cols.append(_rms_bwd(tg[:, vl_] * gnh, on, r))
        dgn_ref[...] = dgn_ref[...] + jnp.concatenate(dgn_cols, axis=1)
        d_o = jnp.concatenate(do_cols, axis=1)

        ri = lax.broadcasted_iota(jnp.int32, (tb, tb), 0)
        ci = lax.broadcasted_iota(jnp.int32, (tb, tb), 1)
        same = (ri // B_CHUNK) == (ci // B_CHUNK)
        low = same & (ri >= ci)
        cum = _dot_ones(low.astype(_F32), la)
        at_row = lambda r: jnp.concatenate([jnp.broadcast_to(cum[c * B_CHUNK + r:c * B_CHUNK + r + 1], (B_CHUNK, B_KEY_WIDTH))
                                            for c in range(cpb)], axis=0)
        last, mid = at_row(B_CHUNK - 1), at_row(B_CHUNK // 2 - 1)
        e_q, e_k, e_l, e_c = jnp.exp(cum - mid), jnp.exp(mid - cum), jnp.exp(last - cum), jnp.exp(cum)
        q = qkb_ref[:, 0:B_KEY_WIDTH] * (B_KEY_DIM ** -0.5)
        k = qkb_ref[:, B_KEY_WIDTH:2 * B_KEY_WIDTH]
        qm, km, kl, qc = q * e_q, k * e_k, k * e_l, q * e_c
        lane_head = lax.broadcasted_iota(jnp.int32, (1, B_KEY_WIDTH), 1) // B_KEY_DIM
        d_o_mx = d_o.astype(_MX)

        def on_diagonal(st):
            z = jnp.zeros((B_KEY_DIM, B_VAL_DIM), st.dtype)
            return jnp.concatenate([jnp.concatenate(
                [st[h * B_KEY_DIM:(h + 1) * B_KEY_DIM] if g == h else z for g in range(B_HEADS)], axis=1)
                for h in range(B_HEADS)], axis=0)

        def diagonal_of(full):
            return jnp.concatenate([full[h * B_KEY_DIM:(h + 1) * B_KEY_DIM, h * B_VAL_DIM:(h + 1) * B_VAL_DIM]
                                    for h in range(B_HEADS)], axis=0)

        dqm, dkm, dv_cols = None, None, []
        for h in range(B_HEADS):
            vl_ = slice(h * B_VAL_DIM, (h + 1) * B_VAL_DIM)
            mine = lane_head == h
            qz, kz = jnp.where(mine, qm, 0.0).astype(_MX), jnp.where(mine, km, 0.0).astype(_MX)
            a = jnp.where(low, _dot_nt(qz, kz), 0.0).astype(_MX)
            da = jnp.where(low, _dot_nt(d_o_mx[:, vl_], vb_ref[:, vl_]), 0.0).astype(_MX)
            dqm_h, dkm_h = _dot(da, kz), _dot_tn(da, qz)
            dqm = dqm_h if dqm is None else dqm + dqm_h
            dkm = dkm_h if dkm is None else dkm + dkm_h
            dv_cols.append(_dot_tn(a, d_o_mx[:, vl_]))
        dv = jnp.concatenate(dv_cols, axis=1)

        chunk = [slice(c * B_CHUNK, (c + 1) * B_CHUNK) for c in range(cpb)]
        dqc_rows, g_loc = [], []
        for c in range(cpb):
            dqc_rows.append(_dot_nt(d_o_mx[chunk[c]], on_diagonal(sst_ref[c].astype(_MX))))
            g_loc.append(diagonal_of(_dot_tn(qc[chunk[c]], d_o_mx[chunk[c]])))
        cur = ds_ref[...]
        d_state = [None] * cpb
        for c in reversed(range(cpb)):
            d_state[c] = cur
            cur = g_loc[c] + jnp.exp(jnp.sum(la[chunk[c]].T, axis=1, keepdims=True)) * cur
        ds_ref[...] = cur
        dkl_rows, dv_rows, dlast_rows = [], [], []
        ones8 = jnp.ones((8, B_VAL_DIM), _F32)
        for c in range(cpb):
            dsd = on_diagonal(d_state[c].astype(_MX))
            dkl_c = _dot_nt(vb_ref[chunk[c], :], dsd)
            dkl_rows.append(dkl_c)
            dv_rows.append(_dot(kl[chunk[c]], dsd))
            prod = d_state[c] * sst_ref[c]
            p0 = prod.astype(jnp.bfloat16)
            p1 = (prod - p0.astype(_F32)).astype(jnp.bfloat16)
            ddec = (_dot_nt(ones8, p0) + _dot_nt(ones8, p1))[0:1]
            r_last = c * B_CHUNK + B_CHUNK - 1
            dlast = jnp.sum(dkl_c * kl[chunk[c]], axis=0, keepdims=True) + ddec * jnp.exp(last[r_last:r_last + 1])
            dlast_rows.append(jnp.broadcast_to(dlast, (B_CHUNK, B_KEY_WIDTH)))
        dqc, dkl = jnp.concatenate(dqc_rows, axis=0), jnp.concatenate(dkl_rows, axis=0)
        dqkb_ref[:, 0:B_KEY_WIDTH] = ((dqm * e_q + dqc * e_c) * (B_KEY_DIM ** -0.5)).astype(_ST)
        dqkb_ref[:, B_KEY_WIDTH:2 * B_KEY_WIDTH] = (dkm * e_k + dkl * e_l).astype(_ST)
        dvb_ref[...] = (dv + jnp.concatenate(dv_rows, axis=0)).astype(_ST)
        dcum = dqm * qm - dkm * km + dqc * qc - dkl * kl
        row = lax.broadcasted_iota(jnp.int32, (tb, B_KEY_WIDTH), 0)
        dcum = jnp.where(row % B_CHUNK == B_CHUNK - 1, dcum + jnp.concatenate(dlast_rows, axis=0), dcum)
        dla = _dot_ones((same & (ri <= ci)).astype(_F32), dcum)

        da_pre = dla * (1.0 / B_GATE_TEMP) * (1.0 - _sigmoid(a_pre))
        dalr_ref[...] = _dot_nt(da_pre, wup_ref[...]).astype(_ST)
        dwup_ref[...] = dwup_ref[...] + _dot_tn(alr_ref[...], da_pre)
        db_ref[...] = db_ref[...] + jnp.sum(da_pre, axis=0, keepdims=True)

    blk = lambda s, i: s * nblk + nblk - 1 - i
    rows = lambda w: pl.BlockSpec((tb, w), lambda s, i: (blk(s, i), 0))
    full = lambda a, b: pl.BlockSpec((a, b), lambda s, i: (0, 0))
    act = lambda w: jax.ShapeDtypeStruct((t, w), _ST)
    return pl.pallas_call(
        body, name="gla_bwd", grid=(nseq, nblk),
        in_specs=[rows(512), rows(512), rows(512), rows(LANE), rows(512), rows(512),
                  pl.BlockSpec((cpb, B_KEY_WIDTH, B_VAL_DIM), lambda s, i: (blk(s, i), 0, 0)),
                  full(LANE, B_KEY_WIDTH), full(1, B_KEY_WIDTH), full(1, B_WIDTH)],
        out_specs=[rows(512), rows(512), rows(512), rows(LANE), full(LANE, B_KEY_WIDTH),
                   full(1, B_KEY_WIDTH), full(1, B_WIDTH)],
        out_shape=[act(512), act(512), act(512), act(LANE),
                   jax.ShapeDtypeStruct((LANE, B_KEY_WIDTH), _F32),
                   jax.ShapeDtypeStruct((1, B_KEY_WIDTH), _F32), jax.ShapeDtypeStruct((1, B_WIDTH), _F32)],
        scratch_shapes=[pltpu.VMEM((B_KEY_WIDTH, B_VAL_DIM), _F32)],
        compiler_params=_cparams(("arbitrary", "arbitrary")),
    )(qkb, vb, zb, alr, oraw, dob, sst, wup, b_alpha, gn)


def _merge_loss(oa, ob, ga, gb, x2, tgt, wa, wb, wo, g_final):
    t = x2.shape[0]
    tm = min(t, 512)
    nt = t // tm

    def body(oa_ref, ob_ref, ga_ref, gb_ref, x_ref, t_ref, wa_ref, wb_ref, wo_ref, gf_ref,
             dh_ref, doa_ref, dob_ref, dga_ref, dgb_ref, dwa_ref, dwb_ref, dwo_ref, dgf_ref, loss_ref,
             ya_s, yb_s, out_s, dmer_s, mrg_s, dya_s, dyb_s):
        first = pl.program_id(0) == 0
        so_far = lambda ref: jnp.where(first, 0.0, ref[...])

        slabs = [slice(s, s + MERGE_SLAB) for s in range(0, tm, MERGE_SLAB)]
        fold = lambda a: a[0:8] + a[8:16]
        ya_s[...] = _dot(oa_ref[...], wa_ref[...])
        yb_s[...] = _dot(ob_ref[...], wb_ref[...])
        for rows_ in slabs:
            sga, sgb = _sigmoid(ga_ref[rows_, :].astype(_F32)), _sigmoid(gb_ref[rows_, :].astype(_F32))
            mrg_s[rows_, :] = (sga * ya_s[rows_, :] + sgb * yb_s[rows_, :]).astype(_MX)
        out_s[...] = x_ref[...] + _dot(mrg_s[...], wo_ref[...])
        gf = gf_ref[...]
        loss8 = jnp.zeros((8, D_MODEL), _F32)
        dgf8 = jnp.zeros((8, D_MODEL), _F32)
        for rows_ in slabs:
            out = out_s[rows_, :]
            r = lax.rsqrt(jnp.mean(out * out, axis=-1, keepdims=True) + NORM_EPS)
            nrm = out * r
            err = nrm * gf - t_ref[rows_, :]
            loss8 = loss8 + fold(err * err)
            dy = err * (1.0 / D_MODEL)
            dgf8 = dgf8 + fold(dy * nrm)
            dh = _rms_bwd(dy * gf, nrm, r)
            dh_ref[rows_, :] = dh.astype(_ST)
        loss_ref[...] = so_far(loss_ref) + (0.5 / D_MODEL) * jnp.sum(loss8, axis=0, keepdims=True)
        dgf_ref[...] = so_far(dgf_ref) + jnp.sum(dgf8, axis=0, keepdims=True)
        dmer_s[...] = _dot_nt(dh_ref[...], wo_ref[...])
        dwo_ref[...] = so_far(dwo_ref) + _dot_tn(mrg_s[...], dh_ref[...])
        for rows_ in slabs:
            sga, sgb = _sigmoid(ga_ref[rows_, :].astype(_F32)), _sigmoid(gb_ref[rows_, :].astype(_F32))
            dmer = dmer_s[rows_, :]
            da, db = dmer * sga, dmer * sgb
            dya_s[rows_, :] = da.astype(_MX)
            dyb_s[rows_, :] = db.astype(_MX)
            dga_ref[rows_, :] = (da * ya_s[rows_, :] * (1.0 - sga)).astype(_ST)
            dgb_ref[rows_, :] = (db * yb_s[rows_, :] * (1.0 - sgb)).astype(_ST)
        doa_ref[...] = _dot_nt(dya_s[...], wa_ref[...]).astype(_ST)
        dob_ref[...] = _dot_nt(dyb_s[...], wb_ref[...]).astype(_ST)
        dwa_ref[...] = so_far(dwa_ref) + _dot_tn(oa_ref[...], dya_s[...])
        dwb_ref[...] = so_far(dwb_ref) + _dot_tn(ob_ref[...], dyb_s[...])

    rows = lambda w: pl.BlockSpec((tm, w), lambda i: (i, 0))
    full = lambda a, b: pl.BlockSpec((a, b), lambda i: (0, 0), pipeline_mode=pl.Buffered(1))
    return pl.pallas_call(
        body, name="merge_loss", grid=(nt,),
        in_specs=[rows(512), rows(512), rows(D_MODEL), rows(D_MODEL), rows(D_MODEL), rows(D_MODEL),
                  full(A_WIDTH, D_MODEL), full(B_WIDTH, D_MODEL), full(D_MODEL, D_MODEL), full(1, D_MODEL)],
        out_specs=[rows(D_MODEL), rows(512), rows(512), rows(D_MODEL), rows(D_MODEL),
                   full(A_WIDTH, D_MODEL), full(B_WIDTH, D_MODEL), full(D_MODEL, D_MODEL),
                   full(1, D_MODEL), full(1, D_MODEL)],
        out_shape=[jax.ShapeDtypeStruct((t, D_MODEL), _ST), jax.ShapeDtypeStruct((t, 512), _ST),
                   jax.ShapeDtypeStruct((t, 512), _ST), jax.ShapeDtypeStruct((t, D_MODEL), _ST),
                   jax.ShapeDtypeStruct((t, D_MODEL), _ST),
                   jax.ShapeDtypeStruct((A_WIDTH, D_MODEL), _F32), jax.ShapeDtypeStruct((B_WIDTH, D_MODEL), _F32),
                   jax.ShapeDtypeStruct((D_MODEL, D_MODEL), _F32), jax.ShapeDtypeStruct((1, D_MODEL), _F32),
                   jax.ShapeDtypeStruct((1, D_MODEL), _F32)],
        scratch_shapes=[pltpu.VMEM((tm, D_MODEL), _F32)] * 4 + [pltpu.VMEM((tm, D_MODEL), _MX)] * 3,
        compiler_params=_cparams(("arbitrary",), VMEM_LIMIT),
    )(oa, ob, ga, gb, x2, tgt, wa, wb, wo, g_final)


def _in_proj_bwd_x(dpieces, wt, x2, dh2, g_in):
    t = x2.shape[0]
    tm = min(t, 512)
    np_ = len(PIECES)

    def body(*refs):
        dp_refs = refs[:np_]
        w_ref, x_ref, dh2_ref, g_ref, gx_ref, dg_ref = refs[np_:]

        @pl.when(pl.program_id(0) == 0)
        def _():
            dg_ref[...] = jnp.zeros_like(dg_ref)

        dh = None
        for (name, a, b), dp in zip(PIECES, dp_refs):
            part = _dot(dp[...], w_ref[a:b, :])
            dh = part if dh is None else dh + part
        xv = x_ref[...]
        r = lax.rsqrt(jnp.mean(xv * xv, axis=-1, keepdims=True) + NORM_EPS)
        nrm = xv * r
        dg_ref[...] = dg_ref[...] + jnp.sum(dh * nrm, axis=0, keepdims=True)
        gx_ref[...] = dh2_ref[...].astype(_F32) + _rms_bwd(dh * g_ref[...], nrm, r)

    rows = lambda w: pl.BlockSpec((tm, w), lambda i: (i, 0))
    full = lambda a, b: pl.BlockSpec((a, b), lambda i: (0, 0), pipeline_mode=pl.Buffered(1))
    return pl.pallas_call(
        body, name="in_proj_bwd_x", grid=(t // tm,),
        in_specs=[rows(b - a) for _, a, b in PIECES] + [full(D_IN, D_MODEL), rows(D_MODEL), rows(D_MODEL),
                                                          full(1, D_MODEL)],
        out_specs=[rows(D_MODEL), full(1, D_MODEL)],
        out_shape=[jax.ShapeDtypeStruct((t, D_MODEL), _F32), jax.ShapeDtypeStruct((1, D_MODEL), _F32)],
        compiler_params=_cparams(("arbitrary",), VMEM_LIMIT),
    )(*dpieces, wt, x2, dh2, g_in)


def _in_proj_bwd_w(h, dpieces, others, osplits):
    t = h.shape[0]
    tm = min(t, 1024)
    nt = t // tm
    np_, no = len(PIECES), len(others)
    half = D_MODEL // 2

    def body(*refs):
        h_ref, dp_refs, o_refs = refs[0], refs[1:1 + np_], refs[1 + np_:1 + np_ + no]
        mine_ref, theirs_ref = refs[1 + np_ + no:3 + np_ + no]
        r_refs = refs[3 + np_ + no:3 + np_ + 2 * no]
        acc_ref, stage_ref, keep_sem, send, recv, o_send, o_recv = refs[3 + np_ + 2 * no:]
        i = pl.program_id(0)
        x, y, c = _place()
        sibling = (x, y, 1 - c)
        early = [pltpu.make_async_remote_copy(
            src_ref=osplits[k].half(o_refs[k], 1 - c), dst_ref=r_refs[k], send_sem=o_send.at[k], recv_sem=o_recv.at[k],
            device_id=sibling, device_id_type=_MESH) for k in range(no)]

        @pl.when(i == 0)
        def _():
            for cp in early:
                cp.start()

        hv = h_ref[...]
        cols = lambda core: pl.ds(pl.multiple_of(core * half, LANE), half)
        writes = []
        by_size = sorted(range(np_), key=lambda j: PIECES[j][1] - PIECES[j][2])
        for j, ((name, a, b), dp) in [(j, (PIECES[j], dp_refs[j])) for j in by_size]:
            part = _dot_tn(dp[...], hv)
            if name == "alr":
                b = a + B_GATE_RANK
                part = part[0:B_GATE_RANK]
            acc_ref[a:b, :] = jnp.where(i == 0, 0.0, acc_ref[a:b, :]) + part
            keep = pltpu.make_async_copy(acc_ref.at[a:b, cols(c)], mine_ref.at[a:b], keep_sem.at[j])
            give = pltpu.make_async_remote_copy(
                src_ref=stage_ref.at[a:b], dst_ref=theirs_ref.at[a:b], send_sem=send.at[j],
                recv_sem=recv.at[j], device_id=sibling, device_id_type=_MESH)
            writes += [keep, give]

            @pl.when(i == nt - 1)
            def _(keep=keep, give=give, a=a, b=b):
                keep.start()
                stage_ref[a:b, :] = jnp.where(c == 0, acc_ref[a:b, half:], acc_ref[a:b, :half]).astype(_MX)
                give.start()

        @pl.when(i == nt - 1)
        def _():
            for cp in writes + early:
                cp.wait()

    rows = lambda w: pl.BlockSpec((tm, w), lambda i: (i, 0))
    halves = [jax.ShapeDtypeStruct((D_IN, half), _F32), jax.ShapeDtypeStruct((D_IN, half), _MX)]
    return pl.pallas_call(
        body, name="in_proj_bwd_w", grid=(nt,),
        in_specs=[rows(D_MODEL)] + [rows(b - a) for _, a, b in PIECES] + [_ANY] * no,
        out_specs=[_ANY] * (2 + no),
        out_shape=halves + [jax.ShapeDtypeStruct(sp.half_shape(g.shape), g.dtype) for g, sp in zip(others, osplits)],
        scratch_shapes=[pltpu.VMEM((D_IN, D_MODEL), _F32), pltpu.VMEM((D_IN, half), _MX),
                        pltpu.SemaphoreType.DMA((np_,)), pltpu.SemaphoreType.DMA((np_,)), pltpu.SemaphoreType.DMA((np_,)),
                        pltpu.SemaphoreType.DMA((no,)), pltpu.SemaphoreType.DMA((no,))],
        compiler_params=_cparams(("arbitrary",), VMEM_LIMIT),
    )(h, *dpieces, *others)


def _place():
    return lax.axis_index("x"), lax.axis_index("y"), lax.axis_index("c")


def _other_chips(x, y):
    return [(1 - x, y), (x, 1 - y), (1 - x, 1 - y)]


class _Split(NamedTuple):
    by_rows: bool
    step: int
    size: int

    def half(self, ref, c):
        r, n = ref.shape[-2:]
        if self.by_rows:
            return ref.at[:, pl.ds(pl.multiple_of(c * (n // 2), LANE), n // 2)]
        return ref.at[pl.ds(pl.multiple_of(c * (r // 2), 16), r // 2), :]

    def chip_part(self, ref, k):
        if self.by_rows:
            return ref.at[pl.ds(pl.multiple_of(k * self.step, 16), self.size), :]
        return ref.at[:, pl.ds(pl.multiple_of(k * self.size, LANE), self.size)]

    def half_shape(self, shape):
        r, n = shape
        return (r, n // 2) if self.by_rows else (r // 2, n)

    def part_shape(self, shape):
        r, n = shape
        return (self.size, n) if self.by_rows else (r, self.size)


SPLIT_W_IN_T = _Split(True, WINDOW_STEP, WINDOW_ROWS)
SPLIT_W_O = _Split(True, 256, 256)
SPLIT_W_OUT = _Split(False, 256, 256)


def _gather_weights(shards, splits, fulls, pos_f):
    nw = len(shards)
    t = pos_f.shape[0]

    def body(*refs):
        ins, (pos_ref, c_ref) = refs[:nw], refs[nw:nw + 2]
        outs, tables = refs[nw + 2:2 * nw + 2], refs[2 * nw + 2:2 * nw + 5]
        send_a, recv_a, send_b, recv_b = refs[2 * nw + 5:]
        x, y, c = _place()
        me = 2 * x + y
        peers = _other_chips(x, y)

        def place(i, k, half):
            if splits[i] is None:
                return outs[i].at[k]
            if fulls[i][0] == 4 and len(fulls[i]) == 3:
                whole = outs[i].at[k]
            else:
                whole = splits[i].chip_part(outs[i], k)
            return splits[i].half(whole, half)

        first, passed = [], []
        for i in range(nw):
            src = ins[i] if splits[i] is None else splits[i].half(ins[i], c)
            for j, (px, py) in enumerate(peers):
                cp = pltpu.make_async_remote_copy(
                    src_ref=src, dst_ref=place(i, me, c), send_sem=send_a.at[3 * i + j],
                    recv_sem=recv_a.at[3 * i + j], device_id=(px, py, c), device_id_type=_MESH)
                cp.start()
                first.append(cp)
        _rope_tables_into(pos_ref, c_ref, *tables)
        for i in range(nw):
            for j, (px, py) in enumerate(peers):
                landed = place(i, 2 * px + py, c)
                pltpu.make_async_remote_copy(
                    src_ref=landed, dst_ref=landed, send_sem=send_a.at[3 * i + j], recv_sem=recv_a.at[3 * i + j],
                    device_id=(px, py, c), device_id_type=_MESH).wait_recv()
                if splits[i] is not None:
                    cp = pltpu.make_async_remote_copy(
                        src_ref=landed, dst_ref=landed, send_sem=send_b.at[3 * i + j], recv_sem=recv_b.at[3 * i + j],
                        device_id=(x, y, 1 - c), device_id_type=_MESH)
                    cp.start()
                    passed.append(cp)
        for i in range(nw):
            if splits[i] is None:
                continue
            for j, (px, py) in enumerate(peers):
                theirs = place(i, 2 * px + py, 1 - c)
                pltpu.make_async_remote_copy(
                    src_ref=theirs, dst_ref=theirs, send_sem=send_b.at[3 * i + j], recv_sem=recv_b.at[3 * i + j],
                    device_id=(x, y, 1 - c), device_id_type=_MESH).wait_recv()
        for cp in first + passed:
            cp.wait_send()

    vm = pl.BlockSpec(memory_space=pltpu.VMEM)
    tab = jax.ShapeDtypeStruct((t, LANE), _F32)
    return pl.pallas_call(
        body, name="gather_weights",
        in_specs=[_ANY] * nw + [vm, vm], out_specs=[_ANY] * nw + [vm] * 3,
        out_shape=[jax.ShapeDtypeStruct(f, s.dtype) for f, s in zip(fulls, shards)] + [tab] * 3,
        scratch_shapes=[pltpu.SemaphoreType.DMA((3 * nw,)) for _ in range(4)],
        compiler_params=_cparams(None, VMEM_LIMIT),
    )(*shards, pos_f, _rope_consts())


def _assemble_w_in_t(slots):
    bw = 256
    ov = WINDOW_ROWS - WINDOW_STEP

    def body(s_ref, o_ref):
        for k in range(4):
            base = k * WINDOW_STEP
            lo = 0 if k == 0 else ov
            if k > 0:
                o_ref[base:base + ov, :] = s_ref[k - 1, WINDOW_STEP:WINDOW_ROWS, :] + s_ref[k, 0:ov, :]
            hi = WINDOW_ROWS if k == 3 else WINDOW_STEP
            o_ref[base + lo:base + hi, :] = s_ref[k, lo:hi, :]

    return pl.pallas_call(
        body, name="assemble_w_in_t", grid=(D_MODEL // bw,),
        in_specs=[pl.BlockSpec((4, WINDOW_ROWS, bw), lambda i: (0, 0, i))],
        out_specs=pl.BlockSpec((D_IN, bw), lambda i: (0, i)),
        out_shape=jax.ShapeDtypeStruct((D_IN, D_MODEL), slots.dtype),
        compiler_params=_cparams(("parallel",)),
    )(slots)


def _row_block(rows):
    for cand in (976, 176, 256, 128):
        if rows % cand == 0:
            return cand
    return rows


def _pair_sum(g, r, split, c_arr, name):
    hr, hn = r.shape
    br = _row_block(hr)
    if split is None:
        g_spec = pl.BlockSpec((br, hn), lambda i, c_ref: (i, 0))
    elif split.by_rows:
        g_spec = pl.BlockSpec((br, hn), lambda i, c_ref: (i, c_ref[0]))
    else:
        g_spec = pl.BlockSpec((br, hn), lambda i, c_ref: (c_ref[0] * (hr // br) + i, 0))

    def body(c_ref, g_ref, r_ref, o_ref):
        o_ref[...] = (g_ref[...] + r_ref[...]).astype(o_ref.dtype)

    return pl.pallas_call(
        body, name=name,
        grid_spec=pltpu.PrefetchScalarGridSpec(
            num_scalar_prefetch=1, grid=(hr // br,),
            in_specs=[g_spec, pl.BlockSpec((br, hn), lambda i, c_ref: (i, 0))],
            out_specs=pl.BlockSpec((br, hn), lambda i, c_ref: (i, 0))),
        out_shape=jax.ShapeDtypeStruct(r.shape, _MX),
        compiler_params=_cparams(("parallel",)),
    )(c_arr, g, r)


_HBM = pl.BlockSpec(memory_space=pltpu.HBM)
_SEM = pl.BlockSpec(memory_space=pltpu.SEMAPHORE)
_FLOWS = pltpu.SideEffectType.DATAFLOW_SIDE_EFFECTING


def _chip_exchange_copies(refs, send, recv, splits):
    nw = len(refs) // 2
    x, y, c = _place()
    me = 2 * x + y
    copies = []
    for i in range(nw):
        for px, py in _other_chips(x, y):
            copies.append((splits[i].chip_part(refs[i], 2 * px + py), refs[nw + i].at[me], (px, py, c)))
    return [pltpu.make_async_remote_copy(src_ref=src, dst_ref=dst, send_sem=send.at[k], recv_sem=recv.at[k],
                                         device_id=peer, device_id_type=_MESH)
            for k, (src, dst, peer) in enumerate(copies)]


def _late_gather_copies(refs, send, recv, splits):
    nw = len(refs) // 2
    x, y, c = _place()
    me = 2 * x + y
    copies = []
    for i in range(nw):
        for px, py in [(x, y)] + _other_chips(x, y):
            copies.append((refs[i], splits[i].chip_part(refs[nw + i], me), (px, py, c)))
    return [pltpu.make_async_remote_copy(src_ref=src, dst_ref=dst, send_sem=send.at[k], recv_sem=recv.at[k],
                                         device_id=peer, device_id_type=_MESH)
            for k, (src, dst, peer) in enumerate(copies)]


def _start_copies(name, flying, copies_of, n_copies, after=None):
    first = [] if after is None else [after]

    def body(*refs):
        ins = refs[:len(flying)]
        send, recv = refs[len(flying) + len(first):len(flying) + len(first) + 2]
        token = refs[-1]
        for cp in copies_of(ins, send, recv):
            cp.start()
        token[...] = jnp.zeros_like(token)

    outs = pl.pallas_call(
        body, name=name,
        in_specs=[_HBM] * len(flying) + [_ANY] * len(first),
        out_specs=[_SEM, _SEM] + [_HBM] * len(flying) + [pl.BlockSpec(memory_space=pltpu.VMEM)],
        out_shape=[pltpu.SemaphoreType.DMA((n_copies,)), pltpu.SemaphoreType.DMA((n_copies,))]
        + [pltpu.HBM(f.shape, f.dtype) for f in flying] + [jax.ShapeDtypeStruct((8, LANE), _F32)],
        input_output_aliases={i: 2 + i for i in range(len(flying))},
        compiler_params=pltpu.CompilerParams(has_side_effects=_FLOWS),
    )(*[pltpu.with_memory_space_constraint(f, pltpu.HBM) for f in flying], *first)
    return outs[0], outs[1], outs[2:2 + len(flying)], outs[-1]


def _wait_copies(name, send, recv, flying, copies_of, after):
    def body(*refs):
        ins = refs[:len(flying)]
        send_ref, recv_ref = refs[len(flying):len(flying) + 2]
        for cp in copies_of(ins, send_ref, recv_ref):
            cp.wait_send()
            cp.wait_recv()

    return pl.pallas_call(
        body, name=name,
        in_specs=[_HBM] * len(flying) + [_SEM, _SEM, _ANY],
        out_specs=[_HBM] * len(flying),
        out_shape=[pltpu.HBM(f.shape, f.dtype) for f in flying],
        input_output_aliases={i: i for i in range(len(flying))},
        compiler_params=pltpu.CompilerParams(has_side_effects=_FLOWS),
    )(*flying, send, recv, after)


def _sum_chips(q, p, split, place_arr, name):
    _, hr, hn = q.shape
    if split.by_rows:
        out_shape = (hr, 2 * hn)
        o_spec = pl.BlockSpec((hr, hn), lambda i, pr: (0, pr[0]))
        p_spec = pl.BlockSpec((pl.Element(hr), pl.Element(hn)), lambda i, pr: (pr[1] * split.step, 0))
    else:
        out_shape = (2 * hr, hn)
        o_spec = pl.BlockSpec((hr, hn), lambda i, pr: (pr[0], 0))
        p_spec = pl.BlockSpec((hr, hn), lambda i, pr: (0, pr[1]))

    def body(pr, q_ref, p_ref, o_ref):
        f = lambda k: jnp.where(pr[1] == k, p_ref[...], q_ref[k]).astype(_F32)
        o_ref[...] = ((f(0) + f(1)) + f(2)) + f(3)

    return pl.pallas_call(
        body, name=name,
        grid_spec=pltpu.PrefetchScalarGridSpec(
            num_scalar_prefetch=1, grid=(1,),
            in_specs=[pl.BlockSpec((4, hr, hn), lambda i, pr: (0, 0, 0)), p_spec], out_specs=o_spec),
        out_shape=jax.ShapeDtypeStruct(out_shape, _F32),
        compiler_params=_cparams(("arbitrary",), VMEM_LIMIT),
    )(place_arr, q, p)


def _pair_share(bufs, splits, small):
    nw = len(bufs)

    def body(*refs):
        ins, small_ref, outs, all_ref = refs[:nw], refs[nw], refs[nw + 1:2 * nw + 1], refs[2 * nw + 1]
        send, recv, s_send, s_recv = refs[2 * nw + 2:]
        x, y, c = _place()
        copies = []
        for r in range(1, 8):
            peer = (1 - x if r & 4 else x, 1 - y if r & 2 else y, 1 - c if r & 1 else c)
            cp = pltpu.make_async_remote_copy(
                src_ref=small_ref, dst_ref=all_ref.at[4 * x + 2 * y + c], send_sem=s_send.at[r - 1],
                recv_sem=s_recv.at[r - 1], device_id=peer, device_id_type=_MESH)
            cp.start()
            copies.append(cp)
        for i in range(nw):
            cp = pltpu.make_async_remote_copy(
                src_ref=splits[i].half(ins[i], c), dst_ref=splits[i].half(outs[i], c), send_sem=send.at[i],
                recv_sem=recv.at[i], device_id=(x, y, 1 - c), device_id_type=_MESH)
            cp.start()
            copies.append(cp)
        for cp in copies:
            cp.wait()

    return pl.pallas_call(
        body, name="grad_pair_share",
        in_specs=[_ANY] * (nw + 1), out_specs=[_ANY] * (nw + 1),
        out_shape=[jax.ShapeDtypeStruct(b.shape, b.dtype) for b in bufs]
        + [jax.ShapeDtypeStruct((8,) + small.shape, small.dtype)],
        input_output_aliases={i: i for i in range(nw)},
        scratch_shapes=[pltpu.SemaphoreType.DMA((nw,)), pltpu.SemaphoreType.DMA((nw,)),
                        pltpu.SemaphoreType.DMA((7,)), pltpu.SemaphoreType.DMA((7,))],
    )(*bufs, small)


def _sum_devices(parts, own, dev_arr):
    def body(dev, p_ref, own_ref, tot_ref):
        f = lambda d: jnp.where(dev[0] == d, own_ref[...], p_ref[d])
        acc = f(0)
        for d in range(1, 8):
            acc = acc + f(d)
        tot_ref[...] = acc

    return pl.pallas_call(
        body, name="small_sum",
        grid_spec=pltpu.PrefetchScalarGridSpec(
            num_scalar_prefetch=1, grid=(1,),
            in_specs=[pl.BlockSpec(parts.shape, lambda i, dev: (0, 0, 0)), pl.BlockSpec(own.shape, lambda i, dev: (0, 0))],
            out_specs=pl.BlockSpec(own.shape, lambda i, dev: (0, 0))),
        out_shape=jax.ShapeDtypeStruct(own.shape, own.dtype),
    )(dev_arr, parts, own)


def _adam_update(w, g, m, v):
    m2 = ADAM_B1 * m + (1.0 - ADAM_B1) * g
    v2 = ADAM_B2 * v + (1.0 - ADAM_B2) * (g * g)
    m_hat = m2 / (1.0 - ADAM_B1 ** ADAM_STEP)
    v_hat = v2 / (1.0 - ADAM_B2 ** ADAM_STEP)
    return -ADAM_LR * (m_hat / (jnp.sqrt(v_hat) + ADAM_EPS) + ADAM_WD * w), m2, v2


SMALL_AT = dict(g_in=(0, 0), g_final=(1, 0), g_gla_norm=(2, 0), b_alpha=(2, B_WIDTH), attn_sinks=(2, B_WIDTH + B_KEY_WIDTH))
LOSS_AT = (2, B_WIDTH + B_KEY_WIDTH + LANE)
WUP_ROWS = (3, 7)


def _adamw_small(tot, g_wup, params):
    names = list(params)

    def body(*refs):
        tot_ref, gw_ref = refs[0], refs[1]
        ins = refs[2:2 + 3 * len(names)]
        outs = refs[2 + 3 * len(names):]
        for i, nm in enumerate(names):
            w_ref, m_ref, v_ref = ins[3 * i:3 * i + 3]
            if nm in SMALL_AT:
                r, a = SMALL_AT[nm]
                g = tot_ref[r:r + 1, a:a + w_ref.shape[1]]
            else:
                g = gw_ref[...]
            d, m2, v2 = _adam_update(w_ref[...], g, m_ref[...], v_ref[...])
            for o_ref, val in zip(outs[4 * i:4 * i + 4], (g, d, m2, v2)):
                o_ref[...] = val

    vm = pl.BlockSpec(memory_space=pltpu.VMEM)
    flat = [a for nm in names for a in params[nm]]
    out_shape = [jax.ShapeDtypeStruct(params[nm][0].shape, _F32) for nm in names for _ in range(4)]
    outs = pl.pallas_call(
        body, name="adamw_small", in_specs=[vm] * (2 + len(flat)), out_specs=[vm] * len(out_shape), out_shape=out_shape,
    )(tot, g_wup, *flat)
    return {nm: tuple(outs[4 * i:4 * i + 4]) for i, nm in enumerate(names)}


def _adamw(w, g, m, v, name):
    lead = w.shape[0] != 1
    r, n = (w.shape[0], w.shape[2]) if lead else w.shape[1:]
    br = r
    for cand in (256, 244, 128):
        if r > cand and r % cand == 0:
            br = cand
            break

    def body(w_ref, g_ref, m_ref, v_ref, d_ref, nm_ref, nv_ref):
        d_ref[...], nm_ref[...], nv_ref[...] = _adam_update(w_ref[...], g_ref[...], m_ref[...], v_ref[...])

    blk = pl.BlockSpec((br, 1, n), lambda i: (i, 0, 0)) if lead else pl.BlockSpec((None, br, n), lambda i: (0, i, 0))
    shp = jax.ShapeDtypeStruct(w.shape, _F32)
    return pl.pallas_call(
        body, name=name, grid=(r // br,),
        in_specs=[blk] * 4, out_specs=[blk] * 3, out_shape=[shp] * 3,
        compiler_params=_cparams(("parallel",)),
    )(w, g, m, v)


def kernel(x, positions, g_in, w_in, w_alpha_up, b_alpha, attn_sinks, g_gla_norm, w_out_a, w_out_b, w_o, g_final, loss_target, m_g_in, m_w_in, m_w_alpha_up, m_b_alpha, m_attn_sinks, m_g_gla_norm, m_w_out_a, m_w_out_b, m_w_o, m_g_final, v_g_in, v_w_in, v_w_alpha_up, v_b_alpha, v_attn_sinks, v_g_gla_norm, v_w_out_a, v_w_out_b, v_w_o, v_g_final):
    nseq, seq, _ = x.shape
    t = nseq * seq
    cx, cy, cc = _place()
    chip = 2 * cx + cy
    c_arr = jnp.reshape(cc, (1,)).astype(jnp.int32)

    tr = lambda w: jnp.transpose(w, (2, 0, 1))
    w_in_t = tr(w_in).reshape(SHARD, D_MODEL).astype(_MX)
    pad = WINDOW_ROWS - SHARD
    window = lax.switch(chip, [lambda w, k=k: jnp.pad(w, ((4 * k, pad - 4 * k), (0, 0))) for k in range(4)], w_in_t)
    shards = [window, w_alpha_up[0].astype(_MX)]
    late = [w_out_a[0].astype(_MX), w_out_b[0].astype(_MX), w_o[0].astype(_MX)]
    late_splits = [SPLIT_W_OUT, SPLIT_W_OUT, SPLIT_W_O]
    splits = [SPLIT_W_IN_T, None]
    fulls = [(4, WINDOW_ROWS, D_MODEL), (4, B_GATE_RANK, B_KEY_WIDTH // 4)]
    pos_f = positions.astype(_F32).reshape(t, 1)
    win_g, wup_g, cos, sa, sb = _gather_weights(shards, splits, fulls, pos_f)
    late_copies = lambda refs, send, recv: _late_gather_copies(refs, send, recv, late_splits)
    late_full = [lax.empty(shape, _MX) for shape in ((A_WIDTH, D_MODEL), (B_WIDTH, D_MODEL), (D_MODEL, D_MODEL))]
    l_send, l_recv, l_flying, l_token = _start_copies("late_gather_start", late + late_full, late_copies,
                                                      4 * len(late), after=win_g)
    win_g = lax.dynamic_update_slice(win_g, window[None], (chip, 0, 0))
    wup_g = lax.dynamic_update_slice(wup_g, shards[1][None], (chip, 0, 0))
    wt = _assemble_w_in_t(win_g)
    wup = jnp.concatenate([jnp.transpose(wup_g, (1, 0, 2)).reshape(B_GATE_RANK, B_KEY_WIDTH),
                           jnp.zeros((LANE - B_GATE_RANK, B_KEY_WIDTH), _MX)], axis=0)

    x2 = x.reshape(t, D_MODEL)
    tgt = loss_target.reshape(t, D_MODEL)
    sinks = attn_sinks.reshape(A_HEADS)
    gf = g_final.reshape(1, D_MODEL)

    h, qkv, za, qkb, vb, zb, alr, ga, gb = _in_proj(x2, g_in + l_token[0, 0], wt, cos, sa, sb)
    oa = _attn_fwd(qkv, za, sinks, nseq)
    ob, oraw, sst = _gla_fwd(qkb, vb, zb, alr, wup, b_alpha, g_gla_norm, nseq)

    wa, wb, wo = _wait_copies("late_gather_wait", l_send, l_recv, l_flying, late_copies, ob)[len(late):]
    dh2, doa, dob, dga, dgb, dwa, dwb, dwo, dgf, lossv = _merge_loss(oa, ob, ga, gb, x2, tgt, wa, wb, wo, gf)

    dqkv, dza, dsink = _attn_bwd(qkv, za, doa, sinks, cos, sa, sb, nseq)
    dqkb, dvb, dzb, dalr, dwup, dba, dgn = _gla_bwd(qkb, vb, zb, alr, oraw, dob, sst, wup, b_alpha, g_gla_norm, nseq)
    dpieces = [dqkv, dza, dqkb, dvb, dzb, dalr, dga, dgb]
    gsplits = [SPLIT_W_IN_T, SPLIT_W_OUT, SPLIT_W_OUT, SPLIT_W_O]
    names = ("w_in", "w_out_a", "w_out_b", "w_o")
    dwin_mine, *from_sibling = _in_proj_bwd_w(h, dpieces, [dwa, dwb, dwo], gsplits[1:])
    pair_sums = [_pair_sum(g, r, sp, c_arr, "pair_sum_" + nm)
                 for g, r, sp, nm in zip([dwin_mine, dwa, dwb, dwo], from_sibling, [None] + gsplits[1:], names)]
    exchange = lambda refs, send, recv: _chip_exchange_copies(refs, send, recv, gsplits)
    lands = [lax.empty((4,) + sp.part_shape(p.shape), p.dtype) for p, sp in zip(pair_sums, gsplits)]
    send, recv, flying, token = _start_copies("grad_chip_exchange_start", pair_sums + lands, exchange, 3 * len(lands))
    grad_x2, dgin = _in_proj_bwd_x(dpieces, wt, x2, dh2, g_in + token[0, 0])
    landed = _wait_copies("grad_chip_exchange_wait", send, recv, flying, exchange, grad_x2)
    place_arr = jnp.stack([cc, chip]).astype(jnp.int32)
    reduced = [_sum_chips(q, p, sp, place_arr, "chip_sum_" + nm)
               for q, p, sp, nm in zip(landed[len(lands):], landed[:len(lands)], gsplits, names)]
    row2 = jnp.concatenate([dgn, dba, jnp.pad(dsink[:, 0].reshape(1, A_HEADS), ((0, 0), (0, LANE - A_HEADS))),
                            jnp.pad(jnp.sum(lossv, axis=1, keepdims=True), ((0, 0), (0, LANE - 1)))], axis=1)
    small = jnp.concatenate([dgin, dgf, row2, dwup[:B_GATE_RANK].reshape(WUP_ROWS[1] - WUP_ROWS[0], D_MODEL),
                             jnp.zeros((1, D_MODEL), _F32)], axis=0)
    g_window, g_wa, g_wb, g_wo, small_parts = _pair_share(reduced, gsplits, small)
    g_win_t = lax.switch(chip, [lambda w, k=k: w[4 * k:4 * k + SHARD].reshape(SHARD, 1, D_MODEL) for k in range(4)],
                         g_window)
    dev_arr = jnp.reshape(2 * chip + cc, (1,)).astype(jnp.int32)
    tot = _sum_devices(small_parts, small, dev_arr)
    loss = tot[LOSS_AT]
    nup = B_KEY_WIDTH // 4
    g_wup = lax.dynamic_slice(tot[WUP_ROWS[0]:WUP_ROWS[1]].reshape(B_GATE_RANK, B_KEY_WIDTH), (0, chip * nup),
                              (B_GATE_RANK, nup))

    row = lambda a: a.reshape(1, -1)
    sm = _adamw_small(tot, g_wup, dict(
        g_in=(g_in, m_g_in, v_g_in), g_final=(row(g_final), row(m_g_final), row(v_g_final)),
        g_gla_norm=(g_gla_norm, m_g_gla_norm, v_g_gla_norm), b_alpha=(b_alpha, m_b_alpha, v_b_alpha),
        attn_sinks=(attn_sinks, m_attn_sinks, v_attn_sinks),
        w_alpha_up=(w_alpha_up[0], m_w_alpha_up[0], v_w_alpha_up[0])))
    sm["g_final"] = tuple(a.reshape(D_MODEL) for a in sm["g_final"])
    sm["w_alpha_up"] = tuple(a[None] for a in sm["w_alpha_up"])

    untr = lambda a: jnp.transpose(a, (1, 2, 0))
    big = dict(w_in=tuple(untr(a) for a in (g_win_t,) + tuple(_adamw(tr(w_in), g_win_t, tr(m_w_in), tr(v_w_in), "adamw_w_in"))))
    for nm, w, g, m, v in (("w_out_a", w_out_a, g_wa, m_w_out_a, v_w_out_a),
                           ("w_out_b", w_out_b, g_wb, m_w_out_b, v_w_out_b), ("w_o", w_o, g_wo, m_w_o, v_w_o)):
        big[nm] = (g[None],) + tuple(_adamw(w, g[None], m, v, "adamw_" + nm))

    order = ("g_in", "w_in", "w_alpha_up", "b_alpha", "attn_sinks", "g_gla_norm", "w_out_a", "w_out_b", "w_o", "g_final")
    outs = [big[nm][kind] if nm in big else sm[nm][kind] for kind in range(4) for nm in order]
    return (loss, grad_x2.reshape(x.shape), *outs)
```

```python
import math
from typing import NamedTuple

import numpy as np
import jax
import jax.numpy as jnp
from jax import lax
from jax.experimental import pallas as pl
from jax.experimental.pallas import tpu as pltpu

D_MODEL = 1024
A_HEADS, A_KV_HEADS, A_HEAD_DIM = 8, 2, 64
A_GROUP = A_HEADS // A_KV_HEADS
A_WIDTH, A_KV_WIDTH = 512, 128
BLOCK = 128
ROPE_THETA = 500000.0
ROPE_DIM = 16
B_HEADS, B_KEY_DIM, B_VAL_DIM = 4, 64, 128
B_KEY_WIDTH, B_WIDTH = 256, 512
B_GATE_RANK = 16
B_GATE_TEMP = 16.0
B_CHUNK = 64
NORM_EPS = 1e-6
NEG_BIG = -1e30
D_IN = 4880

ADAM_LR, ADAM_B1, ADAM_B2, ADAM_EPS, ADAM_WD, ADAM_STEP = 0.001, 0.9, 0.999, 1e-08, 0.01, 10

LANE = 128
ALR_AT = 2816
PIECES = (("qkv", 0, 768), ("za", 768, 1280), ("qkb", 1280, 1792), ("vb", 1792, 2304),
          ("zb", 2304, 2816), ("alr", ALR_AT, ALR_AT + LANE), ("ga", 2832, 3856), ("gb", 3856, 4880))
SHARD = D_IN // 4
WINDOW_STEP = 1216
WINDOW_ROWS = 1232

GLA_BLOCK = 256
GLA_FWD_BLOCK = 1024
GLA_BWD_TILE = 4
MERGE_SLAB = 16
VMEM_LIMIT = 56 * 1024 * 1024

_F32 = jnp.float32
_MX = jnp.bfloat16
_ST = jnp.bfloat16

_MESH = pl.DeviceIdType.MESH
_ANY = pl.BlockSpec(memory_space=pl.ANY)


def _cparams(sem=None, vmem=None):
    return pltpu.CompilerParams(dimension_semantics=sem, vmem_limit_bytes=vmem)


def _dot(a, b):
    return jnp.dot(a.astype(_MX), b.astype(_MX), preferred_element_type=_F32)


def _dot_nt(a, b):
    return lax.dot_general(a.astype(_MX), b.astype(_MX), (((1,), (1,)), ((), ())),
                           preferred_element_type=_F32)


def _dot_tn(a, b):
    return lax.dot_general(a.astype(_MX), b.astype(_MX), (((0,), (0,)), ((), ())),
                           preferred_element_type=_F32)


def _dot_ones(ones_mat, v):
    o = ones_mat.astype(jnp.bfloat16)
    v0 = v.astype(jnp.bfloat16)
    v1 = (v - v0.astype(_F32)).astype(jnp.bfloat16)
    d = lambda t: jnp.dot(o, t, preferred_element_type=_F32)
    return d(v0) + d(v1)


def _sigmoid(x):
    return 0.5 * jnp.tanh(0.5 * x) + 0.5


def _log_sigmoid(x):
    return jnp.minimum(x, 0.0) - jnp.log(1.0 + jnp.exp(-jnp.abs(x)))


def _lane_tile(t, width):
    reps = width // t.shape[1]
    return t if reps == 1 else jnp.tile(t, (1, reps))


def _rope(t, cos, sa, sb, sign):
    w = t.shape[1]
    rot = pltpu.roll(t, w - 8, 1) * _lane_tile(sa, w) + pltpu.roll(t, 8, 1) * _lane_tile(sb, w)
    return t * _lane_tile(cos, w) + sign * rot


def _rms_bwd(dy_g, n, r):
    return r * (dy_g - n * jnp.mean(dy_g * n, axis=-1, keepdims=True))


ROPE_ROWS = 256


def _rope_consts():
    lane = np.arange(LANE) % A_HEAD_DIM
    half = ROPE_DIM // 2
    inv = np.exp((np.float32(-math.log(ROPE_THETA)) * np.arange(half, dtype=np.float32)) * np.float32(2.0 / ROPE_DIM))
    consts = np.zeros((8, LANE), np.float32)
    consts[0] = np.where(lane < ROPE_DIM, inv[lane % half], 0.0)
    consts[1] = np.where(lane < half, -1.0, 0.0)
    consts[2] = np.where((lane >= half) & (lane < ROPE_DIM), 1.0, 0.0)
    return jnp.asarray(consts)


def _rope_tables_into(pos_ref, c_ref, cos_ref, sa_ref, sb_ref):
    def rows_of(b, carry):
        rows = pl.ds(pl.multiple_of(b * ROPE_ROWS, ROPE_ROWS), ROPE_ROWS)
        ang = pos_ref[rows, :] * c_ref[0:1, :]
        s = jnp.sin(ang)
        cos_ref[rows, :] = jnp.cos(ang)
        sa_ref[rows, :] = s * c_ref[1:2, :]
        sb_ref[rows, :] = s * c_ref[2:3, :]
        return carry

    lax.fori_loop(0, pos_ref.shape[0] // ROPE_ROWS, rows_of, 0)


def _in_proj(x2, g_in, wt, cos, sa, sb):
    t = x2.shape[0]
    tm = min(t, 512)

    def body(x_ref, g_ref, w_ref, cos_ref, sa_ref, sb_ref, h_ref, qkv_ref, za_ref, qkb_ref,
             vb_ref, zb_ref, alr_ref, ga_ref, gb_ref):
        xv = x_ref[...]
        r = lax.rsqrt(jnp.mean(xv * xv, axis=-1, keepdims=True) + NORM_EPS)
        h = (xv * r * g_ref[...]).astype(_MX)
        h_ref[...] = h.astype(_ST)
        outs = dict(za=za_ref, qkb=qkb_ref, vb=vb_ref, zb=zb_ref, alr=alr_ref, ga=ga_ref, gb=gb_ref)
        for name, a, b in PIECES:
            p = _dot_nt(h, w_ref[a:b, :])
            if name == "qkv":
                c, s1, s2 = cos_ref[...], sa_ref[...], sb_ref[...]
                qkv_ref[:, 0:512] = _rope(p[:, 0:512], c, s1, s2, 1.0).astype(_ST)
                qkv_ref[:, 512:640] = _rope(p[:, 512:640], c, s1, s2, 1.0).astype(_ST)
                qkv_ref[:, 640:768] = p[:, 640:768].astype(_ST)
            else:
                outs[name][...] = p.astype(outs[name].dtype)

    rows = lambda w: pl.BlockSpec((tm, w), lambda i: (i, 0))
    shp = lambda name, w: jax.ShapeDtypeStruct((t, w), _F32 if name == "qkb" else _ST)
    widths = [D_MODEL] + [b - a for _, a, b in PIECES]
    return pl.pallas_call(
        body, name="in_proj", grid=(t // tm,),
        in_specs=[rows(D_MODEL), pl.BlockSpec((1, D_MODEL), lambda i: (0, 0)),
                  pl.BlockSpec((D_IN, D_MODEL), lambda i: (0, 0), pipeline_mode=pl.Buffered(1)),
                  rows(LANE), rows(LANE), rows(LANE)],
        out_specs=[rows(w) for w in widths],
        out_shape=[shp(n, w) for n, w in zip(["h"] + [p[0] for p in PIECES], widths)],
        compiler_params=_cparams(("parallel",), VMEM_LIMIT),
    )(x2, g_in, wt, cos, sa, sb)


def _attn_operands(k_prev, k_cur, v_prev, v_cur, want_bwd):
    kf = jnp.concatenate([k_prev, k_cur], axis=0).astype(_F32) * (A_HEAD_DIM ** -0.5)
    vf = jnp.concatenate([v_prev, v_cur], axis=0).astype(_F32)
    lo = lax.broadcasted_iota(jnp.int32, (1, LANE), 1) < 64

    def on_lanes(a):
        sw = pltpu.roll(a, 64, 1)
        z = jnp.zeros_like(a)
        return [[jnp.where(lo, a, z).astype(_MX), jnp.where(lo, z, sw).astype(_MX)],
                [jnp.where(lo, sw, z).astype(_MX), jnp.where(lo, z, a).astype(_MX)]]

    def on_rows(a):
        at = a.T.astype(_MX)
        z = jnp.zeros((64, at.shape[1]), _MX)
        top, bot = at[0:64], at[64:128]
        return [[jnp.concatenate([top, z], axis=0), jnp.concatenate([z, top], axis=0)],
                [jnp.concatenate([bot, z], axis=0), jnp.concatenate([z, bot], axis=0)]]

    ops = dict(k_lanes=on_lanes(kf), v_rows=on_rows(vf), lo=lo)
    if want_bwd:
        ops.update(v_lanes=on_lanes(vf), k_rows=on_rows(kf))
    return ops


def _attn_valid(n):
    kj = lax.broadcasted_iota(jnp.int32, (2 * BLOCK, 2 * BLOCK), 0) - BLOCK
    qi = lax.broadcasted_iota(jnp.int32, (2 * BLOCK, 2 * BLOCK), 1) & (BLOCK - 1)
    return (kj <= qi) & (qi - kj < BLOCK) & ((n > 0) | (kj >= 0))


def _attn_sinks(sink_ref, h_a, h_b):
    first = lax.broadcasted_iota(jnp.int32, (1, 2 * BLOCK), 1) < BLOCK
    return jnp.where(first, sink_ref[h_a], sink_ref[h_b])


def _attn_softmax_t(k_lanes, q_pair, valid, sink):
    s = jnp.where(valid, _dot_nt(k_lanes, q_pair), NEG_BIG)
    m = jnp.maximum(jnp.max(s, axis=0, keepdims=True), sink)
    e = jnp.exp(s - m)
    e_sink = jnp.exp(sink - m)
    inv = 1.0 / (jnp.sum(e, axis=0, keepdims=True) + e_sink)
    return e, e_sink, inv


ATTN_TILE = 8


def _attn_kv(qkv_ref, kvp_ref, j):
    rows = slice(j * BLOCK, (j + 1) * BLOCK)
    if j == 0:
        k_prev, v_prev = kvp_ref[:, 0:128], kvp_ref[:, 128:256]
    else:
        before = slice((j - 1) * BLOCK, j * BLOCK)
        k_prev, v_prev = qkv_ref[before, 512:640], qkv_ref[before, 640:768]
    return k_prev, qkv_ref[rows, 512:640], v_prev, qkv_ref[rows, 640:768]


def _attn_fwd(qkv, za, sinks, nseq):
    t = qkv.shape[0]
    nblk = min(ATTN_TILE, t // nseq // BLOCK)
    tile = nblk * BLOCK
    nt = t // nseq // tile

    def body(sink_ref, qkv_ref, kvp_ref, za_ref, oa_ref):
        for j in range(nblk):
            rows = slice(j * BLOCK, (j + 1) * BLOCK)
            ops = _attn_operands(*_attn_kv(qkv_ref, kvp_ref, j), False)
            valid = _attn_valid(nblk * pl.program_id(1) + j)[:, 0:BLOCK]
            for pr in range(A_HEADS // 2):
                lanes = slice(pr * LANE, (pr + 1) * LANE)
                g = pr // (A_GROUP // 2)
                q_pair = qkv_ref[rows, lanes]
                ot = None
                for half in range(2):
                    e, _, inv = _attn_softmax_t(ops["k_lanes"][g][half], q_pair, valid, sink_ref[2 * pr + half])
                    part = _dot(ops["v_rows"][g][half], e) * inv
                    ot = part if ot is None else ot + part
                z = za_ref[rows, lanes].astype(_F32)
                oa_ref[rows, lanes] = (ot.T * (z * _sigmoid(z))).astype(_ST)

    cur = lambda w: pl.BlockSpec((tile, w), lambda s, n: (s * nt + n, 0))
    return pl.pallas_call(
        body, name="attn_fwd", grid=(nseq, nt),
        in_specs=[pl.BlockSpec(memory_space=pltpu.SMEM), cur(768),
                  pl.BlockSpec((BLOCK, 256), lambda s, n: (nblk * (s * nt + n) - jnp.minimum(n, 1), 2)),
                  cur(512)],
        out_specs=cur(512), out_shape=jax.ShapeDtypeStruct((t, A_WIDTH), _ST),
        compiler_params=_cparams(("parallel", "arbitrary")),
    )(sinks, qkv, qkv, za)


def _attn_bwd(qkv, za, doa, sinks, cos, sa, sb, nseq):
    t = qkv.shape[0]
    nblk = min(ATTN_TILE, t // nseq // BLOCK)
    tile = nblk * BLOCK
    nt = t // nseq // tile

    def body(sink_ref, qkv_ref, kvp_ref, za_ref, doa_ref, cos_ref, sa_ref, sb_ref,
             dqkv_ref, dza_ref, dsink_ref, ck_ref, cv_ref):
        s_id, i = pl.program_id(0), pl.program_id(1)

        @pl.when((s_id == 0) & (i == 0))
        def _():
            dsink_ref[...] = jnp.zeros_like(dsink_ref)

        @pl.when(i == 0)
        def _():
            ck_ref[...] = jnp.zeros_like(ck_ref)
            cv_ref[...] = jnp.zeros_like(cv_ref)

        carry_k, carry_v = ck_ref[...], cv_ref[...]
        for j in reversed(range(nblk)):
            rows = slice(j * BLOCK, (j + 1) * BLOCK)
            ops = _attn_operands(*_attn_kv(qkv_ref, kvp_ref, j), True)
            lo = ops["lo"]
            valid = _attn_valid(nblk * (nt - 1 - i) + j)
            dk_acc, dv_acc, dq_pairs = [], [], []
            for g in range(A_KV_HEADS):
                pairs = [slice((2 * g + p) * LANE, (2 * g + p + 1) * LANE) for p in range(2)]
                q_both = jnp.concatenate([qkv_ref[rows, p] for p in pairs], axis=0)
                q_f = q_both.astype(_F32)
                z = [za_ref[rows, p].astype(_F32) for p in pairs]
                sz = [_sigmoid(v) for v in z]
                d_oa = [doa_ref[rows, p].astype(_F32) for p in pairs]
                d_att = jnp.concatenate([d_oa[p] * (z[p] * sz[p]) for p in range(2)], axis=0)
                zero = jnp.zeros_like(d_att)
                ot, dqt, ds_all, pn_all, qz_all, daz_all = None, None, [], [], [], []
                for half in range(2):
                    heads = (4 * g + half, 4 * g + 2 + half)
                    e, e_sink, inv = _attn_softmax_t(ops["k_lanes"][g][half], q_both, valid,
                                                     _attn_sinks(sink_ref, *heads))
                    pn = e * inv
                    dpt = _dot_nt(ops["v_lanes"][g][half], d_att)
                    delta = jnp.sum(pn * dpt, axis=0, keepdims=True)
                    ds = (pn * (dpt - delta)).astype(_MX)
                    pn = pn.astype(_MX)
                    d_sink = e_sink * inv * delta
                    for p, h in enumerate(heads):
                        dsink_ref[h:h + 1, :] = dsink_ref[h:h + 1, :] - jnp.sum(d_sink[:, p * BLOCK:(p + 1) * BLOCK])
                    o_part = _dot(ops["v_rows"][g][half], pn)
                    dq_part = _dot(ops["k_rows"][g][half], ds)
                    ot = o_part if ot is None else ot + o_part
                    dqt = dq_part if dqt is None else dqt + dq_part
                    mine = lo if half == 0 else jnp.logical_not(lo)
                    ds_all.append(ds)
                    pn_all.append(pn)
                    qz_all.append(jnp.where(mine, q_f, zero).astype(_MX))
                    daz_all.append(jnp.where(mine, d_att, zero).astype(_MX))
                dk_acc.append(_dot(jnp.concatenate(ds_all, axis=1), jnp.concatenate(qz_all, axis=0)))
                dv_acc.append(_dot(jnp.concatenate(pn_all, axis=1), jnp.concatenate(daz_all, axis=0)))
                for p, lanes in enumerate(pairs):
                    cols = slice(p * BLOCK, (p + 1) * BLOCK)
                    dza_ref[rows, lanes] = (d_oa[p] * ot[:, cols].T * (sz[p] * (1.0 + z[p] * (1.0 - sz[p])))).astype(_ST)
                    dq_pairs.append(dqt[:, cols].T)

            def fold(acc, scale):
                both = [a + pltpu.roll(a, 64, 1) for a in acc]
                return jnp.where(lo, both[0], both[1]) * scale

            dk_full = fold(dk_acc, A_HEAD_DIM ** -0.5)
            dv_full = fold(dv_acc, 1.0)
            dk_cur, dv_cur = dk_full[BLOCK:] + carry_k, dv_full[BLOCK:] + carry_v
            carry_k, carry_v = dk_full[:BLOCK], dv_full[:BLOCK]
            c, s1, s2 = cos_ref[rows, :], sa_ref[rows, :], sb_ref[rows, :]
            dqkv_ref[rows, 0:512] = _rope(jnp.concatenate(dq_pairs, axis=1), c, s1, s2, -1.0).astype(_ST)
            dqkv_ref[rows, 512:640] = _rope(dk_cur, c, s1, s2, -1.0).astype(_ST)
            dqkv_ref[rows, 640:768] = dv_cur.astype(_ST)
        ck_ref[...] = carry_k
        cv_ref[...] = carry_v

    cur = lambda w: pl.BlockSpec((tile, w), lambda s, i: (s * nt + nt - 1 - i, 0))
    return pl.pallas_call(
        body, name="attn_bwd", grid=(nseq, nt),
        in_specs=[pl.BlockSpec(memory_space=pltpu.SMEM), cur(768),
                  pl.BlockSpec((BLOCK, 256),
                               lambda s, i: (nblk * (s * nt + nt - 1 - i) - jnp.minimum(nt - 1 - i, 1), 2)),
                  cur(512), cur(512), cur(LANE), cur(LANE), cur(LANE)],
        out_specs=[cur(768), cur(512), pl.BlockSpec((8, LANE), lambda s, i: (0, 0))],
        out_shape=[jax.ShapeDtypeStruct((t, 768), _ST), jax.ShapeDtypeStruct((t, 512), _ST),
                   jax.ShapeDtypeStruct((8, LANE), _F32)],
        scratch_shapes=[pltpu.VMEM((BLOCK, A_KV_WIDTH), _F32), pltpu.VMEM((BLOCK, A_KV_WIDTH), _F32)],
        compiler_params=_cparams(("arbitrary", "arbitrary")),
    )(sinks, qkv, qkv, za, doa, cos, sa, sb)


def _gla_chunk_terms(la, qkb_ref, r0):
    g = la[r0:r0 + B_CHUNK, :]
    ri = lax.broadcasted_iota(jnp.int32, (B_CHUNK, B_CHUNK), 0)
    ci = lax.broadcasted_iota(jnp.int32, (B_CHUNK, B_CHUNK), 1)
    cum = _dot_ones((ri >= ci).astype(_F32), g)
    last = cum[B_CHUNK - 1:B_CHUNK, :]
    mid = cum[B_CHUNK // 2 - 1:B_CHUNK // 2, :]
    q = qkb_ref[r0:r0 + B_CHUNK, 0:B_KEY_WIDTH].astype(_F32) * (B_KEY_DIM ** -0.5)
    k = qkb_ref[r0:r0 + B_CHUNK, B_KEY_WIDTH:2 * B_KEY_WIDTH].astype(_F32)
    e_q, e_k, e_l, e_c = jnp.exp(cum - mid), jnp.exp(mid - cum), jnp.exp(last - cum), jnp.exp(cum)
    dec_col = jnp.exp(jnp.sum(g.T, axis=1, keepdims=True))
    return dict(qm=q * e_q, km=k * e_k, kl=k * e_l, qc=q * e_c, e_q=e_q, e_k=e_k, e_l=e_l, e_c=e_c,
                dec_col=dec_col, dec_row=jnp.exp(last), causal=ri >= ci, ri=ri)


def _gate_logits(alr_ref, wup_ref, b_ref):
    return _dot(alr_ref[...], wup_ref[...]) + b_ref[...]


def _gla_fwd(qkb, vb, zb, alr, wup, b_alpha, gn, nseq):
    t = qkb.shape[0]
    tb = min(GLA_FWD_BLOCK, t // nseq)
    nblk = t // nseq // tb
    cpb = tb // B_CHUNK

    def body(qkb_ref, vb_ref, zb_ref, alr_ref, wup_ref, b_ref, gn_ref, ob_ref, oraw_ref, sst_ref, s_ref):
        @pl.when(pl.program_id(1) == 0)
        def _():
            s_ref[...] = jnp.zeros_like(s_ref)

        la = _log_sigmoid(_gate_logits(alr_ref, wup_ref, b_ref)) * (1.0 / B_GATE_TEMP)
        terms = [_gla_chunk_terms(la, qkb_ref, c * B_CHUNK) for c in range(cpb)]
        o_intra, inc = {}, {}
        for c, tm in enumerate(terms):
            for h in range(B_HEADS):
                kl_, vl_ = slice(h * 64, (h + 1) * 64), slice(h * 128, (h + 1) * 128)
                v = vb_ref[c * B_CHUNK:(c + 1) * B_CHUNK, vl_]
                a = jnp.where(tm["causal"], _dot_nt(tm["qm"][:, kl_], tm["km"][:, kl_]), 0.0)
                o_intra[c, h] = _dot(a, v)
                inc[c, h] = _dot_tn(tm["kl"][:, kl_], v)
        o_heads = {}
        for h in range(B_HEADS):
            kl_ = slice(h * 64, (h + 1) * 64)
            st = s_ref[kl_, :]
            for c, tm in enumerate(terms):
                sst_ref[c, kl_, :] = st
                o_heads[c, h] = o_intra[c, h] + _dot(tm["qc"][:, kl_], st)
                st = tm["dec_col"][kl_, :] * st + inc[c, h]
            s_ref[kl_, :] = st
        o = jnp.concatenate([jnp.concatenate([o_heads[c, h] for h in range(B_HEADS)], axis=1)
                             for c in range(cpb)], axis=0)
        oraw_ref[...] = o
        z = zb_ref[...].astype(_F32)
        gate = z * _sigmoid(z)
        for h in range(B_HEADS):
            vl_ = slice(h * 128, (h + 1) * 128)
            oh = o[:, vl_]
            r = lax.rsqrt(jnp.mean(oh * oh, axis=-1, keepdims=True) + NORM_EPS)
            ob_ref[:, vl_] = ((oh * r) * gn_ref[:, vl_] * gate[:, vl_]).astype(_ST)

    rows = lambda w: pl.BlockSpec((tb, w), lambda s, i: (s * nblk + i, 0))
    full = lambda a, b: pl.BlockSpec((a, b), lambda s, i: (0, 0))
    return pl.pallas_call(
        body, name="gla_fwd", grid=(nseq, nblk),
        in_specs=[rows(512), rows(512), rows(512), rows(LANE), full(LANE, B_KEY_WIDTH),
                  full(1, B_KEY_WIDTH), full(1, B_WIDTH)],
        out_specs=[rows(512), rows(512),
                   pl.BlockSpec((cpb, B_KEY_WIDTH, B_VAL_DIM), lambda s, i: (s * nblk + i, 0, 0))],
        out_shape=[jax.ShapeDtypeStruct((t, B_WIDTH), _ST), jax.ShapeDtypeStruct((t, B_WIDTH), _F32),
                   jax.ShapeDtypeStruct((t // B_CHUNK, B_KEY_WIDTH, B_VAL_DIM), _F32)],
        scratch_shapes=[pltpu.VMEM((B_KEY_WIDTH, B_VAL_DIM), _F32)],
        compiler_params=_cparams(("parallel", "arbitrary")),
    )(qkb, vb, zb, alr, wup, b_alpha, gn)


def _gla_bwd(qkb, vb, zb, alr, oraw, dob, sst, wup, b_alpha, gn, nseq):
    t = qkb.shape[0]
    tb = min(GLA_BLOCK, t // nseq)
    tile = min(GLA_BWD_TILE, t // nseq // tb)
    nstep = t // nseq // (tile * tb)
    cpb = tb // B_CHUNK

    def body(qkb_ref, vb_ref, zb_ref, alr_ref, oraw_ref, dob_ref, sst_ref, wup_ref, b_ref, gn_ref,
             dqkb_ref, dvb_ref, dzb_ref, dalr_ref, dwup_ref, db_ref, dgn_ref, ds_ref):
        s_id, i = pl.program_id(0), pl.program_id(1)

        @pl.when((s_id == 0) & (i == 0))
        def _():
            dwup_ref[...] = jnp.zeros_like(dwup_ref)
            db_ref[...] = jnp.zeros_like(db_ref)
            dgn_ref[...] = jnp.zeros_like(dgn_ref)

        @pl.when(i == 0)
        def _():
            ds_ref[...] = jnp.zeros_like(ds_ref)

        for sb in reversed(range(tile)):
            one_block(sb, qkb_ref, vb_ref, zb_ref, alr_ref, oraw_ref, dob_ref, sst_ref, wup_ref, b_ref, gn_ref,
                      dqkb_ref, dvb_ref, dzb_ref, dalr_ref, dwup_ref, db_ref, dgn_ref, ds_ref)

    def one_block(sb, qkb_ref, vb_ref, zb_ref, alr_ref, oraw_ref, dob_ref, sst_ref, wup_ref, b_ref, gn_ref,
                  dqkb_ref, dvb_ref, dzb_ref, dalr_ref, dwup_ref, db_ref, dgn_ref, ds_ref):
        base = sb * tb
        rows = slice(base, base + tb)
        a_pre = _dot(alr_ref[rows, :], wup_ref[...]) + b_ref[...]
        la = _log_sigmoid(a_pre) * (1.0 / B_GATE_TEMP)

        z = zb_ref[rows, :].astype(_F32)
        sz = _sigmoid(z)
        d_ob = dob_ref[rows, :].astype(_F32)
        tg = d_ob * (z * sz)
        dsilu = sz * (1.0 + z * (1.0 - sz))
        do_cols, dgn_cols = [], []
        for h in range(B_HEADS):
            vl_ = slice(h * 128, (h + 1) * 128)
            oh = oraw_ref[rows, vl_].astype(_F32)
            r = lax.rsqrt(jnp.mean(oh * oh, axis=-1, keepdims=True) + NORM_EPS)
            on = oh * r
            gnh = gn_ref[:, vl_]
            dzb_ref[rows, vl_] = (d_ob[:, vl_] * (on * gnh) * dsilu[:, vl_]).astype(_ST)
            dgn_cols.append(jnp.sum(tg[:, vl_] * on, axis=0, keepdims=True))
            do_cols.append(_rms_bwd(tg[:, vl_] * gnh, on, r))
        dgn_ref[...] = dgn_ref[...] + jnp.concatenate(dgn_cols, axis=1)
        d_o = jnp.concatenate(do_cols, axis=1)

        ri = lax.broadcasted_iota(jnp.int32, (tb, tb), 0)
        ci = lax.broadcasted_iota(jnp.int32, (tb, tb), 1)
        same = (ri // B_CHUNK) == (ci // B_CHUNK)
        low = same & (ri >= ci)
        cum = _dot_ones(low.astype(_F32), la)
        at_row = lambda r: jnp.concatenate([jnp.broadcast_to(cum[c * B_CHUNK + r:c * B_CHUNK + r + 1], (B_CHUNK, B_KEY_WIDTH))
                                            for c in range(cpb)], axis=0)
        last, mid = at_row(B_CHUNK - 1), at_row(B_CHUNK // 2 - 1)
        e_q, e_k, e_l, e_c = jnp.exp(cum - mid), jnp.exp(mid - cum), jnp.exp(last - cum), jnp.exp(cum)
        q = qkb_ref[rows, 0:B_KEY_WIDTH] * (B_KEY_DIM ** -0.5)
        k = qkb_ref[rows, B_KEY_WIDTH:2 * B_KEY_WIDTH]
        qm, km, kl, qc = q * e_q, k * e_k, k * e_l, q * e_c
        lane_head = lax.broadcasted_iota(jnp.int32, (1, B_KEY_WIDTH), 1) // B_KEY_DIM
        d_o_mx = d_o.astype(_MX)

        def on_diagonal(st):
            z = jnp.zeros((B_KEY_DIM, B_VAL_DIM), st.dtype)
            return jnp.concatenate([jnp.concatenate(
                [st[h * B_KEY_DIM:(h + 1) * B_KEY_DIM] if g == h else z for g in range(B_HEADS)], axis=1)
                for h in range(B_HEADS)], axis=0)

        def diagonal_of(full):
            return jnp.concatenate([full[h * B_KEY_DIM:(h + 1) * B_KEY_DIM, h * B_VAL_DIM:(h + 1) * B_VAL_DIM]
                                    for h in range(B_HEADS)], axis=0)

        dqm, dkm, dv_cols = None, None, []
        for h in range(B_HEADS):
            vl_ = slice(h * B_VAL_DIM, (h + 1) * B_VAL_DIM)
            mine = lane_head == h
            qz, kz = jnp.where(mine, qm, 0.0).astype(_MX), jnp.where(mine, km, 0.0).astype(_MX)
            a = jnp.where(low, _dot_nt(qz, kz), 0.0).astype(_MX)
            da = jnp.where(low, _dot_nt(d_o_mx[:, vl_], vb_ref[rows, vl_]), 0.0).astype(_MX)
            dqm_h, dkm_h = _dot(da, kz), _dot_tn(da, qz)
            dqm = dqm_h if dqm is None else dqm + dqm_h
            dkm = dkm_h if dkm is None else dkm + dkm_h
            dv_cols.append(_dot_tn(a, d_o_mx[:, vl_]))
        dv = jnp.concatenate(dv_cols, axis=1)

        chunk = [slice(c * B_CHUNK, (c + 1) * B_CHUNK) for c in range(cpb)]
        dqc_rows, g_loc = [], []
        for c in range(cpb):
            dqc_rows.append(_dot_nt(d_o_mx[chunk[c]], on_diagonal(sst_ref[sb * cpb + c].astype(_MX))))
            g_loc.append(diagonal_of(_dot_tn(qc[chunk[c]], d_o_mx[chunk[c]])))
        cur = ds_ref[...]
        d_state = [None] * cpb
        for c in reversed(range(cpb)):
            d_state[c] = cur
            cur = g_loc[c] + jnp.exp(jnp.sum(la[chunk[c]].T, axis=1, keepdims=True)) * cur
        ds_ref[...] = cur
        dkl_rows, dv_rows, dlast_rows = [], [], []
        ones8 = jnp.ones((8, B_VAL_DIM), _F32)
        for c in range(cpb):
            dsd = on_diagonal(d_state[c].astype(_MX))
            dkl_c = _dot_nt(vb_ref[base + c * B_CHUNK:base + (c + 1) * B_CHUNK, :], dsd)
            dkl_rows.append(dkl_c)
            dv_rows.append(_dot(kl[chunk[c]], dsd))
            prod = d_state[c] * sst_ref[sb * cpb + c]
            p0 = prod.astype(jnp.bfloat16)
            p1 = (prod - p0.astype(_F32)).astype(jnp.bfloat16)
            ddec = (_dot_nt(ones8, p0) + _dot_nt(ones8, p1))[0:1]
            r_last = c * B_CHUNK + B_CHUNK - 1
            dlast = jnp.sum(dkl_c * kl[chunk[c]], axis=0, keepdims=True) + ddec * jnp.exp(last[r_last:r_last + 1])
            dlast_rows.append(jnp.broadcast_to(dlast, (B_CHUNK, B_KEY_WIDTH)))
        dqc, dkl = jnp.concatenate(dqc_rows, axis=0), jnp.concatenate(dkl_rows, axis=0)
        dqkb_ref[rows, 0:B_KEY_WIDTH] = ((dqm * e_q + dqc * e_c) * (B_KEY_DIM ** -0.5)).astype(_ST)
        dqkb_ref[rows, B_KEY_WIDTH:2 * B_KEY_WIDTH] = (dkm * e_k + dkl * e_l).astype(_ST)
        dvb_ref[rows, :] = (dv + jnp.concatenate(dv_rows, axis=0)).astype(_ST)
        dcum = dqm * qm - dkm * km + dqc * qc - dkl * kl
        row = lax.broadcasted_iota(jnp.int32, (tb, B_KEY_WIDTH), 0)
        dcum = jnp.where(row % B_CHUNK == B_CHUNK - 1, dcum + jnp.concatenate(dlast_rows, axis=0), dcum)
        dla = _dot_ones((same & (ri <= ci)).astype(_F32), dcum)

        da_pre = dla * (1.0 / B_GATE_TEMP) * (1.0 - _sigmoid(a_pre))
        dalr_ref[rows, :] = _dot_nt(da_pre, wup_ref[...]).astype(_ST)
        dwup_ref[...] = dwup_ref[...] + _dot_tn(alr_ref[rows, :], da_pre)
        db_ref[...] = db_ref[...] + jnp.sum(da_pre, axis=0, keepdims=True)

    blk = lambda s, i: s * nstep + nstep - 1 - i
    rows = lambda w: pl.BlockSpec((tile * tb, w), lambda s, i: (blk(s, i), 0))
    full = lambda a, b: pl.BlockSpec((a, b), lambda s, i: (0, 0))
    act = lambda w: jax.ShapeDtypeStruct((t, w), _ST)
    return pl.pallas_call(
        body, name="gla_bwd", grid=(nseq, nstep),
        in_specs=[rows(512), rows(512), rows(512), rows(LANE), rows(512), rows(512),
                  pl.BlockSpec((tile * cpb, B_KEY_WIDTH, B_VAL_DIM), lambda s, i: (blk(s, i), 0, 0)),
                  full(LANE, B_KEY_WIDTH), full(1, B_KEY_WIDTH), full(1, B_WIDTH)],
        out_specs=[rows(512), rows(512), rows(512), rows(LANE), full(LANE, B_KEY_WIDTH),
                   full(1, B_KEY_WIDTH), full(1, B_WIDTH)],
        out_shape=[act(512), act(512), act(512), act(LANE),
                   jax.ShapeDtypeStruct((LANE, B_KEY_WIDTH), _F32),
                   jax.ShapeDtypeStruct((1, B_KEY_WIDTH), _F32), jax.ShapeDtypeStruct((1, B_WIDTH), _F32)],
        scratch_shapes=[pltpu.VMEM((B_KEY_WIDTH, B_VAL_DIM), _F32)],
        compiler_params=_cparams(("arbitrary", "arbitrary")),
    )(qkb, vb, zb, alr, oraw, dob, sst, wup, b_alpha, gn)


def _merge_loss(oa, ob, ga, gb, x2, tgt, wa, wb, wo, g_final):
    t = x2.shape[0]
    tm = min(t, 512)
    nt = t // tm

    def body(oa_ref, ob_ref, ga_ref, gb_ref, x_ref, t_ref, wa_ref, wb_ref, wo_ref, gf_ref,
             dh_ref, doa_ref, dob_ref, dga_ref, dgb_ref, dwa_ref, dwb_ref, dwo_ref, dgf_ref, loss_ref,
             ya_s, yb_s, out_s, dmer_s, mrg_s, dya_s, dyb_s):
        first = pl.program_id(0) == 0
        so_far = lambda ref: jnp.where(first, 0.0, ref[...])

        slabs = [slice(s, s + MERGE_SLAB) for s in range(0, tm, MERGE_SLAB)]
        fold = lambda a: a[0:8] + a[8:16]
        ya_s[...] = _dot(oa_ref[...], wa_ref[...])
        yb_s[...] = _dot(ob_ref[...], wb_ref[...])
        for rows_ in slabs:
            sga, sgb = _sigmoid(ga_ref[rows_, :].astype(_F32)), _sigmoid(gb_ref[rows_, :].astype(_F32))
            mrg_s[rows_, :] = (sga * ya_s[rows_, :] + sgb * yb_s[rows_, :]).astype(_MX)
        out_s[...] = x_ref[...] + _dot(mrg_s[...], wo_ref[...])
        gf = gf_ref[...]
        loss8 = jnp.zeros((8, D_MODEL), _F32)
        dgf8 = jnp.zeros((8, D_MODEL), _F32)
        for rows_ in slabs:
            out = out_s[rows_, :]
            r = lax.rsqrt(jnp.mean(out * out, axis=-1, keepdims=True) + NORM_EPS)
            nrm = out * r
            err = nrm * gf - t_ref[rows_, :]
            loss8 = loss8 + fold(err * err)
            dy = err * (1.0 / D_MODEL)
            dgf8 = dgf8 + fold(dy * nrm)
            dh = _rms_bwd(dy * gf, nrm, r)
            dh_ref[rows_, :] = dh.astype(_ST)
        loss_ref[...] = so_far(loss_ref) + (0.5 / D_MODEL) * jnp.sum(loss8, axis=0, keepdims=True)
        dgf_ref[...] = so_far(dgf_ref) + jnp.sum(dgf8, axis=0, keepdims=True)
        dmer_s[...] = _dot_nt(dh_ref[...], wo_ref[...])
        dwo_ref[...] = so_far(dwo_ref) + _dot_tn(mrg_s[...], dh_ref[...])
        for rows_ in slabs:
            sga, sgb = _sigmoid(ga_ref[rows_, :].astype(_F32)), _sigmoid(gb_ref[rows_, :].astype(_F32))
            dmer = dmer_s[rows_, :]
            da, db = dmer * sga, dmer * sgb
            dya_s[rows_, :] = da.astype(_MX)
            dyb_s[rows_, :] = db.astype(_MX)
            dga_ref[rows_, :] = (da * ya_s[rows_, :] * (1.0 - sga)).astype(_ST)
            dgb_ref[rows_, :] = (db * yb_s[rows_, :] * (1.0 - sgb)).astype(_ST)
        doa_ref[...] = _dot_nt(dya_s[...], wa_ref[...]).astype(_ST)
        dob_ref[...] = _dot_nt(dyb_s[...], wb_ref[...]).astype(_ST)
        dwa_ref[...] = so_far(dwa_ref) + _dot_tn(oa_ref[...], dya_s[...])
        dwb_ref[...] = so_far(dwb_ref) + _dot_tn(ob_ref[...], dyb_s[...])

    rows = lambda w: pl.BlockSpec((tm, w), lambda i: (i, 0))
    full = lambda a, b: pl.BlockSpec((a, b), lambda i: (0, 0), pipeline_mode=pl.Buffered(1))
    return pl.pallas_call(
        body, name="merge_loss", grid=(nt,),
        in_specs=[rows(512), rows(512), rows(D_MODEL), rows(D_MODEL), rows(D_MODEL), rows(D_MODEL),
                  full(A_WIDTH, D_MODEL), full(B_WIDTH, D_MODEL), full(D_MODEL, D_MODEL), full(1, D_MODEL)],
        out_specs=[rows(D_MODEL), rows(512), rows(512), rows(D_MODEL), rows(D_MODEL),
                   full(A_WIDTH, D_MODEL), full(B_WIDTH, D_MODEL), full(D_MODEL, D_MODEL),
                   full(1, D_MODEL), full(1, D_MODEL)],
        out_shape=[jax.ShapeDtypeStruct((t, D_MODEL), _ST), jax.ShapeDtypeStruct((t, 512), _ST),
                   jax.ShapeDtypeStruct((t, 512), _ST), jax.ShapeDtypeStruct((t, D_MODEL), _ST),
                   jax.ShapeDtypeStruct((t, D_MODEL), _ST),
                   jax.ShapeDtypeStruct((A_WIDTH, D_MODEL), _F32), jax.ShapeDtypeStruct((B_WIDTH, D_MODEL), _F32),
                   jax.ShapeDtypeStruct((D_MODEL, D_MODEL), _F32), jax.ShapeDtypeStruct((1, D_MODEL), _F32),
                   jax.ShapeDtypeStruct((1, D_MODEL), _F32)],
        scratch_shapes=[pltpu.VMEM((tm, D_MODEL), _F32)] * 4 + [pltpu.VMEM((tm, D_MODEL), _MX)] * 3,
        compiler_params=_cparams(("arbitrary",), VMEM_LIMIT),
    )(oa, ob, ga, gb, x2, tgt, wa, wb, wo, g_final)


def _in_proj_bwd_x(dpieces, wt, x2, dh2, g_in):
    t = x2.shape[0]
    tm = min(t, 512)
    np_ = len(PIECES)

    def body(*refs):
        dp_refs = refs[:np_]
        w_ref, x_ref, dh2_ref, g_ref, gx_ref, dg_ref = refs[np_:]

        @pl.when(pl.program_id(0) == 0)
        def _():
            dg_ref[...] = jnp.zeros_like(dg_ref)

        dh = None
        for (name, a, b), dp in zip(PIECES, dp_refs):
            part = _dot(dp[...], w_ref[a:b, :])
            dh = part if dh is None else dh + part
        xv = x_ref[...]
        r = lax.rsqrt(jnp.mean(xv * xv, axis=-1, keepdims=True) + NORM_EPS)
        nrm = xv * r
        dg_ref[...] = dg_ref[...] + jnp.sum(dh * nrm, axis=0, keepdims=True)
        gx_ref[...] = dh2_ref[...].astype(_F32) + _rms_bwd(dh * g_ref[...], nrm, r)

    rows = lambda w: pl.BlockSpec((tm, w), lambda i: (i, 0))
    full = lambda a, b: pl.BlockSpec((a, b), lambda i: (0, 0), pipeline_mode=pl.Buffered(1))
    return pl.pallas_call(
        body, name="in_proj_bwd_x", grid=(t // tm,),
        in_specs=[rows(b - a) for _, a, b in PIECES] + [full(D_IN, D_MODEL), rows(D_MODEL), rows(D_MODEL),
                                                          full(1, D_MODEL)],
        out_specs=[rows(D_MODEL), full(1, D_MODEL)],
        out_shape=[jax.ShapeDtypeStruct((t, D_MODEL), _F32), jax.ShapeDtypeStruct((1, D_MODEL), _F32)],
        compiler_params=_cparams(("arbitrary",), VMEM_LIMIT),
    )(*dpieces, wt, x2, dh2, g_in)


def _in_proj_bwd_w(h, dpieces, others, osplits):
    t = h.shape[0]
    tm = min(t, 1024)
    nt = t // tm
    np_, no = len(PIECES), len(others)
    half = D_MODEL // 2

    def body(*refs):
        h_ref, dp_refs, o_refs = refs[0], refs[1:1 + np_], refs[1 + np_:1 + np_ + no]
        mine_ref, theirs_ref = refs[1 + np_ + no:3 + np_ + no]
        r_refs = refs[3 + np_ + no:3 + np_ + 2 * no]
        acc_ref, stage_ref, keep_sem, send, recv, o_send, o_recv = refs[3 + np_ + 2 * no:]
        i = pl.program_id(0)
        x, y, c = _place()
        sibling = (x, y, 1 - c)
        early = [pltpu.make_async_remote_copy(
            src_ref=osplits[k].half(o_refs[k], 1 - c), dst_ref=r_refs[k], send_sem=o_send.at[k], recv_sem=o_recv.at[k],
            device_id=sibling, device_id_type=_MESH) for k in range(no)]

        @pl.when(i == 0)
        def _():
            for cp in early:
                cp.start()

        hv = h_ref[...]
        cols = lambda core: pl.ds(pl.multiple_of(core * half, LANE), half)
        writes = []
        by_size = sorted(range(np_), key=lambda j: PIECES[j][1] - PIECES[j][2])
        for j, ((name, a, b), dp) in [(j, (PIECES[j], dp_refs[j])) for j in by_size]:
            part = _dot_tn(dp[...], hv)
            if name == "alr":
                b = a + B_GATE_RANK
                part = part[0:B_GATE_RANK]
            acc_ref[a:b, :] = jnp.where(i == 0, 0.0, acc_ref[a:b, :]) + part
            keep = pltpu.make_async_copy(acc_ref.at[a:b, cols(c)], mine_ref.at[a:b], keep_sem.at[j])
            give = pltpu.make_async_remote_copy(
                src_ref=stage_ref.at[a:b], dst_ref=theirs_ref.at[a:b], send_sem=send.at[j],
                recv_sem=recv.at[j], device_id=sibling, device_id_type=_MESH)
            writes += [keep, give]

            @pl.when(i == nt - 1)
            def _(keep=keep, give=give, a=a, b=b):
                keep.start()
                stage_ref[a:b, :] = jnp.where(c == 0, acc_ref[a:b, half:], acc_ref[a:b, :half]).astype(_MX)
                give.start()

        @pl.when(i == nt - 1)
        def _():
            for cp in writes + early:
                cp.wait()

    rows = lambda w: pl.BlockSpec((tm, w), lambda i: (i, 0))
    halves = [jax.ShapeDtypeStruct((D_IN, half), _F32), jax.ShapeDtypeStruct((D_IN, half), _MX)]
    return pl.pallas_call(
        body, name="in_proj_bwd_w", grid=(nt,),
        in_specs=[rows(D_MODEL)] + [rows(b - a) for _, a, b in PIECES] + [_ANY] * no,
        out_specs=[_ANY] * (2 + no),
        out_shape=halves + [jax.ShapeDtypeStruct(sp.half_shape(g.shape), g.dtype) for g, sp in zip(others, osplits)],
        scratch_shapes=[pltpu.VMEM((D_IN, D_MODEL), _F32), pltpu.VMEM((D_IN, half), _MX),
                        pltpu.SemaphoreType.DMA((np_,)), pltpu.SemaphoreType.DMA((np_,)), pltpu.SemaphoreType.DMA((np_,)),
                        pltpu.SemaphoreType.DMA((no,)), pltpu.SemaphoreType.DMA((no,))],
        compiler_params=_cparams(("arbitrary",), VMEM_LIMIT),
    )(h, *dpieces, *others)


def _place():
    return lax.axis_index("x"), lax.axis_index("y"), lax.axis_index("c")


def _other_chips(x, y):
    return [(1 - x, y), (x, 1 - y), (1 - x, 1 - y)]


class _Split(NamedTuple):
    by_rows: bool
    step: int
    size: int

    def half(self, ref, c):
        r, n = ref.shape[-2:]
        if self.by_rows:
            return ref.at[:, pl.ds(pl.multiple_of(c * (n // 2), LANE), n // 2)]
        return ref.at[pl.ds(pl.multiple_of(c * (r // 2), 16), r // 2), :]

    def chip_part(self, ref, k):
        if self.by_rows:
            return ref.at[pl.ds(pl.multiple_of(k * self.step, 16), self.size), :]
        return ref.at[:, pl.ds(pl.multiple_of(k * self.size, LANE), self.size)]

    def half_shape(self, shape):
        r, n = shape
        return (r, n // 2) if self.by_rows else (r // 2, n)

    def part_shape(self, shape):
        r, n = shape
        return (self.size, n) if self.by_rows else (r, self.size)


SPLIT_W_IN_T = _Split(True, WINDOW_STEP, WINDOW_ROWS)
SPLIT_W_O = _Split(True, 256, 256)
SPLIT_W_OUT = _Split(False, 256, 256)


def _gather_weights(shards, splits, fulls, pos_f):
    nw = len(shards)
    t = pos_f.shape[0]

    def body(*refs):
        ins, (pos_ref, c_ref) = refs[:nw], refs[nw:nw + 2]
        outs, tables = refs[nw + 2:2 * nw + 2], refs[2 * nw + 2:2 * nw + 5]
        send_a, recv_a, send_b, recv_b = refs[2 * nw + 5:]
        x, y, c = _place()
        me = 2 * x + y
        peers = _other_chips(x, y)

        def place(i, k, half):
            if splits[i] is None:
                return outs[i].at[k]
            if fulls[i][0] == 4 and len(fulls[i]) == 3:
                whole = outs[i].at[k]
            else:
                whole = splits[i].chip_part(outs[i], k)
            return splits[i].half(whole, half)

        first, passed = [], []
        for i in range(nw):
            src = ins[i] if splits[i] is None else splits[i].half(ins[i], c)
            for j, (px, py) in enumerate(peers):
                cp = pltpu.make_async_remote_copy(
                    src_ref=src, dst_ref=place(i, me, c), send_sem=send_a.at[3 * i + j],
                    recv_sem=recv_a.at[3 * i + j], device_id=(px, py, c), device_id_type=_MESH)
                cp.start()
                first.append(cp)
        _rope_tables_into(pos_ref, c_ref, *tables)
        for i in range(nw):
            for j, (px, py) in enumerate(peers):
                landed = place(i, 2 * px + py, c)
                pltpu.make_async_remote_copy(
                    src_ref=landed, dst_ref=landed, send_sem=send_a.at[3 * i + j], recv_sem=recv_a.at[3 * i + j],
                    device_id=(px, py, c), device_id_type=_MESH).wait_recv()
                if splits[i] is not None:
                    cp = pltpu.make_async_remote_copy(
                        src_ref=landed, dst_ref=landed, send_sem=send_b.at[3 * i + j], recv_sem=recv_b.at[3 * i + j],
                        device_id=(x, y, 1 - c), device_id_type=_MESH)
                    cp.start()
                    passed.append(cp)
        for i in range(nw):
            if splits[i] is None:
                continue
            for j, (px, py) in enumerate(peers):
                theirs = place(i, 2 * px + py, 1 - c)
                pltpu.make_async_remote_copy(
                    src_ref=theirs, dst_ref=theirs, send_sem=send_b.at[3 * i + j], recv_sem=recv_b.at[3 * i + j],
                    device_id=(x, y, 1 - c), device_id_type=_MESH).wait_recv()
        for cp in first + passed:
            cp.wait_send()

    vm = pl.BlockSpec(memory_space=pltpu.VMEM)
    tab = jax.ShapeDtypeStruct((t, LANE), _F32)
    return pl.pallas_call(
        body, name="gather_weights",
        in_specs=[_ANY] * nw + [vm, vm], out_specs=[_ANY] * nw + [vm] * 3,
        out_shape=[jax.ShapeDtypeStruct(f, s.dtype) for f, s in zip(fulls, shards)] + [tab] * 3,
        scratch_shapes=[pltpu.SemaphoreType.DMA((3 * nw,)) for _ in range(4)],
        compiler_params=_cparams(None, VMEM_LIMIT),
    )(*shards, pos_f, _rope_consts())


def _assemble_w_in_t(slots):
    bw = 256
    ov = WINDOW_ROWS - WINDOW_STEP

    def body(s_ref, o_ref):
        for k in range(4):
            base = k * WINDOW_STEP
            lo = 0 if k == 0 else ov
            if k > 0:
                o_ref[base:base + ov, :] = s_ref[k - 1, WINDOW_STEP:WINDOW_ROWS, :] + s_ref[k, 0:ov, :]
            hi = WINDOW_ROWS if k == 3 else WINDOW_STEP
            o_ref[base + lo:base + hi, :] = s_ref[k, lo:hi, :]

    return pl.pallas_call(
        body, name="assemble_w_in_t", grid=(D_MODEL // bw,),
        in_specs=[pl.BlockSpec((4, WINDOW_ROWS, bw), lambda i: (0, 0, i))],
        out_specs=pl.BlockSpec((D_IN, bw), lambda i: (0, i)),
        out_shape=jax.ShapeDtypeStruct((D_IN, D_MODEL), slots.dtype),
        compiler_params=_cparams(("parallel",)),
    )(slots)


def _row_block(rows):
    for cand in (976, 176, 256, 128):
        if rows % cand == 0:
            return cand
    return rows


def _pair_sum(g, r, split, c_arr, name):
    hr, hn = r.shape
    br = _row_block(hr)
    if split is None:
        g_spec = pl.BlockSpec((br, hn), lambda i, c_ref: (i, 0))
    elif split.by_rows:
        g_spec = pl.BlockSpec((br, hn), lambda i, c_ref: (i, c_ref[0]))
    else:
        g_spec = pl.BlockSpec((br, hn), lambda i, c_ref: (c_ref[0] * (hr // br) + i, 0))

    def body(c_ref, g_ref, r_ref, o_ref):
        o_ref[...] = (g_ref[...] + r_ref[...]).astype(o_ref.dtype)

    return pl.pallas_call(
        body, name=name,
        grid_spec=pltpu.PrefetchScalarGridSpec(
            num_scalar_prefetch=1, grid=(hr // br,),
            in_specs=[g_spec, pl.BlockSpec((br, hn), lambda i, c_ref: (i, 0))],
            out_specs=pl.BlockSpec((br, hn), lambda i, c_ref: (i, 0))),
        out_shape=jax.ShapeDtypeStruct(r.shape, _MX),
        compiler_params=_cparams(("parallel",)),
    )(c_arr, g, r)


_HBM = pl.BlockSpec(memory_space=pltpu.HBM)
_SEM = pl.BlockSpec(memory_space=pltpu.SEMAPHORE)
_FLOWS = pltpu.SideEffectType.DATAFLOW_SIDE_EFFECTING


def _chip_exchange_copies(refs, send, recv, splits):
    nw = len(refs) // 2
    x, y, c = _place()
    me = 2 * x + y
    copies = []
    for i in range(nw):
        for px, py in _other_chips(x, y):
            copies.append((splits[i].chip_part(refs[i], 2 * px + py), refs[nw + i].at[me], (px, py, c)))
    return [pltpu.make_async_remote_copy(src_ref=src, dst_ref=dst, send_sem=send.at[k], recv_sem=recv.at[k],
                                         device_id=peer, device_id_type=_MESH)
            for k, (src, dst, peer) in enumerate(copies)]


def _late_gather_copies(refs, send, recv, splits):
    nw = len(refs) // 2
    x, y, c = _place()
    me = 2 * x + y
    copies = []
    for i in range(nw):
        for px, py in [(x, y)] + _other_chips(x, y):
            copies.append((refs[i], splits[i].chip_part(refs[nw + i], me), (px, py, c)))
    return [pltpu.make_async_remote_copy(src_ref=src, dst_ref=dst, send_sem=send.at[k], recv_sem=recv.at[k],
                                         device_id=peer, device_id_type=_MESH)
            for k, (src, dst, peer) in enumerate(copies)]


def _start_copies(name, flying, copies_of, n_copies, after=None):
    first = [] if after is None else [after]

    def body(*refs):
        ins = refs[:len(flying)]
        send, recv = refs[len(flying) + len(first):len(flying) + len(first) + 2]
        token = refs[-1]
        for cp in copies_of(ins, send, recv):
            cp.start()
        token[...] = jnp.zeros_like(token)

    outs = pl.pallas_call(
        body, name=name,
        in_specs=[_HBM] * len(flying) + [_ANY] * len(first),
        out_specs=[_SEM, _SEM] + [_HBM] * len(flying) + [pl.BlockSpec(memory_space=pltpu.VMEM)],
        out_shape=[pltpu.SemaphoreType.DMA((n_copies,)), pltpu.SemaphoreType.DMA((n_copies,))]
        + [pltpu.HBM(f.shape, f.dtype) for f in flying] + [jax.ShapeDtypeStruct((8, LANE), _F32)],
        input_output_aliases={i: 2 + i for i in range(len(flying))},
        compiler_params=pltpu.CompilerParams(has_side_effects=_FLOWS),
    )(*[pltpu.with_memory_space_constraint(f, pltpu.HBM) for f in flying], *first)
    return outs[0], outs[1], outs[2:2 + len(flying)], outs[-1]


def _wait_copies(name, send, recv, flying, copies_of, after):
    def body(*refs):
        ins = refs[:len(flying)]
        send_ref, recv_ref = refs[len(flying):len(flying) + 2]
        for cp in copies_of(ins, send_ref, recv_ref):
            cp.wait_send()
            cp.wait_recv()

    return pl.pallas_call(
        body, name=name,
        in_specs=[_HBM] * len(flying) + [_SEM, _SEM, _ANY],
        out_specs=[_HBM] * len(flying),
        out_shape=[pltpu.HBM(f.shape, f.dtype) for f in flying],
        input_output_aliases={i: i for i in range(len(flying))},
        compiler_params=pltpu.CompilerParams(has_side_effects=_FLOWS),
    )(*flying, send, recv, after)


def _sum_chips(q, p, split, place_arr, name):
    _, hr, hn = q.shape
    if split.by_rows:
        out_shape = (hr, 2 * hn)
        o_spec = pl.BlockSpec((hr, hn), lambda i, pr: (0, pr[0]))
        p_spec = pl.BlockSpec((pl.Element(hr), pl.Element(hn)), lambda i, pr: (pr[1] * split.step, 0))
    else:
        out_shape = (2 * hr, hn)
        o_spec = pl.BlockSpec((hr, hn), lambda i, pr: (pr[0], 0))
        p_spec = pl.BlockSpec((hr, hn), lambda i, pr: (0, pr[1]))

    def body(pr, q_ref, p_ref, o_ref):
        f = lambda k: jnp.where(pr[1] == k, p_ref[...], q_ref[k]).astype(_F32)
        o_ref[...] = ((f(0) + f(1)) + f(2)) + f(3)

    return pl.pallas_call(
        body, name=name,
        grid_spec=pltpu.PrefetchScalarGridSpec(
            num_scalar_prefetch=1, grid=(1,),
            in_specs=[pl.BlockSpec((4, hr, hn), lambda i, pr: (0, 0, 0)), p_spec], out_specs=o_spec),
        out_shape=jax.ShapeDtypeStruct(out_shape, _F32),
        compiler_params=_cparams(("arbitrary",), VMEM_LIMIT),
    )(place_arr, q, p)


def _pair_share(bufs, splits, small):
    nw = len(bufs)

    def body(*refs):
        ins, small_ref, outs, all_ref = refs[:nw], refs[nw], refs[nw + 1:2 * nw + 1], refs[2 * nw + 1]
        send, recv, s_send, s_recv = refs[2 * nw + 2:]
        x, y, c = _place()
        copies = []
        for r in range(1, 8):
            peer = (1 - x if r & 4 else x, 1 - y if r & 2 else y, 1 - c if r & 1 else c)
            cp = pltpu.make_async_remote_copy(
                src_ref=small_ref, dst_ref=all_ref.at[4 * x + 2 * y + c], send_sem=s_send.at[r - 1],
                recv_sem=s_recv.at[r - 1], device_id=peer, device_id_type=_MESH)
            cp.start()
            copies.append(cp)
        for i in range(nw):
            cp = pltpu.make_async_remote_copy(
                src_ref=splits[i].half(ins[i], c), dst_ref=splits[i].half(outs[i], c), send_sem=send.at[i],
                recv_sem=recv.at[i], device_id=(x, y, 1 - c), device_id_type=_MESH)
            cp.start()
            copies.append(cp)
        for cp in copies:
            cp.wait()

    return pl.pallas_call(
        body, name="grad_pair_share",
        in_specs=[_ANY] * (nw + 1), out_specs=[_ANY] * (nw + 1),
        out_shape=[jax.ShapeDtypeStruct(b.shape, b.dtype) for b in bufs]
        + [jax.ShapeDtypeStruct((8,) + small.shape, small.dtype)],
        input_output_aliases={i: i for i in range(nw)},
        scratch_shapes=[pltpu.SemaphoreType.DMA((nw,)), pltpu.SemaphoreType.DMA((nw,)),
                        pltpu.SemaphoreType.DMA((7,)), pltpu.SemaphoreType.DMA((7,))],
    )(*bufs, small)


def _sum_devices(parts, own, dev_arr):
    def body(dev, p_ref, own_ref, tot_ref):
        f = lambda d: jnp.where(dev[0] == d, own_ref[...], p_ref[d])
        acc = f(0)
        for d in range(1, 8):
            acc = acc + f(d)
        tot_ref[...] = acc

    return pl.pallas_call(
        body, name="small_sum",
        grid_spec=pltpu.PrefetchScalarGridSpec(
            num_scalar_prefetch=1, grid=(1,),
            in_specs=[pl.BlockSpec(parts.shape, lambda i, dev: (0, 0, 0)), pl.BlockSpec(own.shape, lambda i, dev: (0, 0))],
            out_specs=pl.BlockSpec(own.shape, lambda i, dev: (0, 0))),
        out_shape=jax.ShapeDtypeStruct(own.shape, own.dtype),
    )(dev_arr, parts, own)


def _adam_update(w, g, m, v):
    m2 = ADAM_B1 * m + (1.0 - ADAM_B1) * g
    v2 = ADAM_B2 * v + (1.0 - ADAM_B2) * (g * g)
    m_hat = m2 / (1.0 - ADAM_B1 ** ADAM_STEP)
    v_hat = v2 / (1.0 - ADAM_B2 ** ADAM_STEP)
    return -ADAM_LR * (m_hat / (jnp.sqrt(v_hat) + ADAM_EPS) + ADAM_WD * w), m2, v2


SMALL_AT = dict(g_in=(0, 0), g_final=(1, 0), g_gla_norm=(2, 0), b_alpha=(2, B_WIDTH), attn_sinks=(2, B_WIDTH + B_KEY_WIDTH))
LOSS_AT = (2, B_WIDTH + B_KEY_WIDTH + LANE)
WUP_ROWS = (3, 7)


def _adamw_small(tot, g_wup, params):
    names = list(params)

    def body(*refs):
        tot_ref, gw_ref = refs[0], refs[1]
        ins = refs[2:2 + 3 * len(names)]
        outs = refs[2 + 3 * len(names):]
        for i, nm in enumerate(names):
            w_ref, m_ref, v_ref = ins[3 * i:3 * i + 3]
            if nm in SMALL_AT:
                r, a = SMALL_AT[nm]
                g = tot_ref[r:r + 1, a:a + w_ref.shape[1]]
            else:
                g = gw_ref[...]
            d, m2, v2 = _adam_update(w_ref[...], g, m_ref[...], v_ref[...])
            for o_ref, val in zip(outs[4 * i:4 * i + 4], (g, d, m2, v2)):
                o_ref[...] = val

    vm = pl.BlockSpec(memory_space=pltpu.VMEM)
    flat = [a for nm in names for a in params[nm]]
    out_shape = [jax.ShapeDtypeStruct(params[nm][0].shape, _F32) for nm in names for _ in range(4)]
    outs = pl.pallas_call(
        body, name="adamw_small", in_specs=[vm] * (2 + len(flat)), out_specs=[vm] * len(out_shape), out_shape=out_shape,
    )(tot, g_wup, *flat)
    return {nm: tuple(outs[4 * i:4 * i + 4]) for i, nm in enumerate(names)}


def _adamw(w, g, m, v, name):
    lead = w.shape[0] != 1
    r, n = (w.shape[0], w.shape[2]) if lead else w.shape[1:]
    br = r
    for cand in (256, 244, 128):
        if r > cand and r % cand == 0:
            br = cand
            break

    def body(w_ref, g_ref, m_ref, v_ref, d_ref, nm_ref, nv_ref):
        d_ref[...], nm_ref[...], nv_ref[...] = _adam_update(w_ref[...], g_ref[...], m_ref[...], v_ref[...])

    blk = pl.BlockSpec((br, 1, n), lambda i: (i, 0, 0)) if lead else pl.BlockSpec((None, br, n), lambda i: (0, i, 0))
    shp = jax.ShapeDtypeStruct(w.shape, _F32)
    return pl.pallas_call(
        body, name=name, grid=(r // br,),
        in_specs=[blk] * 4, out_specs=[blk] * 3, out_shape=[shp] * 3,
        compiler_params=_cparams(("parallel",)),
    )(w, g, m, v)


def kernel(x, positions, g_in, w_in, w_alpha_up, b_alpha, attn_sinks, g_gla_norm, w_out_a, w_out_b, w_o, g_final, loss_target, m_g_in, m_w_in, m_w_alpha_up, m_b_alpha, m_attn_sinks, m_g_gla_norm, m_w_out_a, m_w_out_b, m_w_o, m_g_final, v_g_in, v_w_in, v_w_alpha_up, v_b_alpha, v_attn_sinks, v_g_gla_norm, v_w_out_a, v_w_out_b, v_w_o, v_g_final):
    nseq, seq, _ = x.shape
    t = nseq * seq
    cx, cy, cc = _place()
    chip = 2 * cx + cy
    c_arr = jnp.reshape(cc, (1,)).astype(jnp.int32)

    tr = lambda w: jnp.transpose(w, (2, 0, 1))
    w_in_t = tr(w_in).reshape(SHARD, D_MODEL).astype(_MX)
    pad = WINDOW_ROWS - SHARD
    window = lax.switch(chip, [lambda w, k=k: jnp.pad(w, ((4 * k, pad - 4 * k), (0, 0))) for k in range(4)], w_in_t)
    shards = [window, w_alpha_up[0].astype(_MX)]
    late = [w_out_a[0].astype(_MX), w_out_b[0].astype(_MX), w_o[0].astype(_MX)]
    late_splits = [SPLIT_W_OUT, SPLIT_W_OUT, SPLIT_W_O]
    splits = [SPLIT_W_IN_T, None]
    fulls = [(4, WINDOW_ROWS, D_MODEL), (4, B_GATE_RANK, B_KEY_WIDTH // 4)]
    pos_f = positions.astype(_F32).reshape(t, 1)
    win_g, wup_g, cos, sa, sb = _gather_weights(shards, splits, fulls, pos_f)
    late_copies = lambda refs, send, recv: _late_gather_copies(refs, send, recv, late_splits)
    late_full = [lax.empty(shape, _MX) for shape in ((A_WIDTH, D_MODEL), (B_WIDTH, D_MODEL), (D_MODEL, D_MODEL))]
    l_send, l_recv, l_flying, l_token = _start_copies("late_gather_start", late + late_full, late_copies,
                                                      4 * len(late), after=win_g)
    win_g = lax.dynamic_update_slice(win_g, window[None], (chip, 0, 0))
    wup_g = lax.dynamic_update_slice(wup_g, shards[1][None], (chip, 0, 0))
    wt = _assemble_w_in_t(win_g)
    wup = jnp.concatenate([jnp.transpose(wup_g, (1, 0, 2)).reshape(B_GATE_RANK, B_KEY_WIDTH),
                           jnp.zeros((LANE - B_GATE_RANK, B_KEY_WIDTH), _MX)], axis=0)

    x2 = x.reshape(t, D_MODEL)
    tgt = loss_target.reshape(t, D_MODEL)
    sinks = attn_sinks.reshape(A_HEADS)
    gf = g_final.reshape(1, D_MODEL)

    h, qkv, za, qkb, vb, zb, alr, ga, gb = _in_proj(x2, g_in + l_token[0, 0], wt, cos, sa, sb)
    oa = _attn_fwd(qkv, za, sinks, nseq)
    ob, oraw, sst = _gla_fwd(qkb, vb, zb, alr, wup, b_alpha, g_gla_norm, nseq)

    wa, wb, wo = _wait_copies("late_gather_wait", l_send, l_recv, l_flying, late_copies, ob)[len(late):]
    dh2, doa, dob, dga, dgb, dwa, dwb, dwo, dgf, lossv = _merge_loss(oa, ob, ga, gb, x2, tgt, wa, wb, wo, gf)

    dqkv, dza, dsink = _attn_bwd(qkv, za, doa, sinks, cos, sa, sb, nseq)
    dqkb, dvb, dzb, dalr, dwup, dba, dgn = _gla_bwd(qkb, vb, zb, alr, oraw, dob, sst, wup, b_alpha, g_gla_norm, nseq)
    dpieces = [dqkv, dza, dqkb, dvb, dzb, dalr, dga, dgb]
    gsplits = [SPLIT_W_IN_T, SPLIT_W_OUT, SPLIT_W_OUT, SPLIT_W_O]
    names = ("w_in", "w_out_a", "w_out_b", "w_o")
    dwin_mine, *from_sibling = _in_proj_bwd_w(h, dpieces, [dwa, dwb, dwo], gsplits[1:])
    pair_sums = [_pair_sum(g, r, sp, c_arr, "pair_sum_" + nm)
                 for g, r, sp, nm in zip([dwin_mine, dwa, dwb, dwo], from_sibling, [None] + gsplits[1:], names)]
    exchange = lambda refs, send, recv: _chip_exchange_copies(refs, send, recv, gsplits)
    lands = [lax.empty((4,) + sp.part_shape(p.shape), p.dtype) for p, sp in zip(pair_sums, gsplits)]
    send, recv, flying, token = _start_copies("grad_chip_exchange_start", pair_sums + lands, exchange, 3 * len(lands))
    grad_x2, dgin = _in_proj_bwd_x(dpieces, wt, x2, dh2, g_in + token[0, 0])
    landed = _wait_copies("grad_chip_exchange_wait", send, recv, flying, exchange, grad_x2)
    place_arr = jnp.stack([cc, chip]).astype(jnp.int32)
    reduced = [_sum_chips(q, p, sp, place_arr, "chip_sum_" + nm)
               for q, p, sp, nm in zip(landed[len(lands):], landed[:len(lands)], gsplits, names)]
    row2 = jnp.concatenate([dgn, dba, jnp.pad(dsink[:, 0].reshape(1, A_HEADS), ((0, 0), (0, LANE - A_HEADS))),
                            jnp.pad(jnp.sum(lossv, axis=1, keepdims=True), ((0, 0), (0, LANE - 1)))], axis=1)
    small = jnp.concatenate([dgin, dgf, row2, dwup[:B_GATE_RANK].reshape(WUP_ROWS[1] - WUP_ROWS[0], D_MODEL),
                             jnp.zeros((1, D_MODEL), _F32)], axis=0)
    g_window, g_wa, g_wb, g_wo, small_parts = _pair_share(reduced, gsplits, small)
    g_win_t = lax.switch(chip, [lambda w, k=k: w[4 * k:4 * k + SHARD].reshape(SHARD, 1, D_MODEL) for k in range(4)],
                         g_window)
    dev_arr = jnp.reshape(2 * chip + cc, (1,)).astype(jnp.int32)
    tot = _sum_devices(small_parts, small, dev_arr)
    loss = tot[LOSS_AT]
    nup = B_KEY_WIDTH // 4
    g_wup = lax.dynamic_slice(tot[WUP_ROWS[0]:WUP_ROWS[1]].reshape(B_GATE_RANK, B_KEY_WIDTH), (0, chip * nup),
                              (B_GATE_RANK, nup))

    row = lambda a: a.reshape(1, -1)
    sm = _adamw_small(tot, g_wup, dict(
        g_in=(g_in, m_g_in, v_g_in), g_final=(row(g_final), row(m_g_final), row(v_g_final)),
        g_gla_norm=(g_gla_norm, m_g_gla_norm, v_g_gla_norm), b_alpha=(b_alpha, m_b_alpha, v_b_alpha),
        attn_sinks=(attn_sinks, m_attn_sinks, v_attn_sinks),
        w_alpha_up=(w_alpha_up[0], m_w_alpha_up[0], v_w_alpha_up[0])))
    sm["g_final"] = tuple(a.reshape(D_MODEL) for a in sm["g_final"])
    sm["w_alpha_up"] = tuple(a[None] for a in sm["w_alpha_up"])

    untr = lambda a: jnp.transpose(a, (1, 2, 0))
    big = dict(w_in=tuple(untr(a) for a in (g_win_t,) + tuple(_adamw(tr(w_in), g_win_t, tr(m_w_in), tr(v_w_in), "adamw_w_in"))))
    for nm, w, g, m, v in (("w_out_a", w_out_a, g_wa, m_w_out_a, v_w_out_a),
                           ("w_out_b", w_out_b, g_wb, m_w_out_b, v_w_out_b), ("w_o", w_o, g_wo, m_w_o, v_w_o)):
        big[nm] = (g[None],) + tuple(_adamw(w, g[None], m, v, "adamw_" + nm))

    order = ("g_in", "w_in", "w_alpha_up", "b_alpha", "attn_sinks", "g_gla_norm", "w_out_a", "w_out_b", "w_o", "g_final")
    outs = [big[nm][kind] if nm in big else sm[nm][kind] for kind in range(4) for nm in order]
    return (loss, grad_x2.reshape(x.shape), *outs)
```

```python
import math
from typing import NamedTuple

import numpy as np
import jax
import jax.numpy as jnp
from jax import lax
from jax.experimental import pallas as pl
from jax.experimental.pallas import tpu as pltpu

D_MODEL = 1024
A_HEADS, A_KV_HEADS, A_HEAD_DIM = 8, 2, 64
A_GROUP = A_HEADS // A_KV_HEADS
A_WIDTH, A_KV_WIDTH = 512, 128
BLOCK = 128
ROPE_THETA = 500000.0
ROPE_DIM = 16
B_HEADS, B_KEY_DIM, B_VAL_DIM = 4, 64, 128
B_KEY_WIDTH, B_WIDTH = 256, 512
B_GATE_RANK = 16
B_GATE_TEMP = 16.0
B_CHUNK = 64
NORM_EPS = 1e-6
NEG_BIG = -1e30
D_IN = 4880

ADAM_LR, ADAM_B1, ADAM_B2, ADAM_EPS, ADAM_WD, ADAM_STEP = 0.001, 0.9, 0.999, 1e-08, 0.01, 10

LANE = 128
ALR_AT = 2816
PIECES = (("qkv", 0, 768), ("za", 768, 1280), ("qkb", 1280, 1792), ("vb", 1792, 2304),
          ("zb", 2304, 2816), ("alr", ALR_AT, ALR_AT + LANE), ("ga", 2832, 3856), ("gb", 3856, 4880))
SHARD = D_IN // 4
WINDOW_STEP = 1216
WINDOW_ROWS = 1232

GLA_BLOCK = 256
GLA_FWD_BLOCK = 1024
GLA_BWD_TILE = 4
MERGE_SLAB = 16
VMEM_LIMIT = 56 * 1024 * 1024

_F32 = jnp.float32
_MX = jnp.bfloat16
_ST = jnp.bfloat16

_MESH = pl.DeviceIdType.MESH
_ANY = pl.BlockSpec(memory_space=pl.ANY)


def _cparams(sem=None, vmem=None):
    return pltpu.CompilerParams(dimension_semantics=sem, vmem_limit_bytes=vmem)


def _dot(a, b):
    return jnp.dot(a.astype(_MX), b.astype(_MX), preferred_element_type=_F32)


def _dot_nt(a, b):
    return lax.dot_general(a.astype(_MX), b.astype(_MX), (((1,), (1,)), ((), ())),
                           preferred_element_type=_F32)


def _dot_tn(a, b):
    return lax.dot_general(a.astype(_MX), b.astype(_MX), (((0,), (0,)), ((), ())),
                           preferred_element_type=_F32)


def _dot_ones(ones_mat, v):
    o = ones_mat.astype(jnp.bfloat16)
    v0 = v.astype(jnp.bfloat16)
    v1 = (v - v0.astype(_F32)).astype(jnp.bfloat16)
    d = lambda t: jnp.dot(o, t, preferred_element_type=_F32)
    return d(v0) + d(v1)


def _sigmoid(x):
    return 0.5 * jnp.tanh(0.5 * x) + 0.5


def _log_sigmoid(x):
    return jnp.minimum(x, 0.0) - jnp.log(1.0 + jnp.exp(-jnp.abs(x)))


def _lane_tile(t, width):
    reps = width // t.shape[1]
    return t if reps == 1 else jnp.tile(t, (1, reps))


def _rope(t, cos, sa, sb, sign):
    w = t.shape[1]
    rot = pltpu.roll(t, w - 8, 1) * _lane_tile(sa, w) + pltpu.roll(t, 8, 1) * _lane_tile(sb, w)
    return t * _lane_tile(cos, w) + sign * rot


def _rms_bwd(dy_g, n, r):
    return r * (dy_g - n * jnp.mean(dy_g * n, axis=-1, keepdims=True))


ROPE_ROWS = 256


def _rope_consts():
    lane = np.arange(LANE) % A_HEAD_DIM
    half = ROPE_DIM // 2
    inv = np.exp((np.float32(-math.log(ROPE_THETA)) * np.arange(half, dtype=np.float32)) * np.float32(2.0 / ROPE_DIM))
    consts = np.zeros((8, LANE), np.float32)
    consts[0] = np.where(lane < ROPE_DIM, inv[lane % half], 0.0)
    consts[1] = np.where(lane < half, -1.0, 0.0)
    consts[2] = np.where((lane >= half) & (lane < ROPE_DIM), 1.0, 0.0)
    return jnp.asarray(consts)


def _rope_tables_into(pos_ref, c_ref, cos_ref, sa_ref, sb_ref):
    def rows_of(b, carry):
        rows = pl.ds(pl.multiple_of(b * ROPE_ROWS, ROPE_ROWS), ROPE_ROWS)
        ang = pos_ref[rows, :] * c_ref[0:1, :]
        s = jnp.sin(ang)
        cos_ref[rows, :] = jnp.cos(ang)
        sa_ref[rows, :] = s * c_ref[1:2, :]
        sb_ref[rows, :] = s * c_ref[2:3, :]
        return carry

    lax.fori_loop(0, pos_ref.shape[0] // ROPE_ROWS, rows_of, 0)


def _in_proj(x2, g_in, wt, cos, sa, sb):
    t = x2.shape[0]
    tm = min(t, 512)

    def body(x_ref, g_ref, w_ref, cos_ref, sa_ref, sb_ref, h_ref, qkv_ref, za_ref, qkb_ref,
             vb_ref, zb_ref, alr_ref, ga_ref, gb_ref):
        xv = x_ref[...]
        r = lax.rsqrt(jnp.mean(xv * xv, axis=-1, keepdims=True) + NORM_EPS)
        h = (xv * r * g_ref[...]).astype(_MX)
        h_ref[...] = h.astype(_ST)
        outs = dict(za=za_ref, qkb=qkb_ref, vb=vb_ref, zb=zb_ref, alr=alr_ref, ga=ga_ref, gb=gb_ref)
        for name, a, b in PIECES:
            p = _dot_nt(h, w_ref[a:b, :])
            if name == "qkv":
                c, s1, s2 = cos_ref[...], sa_ref[...], sb_ref[...]
                qkv_ref[:, 0:512] = _rope(p[:, 0:512], c, s1, s2, 1.0).astype(_ST)
                qkv_ref[:, 512:640] = _rope(p[:, 512:640], c, s1, s2, 1.0).astype(_ST)
                qkv_ref[:, 640:768] = p[:, 640:768].astype(_ST)
            else:
                outs[name][...] = p.astype(outs[name].dtype)

    rows = lambda w: pl.BlockSpec((tm, w), lambda i: (i, 0))
    shp = lambda name, w: jax.ShapeDtypeStruct((t, w), _F32 if name == "qkb" else _ST)
    widths = [D_MODEL] + [b - a for _, a, b in PIECES]
    return pl.pallas_call(
        body, name="in_proj", grid=(t // tm,),
        in_specs=[rows(D_MODEL), pl.BlockSpec((1, D_MODEL), lambda i: (0, 0)),
                  pl.BlockSpec((D_IN, D_MODEL), lambda i: (0, 0), pipeline_mode=pl.Buffered(1)),
                  rows(LANE), rows(LANE), rows(LANE)],
        out_specs=[rows(w) for w in widths],
        out_shape=[shp(n, w) for n, w in zip(["h"] + [p[0] for p in PIECES], widths)],
        compiler_params=_cparams(("parallel",), VMEM_LIMIT),
    )(x2, g_in, wt, cos, sa, sb)


def _attn_operands(k_prev, k_cur, v_prev, v_cur, want_bwd):
    kf = jnp.concatenate([k_prev, k_cur], axis=0).astype(_F32) * (A_HEAD_DIM ** -0.5)
    vf = jnp.concatenate([v_prev, v_cur], axis=0).astype(_F32)
    lo = lax.broadcasted_iota(jnp.int32, (1, LANE), 1) < 64

    def on_lanes(a):
        sw = pltpu.roll(a, 64, 1)
        z = jnp.zeros_like(a)
        return [[jnp.where(lo, a, z).astype(_MX), jnp.where(lo, z, sw).astype(_MX)],
                [jnp.where(lo, sw, z).astype(_MX), jnp.where(lo, z, a).astype(_MX)]]

    def on_rows(a):
        at = a.T.astype(_MX)
        z = jnp.zeros((64, at.shape[1]), _MX)
        top, bot = at[0:64], at[64:128]
        return [[jnp.concatenate([top, z], axis=0), jnp.concatenate([z, top], axis=0)],
                [jnp.concatenate([bot, z], axis=0), jnp.concatenate([z, bot], axis=0)]]

    ops = dict(k_lanes=on_lanes(kf), v_rows=on_rows(vf), lo=lo)
    if want_bwd:
        ops.update(v_lanes=on_lanes(vf), k_rows=on_rows(kf))
    return ops


def _attn_valid(n):
    kj = lax.broadcasted_iota(jnp.int32, (2 * BLOCK, 2 * BLOCK), 0) - BLOCK
    qi = lax.broadcasted_iota(jnp.int32, (2 * BLOCK, 2 * BLOCK), 1) & (BLOCK - 1)
    return (kj <= qi) & (qi - kj < BLOCK) & ((n > 0) | (kj >= 0))


def _attn_sinks(sink_ref, h_a, h_b):
    first = lax.broadcasted_iota(jnp.int32, (1, 2 * BLOCK), 1) < BLOCK
    return jnp.where(first, sink_ref[h_a], sink_ref[h_b])


def _attn_softmax_t(k_lanes, q_pair, valid, sink):
    s = jnp.where(valid, _dot_nt(k_lanes, q_pair), NEG_BIG)
    m = jnp.maximum(jnp.max(s, axis=0, keepdims=True), sink)
    e = jnp.exp(s - m)
    e_sink = jnp.exp(sink - m)
    inv = 1.0 / (jnp.sum(e, axis=0, keepdims=True) + e_sink)
    return e, e_sink, inv


ATTN_TILE = 8


def _attn_kv(qkv_ref, kvp_ref, j):
    rows = slice(j * BLOCK, (j + 1) * BLOCK)
    if j == 0:
        k_prev, v_prev = kvp_ref[:, 0:128], kvp_ref[:, 128:256]
    else:
        before = slice((j - 1) * BLOCK, j * BLOCK)
        k_prev, v_prev = qkv_ref[before, 512:640], qkv_ref[before, 640:768]
    return k_prev, qkv_ref[rows, 512:640], v_prev, qkv_ref[rows, 640:768]


def _attn_fwd(qkv, za, sinks, nseq):
    t = qkv.shape[0]
    nblk = min(ATTN_TILE, t // nseq // BLOCK)
    tile = nblk * BLOCK
    nt = t // nseq // tile

    def body(sink_ref, qkv_ref, kvp_ref, za_ref, oa_ref):
        for j in range(nblk):
            rows = slice(j * BLOCK, (j + 1) * BLOCK)
            ops = _attn_operands(*_attn_kv(qkv_ref, kvp_ref, j), False)
            valid = _attn_valid(nblk * pl.program_id(1) + j)[:, 0:BLOCK]
            for pr in range(A_HEADS // 2):
                lanes = slice(pr * LANE, (pr + 1) * LANE)
                g = pr // (A_GROUP // 2)
                q_pair = qkv_ref[rows, lanes]
                ot = None
                for half in range(2):
                    e, _, inv = _attn_softmax_t(ops["k_lanes"][g][half], q_pair, valid, sink_ref[2 * pr + half])
                    part = _dot(ops["v_rows"][g][half], e) * inv
                    ot = part if ot is None else ot + part
                z = za_ref[rows, lanes].astype(_F32)
                oa_ref[rows, lanes] = (ot.T * (z * _sigmoid(z))).astype(_ST)

    cur = lambda w: pl.BlockSpec((tile, w), lambda s, n: (s * nt + n, 0))
    return pl.pallas_call(
        body, name="attn_fwd", grid=(nseq, nt),
        in_specs=[pl.BlockSpec(memory_space=pltpu.SMEM), cur(768),
                  pl.BlockSpec((BLOCK, 256), lambda s, n: (nblk * (s * nt + n) - jnp.minimum(n, 1), 2)),
                  cur(512)],
        out_specs=cur(512), out_shape=jax.ShapeDtypeStruct((t, A_WIDTH), _ST),
        compiler_params=_cparams(("parallel", "arbitrary")),
    )(sinks, qkv, qkv, za)


def _attn_bwd(qkv, za, doa, sinks, cos, sa, sb, nseq):
    t = qkv.shape[0]
    nblk = min(ATTN_TILE, t // nseq // BLOCK)
    tile = nblk * BLOCK
    nt = t // nseq // tile

    def body(sink_ref, qkv_ref, kvp_ref, za_ref, doa_ref, cos_ref, sa_ref, sb_ref,
             dqkv_ref, dza_ref, dsink_ref, ck_ref, cv_ref):
        s_id, i = pl.program_id(0), pl.program_id(1)

        @pl.when((s_id == 0) & (i == 0))
        def _():
            dsink_ref[...] = jnp.zeros_like(dsink_ref)

        @pl.when(i == 0)
        def _():
            ck_ref[...] = jnp.zeros_like(ck_ref)
            cv_ref[...] = jnp.zeros_like(cv_ref)

        carry_k, carry_v = ck_ref[...], cv_ref[...]
        for j in reversed(range(nblk)):
            rows = slice(j * BLOCK, (j + 1) * BLOCK)
            ops = _attn_operands(*_attn_kv(qkv_ref, kvp_ref, j), True)
            lo = ops["lo"]
            valid = _attn_valid(nblk * (nt - 1 - i) + j)
            dk_acc, dv_acc, dq_pairs = [], [], []
            for g in range(A_KV_HEADS):
                pairs = [slice((2 * g + p) * LANE, (2 * g + p + 1) * LANE) for p in range(2)]
                q_both = jnp.concatenate([qkv_ref[rows, p] for p in pairs], axis=0)
                q_f = q_both.astype(_F32)
                z = [za_ref[rows, p].astype(_F32) for p in pairs]
                sz = [_sigmoid(v) for v in z]
                d_oa = [doa_ref[rows, p].astype(_F32) for p in pairs]
                d_att = jnp.concatenate([d_oa[p] * (z[p] * sz[p]) for p in range(2)], axis=0)
                zero = jnp.zeros_like(d_att)
                ot, dqt, ds_all, pn_all, qz_all, daz_all = None, None, [], [], [], []
                for half in range(2):
                    heads = (4 * g + half, 4 * g + 2 + half)
                    e, e_sink, inv = _attn_softmax_t(ops["k_lanes"][g][half], q_both, valid,
                                                     _attn_sinks(sink_ref, *heads))
                    pn = e * inv
                    dpt = _dot_nt(ops["v_lanes"][g][half], d_att)
                    delta = jnp.sum(pn * dpt, axis=0, keepdims=True)
                    ds = (pn * (dpt - delta)).astype(_MX)
                    pn = pn.astype(_MX)
                    d_sink = e_sink * inv * delta
                    for p, h in enumerate(heads):
                        dsink_ref[h:h + 1, :] = dsink_ref[h:h + 1, :] - jnp.sum(d_sink[:, p * BLOCK:(p + 1) * BLOCK])
                    o_part = _dot(ops["v_rows"][g][half], pn)
                    dq_part = _dot(ops["k_rows"][g][half], ds)
                    ot = o_part if ot is None else ot + o_part
                    dqt = dq_part if dqt is None else dqt + dq_part
                    mine = lo if half == 0 else jnp.logical_not(lo)
                    ds_all.append(ds)
                    pn_all.append(pn)
                    qz_all.append(jnp.where(mine, q_f, zero).astype(_MX))
                    daz_all.append(jnp.where(mine, d_att, zero).astype(_MX))
                dk_acc.append(_dot(jnp.concatenate(ds_all, axis=1), jnp.concatenate(qz_all, axis=0)))
                dv_acc.append(_dot(jnp.concatenate(pn_all, axis=1), jnp.concatenate(daz_all, axis=0)))
                for p, lanes in enumerate(pairs):
                    cols = slice(p * BLOCK, (p + 1) * BLOCK)
                    dza_ref[rows, lanes] = (d_oa[p] * ot[:, cols].T * (sz[p] * (1.0 + z[p] * (1.0 - sz[p])))).astype(_ST)
                    dq_pairs.append(dqt[:, cols].T)

            def fold(acc, scale):
                both = [a + pltpu.roll(a, 64, 1) for a in acc]
                return jnp.where(lo, both[0], both[1]) * scale

            dk_full = fold(dk_acc, A_HEAD_DIM ** -0.5)
            dv_full = fold(dv_acc, 1.0)
            dk_cur, dv_cur = dk_full[BLOCK:] + carry_k, dv_full[BLOCK:] + carry_v
            carry_k, carry_v = dk_full[:BLOCK], dv_full[:BLOCK]
            c, s1, s2 = cos_ref[rows, :], sa_ref[rows, :], sb_ref[rows, :]
            dqkv_ref[rows, 0:512] = _rope(jnp.concatenate(dq_pairs, axis=1), c, s1, s2, -1.0).astype(_ST)
            dqkv_ref[rows, 512:640] = _rope(dk_cur, c, s1, s2, -1.0).astype(_ST)
            dqkv_ref[rows, 640:768] = dv_cur.astype(_ST)
        ck_ref[...] = carry_k
        cv_ref[...] = carry_v

    cur = lambda w: pl.BlockSpec((tile, w), lambda s, i: (s * nt + nt - 1 - i, 0))
    return pl.pallas_call(
        body, name="attn_bwd", grid=(nseq, nt),
        in_specs=[pl.BlockSpec(memory_space=pltpu.SMEM), cur(768),
                  pl.BlockSpec((BLOCK, 256),
                               lambda s, i: (nblk * (s * nt + nt - 1 - i) - jnp.minimum(nt - 1 - i, 1), 2)),
                  cur(512), cur(512), cur(LANE), cur(LANE), cur(LANE)],
        out_specs=[cur(768), cur(512), pl.BlockSpec((8, LANE), lambda s, i: (0, 0))],
        out_shape=[jax.ShapeDtypeStruct((t, 768), _ST), jax.ShapeDtypeStruct((t, 512), _ST),
                   jax.ShapeDtypeStruct((8, LANE), _F32)],
        scratch_shapes=[pltpu.VMEM((BLOCK, A_KV_WIDTH), _F32), pltpu.VMEM((BLOCK, A_KV_WIDTH), _F32)],
        compiler_params=_cparams(("arbitrary", "arbitrary")),
    )(sinks, qkv, qkv, za, doa, cos, sa, sb)


def _gla_chunk_terms(la, qkb_ref, r0):
    g = la[r0:r0 + B_CHUNK, :]
    ri = lax.broadcasted_iota(jnp.int32, (B_CHUNK, B_CHUNK), 0)
    ci = lax.broadcasted_iota(jnp.int32, (B_CHUNK, B_CHUNK), 1)
    cum = _dot_ones((ri >= ci).astype(_F32), g)
    last = cum[B_CHUNK - 1:B_CHUNK, :]
    mid = cum[B_CHUNK // 2 - 1:B_CHUNK // 2, :]
    q = qkb_ref[r0:r0 + B_CHUNK, 0:B_KEY_WIDTH].astype(_F32) * (B_KEY_DIM ** -0.5)
    k = qkb_ref[r0:r0 + B_CHUNK, B_KEY_WIDTH:2 * B_KEY_WIDTH].astype(_F32)
    e_q, e_k, e_l, e_c = jnp.exp(cum - mid), jnp.exp(mid - cum), jnp.exp(last - cum), jnp.exp(cum)
    dec_col = jnp.exp(jnp.sum(g.T, axis=1, keepdims=True))
    return dict(qm=q * e_q, km=k * e_k, kl=k * e_l, qc=q * e_c, e_q=e_q, e_k=e_k, e_l=e_l, e_c=e_c,
                dec_col=dec_col, dec_row=jnp.exp(last), causal=ri >= ci, ri=ri)


def _gate_logits(alr_ref, wup_ref, b_ref):
    return _dot(alr_ref[...], wup_ref[...]) + b_ref[...]


def _gla_fwd(qkb, vb, zb, alr, wup, b_alpha, gn, nseq):
    t = qkb.shape[0]
    tb = min(GLA_FWD_BLOCK, t // nseq)
    nblk = t // nseq // tb
    cpb = tb // B_CHUNK

    def body(qkb_ref, vb_ref, zb_ref, alr_ref, wup_ref, b_ref, gn_ref, ob_ref, oraw_ref, sst_ref, s_ref):
        @pl.when(pl.program_id(1) == 0)
        def _():
            s_ref[...] = jnp.zeros_like(s_ref)

        la = _log_sigmoid(_gate_logits(alr_ref, wup_ref, b_ref)) * (1.0 / B_GATE_TEMP)
        terms = [_gla_chunk_terms(la, qkb_ref, c * B_CHUNK) for c in range(cpb)]
        o_intra, inc = {}, {}
        for c, tm in enumerate(terms):
            for h in range(B_HEADS):
                kl_, vl_ = slice(h * 64, (h + 1) * 64), slice(h * 128, (h + 1) * 128)
                v = vb_ref[c * B_CHUNK:(c + 1) * B_CHUNK, vl_]
                a = jnp.where(tm["causal"], _dot_nt(tm["qm"][:, kl_], tm["km"][:, kl_]), 0.0)
                o_intra[c, h] = _dot(a, v)
                inc[c, h] = _dot_tn(tm["kl"][:, kl_], v)
        o_heads = {}
        for h in range(B_HEADS):
            kl_ = slice(h * 64, (h + 1) * 64)
            st = s_ref[kl_, :]
            for c, tm in enumerate(terms):
                sst_ref[c, kl_, :] = st
                o_heads[c, h] = o_intra[c, h] + _dot(tm["qc"][:, kl_], st)
                st = tm["dec_col"][kl_, :] * st + inc[c, h]
            s_ref[kl_, :] = st
        o = jnp.concatenate([jnp.concatenate([o_heads[c, h] for h in range(B_HEADS)], axis=1)
                             for c in range(cpb)], axis=0)
        oraw_ref[...] = o
        z = zb_ref[...].astype(_F32)
        gate = z * _sigmoid(z)
        for h in range(B_HEADS):
            vl_ = slice(h * 128, (h + 1) * 128)
            oh = o[:, vl_]
            r = lax.rsqrt(jnp.mean(oh * oh, axis=-1, keepdims=True) + NORM_EPS)
            ob_ref[:, vl_] = ((oh * r) * gn_ref[:, vl_] * gate[:, vl_]).astype(_ST)

    rows = lambda w: pl.BlockSpec((tb, w), lambda s, i: (s * nblk + i, 0))
    full = lambda a, b: pl.BlockSpec((a, b), lambda s, i: (0, 0))
    return pl.pallas_call(
        body, name="gla_fwd", grid=(nseq, nblk),
        in_specs=[rows(512), rows(512), rows(512), rows(LANE), full(LANE, B_KEY_WIDTH),
                  full(1, B_KEY_WIDTH), full(1, B_WIDTH)],
        out_specs=[rows(512), rows(512),
                   pl.BlockSpec((cpb, B_KEY_WIDTH, B_VAL_DIM), lambda s, i: (s * nblk + i, 0, 0))],
        out_shape=[jax.ShapeDtypeStruct((t, B_WIDTH), _ST), jax.ShapeDtypeStruct((t, B_WIDTH), _F32),
                   jax.ShapeDtypeStruct((t // B_CHUNK, B_KEY_WIDTH, B_VAL_DIM), _F32)],
        scratch_shapes=[pltpu.VMEM((B_KEY_WIDTH, B_VAL_DIM), _F32)],
        compiler_params=_cparams(("parallel", "arbitrary")),
    )(qkb, vb, zb, alr, wup, b_alpha, gn)


def _gla_bwd(qkb, vb, zb, alr, oraw, dob, sst, wup, b_alpha, gn, nseq):
    t = qkb.shape[0]
    tb = min(GLA_BLOCK, t // nseq)
    tile = min(GLA_BWD_TILE, t // nseq // tb)
    nstep = t // nseq // (tile * tb)
    cpb = tb // B_CHUNK

    def body(qkb_ref, vb_ref, zb_ref, alr_ref, oraw_ref, dob_ref, sst_ref, wup_ref, b_ref, gn_ref,
             dqkb_ref, dvb_ref, dzb_ref, dalr_ref, dwup_ref, db_ref, dgn_ref, ds_ref):
        s_id, i = pl.program_id(0), pl.program_id(1)

        @pl.when((s_id == 0) & (i == 0))
        def _():
            dwup_ref[...] = jnp.zeros_like(dwup_ref)
            db_ref[...] = jnp.zeros_like(db_ref)
            dgn_ref[...] = jnp.zeros_like(dgn_ref)

        @pl.when(i == 0)
        def _():
            ds_ref[...] = jnp.zeros_like(ds_ref)

        for sb in reversed(range(tile)):
            one_block(sb, qkb_ref, vb_ref, zb_ref, alr_ref, oraw_ref, dob_ref, sst_ref, wup_ref, b_ref, gn_ref,
                      dqkb_ref, dvb_ref, dzb_ref, dalr_ref, dwup_ref, db_ref, dgn_ref, ds_ref)

    def one_block(sb, qkb_ref, vb_ref, zb_ref, alr_ref, oraw_ref, dob_ref, sst_ref, wup_ref, b_ref, gn_ref,
                  dqkb_ref, dvb_ref, dzb_ref, dalr_ref, dwup_ref, db_ref, dgn_ref, ds_ref):
        base = sb * tb
        rows = slice(base, base + tb)
        a_pre = _dot(alr_ref[rows, :], wup_ref[...]) + b_ref[...]
        la = _log_sigmoid(a_pre) * (1.0 / B_GATE_TEMP)

        z = zb_ref[rows, :].astype(_F32)
        sz = _sigmoid(z)
        d_ob = dob_ref[rows, :].astype(_F32)
        tg = d_ob * (z * sz)
        dsilu = sz * (1.0 + z * (1.0 - sz))
        do_cols, dgn_cols = [], []
        for h in range(B_HEADS):
            vl_ = slice(h * 128, (h + 1) * 128)
            oh = oraw_ref[rows, vl_].astype(_F32)
            r = lax.rsqrt(jnp.mean(oh * oh, axis=-1, keepdims=True) + NORM_EPS)
            on = oh * r
            gnh = gn_ref[:, vl_]
            dzb_ref[rows, vl_] = (d_ob[:, vl_] * (on * gnh) * dsilu[:, vl_]).astype(_ST)
            dgn_cols.append(jnp.sum(tg[:, vl_] * on, axis=0, keepdims=True))
            do_cols.append(_rms_bwd(tg[:, vl_] * gnh, on, r))
        dgn_ref[...] = dgn_ref[...] + jnp.concatenate(dgn_cols, axis=1)
        d_o = jnp.concatenate(do_cols, axis=1)

        ri = lax.broadcasted_iota(jnp.int32, (tb, tb), 0)
        ci = lax.broadcasted_iota(jnp.int32, (tb, tb), 1)
        same = (ri // B_CHUNK) == (ci // B_CHUNK)
        low = same & (ri >= ci)
        cum = _dot_ones(low.astype(_F32), la)
        at_row = lambda r: jnp.concatenate([jnp.broadcast_to(cum[c * B_CHUNK + r:c * B_CHUNK + r + 1], (B_CHUNK, B_KEY_WIDTH))
                                            for c in range(cpb)], axis=0)
        last, mid = at_row(B_CHUNK - 1), at_row(B_CHUNK // 2 - 1)
        e_q, e_k, e_l, e_c = jnp.exp(cum - mid), jnp.exp(mid - cum), jnp.exp(last - cum), jnp.exp(cum)
        q = qkb_ref[rows, 0:B_KEY_WIDTH] * (B_KEY_DIM ** -0.5)
        k = qkb_ref[rows, B_KEY_WIDTH:2 * B_KEY_WIDTH]
        qm, km, kl, qc = q * e_q, k * e_k, k * e_l, q * e_c
        lane_head = lax.broadcasted_iota(jnp.int32, (1, B_KEY_WIDTH), 1) // B_KEY_DIM
        d_o_mx = d_o.astype(_MX)

        def on_diagonal(st):
            z = jnp.zeros((B_KEY_DIM, B_VAL_DIM), st.dtype)
            return jnp.concatenate([jnp.concatenate(
                [st[h * B_KEY_DIM:(h + 1) * B_KEY_DIM] if g == h else z for g in range(B_HEADS)], axis=1)
                for h in range(B_HEADS)], axis=0)

        def diagonal_of(full):
            return jnp.concatenate([full[h * B_KEY_DIM:(h + 1) * B_KEY_DIM, h * B_VAL_DIM:(h + 1) * B_VAL_DIM]
                                    for h in range(B_HEADS)], axis=0)

        dqm, dkm, dv_cols = None, None, []
        for h in range(B_HEADS):
            vl_ = slice(h * B_VAL_DIM, (h + 1) * B_VAL_DIM)
            mine = lane_head == h
            qz, kz = jnp.where(mine, qm, 0.0).astype(_MX), jnp.where(mine, km, 0.0).astype(_MX)
            a = jnp.where(low, _dot_nt(qz, kz), 0.0).astype(_MX)
            da = jnp.where(low, _dot_nt(d_o_mx[:, vl_], vb_ref[rows, vl_]), 0.0).astype(_MX)
            dqm_h, dkm_h = _dot(da, kz), _dot_tn(da, qz)
            dqm = dqm_h if dqm is None else dqm + dqm_h
            dkm = dkm_h if dkm is None else dkm + dkm_h
            dv_cols.append(_dot_tn(a, d_o_mx[:, vl_]))
        dv = jnp.concatenate(dv_cols, axis=1)

        chunk = [slice(c * B_CHUNK, (c + 1) * B_CHUNK) for c in range(cpb)]
        dqc_rows, g_loc = [], []
        for c in range(cpb):
            dqc_rows.append(_dot_nt(d_o_mx[chunk[c]], on_diagonal(sst_ref[sb * cpb + c].astype(_MX))))
            g_loc.append(diagonal_of(_dot_tn(qc[chunk[c]], d_o_mx[chunk[c]])))
        cur = ds_ref[...]
        d_state = [None] * cpb
        for c in reversed(range(cpb)):
            d_state[c] = cur
            cur = g_loc[c] + jnp.exp(jnp.sum(la[chunk[c]].T, axis=1, keepdims=True)) * cur
        ds_ref[...] = cur
        dkl_rows, dv_rows, dlast_rows = [], [], []
        ones8 = jnp.ones((8, B_VAL_DIM), _F32)
        for c in range(cpb):
            dsd = on_diagonal(d_state[c].astype(_MX))
            dkl_c = _dot_nt(vb_ref[base + c * B_CHUNK:base + (c + 1) * B_CHUNK, :], dsd)
            dkl_rows.append(dkl_c)
            dv_rows.append(_dot(kl[chunk[c]], dsd))
            prod = d_state[c] * sst_ref[sb * cpb + c]
            p0 = prod.astype(jnp.bfloat16)
            p1 = (prod - p0.astype(_F32)).astype(jnp.bfloat16)
            ddec = (_dot_nt(ones8, p0) + _dot_nt(ones8, p1))[0:1]
            r_last = c * B_CHUNK + B_CHUNK - 1
            dlast = jnp.sum(dkl_c * kl[chunk[c]], axis=0, keepdims=True) + ddec * jnp.exp(last[r_last:r_last + 1])
            dlast_rows.append(jnp.broadcast_to(dlast, (B_CHUNK, B_KEY_WIDTH)))
        dqc, dkl = jnp.concatenate(dqc_rows, axis=0), jnp.concatenate(dkl_rows, axis=0)
        dqkb_ref[rows, 0:B_KEY_WIDTH] = ((dqm * e_q + dqc * e_c) * (B_KEY_DIM ** -0.5)).astype(_ST)
        dqkb_ref[rows, B_KEY_WIDTH:2 * B_KEY_WIDTH] = (dkm * e_k + dkl * e_l).astype(_ST)
        dvb_ref[rows, :] = (dv + jnp.concatenate(dv_rows, axis=0)).astype(_ST)
        dcum = dqm * qm - dkm * km + dqc * qc - dkl * kl
        row = lax.broadcasted_iota(jnp.int32, (tb, B_KEY_WIDTH), 0)
        dcum = jnp.where(row % B_CHUNK == B_CHUNK - 1, dcum + jnp.concatenate(dlast_rows, axis=0), dcum)
        dla = _dot_ones((same & (ri <= ci)).astype(_F32), dcum)

        da_pre = dla * (1.0 / B_GATE_TEMP) * (1.0 - _sigmoid(a_pre))
        dalr_ref[rows, :] = _dot_nt(da_pre, wup_ref[...]).astype(_ST)
        dwup_ref[...] = dwup_ref[...] + _dot_tn(alr_ref[rows, :], da_pre)
        db_ref[...] = db_ref[...] + jnp.sum(da_pre, axis=0, keepdims=True)

    blk = lambda s, i: s * nstep + nstep - 1 - i
    rows = lambda w: pl.BlockSpec((tile * tb, w), lambda s, i: (blk(s, i), 0))
    full = lambda a, b: pl.BlockSpec((a, b), lambda s, i: (0, 0))
    act = lambda w: jax.ShapeDtypeStruct((t, w), _ST)
    return pl.pallas_call(
        body, name="gla_bwd", grid=(nseq, nstep),
        in_specs=[rows(512), rows(512), rows(512), rows(LANE), rows(512), rows(512),
                  pl.BlockSpec((tile * cpb, B_KEY_WIDTH, B_VAL_DIM), lambda s, i: (blk(s, i), 0, 0)),
                  full(LANE, B_KEY_WIDTH), full(1, B_KEY_WIDTH), full(1, B_WIDTH)],
        out_specs=[rows(512), rows(512), rows(512), rows(LANE), full(LANE, B_KEY_WIDTH),
                   full(1, B_KEY_WIDTH), full(1, B_WIDTH)],
        out_shape=[act(512), act(512), act(512), act(LANE),
                   jax.ShapeDtypeStruct((LANE, B_KEY_WIDTH), _F32),
                   jax.ShapeDtypeStruct((1, B_KEY_WIDTH), _F32), jax.ShapeDtypeStruct((1, B_WIDTH), _F32)],
        scratch_shapes=[pltpu.VMEM((B_KEY_WIDTH, B_VAL_DIM), _F32)],
        compiler_params=_cparams(("arbitrary", "arbitrary")),
    )(qkb, vb, zb, alr, oraw, dob, sst, wup, b_alpha, gn)


def _merge_loss(oa, ob, ga, gb, x2, tgt, wa, wb, wo, g_final):
    t = x2.shape[0]
    tm = min(t, 512)
    nt = t // tm

    def body(oa_ref, ob_ref, ga_ref, gb_ref, x_ref, t_ref, wa_ref, wb_ref, wo_ref, gf_ref,
             dh_ref, doa_ref, dob_ref, dga_ref, dgb_ref, dwa_ref, dwb_ref, dwo_ref, dgf_ref, loss_ref,
             ya_s, yb_s, out_s, dmer_s, mrg_s, dya_s, dyb_s):
        first = pl.program_id(0) == 0
        so_far = lambda ref: jnp.where(first, 0.0, ref[...])

        slabs = [slice(s, s + MERGE_SLAB) for s in range(0, tm, MERGE_SLAB)]
        fold = lambda a: a[0:8] + a[8:16]
        ya_s[...] = _dot(oa_ref[...], wa_ref[...])
        yb_s[...] = _dot(ob_ref[...], wb_ref[...])
        for rows_ in slabs:
            sga, sgb = _sigmoid(ga_ref[rows_, :].astype(_F32)), _sigmoid(gb_ref[rows_, :].astype(_F32))
            mrg_s[rows_, :] = (sga * ya_s[rows_, :] + sgb * yb_s[rows_, :]).astype(_MX)
        out_s[...] = x_ref[...] + _dot(mrg_s[...], wo_ref[...])
        gf = gf_ref[...]
        loss8 = jnp.zeros((8, D_MODEL), _F32)
        dgf8 = jnp.zeros((8, D_MODEL), _F32)
        for rows_ in slabs:
            out = out_s[rows_, :]
            r = lax.rsqrt(jnp.mean(out * out, axis=-1, keepdims=True) + NORM_EPS)
            nrm = out * r
            err = nrm * gf - t_ref[rows_, :]
            loss8 = loss8 + fold(err * err)
            dy = err * (1.0 / D_MODEL)
            dgf8 = dgf8 + fold(dy * nrm)
            dh = _rms_bwd(dy * gf, nrm, r)
            dh_ref[rows_, :] = dh.astype(_ST)
        loss_ref[...] = so_far(loss_ref) + (0.5 / D_MODEL) * jnp.sum(loss8, axis=0, keepdims=True)
        dgf_ref[...] = so_far(dgf_ref) + jnp.sum(dgf8, axis=0, keepdims=True)
        dmer_s[...] = _dot_nt(dh_ref[...], wo_ref[...])
        dwo_ref[...] = so_far(dwo_ref) + _dot_tn(mrg_s[...], dh_ref[...])
        for rows_ in slabs:
            sga, sgb = _sigmoid(ga_ref[rows_, :].astype(_F32)), _sigmoid(gb_ref[rows_, :].astype(_F32))
            dmer = dmer_s[rows_, :]
            da, db = dmer * sga, dmer * sgb
            dya_s[rows_, :] = da.astype(_MX)
            dyb_s[rows_, :] = db.astype(_MX)
            dga_ref[rows_, :] = (da * ya_s[rows_, :] * (1.0 - sga)).astype(_ST)
            dgb_ref[rows_, :] = (db * yb_s[rows_, :] * (1.0 - sgb)).astype(_ST)
        doa_ref[...] = _dot_nt(dya_s[...], wa_ref[...]).astype(_ST)
        dob_ref[...] = _dot_nt(dyb_s[...], wb_ref[...]).astype(_ST)
        dwa_ref[...] = so_far(dwa_ref) + _dot_tn(oa_ref[...], dya_s[...])
        dwb_ref[...] = so_far(dwb_ref) + _dot_tn(ob_ref[...], dyb_s[...])

    rows = lambda w: pl.BlockSpec((tm, w), lambda i: (i, 0))
    full = lambda a, b: pl.BlockSpec((a, b), lambda i: (0, 0), pipeline_mode=pl.Buffered(1))
    return pl.pallas_call(
        body, name="merge_loss", grid=(nt,),
        in_specs=[rows(512), rows(512), rows(D_MODEL), rows(D_MODEL), rows(D_MODEL), rows(D_MODEL),
                  full(A_WIDTH, D_MODEL), full(B_WIDTH, D_MODEL), full(D_MODEL, D_MODEL), full(1, D_MODEL)],
        out_specs=[rows(D_MODEL), rows(512), rows(512), rows(D_MODEL), rows(D_MODEL),
                   full(A_WIDTH, D_MODEL), full(B_WIDTH, D_MODEL), full(D_MODEL, D_MODEL),
                   full(1, D_MODEL), full(1, D_MODEL)],
        out_shape=[jax.ShapeDtypeStruct((t, D_MODEL), _ST), jax.ShapeDtypeStruct((t, 512), _ST),
                   jax.ShapeDtypeStruct((t, 512), _ST), jax.ShapeDtypeStruct((t, D_MODEL), _ST),
                   jax.ShapeDtypeStruct((t, D_MODEL), _ST),
                   jax.ShapeDtypeStruct((A_WIDTH, D_MODEL), _F32), jax.ShapeDtypeStruct((B_WIDTH, D_MODEL), _F32),
                   jax.ShapeDtypeStruct((D_MODEL, D_MODEL), _F32), jax.ShapeDtypeStruct((1, D_MODEL), _F32),
                   jax.ShapeDtypeStruct((1, D_MODEL), _F32)],
        scratch_shapes=[pltpu.VMEM((tm, D_MODEL), _F32)] * 4 + [pltpu.VMEM((tm, D_MODEL), _MX)] * 3,
        compiler_params=_cparams(("arbitrary",), VMEM_LIMIT),
    )(oa, ob, ga, gb, x2, tgt, wa, wb, wo, g_final)


def _in_proj_bwd_x(dpieces, wt, x2, dh2, g_in):
    t = x2.shape[0]
    tm = min(t, 512)
    np_ = len(PIECES)

    def body(*refs):
        dp_refs = refs[:np_]
        w_ref, x_ref, dh2_ref, g_ref, gx_ref, dg_ref = refs[np_:]

        @pl.when(pl.program_id(0) == 0)
        def _():
            dg_ref[...] = jnp.zeros_like(dg_ref)

        dh = None
        for (name, a, b), dp in zip(PIECES, dp_refs):
            part = _dot(dp[...], w_ref[a:b, :])
            dh = part if dh is None else dh + part
        xv = x_ref[...]
        r = lax.rsqrt(jnp.mean(xv * xv, axis=-1, keepdims=True) + NORM_EPS)
        nrm = xv * r
        dg_ref[...] = dg_ref[...] + jnp.sum(dh * nrm, axis=0, keepdims=True)
        gx_ref[...] = dh2_ref[...].astype(_F32) + _rms_bwd(dh * g_ref[...], nrm, r)

    rows = lambda w: pl.BlockSpec((tm, w), lambda i: (i, 0))
    full = lambda a, b: pl.BlockSpec((a, b), lambda i: (0, 0), pipeline_mode=pl.Buffered(1))
    return pl.pallas_call(
        body, name="in_proj_bwd_x", grid=(t // tm,),
        in_specs=[rows(b - a) for _, a, b in PIECES] + [full(D_IN, D_MODEL), rows(D_MODEL), rows(D_MODEL),
                                                          full(1, D_MODEL)],
        out_specs=[rows(D_MODEL), full(1, D_MODEL)],
        out_shape=[jax.ShapeDtypeStruct((t, D_MODEL), _F32), jax.ShapeDtypeStruct((1, D_MODEL), _F32)],
        compiler_params=_cparams(("arbitrary",), VMEM_LIMIT),
    )(*dpieces, wt, x2, dh2, g_in)


def _in_proj_bwd_w(h, dpieces, others, osplits):
    t = h.shape[0]
    tm = min(t, 1024)
    nt = t // tm
    np_, no = len(PIECES), len(others)
    half = D_MODEL // 2

    def body(*refs):
        h_ref, dp_refs, o_refs = refs[0], refs[1:1 + np_], refs[1 + np_:1 + np_ + no]
        mine_ref, theirs_ref = refs[1 + np_ + no:3 + np_ + no]
        r_refs = refs[3 + np_ + no:3 + np_ + 2 * no]
        acc_ref, stage_ref, keep_sem, send, recv, o_send, o_recv = refs[3 + np_ + 2 * no:]
        i = pl.program_id(0)
        x, y, c = _place()
        sibling = (x, y, 1 - c)
        early = [pltpu.make_async_remote_copy(
            src_ref=osplits[k].half(o_refs[k], 1 - c), dst_ref=r_refs[k], send_sem=o_send.at[k], recv_sem=o_recv.at[k],
            device_id=sibling, device_id_type=_MESH) for k in range(no)]

        @pl.when(i == 0)
        def _():
            for cp in early:
                cp.start()

        hv = h_ref[...]
        cols = lambda core: pl.ds(pl.multiple_of(core * half, LANE), half)
        writes = []
        by_size = sorted(range(np_), key=lambda j: PIECES[j][1] - PIECES[j][2])
        for j, ((name, a, b), dp) in [(j, (PIECES[j], dp_refs[j])) for j in by_size]:
            part = _dot_tn(dp[...], hv)
            if name == "alr":
                b = a + B_GATE_RANK
                part = part[0:B_GATE_RANK]
            acc_ref[a:b, :] = jnp.where(i == 0, 0.0, acc_ref[a:b, :]) + part
            keep = pltpu.make_async_copy(acc_ref.at[a:b, cols(c)], mine_ref.at[a:b], keep_sem.at[j])
            give = pltpu.make_async_remote_copy(
                src_ref=stage_ref.at[a:b], dst_ref=theirs_ref.at[a:b], send_sem=send.at[j],
                recv_sem=recv.at[j], device_id=sibling, device_id_type=_MESH)
            writes += [keep, give]

            @pl.when(i == nt - 1)
            def _(keep=keep, give=give, a=a, b=b):
                keep.start()
                stage_ref[a:b, :] = jnp.where(c == 0, acc_ref[a:b, half:], acc_ref[a:b, :half]).astype(_MX)
                give.start()

        @pl.when(i == nt - 1)
        def _():
            for cp in writes + early:
                cp.wait()

    rows = lambda w: pl.BlockSpec((tm, w), lambda i: (i, 0))
    halves = [jax.ShapeDtypeStruct((D_IN, half), _F32), jax.ShapeDtypeStruct((D_IN, half), _MX)]
    return pl.pallas_call(
        body, name="in_proj_bwd_w", grid=(nt,),
        in_specs=[rows(D_MODEL)] + [rows(b - a) for _, a, b in PIECES] + [_ANY] * no,
        out_specs=[_ANY] * (2 + no),
        out_shape=halves + [jax.ShapeDtypeStruct(sp.half_shape(g.shape), g.dtype) for g, sp in zip(others, osplits)],
        scratch_shapes=[pltpu.VMEM((D_IN, D_MODEL), _F32), pltpu.VMEM((D_IN, half), _MX),
                        pltpu.SemaphoreType.DMA((np_,)), pltpu.SemaphoreType.DMA((np_,)), pltpu.SemaphoreType.DMA((np_,)),
                        pltpu.SemaphoreType.DMA((no,)), pltpu.SemaphoreType.DMA((no,))],
        compiler_params=_cparams(("arbitrary",), VMEM_LIMIT),
    )(h, *dpieces, *others)


def _place():
    return lax.axis_index("x"), lax.axis_index("y"), lax.axis_index("c")


def _other_chips(x, y):
    return [(1 - x, y), (x, 1 - y), (1 - x, 1 - y)]


class _Split(NamedTuple):
    by_rows: bool
    step: int
    size: int

    def half(self, ref, c):
        r, n = ref.shape[-2:]
        if self.by_rows:
            return ref.at[:, pl.ds(pl.multiple_of(c * (n // 2), LANE), n // 2)]
        return ref.at[pl.ds(pl.multiple_of(c * (r // 2), 16), r // 2), :]

    def chip_part(self, ref, k):
        if self.by_rows:
            return ref.at[pl.ds(pl.multiple_of(k * self.step, 16), self.size), :]
        return ref.at[:, pl.ds(pl.multiple_of(k * self.size, LANE), self.size)]

    def half_shape(self, shape):
        r, n = shape
        return (r, n // 2) if self.by_rows else (r // 2, n)

    def part_shape(self, shape):
        r, n = shape
        return (self.size, n) if self.by_rows else (r, self.size)


SPLIT_W_IN_T = _Split(True, WINDOW_STEP, WINDOW_ROWS)
SPLIT_W_O = _Split(True, 256, 256)
SPLIT_W_OUT = _Split(False, 256, 256)


def _gather_weights(shards, splits, fulls, pos_f):
    nw = len(shards)
    t = pos_f.shape[0]

    def body(*refs):
        ins, (pos_ref, c_ref) = refs[:nw], refs[nw:nw + 2]
        outs, tables = refs[nw + 2:2 * nw + 2], refs[2 * nw + 2:2 * nw + 5]
        send_a, recv_a, send_b, recv_b = refs[2 * nw + 5:]
        x, y, c = _place()
        me = 2 * x + y
        peers = _other_chips(x, y)

        def place(i, k, half):
            if splits[i] is None:
                return outs[i].at[k]
            if fulls[i][0] == 4 and len(fulls[i]) == 3:
                whole = outs[i].at[k]
            else:
                whole = splits[i].chip_part(outs[i], k)
            return splits[i].half(whole, half)

        first, passed = [], []
        for i in range(nw):
            src = ins[i] if splits[i] is None else splits[i].half(ins[i], c)
            for j, (px, py) in enumerate(peers):
                cp = pltpu.make_async_remote_copy(
                    src_ref=src, dst_ref=place(i, me, c), send_sem=send_a.at[3 * i + j],
                    recv_sem=recv_a.at[3 * i + j], device_id=(px, py, c), device_id_type=_MESH)
                cp.start()
                first.append(cp)
        _rope_tables_into(pos_ref, c_ref, *tables)
        for i in range(nw):
            for j, (px, py) in enumerate(peers):
                landed = place(i, 2 * px + py, c)
                pltpu.make_async_remote_copy(
                    src_ref=landed, dst_ref=landed, send_sem=send_a.at[3 * i + j], recv_sem=recv_a.at[3 * i + j],
                    device_id=(px, py, c), device_id_type=_MESH).wait_recv()
                if splits[i] is not None:
                    cp = pltpu.make_async_remote_copy(
                        src_ref=landed, dst_ref=landed, send_sem=send_b.at[3 * i + j], recv_sem=recv_b.at[3 * i + j],
                        device_id=(x, y, 1 - c), device_id_type=_MESH)
                    cp.start()
                    passed.append(cp)
        for i in range(nw):
            if splits[i] is None:
                continue
            for j, (px, py) in enumerate(peers):
                theirs = place(i, 2 * px + py, 1 - c)
                pltpu.make_async_remote_copy(
                    src_ref=theirs, dst_ref=theirs, send_sem=send_b.at[3 * i + j], recv_sem=recv_b.at[3 * i + j],
                    device_id=(x, y, 1 - c), device_id_type=_MESH).wait_recv()
        for cp in first + passed:
            cp.wait_send()

    vm = pl.BlockSpec(memory_space=pltpu.VMEM)
    tab = jax.ShapeDtypeStruct((t, LANE), _F32)
    return pl.pallas_call(
        body, name="gather_weights",
        in_specs=[_ANY] * nw + [vm, vm], out_specs=[_ANY] * nw + [vm] * 3,
        out_shape=[jax.ShapeDtypeStruct(f, s.dtype) for f, s in zip(fulls, shards)] + [tab] * 3,
        scratch_shapes=[pltpu.SemaphoreType.DMA((3 * nw,)) for _ in range(4)],
        compiler_params=_cparams(None, VMEM_LIMIT),
    )(*shards, pos_f, _rope_consts())


def _assemble_w_in_t(slots):
    bw = 256
    ov = WINDOW_ROWS - WINDOW_STEP

    def body(s_ref, o_ref):
        for k in range(4):
            base = k * WINDOW_STEP
            lo = 0 if k == 0 else ov
            if k > 0:
                o_ref[base:base + ov, :] = s_ref[k - 1, WINDOW_STEP:WINDOW_ROWS, :] + s_ref[k, 0:ov, :]
            hi = WINDOW_ROWS if k == 3 else WINDOW_STEP
            o_ref[base + lo:base + hi, :] = s_ref[k, lo:hi, :]

    return pl.pallas_call(
        body, name="assemble_w_in_t", grid=(D_MODEL // bw,),
        in_specs=[pl.BlockSpec((4, WINDOW_ROWS, bw), lambda i: (0, 0, i))],
        out_specs=pl.BlockSpec((D_IN, bw), lambda i: (0, i)),
        out_shape=jax.ShapeDtypeStruct((D_IN, D_MODEL), slots.dtype),
        compiler_params=_cparams(("parallel",)),
    )(slots)


def _row_block(rows):
    for cand in (976, 176, 256, 128):
        if rows % cand == 0:
            return cand
    return rows


def _pair_sum(g, r, split, c_arr, name):
    hr, hn = r.shape
    br = _row_block(hr)
    if split is None:
        g_spec = pl.BlockSpec((br, hn), lambda i, c_ref: (i, 0))
    elif split.by_rows:
        g_spec = pl.BlockSpec((br, hn), lambda i, c_ref: (i, c_ref[0]))
    else:
        g_spec = pl.BlockSpec((br, hn), lambda i, c_ref: (c_ref[0] * (hr // br) + i, 0))

    def body(c_ref, g_ref, r_ref, o_ref):
        o_ref[...] = (g_ref[...] + r_ref[...]).astype(o_ref.dtype)

    return pl.pallas_call(
        body, name=name,
        grid_spec=pltpu.PrefetchScalarGridSpec(
            num_scalar_prefetch=1, grid=(hr // br,),
            in_specs=[g_spec, pl.BlockSpec((br, hn), lambda i, c_ref: (i, 0))],
            out_specs=pl.BlockSpec((br, hn), lambda i, c_ref: (i, 0))),
        out_shape=jax.ShapeDtypeStruct(r.shape, _MX),
        compiler_params=_cparams(("parallel",)),
    )(c_arr, g, r)


def _pair_sums_whole(gs, rs, splits, c_arr):
    n = len(gs)

    def g_spec(r, split):
        at = (lambda i, c_ref: (0, c_ref[0])) if split.by_rows else (lambda i, c_ref: (c_ref[0], 0))
        return pl.BlockSpec(r.shape, at)

    def body(c_ref, *refs):
        for g_ref, r_ref, o_ref in zip(refs[:n], refs[n:2 * n], refs[2 * n:]):
            o_ref[...] = (g_ref[...] + r_ref[...]).astype(o_ref.dtype)

    whole = lambda r: pl.BlockSpec(r.shape, lambda i, c_ref: (0, 0))
    return pl.pallas_call(
        body, name="pair_sum_small",
        grid_spec=pltpu.PrefetchScalarGridSpec(
            num_scalar_prefetch=1, grid=(1,),
            in_specs=[g_spec(r, sp) for r, sp in zip(rs, splits)] + [whole(r) for r in rs],
            out_specs=[whole(r) for r in rs]),
        out_shape=[jax.ShapeDtypeStruct(r.shape, _MX) for r in rs],
        compiler_params=_cparams(("arbitrary",), VMEM_LIMIT),
    )(c_arr, *gs, *rs)


_HBM = pl.BlockSpec(memory_space=pltpu.HBM)
_SEM = pl.BlockSpec(memory_space=pltpu.SEMAPHORE)
_FLOWS = pltpu.SideEffectType.DATAFLOW_SIDE_EFFECTING


def _chip_exchange_copies(refs, send, recv, splits):
    nw = len(refs) // 2
    x, y, c = _place()
    me = 2 * x + y
    copies = []
    for i in range(nw):
        for px, py in _other_chips(x, y):
            copies.append((splits[i].chip_part(refs[i], 2 * px + py), refs[nw + i].at[me], (px, py, c)))
    return [pltpu.make_async_remote_copy(src_ref=src, dst_ref=dst, send_sem=send.at[k], recv_sem=recv.at[k],
                                         device_id=peer, device_id_type=_MESH)
            for k, (src, dst, peer) in enumerate(copies)]


def _late_gather_copies(refs, send, recv, splits):
    nw = len(refs) // 2
    x, y, c = _place()
    me = 2 * x + y
    copies = []
    for i in range(nw):
        for px, py in [(x, y)] + _other_chips(x, y):
            copies.append((refs[i], splits[i].chip_part(refs[nw + i], me), (px, py, c)))
    return [pltpu.make_async_remote_copy(src_ref=src, dst_ref=dst, send_sem=send.at[k], recv_sem=recv.at[k],
                                         device_id=peer, device_id_type=_MESH)
            for k, (src, dst, peer) in enumerate(copies)]


def _start_copies(name, flying, copies_of, n_copies, after=None):
    first = [] if after is None else [after]

    def body(*refs):
        ins = refs[:len(flying)]
        send, recv = refs[len(flying) + len(first):len(flying) + len(first) + 2]
        token = refs[-1]
        for cp in copies_of(ins, send, recv):
            cp.start()
        token[...] = jnp.zeros_like(token)

    outs = pl.pallas_call(
        body, name=name,
        in_specs=[_HBM] * len(flying) + [_ANY] * len(first),
        out_specs=[_SEM, _SEM] + [_HBM] * len(flying) + [pl.BlockSpec(memory_space=pltpu.VMEM)],
        out_shape=[pltpu.SemaphoreType.DMA((n_copies,)), pltpu.SemaphoreType.DMA((n_copies,))]
        + [pltpu.HBM(f.shape, f.dtype) for f in flying] + [jax.ShapeDtypeStruct((8, LANE), _F32)],
        input_output_aliases={i: 2 + i for i in range(len(flying))},
        compiler_params=pltpu.CompilerParams(has_side_effects=_FLOWS),
    )(*[pltpu.with_memory_space_constraint(f, pltpu.HBM) for f in flying], *first)
    return outs[0], outs[1], outs[2:2 + len(flying)], outs[-1]


def _wait_copies(name, send, recv, flying, copies_of, after):
    def body(*refs):
        ins = refs[:len(flying)]
        send_ref, recv_ref = refs[len(flying):len(flying) + 2]
        for cp in copies_of(ins, send_ref, recv_ref):
            cp.wait_send()
            cp.wait_recv()

    return pl.pallas_call(
        body, name=name,
        in_specs=[_HBM] * len(flying) + [_SEM, _SEM, _ANY],
        out_specs=[_HBM] * len(flying),
        out_shape=[pltpu.HBM(f.shape, f.dtype) for f in flying],
        input_output_aliases={i: i for i in range(len(flying))},
        compiler_params=pltpu.CompilerParams(has_side_effects=_FLOWS),
    )(*flying, send, recv, after)


def _sum_chips(qs, ps, splits, place_arr):
    n = len(qs)
    q_specs, p_specs, o_specs, out_shapes = [], [], [], []
    for q, split in zip(qs, splits):
        _, hr, hn = q.shape
        q_specs.append(pl.BlockSpec((4, hr, hn), lambda i, pr: (0, 0, 0)))
        if split.by_rows:
            out_shapes.append((hr, 2 * hn))
            o_specs.append(pl.BlockSpec((hr, hn), lambda i, pr: (0, pr[0])))
            p_specs.append(pl.BlockSpec((pl.Element(hr), pl.Element(hn)), lambda i, pr, step=split.step: (pr[1] * step, 0)))
        else:
            out_shapes.append((2 * hr, hn))
            o_specs.append(pl.BlockSpec((hr, hn), lambda i, pr: (pr[0], 0)))
            p_specs.append(pl.BlockSpec((hr, hn), lambda i, pr: (0, pr[1])))

    def body(pr, *refs):
        for q_ref, p_ref, o_ref in zip(refs[:n], refs[n:2 * n], refs[2 * n:]):
            f = lambda k: jnp.where(pr[1] == k, p_ref[...], q_ref[k]).astype(_F32)
            o_ref[...] = ((f(0) + f(1)) + f(2)) + f(3)

    return pl.pallas_call(
        body, name="chip_sums",
        grid_spec=pltpu.PrefetchScalarGridSpec(
            num_scalar_prefetch=1, grid=(1,), in_specs=q_specs + p_specs, out_specs=o_specs),
        out_shape=[jax.ShapeDtypeStruct(s, _F32) for s in out_shapes],
        compiler_params=_cparams(("arbitrary",), VMEM_LIMIT),
    )(place_arr, *qs, *ps)


def _pair_share(bufs, splits, small):
    nw = len(bufs)

    def body(*refs):
        ins, small_ref, outs, all_ref = refs[:nw], refs[nw], refs[nw + 1:2 * nw + 1], refs[2 * nw + 1]
        send, recv, s_send, s_recv = refs[2 * nw + 2:]
        x, y, c = _place()
        copies = []
        for r in range(1, 8):
            peer = (1 - x if r & 4 else x, 1 - y if r & 2 else y, 1 - c if r & 1 else c)
            cp = pltpu.make_async_remote_copy(
                src_ref=small_ref, dst_ref=all_ref.at[4 * x + 2 * y + c], send_sem=s_send.at[r - 1],
                recv_sem=s_recv.at[r - 1], device_id=peer, device_id_type=_MESH)
            cp.start()
            copies.append(cp)
        for i in range(nw):
            cp = pltpu.make_async_remote_copy(
                src_ref=splits[i].half(ins[i], c), dst_ref=splits[i].half(outs[i], c), send_sem=send.at[i],
                recv_sem=recv.at[i], device_id=(x, y, 1 - c), device_id_type=_MESH)
            cp.start()
            copies.append(cp)
        for cp in copies:
            cp.wait()

    return pl.pallas_call(
        body, name="grad_pair_share",
        in_specs=[_ANY] * (nw + 1), out_specs=[_ANY] * (nw + 1),
        out_shape=[jax.ShapeDtypeStruct(b.shape, b.dtype) for b in bufs]
        + [jax.ShapeDtypeStruct((8,) + small.shape, small.dtype)],
        input_output_aliases={i: i for i in range(nw)},
        scratch_shapes=[pltpu.SemaphoreType.DMA((nw,)), pltpu.SemaphoreType.DMA((nw,)),
                        pltpu.SemaphoreType.DMA((7,)), pltpu.SemaphoreType.DMA((7,))],
    )(*bufs, small)


def _sum_devices(parts, own, dev_arr):
    def body(dev, p_ref, own_ref, tot_ref):
        f = lambda d: jnp.where(dev[0] == d, own_ref[...], p_ref[d])
        acc = f(0)
        for d in range(1, 8):
            acc = acc + f(d)
        tot_ref[...] = acc

    return pl.pallas_call(
        body, name="small_sum",
        grid_spec=pltpu.PrefetchScalarGridSpec(
            num_scalar_prefetch=1, grid=(1,),
            in_specs=[pl.BlockSpec(parts.shape, lambda i, dev: (0, 0, 0)), pl.BlockSpec(own.shape, lambda i, dev: (0, 0))],
            out_specs=pl.BlockSpec(own.shape, lambda i, dev: (0, 0))),
        out_shape=jax.ShapeDtypeStruct(own.shape, own.dtype),
    )(dev_arr, parts, own)


def _adam_update(w, g, m, v):
    m2 = ADAM_B1 * m + (1.0 - ADAM_B1) * g
    v2 = ADAM_B2 * v + (1.0 - ADAM_B2) * (g * g)
    m_hat = m2 / (1.0 - ADAM_B1 ** ADAM_STEP)
    v_hat = v2 / (1.0 - ADAM_B2 ** ADAM_STEP)
    return -ADAM_LR * (m_hat / (jnp.sqrt(v_hat) + ADAM_EPS) + ADAM_WD * w), m2, v2


SMALL_AT = dict(g_in=(0, 0), g_final=(1, 0), g_gla_norm=(2, 0), b_alpha=(2, B_WIDTH), attn_sinks=(2, B_WIDTH + B_KEY_WIDTH))
LOSS_AT = (2, B_WIDTH + B_KEY_WIDTH + LANE)
WUP_ROWS = (3, 7)


def _adamw_small(tot, g_wup, params):
    names = list(params)

    def body(*refs):
        tot_ref, gw_ref = refs[0], refs[1]
        ins = refs[2:2 + 3 * len(names)]
        outs = refs[2 + 3 * len(names):]
        for i, nm in enumerate(names):
            w_ref, m_ref, v_ref = ins[3 * i:3 * i + 3]
            if nm in SMALL_AT:
                r, a = SMALL_AT[nm]
                g = tot_ref[r:r + 1, a:a + w_ref.shape[1]]
            else:
                g = gw_ref[...]
            d, m2, v2 = _adam_update(w_ref[...], g, m_ref[...], v_ref[...])
            for o_ref, val in zip(outs[4 * i:4 * i + 4], (g, d, m2, v2)):
                o_ref[...] = val

    vm = pl.BlockSpec(memory_space=pltpu.VMEM)
    flat = [a for nm in names for a in params[nm]]
    out_shape = [jax.ShapeDtypeStruct(params[nm][0].shape, _F32) for nm in names for _ in range(4)]
    outs = pl.pallas_call(
        body, name="adamw_small", in_specs=[vm] * (2 + len(flat)), out_specs=[vm] * len(out_shape), out_shape=out_shape,
    )(tot, g_wup, *flat)
    return {nm: tuple(outs[4 * i:4 * i + 4]) for i, nm in enumerate(names)}


def _adamw(w, g, m, v, name):
    lead = w.shape[0] != 1
    r, n = (w.shape[0], w.shape[2]) if lead else w.shape[1:]
    br = r
    for cand in (256, 244, 128):
        if r > cand and r % cand == 0:
            br = cand
            break

    def body(w_ref, g_ref, m_ref, v_ref, d_ref, nm_ref, nv_ref):
        d_ref[...], nm_ref[...], nv_ref[...] = _adam_update(w_ref[...], g_ref[...], m_ref[...], v_ref[...])

    blk = pl.BlockSpec((br, 1, n), lambda i: (i, 0, 0)) if lead else pl.BlockSpec((None, br, n), lambda i: (0, i, 0))
    shp = jax.ShapeDtypeStruct(w.shape, _F32)
    return pl.pallas_call(
        body, name=name, grid=(r // br,),
        in_specs=[blk] * 4, out_specs=[blk] * 3, out_shape=[shp] * 3,
        compiler_params=_cparams(("parallel",)),
    )(w, g, m, v)


def _adamw_whole(items):
    n = len(items)

    def body(*refs):
        for i in range(n):
            w_ref, g_ref, m_ref, v_ref = refs[4 * i:4 * i + 4]
            d_ref, nm_ref, nv_ref = refs[4 * n + 3 * i:4 * n + 3 * i + 3]
            d_ref[...], nm_ref[...], nv_ref[...] = _adam_update(w_ref[...], g_ref[...], m_ref[...], v_ref[...])

    whole = lambda a: pl.BlockSpec((None,) + a.shape[1:], lambda i: (0, 0, 0))
    outs = pl.pallas_call(
        body, name="adamw_out_weights", grid=(1,),
        in_specs=[whole(a) for it in items for a in it],
        out_specs=[whole(it[0]) for it in items for _ in range(3)],
        out_shape=[jax.ShapeDtypeStruct(it[0].shape, _F32) for it in items for _ in range(3)],
        compiler_params=_cparams(("arbitrary",), VMEM_LIMIT),
    )(*[a for it in items for a in it])
    return [tuple(outs[3 * i:3 * i + 3]) for i in range(n)]


def kernel(x, positions, g_in, w_in, w_alpha_up, b_alpha, attn_sinks, g_gla_norm, w_out_a, w_out_b, w_o, g_final, loss_target, m_g_in, m_w_in, m_w_alpha_up, m_b_alpha, m_attn_sinks, m_g_gla_norm, m_w_out_a, m_w_out_b, m_w_o, m_g_final, v_g_in, v_w_in, v_w_alpha_up, v_b_alpha, v_attn_sinks, v_g_gla_norm, v_w_out_a, v_w_out_b, v_w_o, v_g_final):
    nseq, seq, _ = x.shape
    t = nseq * seq
    cx, cy, cc = _place()
    chip = 2 * cx + cy
    c_arr = jnp.reshape(cc, (1,)).astype(jnp.int32)

    tr = lambda w: jnp.transpose(w, (2, 0, 1))
    w_in_t = tr(w_in).reshape(SHARD, D_MODEL).astype(_MX)
    pad = WINDOW_ROWS - SHARD
    window = lax.switch(chip, [lambda w, k=k: jnp.pad(w, ((4 * k, pad - 4 * k), (0, 0))) for k in range(4)], w_in_t)
    shards = [window, w_alpha_up[0].astype(_MX)]
    late = [w_out_a[0].astype(_MX), w_out_b[0].astype(_MX), w_o[0].astype(_MX)]
    late_splits = [SPLIT_W_OUT, SPLIT_W_OUT, SPLIT_W_O]
    splits = [SPLIT_W_IN_T, None]
    fulls = [(4, WINDOW_ROWS, D_MODEL), (4, B_GATE_RANK, B_KEY_WIDTH // 4)]
    pos_f = positions.astype(_F32).reshape(t, 1)
    win_g, wup_g, cos, sa, sb = _gather_weights(shards, splits, fulls, pos_f)
    late_copies = lambda refs, send, recv: _late_gather_copies(refs, send, recv, late_splits)
    late_full = [lax.empty(shape, _MX) for shape in ((A_WIDTH, D_MODEL), (B_WIDTH, D_MODEL), (D_MODEL, D_MODEL))]
    l_send, l_recv, l_flying, l_token = _start_copies("late_gather_start", late + late_full, late_copies,
                                                      4 * len(late), after=win_g)
    win_g = lax.dynamic_update_slice(win_g, window[None], (chip, 0, 0))
    wup_g = lax.dynamic_update_slice(wup_g, shards[1][None], (chip, 0, 0))
    wt = _assemble_w_in_t(win_g)
    wup = jnp.concatenate([jnp.transpose(wup_g, (1, 0, 2)).reshape(B_GATE_RANK, B_KEY_WIDTH),
                           jnp.zeros((LANE - B_GATE_RANK, B_KEY_WIDTH), _MX)], axis=0)

    x2 = x.reshape(t, D_MODEL)
    tgt = loss_target.reshape(t, D_MODEL)
    sinks = attn_sinks.reshape(A_HEADS)
    gf = g_final.reshape(1, D_MODEL)

    h, qkv, za, qkb, vb, zb, alr, ga, gb = _in_proj(x2, g_in + l_token[0, 0], wt, cos, sa, sb)
    oa = _attn_fwd(qkv, za, sinks, nseq)
    ob, oraw, sst = _gla_fwd(qkb, vb, zb, alr, wup, b_alpha, g_gla_norm, nseq)

    wa, wb, wo = _wait_copies("late_gather_wait", l_send, l_recv, l_flying, late_copies, ob)[len(late):]
    dh2, doa, dob, dga, dgb, dwa, dwb, dwo, dgf, lossv = _merge_loss(oa, ob, ga, gb, x2, tgt, wa, wb, wo, gf)

    dqkv, dza, dsink = _attn_bwd(qkv, za, doa, sinks, cos, sa, sb, nseq)
    dqkb, dvb, dzb, dalr, dwup, dba, dgn = _gla_bwd(qkb, vb, zb, alr, oraw, dob, sst, wup, b_alpha, g_gla_norm, nseq)
    dpieces = [dqkv, dza, dqkb, dvb, dzb, dalr, dga, dgb]
    gsplits = [SPLIT_W_IN_T, SPLIT_W_OUT, SPLIT_W_OUT, SPLIT_W_O]
    names = ("w_in", "w_out_a", "w_out_b", "w_o")
    dwin_mine, *from_sibling = _in_proj_bwd_w(h, dpieces, [dwa, dwb, dwo], gsplits[1:])
    pair_sums = [_pair_sum(dwin_mine, from_sibling[0], None, c_arr, "pair_sum_w_in"),
                 *_pair_sums_whole([dwa, dwb, dwo], from_sibling[1:], gsplits[1:], c_arr)]
    exchange = lambda refs, send, recv: _chip_exchange_copies(refs, send, recv, gsplits)
    lands = [lax.empty((4,) + sp.part_shape(p.shape), p.dtype) for p, sp in zip(pair_sums, gsplits)]
    send, recv, flying, token = _start_copies("grad_chip_exchange_start", pair_sums + lands, exchange, 3 * len(lands))
    grad_x2, dgin = _in_proj_bwd_x(dpieces, wt, x2, dh2, g_in + token[0, 0])
    landed = _wait_copies("grad_chip_exchange_wait", send, recv, flying, exchange, grad_x2)
    place_arr = jnp.stack([cc, chip]).astype(jnp.int32)
    reduced = _sum_chips(landed[len(lands):], landed[:len(lands)], gsplits, place_arr)
    row2 = jnp.concatenate([dgn, dba, jnp.pad(dsink[:, 0].reshape(1, A_HEADS), ((0, 0), (0, LANE - A_HEADS))),
                            jnp.pad(jnp.sum(lossv, axis=1, keepdims=True), ((0, 0), (0, LANE - 1)))], axis=1)
    small = jnp.concatenate([dgin, dgf, row2, dwup[:B_GATE_RANK].reshape(WUP_ROWS[1] - WUP_ROWS[0], D_MODEL),
                             jnp.zeros((1, D_MODEL), _F32)], axis=0)
    g_window, g_wa, g_wb, g_wo, small_parts = _pair_share(reduced, gsplits, small)
    g_win_t = lax.switch(chip, [lambda w, k=k: w[4 * k:4 * k + SHARD].reshape(SHARD, 1, D_MODEL) for k in range(4)],
                         g_window)
    dev_arr = jnp.reshape(2 * chip + cc, (1,)).astype(jnp.int32)
    tot = _sum_devices(small_parts, small, dev_arr)
    loss = tot[LOSS_AT]
    nup = B_KEY_WIDTH // 4
    g_wup = lax.dynamic_slice(tot[WUP_ROWS[0]:WUP_ROWS[1]].reshape(B_GATE_RANK, B_KEY_WIDTH), (0, chip * nup),
                              (B_GATE_RANK, nup))

    row = lambda a: a.reshape(1, -1)
    sm = _adamw_small(tot, g_wup, dict(
        g_in=(g_in, m_g_in, v_g_in), g_final=(row(g_final), row(m_g_final), row(v_g_final)),
        g_gla_norm=(g_gla_norm, m_g_gla_norm, v_g_gla_norm), b_alpha=(b_alpha, m_b_alpha, v_b_alpha),
        attn_sinks=(attn_sinks, m_attn_sinks, v_attn_sinks),
        w_alpha_up=(w_alpha_up[0], m_w_alpha_up[0], v_w_alpha_up[0])))
    sm["g_final"] = tuple(a.reshape(D_MODEL) for a in sm["g_final"])
    sm["w_alpha_up"] = tuple(a[None] for a in sm["w_alpha_up"])

    untr = lambda a: jnp.transpose(a, (1, 2, 0))
    big = dict(w_in=tuple(untr(a) for a in (g_win_t,) + tuple(_adamw(tr(w_in), g_win_t, tr(m_w_in), tr(v_w_in), "adamw_w_in"))))
    out_weights = (("w_out_a", w_out_a, g_wa, m_w_out_a, v_w_out_a),
                   ("w_out_b", w_out_b, g_wb, m_w_out_b, v_w_out_b), ("w_o", w_o, g_wo, m_w_o, v_w_o))
    updates = _adamw_whole([(w, g[None], m, v) for _, w, g, m, v in out_weights])
    for (nm, _, g, _, _), upd in zip(out_weights, updates):
        big[nm] = (g[None],) + upd

    order = ("g_in", "w_in", "w_alpha_up", "b_alpha", "attn_sinks", "g_gla_norm", "w_out_a", "w_out_b", "w_o", "g_final")
    outs = [big[nm][kind] if nm in big else sm[nm][kind] for kind in range(4) for nm in order]
    return (loss, grad_x2.reshape(x.shape), *outs)
```

```python
import math
from typing import NamedTuple

import numpy as np
import jax
import jax.numpy as jnp
from jax import lax
from jax.experimental import pallas as pl
from jax.experimental.pallas import tpu as pltpu

D_MODEL = 1024
A_HEADS, A_KV_HEADS, A_HEAD_DIM = 8, 2, 64
A_GROUP = A_HEADS // A_KV_HEADS
A_WIDTH, A_KV_WIDTH = 512, 128
BLOCK = 128
ROPE_THETA = 500000.0
ROPE_DIM = 16
B_HEADS, B_KEY_DIM, B_VAL_DIM = 4, 64, 128
B_KEY_WIDTH, B_WIDTH = 256, 512
B_GATE_RANK = 16
B_GATE_TEMP = 16.0
B_CHUNK = 64
NORM_EPS = 1e-6
NEG_BIG = -1e30
D_IN = 4880

ADAM_LR, ADAM_B1, ADAM_B2, ADAM_EPS, ADAM_WD, ADAM_STEP = 0.001, 0.9, 0.999, 1e-08, 0.01, 10

LANE = 128
ALR_AT = 2816
PIECES = (("qkv", 0, 768), ("za", 768, 1280), ("qkb", 1280, 1792), ("vb", 1792, 2304),
          ("zb", 2304, 2816), ("alr", ALR_AT, ALR_AT + LANE), ("ga", 2832, 3856), ("gb", 3856, 4880))
SHARD = D_IN // 4
WINDOW_STEP = 1216
WINDOW_ROWS = 1232

GLA_BLOCK = 256
GLA_FWD_BLOCK = 1024
GLA_BWD_TILE = 4
MERGE_SLAB = 16
VMEM_LIMIT = 56 * 1024 * 1024

_F32 = jnp.float32
_MX = jnp.bfloat16
_ST = jnp.bfloat16

_MESH = pl.DeviceIdType.MESH
_ANY = pl.BlockSpec(memory_space=pl.ANY)


def _cparams(sem=None, vmem=None):
    return pltpu.CompilerParams(dimension_semantics=sem, vmem_limit_bytes=vmem)


def _dot(a, b):
    return jnp.dot(a.astype(_MX), b.astype(_MX), preferred_element_type=_F32)


def _dot_nt(a, b):
    return lax.dot_general(a.astype(_MX), b.astype(_MX), (((1,), (1,)), ((), ())),
                           preferred_element_type=_F32)


def _dot_tn(a, b):
    return lax.dot_general(a.astype(_MX), b.astype(_MX), (((0,), (0,)), ((), ())),
                           preferred_element_type=_F32)


def _dot_ones(ones_mat, v):
    o = ones_mat.astype(jnp.bfloat16)
    v0 = v.astype(jnp.bfloat16)
    v1 = (v - v0.astype(_F32)).astype(jnp.bfloat16)
    d = lambda t: jnp.dot(o, t, preferred_element_type=_F32)
    return d(v0) + d(v1)


def _sigmoid(x):
    return 0.5 * jnp.tanh(0.5 * x) + 0.5


def _log_sigmoid(x):
    return jnp.minimum(x, 0.0) - jnp.log(1.0 + jnp.exp(-jnp.abs(x)))


def _lane_tile(t, width):
    reps = width // t.shape[1]
    return t if reps == 1 else jnp.tile(t, (1, reps))


def _rope(t, cos, sa, sb, sign):
    w = t.shape[1]
    rot = pltpu.roll(t, w - 8, 1) * _lane_tile(sa, w) + pltpu.roll(t, 8, 1) * _lane_tile(sb, w)
    return t * _lane_tile(cos, w) + sign * rot


def _rms_bwd(dy_g, n, r):
    return r * (dy_g - n * jnp.mean(dy_g * n, axis=-1, keepdims=True))


ROPE_ROWS = 256


def _rope_consts():
    lane = np.arange(LANE) % A_HEAD_DIM
    half = ROPE_DIM // 2
    inv = np.exp((np.float32(-math.log(ROPE_THETA)) * np.arange(half, dtype=np.float32)) * np.float32(2.0 / ROPE_DIM))
    consts = np.zeros((8, LANE), np.float32)
    consts[0] = np.where(lane < ROPE_DIM, inv[lane % half], 0.0)
    consts[1] = np.where(lane < half, -1.0, 0.0)
    consts[2] = np.where((lane >= half) & (lane < ROPE_DIM), 1.0, 0.0)
    return jnp.asarray(consts)


def _rope_tables_into(pos_ref, c_ref, cos_ref, sa_ref, sb_ref):
    def rows_of(b, carry):
        rows = pl.ds(pl.multiple_of(b * ROPE_ROWS, ROPE_ROWS), ROPE_ROWS)
        ang = pos_ref[rows, :] * c_ref[0:1, :]
        s = jnp.sin(ang)
        cos_ref[rows, :] = jnp.cos(ang)
        sa_ref[rows, :] = s * c_ref[1:2, :]
        sb_ref[rows, :] = s * c_ref[2:3, :]
        return carry

    lax.fori_loop(0, pos_ref.shape[0] // ROPE_ROWS, rows_of, 0)


def _in_proj(x2, g_in, wt, cos, sa, sb):
    t = x2.shape[0]
    tm = min(t, 512)

    def body(x_ref, g_ref, w_ref, cos_ref, sa_ref, sb_ref, h_ref, qkv_ref, za_ref, qkb_ref,
             vb_ref, zb_ref, alr_ref, ga_ref, gb_ref):
        xv = x_ref[...]
        r = lax.rsqrt(jnp.mean(xv * xv, axis=-1, keepdims=True) + NORM_EPS)
        h = (xv * r * g_ref[...]).astype(_MX)
        h_ref[...] = h.astype(_ST)
        outs = dict(za=za_ref, qkb=qkb_ref, vb=vb_ref, zb=zb_ref, alr=alr_ref, ga=ga_ref, gb=gb_ref)
        for name, a, b in PIECES:
            p = _dot_nt(h, w_ref[a:b, :])
            if name == "qkv":
                c, s1, s2 = cos_ref[...], sa_ref[...], sb_ref[...]
                qkv_ref[:, 0:512] = _rope(p[:, 0:512], c, s1, s2, 1.0).astype(_ST)
                qkv_ref[:, 512:640] = _rope(p[:, 512:640], c, s1, s2, 1.0).astype(_ST)
                qkv_ref[:, 640:768] = p[:, 640:768].astype(_ST)
            else:
                outs[name][...] = p.astype(outs[name].dtype)

    rows = lambda w: pl.BlockSpec((tm, w), lambda i: (i, 0))
    shp = lambda name, w: jax.ShapeDtypeStruct((t, w), _F32 if name == "qkb" else _ST)
    widths = [D_MODEL] + [b - a for _, a, b in PIECES]
    return pl.pallas_call(
        body, name="in_proj", grid=(t // tm,),
        in_specs=[rows(D_MODEL), pl.BlockSpec((1, D_MODEL), lambda i: (0, 0)),
                  pl.BlockSpec((D_IN, D_MODEL), lambda i: (0, 0), pipeline_mode=pl.Buffered(1)),
                  rows(LANE), rows(LANE), rows(LANE)],
        out_specs=[rows(w) for w in widths],
        out_shape=[shp(n, w) for n, w in zip(["h"] + [p[0] for p in PIECES], widths)],
        compiler_params=_cparams(("parallel",), VMEM_LIMIT),
    )(x2, g_in, wt, cos, sa, sb)


def _attn_operands(k_prev, k_cur, v_prev, v_cur, want_bwd):
    kf = jnp.concatenate([k_prev, k_cur], axis=0).astype(_F32) * (A_HEAD_DIM ** -0.5)
    vf = jnp.concatenate([v_prev, v_cur], axis=0).astype(_F32)
    lo = lax.broadcasted_iota(jnp.int32, (1, LANE), 1) < 64

    def on_lanes(a):
        sw = pltpu.roll(a, 64, 1)
        z = jnp.zeros_like(a)
        return [[jnp.where(lo, a, z).astype(_MX), jnp.where(lo, z, sw).astype(_MX)],
                [jnp.where(lo, sw, z).astype(_MX), jnp.where(lo, z, a).astype(_MX)]]

    def on_rows(a):
        at = a.T.astype(_MX)
        z = jnp.zeros((64, at.shape[1]), _MX)
        top, bot = at[0:64], at[64:128]
        return [[jnp.concatenate([top, z], axis=0), jnp.concatenate([z, top], axis=0)],
                [jnp.concatenate([bot, z], axis=0), jnp.concatenate([z, bot], axis=0)]]

    ops = dict(k_lanes=on_lanes(kf), v_rows=on_rows(vf), lo=lo)
    if want_bwd:
        ops.update(v_lanes=on_lanes(vf), k_rows=on_rows(kf))
    return ops


def _attn_valid(n):
    kj = lax.broadcasted_iota(jnp.int32, (2 * BLOCK, 2 * BLOCK), 0) - BLOCK
    qi = lax.broadcasted_iota(jnp.int32, (2 * BLOCK, 2 * BLOCK), 1) & (BLOCK - 1)
    return (kj <= qi) & (qi - kj < BLOCK) & ((n > 0) | (kj >= 0))


def _attn_sinks(sink_ref, h_a, h_b):
    first = lax.broadcasted_iota(jnp.int32, (1, 2 * BLOCK), 1) < BLOCK
    return jnp.where(first, sink_ref[h_a], sink_ref[h_b])


def _attn_softmax_t(k_lanes, q_pair, valid, sink):
    s = jnp.where(valid, _dot_nt(k_lanes, q_pair), NEG_BIG)
    m = jnp.maximum(jnp.max(s, axis=0, keepdims=True), sink)
    e = jnp.exp(s - m)
    e_sink = jnp.exp(sink - m)
    inv = 1.0 / (jnp.sum(e, axis=0, keepdims=True) + e_sink)
    return e, e_sink, inv


ATTN_TILE = 8


def _attn_kv(qkv_ref, kvp_ref, j):
    rows = slice(j * BLOCK, (j + 1) * BLOCK)
    if j == 0:
        k_prev, v_prev = kvp_ref[:, 0:128], kvp_ref[:, 128:256]
    else:
        before = slice((j - 1) * BLOCK, j * BLOCK)
        k_prev, v_prev = qkv_ref[before, 512:640], qkv_ref[before, 640:768]
    return k_prev, qkv_ref[rows, 512:640], v_prev, qkv_ref[rows, 640:768]


def _attn_fwd(qkv, za, sinks, nseq):
    t = qkv.shape[0]
    nblk = min(ATTN_TILE, t // nseq // BLOCK)
    tile = nblk * BLOCK
    nt = t // nseq // tile

    def body(sink_ref, qkv_ref, kvp_ref, za_ref, oa_ref):
        for j in range(nblk):
            rows = slice(j * BLOCK, (j + 1) * BLOCK)
            ops = _attn_operands(*_attn_kv(qkv_ref, kvp_ref, j), False)
            valid = _attn_valid(nblk * pl.program_id(1) + j)[:, 0:BLOCK]
            for pr in range(A_HEADS // 2):
                lanes = slice(pr * LANE, (pr + 1) * LANE)
                g = pr // (A_GROUP // 2)
                q_pair = qkv_ref[rows, lanes]
                ot = None
                for half in range(2):
                    e, _, inv = _attn_softmax_t(ops["k_lanes"][g][half], q_pair, valid, sink_ref[2 * pr + half])
                    part = _dot(ops["v_rows"][g][half], e) * inv
                    ot = part if ot is None else ot + part
                z = za_ref[rows, lanes].astype(_F32)
                oa_ref[rows, lanes] = (ot.T * (z * _sigmoid(z))).astype(_ST)

    cur = lambda w: pl.BlockSpec((tile, w), lambda s, n: (s * nt + n, 0))
    return pl.pallas_call(
        body, name="attn_fwd", grid=(nseq, nt),
        in_specs=[pl.BlockSpec(memory_space=pltpu.SMEM), cur(768),
                  pl.BlockSpec((BLOCK, 256), lambda s, n: (nblk * (s * nt + n) - jnp.minimum(n, 1), 2)),
                  cur(512)],
        out_specs=cur(512), out_shape=jax.ShapeDtypeStruct((t, A_WIDTH), _ST),
        compiler_params=_cparams(("parallel", "arbitrary")),
    )(sinks, qkv, qkv, za)


def _attn_bwd(qkv, za, doa, sinks, cos, sa, sb, nseq):
    t = qkv.shape[0]
    nblk = min(ATTN_TILE, t // nseq // BLOCK)
    tile = nblk * BLOCK
    nt = t // nseq // tile

    def body(sink_ref, qkv_ref, kvp_ref, za_ref, doa_ref, cos_ref, sa_ref, sb_ref,
             dqkv_ref, dza_ref, dsink_ref, ck_ref, cv_ref):
        s_id, i = pl.program_id(0), pl.program_id(1)

        @pl.when((s_id == 0) & (i == 0))
        def _():
            dsink_ref[...] = jnp.zeros_like(dsink_ref)

        @pl.when(i == 0)
        def _():
            ck_ref[...] = jnp.zeros_like(ck_ref)
            cv_ref[...] = jnp.zeros_like(cv_ref)

        carry_k, carry_v = ck_ref[...], cv_ref[...]
        for j in reversed(range(nblk)):
            rows = slice(j * BLOCK, (j + 1) * BLOCK)
            ops = _attn_operands(*_attn_kv(qkv_ref, kvp_ref, j), True)
            lo = ops["lo"]
            valid = _attn_valid(nblk * (nt - 1 - i) + j)
            dk_acc, dv_acc, dq_pairs = [], [], []
            for g in range(A_KV_HEADS):
                pairs = [slice((2 * g + p) * LANE, (2 * g + p + 1) * LANE) for p in range(2)]
                q_both = jnp.concatenate([qkv_ref[rows, p] for p in pairs], axis=0)
                q_f = q_both.astype(_F32)
                z = [za_ref[rows, p].astype(_F32) for p in pairs]
                sz = [_sigmoid(v) for v in z]
                d_oa = [doa_ref[rows, p].astype(_F32) for p in pairs]
                d_att = jnp.concatenate([d_oa[p] * (z[p] * sz[p]) for p in range(2)], axis=0)
                zero = jnp.zeros_like(d_att)
                ot, dqt, ds_all, pn_all, qz_all, daz_all = None, None, [], [], [], []
                for half in range(2):
                    heads = (4 * g + half, 4 * g + 2 + half)
                    e, e_sink, inv = _attn_softmax_t(ops["k_lanes"][g][half], q_both, valid,
                                                     _attn_sinks(sink_ref, *heads))
                    pn = e * inv
                    dpt = _dot_nt(ops["v_lanes"][g][half], d_att)
                    delta = jnp.sum(pn * dpt, axis=0, keepdims=True)
                    ds = (pn * (dpt - delta)).astype(_MX)
                    pn = pn.astype(_MX)
                    d_sink = e_sink * inv * delta
                    for p, h in enumerate(heads):
                        dsink_ref[h:h + 1, :] = dsink_ref[h:h + 1, :] - jnp.sum(d_sink[:, p * BLOCK:(p + 1) * BLOCK])
                    o_part = _dot(ops["v_rows"][g][half], pn)
                    dq_part = _dot(ops["k_rows"][g][half], ds)
                    ot = o_part if ot is None else ot + o_part
                    dqt = dq_part if dqt is None else dqt + dq_part
                    mine = lo if half == 0 else jnp.logical_not(lo)
                    ds_all.append(ds)
                    pn_all.append(pn)
                    qz_all.append(jnp.where(mine, q_f, zero).astype(_MX))
                    daz_all.append(jnp.where(mine, d_att, zero).astype(_MX))
                dk_acc.append(_dot(jnp.concatenate(ds_all, axis=1), jnp.concatenate(qz_all, axis=0)))
                dv_acc.append(_dot(jnp.concatenate(pn_all, axis=1), jnp.concatenate(daz_all, axis=0)))
                for p, lanes in enumerate(pairs):
                    cols = slice(p * BLOCK, (p + 1) * BLOCK)
                    dza_ref[rows, lanes] = (d_oa[p] * ot[:, cols].T * (sz[p] * (1.0 + z[p] * (1.0 - sz[p])))).astype(_ST)
                    dq_pairs.append(dqt[:, cols].T)

            def fold(acc, scale):
                both = [a + pltpu.roll(a, 64, 1) for a in acc]
                return jnp.where(lo, both[0], both[1]) * scale

            dk_full = fold(dk_acc, A_HEAD_DIM ** -0.5)
            dv_full = fold(dv_acc, 1.0)
            dk_cur, dv_cur = dk_full[BLOCK:] + carry_k, dv_full[BLOCK:] + carry_v
            carry_k, carry_v = dk_full[:BLOCK], dv_full[:BLOCK]
            c, s1, s2 = cos_ref[rows, :], sa_ref[rows, :], sb_ref[rows, :]
            dqkv_ref[rows, 0:512] = _rope(jnp.concatenate(dq_pairs, axis=1), c, s1, s2, -1.0).astype(_ST)
            dqkv_ref[rows, 512:640] = _rope(dk_cur, c, s1, s2, -1.0).astype(_ST)
            dqkv_ref[rows, 640:768] = dv_cur.astype(_ST)
        ck_ref[...] = carry_k
        cv_ref[...] = carry_v

    cur = lambda w: pl.BlockSpec((tile, w), lambda s, i: (s * nt + nt - 1 - i, 0))
    return pl.pallas_call(
        body, name="attn_bwd", grid=(nseq, nt),
        in_specs=[pl.BlockSpec(memory_space=pltpu.SMEM), cur(768),
                  pl.BlockSpec((BLOCK, 256),
                               lambda s, i: (nblk * (s * nt + nt - 1 - i) - jnp.minimum(nt - 1 - i, 1), 2)),
                  cur(512), cur(512), cur(LANE), cur(LANE), cur(LANE)],
        out_specs=[cur(768), cur(512), pl.BlockSpec((8, LANE), lambda s, i: (0, 0))],
        out_shape=[jax.ShapeDtypeStruct((t, 768), _ST), jax.ShapeDtypeStruct((t, 512), _ST),
                   jax.ShapeDtypeStruct((8, LANE), _F32)],
        scratch_shapes=[pltpu.VMEM((BLOCK, A_KV_WIDTH), _F32), pltpu.VMEM((BLOCK, A_KV_WIDTH), _F32)],
        compiler_params=_cparams(("arbitrary", "arbitrary")),
    )(sinks, qkv, qkv, za, doa, cos, sa, sb)


def _gla_chunk_terms(la, qkb_ref, r0):
    g = la[r0:r0 + B_CHUNK, :]
    ri = lax.broadcasted_iota(jnp.int32, (B_CHUNK, B_CHUNK), 0)
    ci = lax.broadcasted_iota(jnp.int32, (B_CHUNK, B_CHUNK), 1)
    cum = _dot_ones((ri >= ci).astype(_F32), g)
    last = cum[B_CHUNK - 1:B_CHUNK, :]
    mid = cum[B_CHUNK // 2 - 1:B_CHUNK // 2, :]
    q = qkb_ref[r0:r0 + B_CHUNK, 0:B_KEY_WIDTH].astype(_F32) * (B_KEY_DIM ** -0.5)
    k = qkb_ref[r0:r0 + B_CHUNK, B_KEY_WIDTH:2 * B_KEY_WIDTH].astype(_F32)
    e_q, e_k, e_l, e_c = jnp.exp(cum - mid), jnp.exp(mid - cum), jnp.exp(last - cum), jnp.exp(cum)
    dec_col = jnp.exp(jnp.sum(g.T, axis=1, keepdims=True))
    return dict(qm=q * e_q, km=k * e_k, kl=k * e_l, qc=q * e_c, e_q=e_q, e_k=e_k, e_l=e_l, e_c=e_c,
                dec_col=dec_col, dec_row=jnp.exp(last), causal=ri >= ci, ri=ri)


def _gate_logits(alr_ref, wup_ref, b_ref):
    return _dot(alr_ref[...], wup_ref[...]) + b_ref[...]


def _gla_fwd(qkb, vb, zb, alr, wup, b_alpha, gn, nseq):
    t = qkb.shape[0]
    tb = min(GLA_FWD_BLOCK, t // nseq)
    nblk = t // nseq // tb
    cpb = tb // B_CHUNK

    def body(qkb_ref, vb_ref, zb_ref, alr_ref, wup_ref, b_ref, gn_ref, ob_ref, oraw_ref, sst_ref, s_ref):
        @pl.when(pl.program_id(1) == 0)
        def _():
            s_ref[...] = jnp.zeros_like(s_ref)

        la = _log_sigmoid(_gate_logits(alr_ref, wup_ref, b_ref)) * (1.0 / B_GATE_TEMP)
        terms = [_gla_chunk_terms(la, qkb_ref, c * B_CHUNK) for c in range(cpb)]
        o_intra, inc = {}, {}
        for c, tm in enumerate(terms):
            for h in range(B_HEADS):
                kl_, vl_ = slice(h * 64, (h + 1) * 64), slice(h * 128, (h + 1) * 128)
                v = vb_ref[c * B_CHUNK:(c + 1) * B_CHUNK, vl_]
                a = jnp.where(tm["causal"], _dot_nt(tm["qm"][:, kl_], tm["km"][:, kl_]), 0.0)
                o_intra[c, h] = _dot(a, v)
                inc[c, h] = _dot_tn(tm["kl"][:, kl_], v)
        o_heads = {}
        for h in range(B_HEADS):
            kl_ = slice(h * 64, (h + 1) * 64)
            st = s_ref[kl_, :]
            for c, tm in enumerate(terms):
                sst_ref[c, kl_, :] = st
                o_heads[c, h] = o_intra[c, h] + _dot(tm["qc"][:, kl_], st)
                st = tm["dec_col"][kl_, :] * st + inc[c, h]
            s_ref[kl_, :] = st
        o = jnp.concatenate([jnp.concatenate([o_heads[c, h] for h in range(B_HEADS)], axis=1)
                             for c in range(cpb)], axis=0)
        oraw_ref[...] = o
        z = zb_ref[...].astype(_F32)
        gate = z * _sigmoid(z)
        for h in range(B_HEADS):
            vl_ = slice(h * 128, (h + 1) * 128)
            oh = o[:, vl_]
            r = lax.rsqrt(jnp.mean(oh * oh, axis=-1, keepdims=True) + NORM_EPS)
            ob_ref[:, vl_] = ((oh * r) * gn_ref[:, vl_] * gate[:, vl_]).astype(_ST)

    rows = lambda w: pl.BlockSpec((tb, w), lambda s, i: (s * nblk + i, 0))
    full = lambda a, b: pl.BlockSpec((a, b), lambda s, i: (0, 0))
    return pl.pallas_call(
        body, name="gla_fwd", grid=(nseq, nblk),
        in_specs=[rows(512), rows(512), rows(512), rows(LANE), full(LANE, B_KEY_WIDTH),
                  full(1, B_KEY_WIDTH), full(1, B_WIDTH)],
        out_specs=[rows(512), rows(512),
                   pl.BlockSpec((cpb, B_KEY_WIDTH, B_VAL_DIM), lambda s, i: (s * nblk + i, 0, 0))],
        out_shape=[jax.ShapeDtypeStruct((t, B_WIDTH), _ST), jax.ShapeDtypeStruct((t, B_WIDTH), _F32),
                   jax.ShapeDtypeStruct((t // B_CHUNK, B_KEY_WIDTH, B_VAL_DIM), _F32)],
        scratch_shapes=[pltpu.VMEM((B_KEY_WIDTH, B_VAL_DIM), _F32)],
        compiler_params=_cparams(("parallel", "arbitrary")),
    )(qkb, vb, zb, alr, wup, b_alpha, gn)


def _gla_bwd(qkb, vb, zb, alr, oraw, dob, sst, wup, b_alpha, gn, nseq):
    t = qkb.shape[0]
    tb = min(GLA_BLOCK, t // nseq)
    tile = min(GLA_BWD_TILE, t // nseq // tb)
    nstep = t // nseq // (tile * tb)
    cpb = tb // B_CHUNK

    def body(qkb_ref, vb_ref, zb_ref, alr_ref, oraw_ref, dob_ref, sst_ref, wup_ref, b_ref, gn_ref,
             dqkb_ref, dvb_ref, dzb_ref, dalr_ref, dwup_ref, db_ref, dgn_ref, ds_ref):
        s_id, i = pl.program_id(0), pl.program_id(1)

        @pl.when((s_id == 0) & (i == 0))
        def _():
            dwup_ref[...] = jnp.zeros_like(dwup_ref)
            db_ref[...] = jnp.zeros_like(db_ref)
            dgn_ref[...] = jnp.zeros_like(dgn_ref)

        @pl.when(i == 0)
        def _():
            ds_ref[...] = jnp.zeros_like(ds_ref)

        for sb in reversed(range(tile)):
            one_block(sb, qkb_ref, vb_ref, zb_ref, alr_ref, oraw_ref, dob_ref, sst_ref, wup_ref, b_ref, gn_ref,
                      dqkb_ref, dvb_ref, dzb_ref, dalr_ref, dwup_ref, db_ref, dgn_ref, ds_ref)

    def one_block(sb, qkb_ref, vb_ref, zb_ref, alr_ref, oraw_ref, dob_ref, sst_ref, wup_ref, b_ref, gn_ref,
                  dqkb_ref, dvb_ref, dzb_ref, dalr_ref, dwup_ref, db_ref, dgn_ref, ds_ref):
        base = sb * tb
        rows = slice(base, base + tb)
        a_pre = _dot(alr_ref[rows, :], wup_ref[...]) + b_ref[...]
        la = _log_sigmoid(a_pre) * (1.0 / B_GATE_TEMP)

        z = zb_ref[rows, :].astype(_F32)
        sz = _sigmoid(z)
        d_ob = dob_ref[rows, :].astype(_F32)
        tg = d_ob * (z * sz)
        dsilu = sz * (1.0 + z * (1.0 - sz))
        do_cols, dgn_cols = [], []
        for h in range(B_HEADS):
            vl_ = slice(h * 128, (h + 1) * 128)
            oh = oraw_ref[rows, vl_].astype(_F32)
            r = lax.rsqrt(jnp.mean(oh * oh, axis=-1, keepdims=True) + NORM_EPS)
            on = oh * r
            gnh = gn_ref[:, vl_]
            dzb_ref[rows, vl_] = (d_ob[:, vl_] * (on * gnh) * dsilu[:, vl_]).astype(_ST)
            dgn_cols.append(jnp.sum(tg[:, vl_] * on, axis=0, keepdims=True))
            do_cols.append(_rms_bwd(tg[:, vl_] * gnh, on, r))
        dgn_ref[...] = dgn_ref[...] + jnp.concatenate(dgn_cols, axis=1)
        d_o = jnp.concatenate(do_cols, axis=1)

        ri = lax.broadcasted_iota(jnp.int32, (tb, tb), 0)
        ci = lax.broadcasted_iota(jnp.int32, (tb, tb), 1)
        same = (ri // B_CHUNK) == (ci // B_CHUNK)
        low = same & (ri >= ci)
        cum = _dot_ones(low.astype(_F32), la)
        at_row = lambda r: jnp.concatenate([jnp.broadcast_to(cum[c * B_CHUNK + r:c * B_CHUNK + r + 1], (B_CHUNK, B_KEY_WIDTH))
                                            for c in range(cpb)], axis=0)
        last, mid = at_row(B_CHUNK - 1), at_row(B_CHUNK // 2 - 1)
        e_q, e_k, e_l, e_c = jnp.exp(cum - mid), jnp.exp(mid - cum), jnp.exp(last - cum), jnp.exp(cum)
        q = qkb_ref[rows, 0:B_KEY_WIDTH] * (B_KEY_DIM ** -0.5)
        k = qkb_ref[rows, B_KEY_WIDTH:2 * B_KEY_WIDTH]
        qm, km, kl, qc = q * e_q, k * e_k, k * e_l, q * e_c
        lane_head = lax.broadcasted_iota(jnp.int32, (1, B_KEY_WIDTH), 1) // B_KEY_DIM
        d_o_mx = d_o.astype(_MX)

        def on_diagonal(st):
            z = jnp.zeros((B_KEY_DIM, B_VAL_DIM), st.dtype)
            return jnp.concatenate([jnp.concatenate(
                [st[h * B_KEY_DIM:(h + 1) * B_KEY_DIM] if g == h else z for g in range(B_HEADS)], axis=1)
                for h in range(B_HEADS)], axis=0)

        def diagonal_of(full):
            return jnp.concatenate([full[h * B_KEY_DIM:(h + 1) * B_KEY_DIM, h * B_VAL_DIM:(h + 1) * B_VAL_DIM]
                                    for h in range(B_HEADS)], axis=0)

        dqm, dkm, dv_cols = None, None, []
        for h in range(B_HEADS):
            vl_ = slice(h * B_VAL_DIM, (h + 1) * B_VAL_DIM)
            mine = lane_head == h
            qz, kz = jnp.where(mine, qm, 0.0).astype(_MX), jnp.where(mine, km, 0.0).astype(_MX)
            a = jnp.where(low, _dot_nt(qz, kz), 0.0).astype(_MX)
            da = jnp.where(low, _dot_nt(d_o_mx[:, vl_], vb_ref[rows, vl_]), 0.0).astype(_MX)
            dqm_h, dkm_h = _dot(da, kz), _dot_tn(da, qz)
            dqm = dqm_h if dqm is None else dqm + dqm_h
            dkm = dkm_h if dkm is None else dkm + dkm_h
            dv_cols.append(_dot_tn(a, d_o_mx[:, vl_]))
        dv = jnp.concatenate(dv_cols, axis=1)

        chunk = [slice(c * B_CHUNK, (c + 1) * B_CHUNK) for c in range(cpb)]
        dqc_rows, g_loc = [], []
        for c in range(cpb):
            dqc_rows.append(_dot_nt(d_o_mx[chunk[c]], on_diagonal(sst_ref[sb * cpb + c].astype(_MX))))
            g_loc.append(diagonal_of(_dot_tn(qc[chunk[c]], d_o_mx[chunk[c]])))
        cur = ds_ref[...]
        d_state = [None] * cpb
        for c in reversed(range(cpb)):
            d_state[c] = cur
            cur = g_loc[c] + jnp.exp(jnp.sum(la[chunk[c]].T, axis=1, keepdims=True)) * cur
        ds_ref[...] = cur
        dkl_rows, dv_rows, dlast_rows = [], [], []
        ones8 = jnp.ones((8, B_VAL_DIM), _F32)
        for c in range(cpb):
            dsd = on_diagonal(d_state[c].astype(_MX))
            dkl_c = _dot_nt(vb_ref[base + c * B_CHUNK:base + (c + 1) * B_CHUNK, :], dsd)
            dkl_rows.append(dkl_c)
            dv_rows.append(_dot(kl[chunk[c]], dsd))
            prod = d_state[c] * sst_ref[sb * cpb + c]
            p0 = prod.astype(jnp.bfloat16)
            p1 = (prod - p0.astype(_F32)).astype(jnp.bfloat16)
            ddec = (_dot_nt(ones8, p0) + _dot_nt(ones8, p1))[0:1]
            r_last = c * B_CHUNK + B_CHUNK - 1
            dlast = jnp.sum(dkl_c * kl[chunk[c]], axis=0, keepdims=True) + ddec * jnp.exp(last[r_last:r_last + 1])
            dlast_rows.append(jnp.broadcast_to(dlast, (B_CHUNK, B_KEY_WIDTH)))
        dqc, dkl = jnp.concatenate(dqc_rows, axis=0), jnp.concatenate(dkl_rows, axis=0)
        dqkb_ref[rows, 0:B_KEY_WIDTH] = ((dqm * e_q + dqc * e_c) * (B_KEY_DIM ** -0.5)).astype(_ST)
        dqkb_ref[rows, B_KEY_WIDTH:2 * B_KEY_WIDTH] = (dkm * e_k + dkl * e_l).astype(_ST)
        dvb_ref[rows, :] = (dv + jnp.concatenate(dv_rows, axis=0)).astype(_ST)
        dcum = dqm * qm - dkm * km + dqc * qc - dkl * kl
        row = lax.broadcasted_iota(jnp.int32, (tb, B_KEY_WIDTH), 0)
        dcum = jnp.where(row % B_CHUNK == B_CHUNK - 1, dcum + jnp.concatenate(dlast_rows, axis=0), dcum)
        dla = _dot_ones((same & (ri <= ci)).astype(_F32), dcum)

        da_pre = dla * (1.0 / B_GATE_TEMP) * (1.0 - _sigmoid(a_pre))
        dalr_ref[rows, :] = _dot_nt(da_pre, wup_ref[...]).astype(_ST)
        dwup_ref[...] = dwup_ref[...] + _dot_tn(alr_ref[rows, :], da_pre)
        db_ref[...] = db_ref[...] + jnp.sum(da_pre, axis=0, keepdims=True)

    blk = lambda s, i: s * nstep + nstep - 1 - i
    rows = lambda w: pl.BlockSpec((tile * tb, w), lambda s, i: (blk(s, i), 0))
    full = lambda a, b: pl.BlockSpec((a, b), lambda s, i: (0, 0))
    act = lambda w: jax.ShapeDtypeStruct((t, w), _ST)
    return pl.pallas_call(
        body, name="gla_bwd", grid=(nseq, nstep),
        in_specs=[rows(512), rows(512), rows(512), rows(LANE), rows(512), rows(512),
                  pl.BlockSpec((tile * cpb, B_KEY_WIDTH, B_VAL_DIM), lambda s, i: (blk(s, i), 0, 0)),
                  full(LANE, B_KEY_WIDTH), full(1, B_KEY_WIDTH), full(1, B_WIDTH)],
        out_specs=[rows(512), rows(512), rows(512), rows(LANE), full(LANE, B_KEY_WIDTH),
                   full(1, B_KEY_WIDTH), full(1, B_WIDTH)],
        out_shape=[act(512), act(512), act(512), act(LANE),
                   jax.ShapeDtypeStruct((LANE, B_KEY_WIDTH), _F32),
                   jax.ShapeDtypeStruct((1, B_KEY_WIDTH), _F32), jax.ShapeDtypeStruct((1, B_WIDTH), _F32)],
        scratch_shapes=[pltpu.VMEM((B_KEY_WIDTH, B_VAL_DIM), _F32)],
        compiler_params=_cparams(("arbitrary", "arbitrary")),
    )(qkb, vb, zb, alr, oraw, dob, sst, wup, b_alpha, gn)


def _merge_loss(oa, ob, ga, gb, x2, tgt, wa, wb, wo, g_final):
    t = x2.shape[0]
    tm = min(t, 512)
    nt = t // tm

    def body(oa_ref, ob_ref, ga_ref, gb_ref, x_ref, t_ref, wa_ref, wb_ref, wo_ref, gf_ref,
             dh_ref, doa_ref, dob_ref, dga_ref, dgb_ref, dwa_ref, dwb_ref, dwo_ref, dgf_ref, loss_ref,
             ya_s, yb_s, out_s, dmer_s, mrg_s, dya_s, dyb_s):
        first = pl.program_id(0) == 0
        so_far = lambda ref: jnp.where(first, 0.0, ref[...])

        slabs = [slice(s, s + MERGE_SLAB) for s in range(0, tm, MERGE_SLAB)]
        fold = lambda a: a[0:8] + a[8:16]
        ya_s[...] = _dot(oa_ref[...], wa_ref[...])
        yb_s[...] = _dot(ob_ref[...], wb_ref[...])
        for rows_ in slabs:
            sga, sgb = _sigmoid(ga_ref[rows_, :].astype(_F32)), _sigmoid(gb_ref[rows_, :].astype(_F32))
            mrg_s[rows_, :] = (sga * ya_s[rows_, :] + sgb * yb_s[rows_, :]).astype(_MX)
        out_s[...] = x_ref[...] + _dot(mrg_s[...], wo_ref[...])
        gf = gf_ref[...]
        loss8 = jnp.zeros((8, D_MODEL), _F32)
        dgf8 = jnp.zeros((8, D_MODEL), _F32)
        for rows_ in slabs:
            out = out_s[rows_, :]
            r = lax.rsqrt(jnp.mean(out * out, axis=-1, keepdims=True) + NORM_EPS)
            nrm = out * r
            err = nrm * gf - t_ref[rows_, :]
            loss8 = loss8 + fold(err * err)
            dy = err * (1.0 / D_MODEL)
            dgf8 = dgf8 + fold(dy * nrm)
            dh = _rms_bwd(dy * gf, nrm, r)
            dh_ref[rows_, :] = dh.astype(_ST)
        loss_ref[...] = so_far(loss_ref) + (0.5 / D_MODEL) * jnp.sum(loss8, axis=0, keepdims=True)
        dgf_ref[...] = so_far(dgf_ref) + jnp.sum(dgf8, axis=0, keepdims=True)
        dmer_s[...] = _dot_nt(dh_ref[...], wo_ref[...])
        dwo_ref[...] = so_far(dwo_ref) + _dot_tn(mrg_s[...], dh_ref[...])
        for rows_ in slabs:
            sga, sgb = _sigmoid(ga_ref[rows_, :].astype(_F32)), _sigmoid(gb_ref[rows_, :].astype(_F32))
            dmer = dmer_s[rows_, :]
            da, db = dmer * sga, dmer * sgb
            dya_s[rows_, :] = da.astype(_MX)
            dyb_s[rows_, :] = db.astype(_MX)
            dga_ref[rows_, :] = (da * ya_s[rows_, :] * (1.0 - sga)).astype(_ST)
            dgb_ref[rows_, :] = (db * yb_s[rows_, :] * (1.0 - sgb)).astype(_ST)
        doa_ref[...] = _dot_nt(dya_s[...], wa_ref[...]).astype(_ST)
        dob_ref[...] = _dot_nt(dyb_s[...], wb_ref[...]).astype(_ST)
        dwa_ref[...] = so_far(dwa_ref) + _dot_tn(oa_ref[...], dya_s[...])
        dwb_ref[...] = so_far(dwb_ref) + _dot_tn(ob_ref[...], dyb_s[...])

    rows = lambda w: pl.BlockSpec((tm, w), lambda i: (i, 0))
    full = lambda a, b: pl.BlockSpec((a, b), lambda i: (0, 0), pipeline_mode=pl.Buffered(1))
    return pl.pallas_call(
        body, name="merge_loss", grid=(nt,),
        in_specs=[rows(512), rows(512), rows(D_MODEL), rows(D_MODEL), rows(D_MODEL), rows(D_MODEL),
                  full(A_WIDTH, D_MODEL), full(B_WIDTH, D_MODEL), full(D_MODEL, D_MODEL), full(1, D_MODEL)],
        out_specs=[rows(D_MODEL), rows(512), rows(512), rows(D_MODEL), rows(D_MODEL),
                   full(A_WIDTH, D_MODEL), full(B_WIDTH, D_MODEL), full(D_MODEL, D_MODEL),
                   full(1, D_MODEL), full(1, D_MODEL)],
        out_shape=[jax.ShapeDtypeStruct((t, D_MODEL), _ST), jax.ShapeDtypeStruct((t, 512), _ST),
                   jax.ShapeDtypeStruct((t, 512), _ST), jax.ShapeDtypeStruct((t, D_MODEL), _ST),
                   jax.ShapeDtypeStruct((t, D_MODEL), _ST),
                   jax.ShapeDtypeStruct((A_WIDTH, D_MODEL), _F32), jax.ShapeDtypeStruct((B_WIDTH, D_MODEL), _F32),
                   jax.ShapeDtypeStruct((D_MODEL, D_MODEL), _F32), jax.ShapeDtypeStruct((1, D_MODEL), _F32),
                   jax.ShapeDtypeStruct((1, D_MODEL), _F32)],
        scratch_shapes=[pltpu.VMEM((tm, D_MODEL), _F32)] * 4 + [pltpu.VMEM((tm, D_MODEL), _MX)] * 3,
        compiler_params=_cparams(("arbitrary",), VMEM_LIMIT),
    )(oa, ob, ga, gb, x2, tgt, wa, wb, wo, g_final)


def _in_proj_bwd_x(dpieces, wt, x2, dh2, g_in):
    t = x2.shape[0]
    tm = min(t, 512)
    np_ = len(PIECES)

    def body(*refs):
        dp_refs = refs[:np_]
        w_ref, x_ref, dh2_ref, g_ref, gx_ref, dg_ref = refs[np_:]

        @pl.when(pl.program_id(0) == 0)
        def _():
            dg_ref[...] = jnp.zeros_like(dg_ref)

        dh = None
        for (name, a, b), dp in zip(PIECES, dp_refs):
            part = _dot(dp[...], w_ref[a:b, :])
            dh = part if dh is None else dh + part
        xv = x_ref[...]
        r = lax.rsqrt(jnp.mean(xv * xv, axis=-1, keepdims=True) + NORM_EPS)
        nrm = xv * r
        dg_ref[...] = dg_ref[...] + jnp.sum(dh * nrm, axis=0, keepdims=True)
        gx_ref[...] = dh2_ref[...].astype(_F32) + _rms_bwd(dh * g_ref[...], nrm, r)

    rows = lambda w: pl.BlockSpec((tm, w), lambda i: (i, 0))
    full = lambda a, b: pl.BlockSpec((a, b), lambda i: (0, 0), pipeline_mode=pl.Buffered(1))
    return pl.pallas_call(
        body, name="in_proj_bwd_x", grid=(t // tm,),
        in_specs=[rows(b - a) for _, a, b in PIECES] + [full(D_IN, D_MODEL), rows(D_MODEL), rows(D_MODEL),
                                                          full(1, D_MODEL)],
        out_specs=[rows(D_MODEL), full(1, D_MODEL)],
        out_shape=[jax.ShapeDtypeStruct((t, D_MODEL), _F32), jax.ShapeDtypeStruct((1, D_MODEL), _F32)],
        compiler_params=_cparams(("arbitrary",), VMEM_LIMIT),
    )(*dpieces, wt, x2, dh2, g_in)


def _in_proj_bwd_w(h, dpieces, others, osplits):
    t = h.shape[0]
    tm = min(t, 1024)
    nt = t // tm
    np_, no = len(PIECES), len(others)
    half = D_MODEL // 2

    def body(*refs):
        h_ref, dp_refs, o_refs = refs[0], refs[1:1 + np_], refs[1 + np_:1 + np_ + no]
        mine_ref, theirs_ref = refs[1 + np_ + no:3 + np_ + no]
        r_refs = refs[3 + np_ + no:3 + np_ + 2 * no]
        acc_ref, stage_ref, keep_sem, send, recv, o_send, o_recv = refs[3 + np_ + 2 * no:]
        i = pl.program_id(0)
        x, y, c = _place()
        sibling = (x, y, 1 - c)
        early = [pltpu.make_async_remote_copy(
            src_ref=osplits[k].half(o_refs[k], 1 - c), dst_ref=r_refs[k], send_sem=o_send.at[k], recv_sem=o_recv.at[k],
            device_id=sibling, device_id_type=_MESH) for k in range(no)]

        @pl.when(i == 0)
        def _():
            for cp in early:
                cp.start()

        hv = h_ref[...]
        cols = lambda core: pl.ds(pl.multiple_of(core * half, LANE), half)
        writes = []
        by_size = sorted(range(np_), key=lambda j: PIECES[j][1] - PIECES[j][2])
        for j, ((name, a, b), dp) in [(j, (PIECES[j], dp_refs[j])) for j in by_size]:
            part = _dot_tn(dp[...], hv)
            if name == "alr":
                b = a + B_GATE_RANK
                part = part[0:B_GATE_RANK]
            acc_ref[a:b, :] = jnp.where(i == 0, 0.0, acc_ref[a:b, :]) + part
            keep = pltpu.make_async_copy(acc_ref.at[a:b, cols(c)], mine_ref.at[a:b], keep_sem.at[j])
            give = pltpu.make_async_remote_copy(
                src_ref=stage_ref.at[a:b], dst_ref=theirs_ref.at[a:b], send_sem=send.at[j],
                recv_sem=recv.at[j], device_id=sibling, device_id_type=_MESH)
            writes += [keep, give]

            @pl.when(i == nt - 1)
            def _(keep=keep, give=give, a=a, b=b):
                keep.start()
                stage_ref[a:b, :] = jnp.where(c == 0, acc_ref[a:b, half:], acc_ref[a:b, :half]).astype(_MX)
                give.start()

        @pl.when(i == nt - 1)
        def _():
            for cp in writes + early:
                cp.wait()

    rows = lambda w: pl.BlockSpec((tm, w), lambda i: (i, 0))
    halves = [jax.ShapeDtypeStruct((D_IN, half), _F32), jax.ShapeDtypeStruct((D_IN, half), _MX)]
    return pl.pallas_call(
        body, name="in_proj_bwd_w", grid=(nt,),
        in_specs=[rows(D_MODEL)] + [rows(b - a) for _, a, b in PIECES] + [_ANY] * no,
        out_specs=[_ANY] * (2 + no),
        out_shape=halves + [jax.ShapeDtypeStruct(sp.half_shape(g.shape), g.dtype) for g, sp in zip(others, osplits)],
        scratch_shapes=[pltpu.VMEM((D_IN, D_MODEL), _F32), pltpu.VMEM((D_IN, half), _MX),
                        pltpu.SemaphoreType.DMA((np_,)), pltpu.SemaphoreType.DMA((np_,)), pltpu.SemaphoreType.DMA((np_,)),
                        pltpu.SemaphoreType.DMA((no,)), pltpu.SemaphoreType.DMA((no,))],
        compiler_params=_cparams(("arbitrary",), VMEM_LIMIT),
    )(h, *dpieces, *others)


def _place():
    return lax.axis_index("x"), lax.axis_index("y"), lax.axis_index("c")


def _other_chips(x, y):
    return [(1 - x, y), (x, 1 - y), (1 - x, 1 - y)]


class _Split(NamedTuple):
    by_rows: bool
    step: int
    size: int

    def half(self, ref, c):
        r, n = ref.shape[-2:]
        if self.by_rows:
            return ref.at[:, pl.ds(pl.multiple_of(c * (n // 2), LANE), n // 2)]
        return ref.at[pl.ds(pl.multiple_of(c * (r // 2), 16), r // 2), :]

    def chip_part(self, ref, k):
        if self.by_rows:
            return ref.at[pl.ds(pl.multiple_of(k * self.step, 16), self.size), :]
        return ref.at[:, pl.ds(pl.multiple_of(k * self.size, LANE), self.size)]

    def half_shape(self, shape):
        r, n = shape
        return (r, n // 2) if self.by_rows else (r // 2, n)

    def part_shape(self, shape):
        r, n = shape
        return (self.size, n) if self.by_rows else (r, self.size)


SPLIT_W_IN_T = _Split(True, WINDOW_STEP, WINDOW_ROWS)
SPLIT_W_O = _Split(True, 256, 256)
SPLIT_W_OUT = _Split(False, 256, 256)


def _gather_weights(shards, splits, fulls, pos_f):
    nw = len(shards)
    t = pos_f.shape[0]

    def body(*refs):
        ins, (pos_ref, c_ref) = refs[:nw], refs[nw:nw + 2]
        outs, tables = refs[nw + 2:2 * nw + 2], refs[2 * nw + 2:2 * nw + 5]
        send_a, recv_a, send_b, recv_b = refs[2 * nw + 5:]
        x, y, c = _place()
        me = 2 * x + y
        peers = _other_chips(x, y)

        def place(i, k, half):
            if splits[i] is None:
                return outs[i].at[k]
            if fulls[i][0] == 4 and len(fulls[i]) == 3:
                whole = outs[i].at[k]
            else:
                whole = splits[i].chip_part(outs[i], k)
            return splits[i].half(whole, half)

        first, passed = [], []
        for i in range(nw):
            src = ins[i] if splits[i] is None else splits[i].half(ins[i], c)
            for j, (px, py) in enumerate(peers):
                cp = pltpu.make_async_remote_copy(
                    src_ref=src, dst_ref=place(i, me, c), send_sem=send_a.at[3 * i + j],
                    recv_sem=recv_a.at[3 * i + j], device_id=(px, py, c), device_id_type=_MESH)
                cp.start()
                first.append(cp)
        _rope_tables_into(pos_ref, c_ref, *tables)
        for i in range(nw):
            for j, (px, py) in enumerate(peers):
                landed = place(i, 2 * px + py, c)
                pltpu.make_async_remote_copy(
                    src_ref=landed, dst_ref=landed, send_sem=send_a.at[3 * i + j], recv_sem=recv_a.at[3 * i + j],
                    device_id=(px, py, c), device_id_type=_MESH).wait_recv()
                if splits[i] is not None:
                    cp = pltpu.make_async_remote_copy(
                        src_ref=landed, dst_ref=landed, send_sem=send_b.at[3 * i + j], recv_sem=recv_b.at[3 * i + j],
                        device_id=(x, y, 1 - c), device_id_type=_MESH)
                    cp.start()
                    passed.append(cp)
        for i in range(nw):
            if splits[i] is None:
                continue
            for j, (px, py) in enumerate(peers):
                theirs = place(i, 2 * px + py, 1 - c)
                pltpu.make_async_remote_copy(
                    src_ref=theirs, dst_ref=theirs, send_sem=send_b.at[3 * i + j], recv_sem=recv_b.at[3 * i + j],
                    device_id=(x, y, 1 - c), device_id_type=_MESH).wait_recv()
        for cp in first + passed:
            cp.wait_send()

    vm = pl.BlockSpec(memory_space=pltpu.VMEM)
    tab = jax.ShapeDtypeStruct((t, LANE), _F32)
    return pl.pallas_call(
        body, name="gather_weights",
        in_specs=[_ANY] * nw + [vm, vm], out_specs=[_ANY] * nw + [vm] * 3,
        out_shape=[jax.ShapeDtypeStruct(f, s.dtype) for f, s in zip(fulls, shards)] + [tab] * 3,
        scratch_shapes=[pltpu.SemaphoreType.DMA((3 * nw,)) for _ in range(4)],
        compiler_params=_cparams(None, VMEM_LIMIT),
    )(*shards, pos_f, _rope_consts())


def _assemble_w_in_t(slots):
    bw = 256
    ov = WINDOW_ROWS - WINDOW_STEP

    def body(s_ref, o_ref):
        for k in range(4):
            base = k * WINDOW_STEP
            lo = 0 if k == 0 else ov
            if k > 0:
                o_ref[base:base + ov, :] = s_ref[k - 1, WINDOW_STEP:WINDOW_ROWS, :] + s_ref[k, 0:ov, :]
            hi = WINDOW_ROWS if k == 3 else WINDOW_STEP
            o_ref[base + lo:base + hi, :] = s_ref[k, lo:hi, :]

    return pl.pallas_call(
        body, name="assemble_w_in_t", grid=(D_MODEL // bw,),
        in_specs=[pl.BlockSpec((4, WINDOW_ROWS, bw), lambda i: (0, 0, i))],
        out_specs=pl.BlockSpec((D_IN, bw), lambda i: (0, i)),
        out_shape=jax.ShapeDtypeStruct((D_IN, D_MODEL), slots.dtype),
        compiler_params=_cparams(("parallel",)),
    )(slots)


def _row_block(rows):
    for cand in (976, 176, 256, 128):
        if rows % cand == 0:
            return cand
    return rows


def _pair_sum(g, r, split, c_arr, name):
    hr, hn = r.shape
    br = _row_block(hr)
    if split is None:
        g_spec = pl.BlockSpec((br, hn), lambda i, c_ref: (i, 0))
    elif split.by_rows:
        g_spec = pl.BlockSpec((br, hn), lambda i, c_ref: (i, c_ref[0]))
    else:
        g_spec = pl.BlockSpec((br, hn), lambda i, c_ref: (c_ref[0] * (hr // br) + i, 0))

    def body(c_ref, g_ref, r_ref, o_ref):
        o_ref[...] = (g_ref[...] + r_ref[...]).astype(o_ref.dtype)

    return pl.pallas_call(
        body, name=name,
        grid_spec=pltpu.PrefetchScalarGridSpec(
            num_scalar_prefetch=1, grid=(hr // br,),
            in_specs=[g_spec, pl.BlockSpec((br, hn), lambda i, c_ref: (i, 0))],
            out_specs=pl.BlockSpec((br, hn), lambda i, c_ref: (i, 0))),
        out_shape=jax.ShapeDtypeStruct(r.shape, _MX),
        compiler_params=_cparams(("parallel",)),
    )(c_arr, g, r)


def _pair_sums_whole(gs, rs, splits, c_arr):
    n = len(gs)

    def g_spec(r, split):
        at = (lambda i, c_ref: (0, c_ref[0])) if split.by_rows else (lambda i, c_ref: (c_ref[0], 0))
        return pl.BlockSpec(r.shape, at)

    def body(c_ref, *refs):
        for g_ref, r_ref, o_ref in zip(refs[:n], refs[n:2 * n], refs[2 * n:]):
            o_ref[...] = (g_ref[...] + r_ref[...]).astype(o_ref.dtype)

    whole = lambda r: pl.BlockSpec(r.shape, lambda i, c_ref: (0, 0))
    return pl.pallas_call(
        body, name="pair_sum_small",
        grid_spec=pltpu.PrefetchScalarGridSpec(
            num_scalar_prefetch=1, grid=(1,),
            in_specs=[g_spec(r, sp) for r, sp in zip(rs, splits)] + [whole(r) for r in rs],
            out_specs=[whole(r) for r in rs]),
        out_shape=[jax.ShapeDtypeStruct(r.shape, _MX) for r in rs],
        compiler_params=_cparams(("arbitrary",), VMEM_LIMIT),
    )(c_arr, *gs, *rs)


_HBM = pl.BlockSpec(memory_space=pltpu.HBM)
_SEM = pl.BlockSpec(memory_space=pltpu.SEMAPHORE)
_FLOWS = pltpu.SideEffectType.DATAFLOW_SIDE_EFFECTING


def _chip_exchange_copies(refs, send, recv, splits):
    nw = len(refs) // 2
    x, y, c = _place()
    me = 2 * x + y
    copies = []
    for i in range(nw):
        for px, py in _other_chips(x, y):
            copies.append((splits[i].chip_part(refs[i], 2 * px + py), refs[nw + i].at[me], (px, py, c)))
    return [pltpu.make_async_remote_copy(src_ref=src, dst_ref=dst, send_sem=send.at[k], recv_sem=recv.at[k],
                                         device_id=peer, device_id_type=_MESH)
            for k, (src, dst, peer) in enumerate(copies)]


def _late_gather_copies(refs, send, recv, splits):
    nw = len(refs) // 2
    x, y, c = _place()
    me = 2 * x + y
    copies = []
    for i in range(nw):
        for px, py in [(x, y)] + _other_chips(x, y):
            copies.append((refs[i], splits[i].chip_part(refs[nw + i], me), (px, py, c)))
    return [pltpu.make_async_remote_copy(src_ref=src, dst_ref=dst, send_sem=send.at[k], recv_sem=recv.at[k],
                                         device_id=peer, device_id_type=_MESH)
            for k, (src, dst, peer) in enumerate(copies)]


def _start_copies(name, flying, copies_of, n_copies, after=None):
    first = [] if after is None else [after]

    def body(*refs):
        ins = refs[:len(flying)]
        send, recv = refs[len(flying) + len(first):len(flying) + len(first) + 2]
        token = refs[-1]
        for cp in copies_of(ins, send, recv):
            cp.start()
        token[...] = jnp.zeros_like(token)

    outs = pl.pallas_call(
        body, name=name,
        in_specs=[_HBM] * len(flying) + [_ANY] * len(first),
        out_specs=[_SEM, _SEM] + [_HBM] * len(flying) + [pl.BlockSpec(memory_space=pltpu.VMEM)],
        out_shape=[pltpu.SemaphoreType.DMA((n_copies,)), pltpu.SemaphoreType.DMA((n_copies,))]
        + [pltpu.HBM(f.shape, f.dtype) for f in flying] + [jax.ShapeDtypeStruct((8, LANE), _F32)],
        input_output_aliases={i: 2 + i for i in range(len(flying))},
        compiler_params=pltpu.CompilerParams(has_side_effects=_FLOWS),
    )(*[pltpu.with_memory_space_constraint(f, pltpu.HBM) for f in flying], *first)
    return outs[0], outs[1], outs[2:2 + len(flying)], outs[-1]


def _wait_copies(name, send, recv, flying, copies_of, after):
    def body(*refs):
        ins = refs[:len(flying)]
        send_ref, recv_ref = refs[len(flying):len(flying) + 2]
        for cp in copies_of(ins, send_ref, recv_ref):
            cp.wait_send()
            cp.wait_recv()

    return pl.pallas_call(
        body, name=name,
        in_specs=[_HBM] * len(flying) + [_SEM, _SEM, _ANY],
        out_specs=[_HBM] * len(flying),
        out_shape=[pltpu.HBM(f.shape, f.dtype) for f in flying],
        input_output_aliases={i: i for i in range(len(flying))},
        compiler_params=pltpu.CompilerParams(has_side_effects=_FLOWS),
    )(*flying, send, recv, after)


def _sum_chips(qs, ps, splits, place_arr):
    n = len(qs)
    q_specs, p_specs, o_specs, out_shapes = [], [], [], []
    for q, split in zip(qs, splits):
        _, hr, hn = q.shape
        q_specs.append(pl.BlockSpec((4, hr, hn), lambda i, pr: (0, 0, 0)))
        if split.by_rows:
            out_shapes.append((hr, 2 * hn))
            o_specs.append(pl.BlockSpec((hr, hn), lambda i, pr: (0, pr[0])))
            p_specs.append(pl.BlockSpec((pl.Element(hr), pl.Element(hn)), lambda i, pr, step=split.step: (pr[1] * step, 0)))
        else:
            out_shapes.append((2 * hr, hn))
            o_specs.append(pl.BlockSpec((hr, hn), lambda i, pr: (pr[0], 0)))
            p_specs.append(pl.BlockSpec((hr, hn), lambda i, pr: (0, pr[1])))

    def body(pr, *refs):
        for q_ref, p_ref, o_ref in zip(refs[:n], refs[n:2 * n], refs[2 * n:]):
            f = lambda k: jnp.where(pr[1] == k, p_ref[...], q_ref[k]).astype(_F32)
            o_ref[...] = ((f(0) + f(1)) + f(2)) + f(3)

    return pl.pallas_call(
        body, name="chip_sums",
        grid_spec=pltpu.PrefetchScalarGridSpec(
            num_scalar_prefetch=1, grid=(1,), in_specs=q_specs + p_specs, out_specs=o_specs),
        out_shape=[jax.ShapeDtypeStruct(s, _F32) for s in out_shapes],
        compiler_params=_cparams(("arbitrary",), VMEM_LIMIT),
    )(place_arr, *qs, *ps)


def _pair_share(bufs, splits, small):
    nw = len(bufs)

    def body(*refs):
        ins, small_ref, outs, all_ref = refs[:nw], refs[nw], refs[nw + 1:2 * nw + 1], refs[2 * nw + 1]
        send, recv, s_send, s_recv = refs[2 * nw + 2:]
        x, y, c = _place()
        copies = []
        for r in range(1, 8):
            peer = (1 - x if r & 4 else x, 1 - y if r & 2 else y, 1 - c if r & 1 else c)
            cp = pltpu.make_async_remote_copy(
                src_ref=small_ref, dst_ref=all_ref.at[4 * x + 2 * y + c], send_sem=s_send.at[r - 1],
                recv_sem=s_recv.at[r - 1], device_id=peer, device_id_type=_MESH)
            cp.start()
            copies.append(cp)
        for i in range(nw):
            cp = pltpu.make_async_remote_copy(
                src_ref=splits[i].half(ins[i], c), dst_ref=splits[i].half(outs[i], c), send_sem=send.at[i],
                recv_sem=recv.at[i], device_id=(x, y, 1 - c), device_id_type=_MESH)
            cp.start()
            copies.append(cp)
        for cp in copies:
            cp.wait()

    return pl.pallas_call(
        body, name="grad_pair_share",
        in_specs=[_ANY] * (nw + 1), out_specs=[_ANY] * (nw + 1),
        out_shape=[jax.ShapeDtypeStruct(b.shape, b.dtype) for b in bufs]
        + [jax.ShapeDtypeStruct((8,) + small.shape, small.dtype)],
        input_output_aliases={i: i for i in range(nw)},
        scratch_shapes=[pltpu.SemaphoreType.DMA((nw,)), pltpu.SemaphoreType.DMA((nw,)),
                        pltpu.SemaphoreType.DMA((7,)), pltpu.SemaphoreType.DMA((7,))],
    )(*bufs, small)


def _sum_devices(parts, own, dev_arr):
    def body(dev, p_ref, own_ref, tot_ref):
        f = lambda d: jnp.where(dev[0] == d, own_ref[...], p_ref[d])
        acc = f(0)
        for d in range(1, 8):
            acc = acc + f(d)
        tot_ref[...] = acc

    return pl.pallas_call(
        body, name="small_sum",
        grid_spec=pltpu.PrefetchScalarGridSpec(
            num_scalar_prefetch=1, grid=(1,),
            in_specs=[pl.BlockSpec(parts.shape, lambda i, dev: (0, 0, 0)), pl.BlockSpec(own.shape, lambda i, dev: (0, 0))],
            out_specs=pl.BlockSpec(own.shape, lambda i, dev: (0, 0))),
        out_shape=jax.ShapeDtypeStruct(own.shape, own.dtype),
    )(dev_arr, parts, own)


def _adam_update(w, g, m, v):
    m2 = ADAM_B1 * m + (1.0 - ADAM_B1) * g
    v2 = ADAM_B2 * v + (1.0 - ADAM_B2) * (g * g)
    m_hat = m2 / (1.0 - ADAM_B1 ** ADAM_STEP)
    v_hat = v2 / (1.0 - ADAM_B2 ** ADAM_STEP)
    return -ADAM_LR * (m_hat / (jnp.sqrt(v_hat) + ADAM_EPS) + ADAM_WD * w), m2, v2


SMALL_AT = dict(g_in=(0, 0), g_final=(1, 0), g_gla_norm=(2, 0), b_alpha=(2, B_WIDTH), attn_sinks=(2, B_WIDTH + B_KEY_WIDTH))
LOSS_AT = (2, B_WIDTH + B_KEY_WIDTH + LANE)
WUP_ROWS = (3, 7)


def _adamw_small(tot, g_wup, params):
    names = list(params)

    def body(*refs):
        tot_ref, gw_ref = refs[0], refs[1]
        ins = refs[2:2 + 3 * len(names)]
        outs = refs[2 + 3 * len(names):]
        for i, nm in enumerate(names):
            w_ref, m_ref, v_ref = ins[3 * i:3 * i + 3]
            if nm in SMALL_AT:
                r, a = SMALL_AT[nm]
                g = tot_ref[r:r + 1, a:a + w_ref.shape[1]]
            else:
                g = gw_ref[...]
            d, m2, v2 = _adam_update(w_ref[...], g, m_ref[...], v_ref[...])
            for o_ref, val in zip(outs[4 * i:4 * i + 4], (g, d, m2, v2)):
                o_ref[...] = val

    vm = pl.BlockSpec(memory_space=pltpu.VMEM)
    flat = [a for nm in names for a in params[nm]]
    out_shape = [jax.ShapeDtypeStruct(params[nm][0].shape, _F32) for nm in names for _ in range(4)]
    outs = pl.pallas_call(
        body, name="adamw_small", in_specs=[vm] * (2 + len(flat)), out_specs=[vm] * len(out_shape), out_shape=out_shape,
    )(tot, g_wup, *flat)
    return {nm: tuple(outs[4 * i:4 * i + 4]) for i, nm in enumerate(names)}


def _adamw(w, g, m, v, name):
    lead = w.shape[0] != 1
    r, n = (w.shape[0], w.shape[2]) if lead else w.shape[1:]
    br = r
    for cand in (256, 244, 128):
        if r > cand and r % cand == 0:
            br = cand
            break

    def body(w_ref, g_ref, m_ref, v_ref, d_ref, nm_ref, nv_ref):
        d_ref[...], nm_ref[...], nv_ref[...] = _adam_update(w_ref[...], g_ref[...], m_ref[...], v_ref[...])

    blk = pl.BlockSpec((br, 1, n), lambda i: (i, 0, 0)) if lead else pl.BlockSpec((None, br, n), lambda i: (0, i, 0))
    shp = jax.ShapeDtypeStruct(w.shape, _F32)
    return pl.pallas_call(
        body, name=name, grid=(r // br,),
        in_specs=[blk] * 4, out_specs=[blk] * 3, out_shape=[shp] * 3,
        compiler_params=_cparams(("parallel",)),
    )(w, g, m, v)


def _adamw_w_in(w, g_window, m, v, chip_arr):
    r, _, n = w.shape
    br = r // 5

    def body(k_ref, w_ref, g_ref, m_ref, v_ref, go_ref, d_ref, nm_ref, nv_ref):
        first = (SHARD - WINDOW_STEP) * k_ref[0] + br * pl.program_id(0)
        wide = g_ref[pl.ds(pl.multiple_of((first // 8) * 8, 8), br + 4), :]
        g = jnp.where(first % 8 == 0, wide[:br], wide[4:]).reshape(br, 1, n)
        go_ref[...] = g
        d_ref[...], nm_ref[...], nv_ref[...] = _adam_update(w_ref[...], g, m_ref[...], v_ref[...])

    blk = pl.BlockSpec((br, 1, n), lambda i, k_ref: (i, 0, 0))
    g_spec = pl.BlockSpec(memory_space=pltpu.VMEM)
    shp = jax.ShapeDtypeStruct(w.shape, _F32)
    return pl.pallas_call(
        body, name="adamw_w_in",
        grid_spec=pltpu.PrefetchScalarGridSpec(
            num_scalar_prefetch=1, grid=(r // br,), in_specs=[blk, g_spec, blk, blk], out_specs=[blk] * 4),
        out_shape=[shp] * 4,
        compiler_params=_cparams(("parallel",)),
    )(chip_arr, w, g_window, m, v)


def _adamw_whole(items):
    n = len(items)

    def body(*refs):
        for i in range(n):
            w_ref, g_ref, m_ref, v_ref = refs[4 * i:4 * i + 4]
            d_ref, nm_ref, nv_ref = refs[4 * n + 3 * i:4 * n + 3 * i + 3]
            d_ref[...], nm_ref[...], nv_ref[...] = _adam_update(w_ref[...], g_ref[...], m_ref[...], v_ref[...])

    whole = lambda a: pl.BlockSpec((None,) + a.shape[1:], lambda i: (0, 0, 0))
    outs = pl.pallas_call(
        body, name="adamw_out_weights", grid=(1,),
        in_specs=[whole(a) for it in items for a in it],
        out_specs=[whole(it[0]) for it in items for _ in range(3)],
        out_shape=[jax.ShapeDtypeStruct(it[0].shape, _F32) for it in items for _ in range(3)],
        compiler_params=_cparams(("arbitrary",), VMEM_LIMIT),
    )(*[a for it in items for a in it])
    return [tuple(outs[3 * i:3 * i + 3]) for i in range(n)]


def kernel(x, positions, g_in, w_in, w_alpha_up, b_alpha, attn_sinks, g_gla_norm, w_out_a, w_out_b, w_o, g_final, loss_target, m_g_in, m_w_in, m_w_alpha_up, m_b_alpha, m_attn_sinks, m_g_gla_norm, m_w_out_a, m_w_out_b, m_w_o, m_g_final, v_g_in, v_w_in, v_w_alpha_up, v_b_alpha, v_attn_sinks, v_g_gla_norm, v_w_out_a, v_w_out_b, v_w_o, v_g_final):
    nseq, seq, _ = x.shape
    t = nseq * seq
    cx, cy, cc = _place()
    chip = 2 * cx + cy
    c_arr = jnp.reshape(cc, (1,)).astype(jnp.int32)

    tr = lambda w: jnp.transpose(w, (2, 0, 1))
    w_in_t = tr(w_in).reshape(SHARD, D_MODEL).astype(_MX)
    pad = WINDOW_ROWS - SHARD
    window = lax.switch(chip, [lambda w, k=k: jnp.pad(w, ((4 * k, pad - 4 * k), (0, 0))) for k in range(4)], w_in_t)
    shards = [window, w_alpha_up[0].astype(_MX)]
    late = [w_out_a[0].astype(_MX), w_out_b[0].astype(_MX), w_o[0].astype(_MX)]
    late_splits = [SPLIT_W_OUT, SPLIT_W_OUT, SPLIT_W_O]
    splits = [SPLIT_W_IN_T, None]
    fulls = [(4, WINDOW_ROWS, D_MODEL), (4, B_GATE_RANK, B_KEY_WIDTH // 4)]
    pos_f = positions.astype(_F32).reshape(t, 1)
    win_g, wup_g, cos, sa, sb = _gather_weights(shards, splits, fulls, pos_f)
    late_copies = lambda refs, send, recv: _late_gather_copies(refs, send, recv, late_splits)
    late_full = [lax.empty(shape, _MX) for shape in ((A_WIDTH, D_MODEL), (B_WIDTH, D_MODEL), (D_MODEL, D_MODEL))]
    l_send, l_recv, l_flying, l_token = _start_copies("late_gather_start", late + late_full, late_copies,
                                                      4 * len(late), after=win_g)
    win_g = lax.dynamic_update_slice(win_g, window[None], (chip, 0, 0))
    wup_g = lax.dynamic_update_slice(wup_g, shards[1][None], (chip, 0, 0))
    wt = _assemble_w_in_t(win_g)
    wup = jnp.concatenate([jnp.transpose(wup_g, (1, 0, 2)).reshape(B_GATE_RANK, B_KEY_WIDTH),
                           jnp.zeros((LANE - B_GATE_RANK, B_KEY_WIDTH), _MX)], axis=0)

    x2 = x.reshape(t, D_MODEL)
    tgt = loss_target.reshape(t, D_MODEL)
    sinks = attn_sinks.reshape(A_HEADS)
    gf = g_final.reshape(1, D_MODEL)

    h, qkv, za, qkb, vb, zb, alr, ga, gb = _in_proj(x2, g_in + l_token[0, 0], wt, cos, sa, sb)
    oa = _attn_fwd(qkv, za, sinks, nseq)
    ob, oraw, sst = _gla_fwd(qkb, vb, zb, alr, wup, b_alpha, g_gla_norm, nseq)

    wa, wb, wo = _wait_copies("late_gather_wait", l_send, l_recv, l_flying, late_copies, ob)[len(late):]
    dh2, doa, dob, dga, dgb, dwa, dwb, dwo, dgf, lossv = _merge_loss(oa, ob, ga, gb, x2, tgt, wa, wb, wo, gf)

    dqkv, dza, dsink = _attn_bwd(qkv, za, doa, sinks, cos, sa, sb, nseq)
    dqkb, dvb, dzb, dalr, dwup, dba, dgn = _gla_bwd(qkb, vb, zb, alr, oraw, dob, sst, wup, b_alpha, g_gla_norm, nseq)
    dpieces = [dqkv, dza, dqkb, dvb, dzb, dalr, dga, dgb]
    gsplits = [SPLIT_W_IN_T, SPLIT_W_OUT, SPLIT_W_OUT, SPLIT_W_O]
    names = ("w_in", "w_out_a", "w_out_b", "w_o")
    dwin_mine, *from_sibling = _in_proj_bwd_w(h, dpieces, [dwa, dwb, dwo], gsplits[1:])
    pair_sums = [_pair_sum(dwin_mine, from_sibling[0], None, c_arr, "pair_sum_w_in"),
                 *_pair_sums_whole([dwa, dwb, dwo], from_sibling[1:], gsplits[1:], c_arr)]
    exchange = lambda refs, send, recv: _chip_exchange_copies(refs, send, recv, gsplits)
    lands = [lax.empty((4,) + sp.part_shape(p.shape), p.dtype) for p, sp in zip(pair_sums, gsplits)]
    send, recv, flying, token = _start_copies("grad_chip_exchange_start", pair_sums + lands, exchange, 3 * len(lands))
    grad_x2, dgin = _in_proj_bwd_x(dpieces, wt, x2, dh2, g_in + token[0, 0])
    landed = _wait_copies("grad_chip_exchange_wait", send, recv, flying, exchange, grad_x2)
    place_arr = jnp.stack([cc, chip]).astype(jnp.int32)
    reduced = _sum_chips(landed[len(lands):], landed[:len(lands)], gsplits, place_arr)
    row2 = jnp.concatenate([dgn, dba, jnp.pad(dsink[:, 0].reshape(1, A_HEADS), ((0, 0), (0, LANE - A_HEADS))),
                            jnp.pad(jnp.sum(lossv, axis=1, keepdims=True), ((0, 0), (0, LANE - 1)))], axis=1)
    small = jnp.concatenate([dgin, dgf, row2, dwup[:B_GATE_RANK].reshape(WUP_ROWS[1] - WUP_ROWS[0], D_MODEL),
                             jnp.zeros((1, D_MODEL), _F32)], axis=0)
    g_window, g_wa, g_wb, g_wo, small_parts = _pair_share(reduced, gsplits, small)
    dev_arr = jnp.reshape(2 * chip + cc, (1,)).astype(jnp.int32)
    tot = _sum_devices(small_parts, small, dev_arr)
    loss = tot[LOSS_AT]
    nup = B_KEY_WIDTH // 4
    g_wup = lax.dynamic_slice(tot[WUP_ROWS[0]:WUP_ROWS[1]].reshape(B_GATE_RANK, B_KEY_WIDTH), (0, chip * nup),
                              (B_GATE_RANK, nup))

    row = lambda a: a.reshape(1, -1)
    sm = _adamw_small(tot, g_wup, dict(
        g_in=(g_in, m_g_in, v_g_in), g_final=(row(g_final), row(m_g_final), row(v_g_final)),
        g_gla_norm=(g_gla_norm, m_g_gla_norm, v_g_gla_norm), b_alpha=(b_alpha, m_b_alpha, v_b_alpha),
        attn_sinks=(attn_sinks, m_attn_sinks, v_attn_sinks),
        w_alpha_up=(w_alpha_up[0], m_w_alpha_up[0], v_w_alpha_up[0])))
    sm["g_final"] = tuple(a.reshape(D_MODEL) for a in sm["g_final"])
    sm["w_alpha_up"] = tuple(a[None] for a in sm["w_alpha_up"])

    untr = lambda a: jnp.transpose(a, (1, 2, 0))
    chip_arr = jnp.reshape(chip, (1,)).astype(jnp.int32)
    big = dict(w_in=tuple(untr(a) for a in _adamw_w_in(tr(w_in), g_window, tr(m_w_in), tr(v_w_in), chip_arr)))
    out_weights = (("w_out_a", w_out_a, g_wa, m_w_out_a, v_w_out_a),
                   ("w_out_b", w_out_b, g_wb, m_w_out_b, v_w_out_b), ("w_o", w_o, g_wo, m_w_o, v_w_o))
    updates = _adamw_whole([(w, g[None], m, v) for _, w, g, m, v in out_weights])
    for (nm, _, g, _, _), upd in zip(out_weights, updates):
        big[nm] = (g[None],) + upd

    order = ("g_in", "w_in", "w_alpha_up", "b_alpha", "attn_sinks", "g_gla_norm", "w_out_a", "w_out_b", "w_o", "g_final")
    outs = [big[nm][kind] if nm in big else sm[nm][kind] for kind in range(4) for nm in order]
    return (loss, grad_x2.reshape(x.shape), *outs)
```

```python
import math
from typing import NamedTuple

import numpy as np
import jax
import jax.numpy as jnp
from jax import lax
from jax.experimental import pallas as pl
from jax.experimental.pallas import tpu as pltpu

D_MODEL = 1024
A_HEADS, A_KV_HEADS, A_HEAD_DIM = 8, 2, 64
A_GROUP = A_HEADS // A_KV_HEADS
A_WIDTH, A_KV_WIDTH = 512, 128
BLOCK = 128
ROPE_THETA = 500000.0
ROPE_DIM = 16
B_HEADS, B_KEY_DIM, B_VAL_DIM = 4, 64, 128
B_KEY_WIDTH, B_WIDTH = 256, 512
B_GATE_RANK = 16
B_GATE_TEMP = 16.0
B_CHUNK = 64
NORM_EPS = 1e-6
NEG_BIG = -1e30
D_IN = 4880

ADAM_LR, ADAM_B1, ADAM_B2, ADAM_EPS, ADAM_WD, ADAM_STEP = 0.001, 0.9, 0.999, 1e-08, 0.01, 10

LANE = 128
ALR_AT = 2816
PIECES = (("qkv", 0, 768), ("za", 768, 1280), ("qkb", 1280, 1792), ("vb", 1792, 2304),
          ("zb", 2304, 2816), ("alr", ALR_AT, ALR_AT + LANE), ("ga", 2832, 3856), ("gb", 3856, 4880))
SHARD = D_IN // 4
WINDOW_STEP = 1216
WINDOW_ROWS = 1232

GLA_BLOCK = 256
GLA_FWD_BLOCK = 1024
GLA_BWD_TILE = 4
MERGE_SLAB = 16
VMEM_LIMIT = 56 * 1024 * 1024

_F32 = jnp.float32
_MX = jnp.bfloat16
_ST = jnp.bfloat16

_MESH = pl.DeviceIdType.MESH
_ANY = pl.BlockSpec(memory_space=pl.ANY)


def _cparams(sem=None, vmem=None):
    return pltpu.CompilerParams(dimension_semantics=sem, vmem_limit_bytes=vmem)


def _dot(a, b):
    return jnp.dot(a.astype(_MX), b.astype(_MX), preferred_element_type=_F32)


def _dot_nt(a, b):
    return lax.dot_general(a.astype(_MX), b.astype(_MX), (((1,), (1,)), ((), ())),
                           preferred_element_type=_F32)


def _dot_tn(a, b):
    return lax.dot_general(a.astype(_MX), b.astype(_MX), (((0,), (0,)), ((), ())),
                           preferred_element_type=_F32)


def _dot_ones(ones_mat, v):
    o = ones_mat.astype(jnp.bfloat16)
    v0 = v.astype(jnp.bfloat16)
    v1 = (v - v0.astype(_F32)).astype(jnp.bfloat16)
    d = lambda t: jnp.dot(o, t, preferred_element_type=_F32)
    return d(v0) + d(v1)


def _sigmoid(x):
    return 0.5 * jnp.tanh(0.5 * x) + 0.5


def _log_sigmoid(x):
    return jnp.minimum(x, 0.0) - jnp.log(1.0 + jnp.exp(-jnp.abs(x)))


def _lane_tile(t, width):
    reps = width // t.shape[1]
    return t if reps == 1 else jnp.tile(t, (1, reps))


def _rope(t, cos, sa, sb, sign):
    w = t.shape[1]
    rot = pltpu.roll(t, w - 8, 1) * _lane_tile(sa, w) + pltpu.roll(t, 8, 1) * _lane_tile(sb, w)
    return t * _lane_tile(cos, w) + sign * rot


def _rms_bwd(dy_g, n, r):
    return r * (dy_g - n * jnp.mean(dy_g * n, axis=-1, keepdims=True))


ROPE_ROWS = 256


def _rope_consts():
    lane = np.arange(LANE) % A_HEAD_DIM
    half = ROPE_DIM // 2
    inv = np.exp((np.float32(-math.log(ROPE_THETA)) * np.arange(half, dtype=np.float32)) * np.float32(2.0 / ROPE_DIM))
    consts = np.zeros((8, LANE), np.float32)
    consts[0] = np.where(lane < ROPE_DIM, inv[lane % half], 0.0)
    consts[1] = np.where(lane < half, -1.0, 0.0)
    consts[2] = np.where((lane >= half) & (lane < ROPE_DIM), 1.0, 0.0)
    return jnp.asarray(consts)


def _rope_tables_into(pos_ref, c_ref, cos_ref, sa_ref, sb_ref):
    def rows_of(b, carry):
        rows = pl.ds(pl.multiple_of(b * ROPE_ROWS, ROPE_ROWS), ROPE_ROWS)
        ang = pos_ref[rows, :] * c_ref[0:1, :]
        s = jnp.sin(ang)
        cos_ref[rows, :] = jnp.cos(ang)
        sa_ref[rows, :] = s * c_ref[1:2, :]
        sb_ref[rows, :] = s * c_ref[2:3, :]
        return carry

    lax.fori_loop(0, pos_ref.shape[0] // ROPE_ROWS, rows_of, 0)


def _in_proj(x2, g_in, wt, cos, sa, sb, after):
    t = x2.shape[0]
    tm = min(t, 512)

    def body(x_ref, g_ref, w_ref, cos_ref, sa_ref, sb_ref, after_ref, h_ref, qkv_ref, za_ref, qkb_ref,
             vb_ref, zb_ref, alr_ref, ga_ref, gb_ref):
        xv = x_ref[...]
        r = lax.rsqrt(jnp.mean(xv * xv, axis=-1, keepdims=True) + NORM_EPS)
        h = (xv * r * g_ref[...]).astype(_MX)
        h_ref[...] = h.astype(_ST)
        outs = dict(za=za_ref, qkb=qkb_ref, vb=vb_ref, zb=zb_ref, alr=alr_ref, ga=ga_ref, gb=gb_ref)
        for name, a, b in PIECES:
            p = _dot_nt(h, w_ref[a:b, :])
            if name == "qkv":
                c, s1, s2 = cos_ref[...], sa_ref[...], sb_ref[...]
                qkv_ref[:, 0:512] = _rope(p[:, 0:512], c, s1, s2, 1.0).astype(_ST)
                qkv_ref[:, 512:640] = _rope(p[:, 512:640], c, s1, s2, 1.0).astype(_ST)
                qkv_ref[:, 640:768] = p[:, 640:768].astype(_ST)
            else:
                outs[name][...] = p.astype(outs[name].dtype)

    rows = lambda w: pl.BlockSpec((tm, w), lambda i: (i, 0))
    shp = lambda name, w: jax.ShapeDtypeStruct((t, w), _F32 if name == "qkb" else _ST)
    widths = [D_MODEL] + [b - a for _, a, b in PIECES]
    return pl.pallas_call(
        body, name="in_proj", grid=(t // tm,),
        in_specs=[rows(D_MODEL), pl.BlockSpec((1, D_MODEL), lambda i: (0, 0)),
                  pl.BlockSpec((D_IN, D_MODEL), lambda i: (0, 0), pipeline_mode=pl.Buffered(1)),
                  rows(LANE), rows(LANE), rows(LANE), _ANY],
        out_specs=[rows(w) for w in widths],
        out_shape=[shp(n, w) for n, w in zip(["h"] + [p[0] for p in PIECES], widths)],
        compiler_params=_cparams(("parallel",), VMEM_LIMIT),
    )(x2, g_in, wt, cos, sa, sb, after)


def _attn_operands(k_prev, k_cur, v_prev, v_cur, want_bwd):
    kf = jnp.concatenate([k_prev, k_cur], axis=0).astype(_F32) * (A_HEAD_DIM ** -0.5)
    vf = jnp.concatenate([v_prev, v_cur], axis=0).astype(_F32)
    lo = lax.broadcasted_iota(jnp.int32, (1, LANE), 1) < 64

    def on_lanes(a):
        sw = pltpu.roll(a, 64, 1)
        z = jnp.zeros_like(a)
        return [[jnp.where(lo, a, z).astype(_MX), jnp.where(lo, z, sw).astype(_MX)],
                [jnp.where(lo, sw, z).astype(_MX), jnp.where(lo, z, a).astype(_MX)]]

    def on_rows(a):
        at = a.T.astype(_MX)
        z = jnp.zeros((64, at.shape[1]), _MX)
        top, bot = at[0:64], at[64:128]
        return [[jnp.concatenate([top, z], axis=0), jnp.concatenate([z, top], axis=0)],
                [jnp.concatenate([bot, z], axis=0), jnp.concatenate([z, bot], axis=0)]]

    ops = dict(k_lanes=on_lanes(kf), v_rows=on_rows(vf), lo=lo)
    if want_bwd:
        ops.update(v_lanes=on_lanes(vf), k_rows=on_rows(kf))
    return ops


def _attn_valid(n):
    kj = lax.broadcasted_iota(jnp.int32, (2 * BLOCK, 2 * BLOCK), 0) - BLOCK
    qi = lax.broadcasted_iota(jnp.int32, (2 * BLOCK, 2 * BLOCK), 1) & (BLOCK - 1)
    return (kj <= qi) & (qi - kj < BLOCK) & ((n > 0) | (kj >= 0))


def _attn_sinks(sink_ref, h_a, h_b):
    first = lax.broadcasted_iota(jnp.int32, (1, 2 * BLOCK), 1) < BLOCK
    return jnp.where(first, sink_ref[h_a], sink_ref[h_b])


def _attn_softmax_t(k_lanes, q_pair, valid, sink):
    s = jnp.where(valid, _dot_nt(k_lanes, q_pair), NEG_BIG)
    m = jnp.maximum(jnp.max(s, axis=0, keepdims=True), sink)
    e = jnp.exp(s - m)
    e_sink = jnp.exp(sink - m)
    inv = 1.0 / (jnp.sum(e, axis=0, keepdims=True) + e_sink)
    return e, e_sink, inv


ATTN_TILE = 8


def _attn_kv(qkv_ref, kvp_ref, j):
    rows = slice(j * BLOCK, (j + 1) * BLOCK)
    if j == 0:
        k_prev, v_prev = kvp_ref[:, 0:128], kvp_ref[:, 128:256]
    else:
        before = slice((j - 1) * BLOCK, j * BLOCK)
        k_prev, v_prev = qkv_ref[before, 512:640], qkv_ref[before, 640:768]
    return k_prev, qkv_ref[rows, 512:640], v_prev, qkv_ref[rows, 640:768]


def _attn_fwd(qkv, za, sinks, nseq):
    t = qkv.shape[0]
    nblk = min(ATTN_TILE, t // nseq // BLOCK)
    tile = nblk * BLOCK
    nt = t // nseq // tile

    def body(sink_ref, qkv_ref, kvp_ref, za_ref, oa_ref):
        for j in range(nblk):
            rows = slice(j * BLOCK, (j + 1) * BLOCK)
            ops = _attn_operands(*_attn_kv(qkv_ref, kvp_ref, j), False)
            valid = _attn_valid(nblk * pl.program_id(1) + j)[:, 0:BLOCK]
            for pr in range(A_HEADS // 2):
                lanes = slice(pr * LANE, (pr + 1) * LANE)
                g = pr // (A_GROUP // 2)
                q_pair = qkv_ref[rows, lanes]
                ot = None
                for half in range(2):
                    e, _, inv = _attn_softmax_t(ops["k_lanes"][g][half], q_pair, valid, sink_ref[2 * pr + half])
                    part = _dot(ops["v_rows"][g][half], e) * inv
                    ot = part if ot is None else ot + part
                z = za_ref[rows, lanes].astype(_F32)
                oa_ref[rows, lanes] = (ot.T * (z * _sigmoid(z))).astype(_ST)

    cur = lambda w: pl.BlockSpec((tile, w), lambda s, n: (s * nt + n, 0))
    return pl.pallas_call(
        body, name="attn_fwd", grid=(nseq, nt),
        in_specs=[pl.BlockSpec(memory_space=pltpu.SMEM), cur(768),
                  pl.BlockSpec((BLOCK, 256), lambda s, n: (nblk * (s * nt + n) - jnp.minimum(n, 1), 2)),
                  cur(512)],
        out_specs=cur(512), out_shape=jax.ShapeDtypeStruct((t, A_WIDTH), _ST),
        compiler_params=_cparams(("parallel", "arbitrary")),
    )(sinks, qkv, qkv, za)


def _attn_bwd(qkv, za, doa, sinks, cos, sa, sb, nseq):
    t = qkv.shape[0]
    nblk = min(ATTN_TILE, t // nseq // BLOCK)
    tile = nblk * BLOCK
    nt = t // nseq // tile

    def body(sink_ref, qkv_ref, kvp_ref, za_ref, doa_ref, cos_ref, sa_ref, sb_ref,
             dqkv_ref, dza_ref, dsink_ref, ck_ref, cv_ref):
        s_id, i = pl.program_id(0), pl.program_id(1)

        @pl.when((s_id == 0) & (i == 0))
        def _():
            dsink_ref[...] = jnp.zeros_like(dsink_ref)

        @pl.when(i == 0)
        def _():
            ck_ref[...] = jnp.zeros_like(ck_ref)
            cv_ref[...] = jnp.zeros_like(cv_ref)

        carry_k, carry_v = ck_ref[...], cv_ref[...]
        for j in reversed(range(nblk)):
            rows = slice(j * BLOCK, (j + 1) * BLOCK)
            ops = _attn_operands(*_attn_kv(qkv_ref, kvp_ref, j), True)
            lo = ops["lo"]
            valid = _attn_valid(nblk * (nt - 1 - i) + j)
            dk_acc, dv_acc, dq_pairs = [], [], []
            for g in range(A_KV_HEADS):
                pairs = [slice((2 * g + p) * LANE, (2 * g + p + 1) * LANE) for p in range(2)]
                q_both = jnp.concatenate([qkv_ref[rows, p] for p in pairs], axis=0)
                q_f = q_both.astype(_F32)
                z = [za_ref[rows, p].astype(_F32) for p in pairs]
                sz = [_sigmoid(v) for v in z]
                d_oa = [doa_ref[rows, p].astype(_F32) for p in pairs]
                d_att = jnp.concatenate([d_oa[p] * (z[p] * sz[p]) for p in range(2)], axis=0)
                zero = jnp.zeros_like(d_att)
                ot, dqt, ds_all, pn_all, qz_all, daz_all = None, None, [], [], [], []
                for half in range(2):
                    heads = (4 * g + half, 4 * g + 2 + half)
                    e, e_sink, inv = _attn_softmax_t(ops["k_lanes"][g][half], q_both, valid,
                                                     _attn_sinks(sink_ref, *heads))
                    pn = e * inv
                    dpt = _dot_nt(ops["v_lanes"][g][half], d_att)
                    delta = jnp.sum(pn * dpt, axis=0, keepdims=True)
                    ds = (pn * (dpt - delta)).astype(_MX)
                    pn = pn.astype(_MX)
                    d_sink = e_sink * inv * delta
                    for p, h in enumerate(heads):
                        dsink_ref[h:h + 1, :] = dsink_ref[h:h + 1, :] - jnp.sum(d_sink[:, p * BLOCK:(p + 1) * BLOCK])
                    o_part = _dot(ops["v_rows"][g][half], pn)
                    dq_part = _dot(ops["k_rows"][g][half], ds)
                    ot = o_part if ot is None else ot + o_part
                    dqt = dq_part if dqt is None else dqt + dq_part
                    mine = lo if half == 0 else jnp.logical_not(lo)
                    ds_all.append(ds)
                    pn_all.append(pn)
                    qz_all.append(jnp.where(mine, q_f, zero).astype(_MX))
                    daz_all.append(jnp.where(mine, d_att, zero).astype(_MX))
                dk_acc.append(_dot(jnp.concatenate(ds_all, axis=1), jnp.concatenate(qz_all, axis=0)))
                dv_acc.append(_dot(jnp.concatenate(pn_all, axis=1), jnp.concatenate(daz_all, axis=0)))
                for p, lanes in enumerate(pairs):
                    cols = slice(p * BLOCK, (p + 1) * BLOCK)
                    dza_ref[rows, lanes] = (d_oa[p] * ot[:, cols].T * (sz[p] * (1.0 + z[p] * (1.0 - sz[p])))).astype(_ST)
                    dq_pairs.append(dqt[:, cols].T)

            def fold(acc, scale):
                both = [a + pltpu.roll(a, 64, 1) for a in acc]
                return jnp.where(lo, both[0], both[1]) * scale

            dk_full = fold(dk_acc, A_HEAD_DIM ** -0.5)
            dv_full = fold(dv_acc, 1.0)
            dk_cur, dv_cur = dk_full[BLOCK:] + carry_k, dv_full[BLOCK:] + carry_v
            carry_k, carry_v = dk_full[:BLOCK], dv_full[:BLOCK]
            c, s1, s2 = cos_ref[rows, :], sa_ref[rows, :], sb_ref[rows, :]
            dqkv_ref[rows, 0:512] = _rope(jnp.concatenate(dq_pairs, axis=1), c, s1, s2, -1.0).astype(_ST)
            dqkv_ref[rows, 512:640] = _rope(dk_cur, c, s1, s2, -1.0).astype(_ST)
            dqkv_ref[rows, 640:768] = dv_cur.astype(_ST)
        ck_ref[...] = carry_k
        cv_ref[...] = carry_v

    cur = lambda w: pl.BlockSpec((tile, w), lambda s, i: (s * nt + nt - 1 - i, 0))
    return pl.pallas_call(
        body, name="attn_bwd", grid=(nseq, nt),
        in_specs=[pl.BlockSpec(memory_space=pltpu.SMEM), cur(768),
                  pl.BlockSpec((BLOCK, 256),
                               lambda s, i: (nblk * (s * nt + nt - 1 - i) - jnp.minimum(nt - 1 - i, 1), 2)),
                  cur(512), cur(512), cur(LANE), cur(LANE), cur(LANE)],
        out_specs=[cur(768), cur(512), pl.BlockSpec((8, LANE), lambda s, i: (0, 0))],
        out_shape=[jax.ShapeDtypeStruct((t, 768), _ST), jax.ShapeDtypeStruct((t, 512), _ST),
                   jax.ShapeDtypeStruct((8, LANE), _F32)],
        scratch_shapes=[pltpu.VMEM((BLOCK, A_KV_WIDTH), _F32), pltpu.VMEM((BLOCK, A_KV_WIDTH), _F32)],
        compiler_params=_cparams(("arbitrary", "arbitrary")),
    )(sinks, qkv, qkv, za, doa, cos, sa, sb)


def _gla_chunk_terms(la, qkb_ref, r0):
    g = la[r0:r0 + B_CHUNK, :]
    ri = lax.broadcasted_iota(jnp.int32, (B_CHUNK, B_CHUNK), 0)
    ci = lax.broadcasted_iota(jnp.int32, (B_CHUNK, B_CHUNK), 1)
    cum = _dot_ones((ri >= ci).astype(_F32), g)
    last = cum[B_CHUNK - 1:B_CHUNK, :]
    mid = cum[B_CHUNK // 2 - 1:B_CHUNK // 2, :]
    q = qkb_ref[r0:r0 + B_CHUNK, 0:B_KEY_WIDTH].astype(_F32) * (B_KEY_DIM ** -0.5)
    k = qkb_ref[r0:r0 + B_CHUNK, B_KEY_WIDTH:2 * B_KEY_WIDTH].astype(_F32)
    e_q, e_k, e_l, e_c = jnp.exp(cum - mid), jnp.exp(mid - cum), jnp.exp(last - cum), jnp.exp(cum)
    dec_col = jnp.exp(jnp.sum(g.T, axis=1, keepdims=True))
    return dict(qm=q * e_q, km=k * e_k, kl=k * e_l, qc=q * e_c, e_q=e_q, e_k=e_k, e_l=e_l, e_c=e_c,
                dec_col=dec_col, dec_row=jnp.exp(last), causal=ri >= ci, ri=ri)


def _gate_logits(alr_ref, wup_ref, b_ref):
    return _dot(alr_ref[...], wup_ref[...]) + b_ref[...]


def _gla_fwd(qkb, vb, zb, alr, wup, b_alpha, gn, nseq):
    t = qkb.shape[0]
    tb = min(GLA_FWD_BLOCK, t // nseq)
    nblk = t // nseq // tb
    cpb = tb // B_CHUNK

    def body(qkb_ref, vb_ref, zb_ref, alr_ref, wup_ref, b_ref, gn_ref, ob_ref, oraw_ref, sst_ref, s_ref):
        @pl.when(pl.program_id(1) == 0)
        def _():
            s_ref[...] = jnp.zeros_like(s_ref)

        la = _log_sigmoid(_gate_logits(alr_ref, wup_ref, b_ref)) * (1.0 / B_GATE_TEMP)
        terms = [_gla_chunk_terms(la, qkb_ref, c * B_CHUNK) for c in range(cpb)]
        o_intra, inc = {}, {}
        for c, tm in enumerate(terms):
            for h in range(B_HEADS):
                kl_, vl_ = slice(h * 64, (h + 1) * 64), slice(h * 128, (h + 1) * 128)
                v = vb_ref[c * B_CHUNK:(c + 1) * B_CHUNK, vl_]
                a = jnp.where(tm["causal"], _dot_nt(tm["qm"][:, kl_], tm["km"][:, kl_]), 0.0)
                o_intra[c, h] = _dot(a, v)
                inc[c, h] = _dot_tn(tm["kl"][:, kl_], v)
        o_heads = {}
        for h in range(B_HEADS):
            kl_ = slice(h * 64, (h + 1) * 64)
            st = s_ref[kl_, :]
            for c, tm in enumerate(terms):
                sst_ref[c, kl_, :] = st
                o_heads[c, h] = o_intra[c, h] + _dot(tm["qc"][:, kl_], st)
                st = tm["dec_col"][kl_, :] * st + inc[c, h]
            s_ref[kl_, :] = st
        o = jnp.concatenate([jnp.concatenate([o_heads[c, h] for h in range(B_HEADS)], axis=1)
                             for c in range(cpb)], axis=0)
        oraw_ref[...] = o
        z = zb_ref[...].astype(_F32)
        gate = z * _sigmoid(z)
        for h in range(B_HEADS):
            vl_ = slice(h * 128, (h + 1) * 128)
            oh = o[:, vl_]
            r = lax.rsqrt(jnp.mean(oh * oh, axis=-1, keepdims=True) + NORM_EPS)
            ob_ref[:, vl_] = ((oh * r) * gn_ref[:, vl_] * gate[:, vl_]).astype(_ST)

    rows = lambda w: pl.BlockSpec((tb, w), lambda s, i: (s * nblk + i, 0))
    full = lambda a, b: pl.BlockSpec((a, b), lambda s, i: (0, 0))
    return pl.pallas_call(
        body, name="gla_fwd", grid=(nseq, nblk),
        in_specs=[rows(512), rows(512), rows(512), rows(LANE), full(LANE, B_KEY_WIDTH),
                  full(1, B_KEY_WIDTH), full(1, B_WIDTH)],
        out_specs=[rows(512), rows(512),
                   pl.BlockSpec((cpb, B_KEY_WIDTH, B_VAL_DIM), lambda s, i: (s * nblk + i, 0, 0))],
        out_shape=[jax.ShapeDtypeStruct((t, B_WIDTH), _ST), jax.ShapeDtypeStruct((t, B_WIDTH), _F32),
                   jax.ShapeDtypeStruct((t // B_CHUNK, B_KEY_WIDTH, B_VAL_DIM), _F32)],
        scratch_shapes=[pltpu.VMEM((B_KEY_WIDTH, B_VAL_DIM), _F32)],
        compiler_params=_cparams(("parallel", "arbitrary")),
    )(qkb, vb, zb, alr, wup, b_alpha, gn)


def _gla_bwd(qkb, vb, zb, alr, oraw, dob, sst, wup, b_alpha, gn, nseq):
    t = qkb.shape[0]
    tb = min(GLA_BLOCK, t // nseq)
    tile = min(GLA_BWD_TILE, t // nseq // tb)
    nstep = t // nseq // (tile * tb)
    cpb = tb // B_CHUNK

    def body(qkb_ref, vb_ref, zb_ref, alr_ref, oraw_ref, dob_ref, sst_ref, wup_ref, b_ref, gn_ref,
             dqkb_ref, dvb_ref, dzb_ref, dalr_ref, dwup_ref, db_ref, dgn_ref, ds_ref):
        s_id, i = pl.program_id(0), pl.program_id(1)

        @pl.when((s_id == 0) & (i == 0))
        def _():
            dwup_ref[...] = jnp.zeros_like(dwup_ref)
            db_ref[...] = jnp.zeros_like(db_ref)
            dgn_ref[...] = jnp.zeros_like(dgn_ref)

        @pl.when(i == 0)
        def _():
            ds_ref[...] = jnp.zeros_like(ds_ref)

        for sb in reversed(range(tile)):
            one_block(sb, qkb_ref, vb_ref, zb_ref, alr_ref, oraw_ref, dob_ref, sst_ref, wup_ref, b_ref, gn_ref,
                      dqkb_ref, dvb_ref, dzb_ref, dalr_ref, dwup_ref, db_ref, dgn_ref, ds_ref)

    def one_block(sb, qkb_ref, vb_ref, zb_ref, alr_ref, oraw_ref, dob_ref, sst_ref, wup_ref, b_ref, gn_ref,
                  dqkb_ref, dvb_ref, dzb_ref, dalr_ref, dwup_ref, db_ref, dgn_ref, ds_ref):
        base = sb * tb
        rows = slice(base, base + tb)
        a_pre = _dot(alr_ref[rows, :], wup_ref[...]) + b_ref[...]
        la = _log_sigmoid(a_pre) * (1.0 / B_GATE_TEMP)

        z = zb_ref[rows, :].astype(_F32)
        sz = _sigmoid(z)
        d_ob = dob_ref[rows, :].astype(_F32)
        tg = d_ob * (z * sz)
        dsilu = sz * (1.0 + z * (1.0 - sz))
        do_cols, dgn_cols = [], []
        for h in range(B_HEADS):
            vl_ = slice(h * 128, (h + 1) * 128)
            oh = oraw_ref[rows, vl_].astype(_F32)
            r = lax.rsqrt(jnp.mean(oh * oh, axis=-1, keepdims=True) + NORM_EPS)
            on = oh * r
            gnh = gn_ref[:, vl_]
            dzb_ref[rows, vl_] = (d_ob[:, vl_] * (on * gnh) * dsilu[:, vl_]).astype(_ST)
            dgn_cols.append(jnp.sum(tg[:, vl_] * on, axis=0, keepdims=True))
            do_cols.append(_rms_bwd(tg[:, vl_] * gnh, on, r))
        dgn_ref[...] = dgn_ref[...] + jnp.concatenate(dgn_cols, axis=1)
        d_o = jnp.concatenate(do_cols, axis=1)

        ri = lax.broadcasted_iota(jnp.int32, (tb, tb), 0)
        ci = lax.broadcasted_iota(jnp.int32, (tb, tb), 1)
        same = (ri // B_CHUNK) == (ci // B_CHUNK)
        low = same & (ri >= ci)
        cum = _dot_ones(low.astype(_F32), la)
        at_row = lambda r: jnp.concatenate([jnp.broadcast_to(cum[c * B_CHUNK + r:c * B_CHUNK + r + 1], (B_CHUNK, B_KEY_WIDTH))
                                            for c in range(cpb)], axis=0)
        last, mid = at_row(B_CHUNK - 1), at_row(B_CHUNK // 2 - 1)
        e_q, e_k, e_l, e_c = jnp.exp(cum - mid), jnp.exp(mid - cum), jnp.exp(last - cum), jnp.exp(cum)
        q = qkb_ref[rows, 0:B_KEY_WIDTH] * (B_KEY_DIM ** -0.5)
        k = qkb_ref[rows, B_KEY_WIDTH:2 * B_KEY_WIDTH]
        qm, km, kl, qc = q * e_q, k * e_k, k * e_l, q * e_c
        lane_head = lax.broadcasted_iota(jnp.int32, (1, B_KEY_WIDTH), 1) // B_KEY_DIM
        d_o_mx = d_o.astype(_MX)

        def on_diagonal(st):
            z = jnp.zeros((B_KEY_DIM, B_VAL_DIM), st.dtype)
            return jnp.concatenate([jnp.concatenate(
                [st[h * B_KEY_DIM:(h + 1) * B_KEY_DIM] if g == h else z for g in range(B_HEADS)], axis=1)
                for h in range(B_HEADS)], axis=0)

        def diagonal_of(full):
            return jnp.concatenate([full[h * B_KEY_DIM:(h + 1) * B_KEY_DIM, h * B_VAL_DIM:(h + 1) * B_VAL_DIM]
                                    for h in range(B_HEADS)], axis=0)

        dqm, dkm, dv_cols = None, None, []
        for h in range(B_HEADS):
            vl_ = slice(h * B_VAL_DIM, (h + 1) * B_VAL_DIM)
            mine = lane_head == h
            qz, kz = jnp.where(mine, qm, 0.0).astype(_MX), jnp.where(mine, km, 0.0).astype(_MX)
            a = jnp.where(low, _dot_nt(qz, kz), 0.0).astype(_MX)
            da = jnp.where(low, _dot_nt(d_o_mx[:, vl_], vb_ref[rows, vl_]), 0.0).astype(_MX)
            dqm_h, dkm_h = _dot(da, kz), _dot_tn(da, qz)
            dqm = dqm_h if dqm is None else dqm + dqm_h
            dkm = dkm_h if dkm is None else dkm + dkm_h
            dv_cols.append(_dot_tn(a, d_o_mx[:, vl_]))
        dv = jnp.concatenate(dv_cols, axis=1)

        chunk = [slice(c * B_CHUNK, (c + 1) * B_CHUNK) for c in range(cpb)]
        dqc_rows, g_loc = [], []
        for c in range(cpb):
            dqc_rows.append(_dot_nt(d_o_mx[chunk[c]], on_diagonal(sst_ref[sb * cpb + c].astype(_MX))))
            g_loc.append(diagonal_of(_dot_tn(qc[chunk[c]], d_o_mx[chunk[c]])))
        cur = ds_ref[...]
        d_state = [None] * cpb
        for c in reversed(range(cpb)):
            d_state[c] = cur
            cur = g_loc[c] + jnp.exp(jnp.sum(la[chunk[c]].T, axis=1, keepdims=True)) * cur
        ds_ref[...] = cur
        dkl_rows, dv_rows, dlast_rows = [], [], []
        ones8 = jnp.ones((8, B_VAL_DIM), _F32)
        for c in range(cpb):
            dsd = on_diagonal(d_state[c].astype(_MX))
            dkl_c = _dot_nt(vb_ref[base + c * B_CHUNK:base + (c + 1) * B_CHUNK, :], dsd)
            dkl_rows.append(dkl_c)
            dv_rows.append(_dot(kl[chunk[c]], dsd))
            prod = d_state[c] * sst_ref[sb * cpb + c]
            p0 = prod.astype(jnp.bfloat16)
            p1 = (prod - p0.astype(_F32)).astype(jnp.bfloat16)
            ddec = (_dot_nt(ones8, p0) + _dot_nt(ones8, p1))[0:1]
            r_last = c * B_CHUNK + B_CHUNK - 1
            dlast = jnp.sum(dkl_c * kl[chunk[c]], axis=0, keepdims=True) + ddec * jnp.exp(last[r_last:r_last + 1])
            dlast_rows.append(jnp.broadcast_to(dlast, (B_CHUNK, B_KEY_WIDTH)))
        dqc, dkl = jnp.concatenate(dqc_rows, axis=0), jnp.concatenate(dkl_rows, axis=0)
        dqkb_ref[rows, 0:B_KEY_WIDTH] = ((dqm * e_q + dqc * e_c) * (B_KEY_DIM ** -0.5)).astype(_ST)
        dqkb_ref[rows, B_KEY_WIDTH:2 * B_KEY_WIDTH] = (dkm * e_k + dkl * e_l).astype(_ST)
        dvb_ref[rows, :] = (dv + jnp.concatenate(dv_rows, axis=0)).astype(_ST)
        dcum = dqm * qm - dkm * km + dqc * qc - dkl * kl
        row = lax.broadcasted_iota(jnp.int32, (tb, B_KEY_WIDTH), 0)
        dcum = jnp.where(row % B_CHUNK == B_CHUNK - 1, dcum + jnp.concatenate(dlast_rows, axis=0), dcum)
        dla = _dot_ones((same & (ri <= ci)).astype(_F32), dcum)

        da_pre = dla * (1.0 / B_GATE_TEMP) * (1.0 - _sigmoid(a_pre))
        dalr_ref[rows, :] = _dot_nt(da_pre, wup_ref[...]).astype(_ST)
        dwup_ref[...] = dwup_ref[...] + _dot_tn(alr_ref[rows, :], da_pre)
        db_ref[...] = db_ref[...] + jnp.sum(da_pre, axis=0, keepdims=True)

    blk = lambda s, i: s * nstep + nstep - 1 - i
    rows = lambda w: pl.BlockSpec((tile * tb, w), lambda s, i: (blk(s, i), 0))
    full = lambda a, b: pl.BlockSpec((a, b), lambda s, i: (0, 0))
    act = lambda w: jax.ShapeDtypeStruct((t, w), _ST)
    return pl.pallas_call(
        body, name="gla_bwd", grid=(nseq, nstep),
        in_specs=[rows(512), rows(512), rows(512), rows(LANE), rows(512), rows(512),
                  pl.BlockSpec((tile * cpb, B_KEY_WIDTH, B_VAL_DIM), lambda s, i: (blk(s, i), 0, 0)),
                  full(LANE, B_KEY_WIDTH), full(1, B_KEY_WIDTH), full(1, B_WIDTH)],
        out_specs=[rows(512), rows(512), rows(512), rows(LANE), full(LANE, B_KEY_WIDTH),
                   full(1, B_KEY_WIDTH), full(1, B_WIDTH)],
        out_shape=[act(512), act(512), act(512), act(LANE),
                   jax.ShapeDtypeStruct((LANE, B_KEY_WIDTH), _F32),
                   jax.ShapeDtypeStruct((1, B_KEY_WIDTH), _F32), jax.ShapeDtypeStruct((1, B_WIDTH), _F32)],
        scratch_shapes=[pltpu.VMEM((B_KEY_WIDTH, B_VAL_DIM), _F32)],
        compiler_params=_cparams(("arbitrary", "arbitrary")),
    )(qkb, vb, zb, alr, oraw, dob, sst, wup, b_alpha, gn)


def _merge_loss(oa, ob, ga, gb, x2, tgt, wa, wb, wo, g_final):
    t = x2.shape[0]
    tm = min(t, 512)
    nt = t // tm

    def body(oa_ref, ob_ref, ga_ref, gb_ref, x_ref, t_ref, wa_ref, wb_ref, wo_ref, gf_ref,
             dh_ref, doa_ref, dob_ref, dga_ref, dgb_ref, dwa_ref, dwb_ref, dwo_ref, dgf_ref, loss_ref,
             ya_s, yb_s, out_s, dmer_s, mrg_s, dya_s, dyb_s):
        first = pl.program_id(0) == 0
        so_far = lambda ref: jnp.where(first, 0.0, ref[...])

        slabs = [slice(s, s + MERGE_SLAB) for s in range(0, tm, MERGE_SLAB)]
        fold = lambda a: a[0:8] + a[8:16]
        ya_s[...] = _dot(oa_ref[...], wa_ref[...])
        yb_s[...] = _dot(ob_ref[...], wb_ref[...])
        for rows_ in slabs:
            sga, sgb = _sigmoid(ga_ref[rows_, :].astype(_F32)), _sigmoid(gb_ref[rows_, :].astype(_F32))
            mrg_s[rows_, :] = (sga * ya_s[rows_, :] + sgb * yb_s[rows_, :]).astype(_MX)
        out_s[...] = x_ref[...] + _dot(mrg_s[...], wo_ref[...])
        gf = gf_ref[...]
        loss8 = jnp.zeros((8, D_MODEL), _F32)
        dgf8 = jnp.zeros((8, D_MODEL), _F32)
        for rows_ in slabs:
            out = out_s[rows_, :]
            r = lax.rsqrt(jnp.mean(out * out, axis=-1, keepdims=True) + NORM_EPS)
            nrm = out * r
            err = nrm * gf - t_ref[rows_, :]
            loss8 = loss8 + fold(err * err)
            dy = err * (1.0 / D_MODEL)
            dgf8 = dgf8 + fold(dy * nrm)
            dh = _rms_bwd(dy * gf, nrm, r)
            dh_ref[rows_, :] = dh.astype(_ST)
        loss_ref[...] = so_far(loss_ref) + (0.5 / D_MODEL) * jnp.sum(loss8, axis=0, keepdims=True)
        dgf_ref[...] = so_far(dgf_ref) + jnp.sum(dgf8, axis=0, keepdims=True)
        dmer_s[...] = _dot_nt(dh_ref[...], wo_ref[...])
        dwo_ref[...] = so_far(dwo_ref) + _dot_tn(mrg_s[...], dh_ref[...])
        for rows_ in slabs:
            sga, sgb = _sigmoid(ga_ref[rows_, :].astype(_F32)), _sigmoid(gb_ref[rows_, :].astype(_F32))
            dmer = dmer_s[rows_, :]
            da, db = dmer * sga, dmer * sgb
            dya_s[rows_, :] = da.astype(_MX)
            dyb_s[rows_, :] = db.astype(_MX)
            dga_ref[rows_, :] = (da * ya_s[rows_, :] * (1.0 - sga)).astype(_ST)
            dgb_ref[rows_, :] = (db * yb_s[rows_, :] * (1.0 - sgb)).astype(_ST)
        doa_ref[...] = _dot_nt(dya_s[...], wa_ref[...]).astype(_ST)
        dob_ref[...] = _dot_nt(dyb_s[...], wb_ref[...]).astype(_ST)
        dwa_ref[...] = so_far(dwa_ref) + _dot_tn(oa_ref[...], dya_s[...])
        dwb_ref[...] = so_far(dwb_ref) + _dot_tn(ob_ref[...], dyb_s[...])

    rows = lambda w: pl.BlockSpec((tm, w), lambda i: (i, 0))
    full = lambda a, b: pl.BlockSpec((a, b), lambda i: (0, 0), pipeline_mode=pl.Buffered(1))
    return pl.pallas_call(
        body, name="merge_loss", grid=(nt,),
        in_specs=[rows(512), rows(512), rows(D_MODEL), rows(D_MODEL), rows(D_MODEL), rows(D_MODEL),
                  full(A_WIDTH, D_MODEL), full(B_WIDTH, D_MODEL), full(D_MODEL, D_MODEL), full(1, D_MODEL)],
        out_specs=[rows(D_MODEL), rows(512), rows(512), rows(D_MODEL), rows(D_MODEL),
                   full(A_WIDTH, D_MODEL), full(B_WIDTH, D_MODEL), full(D_MODEL, D_MODEL),
                   full(1, D_MODEL), full(1, D_MODEL)],
        out_shape=[jax.ShapeDtypeStruct((t, D_MODEL), _ST), jax.ShapeDtypeStruct((t, 512), _ST),
                   jax.ShapeDtypeStruct((t, 512), _ST), jax.ShapeDtypeStruct((t, D_MODEL), _ST),
                   jax.ShapeDtypeStruct((t, D_MODEL), _ST),
                   jax.ShapeDtypeStruct((A_WIDTH, D_MODEL), _F32), jax.ShapeDtypeStruct((B_WIDTH, D_MODEL), _F32),
                   jax.ShapeDtypeStruct((D_MODEL, D_MODEL), _F32), jax.ShapeDtypeStruct((1, D_MODEL), _F32),
                   jax.ShapeDtypeStruct((1, D_MODEL), _F32)],
        scratch_shapes=[pltpu.VMEM((tm, D_MODEL), _F32)] * 4 + [pltpu.VMEM((tm, D_MODEL), _MX)] * 3,
        compiler_params=_cparams(("arbitrary",), VMEM_LIMIT),
    )(oa, ob, ga, gb, x2, tgt, wa, wb, wo, g_final)


def _in_proj_bwd_x(dpieces, wt, x2, dh2, g_in, after):
    t = x2.shape[0]
    tm = min(t, 512)
    np_ = len(PIECES)

    def body(*refs):
        dp_refs = refs[:np_]
        w_ref, x_ref, dh2_ref, g_ref, after_ref, gx_ref, dg_ref = refs[np_:]

        @pl.when(pl.program_id(0) == 0)
        def _():
            dg_ref[...] = jnp.zeros_like(dg_ref)

        dh = None
        for (name, a, b), dp in zip(PIECES, dp_refs):
            part = _dot(dp[...], w_ref[a:b, :])
            dh = part if dh is None else dh + part
        xv = x_ref[...]
        r = lax.rsqrt(jnp.mean(xv * xv, axis=-1, keepdims=True) + NORM_EPS)
        nrm = xv * r
        dg_ref[...] = dg_ref[...] + jnp.sum(dh * nrm, axis=0, keepdims=True)
        gx_ref[...] = dh2_ref[...].astype(_F32) + _rms_bwd(dh * g_ref[...], nrm, r)

    rows = lambda w: pl.BlockSpec((tm, w), lambda i: (i, 0))
    full = lambda a, b: pl.BlockSpec((a, b), lambda i: (0, 0), pipeline_mode=pl.Buffered(1))
    return pl.pallas_call(
        body, name="in_proj_bwd_x", grid=(t // tm,),
        in_specs=[rows(b - a) for _, a, b in PIECES] + [full(D_IN, D_MODEL), rows(D_MODEL), rows(D_MODEL),
                                                          full(1, D_MODEL), _ANY],
        out_specs=[rows(D_MODEL), full(1, D_MODEL)],
        out_shape=[jax.ShapeDtypeStruct((t, D_MODEL), _F32), jax.ShapeDtypeStruct((1, D_MODEL), _F32)],
        compiler_params=_cparams(("arbitrary",), VMEM_LIMIT),
    )(*dpieces, wt, x2, dh2, g_in, after)


def _in_proj_bwd_w(h, dpieces, others, osplits):
    t = h.shape[0]
    tm = min(t, 1024)
    nt = t // tm
    np_, no = len(PIECES), len(others)
    half = D_MODEL // 2

    def body(*refs):
        h_ref, dp_refs, o_refs = refs[0], refs[1:1 + np_], refs[1 + np_:1 + np_ + no]
        mine_ref, theirs_ref = refs[1 + np_ + no:3 + np_ + no]
        r_refs = refs[3 + np_ + no:3 + np_ + 2 * no]
        acc_ref, stage_ref, keep_sem, send, recv, o_send, o_recv = refs[3 + np_ + 2 * no:]
        i = pl.program_id(0)
        x, y, c = _place()
        sibling = (x, y, 1 - c)
        early = [pltpu.make_async_remote_copy(
            src_ref=osplits[k].half(o_refs[k], 1 - c), dst_ref=r_refs[k], send_sem=o_send.at[k], recv_sem=o_recv.at[k],
            device_id=sibling, device_id_type=_MESH) for k in range(no)]

        @pl.when(i == 0)
        def _():
            for cp in early:
                cp.start()

        hv = h_ref[...]
        cols = lambda core: pl.ds(pl.multiple_of(core * half, LANE), half)
        writes = []
        by_size = sorted(range(np_), key=lambda j: PIECES[j][1] - PIECES[j][2])
        for j, ((name, a, b), dp) in [(j, (PIECES[j], dp_refs[j])) for j in by_size]:
            part = _dot_tn(dp[...], hv)
            if name == "alr":
                b = a + B_GATE_RANK
                part = part[0:B_GATE_RANK]
            acc_ref[a:b, :] = jnp.where(i == 0, 0.0, acc_ref[a:b, :]) + part
            keep = pltpu.make_async_copy(acc_ref.at[a:b, cols(c)], mine_ref.at[a:b], keep_sem.at[j])
            give = pltpu.make_async_remote_copy(
                src_ref=stage_ref.at[a:b], dst_ref=theirs_ref.at[a:b], send_sem=send.at[j],
                recv_sem=recv.at[j], device_id=sibling, device_id_type=_MESH)
            writes += [keep, give]

            @pl.when(i == nt - 1)
            def _(keep=keep, give=give, a=a, b=b):
                keep.start()
                stage_ref[a:b, :] = jnp.where(c == 0, acc_ref[a:b, half:], acc_ref[a:b, :half]).astype(_MX)
                give.start()

        @pl.when(i == nt - 1)
        def _():
            for cp in writes + early:
                cp.wait()

    rows = lambda w: pl.BlockSpec((tm, w), lambda i: (i, 0))
    halves = [jax.ShapeDtypeStruct((D_IN, half), _F32), jax.ShapeDtypeStruct((D_IN, half), _MX)]
    return pl.pallas_call(
        body, name="in_proj_bwd_w", grid=(nt,),
        in_specs=[rows(D_MODEL)] + [rows(b - a) for _, a, b in PIECES] + [_ANY] * no,
        out_specs=[_ANY] * (2 + no),
        out_shape=halves + [jax.ShapeDtypeStruct(sp.half_shape(g.shape), g.dtype) for g, sp in zip(others, osplits)],
        scratch_shapes=[pltpu.VMEM((D_IN, D_MODEL), _F32), pltpu.VMEM((D_IN, half), _MX),
                        pltpu.SemaphoreType.DMA((np_,)), pltpu.SemaphoreType.DMA((np_,)), pltpu.SemaphoreType.DMA((np_,)),
                        pltpu.SemaphoreType.DMA((no,)), pltpu.SemaphoreType.DMA((no,))],
        compiler_params=_cparams(("arbitrary",), VMEM_LIMIT),
    )(h, *dpieces, *others)


def _place():
    return lax.axis_index("x"), lax.axis_index("y"), lax.axis_index("c")


def _other_chips(x, y):
    return [(1 - x, y), (x, 1 - y), (1 - x, 1 - y)]


class _Split(NamedTuple):
    by_rows: bool
    step: int
    size: int

    def half(self, ref, c):
        r, n = ref.shape[-2:]
        if self.by_rows:
            return ref.at[:, pl.ds(pl.multiple_of(c * (n // 2), LANE), n // 2)]
        return ref.at[pl.ds(pl.multiple_of(c * (r // 2), 16), r // 2), :]

    def chip_part(self, ref, k):
        if self.by_rows:
            return ref.at[pl.ds(pl.multiple_of(k * self.step, 16), self.size), :]
        return ref.at[:, pl.ds(pl.multiple_of(k * self.size, LANE), self.size)]

    def half_shape(self, shape):
        r, n = shape
        return (r, n // 2) if self.by_rows else (r // 2, n)

    def part_shape(self, shape):
        r, n = shape
        return (self.size, n) if self.by_rows else (r, self.size)


SPLIT_W_IN_T = _Split(True, WINDOW_STEP, WINDOW_ROWS)
SPLIT_W_O = _Split(True, 256, 256)
SPLIT_W_OUT = _Split(False, 256, 256)


def _gather_weights(shards, splits, fulls, pos_f):
    nw = len(shards)
    t = pos_f.shape[0]

    def body(*refs):
        ins, (pos_ref, c_ref) = refs[:nw], refs[nw:nw + 2]
        outs, tables = refs[nw + 2:2 * nw + 2], refs[2 * nw + 2:2 * nw + 5]
        send_a, recv_a, send_b, recv_b = refs[2 * nw + 5:]
        x, y, c = _place()
        me = 2 * x + y
        peers = _other_chips(x, y)

        def place(i, k, half):
            if splits[i] is None:
                return outs[i].at[k]
            if fulls[i][0] == 4 and len(fulls[i]) == 3:
                whole = outs[i].at[k]
            else:
                whole = splits[i].chip_part(outs[i], k)
            return splits[i].half(whole, half)

        first, passed = [], []
        for i in range(nw):
            src = ins[i] if splits[i] is None else splits[i].half(ins[i], c)
            for j, (px, py) in enumerate(peers):
                cp = pltpu.make_async_remote_copy(
                    src_ref=src, dst_ref=place(i, me, c), send_sem=send_a.at[3 * i + j],
                    recv_sem=recv_a.at[3 * i + j], device_id=(px, py, c), device_id_type=_MESH)
                cp.start()
                first.append(cp)
        _rope_tables_into(pos_ref, c_ref, *tables)
        for i in range(nw):
            for j, (px, py) in enumerate(peers):
                landed = place(i, 2 * px + py, c)
                pltpu.make_async_remote_copy(
                    src_ref=landed, dst_ref=landed, send_sem=send_a.at[3 * i + j], recv_sem=recv_a.at[3 * i + j],
                    device_id=(px, py, c), device_id_type=_MESH).wait_recv()
                if splits[i] is not None:
                    cp = pltpu.make_async_remote_copy(
                        src_ref=landed, dst_ref=landed, send_sem=send_b.at[3 * i + j], recv_sem=recv_b.at[3 * i + j],
                        device_id=(x, y, 1 - c), device_id_type=_MESH)
                    cp.start()
                    passed.append(cp)
        for i in range(nw):
            if splits[i] is None:
                continue
            for j, (px, py) in enumerate(peers):
                theirs = place(i, 2 * px + py, 1 - c)
                pltpu.make_async_remote_copy(
                    src_ref=theirs, dst_ref=theirs, send_sem=send_b.at[3 * i + j], recv_sem=recv_b.at[3 * i + j],
                    device_id=(x, y, 1 - c), device_id_type=_MESH).wait_recv()
        for cp in first + passed:
            cp.wait_send()

    vm = pl.BlockSpec(memory_space=pltpu.VMEM)
    tab = jax.ShapeDtypeStruct((t, LANE), _F32)
    return pl.pallas_call(
        body, name="gather_weights",
        in_specs=[_ANY] * nw + [vm, vm], out_specs=[_ANY] * nw + [vm] * 3,
        out_shape=[jax.ShapeDtypeStruct(f, s.dtype) for f, s in zip(fulls, shards)] + [tab] * 3,
        scratch_shapes=[pltpu.SemaphoreType.DMA((3 * nw,)) for _ in range(4)],
        compiler_params=_cparams(None, VMEM_LIMIT),
    )(*shards, pos_f, _rope_consts())


def _assemble_w_in_t(slots):
    bw = 256
    ov = WINDOW_ROWS - WINDOW_STEP

    def body(s_ref, o_ref):
        for k in range(4):
            base = k * WINDOW_STEP
            lo = 0 if k == 0 else ov
            if k > 0:
                o_ref[base:base + ov, :] = s_ref[k - 1, WINDOW_STEP:WINDOW_ROWS, :] + s_ref[k, 0:ov, :]
            hi = WINDOW_ROWS if k == 3 else WINDOW_STEP
            o_ref[base + lo:base + hi, :] = s_ref[k, lo:hi, :]

    return pl.pallas_call(
        body, name="assemble_w_in_t", grid=(D_MODEL // bw,),
        in_specs=[pl.BlockSpec((4, WINDOW_ROWS, bw), lambda i: (0, 0, i))],
        out_specs=pl.BlockSpec((D_IN, bw), lambda i: (0, i)),
        out_shape=jax.ShapeDtypeStruct((D_IN, D_MODEL), slots.dtype),
        compiler_params=_cparams(("parallel",)),
    )(slots)


def _row_block(rows):
    for cand in (976, 176, 256, 128):
        if rows % cand == 0:
            return cand
    return rows


def _pair_sum(g, r, split, c_arr, name):
    hr, hn = r.shape
    br = _row_block(hr)
    if split is None:
        g_spec = pl.BlockSpec((br, hn), lambda i, c_ref: (i, 0))
    elif split.by_rows:
        g_spec = pl.BlockSpec((br, hn), lambda i, c_ref: (i, c_ref[0]))
    else:
        g_spec = pl.BlockSpec((br, hn), lambda i, c_ref: (c_ref[0] * (hr // br) + i, 0))

    def body(c_ref, g_ref, r_ref, o_ref):
        o_ref[...] = (g_ref[...] + r_ref[...]).astype(o_ref.dtype)

    return pl.pallas_call(
        body, name=name,
        grid_spec=pltpu.PrefetchScalarGridSpec(
            num_scalar_prefetch=1, grid=(hr // br,),
            in_specs=[g_spec, pl.BlockSpec((br, hn), lambda i, c_ref: (i, 0))],
            out_specs=pl.BlockSpec((br, hn), lambda i, c_ref: (i, 0))),
        out_shape=jax.ShapeDtypeStruct(r.shape, _MX),
        compiler_params=_cparams(("parallel",)),
    )(c_arr, g, r)


def _pair_sums_whole(gs, rs, splits, c_arr):
    n = len(gs)

    def g_spec(r, split):
        at = (lambda i, c_ref: (0, c_ref[0])) if split.by_rows else (lambda i, c_ref: (c_ref[0], 0))
        return pl.BlockSpec(r.shape, at)

    def body(c_ref, *refs):
        for g_ref, r_ref, o_ref in zip(refs[:n], refs[n:2 * n], refs[2 * n:]):
            o_ref[...] = (g_ref[...] + r_ref[...]).astype(o_ref.dtype)

    whole = lambda r: pl.BlockSpec(r.shape, lambda i, c_ref: (0, 0))
    return pl.pallas_call(
        body, name="pair_sum_small",
        grid_spec=pltpu.PrefetchScalarGridSpec(
            num_scalar_prefetch=1, grid=(1,),
            in_specs=[g_spec(r, sp) for r, sp in zip(rs, splits)] + [whole(r) for r in rs],
            out_specs=[whole(r) for r in rs]),
        out_shape=[jax.ShapeDtypeStruct(r.shape, _MX) for r in rs],
        compiler_params=_cparams(("arbitrary",), VMEM_LIMIT),
    )(c_arr, *gs, *rs)


_HBM = pl.BlockSpec(memory_space=pltpu.HBM)
_SEM = pl.BlockSpec(memory_space=pltpu.SEMAPHORE)
_FLOWS = pltpu.SideEffectType.DATAFLOW_SIDE_EFFECTING


def _chip_exchange_copies(refs, send, recv, splits):
    nw = len(refs) // 2
    x, y, c = _place()
    me = 2 * x + y
    copies = []
    for i in range(nw):
        for px, py in _other_chips(x, y):
            copies.append((splits[i].chip_part(refs[i], 2 * px + py), refs[nw + i].at[me], (px, py, c)))
    return [pltpu.make_async_remote_copy(src_ref=src, dst_ref=dst, send_sem=send.at[k], recv_sem=recv.at[k],
                                         device_id=peer, device_id_type=_MESH)
            for k, (src, dst, peer) in enumerate(copies)]


def _late_gather_copies(refs, send, recv, splits):
    nw = len(refs) // 2
    x, y, c = _place()
    me = 2 * x + y
    copies = []
    for i in range(nw):
        for px, py in [(x, y)] + _other_chips(x, y):
            copies.append((refs[i], splits[i].chip_part(refs[nw + i], me), (px, py, c)))
    return [pltpu.make_async_remote_copy(src_ref=src, dst_ref=dst, send_sem=send.at[k], recv_sem=recv.at[k],
                                         device_id=peer, device_id_type=_MESH)
            for k, (src, dst, peer) in enumerate(copies)]


def _start_copies(name, flying, copies_of, n_copies, after=None):
    first = [] if after is None else [after]

    def body(*refs):
        ins = refs[:len(flying)]
        send, recv = refs[len(flying) + len(first):len(flying) + len(first) + 2]
        token = refs[-1]
        for cp in copies_of(ins, send, recv):
            cp.start()
        token[...] = jnp.zeros_like(token)

    outs = pl.pallas_call(
        body, name=name,
        in_specs=[_HBM] * len(flying) + [_ANY] * len(first),
        out_specs=[_SEM, _SEM] + [_HBM] * len(flying) + [pl.BlockSpec(memory_space=pltpu.VMEM)],
        out_shape=[pltpu.SemaphoreType.DMA((n_copies,)), pltpu.SemaphoreType.DMA((n_copies,))]
        + [pltpu.HBM(f.shape, f.dtype) for f in flying] + [jax.ShapeDtypeStruct((8, LANE), _F32)],
        input_output_aliases={i: 2 + i for i in range(len(flying))},
        compiler_params=pltpu.CompilerParams(has_side_effects=_FLOWS),
    )(*[pltpu.with_memory_space_constraint(f, pltpu.HBM) for f in flying], *first)
    return outs[0], outs[1], outs[2:2 + len(flying)], outs[-1]


def _wait_copies(name, send, recv, flying, copies_of, after):
    def body(*refs):
        ins = refs[:len(flying)]
        send_ref, recv_ref = refs[len(flying):len(flying) + 2]
        for cp in copies_of(ins, send_ref, recv_ref):
            cp.wait_send()
            cp.wait_recv()

    return pl.pallas_call(
        body, name=name,
        in_specs=[_HBM] * len(flying) + [_SEM, _SEM, _ANY],
        out_specs=[_HBM] * len(flying),
        out_shape=[pltpu.HBM(f.shape, f.dtype) for f in flying],
        input_output_aliases={i: i for i in range(len(flying))},
        compiler_params=pltpu.CompilerParams(has_side_effects=_FLOWS),
    )(*flying, send, recv, after)


def _sum_chips(qs, ps, splits, place_arr):
    n = len(qs)
    q_specs, p_specs, o_specs, out_shapes = [], [], [], []
    for q, split in zip(qs, splits):
        _, hr, hn = q.shape
        q_specs.append(pl.BlockSpec((4, hr, hn), lambda i, pr: (0, 0, 0)))
        if split.by_rows:
            out_shapes.append((hr, 2 * hn))
            o_specs.append(pl.BlockSpec((hr, hn), lambda i, pr: (0, pr[0])))
            p_specs.append(pl.BlockSpec((pl.Element(hr), pl.Element(hn)), lambda i, pr, step=split.step: (pr[1] * step, 0)))
        else:
            out_shapes.append((2 * hr, hn))
            o_specs.append(pl.BlockSpec((hr, hn), lambda i, pr: (pr[0], 0)))
            p_specs.append(pl.BlockSpec((hr, hn), lambda i, pr: (0, pr[1])))

    def body(pr, *refs):
        for q_ref, p_ref, o_ref in zip(refs[:n], refs[n:2 * n], refs[2 * n:]):
            f = lambda k: jnp.where(pr[1] == k, p_ref[...], q_ref[k]).astype(_F32)
            o_ref[...] = ((f(0) + f(1)) + f(2)) + f(3)

    return pl.pallas_call(
        body, name="chip_sums",
        grid_spec=pltpu.PrefetchScalarGridSpec(
            num_scalar_prefetch=1, grid=(1,), in_specs=q_specs + p_specs, out_specs=o_specs),
        out_shape=[jax.ShapeDtypeStruct(s, _F32) for s in out_shapes],
        compiler_params=_cparams(("arbitrary",), VMEM_LIMIT),
    )(place_arr, *qs, *ps)


def _pair_share(bufs, splits, small):
    nw = len(bufs)

    def body(*refs):
        ins, small_ref, outs, all_ref = refs[:nw], refs[nw], refs[nw + 1:2 * nw + 1], refs[2 * nw + 1]
        send, recv, s_send, s_recv = refs[2 * nw + 2:]
        x, y, c = _place()
        copies = []
        for r in range(1, 8):
            peer = (1 - x if r & 4 else x, 1 - y if r & 2 else y, 1 - c if r & 1 else c)
            cp = pltpu.make_async_remote_copy(
                src_ref=small_ref, dst_ref=all_ref.at[4 * x + 2 * y + c], send_sem=s_send.at[r - 1],
                recv_sem=s_recv.at[r - 1], device_id=peer, device_id_type=_MESH)
            cp.start()
            copies.append(cp)
        for i in range(nw):
            cp = pltpu.make_async_remote_copy(
                src_ref=splits[i].half(ins[i], c), dst_ref=splits[i].half(outs[i], c), send_sem=send.at[i],
                recv_sem=recv.at[i], device_id=(x, y, 1 - c), device_id_type=_MESH)
            cp.start()
            copies.append(cp)
        for cp in copies:
            cp.wait()

    return pl.pallas_call(
        body, name="grad_pair_share",
        in_specs=[_ANY] * (nw + 1), out_specs=[_ANY] * (nw + 1),
        out_shape=[jax.ShapeDtypeStruct(b.shape, b.dtype) for b in bufs]
        + [jax.ShapeDtypeStruct((8,) + small.shape, small.dtype)],
        input_output_aliases={i: i for i in range(nw)},
        scratch_shapes=[pltpu.SemaphoreType.DMA((nw,)), pltpu.SemaphoreType.DMA((nw,)),
                        pltpu.SemaphoreType.DMA((7,)), pltpu.SemaphoreType.DMA((7,))],
    )(*bufs, small)


def _sum_devices(parts, own, dev_arr):
    def body(dev, p_ref, own_ref, tot_ref):
        f = lambda d: jnp.where(dev[0] == d, own_ref[...], p_ref[d])
        acc = f(0)
        for d in range(1, 8):
            acc = acc + f(d)
        tot_ref[...] = acc

    return pl.pallas_call(
        body, name="small_sum",
        grid_spec=pltpu.PrefetchScalarGridSpec(
            num_scalar_prefetch=1, grid=(1,),
            in_specs=[pl.BlockSpec(parts.shape, lambda i, dev: (0, 0, 0)), pl.BlockSpec(own.shape, lambda i, dev: (0, 0))],
            out_specs=pl.BlockSpec(own.shape, lambda i, dev: (0, 0))),
        out_shape=jax.ShapeDtypeStruct(own.shape, own.dtype),
    )(dev_arr, parts, own)


def _adam_update(w, g, m, v):
    m2 = ADAM_B1 * m + (1.0 - ADAM_B1) * g
    v2 = ADAM_B2 * v + (1.0 - ADAM_B2) * (g * g)
    m_hat = m2 / (1.0 - ADAM_B1 ** ADAM_STEP)
    v_hat = v2 / (1.0 - ADAM_B2 ** ADAM_STEP)
    return -ADAM_LR * (m_hat / (jnp.sqrt(v_hat) + ADAM_EPS) + ADAM_WD * w), m2, v2


SMALL_AT = dict(g_in=(0, 0), g_final=(1, 0), g_gla_norm=(2, 0), b_alpha=(2, B_WIDTH), attn_sinks=(2, B_WIDTH + B_KEY_WIDTH))
LOSS_AT = (2, B_WIDTH + B_KEY_WIDTH + LANE)
WUP_ROWS = (3, 7)


def _adamw_small(tot, g_wup, params):
    names = list(params)

    def body(*refs):
        tot_ref, gw_ref = refs[0], refs[1]
        ins = refs[2:2 + 3 * len(names)]
        outs = refs[2 + 3 * len(names):]
        for i, nm in enumerate(names):
            w_ref, m_ref, v_ref = ins[3 * i:3 * i + 3]
            if nm in SMALL_AT:
                r, a = SMALL_AT[nm]
                g = tot_ref[r:r + 1, a:a + w_ref.shape[1]]
            else:
                g = gw_ref[...]
            d, m2, v2 = _adam_update(w_ref[...], g, m_ref[...], v_ref[...])
            for o_ref, val in zip(outs[4 * i:4 * i + 4], (g, d, m2, v2)):
                o_ref[...] = val

    vm = pl.BlockSpec(memory_space=pltpu.VMEM)
    flat = [a for nm in names for a in params[nm]]
    out_shape = [jax.ShapeDtypeStruct(params[nm][0].shape, _F32) for nm in names for _ in range(4)]
    outs = pl.pallas_call(
        body, name="adamw_small", in_specs=[vm] * (2 + len(flat)), out_specs=[vm] * len(out_shape), out_shape=out_shape,
    )(tot, g_wup, *flat)
    return {nm: tuple(outs[4 * i:4 * i + 4]) for i, nm in enumerate(names)}


def _adamw(w, g, m, v, name):
    lead = w.shape[0] != 1
    r, n = (w.shape[0], w.shape[2]) if lead else w.shape[1:]
    br = r
    for cand in (256, 244, 128):
        if r > cand and r % cand == 0:
            br = cand
            break

    def body(w_ref, g_ref, m_ref, v_ref, d_ref, nm_ref, nv_ref):
        d_ref[...], nm_ref[...], nv_ref[...] = _adam_update(w_ref[...], g_ref[...], m_ref[...], v_ref[...])

    blk = pl.BlockSpec((br, 1, n), lambda i: (i, 0, 0)) if lead else pl.BlockSpec((None, br, n), lambda i: (0, i, 0))
    shp = jax.ShapeDtypeStruct(w.shape, _F32)
    return pl.pallas_call(
        body, name=name, grid=(r // br,),
        in_specs=[blk] * 4, out_specs=[blk] * 3, out_shape=[shp] * 3,
        compiler_params=_cparams(("parallel",)),
    )(w, g, m, v)


def _adamw_w_in(w, g_window, m, v, chip_arr):
    r, _, n = w.shape
    br = r // 5

    def body(k_ref, w_ref, g_ref, m_ref, v_ref, go_ref, d_ref, nm_ref, nv_ref):
        first = (SHARD - WINDOW_STEP) * k_ref[0] + br * pl.program_id(0)
        wide = g_ref[pl.ds(pl.multiple_of((first // 8) * 8, 8), br + 4), :]
        g = jnp.where(first % 8 == 0, wide[:br], wide[4:]).reshape(br, 1, n)
        go_ref[...] = g
        d_ref[...], nm_ref[...], nv_ref[...] = _adam_update(w_ref[...], g, m_ref[...], v_ref[...])

    blk = pl.BlockSpec((br, 1, n), lambda i, k_ref: (i, 0, 0))
    g_spec = pl.BlockSpec(memory_space=pltpu.VMEM)
    shp = jax.ShapeDtypeStruct(w.shape, _F32)
    return pl.pallas_call(
        body, name="adamw_w_in",
        grid_spec=pltpu.PrefetchScalarGridSpec(
            num_scalar_prefetch=1, grid=(r // br,), in_specs=[blk, g_spec, blk, blk], out_specs=[blk] * 4),
        out_shape=[shp] * 4,
        compiler_params=_cparams(("parallel",)),
    )(chip_arr, w, g_window, m, v)


def _adamw_whole(items):
    n = len(items)

    def body(*refs):
        for i in range(n):
            w_ref, g_ref, m_ref, v_ref = refs[4 * i:4 * i + 4]
            d_ref, nm_ref, nv_ref = refs[4 * n + 3 * i:4 * n + 3 * i + 3]
            d_ref[...], nm_ref[...], nv_ref[...] = _adam_update(w_ref[...], g_ref[...], m_ref[...], v_ref[...])

    whole = lambda a: pl.BlockSpec((None,) + a.shape[1:], lambda i: (0, 0, 0))
    outs = pl.pallas_call(
        body, name="adamw_out_weights", grid=(1,),
        in_specs=[whole(a) for it in items for a in it],
        out_specs=[whole(it[0]) for it in items for _ in range(3)],
        out_shape=[jax.ShapeDtypeStruct(it[0].shape, _F32) for it in items for _ in range(3)],
        compiler_params=_cparams(("arbitrary",), VMEM_LIMIT),
    )(*[a for it in items for a in it])
    return [tuple(outs[3 * i:3 * i + 3]) for i in range(n)]


def kernel(x, positions, g_in, w_in, w_alpha_up, b_alpha, attn_sinks, g_gla_norm, w_out_a, w_out_b, w_o, g_final, loss_target, m_g_in, m_w_in, m_w_alpha_up, m_b_alpha, m_attn_sinks, m_g_gla_norm, m_w_out_a, m_w_out_b, m_w_o, m_g_final, v_g_in, v_w_in, v_w_alpha_up, v_b_alpha, v_attn_sinks, v_g_gla_norm, v_w_out_a, v_w_out_b, v_w_o, v_g_final):
    nseq, seq, _ = x.shape
    t = nseq * seq
    cx, cy, cc = _place()
    chip = 2 * cx + cy
    c_arr = jnp.reshape(cc, (1,)).astype(jnp.int32)

    tr = lambda w: jnp.transpose(w, (2, 0, 1))
    w_in_t = tr(w_in).reshape(SHARD, D_MODEL).astype(_MX)
    pad = WINDOW_ROWS - SHARD
    window = lax.switch(chip, [lambda w, k=k: jnp.pad(w, ((4 * k, pad - 4 * k), (0, 0))) for k in range(4)], w_in_t)
    shards = [window, w_alpha_up[0].astype(_MX)]
    late = [w_out_a[0].astype(_MX), w_out_b[0].astype(_MX), w_o[0].astype(_MX)]
    late_splits = [SPLIT_W_OUT, SPLIT_W_OUT, SPLIT_W_O]
    splits = [SPLIT_W_IN_T, None]
    fulls = [(4, WINDOW_ROWS, D_MODEL), (4, B_GATE_RANK, B_KEY_WIDTH // 4)]
    pos_f = positions.astype(_F32).reshape(t, 1)
    win_g, wup_g, cos, sa, sb = _gather_weights(shards, splits, fulls, pos_f)
    late_copies = lambda refs, send, recv: _late_gather_copies(refs, send, recv, late_splits)
    late_full = [lax.empty(shape, _MX) for shape in ((A_WIDTH, D_MODEL), (B_WIDTH, D_MODEL), (D_MODEL, D_MODEL))]
    l_send, l_recv, l_flying, l_token = _start_copies("late_gather_start", late + late_full, late_copies,
                                                      4 * len(late), after=win_g)
    win_g = lax.dynamic_update_slice(win_g, window[None], (chip, 0, 0))
    wup_g = lax.dynamic_update_slice(wup_g, shards[1][None], (chip, 0, 0))
    wt = _assemble_w_in_t(win_g)
    wup = jnp.concatenate([jnp.transpose(wup_g, (1, 0, 2)).reshape(B_GATE_RANK, B_KEY_WIDTH),
                           jnp.zeros((LANE - B_GATE_RANK, B_KEY_WIDTH), _MX)], axis=0)

    x2 = x.reshape(t, D_MODEL)
    tgt = loss_target.reshape(t, D_MODEL)
    sinks = attn_sinks.reshape(A_HEADS)
    gf = g_final.reshape(1, D_MODEL)

    h, qkv, za, qkb, vb, zb, alr, ga, gb = _in_proj(x2, g_in, wt, cos, sa, sb, l_token)
    oa = _attn_fwd(qkv, za, sinks, nseq)
    ob, oraw, sst = _gla_fwd(qkb, vb, zb, alr, wup, b_alpha, g_gla_norm, nseq)

    wa, wb, wo = _wait_copies("late_gather_wait", l_send, l_recv, l_flying, late_copies, ob)[len(late):]
    dh2, doa, dob, dga, dgb, dwa, dwb, dwo, dgf, lossv = _merge_loss(oa, ob, ga, gb, x2, tgt, wa, wb, wo, gf)

    dqkv, dza, dsink = _attn_bwd(qkv, za, doa, sinks, cos, sa, sb, nseq)
    dqkb, dvb, dzb, dalr, dwup, dba, dgn = _gla_bwd(qkb, vb, zb, alr, oraw, dob, sst, wup, b_alpha, g_gla_norm, nseq)
    dpieces = [dqkv, dza, dqkb, dvb, dzb, dalr, dga, dgb]
    gsplits = [SPLIT_W_IN_T, SPLIT_W_OUT, SPLIT_W_OUT, SPLIT_W_O]
    names = ("w_in", "w_out_a", "w_out_b", "w_o")
    dwin_mine, *from_sibling = _in_proj_bwd_w(h, dpieces, [dwa, dwb, dwo], gsplits[1:])
    pair_sums = [_pair_sum(dwin_mine, from_sibling[0], None, c_arr, "pair_sum_w_in"),
                 *_pair_sums_whole([dwa, dwb, dwo], from_sibling[1:], gsplits[1:], c_arr)]
    exchange = lambda refs, send, recv: _chip_exchange_copies(refs, send, recv, gsplits)
    lands = [lax.empty((4,) + sp.part_shape(p.shape), p.dtype) for p, sp in zip(pair_sums, gsplits)]
    send, recv, flying, token = _start_copies("grad_chip_exchange_start", pair_sums + lands, exchange, 3 * len(lands))
    grad_x2, dgin = _in_proj_bwd_x(dpieces, wt, x2, dh2, g_in, token)
    landed = _wait_copies("grad_chip_exchange_wait", send, recv, flying, exchange, grad_x2)
    place_arr = jnp.stack([cc, chip]).astype(jnp.int32)
    reduced = _sum_chips(landed[len(lands):], landed[:len(lands)], gsplits, place_arr)
    row2 = jnp.concatenate([dgn, dba, jnp.pad(dsink[:, 0].reshape(1, A_HEADS), ((0, 0), (0, LANE - A_HEADS))),
                            jnp.pad(jnp.sum(lossv, axis=1, keepdims=True), ((0, 0), (0, LANE - 1)))], axis=1)
    small = jnp.concatenate([dgin, dgf, row2, dwup[:B_GATE_RANK].reshape(WUP_ROWS[1] - WUP_ROWS[0], D_MODEL),
                             jnp.zeros((1, D_MODEL), _F32)], axis=0)
    g_window, g_wa, g_wb, g_wo, small_parts = _pair_share(reduced, gsplits, small)
    dev_arr = jnp.reshape(2 * chip + cc, (1,)).astype(jnp.int32)
    tot = _sum_devices(small_parts, small, dev_arr)
    loss = tot[LOSS_AT]
    nup = B_KEY_WIDTH // 4
    g_wup = lax.dynamic_slice(tot[WUP_ROWS[0]:WUP_ROWS[1]].reshape(B_GATE_RANK, B_KEY_WIDTH), (0, chip * nup),
                              (B_GATE_RANK, nup))

    row = lambda a: a.reshape(1, -1)
    sm = _adamw_small(tot, g_wup, dict(
        g_in=(g_in, m_g_in, v_g_in), g_final=(row(g_final), row(m_g_final), row(v_g_final)),
        g_gla_norm=(g_gla_norm, m_g_gla_norm, v_g_gla_norm), b_alpha=(b_alpha, m_b_alpha, v_b_alpha),
        attn_sinks=(attn_sinks, m_attn_sinks, v_attn_sinks),
        w_alpha_up=(w_alpha_up[0], m_w_alpha_up[0], v_w_alpha_up[0])))
    sm["g_final"] = tuple(a.reshape(D_MODEL) for a in sm["g_final"])
    sm["w_alpha_up"] = tuple(a[None] for a in sm["w_alpha_up"])

    untr = lambda a: jnp.transpose(a, (1, 2, 0))
    chip_arr = jnp.reshape(chip, (1,)).astype(jnp.int32)
    big = dict(w_in=tuple(untr(a) for a in _adamw_w_in(tr(w_in), g_window, tr(m_w_in), tr(v_w_in), chip_arr)))
    out_weights = (("w_out_a", w_out_a, g_wa, m_w_out_a, v_w_out_a),
                   ("w_out_b", w_out_b, g_wb, m_w_out_b, v_w_out_b), ("w_o", w_o, g_wo, m_w_o, v_w_o))
    updates = _adamw_whole([(w, g[None], m, v) for _, w, g, m, v in out_weights])
    for (nm, _, g, _, _), upd in zip(out_weights, updates):
        big[nm] = (g[None],) + upd

    order = ("g_in", "w_in", "w_alpha_up", "b_alpha", "attn_sinks", "g_gla_norm", "w_out_a", "w_out_b", "w_o", "g_final")
    outs = [big[nm][kind] if nm in big else sm[nm][kind] for kind in range(4) for nm in order]
    return (loss, grad_x2.reshape(x.shape), *outs)
```

```python
import math
from typing import NamedTuple

import numpy as np
import jax
import jax.numpy as jnp
from jax import lax
from jax.experimental import pallas as pl
from jax.experimental.pallas import tpu as pltpu

D_MODEL = 1024
A_HEADS, A_KV_HEADS, A_HEAD_DIM = 8, 2, 64
A_GROUP = A_HEADS // A_KV_HEADS
A_WIDTH, A_KV_WIDTH = 512, 128
BLOCK = 128
ROPE_THETA = 500000.0
ROPE_DIM = 16
B_HEADS, B_KEY_DIM, B_VAL_DIM = 4, 64, 128
B_KEY_WIDTH, B_WIDTH = 256, 512
B_GATE_RANK = 16
B_GATE_TEMP = 16.0
B_CHUNK = 64
NORM_EPS = 1e-6
NEG_BIG = -1e30
D_IN = 4880

ADAM_LR, ADAM_B1, ADAM_B2, ADAM_EPS, ADAM_WD, ADAM_STEP = 0.001, 0.9, 0.999, 1e-08, 0.01, 10

LANE = 128
ALR_AT = 2816
PIECES = (("qkv", 0, 768), ("za", 768, 1280), ("qkb", 1280, 1792), ("vb", 1792, 2304),
          ("zb", 2304, 2816), ("alr", ALR_AT, ALR_AT + LANE), ("ga", 2832, 3856), ("gb", 3856, 4880))
SHARD = D_IN // 4
WINDOW_STEP = 1216
WINDOW_ROWS = 1232

GLA_BLOCK = 256
GLA_FWD_BLOCK = 1024
GLA_BWD_TILE = 4
MERGE_CHUNK = 256
MERGE_SLAB = 16
VMEM_LIMIT = 56 * 1024 * 1024

_F32 = jnp.float32
_MX = jnp.bfloat16
_ST = jnp.bfloat16

_MESH = pl.DeviceIdType.MESH
_ANY = pl.BlockSpec(memory_space=pl.ANY)


def _cparams(sem=None, vmem=None):
    return pltpu.CompilerParams(dimension_semantics=sem, vmem_limit_bytes=vmem)


def _dot(a, b):
    return jnp.dot(a.astype(_MX), b.astype(_MX), preferred_element_type=_F32)


def _dot_nt(a, b):
    return lax.dot_general(a.astype(_MX), b.astype(_MX), (((1,), (1,)), ((), ())),
                           preferred_element_type=_F32)


def _dot_tn(a, b):
    return lax.dot_general(a.astype(_MX), b.astype(_MX), (((0,), (0,)), ((), ())),
                           preferred_element_type=_F32)


def _dot_ones(ones_mat, v):
    o = ones_mat.astype(jnp.bfloat16)
    v0 = v.astype(jnp.bfloat16)
    v1 = (v - v0.astype(_F32)).astype(jnp.bfloat16)
    d = lambda t: jnp.dot(o, t, preferred_element_type=_F32)
    return d(v0) + d(v1)


def _sigmoid(x):
    return 0.5 * jnp.tanh(0.5 * x) + 0.5


def _log_sigmoid(x):
    return jnp.minimum(x, 0.0) - jnp.log(1.0 + jnp.exp(-jnp.abs(x)))


def _lane_tile(t, width):
    reps = width // t.shape[1]
    return t if reps == 1 else jnp.tile(t, (1, reps))


def _rope(t, cos, sa, sb, sign):
    w = t.shape[1]
    rot = pltpu.roll(t, w - 8, 1) * _lane_tile(sa, w) + pltpu.roll(t, 8, 1) * _lane_tile(sb, w)
    return t * _lane_tile(cos, w) + sign * rot


def _rms_bwd(dy_g, n, r):
    return r * (dy_g - n * jnp.mean(dy_g * n, axis=-1, keepdims=True))


ROPE_ROWS = 256


def _rope_consts():
    lane = np.arange(LANE) % A_HEAD_DIM
    half = ROPE_DIM // 2
    inv = np.exp((np.float32(-math.log(ROPE_THETA)) * np.arange(half, dtype=np.float32)) * np.float32(2.0 / ROPE_DIM))
    consts = np.zeros((8, LANE), np.float32)
    consts[0] = np.where(lane < ROPE_DIM, inv[lane % half], 0.0)
    consts[1] = np.where(lane < half, -1.0, 0.0)
    consts[2] = np.where((lane >= half) & (lane < ROPE_DIM), 1.0, 0.0)
    return jnp.asarray(consts)


def _rope_tables_into(pos_ref, c_ref, cos_ref, sa_ref, sb_ref):
    def rows_of(b, carry):
        rows = pl.ds(pl.multiple_of(b * ROPE_ROWS, ROPE_ROWS), ROPE_ROWS)
        ang = pos_ref[rows, :] * c_ref[0:1, :]
        s = jnp.sin(ang)
        cos_ref[rows, :] = jnp.cos(ang)
        sa_ref[rows, :] = s * c_ref[1:2, :]
        sb_ref[rows, :] = s * c_ref[2:3, :]
        return carry

    lax.fori_loop(0, pos_ref.shape[0] // ROPE_ROWS, rows_of, 0)


def _in_proj(x2, g_in, wt, cos, sa, sb, after):
    t = x2.shape[0]
    tm = min(t, 512)

    def body(x_ref, g_ref, w_ref, cos_ref, sa_ref, sb_ref, after_ref, h_ref, qkv_ref, za_ref, qkb_ref,
             vb_ref, zb_ref, alr_ref, ga_ref, gb_ref):
        xv = x_ref[...]
        r = lax.rsqrt(jnp.mean(xv * xv, axis=-1, keepdims=True) + NORM_EPS)
        h = (xv * r * g_ref[...]).astype(_MX)
        h_ref[...] = h.astype(_ST)
        outs = dict(za=za_ref, qkb=qkb_ref, vb=vb_ref, zb=zb_ref, alr=alr_ref, ga=ga_ref, gb=gb_ref)
        for name, a, b in PIECES:
            p = _dot_nt(h, w_ref[a:b, :])
            if name == "qkv":
                c, s1, s2 = cos_ref[...], sa_ref[...], sb_ref[...]
                qkv_ref[:, 0:512] = _rope(p[:, 0:512], c, s1, s2, 1.0).astype(_ST)
                qkv_ref[:, 512:640] = _rope(p[:, 512:640], c, s1, s2, 1.0).astype(_ST)
                qkv_ref[:, 640:768] = p[:, 640:768].astype(_ST)
            else:
                outs[name][...] = p.astype(outs[name].dtype)

    rows = lambda w: pl.BlockSpec((tm, w), lambda i: (i, 0))
    shp = lambda name, w: jax.ShapeDtypeStruct((t, w), _F32 if name == "qkb" else _ST)
    widths = [D_MODEL] + [b - a for _, a, b in PIECES]
    return pl.pallas_call(
        body, name="in_proj", grid=(t // tm,),
        in_specs=[rows(D_MODEL), pl.BlockSpec((1, D_MODEL), lambda i: (0, 0)),
                  pl.BlockSpec((D_IN, D_MODEL), lambda i: (0, 0), pipeline_mode=pl.Buffered(1)),
                  rows(LANE), rows(LANE), rows(LANE), _ANY],
        out_specs=[rows(w) for w in widths],
        out_shape=[shp(n, w) for n, w in zip(["h"] + [p[0] for p in PIECES], widths)],
        compiler_params=_cparams(("parallel",), VMEM_LIMIT),
    )(x2, g_in, wt, cos, sa, sb, after)


def _attn_operands(k_prev, k_cur, v_prev, v_cur, want_bwd):
    kf = jnp.concatenate([k_prev, k_cur], axis=0).astype(_F32) * (A_HEAD_DIM ** -0.5)
    vf = jnp.concatenate([v_prev, v_cur], axis=0).astype(_F32)
    lo = lax.broadcasted_iota(jnp.int32, (1, LANE), 1) < 64

    def on_lanes(a):
        sw = pltpu.roll(a, 64, 1)
        z = jnp.zeros_like(a)
        return [[jnp.where(lo, a, z).astype(_MX), jnp.where(lo, z, sw).astype(_MX)],
                [jnp.where(lo, sw, z).astype(_MX), jnp.where(lo, z, a).astype(_MX)]]

    def on_rows(a):
        at = a.T.astype(_MX)
        z = jnp.zeros((64, at.shape[1]), _MX)
        top, bot = at[0:64], at[64:128]
        return [[jnp.concatenate([top, z], axis=0), jnp.concatenate([z, top], axis=0)],
                [jnp.concatenate([bot, z], axis=0), jnp.concatenate([z, bot], axis=0)]]

    ops = dict(k_lanes=on_lanes(kf), v_rows=on_rows(vf), lo=lo)
    if want_bwd:
        ops.update(v_lanes=on_lanes(vf), k_rows=on_rows(kf))
    return ops


def _attn_valid(n):
    kj = lax.broadcasted_iota(jnp.int32, (2 * BLOCK, 2 * BLOCK), 0) - BLOCK
    qi = lax.broadcasted_iota(jnp.int32, (2 * BLOCK, 2 * BLOCK), 1) & (BLOCK - 1)
    return (kj <= qi) & (qi - kj < BLOCK) & ((n > 0) | (kj >= 0))


def _attn_sinks(sink_ref, h_a, h_b):
    first = lax.broadcasted_iota(jnp.int32, (1, 2 * BLOCK), 1) < BLOCK
    return jnp.where(first, sink_ref[h_a], sink_ref[h_b])


def _attn_softmax_t(k_lanes, q_pair, valid, sink):
    s = jnp.where(valid, _dot_nt(k_lanes, q_pair), NEG_BIG)
    m = jnp.maximum(jnp.max(s, axis=0, keepdims=True), sink)
    e = jnp.exp(s - m)
    e_sink = jnp.exp(sink - m)
    inv = 1.0 / (jnp.sum(e, axis=0, keepdims=True) + e_sink)
    return e, e_sink, inv


ATTN_TILE = 8


def _attn_kv(qkv_ref, kvp_ref, j):
    rows = slice(j * BLOCK, (j + 1) * BLOCK)
    if j == 0:
        k_prev, v_prev = kvp_ref[:, 0:128], kvp_ref[:, 128:256]
    else:
        before = slice((j - 1) * BLOCK, j * BLOCK)
        k_prev, v_prev = qkv_ref[before, 512:640], qkv_ref[before, 640:768]
    return k_prev, qkv_ref[rows, 512:640], v_prev, qkv_ref[rows, 640:768]


def _attn_fwd(qkv, za, sinks, nseq):
    t = qkv.shape[0]
    nblk = min(ATTN_TILE, t // nseq // BLOCK)
    tile = nblk * BLOCK
    nt = t // nseq // tile

    def body(sink_ref, qkv_ref, kvp_ref, za_ref, oa_ref):
        for j in range(nblk):
            rows = slice(j * BLOCK, (j + 1) * BLOCK)
            ops = _attn_operands(*_attn_kv(qkv_ref, kvp_ref, j), False)
            valid = _attn_valid(nblk * pl.program_id(1) + j)[:, 0:BLOCK]
            for pr in range(A_HEADS // 2):
                lanes = slice(pr * LANE, (pr + 1) * LANE)
                g = pr // (A_GROUP // 2)
                q_pair = qkv_ref[rows, lanes]
                ot = None
                for half in range(2):
                    e, _, inv = _attn_softmax_t(ops["k_lanes"][g][half], q_pair, valid, sink_ref[2 * pr + half])
                    part = _dot(ops["v_rows"][g][half], e) * inv
                    ot = part if ot is None else ot + part
                z = za_ref[rows, lanes].astype(_F32)
                oa_ref[rows, lanes] = (ot.T * (z * _sigmoid(z))).astype(_ST)

    cur = lambda w: pl.BlockSpec((tile, w), lambda s, n: (s * nt + n, 0))
    return pl.pallas_call(
        body, name="attn_fwd", grid=(nseq, nt),
        in_specs=[pl.BlockSpec(memory_space=pltpu.SMEM), cur(768),
                  pl.BlockSpec((BLOCK, 256), lambda s, n: (nblk * (s * nt + n) - jnp.minimum(n, 1), 2)),
                  cur(512)],
        out_specs=cur(512), out_shape=jax.ShapeDtypeStruct((t, A_WIDTH), _ST),
        compiler_params=_cparams(("parallel", "arbitrary")),
    )(sinks, qkv, qkv, za)


def _attn_bwd(qkv, za, doa, sinks, cos, sa, sb, nseq):
    t = qkv.shape[0]
    nblk = min(ATTN_TILE, t // nseq // BLOCK)
    tile = nblk * BLOCK
    nt = t // nseq // tile

    def body(sink_ref, qkv_ref, kvp_ref, za_ref, doa_ref, cos_ref, sa_ref, sb_ref,
             dqkv_ref, dza_ref, dsink_ref, ck_ref, cv_ref):
        s_id, i = pl.program_id(0), pl.program_id(1)

        @pl.when((s_id == 0) & (i == 0))
        def _():
            dsink_ref[...] = jnp.zeros_like(dsink_ref)

        @pl.when(i == 0)
        def _():
            ck_ref[...] = jnp.zeros_like(ck_ref)
            cv_ref[...] = jnp.zeros_like(cv_ref)

        carry_k, carry_v = ck_ref[...], cv_ref[...]
        for j in reversed(range(nblk)):
            rows = slice(j * BLOCK, (j + 1) * BLOCK)
            ops = _attn_operands(*_attn_kv(qkv_ref, kvp_ref, j), True)
            lo = ops["lo"]
            valid = _attn_valid(nblk * (nt - 1 - i) + j)
            dk_acc, dv_acc, dq_pairs = [], [], []
            for g in range(A_KV_HEADS):
                pairs = [slice((2 * g + p) * LANE, (2 * g + p + 1) * LANE) for p in range(2)]
                q_both = jnp.concatenate([qkv_ref[rows, p] for p in pairs], axis=0)
                q_f = q_both.astype(_F32)
                z = [za_ref[rows, p].astype(_F32) for p in pairs]
                sz = [_sigmoid(v) for v in z]
                d_oa = [doa_ref[rows, p].astype(_F32) for p in pairs]
                d_att = jnp.concatenate([d_oa[p] * (z[p] * sz[p]) for p in range(2)], axis=0)
                zero = jnp.zeros_like(d_att)
                ot, dqt, ds_all, pn_all, qz_all, daz_all = None, None, [], [], [], []
                for half in range(2):
                    heads = (4 * g + half, 4 * g + 2 + half)
                    e, e_sink, inv = _attn_softmax_t(ops["k_lanes"][g][half], q_both, valid,
                                                     _attn_sinks(sink_ref, *heads))
                    pn = e * inv
                    dpt = _dot_nt(ops["v_lanes"][g][half], d_att)
                    delta = jnp.sum(pn * dpt, axis=0, keepdims=True)
                    ds = (pn * (dpt - delta)).astype(_MX)
                    pn = pn.astype(_MX)
                    d_sink = e_sink * inv * delta
                    for p, h in enumerate(heads):
                        dsink_ref[h:h + 1, :] = dsink_ref[h:h + 1, :] - jnp.sum(d_sink[:, p * BLOCK:(p + 1) * BLOCK])
                    o_part = _dot(ops["v_rows"][g][half], pn)
                    dq_part = _dot(ops["k_rows"][g][half], ds)
                    ot = o_part if ot is None else ot + o_part
                    dqt = dq_part if dqt is None else dqt + dq_part
                    mine = lo if half == 0 else jnp.logical_not(lo)
                    ds_all.append(ds)
                    pn_all.append(pn)
                    qz_all.append(jnp.where(mine, q_f, zero).astype(_MX))
                    daz_all.append(jnp.where(mine, d_att, zero).astype(_MX))
                dk_acc.append(_dot(jnp.concatenate(ds_all, axis=1), jnp.concatenate(qz_all, axis=0)))
                dv_acc.append(_dot(jnp.concatenate(pn_all, axis=1), jnp.concatenate(daz_all, axis=0)))
                for p, lanes in enumerate(pairs):
                    cols = slice(p * BLOCK, (p + 1) * BLOCK)
                    dza_ref[rows, lanes] = (d_oa[p] * ot[:, cols].T * (sz[p] * (1.0 + z[p] * (1.0 - sz[p])))).astype(_ST)
                    dq_pairs.append(dqt[:, cols].T)

            def fold(acc, scale):
                both = [a + pltpu.roll(a, 64, 1) for a in acc]
                return jnp.where(lo, both[0], both[1]) * scale

            dk_full = fold(dk_acc, A_HEAD_DIM ** -0.5)
            dv_full = fold(dv_acc, 1.0)
            dk_cur, dv_cur = dk_full[BLOCK:] + carry_k, dv_full[BLOCK:] + carry_v
            carry_k, carry_v = dk_full[:BLOCK], dv_full[:BLOCK]
            c, s1, s2 = cos_ref[rows, :], sa_ref[rows, :], sb_ref[rows, :]
            dqkv_ref[rows, 0:512] = _rope(jnp.concatenate(dq_pairs, axis=1), c, s1, s2, -1.0).astype(_ST)
            dqkv_ref[rows, 512:640] = _rope(dk_cur, c, s1, s2, -1.0).astype(_ST)
            dqkv_ref[rows, 640:768] = dv_cur.astype(_ST)
        ck_ref[...] = carry_k
        cv_ref[...] = carry_v

    cur = lambda w: pl.BlockSpec((tile, w), lambda s, i: (s * nt + nt - 1 - i, 0))
    return pl.pallas_call(
        body, name="attn_bwd", grid=(nseq, nt),
        in_specs=[pl.BlockSpec(memory_space=pltpu.SMEM), cur(768),
                  pl.BlockSpec((BLOCK, 256),
                               lambda s, i: (nblk * (s * nt + nt - 1 - i) - jnp.minimum(nt - 1 - i, 1), 2)),
                  cur(512), cur(512), cur(LANE), cur(LANE), cur(LANE)],
        out_specs=[cur(768), cur(512), pl.BlockSpec((8, LANE), lambda s, i: (0, 0))],
        out_shape=[jax.ShapeDtypeStruct((t, 768), _ST), jax.ShapeDtypeStruct((t, 512), _ST),
                   jax.ShapeDtypeStruct((8, LANE), _F32)],
        scratch_shapes=[pltpu.VMEM((BLOCK, A_KV_WIDTH), _F32), pltpu.VMEM((BLOCK, A_KV_WIDTH), _F32)],
        compiler_params=_cparams(("arbitrary", "arbitrary")),
    )(sinks, qkv, qkv, za, doa, cos, sa, sb)


def _gla_chunk_terms(la, qkb_ref, r0):
    g = la[r0:r0 + B_CHUNK, :]
    ri = lax.broadcasted_iota(jnp.int32, (B_CHUNK, B_CHUNK), 0)
    ci = lax.broadcasted_iota(jnp.int32, (B_CHUNK, B_CHUNK), 1)
    cum = _dot_ones((ri >= ci).astype(_F32), g)
    last = cum[B_CHUNK - 1:B_CHUNK, :]
    mid = cum[B_CHUNK // 2 - 1:B_CHUNK // 2, :]
    q = qkb_ref[r0:r0 + B_CHUNK, 0:B_KEY_WIDTH].astype(_F32) * (B_KEY_DIM ** -0.5)
    k = qkb_ref[r0:r0 + B_CHUNK, B_KEY_WIDTH:2 * B_KEY_WIDTH].astype(_F32)
    e_q, e_k, e_l, e_c = jnp.exp(cum - mid), jnp.exp(mid - cum), jnp.exp(last - cum), jnp.exp(cum)
    dec_col = jnp.exp(jnp.sum(g.T, axis=1, keepdims=True))
    return dict(qm=q * e_q, km=k * e_k, kl=k * e_l, qc=q * e_c, e_q=e_q, e_k=e_k, e_l=e_l, e_c=e_c,
                dec_col=dec_col, dec_row=jnp.exp(last), causal=ri >= ci, ri=ri)


def _gate_logits(alr_ref, wup_ref, b_ref):
    return _dot(alr_ref[...], wup_ref[...]) + b_ref[...]


def _gla_fwd(qkb, vb, zb, alr, wup, b_alpha, gn, nseq):
    t = qkb.shape[0]
    tb = min(GLA_FWD_BLOCK, t // nseq)
    nblk = t // nseq // tb
    cpb = tb // B_CHUNK

    def body(qkb_ref, vb_ref, zb_ref, alr_ref, wup_ref, b_ref, gn_ref, ob_ref, oraw_ref, sst_ref, s_ref):
        @pl.when(pl.program_id(1) == 0)
        def _():
            s_ref[...] = jnp.zeros_like(s_ref)

        la = _log_sigmoid(_gate_logits(alr_ref, wup_ref, b_ref)) * (1.0 / B_GATE_TEMP)
        terms = [_gla_chunk_terms(la, qkb_ref, c * B_CHUNK) for c in range(cpb)]
        o_intra, inc = {}, {}
        for c, tm in enumerate(terms):
            for h in range(B_HEADS):
                kl_, vl_ = slice(h * 64, (h + 1) * 64), slice(h * 128, (h + 1) * 128)
                v = vb_ref[c * B_CHUNK:(c + 1) * B_CHUNK, vl_]
                a = jnp.where(tm["causal"], _dot_nt(tm["qm"][:, kl_], tm["km"][:, kl_]), 0.0)
                o_intra[c, h] = _dot(a, v)
                inc[c, h] = _dot_tn(tm["kl"][:, kl_], v)
        o_heads = {}
        for h in range(B_HEADS):
            kl_ = slice(h * 64, (h + 1) * 64)
            st = s_ref[kl_, :]
            for c, tm in enumerate(terms):
                sst_ref[c, kl_, :] = st
                o_heads[c, h] = o_intra[c, h] + _dot(tm["qc"][:, kl_], st)
                st = tm["dec_col"][kl_, :] * st + inc[c, h]
            s_ref[kl_, :] = st
        o = jnp.concatenate([jnp.concatenate([o_heads[c, h] for h in range(B_HEADS)], axis=1)
                             for c in range(cpb)], axis=0)
        oraw_ref[...] = o
        z = zb_ref[...].astype(_F32)
        gate = z * _sigmoid(z)
        for h in range(B_HEADS):
            vl_ = slice(h * 128, (h + 1) * 128)
            oh = o[:, vl_]
            r = lax.rsqrt(jnp.mean(oh * oh, axis=-1, keepdims=True) + NORM_EPS)
            ob_ref[:, vl_] = ((oh * r) * gn_ref[:, vl_] * gate[:, vl_]).astype(_ST)

    rows = lambda w: pl.BlockSpec((tb, w), lambda s, i: (s * nblk + i, 0))
    full = lambda a, b: pl.BlockSpec((a, b), lambda s, i: (0, 0))
    return pl.pallas_call(
        body, name="gla_fwd", grid=(nseq, nblk),
        in_specs=[rows(512), rows(512), rows(512), rows(LANE), full(LANE, B_KEY_WIDTH),
                  full(1, B_KEY_WIDTH), full(1, B_WIDTH)],
        out_specs=[rows(512), rows(512),
                   pl.BlockSpec((cpb, B_KEY_WIDTH, B_VAL_DIM), lambda s, i: (s * nblk + i, 0, 0))],
        out_shape=[jax.ShapeDtypeStruct((t, B_WIDTH), _ST), jax.ShapeDtypeStruct((t, B_WIDTH), _F32),
                   jax.ShapeDtypeStruct((t // B_CHUNK, B_KEY_WIDTH, B_VAL_DIM), _F32)],
        scratch_shapes=[pltpu.VMEM((B_KEY_WIDTH, B_VAL_DIM), _F32)],
        compiler_params=_cparams(("parallel", "arbitrary")),
    )(qkb, vb, zb, alr, wup, b_alpha, gn)


def _gla_bwd(qkb, vb, zb, alr, oraw, dob, sst, wup, b_alpha, gn, nseq):
    t = qkb.shape[0]
    tb = min(GLA_BLOCK, t // nseq)
    tile = min(GLA_BWD_TILE, t // nseq // tb)
    nstep = t // nseq // (tile * tb)
    cpb = tb // B_CHUNK

    def body(qkb_ref, vb_ref, zb_ref, alr_ref, oraw_ref, dob_ref, sst_ref, wup_ref, b_ref, gn_ref,
             dqkb_ref, dvb_ref, dzb_ref, dalr_ref, dwup_ref, db_ref, dgn_ref, ds_ref):
        s_id, i = pl.program_id(0), pl.program_id(1)

        @pl.when((s_id == 0) & (i == 0))
        def _():
            dwup_ref[...] = jnp.zeros_like(dwup_ref)
            db_ref[...] = jnp.zeros_like(db_ref)
            dgn_ref[...] = jnp.zeros_like(dgn_ref)

        @pl.when(i == 0)
        def _():
            ds_ref[...] = jnp.zeros_like(ds_ref)

        for sb in reversed(range(tile)):
            one_block(sb, qkb_ref, vb_ref, zb_ref, alr_ref, oraw_ref, dob_ref, sst_ref, wup_ref, b_ref, gn_ref,
                      dqkb_ref, dvb_ref, dzb_ref, dalr_ref, dwup_ref, db_ref, dgn_ref, ds_ref)

    def one_block(sb, qkb_ref, vb_ref, zb_ref, alr_ref, oraw_ref, dob_ref, sst_ref, wup_ref, b_ref, gn_ref,
                  dqkb_ref, dvb_ref, dzb_ref, dalr_ref, dwup_ref, db_ref, dgn_ref, ds_ref):
        base = sb * tb
        rows = slice(base, base + tb)
        a_pre = _dot(alr_ref[rows, :], wup_ref[...]) + b_ref[...]
        la = _log_sigmoid(a_pre) * (1.0 / B_GATE_TEMP)

        z = zb_ref[rows, :].astype(_F32)
        sz = _sigmoid(z)
        d_ob = dob_ref[rows, :].astype(_F32)
        tg = d_ob * (z * sz)
        dsilu = sz * (1.0 + z * (1.0 - sz))
        do_cols, dgn_cols = [], []
        for h in range(B_HEADS):
            vl_ = slice(h * 128, (h + 1) * 128)
            oh = oraw_ref[rows, vl_].astype(_F32)
            r = lax.rsqrt(jnp.mean(oh * oh, axis=-1, keepdims=True) + NORM_EPS)
            on = oh * r
            gnh = gn_ref[:, vl_]
            dzb_ref[rows, vl_] = (d_ob[:, vl_] * (on * gnh) * dsilu[:, vl_]).astype(_ST)
            dgn_cols.append(jnp.sum(tg[:, vl_] * on, axis=0, keepdims=True))
            do_cols.append(_rms_bwd(tg[:, vl_] * gnh, on, r))
        dgn_ref[...] = dgn_ref[...] + jnp.concatenate(dgn_cols, axis=1)
        d_o = jnp.concatenate(do_cols, axis=1)

        ri = lax.broadcasted_iota(jnp.int32, (tb, tb), 0)
        ci = lax.broadcasted_iota(jnp.int32, (tb, tb), 1)
        same = (ri // B_CHUNK) == (ci // B_CHUNK)
        low = same & (ri >= ci)
        cum = _dot_ones(low.astype(_F32), la)
        at_row = lambda r: jnp.concatenate([jnp.broadcast_to(cum[c * B_CHUNK + r:c * B_CHUNK + r + 1], (B_CHUNK, B_KEY_WIDTH))
                                            for c in range(cpb)], axis=0)
        last, mid = at_row(B_CHUNK - 1), at_row(B_CHUNK // 2 - 1)
        e_q, e_k, e_l, e_c = jnp.exp(cum - mid), jnp.exp(mid - cum), jnp.exp(last - cum), jnp.exp(cum)
        q = qkb_ref[rows, 0:B_KEY_WIDTH] * (B_KEY_DIM ** -0.5)
        k = qkb_ref[rows, B_KEY_WIDTH:2 * B_KEY_WIDTH]
        qm, km, kl, qc = q * e_q, k * e_k, k * e_l, q * e_c
        lane_head = lax.broadcasted_iota(jnp.int32, (1, B_KEY_WIDTH), 1) // B_KEY_DIM
        d_o_mx = d_o.astype(_MX)

        def on_diagonal(st):
            z = jnp.zeros((B_KEY_DIM, B_VAL_DIM), st.dtype)
            return jnp.concatenate([jnp.concatenate(
                [st[h * B_KEY_DIM:(h + 1) * B_KEY_DIM] if g == h else z for g in range(B_HEADS)], axis=1)
                for h in range(B_HEADS)], axis=0)

        def diagonal_of(full):
            return jnp.concatenate([full[h * B_KEY_DIM:(h + 1) * B_KEY_DIM, h * B_VAL_DIM:(h + 1) * B_VAL_DIM]
                                    for h in range(B_HEADS)], axis=0)

        dqm, dkm, dv_cols = None, None, []
        for h in range(B_HEADS):
            vl_ = slice(h * B_VAL_DIM, (h + 1) * B_VAL_DIM)
            mine = lane_head == h
            qz, kz = jnp.where(mine, qm, 0.0).astype(_MX), jnp.where(mine, km, 0.0).astype(_MX)
            a = jnp.where(low, _dot_nt(qz, kz), 0.0).astype(_MX)
            da = jnp.where(low, _dot_nt(d_o_mx[:, vl_], vb_ref[rows, vl_]), 0.0).astype(_MX)
            dqm_h, dkm_h = _dot(da, kz), _dot_tn(da, qz)
            dqm = dqm_h if dqm is None else dqm + dqm_h
            dkm = dkm_h if dkm is None else dkm + dkm_h
            dv_cols.append(_dot_tn(a, d_o_mx[:, vl_]))
        dv = jnp.concatenate(dv_cols, axis=1)

        chunk = [slice(c * B_CHUNK, (c + 1) * B_CHUNK) for c in range(cpb)]
        dqc_rows, g_loc = [], []
        for c in range(cpb):
            dqc_rows.append(_dot_nt(d_o_mx[chunk[c]], on_diagonal(sst_ref[sb * cpb + c].astype(_MX))))
            g_loc.append(diagonal_of(_dot_tn(qc[chunk[c]], d_o_mx[chunk[c]])))
        cur = ds_ref[...]
        d_state = [None] * cpb
        for c in reversed(range(cpb)):
            d_state[c] = cur
            cur = g_loc[c] + jnp.exp(jnp.sum(la[chunk[c]].T, axis=1, keepdims=True)) * cur
        ds_ref[...] = cur
        dkl_rows, dv_rows, dlast_rows = [], [], []
        ones8 = jnp.ones((8, B_VAL_DIM), _F32)
        for c in range(cpb):
            dsd = on_diagonal(d_state[c].astype(_MX))
            dkl_c = _dot_nt(vb_ref[base + c * B_CHUNK:base + (c + 1) * B_CHUNK, :], dsd)
            dkl_rows.append(dkl_c)
            dv_rows.append(_dot(kl[chunk[c]], dsd))
            prod = d_state[c] * sst_ref[sb * cpb + c]
            p0 = prod.astype(jnp.bfloat16)
            p1 = (prod - p0.astype(_F32)).astype(jnp.bfloat16)
            ddec = (_dot_nt(ones8, p0) + _dot_nt(ones8, p1))[0:1]
            r_last = c * B_CHUNK + B_CHUNK - 1
            dlast = jnp.sum(dkl_c * kl[chunk[c]], axis=0, keepdims=True) + ddec * jnp.exp(last[r_last:r_last + 1])
            dlast_rows.append(jnp.broadcast_to(dlast, (B_CHUNK, B_KEY_WIDTH)))
        dqc, dkl = jnp.concatenate(dqc_rows, axis=0), jnp.concatenate(dkl_rows, axis=0)
        dqkb_ref[rows, 0:B_KEY_WIDTH] = ((dqm * e_q + dqc * e_c) * (B_KEY_DIM ** -0.5)).astype(_ST)
        dqkb_ref[rows, B_KEY_WIDTH:2 * B_KEY_WIDTH] = (dkm * e_k + dkl * e_l).astype(_ST)
        dvb_ref[rows, :] = (dv + jnp.concatenate(dv_rows, axis=0)).astype(_ST)
        dcum = dqm * qm - dkm * km + dqc * qc - dkl * kl
        row = lax.broadcasted_iota(jnp.int32, (tb, B_KEY_WIDTH), 0)
        dcum = jnp.where(row % B_CHUNK == B_CHUNK - 1, dcum + jnp.concatenate(dlast_rows, axis=0), dcum)
        dla = _dot_ones((same & (ri <= ci)).astype(_F32), dcum)

        da_pre = dla * (1.0 / B_GATE_TEMP) * (1.0 - _sigmoid(a_pre))
        dalr_ref[rows, :] = _dot_nt(da_pre, wup_ref[...]).astype(_ST)
        dwup_ref[...] = dwup_ref[...] + _dot_tn(alr_ref[rows, :], da_pre)
        db_ref[...] = db_ref[...] + jnp.sum(da_pre, axis=0, keepdims=True)

    blk = lambda s, i: s * nstep + nstep - 1 - i
    rows = lambda w: pl.BlockSpec((tile * tb, w), lambda s, i: (blk(s, i), 0))
    full = lambda a, b: pl.BlockSpec((a, b), lambda s, i: (0, 0))
    act = lambda w: jax.ShapeDtypeStruct((t, w), _ST)
    return pl.pallas_call(
        body, name="gla_bwd", grid=(nseq, nstep),
        in_specs=[rows(512), rows(512), rows(512), rows(LANE), rows(512), rows(512),
                  pl.BlockSpec((tile * cpb, B_KEY_WIDTH, B_VAL_DIM), lambda s, i: (blk(s, i), 0, 0)),
                  full(LANE, B_KEY_WIDTH), full(1, B_KEY_WIDTH), full(1, B_WIDTH)],
        out_specs=[rows(512), rows(512), rows(512), rows(LANE), full(LANE, B_KEY_WIDTH),
                   full(1, B_KEY_WIDTH), full(1, B_WIDTH)],
        out_shape=[act(512), act(512), act(512), act(LANE),
                   jax.ShapeDtypeStruct((LANE, B_KEY_WIDTH), _F32),
                   jax.ShapeDtypeStruct((1, B_KEY_WIDTH), _F32), jax.ShapeDtypeStruct((1, B_WIDTH), _F32)],
        scratch_shapes=[pltpu.VMEM((B_KEY_WIDTH, B_VAL_DIM), _F32)],
        compiler_params=_cparams(("arbitrary", "arbitrary")),
    )(qkb, vb, zb, alr, oraw, dob, sst, wup, b_alpha, gn)


def _merge_loss(oa, ob, ga, gb, x2, tgt, wa, wb, wo, g_final):
    t = x2.shape[0]
    tm = min(t, 512)
    nt = t // tm

    def body(oa_ref, ob_ref, ga_ref, gb_ref, x_ref, t_ref, wa_ref, wb_ref, wo_ref, gf_ref,
             dh_ref, doa_ref, dob_ref, dga_ref, dgb_ref, dwa_ref, dwb_ref, dwo_ref, dgf_ref, loss_ref,
             ya_s, yb_s, out_s, dmer_s, mrg_s, dya_s, dyb_s):
        first = pl.program_id(0) == 0
        so_far = lambda ref: jnp.where(first, 0.0, ref[...])

        fold = lambda a: a[0:8] + a[8:16]
        gf = gf_ref[...]
        loss8 = jnp.zeros((8, D_MODEL), _F32)
        dgf8 = jnp.zeros((8, D_MODEL), _F32)
        for c0 in range(0, tm, MERGE_CHUNK):
            ch = slice(c0, c0 + MERGE_CHUNK)
            slabs = [slice(s, s + MERGE_SLAB) for s in range(c0, c0 + MERGE_CHUNK, MERGE_SLAB)]
            ya_s[ch, :] = _dot(oa_ref[ch, :], wa_ref[...])
            yb_s[ch, :] = _dot(ob_ref[ch, :], wb_ref[...])
            for rows_ in slabs:
                sga, sgb = _sigmoid(ga_ref[rows_, :].astype(_F32)), _sigmoid(gb_ref[rows_, :].astype(_F32))
                mrg_s[rows_, :] = (sga * ya_s[rows_, :] + sgb * yb_s[rows_, :]).astype(_MX)
            out_s[ch, :] = x_ref[ch, :] + _dot(mrg_s[ch, :], wo_ref[...])
            for rows_ in slabs:
                out = out_s[rows_, :]
                r = lax.rsqrt(jnp.mean(out * out, axis=-1, keepdims=True) + NORM_EPS)
                nrm = out * r
                err = nrm * gf - t_ref[rows_, :]
                loss8 = loss8 + fold(err * err)
                dy = err * (1.0 / D_MODEL)
                dgf8 = dgf8 + fold(dy * nrm)
                dh = _rms_bwd(dy * gf, nrm, r)
                dh_ref[rows_, :] = dh.astype(_ST)
            dmer_s[ch, :] = _dot_nt(dh_ref[ch, :], wo_ref[...])
            for rows_ in slabs:
                sga, sgb = _sigmoid(ga_ref[rows_, :].astype(_F32)), _sigmoid(gb_ref[rows_, :].astype(_F32))
                dmer = dmer_s[rows_, :]
                da, db = dmer * sga, dmer * sgb
                dya_s[rows_, :] = da.astype(_MX)
                dyb_s[rows_, :] = db.astype(_MX)
                dga_ref[rows_, :] = (da * ya_s[rows_, :] * (1.0 - sga)).astype(_ST)
                dgb_ref[rows_, :] = (db * yb_s[rows_, :] * (1.0 - sgb)).astype(_ST)
            doa_ref[ch, :] = _dot_nt(dya_s[ch, :], wa_ref[...]).astype(_ST)
            dob_ref[ch, :] = _dot_nt(dyb_s[ch, :], wb_ref[...]).astype(_ST)
        loss_ref[...] = so_far(loss_ref) + (0.5 / D_MODEL) * jnp.sum(loss8, axis=0, keepdims=True)
        dgf_ref[...] = so_far(dgf_ref) + jnp.sum(dgf8, axis=0, keepdims=True)
        dwo_ref[...] = so_far(dwo_ref) + _dot_tn(mrg_s[...], dh_ref[...])
        dwa_ref[...] = so_far(dwa_ref) + _dot_tn(oa_ref[...], dya_s[...])
        dwb_ref[...] = so_far(dwb_ref) + _dot_tn(ob_ref[...], dyb_s[...])

    rows = lambda w: pl.BlockSpec((tm, w), lambda i: (i, 0))
    full = lambda a, b: pl.BlockSpec((a, b), lambda i: (0, 0), pipeline_mode=pl.Buffered(1))
    return pl.pallas_call(
        body, name="merge_loss", grid=(nt,),
        in_specs=[rows(512), rows(512), rows(D_MODEL), rows(D_MODEL), rows(D_MODEL), rows(D_MODEL),
                  full(A_WIDTH, D_MODEL), full(B_WIDTH, D_MODEL), full(D_MODEL, D_MODEL), full(1, D_MODEL)],
        out_specs=[rows(D_MODEL), rows(512), rows(512), rows(D_MODEL), rows(D_MODEL),
                   full(A_WIDTH, D_MODEL), full(B_WIDTH, D_MODEL), full(D_MODEL, D_MODEL),
                   full(1, D_MODEL), full(1, D_MODEL)],
        out_shape=[jax.ShapeDtypeStruct((t, D_MODEL), _ST), jax.ShapeDtypeStruct((t, 512), _ST),
                   jax.ShapeDtypeStruct((t, 512), _ST), jax.ShapeDtypeStruct((t, D_MODEL), _ST),
                   jax.ShapeDtypeStruct((t, D_MODEL), _ST),
                   jax.ShapeDtypeStruct((A_WIDTH, D_MODEL), _F32), jax.ShapeDtypeStruct((B_WIDTH, D_MODEL), _F32),
                   jax.ShapeDtypeStruct((D_MODEL, D_MODEL), _F32), jax.ShapeDtypeStruct((1, D_MODEL), _F32),
                   jax.ShapeDtypeStruct((1, D_MODEL), _F32)],
        scratch_shapes=[pltpu.VMEM((tm, D_MODEL), _F32)] * 4 + [pltpu.VMEM((tm, D_MODEL), _MX)] * 3,
        compiler_params=_cparams(("arbitrary",), VMEM_LIMIT),
    )(oa, ob, ga, gb, x2, tgt, wa, wb, wo, g_final)


def _in_proj_bwd_x(dpieces, wt, x2, dh2, g_in, after):
    t = x2.shape[0]
    tm = min(t, 512)
    np_ = len(PIECES)

    def body(*refs):
        dp_refs = refs[:np_]
        w_ref, x_ref, dh2_ref, g_ref, after_ref, gx_ref, dg_ref = refs[np_:]

        @pl.when(pl.program_id(0) == 0)
        def _():
            dg_ref[...] = jnp.zeros_like(dg_ref)

        dh = None
        for (name, a, b), dp in zip(PIECES, dp_refs):
            part = _dot(dp[...], w_ref[a:b, :])
            dh = part if dh is None else dh + part
        xv = x_ref[...]
        r = lax.rsqrt(jnp.mean(xv * xv, axis=-1, keepdims=True) + NORM_EPS)
        nrm = xv * r
        dg_ref[...] = dg_ref[...] + jnp.sum(dh * nrm, axis=0, keepdims=True)
        gx_ref[...] = dh2_ref[...].astype(_F32) + _rms_bwd(dh * g_ref[...], nrm, r)

    rows = lambda w: pl.BlockSpec((tm, w), lambda i: (i, 0))
    full = lambda a, b: pl.BlockSpec((a, b), lambda i: (0, 0), pipeline_mode=pl.Buffered(1))
    return pl.pallas_call(
        body, name="in_proj_bwd_x", grid=(t // tm,),
        in_specs=[rows(b - a) for _, a, b in PIECES] + [full(D_IN, D_MODEL), rows(D_MODEL), rows(D_MODEL),
                                                          full(1, D_MODEL), _ANY],
        out_specs=[rows(D_MODEL), full(1, D_MODEL)],
        out_shape=[jax.ShapeDtypeStruct((t, D_MODEL), _F32), jax.ShapeDtypeStruct((1, D_MODEL), _F32)],
        compiler_params=_cparams(("arbitrary",), VMEM_LIMIT),
    )(*dpieces, wt, x2, dh2, g_in, after)


def _in_proj_bwd_w(h, dpieces, others, osplits):
    t = h.shape[0]
    tm = min(t, 1024)
    nt = t // tm
    np_, no = len(PIECES), len(others)
    half = D_MODEL // 2

    def body(*refs):
        h_ref, dp_refs, o_refs = refs[0], refs[1:1 + np_], refs[1 + np_:1 + np_ + no]
        mine_ref, theirs_ref = refs[1 + np_ + no:3 + np_ + no]
        r_refs = refs[3 + np_ + no:3 + np_ + 2 * no]
        acc_ref, stage_ref, keep_sem, send, recv, o_send, o_recv = refs[3 + np_ + 2 * no:]
        i = pl.program_id(0)
        x, y, c = _place()
        sibling = (x, y, 1 - c)
        early = [pltpu.make_async_remote_copy(
            src_ref=osplits[k].half(o_refs[k], 1 - c), dst_ref=r_refs[k], send_sem=o_send.at[k], recv_sem=o_recv.at[k],
            device_id=sibling, device_id_type=_MESH) for k in range(no)]

        @pl.when(i == 0)
        def _():
            for cp in early:
                cp.start()

        hv = h_ref[...]
        cols = lambda core: pl.ds(pl.multiple_of(core * half, LANE), half)
        writes = []
        by_size = sorted(range(np_), key=lambda j: PIECES[j][1] - PIECES[j][2])
        for j, ((name, a, b), dp) in [(j, (PIECES[j], dp_refs[j])) for j in by_size]:
            part = _dot_tn(dp[...], hv)
            if name == "alr":
                b = a + B_GATE_RANK
                part = part[0:B_GATE_RANK]
            acc_ref[a:b, :] = jnp.where(i == 0, 0.0, acc_ref[a:b, :]) + part
            keep = pltpu.make_async_copy(acc_ref.at[a:b, cols(c)], mine_ref.at[a:b], keep_sem.at[j])
            give = pltpu.make_async_remote_copy(
                src_ref=stage_ref.at[a:b], dst_ref=theirs_ref.at[a:b], send_sem=send.at[j],
                recv_sem=recv.at[j], device_id=sibling, device_id_type=_MESH)
            writes += [keep, give]

            @pl.when(i == nt - 1)
            def _(keep=keep, give=give, a=a, b=b):
                keep.start()
                stage_ref[a:b, :] = jnp.where(c == 0, acc_ref[a:b, half:], acc_ref[a:b, :half]).astype(_MX)
                give.start()

        @pl.when(i == nt - 1)
        def _():
            for cp in writes + early:
                cp.wait()

    rows = lambda w: pl.BlockSpec((tm, w), lambda i: (i, 0))
    halves = [jax.ShapeDtypeStruct((D_IN, half), _F32), jax.ShapeDtypeStruct((D_IN, half), _MX)]
    return pl.pallas_call(
        body, name="in_proj_bwd_w", grid=(nt,),
        in_specs=[rows(D_MODEL)] + [rows(b - a) for _, a, b in PIECES] + [_ANY] * no,
        out_specs=[_ANY] * (2 + no),
        out_shape=halves + [jax.ShapeDtypeStruct(sp.half_shape(g.shape), g.dtype) for g, sp in zip(others, osplits)],
        scratch_shapes=[pltpu.VMEM((D_IN, D_MODEL), _F32), pltpu.VMEM((D_IN, half), _MX),
                        pltpu.SemaphoreType.DMA((np_,)), pltpu.SemaphoreType.DMA((np_,)), pltpu.SemaphoreType.DMA((np_,)),
                        pltpu.SemaphoreType.DMA((no,)), pltpu.SemaphoreType.DMA((no,))],
        compiler_params=_cparams(("arbitrary",), VMEM_LIMIT),
    )(h, *dpieces, *others)


def _place():
    return lax.axis_index("x"), lax.axis_index("y"), lax.axis_index("c")


def _other_chips(x, y):
    return [(1 - x, y), (x, 1 - y), (1 - x, 1 - y)]


class _Split(NamedTuple):
    by_rows: bool
    step: int
    size: int

    def half(self, ref, c):
        r, n = ref.shape[-2:]
        if self.by_rows:
            return ref.at[:, pl.ds(pl.multiple_of(c * (n // 2), LANE), n // 2)]
        return ref.at[pl.ds(pl.multiple_of(c * (r // 2), 16), r // 2), :]

    def chip_part(self, ref, k):
        if self.by_rows:
            return ref.at[pl.ds(pl.multiple_of(k * self.step, 16), self.size), :]
        return ref.at[:, pl.ds(pl.multiple_of(k * self.size, LANE), self.size)]

    def half_shape(self, shape):
        r, n = shape
        return (r, n // 2) if self.by_rows else (r // 2, n)

    def part_shape(self, shape):
        r, n = shape
        return (self.size, n) if self.by_rows else (r, self.size)


SPLIT_W_IN_T = _Split(True, WINDOW_STEP, WINDOW_ROWS)
SPLIT_W_O = _Split(True, 256, 256)
SPLIT_W_OUT = _Split(False, 256, 256)


def _gather_weights(shards, splits, fulls, pos_f):
    nw = len(shards)
    t = pos_f.shape[0]

    def body(*refs):
        ins, (pos_ref, c_ref) = refs[:nw], refs[nw:nw + 2]
        outs, tables = refs[nw + 2:2 * nw + 2], refs[2 * nw + 2:2 * nw + 5]
        send_a, recv_a, send_b, recv_b = refs[2 * nw + 5:]
        x, y, c = _place()
        me = 2 * x + y
        peers = _other_chips(x, y)

        def place(i, k, half):
            if splits[i] is None:
                return outs[i].at[k]
            if fulls[i][0] == 4 and len(fulls[i]) == 3:
                whole = outs[i].at[k]
            else:
                whole = splits[i].chip_part(outs[i], k)
            return splits[i].half(whole, half)

        first, passed = [], []
        for i in range(nw):
            src = ins[i] if splits[i] is None else splits[i].half(ins[i], c)
            for j, (px, py) in enumerate(peers):
                cp = pltpu.make_async_remote_copy(
                    src_ref=src, dst_ref=place(i, me, c), send_sem=send_a.at[3 * i + j],
                    recv_sem=recv_a.at[3 * i + j], device_id=(px, py, c), device_id_type=_MESH)
                cp.start()
                first.append(cp)
        _rope_tables_into(pos_ref, c_ref, *tables)
        for i in range(nw):
            for j, (px, py) in enumerate(peers):
                landed = place(i, 2 * px + py, c)
                pltpu.make_async_remote_copy(
                    src_ref=landed, dst_ref=landed, send_sem=send_a.at[3 * i + j], recv_sem=recv_a.at[3 * i + j],
                    device_id=(px, py, c), device_id_type=_MESH).wait_recv()
                if splits[i] is not None:
                    cp = pltpu.make_async_remote_copy(
                        src_ref=landed, dst_ref=landed, send_sem=send_b.at[3 * i + j], recv_sem=recv_b.at[3 * i + j],
                        device_id=(x, y, 1 - c), device_id_type=_MESH)
                    cp.start()
                    passed.append(cp)
        for i in range(nw):
            if splits[i] is None:
                continue
            for j, (px, py) in enumerate(peers):
                theirs = place(i, 2 * px + py, 1 - c)
                pltpu.make_async_remote_copy(
                    src_ref=theirs, dst_ref=theirs, send_sem=send_b.at[3 * i + j], recv_sem=recv_b.at[3 * i + j],
                    device_id=(x, y, 1 - c), device_id_type=_MESH).wait_recv()
        for cp in first + passed:
            cp.wait_send()

    vm = pl.BlockSpec(memory_space=pltpu.VMEM)
    tab = jax.ShapeDtypeStruct((t, LANE), _F32)
    return pl.pallas_call(
        body, name="gather_weights",
        in_specs=[_ANY] * nw + [vm, vm], out_specs=[_ANY] * nw + [vm] * 3,
        out_shape=[jax.ShapeDtypeStruct(f, s.dtype) for f, s in zip(fulls, shards)] + [tab] * 3,
        scratch_shapes=[pltpu.SemaphoreType.DMA((3 * nw,)) for _ in range(4)],
        compiler_params=_cparams(None, VMEM_LIMIT),
    )(*shards, pos_f, _rope_consts())


def _assemble_w_in_t(slots):
    bw = 256
    ov = WINDOW_ROWS - WINDOW_STEP

    def body(s_ref, o_ref):
        for k in range(4):
            base = k * WINDOW_STEP
            lo = 0 if k == 0 else ov
            if k > 0:
                o_ref[base:base + ov, :] = s_ref[k - 1, WINDOW_STEP:WINDOW_ROWS, :] + s_ref[k, 0:ov, :]
            hi = WINDOW_ROWS if k == 3 else WINDOW_STEP
            o_ref[base + lo:base + hi, :] = s_ref[k, lo:hi, :]

    return pl.pallas_call(
        body, name="assemble_w_in_t", grid=(D_MODEL // bw,),
        in_specs=[pl.BlockSpec((4, WINDOW_ROWS, bw), lambda i: (0, 0, i))],
        out_specs=pl.BlockSpec((D_IN, bw), lambda i: (0, i)),
        out_shape=jax.ShapeDtypeStruct((D_IN, D_MODEL), slots.dtype),
        compiler_params=_cparams(("parallel",)),
    )(slots)


def _row_block(rows):
    for cand in (976, 176, 256, 128):
        if rows % cand == 0:
            return cand
    return rows


def _pair_sum(g, r, split, c_arr, name):
    hr, hn = r.shape
    br = _row_block(hr)
    if split is None:
        g_spec = pl.BlockSpec((br, hn), lambda i, c_ref: (i, 0))
    elif split.by_rows:
        g_spec = pl.BlockSpec((br, hn), lambda i, c_ref: (i, c_ref[0]))
    else:
        g_spec = pl.BlockSpec((br, hn), lambda i, c_ref: (c_ref[0] * (hr // br) + i, 0))

    def body(c_ref, g_ref, r_ref, o_ref):
        o_ref[...] = (g_ref[...] + r_ref[...]).astype(o_ref.dtype)

    return pl.pallas_call(
        body, name=name,
        grid_spec=pltpu.PrefetchScalarGridSpec(
            num_scalar_prefetch=1, grid=(hr // br,),
            in_specs=[g_spec, pl.BlockSpec((br, hn), lambda i, c_ref: (i, 0))],
            out_specs=pl.BlockSpec((br, hn), lambda i, c_ref: (i, 0))),
        out_shape=jax.ShapeDtypeStruct(r.shape, _MX),
        compiler_params=_cparams(("parallel",)),
    )(c_arr, g, r)


def _pair_sums_whole(gs, rs, splits, c_arr):
    n = len(gs)

    def g_spec(r, split):
        at = (lambda i, c_ref: (0, c_ref[0])) if split.by_rows else (lambda i, c_ref: (c_ref[0], 0))
        return pl.BlockSpec(r.shape, at)

    def body(c_ref, *refs):
        for g_ref, r_ref, o_ref in zip(refs[:n], refs[n:2 * n], refs[2 * n:]):
            o_ref[...] = (g_ref[...] + r_ref[...]).astype(o_ref.dtype)

    whole = lambda r: pl.BlockSpec(r.shape, lambda i, c_ref: (0, 0))
    return pl.pallas_call(
        body, name="pair_sum_small",
        grid_spec=pltpu.PrefetchScalarGridSpec(
            num_scalar_prefetch=1, grid=(1,),
            in_specs=[g_spec(r, sp) for r, sp in zip(rs, splits)] + [whole(r) for r in rs],
            out_specs=[whole(r) for r in rs]),
        out_shape=[jax.ShapeDtypeStruct(r.shape, _MX) for r in rs],
        compiler_params=_cparams(("arbitrary",), VMEM_LIMIT),
    )(c_arr, *gs, *rs)


_HBM = pl.BlockSpec(memory_space=pltpu.HBM)
_SEM = pl.BlockSpec(memory_space=pltpu.SEMAPHORE)
_FLOWS = pltpu.SideEffectType.DATAFLOW_SIDE_EFFECTING


def _chip_exchange_copies(refs, send, recv, splits):
    nw = len(refs) // 2
    x, y, c = _place()
    me = 2 * x + y
    copies = []
    for i in range(nw):
        for px, py in _other_chips(x, y):
            copies.append((splits[i].chip_part(refs[i], 2 * px + py), refs[nw + i].at[me], (px, py, c)))
    return [pltpu.make_async_remote_copy(src_ref=src, dst_ref=dst, send_sem=send.at[k], recv_sem=recv.at[k],
                                         device_id=peer, device_id_type=_MESH)
            for k, (src, dst, peer) in enumerate(copies)]


def _late_gather_copies(refs, send, recv, splits):
    nw = len(refs) // 2
    x, y, c = _place()
    me = 2 * x + y
    copies = []
    for i in range(nw):
        for px, py in [(x, y)] + _other_chips(x, y):
            copies.append((refs[i], splits[i].chip_part(refs[nw + i], me), (px, py, c)))
    return [pltpu.make_async_remote_copy(src_ref=src, dst_ref=dst, send_sem=send.at[k], recv_sem=recv.at[k],
                                         device_id=peer, device_id_type=_MESH)
            for k, (src, dst, peer) in enumerate(copies)]


def _start_copies(name, flying, copies_of, n_copies, after=None):
    first = [] if after is None else [after]

    def body(*refs):
        ins = refs[:len(flying)]
        send, recv = refs[len(flying) + len(first):len(flying) + len(first) + 2]
        token = refs[-1]
        for cp in copies_of(ins, send, recv):
            cp.start()
        token[...] = jnp.zeros_like(token)

    outs = pl.pallas_call(
        body, name=name,
        in_specs=[_HBM] * len(flying) + [_ANY] * len(first),
        out_specs=[_SEM, _SEM] + [_HBM] * len(flying) + [pl.BlockSpec(memory_space=pltpu.VMEM)],
        out_shape=[pltpu.SemaphoreType.DMA((n_copies,)), pltpu.SemaphoreType.DMA((n_copies,))]
        + [pltpu.HBM(f.shape, f.dtype) for f in flying] + [jax.ShapeDtypeStruct((8, LANE), _F32)],
        input_output_aliases={i: 2 + i for i in range(len(flying))},
        compiler_params=pltpu.CompilerParams(has_side_effects=_FLOWS),
    )(*[pltpu.with_memory_space_constraint(f, pltpu.HBM) for f in flying], *first)
    return outs[0], outs[1], outs[2:2 + len(flying)], outs[-1]


def _wait_copies(name, send, recv, flying, copies_of, after):
    def body(*refs):
        ins = refs[:len(flying)]
        send_ref, recv_ref = refs[len(flying):len(flying) + 2]
        for cp in copies_of(ins, send_ref, recv_ref):
            cp.wait_send()
            cp.wait_recv()

    return pl.pallas_call(
        body, name=name,
        in_specs=[_HBM] * len(flying) + [_SEM, _SEM, _ANY],
        out_specs=[_HBM] * len(flying),
        out_shape=[pltpu.HBM(f.shape, f.dtype) for f in flying],
        input_output_aliases={i: i for i in range(len(flying))},
        compiler_params=pltpu.CompilerParams(has_side_effects=_FLOWS),
    )(*flying, send, recv, after)


def _sum_chips(qs, ps, splits, place_arr):
    n = len(qs)
    q_specs, p_specs, o_specs, out_shapes = [], [], [], []
    for q, split in zip(qs, splits):
        _, hr, hn = q.shape
        q_specs.append(pl.BlockSpec((4, hr, hn), lambda i, pr: (0, 0, 0)))
        if split.by_rows:
            out_shapes.append((hr, 2 * hn))
            o_specs.append(pl.BlockSpec((hr, hn), lambda i, pr: (0, pr[0])))
            p_specs.append(pl.BlockSpec((pl.Element(hr), pl.Element(hn)), lambda i, pr, step=split.step: (pr[1] * step, 0)))
        else:
            out_shapes.append((2 * hr, hn))
            o_specs.append(pl.BlockSpec((hr, hn), lambda i, pr: (pr[0], 0)))
            p_specs.append(pl.BlockSpec((hr, hn), lambda i, pr: (0, pr[1])))

    def body(pr, *refs):
        for q_ref, p_ref, o_ref in zip(refs[:n], refs[n:2 * n], refs[2 * n:]):
            f = lambda k: jnp.where(pr[1] == k, p_ref[...], q_ref[k]).astype(_F32)
            o_ref[...] = ((f(0) + f(1)) + f(2)) + f(3)

    return pl.pallas_call(
        body, name="chip_sums",
        grid_spec=pltpu.PrefetchScalarGridSpec(
            num_scalar_prefetch=1, grid=(1,), in_specs=q_specs + p_specs, out_specs=o_specs),
        out_shape=[jax.ShapeDtypeStruct(s, _F32) for s in out_shapes],
        compiler_params=_cparams(("arbitrary",), VMEM_LIMIT),
    )(place_arr, *qs, *ps)


def _pair_share(bufs, splits, small):
    nw = len(bufs)

    def body(*refs):
        ins, small_ref, outs, all_ref = refs[:nw], refs[nw], refs[nw + 1:2 * nw + 1], refs[2 * nw + 1]
        send, recv, s_send, s_recv = refs[2 * nw + 2:]
        x, y, c = _place()
        copies = []
        for r in range(1, 8):
            peer = (1 - x if r & 4 else x, 1 - y if r & 2 else y, 1 - c if r & 1 else c)
            cp = pltpu.make_async_remote_copy(
                src_ref=small_ref, dst_ref=all_ref.at[4 * x + 2 * y + c], send_sem=s_send.at[r - 1],
                recv_sem=s_recv.at[r - 1], device_id=peer, device_id_type=_MESH)
            cp.start()
            copies.append(cp)
        for i in range(nw):
            cp = pltpu.make_async_remote_copy(
                src_ref=splits[i].half(ins[i], c), dst_ref=splits[i].half(outs[i], c), send_sem=send.at[i],
                recv_sem=recv.at[i], device_id=(x, y, 1 - c), device_id_type=_MESH)
            cp.start()
            copies.append(cp)
        for cp in copies:
            cp.wait()

    return pl.pallas_call(
        body, name="grad_pair_share",
        in_specs=[_ANY] * (nw + 1), out_specs=[_ANY] * (nw + 1),
        out_shape=[jax.ShapeDtypeStruct(b.shape, b.dtype) for b in bufs]
        + [jax.ShapeDtypeStruct((8,) + small.shape, small.dtype)],
        input_output_aliases={i: i for i in range(nw)},
        scratch_shapes=[pltpu.SemaphoreType.DMA((nw,)), pltpu.SemaphoreType.DMA((nw,)),
                        pltpu.SemaphoreType.DMA((7,)), pltpu.SemaphoreType.DMA((7,))],
    )(*bufs, small)


def _sum_devices(parts, own, dev_arr):
    def body(dev, p_ref, own_ref, tot_ref):
        f = lambda d: jnp.where(dev[0] == d, own_ref[...], p_ref[d])
        acc = f(0)
        for d in range(1, 8):
            acc = acc + f(d)
        tot_ref[...] = acc

    return pl.pallas_call(
        body, name="small_sum",
        grid_spec=pltpu.PrefetchScalarGridSpec(
            num_scalar_prefetch=1, grid=(1,),
            in_specs=[pl.BlockSpec(parts.shape, lambda i, dev: (0, 0, 0)), pl.BlockSpec(own.shape, lambda i, dev: (0, 0))],
            out_specs=pl.BlockSpec(own.shape, lambda i, dev: (0, 0))),
        out_shape=jax.ShapeDtypeStruct(own.shape, own.dtype),
    )(dev_arr, parts, own)


def _adam_update(w, g, m, v):
    m2 = ADAM_B1 * m + (1.0 - ADAM_B1) * g
    v2 = ADAM_B2 * v + (1.0 - ADAM_B2) * (g * g)
    m_hat = m2 / (1.0 - ADAM_B1 ** ADAM_STEP)
    v_hat = v2 / (1.0 - ADAM_B2 ** ADAM_STEP)
    return -ADAM_LR * (m_hat / (jnp.sqrt(v_hat) + ADAM_EPS) + ADAM_WD * w), m2, v2


SMALL_AT = dict(g_in=(0, 0), g_final=(1, 0), g_gla_norm=(2, 0), b_alpha=(2, B_WIDTH), attn_sinks=(2, B_WIDTH + B_KEY_WIDTH))
LOSS_AT = (2, B_WIDTH + B_KEY_WIDTH + LANE)
WUP_ROWS = (3, 7)


def _adamw_small(tot, g_wup, params):
    names = list(params)

    def body(*refs):
        tot_ref, gw_ref = refs[0], refs[1]
        ins = refs[2:2 + 3 * len(names)]
        outs = refs[2 + 3 * len(names):]
        for i, nm in enumerate(names):
            w_ref, m_ref, v_ref = ins[3 * i:3 * i + 3]
            if nm in SMALL_AT:
                r, a = SMALL_AT[nm]
                g = tot_ref[r:r + 1, a:a + w_ref.shape[1]]
            else:
                g = gw_ref[...]
            d, m2, v2 = _adam_update(w_ref[...], g, m_ref[...], v_ref[...])
            for o_ref, val in zip(outs[4 * i:4 * i + 4], (g, d, m2, v2)):
                o_ref[...] = val

    vm = pl.BlockSpec(memory_space=pltpu.VMEM)
    flat = [a for nm in names for a in params[nm]]
    out_shape = [jax.ShapeDtypeStruct(params[nm][0].shape, _F32) for nm in names for _ in range(4)]
    outs = pl.pallas_call(
        body, name="adamw_small", in_specs=[vm] * (2 + len(flat)), out_specs=[vm] * len(out_shape), out_shape=out_shape,
    )(tot, g_wup, *flat)
    return {nm: tuple(outs[4 * i:4 * i + 4]) for i, nm in enumerate(names)}


def _adamw_w_in(w, g_window, m, v, chip_arr):
    r, _, n = w.shape
    br = r // 5

    def body(k_ref, w_ref, g_ref, m_ref, v_ref, go_ref, d_ref, nm_ref, nv_ref):
        first = (SHARD - WINDOW_STEP) * k_ref[0] + br * pl.program_id(0)
        wide = g_ref[pl.ds(pl.multiple_of((first // 8) * 8, 8), br + 4), :]
        g = jnp.where(first % 8 == 0, wide[:br], wide[4:]).reshape(br, 1, n)
        go_ref[...] = g
        d_ref[...], nm_ref[...], nv_ref[...] = _adam_update(w_ref[...], g, m_ref[...], v_ref[...])

    blk = pl.BlockSpec((br, 1, n), lambda i, k_ref: (i, 0, 0))
    g_spec = pl.BlockSpec(memory_space=pltpu.VMEM)
    shp = jax.ShapeDtypeStruct(w.shape, _F32)
    return pl.pallas_call(
        body, name="adamw_w_in",
        grid_spec=pltpu.PrefetchScalarGridSpec(
            num_scalar_prefetch=1, grid=(r // br,), in_specs=[blk, g_spec, blk, blk], out_specs=[blk] * 4),
        out_shape=[shp] * 4,
        compiler_params=_cparams(("parallel",)),
    )(chip_arr, w, g_window, m, v)


def _adamw_whole(items):
    n = len(items)

    def body(*refs):
        for i in range(n):
            w_ref, g_ref, m_ref, v_ref = refs[4 * i:4 * i + 4]
            d_ref, nm_ref, nv_ref = refs[4 * n + 3 * i:4 * n + 3 * i + 3]
            d_ref[...], nm_ref[...], nv_ref[...] = _adam_update(w_ref[...], g_ref[...], m_ref[...], v_ref[...])

    whole = lambda a: pl.BlockSpec((None,) + a.shape[1:], lambda i: (0, 0, 0))
    outs = pl.pallas_call(
        body, name="adamw_out_weights", grid=(1,),
        in_specs=[whole(a) for it in items for a in it],
        out_specs=[whole(it[0]) for it in items for _ in range(3)],
        out_shape=[jax.ShapeDtypeStruct(it[0].shape, _F32) for it in items for _ in range(3)],
        compiler_params=_cparams(("arbitrary",), VMEM_LIMIT),
    )(*[a for it in items for a in it])
    return [tuple(outs[3 * i:3 * i + 3]) for i in range(n)]


def kernel(x, positions, g_in, w_in, w_alpha_up, b_alpha, attn_sinks, g_gla_norm, w_out_a, w_out_b, w_o, g_final, loss_target, m_g_in, m_w_in, m_w_alpha_up, m_b_alpha, m_attn_sinks, m_g_gla_norm, m_w_out_a, m_w_out_b, m_w_o, m_g_final, v_g_in, v_w_in, v_w_alpha_up, v_b_alpha, v_attn_sinks, v_g_gla_norm, v_w_out_a, v_w_out_b, v_w_o, v_g_final):
    nseq, seq, _ = x.shape
    t = nseq * seq
    cx, cy, cc = _place()
    chip = 2 * cx + cy
    c_arr = jnp.reshape(cc, (1,)).astype(jnp.int32)

    tr = lambda w: jnp.transpose(w, (2, 0, 1))
    w_in_t = tr(w_in).reshape(SHARD, D_MODEL).astype(_MX)
    pad = WINDOW_ROWS - SHARD
    window = lax.switch(chip, [lambda w, k=k: jnp.pad(w, ((4 * k, pad - 4 * k), (0, 0))) for k in range(4)], w_in_t)
    shards = [window, w_alpha_up[0].astype(_MX)]
    late = [w_out_a[0].astype(_MX), w_out_b[0].astype(_MX), w_o[0].astype(_MX)]
    late_splits = [SPLIT_W_OUT, SPLIT_W_OUT, SPLIT_W_O]
    splits = [SPLIT_W_IN_T, None]
    fulls = [(4, WINDOW_ROWS, D_MODEL), (4, B_GATE_RANK, B_KEY_WIDTH // 4)]
    pos_f = positions.astype(_F32).reshape(t, 1)
    win_g, wup_g, cos, sa, sb = _gather_weights(shards, splits, fulls, pos_f)
    late_copies = lambda refs, send, recv: _late_gather_copies(refs, send, recv, late_splits)
    late_full = [lax.empty(shape, _MX) for shape in ((A_WIDTH, D_MODEL), (B_WIDTH, D_MODEL), (D_MODEL, D_MODEL))]
    l_send, l_recv, l_flying, l_token = _start_copies("late_gather_start", late + late_full, late_copies,
                                                      4 * len(late), after=win_g)
    win_g = lax.dynamic_update_slice(win_g, window[None], (chip, 0, 0))
    wup_g = lax.dynamic_update_slice(wup_g, shards[1][None], (chip, 0, 0))
    wt = _assemble_w_in_t(win_g)
    wup = jnp.concatenate([jnp.transpose(wup_g, (1, 0, 2)).reshape(B_GATE_RANK, B_KEY_WIDTH),
                           jnp.zeros((LANE - B_GATE_RANK, B_KEY_WIDTH), _MX)], axis=0)

    x2 = x.reshape(t, D_MODEL)
    tgt = loss_target.reshape(t, D_MODEL)
    sinks = attn_sinks.reshape(A_HEADS)
    gf = g_final.reshape(1, D_MODEL)

    h, qkv, za, qkb, vb, zb, alr, ga, gb = _in_proj(x2, g_in, wt, cos, sa, sb, l_token)
    oa = _attn_fwd(qkv, za, sinks, nseq)
    ob, oraw, sst = _gla_fwd(qkb, vb, zb, alr, wup, b_alpha, g_gla_norm, nseq)

    wa, wb, wo = _wait_copies("late_gather_wait", l_send, l_recv, l_flying, late_copies, ob)[len(late):]
    dh2, doa, dob, dga, dgb, dwa, dwb, dwo, dgf, lossv = _merge_loss(oa, ob, ga, gb, x2, tgt, wa, wb, wo, gf)

    dqkv, dza, dsink = _attn_bwd(qkv, za, doa, sinks, cos, sa, sb, nseq)
    dqkb, dvb, dzb, dalr, dwup, dba, dgn = _gla_bwd(qkb, vb, zb, alr, oraw, dob, sst, wup, b_alpha, g_gla_norm, nseq)
    dpieces = [dqkv, dza, dqkb, dvb, dzb, dalr, dga, dgb]
    gsplits = [SPLIT_W_IN_T, SPLIT_W_OUT, SPLIT_W_OUT, SPLIT_W_O]
    dwin_mine, *from_sibling = _in_proj_bwd_w(h, dpieces, [dwa, dwb, dwo], gsplits[1:])
    pair_sums = [_pair_sum(dwin_mine, from_sibling[0], None, c_arr, "pair_sum_w_in"),
                 *_pair_sums_whole([dwa, dwb, dwo], from_sibling[1:], gsplits[1:], c_arr)]
    exchange = lambda refs, send, recv: _chip_exchange_copies(refs, send, recv, gsplits)
    lands = [lax.empty((4,) + sp.part_shape(p.shape), p.dtype) for p, sp in zip(pair_sums, gsplits)]
    send, recv, flying, token = _start_copies("grad_chip_exchange_start", pair_sums + lands, exchange, 3 * len(lands))
    grad_x2, dgin = _in_proj_bwd_x(dpieces, wt, x2, dh2, g_in, token)
    landed = _wait_copies("grad_chip_exchange_wait", send, recv, flying, exchange, grad_x2)
    place_arr = jnp.stack([cc, chip]).astype(jnp.int32)
    reduced = _sum_chips(landed[len(lands):], landed[:len(lands)], gsplits, place_arr)
    row2 = jnp.concatenate([dgn, dba, jnp.pad(dsink[:, 0].reshape(1, A_HEADS), ((0, 0), (0, LANE - A_HEADS))),
                            jnp.pad(jnp.sum(lossv, axis=1, keepdims=True), ((0, 0), (0, LANE - 1)))], axis=1)
    small = jnp.concatenate([dgin, dgf, row2, dwup[:B_GATE_RANK].reshape(WUP_ROWS[1] - WUP_ROWS[0], D_MODEL),
                             jnp.zeros((1, D_MODEL), _F32)], axis=0)
    g_window, g_wa, g_wb, g_wo, small_parts = _pair_share(reduced, gsplits, small)
    dev_arr = jnp.reshape(2 * chip + cc, (1,)).astype(jnp.int32)
    tot = _sum_devices(small_parts, small, dev_arr)
    loss = tot[LOSS_AT]
    nup = B_KEY_WIDTH // 4
    g_wup = lax.dynamic_slice(tot[WUP_ROWS[0]:WUP_ROWS[1]].reshape(B_GATE_RANK, B_KEY_WIDTH), (0, chip * nup),
                              (B_GATE_RANK, nup))

    row = lambda a: a.reshape(1, -1)
    sm = _adamw_small(tot, g_wup, dict(
        g_in=(g_in, m_g_in, v_g_in), g_final=(row(g_final), row(m_g_final), row(v_g_final)),
        g_gla_norm=(g_gla_norm, m_g_gla_norm, v_g_gla_norm), b_alpha=(b_alpha, m_b_alpha, v_b_alpha),
        attn_sinks=(attn_sinks, m_attn_sinks, v_attn_sinks),
        w_alpha_up=(w_alpha_up[0], m_w_alpha_up[0], v_w_alpha_up[0])))
    sm["g_final"] = tuple(a.reshape(D_MODEL) for a in sm["g_final"])
    sm["w_alpha_up"] = tuple(a[None] for a in sm["w_alpha_up"])

    untr = lambda a: jnp.transpose(a, (1, 2, 0))
    chip_arr = jnp.reshape(chip, (1,)).astype(jnp.int32)
    big = dict(w_in=tuple(untr(a) for a in _adamw_w_in(tr(w_in), g_window, tr(m_w_in), tr(v_w_in), chip_arr)))
    out_weights = (("w_out_a", w_out_a, g_wa, m_w_out_a, v_w_out_a),
                   ("w_out_b", w_out_b, g_wb, m_w_out_b, v_w_out_b), ("w_o", w_o, g_wo, m_w_o, v_w_o))
    updates = _adamw_whole([(w, g[None], m, v) for _, w, g, m, v in out_weights])
    for (nm, _, g, _, _), upd in zip(out_weights, updates):
        big[nm] = (g[None],) + upd

    order = ("g_in", "w_in", "w_alpha_up", "b_alpha", "attn_sinks", "g_gla_norm", "w_out_a", "w_out_b", "w_o", "g_final")
    outs = [big[nm][kind] if nm in big else sm[nm][kind] for kind in range(4) for nm in order]
    return (loss, grad_x2.reshape(x.shape), *outs)
```
